```python
import math
import jax, jax.numpy as jnp
from jax import lax
import numpy as np

D_MODEL = 1024
BATCH = 16
SEQ = 4096
DEPTH = 4

N_MIXERS = 3
CONV_WIDTH = 31
POOL_WINDOWS = (2, 4, 8, 16)
N_POOL_GROUPS = len(POOL_WINDOWS)
POOL_GROUP_DIM = D_MODEL // N_POOL_GROUPS
N_HEADS = 16
HEAD_DIM = D_MODEL // N_HEADS
Q_BLOCK = 128
D_FF = ((8 * D_MODEL // 3 + 127) // 128) * 128
FFN_CONV_WIDTH = 3
EPS = 1e-6
N_A = (DEPTH + 2) // 3
N_B = (DEPTH + 1) // 3
N_C = DEPTH // 3

kernel_name = "hybrid_conv_pool_fox_trunk"


def rms_norm(x, g):
    x32 = x.astype(jnp.float32)
    y = x32 * lax.rsqrt(jnp.mean(x32 * x32, axis=-1, keepdims=True) + EPS)
    return (y * g.astype(jnp.float32)).astype(x.dtype)


def causal_dwconv(x, w, b):
    k_width, c = w.shape
    y = lax.conv_general_dilated(x, w[:, None, :].astype(x.dtype), window_strides=(1,),
                                 padding=[(k_width - 1, 0)],
                                 dimension_numbers=("NWC", "WIO", "NWC"),
                                 feature_group_count=c)
    return y + b.astype(x.dtype)


def conformer_conv(h, w_in, b_in, dw, dw_b, ln_g, ln_b, w_out, b_out):
    a, g = jnp.split(h @ w_in + b_in, 2, axis=-1)
    u = causal_dwconv(a * jax.nn.sigmoid(g), dw, dw_b)
    u32 = u.astype(jnp.float32)
    mu = jnp.mean(u32, axis=-1, keepdims=True)
    var = jnp.mean(jnp.square(u32 - mu), axis=-1, keepdims=True)
    u = ((u32 - mu) * lax.rsqrt(var + EPS) * ln_g.astype(jnp.float32) + ln_b.astype(jnp.float32)).astype(h.dtype)
    return jax.nn.silu(u) @ w_out + b_out


def multiscale_pool(h, w_grp, b_grp, scale):
    bsz, s, d = h.shape
    hg = h.reshape(bsz, s, N_POOL_GROUPS, POOL_GROUP_DIM).astype(jnp.float32)
    t = jnp.arange(s)
    outs = []
    for gi, w in enumerate(POOL_WINDOWS):
        xg = hg[:, :, gi]
        cs = jnp.cumsum(xg, axis=1)
        lag = jnp.pad(cs[:, :s - w], ((0, 0), (w, 0), (0, 0)))
        cnt = jnp.minimum(t + 1, w).astype(jnp.float32)[None, :, None]
        outs.append((cs - lag) / cnt - xg)
    p = jnp.stack(outs, axis=2).astype(h.dtype)
    y = jnp.einsum("bsgc,gcd->bsgd", p, w_grp) + b_grp
    return y.reshape(bsz, s, d) * scale


def fox_block_attention(q, k, v, c):
    bsz, nh, s, hd = q.shape
    nb = s // Q_BLOCK
    scale = 1.0 / math.sqrt(hd)
    qb = q.reshape(bsz, nh, nb, Q_BLOCK, hd).transpose(2, 0, 1, 3, 4)
    cb = c.reshape(bsz, nh, nb, Q_BLOCK).transpose(2, 0, 1, 3)
    pos = jnp.arange(s)
    qpos = pos.reshape(nb, Q_BLOCK)

    def one_block(args):
        qi, ci, pi = args
        logits = jnp.einsum("bhqd,bhkd->bhqk", qi, k).astype(jnp.float32) * scale
        logits = logits + ci[..., :, None] - c[:, :, None, :]
        mask = pi[:, None] >= pos[None, :]
        logits = jnp.where(mask[None, None], logits, -jnp.inf)
        probs = jax.nn.softmax(logits, axis=-1)
        return jnp.einsum("bhqk,bhkd->bhqd", probs.astype(v.dtype), v)

    out = lax.map(one_block, (qb, cb, qpos))
    return out.transpose(1, 2, 0, 3, 4).reshape(bsz, nh, s, hd)


def forgetting_attention(h, w_in, b_f, q_gain, k_gain, w_o):
    bsz, s, d = h.shape
    proj = h @ w_in
    q, k, v, fl = jnp.split(proj, [d, 2 * d, 3 * d], axis=-1)
    to_heads = lambda z: z.reshape(bsz, s, N_HEADS, HEAD_DIM).transpose(0, 2, 1, 3)
    q = rms_norm(to_heads(q), q_gain)
    k = rms_norm(to_heads(k), k_gain)
    v = to_heads(v)
    logf = jax.nn.log_sigmoid(fl.astype(jnp.float32) + b_f.astype(jnp.float32))
    c = jnp.cumsum(logf, axis=1).transpose(0, 2, 1)
    o = fox_block_attention(q, k, v, c)
    return o.transpose(0, 2, 1, 3).reshape(bsz, s, d) @ w_o


def conv_ffn(h, w_up, dw, dw_b, w_down):
    u = causal_dwconv(h @ w_up, dw, dw_b)
    val, gate = jnp.split(u, 2, axis=-1)
    return (jax.nn.silu(gate) * val) @ w_down


def _fwd_setup_inputs(seed: int = 0) -> dict:
    key = jax.random.key(seed)
    ks = iter(jax.random.split(key, 32))
    nrm = lambda shape, s: jax.random.normal(next(ks), shape, jnp.float32) * s
    D, F = D_MODEL, D_FF
    return {
        "x": nrm((BATCH, SEQ, D), 1.0),
        "norm_mix": 1.0 + nrm((DEPTH, D), 0.02),
        "norm_ffn": 1.0 + nrm((DEPTH, D), 0.02),
        "conv_w_in": nrm((N_A, D, 2 * D), D ** -0.5),
        "conv_b_in": nrm((N_A, 2 * D), 0.02),
        "conv_dw": nrm((N_A, CONV_WIDTH, D), CONV_WIDTH ** -0.5),
        "conv_dw_b": nrm((N_A, D), 0.02),
        "conv_ln_g": 1.0 + nrm((N_A, D), 0.02),
        "conv_ln_b": nrm((N_A, D), 0.02),
        "conv_w_out": nrm((N_A, D, D), D ** -0.5),
        "conv_b_out": nrm((N_A, D), 0.02),
        "pool_w": nrm((N_B, N_POOL_GROUPS, POOL_GROUP_DIM, POOL_GROUP_DIM), POOL_GROUP_DIM ** -0.5),
        "pool_b": nrm((N_B, N_POOL_GROUPS, POOL_GROUP_DIM), 0.02),
        "pool_scale": 0.5 + nrm((N_B, D), 0.05),
        "fox_w_in": nrm((N_C, D, 3 * D + N_HEADS), D ** -0.5),
        "fox_b_f": 2.0 + nrm((N_C, N_HEADS), 0.5),
        "fox_q_gain": 1.0 + nrm((N_C, HEAD_DIM), 0.02),
        "fox_k_gain": 1.0 + nrm((N_C, HEAD_DIM), 0.02),
        "fox_w_o": nrm((N_C, D, D), D ** -0.5),
        "ffn_w_up": nrm((DEPTH, D, 2 * F), D ** -0.5),
        "ffn_dw": nrm((DEPTH, FFN_CONV_WIDTH, 2 * F), FFN_CONV_WIDTH ** -0.5),
        "ffn_dw_b": nrm((DEPTH, 2 * F), 0.02),
        "ffn_w_down": nrm((DEPTH, F, D), F ** -0.5),
    }


def _fwd_reference(x, norm_mix, norm_ffn, conv_w_in, conv_b_in, conv_dw, conv_dw_b, conv_ln_g, conv_ln_b,
              conv_w_out, conv_b_out, pool_w, pool_b, pool_scale, fox_w_in, fox_b_f, fox_q_gain,
              fox_k_gain, fox_w_o, ffn_w_up, ffn_dw, ffn_dw_b, ffn_w_down):
    for i in range(DEPTH):
        j = i // N_MIXERS
        h = rms_norm(x, norm_mix[i])
        kind = i % N_MIXERS
        if kind == 0:
            y = conformer_conv(h, conv_w_in[j], conv_b_in[j], conv_dw[j], conv_dw_b[j],
                               conv_ln_g[j], conv_ln_b[j], conv_w_out[j], conv_b_out[j])
        elif kind == 1:
            y = multiscale_pool(h, pool_w[j], pool_b[j], pool_scale[j])
        else:
            y = forgetting_attention(h, fox_w_in[j], fox_b_f[j], fox_q_gain[j], fox_k_gain[j], fox_w_o[j])
        x = x + y
        x = x + conv_ffn(rms_norm(x, norm_ffn[i]), ffn_w_up[i], ffn_dw[i], ffn_dw_b[i], ffn_w_down[i])
    return x


import jax as _jax
import jax.numpy as _jnp

TWIN_FORMAT = 'train_step'
FWD_PARAMS = ['x', 'norm_mix', 'norm_ffn', 'conv_w_in', 'conv_b_in', 'conv_dw', 'conv_dw_b', 'conv_ln_g', 'conv_ln_b', 'conv_w_out', 'conv_b_out', 'pool_w', 'pool_b', 'pool_scale', 'fox_w_in', 'fox_b_f', 'fox_q_gain', 'fox_k_gain', 'fox_w_o', 'ffn_w_up', 'ffn_dw', 'ffn_dw_b', 'ffn_w_down']
TWIN_WEIGHTS = ['norm_mix', 'norm_ffn', 'conv_w_in', 'conv_b_in', 'conv_dw', 'conv_dw_b', 'conv_ln_g', 'conv_ln_b', 'conv_w_out', 'conv_b_out', 'pool_w', 'pool_b', 'pool_scale', 'fox_w_in', 'fox_b_f', 'fox_q_gain', 'fox_k_gain', 'fox_w_o', 'ffn_w_up', 'ffn_dw', 'ffn_dw_b', 'ffn_w_down']
TWIN_DIFF_INPUT = 'x'
TWIN_INPUTS = ['x', 'norm_mix', 'norm_ffn', 'conv_w_in', 'conv_b_in', 'conv_dw', 'conv_dw_b', 'conv_ln_g', 'conv_ln_b', 'conv_w_out', 'conv_b_out', 'pool_w', 'pool_b', 'pool_scale', 'fox_w_in', 'fox_b_f', 'fox_q_gain', 'fox_k_gain', 'fox_w_o', 'ffn_w_up', 'ffn_dw', 'ffn_dw_b', 'ffn_w_down', 'loss_target', 'm_norm_mix', 'm_norm_ffn', 'm_conv_w_in', 'm_conv_b_in', 'm_conv_dw', 'm_conv_dw_b', 'm_conv_ln_g', 'm_conv_ln_b', 'm_conv_w_out', 'm_conv_b_out', 'm_pool_w', 'm_pool_b', 'm_pool_scale', 'm_fox_w_in', 'm_fox_b_f', 'm_fox_q_gain', 'm_fox_k_gain', 'm_fox_w_o', 'm_ffn_w_up', 'm_ffn_dw', 'm_ffn_dw_b', 'm_ffn_w_down', 'v_norm_mix', 'v_norm_ffn', 'v_conv_w_in', 'v_conv_b_in', 'v_conv_dw', 'v_conv_dw_b', 'v_conv_ln_g', 'v_conv_ln_b', 'v_conv_w_out', 'v_conv_b_out', 'v_pool_w', 'v_pool_b', 'v_pool_scale', 'v_fox_w_in', 'v_fox_b_f', 'v_fox_q_gain', 'v_fox_k_gain', 'v_fox_w_o', 'v_ffn_w_up', 'v_ffn_dw', 'v_ffn_dw_b', 'v_ffn_w_down']
TWIN_OUTPUTS = ['loss', 'grad_x', 'grad_norm_mix', 'grad_norm_ffn', 'grad_conv_w_in', 'grad_conv_b_in', 'grad_conv_dw', 'grad_conv_dw_b', 'grad_conv_ln_g', 'grad_conv_ln_b', 'grad_conv_w_out', 'grad_conv_b_out', 'grad_pool_w', 'grad_pool_b', 'grad_pool_scale', 'grad_fox_w_in', 'grad_fox_b_f', 'grad_fox_q_gain', 'grad_fox_k_gain', 'grad_fox_w_o', 'grad_ffn_w_up', 'grad_ffn_dw', 'grad_ffn_dw_b', 'grad_ffn_w_down', 'delta_norm_mix', 'delta_norm_ffn', 'delta_conv_w_in', 'delta_conv_b_in', 'delta_conv_dw', 'delta_conv_dw_b', 'delta_conv_ln_g', 'delta_conv_ln_b', 'delta_conv_w_out', 'delta_conv_b_out', 'delta_pool_w', 'delta_pool_b', 'delta_pool_scale', 'delta_fox_w_in', 'delta_fox_b_f', 'delta_fox_q_gain', 'delta_fox_k_gain', 'delta_fox_w_o', 'delta_ffn_w_up', 'delta_ffn_dw', 'delta_ffn_dw_b', 'delta_ffn_w_down', 'new_m_norm_mix', 'new_m_norm_ffn', 'new_m_conv_w_in', 'new_m_conv_b_in', 'new_m_conv_dw', 'new_m_conv_dw_b', 'new_m_conv_ln_g', 'new_m_conv_ln_b', 'new_m_conv_w_out', 'new_m_conv_b_out', 'new_m_pool_w', 'new_m_pool_b', 'new_m_pool_scale', 'new_m_fox_w_in', 'new_m_fox_b_f', 'new_m_fox_q_gain', 'new_m_fox_k_gain', 'new_m_fox_w_o', 'new_m_ffn_w_up', 'new_m_ffn_dw', 'new_m_ffn_dw_b', 'new_m_ffn_w_down', 'new_v_norm_mix', 'new_v_norm_ffn', 'new_v_conv_w_in', 'new_v_conv_b_in', 'new_v_conv_dw', 'new_v_conv_dw_b', 'new_v_conv_ln_g', 'new_v_conv_ln_b', 'new_v_conv_w_out', 'new_v_conv_b_out', 'new_v_pool_w', 'new_v_pool_b', 'new_v_pool_scale', 'new_v_fox_w_in', 'new_v_fox_b_f', 'new_v_fox_q_gain', 'new_v_fox_k_gain', 'new_v_fox_w_o', 'new_v_ffn_w_up', 'new_v_ffn_dw', 'new_v_ffn_dw_b', 'new_v_ffn_w_down']
TWIN_LEAF_KINDS = {'loss': 'loss', 'grad_x': 'grad_x', 'grad_norm_mix': 'grad_w', 'grad_norm_ffn': 'grad_w', 'grad_conv_w_in': 'grad_w', 'grad_conv_b_in': 'grad_w', 'grad_conv_dw': 'grad_w', 'grad_conv_dw_b': 'grad_w', 'grad_conv_ln_g': 'grad_w', 'grad_conv_ln_b': 'grad_w', 'grad_conv_w_out': 'grad_w', 'grad_conv_b_out': 'grad_w', 'grad_pool_w': 'grad_w', 'grad_pool_b': 'grad_w', 'grad_pool_scale': 'grad_w', 'grad_fox_w_in': 'grad_w', 'grad_fox_b_f': 'grad_w', 'grad_fox_q_gain': 'grad_w', 'grad_fox_k_gain': 'grad_w', 'grad_fox_w_o': 'grad_w', 'grad_ffn_w_up': 'grad_w', 'grad_ffn_dw': 'grad_w', 'grad_ffn_dw_b': 'grad_w', 'grad_ffn_w_down': 'grad_w', 'delta_norm_mix': 'delta_w', 'delta_norm_ffn': 'delta_w', 'delta_conv_w_in': 'delta_w', 'delta_conv_b_in': 'delta_w', 'delta_conv_dw': 'delta_w', 'delta_conv_dw_b': 'delta_w', 'delta_conv_ln_g': 'delta_w', 'delta_conv_ln_b': 'delta_w', 'delta_conv_w_out': 'delta_w', 'delta_conv_b_out': 'delta_w', 'delta_pool_w': 'delta_w', 'delta_pool_b': 'delta_w', 'delta_pool_scale': 'delta_w', 'delta_fox_w_in': 'delta_w', 'delta_fox_b_f': 'delta_w', 'delta_fox_q_gain': 'delta_w', 'delta_fox_k_gain': 'delta_w', 'delta_fox_w_o': 'delta_w', 'delta_ffn_w_up': 'delta_w', 'delta_ffn_dw': 'delta_w', 'delta_ffn_dw_b': 'delta_w', 'delta_ffn_w_down': 'delta_w', 'new_m_norm_mix': 'new_m', 'new_m_norm_ffn': 'new_m', 'new_m_conv_w_in': 'new_m', 'new_m_conv_b_in': 'new_m', 'new_m_conv_dw': 'new_m', 'new_m_conv_dw_b': 'new_m', 'new_m_conv_ln_g': 'new_m', 'new_m_conv_ln_b': 'new_m', 'new_m_conv_w_out': 'new_m', 'new_m_conv_b_out': 'new_m', 'new_m_pool_w': 'new_m', 'new_m_pool_b': 'new_m', 'new_m_pool_scale': 'new_m', 'new_m_fox_w_in': 'new_m', 'new_m_fox_b_f': 'new_m', 'new_m_fox_q_gain': 'new_m', 'new_m_fox_k_gain': 'new_m', 'new_m_fox_w_o': 'new_m', 'new_m_ffn_w_up': 'new_m', 'new_m_ffn_dw': 'new_m', 'new_m_ffn_dw_b': 'new_m', 'new_m_ffn_w_down': 'new_m', 'new_v_norm_mix': 'new_v', 'new_v_norm_ffn': 'new_v', 'new_v_conv_w_in': 'new_v', 'new_v_conv_b_in': 'new_v', 'new_v_conv_dw': 'new_v', 'new_v_conv_dw_b': 'new_v', 'new_v_conv_ln_g': 'new_v', 'new_v_conv_ln_b': 'new_v', 'new_v_conv_w_out': 'new_v', 'new_v_conv_b_out': 'new_v', 'new_v_pool_w': 'new_v', 'new_v_pool_b': 'new_v', 'new_v_pool_scale': 'new_v', 'new_v_fox_w_in': 'new_v', 'new_v_fox_b_f': 'new_v', 'new_v_fox_q_gain': 'new_v', 'new_v_fox_k_gain': 'new_v', 'new_v_fox_w_o': 'new_v', 'new_v_ffn_w_up': 'new_v', 'new_v_ffn_dw': 'new_v', 'new_v_ffn_dw_b': 'new_v', 'new_v_ffn_w_down': 'new_v'}


def _forward(args):
    return _fwd_reference(*[args[k] for k in FWD_PARAMS])


def _output_shape():
    out = _jax.eval_shape(lambda: _forward(_fwd_setup_inputs(0)))
    return out.shape, out.dtype

N_MICROBATCH = 1
ADAM_LR = 0.001
ADAM_B1 = 0.9
ADAM_B2 = 0.999
ADAM_EPS = 1e-08
ADAM_WD = 0.01
ADAM_STEP = 10
PER_EXAMPLE_BATCH_AXIS = {'x': 0, 'loss_target': 0}
SHARED_INPUTS = []
_WEIGHT_DTYPES = {'norm_mix': _jnp.float32, 'norm_ffn': _jnp.float32, 'conv_w_in': _jnp.float32, 'conv_b_in': _jnp.float32, 'conv_dw': _jnp.float32, 'conv_dw_b': _jnp.float32, 'conv_ln_g': _jnp.float32, 'conv_ln_b': _jnp.float32, 'conv_w_out': _jnp.float32, 'conv_b_out': _jnp.float32, 'pool_w': _jnp.float32, 'pool_b': _jnp.float32, 'pool_scale': _jnp.float32, 'fox_w_in': _jnp.float32, 'fox_b_f': _jnp.float32, 'fox_q_gain': _jnp.float32, 'fox_k_gain': _jnp.float32, 'fox_w_o': _jnp.float32, 'ffn_w_up': _jnp.float32, 'ffn_dw': _jnp.float32, 'ffn_dw_b': _jnp.float32, 'ffn_w_down': _jnp.float32}
MOMENT_SCALE = {'norm_mix': 8.296337e+00, 'norm_ffn': 5.256689e+01, 'conv_w_in': 8.209680e-01, 'conv_b_in': 1.585180e+01, 'conv_dw': 1.920802e+00, 'conv_dw_b': 3.494305e+01, 'conv_ln_g': 3.216481e+01, 'conv_ln_b': 2.819803e+01, 'conv_w_out': 7.989604e+00, 'conv_b_out': 4.324596e+01, 'pool_w': 1.328940e+00, 'pool_b': 2.546460e+01, 'pool_scale': 2.367495e+01, 'fox_w_in': 2.896236e+00, 'fox_b_f': 1.912450e+02, 'fox_q_gain': 5.861146e+01, 'fox_k_gain': 5.880637e+01, 'fox_w_o': 4.041756e+00, 'ffn_w_up': 1.265804e+00, 'ffn_dw': 7.266421e+00, 'ffn_dw_b': 7.727593e+00, 'ffn_w_down': 1.171630e+00}


def _to_microbatches(a, axis):
    t = _jnp.moveaxis(a, axis, 0)
    t = t.reshape((N_MICROBATCH, t.shape[0] // N_MICROBATCH) + t.shape[1:])
    return _jnp.moveaxis(t, 1, axis + 1)


def setup_inputs(seed: int = 0) -> dict:
    inp = _fwd_setup_inputs(seed)
    key = _jax.random.fold_in(_jax.random.key(seed), 7919)
    shape, _ = _output_shape()
    out = dict(inp)
    out["loss_target"] = _jax.random.normal(_jax.random.fold_in(key, 0), shape, _jnp.float32)
    for i, name in enumerate(TWIN_WEIGHTS):
        w = inp[name].astype(_jnp.float32)
        if MOMENT_SCALE is None:
            s = _jnp.sqrt(_jnp.mean(_jnp.square(w)) + 1e-30)
        else:
            s = MOMENT_SCALE[name]
        km, kv = _jax.random.split(_jax.random.fold_in(key, i + 1))
        out[name] = w
        out["m_" + name] = s * _jax.random.normal(km, w.shape, _jnp.float32)
        out["v_" + name] = (s * s) * _jax.random.uniform(kv, w.shape, _jnp.float32, 0.5, 1.5)
    if N_MICROBATCH > 1:
        for name, axis in PER_EXAMPLE_BATCH_AXIS.items():
            out[name] = _to_microbatches(out[name], axis)
    return {'x': out['x'], 'norm_mix': out['norm_mix'], 'norm_ffn': out['norm_ffn'], 'conv_w_in': out['conv_w_in'], 'conv_b_in': out['conv_b_in'], 'conv_dw': out['conv_dw'], 'conv_dw_b': out['conv_dw_b'], 'conv_ln_g': out['conv_ln_g'], 'conv_ln_b': out['conv_ln_b'], 'conv_w_out': out['conv_w_out'], 'conv_b_out': out['conv_b_out'], 'pool_w': out['pool_w'], 'pool_b': out['pool_b'], 'pool_scale': out['pool_scale'], 'fox_w_in': out['fox_w_in'], 'fox_b_f': out['fox_b_f'], 'fox_q_gain': out['fox_q_gain'], 'fox_k_gain': out['fox_k_gain'], 'fox_w_o': out['fox_w_o'], 'ffn_w_up': out['ffn_w_up'], 'ffn_dw': out['ffn_dw'], 'ffn_dw_b': out['ffn_dw_b'], 'ffn_w_down': out['ffn_w_down'], 'loss_target': out['loss_target'], 'm_norm_mix': out['m_norm_mix'], 'm_norm_ffn': out['m_norm_ffn'], 'm_conv_w_in': out['m_conv_w_in'], 'm_conv_b_in': out['m_conv_b_in'], 'm_conv_dw': out['m_conv_dw'], 'm_conv_dw_b': out['m_conv_dw_b'], 'm_conv_ln_g': out['m_conv_ln_g'], 'm_conv_ln_b': out['m_conv_ln_b'], 'm_conv_w_out': out['m_conv_w_out'], 'm_conv_b_out': out['m_conv_b_out'], 'm_pool_w': out['m_pool_w'], 'm_pool_b': out['m_pool_b'], 'm_pool_scale': out['m_pool_scale'], 'm_fox_w_in': out['m_fox_w_in'], 'm_fox_b_f': out['m_fox_b_f'], 'm_fox_q_gain': out['m_fox_q_gain'], 'm_fox_k_gain': out['m_fox_k_gain'], 'm_fox_w_o': out['m_fox_w_o'], 'm_ffn_w_up': out['m_ffn_w_up'], 'm_ffn_dw': out['m_ffn_dw'], 'm_ffn_dw_b': out['m_ffn_dw_b'], 'm_ffn_w_down': out['m_ffn_w_down'], 'v_norm_mix': out['v_norm_mix'], 'v_norm_ffn': out['v_norm_ffn'], 'v_conv_w_in': out['v_conv_w_in'], 'v_conv_b_in': out['v_conv_b_in'], 'v_conv_dw': out['v_conv_dw'], 'v_conv_dw_b': out['v_conv_dw_b'], 'v_conv_ln_g': out['v_conv_ln_g'], 'v_conv_ln_b': out['v_conv_ln_b'], 'v_conv_w_out': out['v_conv_w_out'], 'v_conv_b_out': out['v_conv_b_out'], 'v_pool_w': out['v_pool_w'], 'v_pool_b': out['v_pool_b'], 'v_pool_scale': out['v_pool_scale'], 'v_fox_w_in': out['v_fox_w_in'], 'v_fox_b_f': out['v_fox_b_f'], 'v_fox_q_gain': out['v_fox_q_gain'], 'v_fox_k_gain': out['v_fox_k_gain'], 'v_fox_w_o': out['v_fox_w_o'], 'v_ffn_w_up': out['v_ffn_w_up'], 'v_ffn_dw': out['v_ffn_dw'], 'v_ffn_dw_b': out['v_ffn_dw_b'], 'v_ffn_w_down': out['v_ffn_w_down']}


def _loss(weights, diff, rest, loss_target):
    with _jax.named_scope("forward"):
        args = {**rest, TWIN_DIFF_INPUT: diff, **{k: w.astype(_WEIGHT_DTYPES[k]) for k, w in weights.items()}}
        y = _forward(args)
    with _jax.named_scope("loss_head"):
        err = _jnp.square(y.astype(_jnp.float32) - loss_target)
        return 0.5 * _jnp.sum(_jnp.mean(err, axis=-1)) if err.ndim else 0.5 * err


def _adamw(w, g, m, v):
    m = ADAM_B1 * m + (1.0 - ADAM_B1) * g
    v = ADAM_B2 * v + (1.0 - ADAM_B2) * _jnp.square(g)
    m_hat = m / (1.0 - ADAM_B1 ** ADAM_STEP)
    v_hat = v / (1.0 - ADAM_B2 ** ADAM_STEP)
    delta = -ADAM_LR * (m_hat / (_jnp.sqrt(v_hat) + ADAM_EPS) + ADAM_WD * w)
    return delta, m, v


def reference(x, norm_mix, norm_ffn, conv_w_in, conv_b_in, conv_dw, conv_dw_b, conv_ln_g, conv_ln_b, conv_w_out, conv_b_out, pool_w, pool_b, pool_scale, fox_w_in, fox_b_f, fox_q_gain, fox_k_gain, fox_w_o, ffn_w_up, ffn_dw, ffn_dw_b, ffn_w_down, loss_target, m_norm_mix, m_norm_ffn, m_conv_w_in, m_conv_b_in, m_conv_dw, m_conv_dw_b, m_conv_ln_g, m_conv_ln_b, m_conv_w_out, m_conv_b_out, m_pool_w, m_pool_b, m_pool_scale, m_fox_w_in, m_fox_b_f, m_fox_q_gain, m_fox_k_gain, m_fox_w_o, m_ffn_w_up, m_ffn_dw, m_ffn_dw_b, m_ffn_w_down, v_norm_mix, v_norm_ffn, v_conv_w_in, v_conv_b_in, v_conv_dw, v_conv_dw_b, v_conv_ln_g, v_conv_ln_b, v_conv_w_out, v_conv_b_out, v_pool_w, v_pool_b, v_pool_scale, v_fox_w_in, v_fox_b_f, v_fox_q_gain, v_fox_k_gain, v_fox_w_o, v_ffn_w_up, v_ffn_dw, v_ffn_dw_b, v_ffn_w_down):
    given = dict(x=x, norm_mix=norm_mix, norm_ffn=norm_ffn, conv_w_in=conv_w_in, conv_b_in=conv_b_in, conv_dw=conv_dw, conv_dw_b=conv_dw_b, conv_ln_g=conv_ln_g, conv_ln_b=conv_ln_b, conv_w_out=conv_w_out, conv_b_out=conv_b_out, pool_w=pool_w, pool_b=pool_b, pool_scale=pool_scale, fox_w_in=fox_w_in, fox_b_f=fox_b_f, fox_q_gain=fox_q_gain, fox_k_gain=fox_k_gain, fox_w_o=fox_w_o, ffn_w_up=ffn_w_up, ffn_dw=ffn_dw, ffn_dw_b=ffn_dw_b, ffn_w_down=ffn_w_down, loss_target=loss_target, m_norm_mix=m_norm_mix, m_norm_ffn=m_norm_ffn, m_conv_w_in=m_conv_w_in, m_conv_b_in=m_conv_b_in, m_conv_dw=m_conv_dw, m_conv_dw_b=m_conv_dw_b, m_conv_ln_g=m_conv_ln_g, m_conv_ln_b=m_conv_ln_b, m_conv_w_out=m_conv_w_out, m_conv_b_out=m_conv_b_out, m_pool_w=m_pool_w, m_pool_b=m_pool_b, m_pool_scale=m_pool_scale, m_fox_w_in=m_fox_w_in, m_fox_b_f=m_fox_b_f, m_fox_q_gain=m_fox_q_gain, m_fox_k_gain=m_fox_k_gain, m_fox_w_o=m_fox_w_o, m_ffn_w_up=m_ffn_w_up, m_ffn_dw=m_ffn_dw, m_ffn_dw_b=m_ffn_dw_b, m_ffn_w_down=m_ffn_w_down, v_norm_mix=v_norm_mix, v_norm_ffn=v_norm_ffn, v_conv_w_in=v_conv_w_in, v_conv_b_in=v_conv_b_in, v_conv_dw=v_conv_dw, v_conv_dw_b=v_conv_dw_b, v_conv_ln_g=v_conv_ln_g, v_conv_ln_b=v_conv_ln_b, v_conv_w_out=v_conv_w_out, v_conv_b_out=v_conv_b_out, v_pool_w=v_pool_w, v_pool_b=v_pool_b, v_pool_scale=v_pool_scale, v_fox_w_in=v_fox_w_in, v_fox_b_f=v_fox_b_f, v_fox_q_gain=v_fox_q_gain, v_fox_k_gain=v_fox_k_gain, v_fox_w_o=v_fox_w_o, v_ffn_w_up=v_ffn_w_up, v_ffn_dw=v_ffn_dw, v_ffn_dw_b=v_ffn_dw_b, v_ffn_w_down=v_ffn_w_down)
    weights = {n: given[n] for n in TWIN_WEIGHTS}
    shared = {n: given[n] for n in SHARED_INPUTS}
    per_example = {n: given[n] for n in ['x']}
    grad_fn = _jax.value_and_grad(_loss, argnums=(0, 1))

    def one_microbatch(ex, loss_target):
        ex = dict(ex)
        diff = ex.pop(TWIN_DIFF_INPUT)
        return grad_fn(weights, diff, {**shared, **ex}, loss_target)

    if N_MICROBATCH == 1:
        loss, (grad_w, grad_x) = one_microbatch(per_example, given["loss_target"])
    else:
        def body(carry, xs):
            loss_sum, grad_sum = carry
            l_k, (gw_k, gx_k) = one_microbatch(xs[0], xs[1])
            with _jax.named_scope("update"):
                return (loss_sum + l_k, _jax.tree.map(_jnp.add, grad_sum, gw_k)), gx_k

        init = (_jnp.zeros((), _jnp.float32), _jax.tree.map(_jnp.zeros_like, weights))
        (loss, grad_w), grad_x = _jax.lax.scan(body, init, (per_example, given["loss_target"]))
    with _jax.named_scope("update"):
        delta_w, new_m, new_v = {}, {}, {}
        for n in TWIN_WEIGHTS:
            delta_w[n], new_m[n], new_v[n] = _adamw(weights[n], grad_w[n], given["m_" + n], given["v_" + n])
    return (loss, grad_x, *[grad_w[n] for n in TWIN_WEIGHTS], *[delta_w[n] for n in TWIN_WEIGHTS],
            *[new_m[n] for n in TWIN_WEIGHTS], *[new_v[n] for n in TWIN_WEIGHTS])
```

```python
import functools
import math

import jax
import jax.numpy as jnp
from jax import lax
from jax.experimental import pallas as pl
from jax.experimental.pallas import tpu as pltpu

F32, BF16 = jnp.float32, jnp.bfloat16
SDS = jax.ShapeDtypeStruct

N_DEV = 8
EPS = 1e-6
POOL_WINDOWS = (2, 4, 8, 16)
HEAD_DIM = 64
ADAM_LR, ADAM_B1, ADAM_B2, ADAM_EPS, ADAM_WD, ADAM_STEP = 0.001, 0.9, 0.999, 1e-08, 0.01, 10
LANES = 128
VMEM_LIMIT_BYTES = 48 * 1024 * 1024
NEG = -1e30

WEIGHTS = ['norm_mix', 'norm_ffn', 'conv_w_in', 'conv_b_in', 'conv_dw', 'conv_dw_b', 'conv_ln_g', 'conv_ln_b',
           'conv_w_out', 'conv_b_out', 'pool_w', 'pool_b', 'pool_scale', 'fox_w_in', 'fox_b_f', 'fox_q_gain',
           'fox_k_gain', 'fox_w_o', 'ffn_w_up', 'ffn_dw', 'ffn_dw_b', 'ffn_w_down']
SHARD_AXIS = {'conv_w_in': 2, 'conv_b_in': 1, 'conv_dw': 2, 'conv_dw_b': 1, 'conv_ln_g': 1, 'conv_ln_b': 1,
              'conv_w_out': 1, 'conv_b_out': 1, 'pool_w': 2, 'pool_b': 2, 'fox_w_in': 2, 'fox_w_o': 1,
              'ffn_w_up': 2, 'ffn_dw': 2, 'ffn_w_down': 1}
MATRICES = ('conv_w_in', 'conv_w_out', 'pool_w', 'fox_w_in', 'fox_w_o', 'ffn_w_up', 'ffn_w_down')
SHARDED = [n for n in WEIGHTS if n in SHARD_AXIS]
REPLICATED = [n for n in WEIGHTS if n not in SHARD_AXIS]


def _cp(*sem):
    return pltpu.CompilerParams(dimension_semantics=sem, vmem_limit_bytes=VMEM_LIMIT_BYTES)


def _tile(n, pref, align=8):
    if n <= pref:
        return n
    t = (pref // align) * align
    while t >= align:
        if n % t == 0:
            return t
        t -= align
    return n


def _fold8(x):
    r, c = x.shape
    return x.reshape(r // 8, 8, c).sum(axis=0)


def _sigmoid(x):
    return 1.0 / (1.0 + jnp.exp(-x))


def _shifts_back(cur, tail, n):
    hb = tail.shape[0]
    xe = jnp.concatenate([tail, cur], axis=0)
    return [cur] + [pltpu.roll(xe, j, axis=0)[hb:] for j in range(1, n)]


def _shifts_fwd(cur, head, n):
    ts = cur.shape[0]
    xe = jnp.concatenate([cur, head], axis=0)
    ln = xe.shape[0]
    return [cur] + [pltpu.roll(xe, ln - j, axis=0)[:ts] for j in range(1, n)]


def _dot_hi(a, b):
    return jnp.dot(a, b, preferred_element_type=F32, precision=lax.Precision.HIGHEST)


def _rmsnorm_fwd(x, g):
    t, d = x.shape
    tm = _tile(t, 512)

    def body(x_ref, g_ref, h_ref):
        xv = x_ref[...]
        r = lax.rsqrt(jnp.mean(xv * xv, axis=-1, keepdims=True) + EPS)
        h_ref[...] = (xv * r * g_ref[...]).astype(BF16)

    return pl.pallas_call(
        body, out_shape=SDS((t, d), BF16), grid=(t // tm,), name="rmsnorm_fwd",
        in_specs=[pl.BlockSpec((tm, d), lambda i: (i, 0)), pl.BlockSpec((1, d), lambda i: (0, 0))],
        out_specs=pl.BlockSpec((tm, d), lambda i: (i, 0)), compiler_params=_cp("parallel"))(x, g)


def _rmsnorm_bwd(x, g, dh, dres):
    t, d = x.shape
    tm = _tile(t, 512)

    def body(x_ref, g_ref, dh_ref, dres_ref, dx_ref, dg_ref):
        @pl.when(pl.program_id(0) == 0)
        def _():
            dg_ref[...] = jnp.zeros_like(dg_ref)

        xv = x_ref[...]
        r = lax.rsqrt(jnp.mean(xv * xv, axis=-1, keepdims=True) + EPS)
        xh = xv * r
        dhv = dh_ref[...].astype(F32)
        u = dhv * g_ref[...]
        dx_ref[...] = dres_ref[...] + r * (u - xh * jnp.mean(u * xh, axis=-1, keepdims=True))
        dg_ref[...] += _fold8(dhv * xh)

    row = pl.BlockSpec((tm, d), lambda i: (i, 0))
    return pl.pallas_call(
        body, out_shape=(SDS((t, d), F32), SDS((8, d), F32)), grid=(t // tm,), name="rmsnorm_bwd",
        in_specs=[row, pl.BlockSpec((1, d), lambda i: (0, 0)), row, row],
        out_specs=(row, pl.BlockSpec((8, d), lambda i: (0, 0))), compiler_params=_cp("arbitrary"))(x, g, dh, dres)


def _mm(a, b, *, trans_b=False, bias=None, residual=None, out_dtype=BF16, name="mm"):
    m, k = a.shape
    n = b.shape[0] if trans_b else b.shape[1]
    tm, tn, tk = _tile(m, 512, 16), _tile(n, 1024, LANES), _tile(k, 1536, LANES)
    nk = k // tk
    dims = (((1,), (1,)), ((), ())) if trans_b else (((1,), (0,)), ((), ()))
    has_bias, has_res = bias is not None, residual is not None

    def body(*refs):
        a_ref, b_ref = refs[0], refs[1]
        bias_ref = refs[2] if has_bias else None
        res_ref = refs[2 + has_bias] if has_res else None
        o_ref, acc_ref = refs[-2], refs[-1]
        kk = pl.program_id(2)

        @pl.when(kk == 0)
        def _():
            acc_ref[...] = jnp.zeros_like(acc_ref)

        acc_ref[...] += lax.dot_general(a_ref[...].astype(BF16), b_ref[...].astype(BF16), dims,
                                        preferred_element_type=F32)

        @pl.when(kk == nk - 1)
        def _():
            r = acc_ref[...]
            if has_bias:
                r = r + bias_ref[...]
            if has_res:
                r = r + res_ref[...]
            o_ref[...] = r.astype(out_dtype)

    in_specs = [pl.BlockSpec((tm, tk), lambda i, j, kk: (i, kk)),
                pl.BlockSpec((tn, tk), lambda i, j, kk: (j, kk)) if trans_b
                else pl.BlockSpec((tk, tn), lambda i, j, kk: (kk, j))]
    args = [a, b]
    if has_bias:
        in_specs.append(pl.BlockSpec((1, tn), lambda i, j, kk: (0, j)))
        args.append(bias)
    if has_res:
        in_specs.append(pl.BlockSpec((tm, tn), lambda i, j, kk: (i, j)))
        args.append(residual)
    return pl.pallas_call(
        body, out_shape=SDS((m, n), out_dtype), grid=(m // tm, n // tn, nk), name=name,
        in_specs=in_specs, out_specs=pl.BlockSpec((tm, tn), lambda i, j, kk: (i, j)),
        scratch_shapes=[pltpu.VMEM((tm, tn), F32)],
        compiler_params=_cp("parallel", "parallel", "arbitrary"))(*args)


def _wgrad(a, g, *, name="wgrad"):
    m, ka = a.shape
    n = g.shape[1]
    ta, tn, tm = _tile(ka, 512, LANES), _tile(n, 1024, LANES), _tile(m, 1024)
    nm = m // tm

    def body(a_ref, g_ref, o_ref, cs_ref):
        i, mm = pl.program_id(1), pl.program_id(2)

        @pl.when(mm == 0)
        def _():
            o_ref[...] = jnp.zeros_like(o_ref)

        @pl.when((mm == 0) & (i == 0))
        def _():
            cs_ref[...] = jnp.zeros_like(cs_ref)

        gv = g_ref[...]
        o_ref[...] += lax.dot_general(a_ref[...].astype(BF16), gv.astype(BF16), (((0,), (0,)), ((), ())),
                                      preferred_element_type=F32)

        @pl.when(i == 0)
        def _():
            cs_ref[...] += _fold8(gv.astype(F32))

    return pl.pallas_call(
        body, out_shape=(SDS((ka, n), F32), SDS((8, n), F32)), grid=(n // tn, ka // ta, nm), name=name,
        in_specs=[pl.BlockSpec((tm, ta), lambda j, i, mm: (mm, i)), pl.BlockSpec((tm, tn), lambda j, i, mm: (mm, j))],
        out_specs=(pl.BlockSpec((ta, tn), lambda j, i, mm: (i, j)), pl.BlockSpec((8, tn), lambda j, i, mm: (0, j))),
        compiler_params=_cp("arbitrary", "arbitrary", "arbitrary"))(a, g)


FFN_HALO = 16


def _ffn_conv(uc_ref, up_ref, w_ref, b_ref, s):
    u = uc_ref[...].astype(F32)
    tail = jnp.where(s > 0, up_ref[...].astype(F32), 0.0)
    sh = _shifts_back(u, tail, 3)
    return sh, sh[2] * w_ref[0:1, :] + sh[1] * w_ref[1:2, :] + sh[0] * w_ref[2:3, :] + b_ref[...]


def _ffn_act_fwd(u, dw8, b):
    bsz, s_len, f2 = u.shape
    f = f2 // 2
    tc, ts = _tile(f, 256, LANES), _tile(s_len, 512, FFN_HALO)
    nf, r = f // tc, ts // FFN_HALO

    def body(uv_ref, uvp_ref, ug_ref, ugp_ref, wv_ref, wg_ref, bv_ref, bg_ref, a_ref):
        s = pl.program_id(2)
        _, val = _ffn_conv(uv_ref, uvp_ref, wv_ref, bv_ref, s)
        _, gate = _ffn_conv(ug_ref, ugp_ref, wg_ref, bg_ref, s)
        a_ref[...] = (gate * _sigmoid(gate) * val).astype(BF16)

    def cur(off):
        return pl.BlockSpec((None, ts, tc), lambda bi, j, s: (bi, s, j + off))

    def prev(off):
        return pl.BlockSpec((None, FFN_HALO, tc), lambda bi, j, s: (bi, jnp.maximum(s * r - 1, 0), j + off))

    def par(rows, off):
        return pl.BlockSpec((rows, tc), lambda bi, j, s: (0, j + off))

    return pl.pallas_call(
        body, out_shape=SDS((bsz, s_len, f), BF16), grid=(bsz, nf, s_len // ts), name="ffn_act_fwd",
        in_specs=[cur(0), prev(0), cur(nf), prev(nf), par(8, 0), par(8, nf), par(1, 0), par(1, nf)],
        out_specs=pl.BlockSpec((None, ts, tc), lambda bi, j, s: (bi, s, j)),
        compiler_params=_cp("parallel", "parallel", "arbitrary"))(u, u, u, u, dw8, dw8, b, b)


def _ffn_act_bwd1(u, da, dw8, b):
    bsz, s_len, f2 = u.shape
    f = f2 // 2
    tc, ts = _tile(f, 256, LANES), _tile(s_len, 512, FFN_HALO)
    nf, r = f // tc, ts // FFN_HALO

    def body(uv_ref, uvp_ref, ug_ref, ugp_ref, da_ref, wv_ref, wg_ref, bv_ref, bg_ref, dv_ref, ddw_ref, db_ref):
        half, bi, s = pl.program_id(0), pl.program_id(2), pl.program_id(3)

        @pl.when((bi == 0) & (s == 0))
        def _():
            ddw_ref[...] = jnp.zeros_like(ddw_ref)
            db_ref[...] = jnp.zeros_like(db_ref)

        shv, val = _ffn_conv(uv_ref, uvp_ref, wv_ref, bv_ref, s)
        shg, gate = _ffn_conv(ug_ref, ugp_ref, wg_ref, bg_ref, s)
        sg = _sigmoid(gate)
        dav = da_ref[...].astype(F32)

        def emit(dv, sh):
            dv_ref[...] = dv.astype(BF16)
            db_ref[...] += _fold8(dv)
            for k in range(3):
                ddw_ref[k] += _fold8(dv * sh[2 - k])

        @pl.when(half == 0)
        def _():
            emit(dav * gate * sg, shv)

        @pl.when(half == 1)
        def _():
            emit(dav * val * (sg * (1.0 + gate * (1.0 - sg))), shg)

    def cur(off):
        return pl.BlockSpec((None, ts, tc), lambda h, j, bi, s: (bi, s, j + off))

    def prev(off):
        return pl.BlockSpec((None, FFN_HALO, tc), lambda h, j, bi, s: (bi, jnp.maximum(s * r - 1, 0), j + off))

    def par(rows, off):
        return pl.BlockSpec((rows, tc), lambda h, j, bi, s: (0, j + off))

    return pl.pallas_call(
        body, out_shape=(SDS((bsz, s_len, f2), BF16), SDS((3, 8, f2), F32), SDS((8, f2), F32)),
        grid=(2, nf, bsz, s_len // ts), name="ffn_act_bwd1",
        in_specs=[cur(0), prev(0), cur(nf), prev(nf),
                  pl.BlockSpec((None, ts, tc), lambda h, j, bi, s: (bi, s, j)),
                  par(8, 0), par(8, nf), par(1, 0), par(1, nf)],
        out_specs=(pl.BlockSpec((None, ts, tc), lambda h, j, bi, s: (bi, s, h * nf + j)),
                   pl.BlockSpec((3, 8, tc), lambda h, j, bi, s: (0, 0, h * nf + j)),
                   pl.BlockSpec((8, tc), lambda h, j, bi, s: (0, h * nf + j))),
        compiler_params=_cp("arbitrary", "arbitrary", "arbitrary", "arbitrary"))(u, u, u, u, da, dw8, dw8, b, b)


def _ffn_act_bwd2(dv, dw8):
    bsz, s_len, f2 = dv.shape
    tc, ts = _tile(f2, 512, LANES), _tile(s_len, 512, FFN_HALO)
    r, ns = ts // FFN_HALO, s_len // ts

    def body(dc_ref, dn_ref, w_ref, du_ref):
        s = pl.program_id(2)
        d = dc_ref[...].astype(F32)
        head = jnp.where(s < ns - 1, dn_ref[...].astype(F32), 0.0)
        sh = _shifts_fwd(d, head, 3)
        du_ref[...] = (sh[0] * w_ref[2:3, :] + sh[1] * w_ref[1:2, :] + sh[2] * w_ref[0:1, :]).astype(BF16)

    return pl.pallas_call(
        body, out_shape=SDS((bsz, s_len, f2), BF16), grid=(bsz, f2 // tc, ns), name="ffn_act_bwd2",
        in_specs=[pl.BlockSpec((None, ts, tc), lambda bi, j, s: (bi, s, j)),
                  pl.BlockSpec((None, FFN_HALO, tc),
                               lambda bi, j, s: (bi, jnp.minimum((s + 1) * r, s_len // FFN_HALO - 1), j)),
                  pl.BlockSpec((8, tc), lambda bi, j, s: (0, j))],
        out_specs=pl.BlockSpec((None, ts, tc), lambda bi, j, s: (bi, s, j)),
        compiler_params=_cp("parallel", "parallel", "arbitrary"))(dv, dv, dw8)


CONV_HALO = 32
CONV_CHUNK = 256


def _conv_act_fwd(p, dw32, dwb, ln_g, ln_b):
    bsz, s_len, d2 = p.shape
    d = d2 // 2
    kw = 31
    ts = _tile(s_len, 256, CONV_HALO)
    r = ts // CONV_HALO
    cc = min(CONV_CHUNK, d)

    def body(pc_ref, pp_ref, w_ref, wb_ref, g_ref, b_ref, u_ref, s_ref):
        s = pl.program_id(1)
        tot = jnp.zeros((ts, 1), F32)
        for c0 in range(0, d, cc):
            a = pc_ref[:, c0:c0 + cc].astype(F32)
            g = pc_ref[:, d + c0:d + c0 + cc].astype(F32)
            z = a * _sigmoid(g)
            ap = pp_ref[:, c0:c0 + cc].astype(F32)
            gp = pp_ref[:, d + c0:d + c0 + cc].astype(F32)
            tail = jnp.where(s > 0, ap * _sigmoid(gp), 0.0)
            sh = _shifts_back(z, tail, kw)
            acc = wb_ref[:, c0:c0 + cc] + sh[0] * w_ref[kw - 1:kw, c0:c0 + cc]
            for j in range(1, kw):
                acc = acc + sh[j] * w_ref[kw - 1 - j:kw - j, c0:c0 + cc]
            u_ref[:, c0:c0 + cc] = acc
            tot = tot + jnp.sum(acc, axis=-1, keepdims=True)
        u = u_ref[...]
        mu = tot / d
        uc = u - mu
        var = jnp.mean(uc * uc, axis=-1, keepdims=True)
        ul = uc * lax.rsqrt(var + EPS) * g_ref[...] + b_ref[...]
        s_ref[...] = (ul * _sigmoid(ul)).astype(BF16)

    vec = pl.BlockSpec((1, d), lambda bi, s: (0, 0))
    return pl.pallas_call(
        body, out_shape=(SDS((bsz, s_len, d), F32), SDS((bsz, s_len, d), BF16)), grid=(bsz, s_len // ts),
        name="conv_act_fwd",
        in_specs=[pl.BlockSpec((None, ts, d2), lambda bi, s: (bi, s, 0)),
                  pl.BlockSpec((None, CONV_HALO, d2), lambda bi, s: (bi, jnp.maximum(s * r - 1, 0), 0)),
                  pl.BlockSpec((32, d), lambda bi, s: (0, 0)), vec, vec, vec],
        out_specs=(pl.BlockSpec((None, ts, d), lambda bi, s: (bi, s, 0)),
                   pl.BlockSpec((None, ts, d), lambda bi, s: (bi, s, 0))),
        compiler_params=_cp("parallel", "arbitrary"))(p, p, dw32, dwb, ln_g, ln_b)


def _conv_act_bwd1(u, ds, ln_g, ln_b):
    t, d = u.shape
    ts = _tile(t, 256)

    def body(u_ref, ds_ref, g_ref, b_ref, du_ref, dg_ref, db_ref, dwb_ref):
        @pl.when(pl.program_id(0) == 0)
        def _():
            dg_ref[...] = jnp.zeros_like(dg_ref)
            db_ref[...] = jnp.zeros_like(db_ref)
            dwb_ref[...] = jnp.zeros_like(dwb_ref)

        uv = u_ref[...]
        uc = uv - jnp.mean(uv, axis=-1, keepdims=True)
        rstd = lax.rsqrt(jnp.mean(uc * uc, axis=-1, keepdims=True) + EPS)
        uh = uc * rstd
        ul = uh * g_ref[...] + b_ref[...]
        sg = _sigmoid(ul)
        dul = ds_ref[...].astype(F32) * (sg * (1.0 + ul * (1.0 - sg)))
        duh = dul * g_ref[...]
        du = rstd * (duh - jnp.mean(duh, axis=-1, keepdims=True) - uh * jnp.mean(duh * uh, axis=-1, keepdims=True))
        du_ref[...] = du
        dg_ref[...] += _fold8(dul * uh)
        db_ref[...] += _fold8(dul)
        dwb_ref[...] += _fold8(du)

    row = pl.BlockSpec((ts, d), lambda i: (i, 0))
    vec = pl.BlockSpec((1, d), lambda i: (0, 0))
    acc = pl.BlockSpec((8, d), lambda i: (0, 0))
    return pl.pallas_call(
        body, out_shape=(SDS((t, d), F32), SDS((8, d), F32), SDS((8, d), F32), SDS((8, d), F32)), grid=(t // ts,),
        name="conv_act_bwd1", in_specs=[row, row, vec, vec], out_specs=(row, acc, acc, acc),
        compiler_params=_cp("arbitrary"))(u, ds, ln_g, ln_b)


def _conv_act_bwd2(du, p, dw32):
    bsz, s_len, d2 = p.shape
    d = d2 // 2
    kw = 31
    ts = _tile(s_len, 256, CONV_HALO)
    r, ns = ts // CONV_HALO, s_len // ts
    cc = min(CONV_CHUNK, d)

    def body(dc_ref, dn_ref, pc_ref, pp_ref, w_ref, dp_ref, ddw_ref):
        bi, s = pl.program_id(0), pl.program_id(1)

        @pl.when((bi == 0) & (s == 0))
        def _():
            ddw_ref[...] = jnp.zeros_like(ddw_ref)

        for c0 in range(0, d, cc):
            a = pc_ref[:, c0:c0 + cc].astype(F32)
            g = pc_ref[:, d + c0:d + c0 + cc].astype(F32)
            sg = _sigmoid(g)
            z = a * sg
            ap = pp_ref[:, c0:c0 + cc].astype(F32)
            gp = pp_ref[:, d + c0:d + c0 + cc].astype(F32)
            tail = jnp.where(s > 0, ap * _sigmoid(gp), 0.0)
            duv = dc_ref[:, c0:c0 + cc]
            head = jnp.where(s < ns - 1, dn_ref[:, c0:c0 + cc], 0.0)
            zb = _shifts_back(z, tail, kw)
            for k in range(kw):
                ddw_ref[k, :, c0:c0 + cc] += _fold8(duv * zb[kw - 1 - k])
            df = _shifts_fwd(duv, head, kw)
            dz = df[0] * w_ref[kw - 1:kw, c0:c0 + cc]
            for j in range(1, kw):
                dz = dz + df[j] * w_ref[kw - 1 - j:kw - j, c0:c0 + cc]
            dp_ref[:, c0:c0 + cc] = (dz * sg).astype(BF16)
            dp_ref[:, d + c0:d + c0 + cc] = (dz * a * sg * (1.0 - sg)).astype(BF16)

    return pl.pallas_call(
        body, out_shape=(SDS((bsz, s_len, d2), BF16), SDS((32, 8, d), F32)), grid=(bsz, ns), name="conv_act_bwd2",
        in_specs=[pl.BlockSpec((None, ts, d), lambda bi, s: (bi, s, 0)),
                  pl.BlockSpec((None, CONV_HALO, d),
                               lambda bi, s: (bi, jnp.minimum((s + 1) * r, s_len // CONV_HALO - 1), 0)),
                  pl.BlockSpec((None, ts, d2), lambda bi, s: (bi, s, 0)),
                  pl.BlockSpec((None, CONV_HALO, d2), lambda bi, s: (bi, jnp.maximum(s * r - 1, 0), 0)),
                  pl.BlockSpec((32, d), lambda bi, s: (0, 0))],
        out_specs=(pl.BlockSpec((None, ts, d2), lambda bi, s: (bi, s, 0)),
                   pl.BlockSpec((32, 8, d), lambda bi, s: (0, 0, 0))),
        compiler_params=_cp("arbitrary", "arbitrary"))(du, du, p, p, dw32)


POOL_HALO = 16


def _pool_counts(s, ts, rows, w):
    t = s * ts + lax.broadcasted_iota(jnp.int32, (rows, 1), 0)
    return jnp.minimum(t + 1, w).astype(F32)


def _pool_fwd(x, gmix, w, b, scale):
    bsz, s_len, d = x.shape
    ng = len(POOL_WINDOWS)
    cg = d // ng
    ts = _tile(s_len, 512, POOL_HALO)
    r = ts // POOL_HALO

    def body(xc_ref, xp_ref, g_ref, w_ref, b_ref, sc_ref, y_ref, p_ref):
        s = pl.program_id(1)

        def norm(v):
            return v * lax.rsqrt(jnp.mean(v * v, axis=-1, keepdims=True) + EPS) * g_ref[...]

        xc = xc_ref[...]
        h = norm(xc)
        tail = jnp.where(s > 0, norm(xp_ref[...]), 0.0)
        for gi, win in enumerate(POOL_WINDOWS):
            lo, hi = gi * cg, (gi + 1) * cg
            hg = h[:, lo:hi]
            acc = jnp.concatenate([tail[:, lo:hi], hg], axis=0)
            step = 1
            while step < win:
                acc = acc + pltpu.roll(acc, step, axis=0)
                step *= 2
            pg = acc[POOL_HALO:] / _pool_counts(s, ts, ts, win) - hg
            pb = pg.astype(BF16)
            p_ref[:, lo:hi] = pb
            yg = jnp.dot(pb, w_ref[gi], preferred_element_type=F32) + b_ref[:, lo:hi]
            y_ref[:, lo:hi] = xc[:, lo:hi] + yg * sc_ref[:, lo:hi]

    vec = pl.BlockSpec((1, d), lambda bi, s: (0, 0))
    blk = pl.BlockSpec((None, ts, d), lambda bi, s: (bi, s, 0))
    return pl.pallas_call(
        body, out_shape=(SDS((bsz, s_len, d), F32), SDS((bsz, s_len, d), BF16)), grid=(bsz, s_len // ts),
        name="pool_fwd",
        in_specs=[blk, pl.BlockSpec((None, POOL_HALO, d), lambda bi, s: (bi, jnp.maximum(s * r - 1, 0), 0)),
                  vec, pl.BlockSpec((ng, cg, cg), lambda bi, s: (0, 0, 0)), vec, vec],
        out_specs=(blk, blk), compiler_params=_cp("parallel", "arbitrary"))(x, x, gmix, w, b, scale)


def _pool_bwd(x, dy, p, gmix, w, b, scale):
    bsz, s_len, d = x.shape
    ng = len(POOL_WINDOWS)
    cg = d // ng
    ts = _tile(s_len, 512, POOL_HALO)
    r, ns = ts // POOL_HALO, s_len // ts
    nt = (((1,), (1,)), ((), ()))
    tn = (((0,), (0,)), ((), ()))

    def body(x_ref, dy_ref, dyn_ref, p_ref, g_ref, w_ref, b_ref, sc_ref, dx_ref, dw_ref, db_ref, dsc_ref, dg_ref):
        bi, s = pl.program_id(0), pl.program_id(1)

        @pl.when((bi == 0) & (s == 0))
        def _():
            dw_ref[...] = jnp.zeros_like(dw_ref)
            db_ref[...] = jnp.zeros_like(db_ref)
            dsc_ref[...] = jnp.zeros_like(dsc_ref)
            dg_ref[...] = jnp.zeros_like(dg_ref)

        dy = dy_ref[...]
        dyy = dy * sc_ref[...]
        dyy_n = jnp.where(s < ns - 1, dyn_ref[...] * sc_ref[...], 0.0)
        db_ref[...] += _fold8(dyy)
        xv = x_ref[...]
        rr = lax.rsqrt(jnp.mean(xv * xv, axis=-1, keepdims=True) + EPS)
        xh = xv * rr
        for gi, win in enumerate(POOL_WINDOWS):
            lo, hi = gi * cg, (gi + 1) * cg
            pb = p_ref[:, lo:hi]
            wg = w_ref[gi]
            pre = jnp.dot(pb, wg, preferred_element_type=F32) + b_ref[:, lo:hi]
            dsc_ref[:, lo:hi] += _fold8(dy[:, lo:hi] * pre)
            dyb = dyy[:, lo:hi].astype(BF16)
            dw_ref[gi] += lax.dot_general(pb, dyb, tn, preferred_element_type=F32)
            dp = lax.dot_general(dyb, wg, nt, preferred_element_type=F32)
            dp_n = lax.dot_general(dyy_n[:, lo:hi].astype(BF16), wg, nt, preferred_element_type=F32)
            q = dp / _pool_counts(s, ts, ts, win)
            q_n = dp_n / _pool_counts(s + 1, ts, POOL_HALO, win)
            acc = jnp.concatenate([q, q_n], axis=0)
            ln = ts + POOL_HALO
            step = 1
            while step < win:
                acc = acc + pltpu.roll(acc, ln - step, axis=0)
                step *= 2
            dh = acc[:ts] - dp
            xhg = xh[:, lo:hi]
            dg_ref[:, lo:hi] += _fold8(dh * xhg)
            dx_ref[:, lo:hi] = dh * g_ref[:, lo:hi]
        u = dx_ref[...]
        dx_ref[...] = dy + rr * (u - xh * jnp.mean(u * xh, axis=-1, keepdims=True))

    vec = pl.BlockSpec((1, d), lambda bi, s: (0, 0))
    acc8 = pl.BlockSpec((8, d), lambda bi, s: (0, 0))
    blk = pl.BlockSpec((None, ts, d), lambda bi, s: (bi, s, 0))
    wspec = pl.BlockSpec((ng, cg, cg), lambda bi, s: (0, 0, 0))
    return pl.pallas_call(
        body, out_shape=(SDS((bsz, s_len, d), F32), SDS((ng, cg, cg), F32), SDS((8, d), F32), SDS((8, d), F32),
                         SDS((8, d), F32)),
        grid=(bsz, ns), name="pool_bwd",
        in_specs=[blk, blk,
                  pl.BlockSpec((None, POOL_HALO, d),
                               lambda bi, s: (bi, jnp.minimum((s + 1) * r, s_len // POOL_HALO - 1), 0)),
                  blk, vec, wspec, vec, vec],
        out_specs=(blk, wspec, acc8, acc8, acc8),
        compiler_params=_cp("arbitrary", "arbitrary"))(x, dy, dy, p, gmix, w, b, scale)


def _tri(n, upper):
    row = lax.broadcasted_iota(jnp.int32, (n, n), 0)
    col = lax.broadcasted_iota(jnp.int32, (n, n), 1)
    return jnp.where((col >= row) if upper else (col <= row), 1.0, 0.0).astype(F32)


def _fox_gate_fwd(proj, bf, n_heads):
    bsz, s_len, width = proj.shape
    col = width // LANES - 1
    ts = _tile(s_len, 512)

    def body(fl_ref, b_ref, c_ref, carry_ref):
        @pl.when(pl.program_id(1) == 0)
        def _():
            carry_ref[...] = jnp.zeros_like(carry_ref)

        xv = fl_ref[...] + b_ref[...]
        logf = jnp.minimum(xv, 0.0) - jnp.log(1.0 + jnp.exp(-jnp.abs(xv)))
        lane = lax.broadcasted_iota(jnp.int32, (1, LANES), 1)
        logf = jnp.where(lane < n_heads, logf, 0.0)
        c = _dot_hi(_tri(ts, False), logf) + carry_ref[0:1, :]
        c_ref[...] = c
        carry_ref[0:1, :] = c[ts - 1:ts, :]

    return pl.pallas_call(
        body, out_shape=SDS((bsz, s_len, LANES), F32), grid=(bsz, s_len // ts), name="fox_gate_fwd",
        in_specs=[pl.BlockSpec((None, ts, LANES), lambda bi, s: (bi, s, col)),
                  pl.BlockSpec((1, LANES), lambda bi, s: (0, 0))],
        out_specs=pl.BlockSpec((None, ts, LANES), lambda bi, s: (bi, s, 0)),
        scratch_shapes=[pltpu.VMEM((8, LANES), F32)],
        compiler_params=_cp("arbitrary", "arbitrary"))(proj, bf)


def _fox_gate_bwd(dc, proj, bf, n_heads):
    bsz, s_len, width = proj.shape
    col = width // LANES - 1
    ts = _tile(s_len, 512)
    ns = s_len // ts

    def body(dc_ref, fl_ref, b_ref, dfl_ref, db_ref, carry_ref):
        bi, s = pl.program_id(0), pl.program_id(1)

        @pl.when((bi == 0) & (s == 0))
        def _():
            db_ref[...] = jnp.zeros_like(db_ref)

        @pl.when(s == 0)
        def _():
            carry_ref[...] = jnp.zeros_like(carry_ref)

        dlogf = _dot_hi(_tri(ts, True), dc_ref[...]) + carry_ref[0:1, :]
        carry_ref[0:1, :] = dlogf[0:1, :]
        lane = lax.broadcasted_iota(jnp.int32, (1, LANES), 1)
        dfl = jnp.where(lane < n_heads, dlogf * (1.0 - _sigmoid(fl_ref[...] + b_ref[...])), 0.0)
        dfl_ref[...] = dfl.astype(BF16)
        db_ref[...] += _fold8(dfl)

    return pl.pallas_call(
        body, out_shape=(SDS((bsz, s_len, LANES), BF16), SDS((8, LANES), F32)), grid=(bsz, ns), name="fox_gate_bwd",
        in_specs=[pl.BlockSpec((None, ts, LANES), lambda bi, s: (bi, ns - 1 - s, 0)),
                  pl.BlockSpec((None, ts, LANES), lambda bi, s: (bi, ns - 1 - s, col)),
                  pl.BlockSpec((1, LANES), lambda bi, s: (0, 0))],
        out_specs=(pl.BlockSpec((None, ts, LANES), lambda bi, s: (bi, ns - 1 - s, 0)),
                   pl.BlockSpec((8, LANES), lambda bi, s: (0, 0))),
        scratch_shapes=[pltpu.VMEM((8, LANES), F32)],
        compiler_params=_cp("arbitrary", "arbitrary"))(dc, proj, bf)


def _head_maps(d):
    ch = lax.broadcasted_iota(jnp.int32, (d, LANES), 0) // HEAD_DIM
    hd = lax.broadcasted_iota(jnp.int32, (d, LANES), 1)
    e = jnp.where(ch == hd, 1.0, 0.0).astype(F32)
    cht = lax.broadcasted_iota(jnp.int32, (LANES, d), 1) // HEAD_DIM
    hdt = lax.broadcasted_iota(jnp.int32, (LANES, d), 0)
    et = jnp.where(cht == hdt, 1.0, 0.0).astype(F32)
    return e, et


def _fox_qknorm_fwd(proj, gq, gk, d):
    t = proj.shape[0]
    ts = _tile(t, 256)
    scale = 1.0 / math.sqrt(HEAD_DIM)

    def body(q_ref, k_ref, v_ref, gq_ref, gk_ref, qn_ref, kn_ref, vb_ref):
        e, et = _head_maps(d)

        def norm(v, g):
            r = lax.rsqrt(_dot_hi(v * v, e) / HEAD_DIM + EPS)
            return v * _dot_hi(r, et) * g

        qn_ref[...] = (norm(q_ref[...], gq_ref[...]) * scale).astype(BF16)
        kn_ref[...] = norm(k_ref[...], gk_ref[...]).astype(BF16)
        vb_ref[...] = v_ref[...].astype(BF16)

    def colblk(j):
        return pl.BlockSpec((ts, d), lambda i: (i, j))

    vec = pl.BlockSpec((1, d), lambda i: (0, 0))
    out = pl.BlockSpec((ts, d), lambda i: (i, 0))
    return pl.pallas_call(
        body, out_shape=(SDS((t, d), BF16),) * 3, grid=(t // ts,), name="fox_qknorm_fwd",
        in_specs=[colblk(0), colblk(1), colblk(2), vec, vec], out_specs=(out, out, out),
        compiler_params=_cp("parallel"))(proj, proj, proj, gq, gk)


def _fox_qknorm_bwd(proj, dq, dk, dv, gq, gk, d):
    t = proj.shape[0]
    ts = _tile(t, 256)
    scale = 1.0 / math.sqrt(HEAD_DIM)

    def body(q_ref, k_ref, dq_ref, dk_ref, dv_ref, gq_ref, gk_ref, dp_ref, dgq_ref, dgk_ref):
        @pl.when(pl.program_id(0) == 0)
        def _():
            dgq_ref[...] = jnp.zeros_like(dgq_ref)
            dgk_ref[...] = jnp.zeros_like(dgk_ref)

        e, et = _head_maps(d)

        def back(v, g, dn, dg_ref):
            r = _dot_hi(lax.rsqrt(_dot_hi(v * v, e) / HEAD_DIM + EPS), et)
            vh = v * r
            dg_ref[...] += _fold8(dn * vh)
            u = dn * g
            mh = _dot_hi(_dot_hi(u * vh, e) / HEAD_DIM, et)
            return r * (u - vh * mh)

        dp_ref[:, 0:d] = back(q_ref[...], gq_ref[...], dq_ref[...] * scale, dgq_ref).astype(BF16)
        dp_ref[:, d:2 * d] = back(k_ref[...], gk_ref[...], dk_ref[...], dgk_ref).astype(BF16)
        dp_ref[:, 2 * d:3 * d] = dv_ref[...]

    def colblk(j):
        return pl.BlockSpec((ts, d), lambda i: (i, j))

    row = pl.BlockSpec((ts, d), lambda i: (i, 0))
    vec = pl.BlockSpec((1, d), lambda i: (0, 0))
    acc = pl.BlockSpec((8, d), lambda i: (0, 0))
    return pl.pallas_call(
        body, out_shape=(SDS((t, 3 * d), BF16), SDS((8, d), F32), SDS((8, d), F32)), grid=(t // ts,),
        name="fox_qknorm_bwd", in_specs=[colblk(0), colblk(1), row, row, row, vec, vec],
        out_specs=(pl.BlockSpec((ts, 3 * d), lambda i: (i, 0)), acc, acc),
        compiler_params=_cp("arbitrary"))(proj, proj, dq, dk, dv, gq, gk)


ATT_BLOCK = 512
_NT = (((1,), (1,)), ((), ()))
_TN = (((0,), (0,)), ((), ()))


def _head_mask(h):
    return (lax.broadcasted_iota(jnp.int32, (1, LANES), 1) // HEAD_DIM) == h


def _causal(qi, ki, tq, tk):
    row = qi * tq + lax.broadcasted_iota(jnp.int32, (tq, 1), 0)
    col = ki * tk + lax.broadcasted_iota(jnp.int32, (1, tk), 1)
    return col <= row


def _flash_fwd(q, k, v, crow):
    bsz, s_len, d = q.shape
    nj = d // LANES
    tq = tk = _tile(s_len, ATT_BLOCK, LANES)
    nq = s_len // tq

    def body(q_ref, k_ref, v_ref, c_ref, o_ref, lse_ref, m_ref, l_ref, acc_ref):
        qi, ki = pl.program_id(2), pl.program_id(3)

        @pl.when(ki == 0)
        def _():
            m_ref[...] = jnp.full_like(m_ref, NEG)
            l_ref[...] = jnp.zeros_like(l_ref)
            acc_ref[...] = jnp.zeros_like(acc_ref)

        @pl.when(ki <= qi)
        def _():
            qv, kv, vv = q_ref[...], k_ref[...], v_ref[...]
            mask = _causal(qi, ki, tq, tk)
            for h in range(2):
                qh = jnp.where(_head_mask(h), qv, jnp.zeros_like(qv))
                s = lax.dot_general(qh, kv, _NT, preferred_element_type=F32) - c_ref[h:h + 1, :]
                s = jnp.where(mask, s, NEG)
                m_prev = m_ref[h]
                m_new = jnp.maximum(m_prev, jnp.max(s, axis=1, keepdims=True))
                pm = jnp.exp(s - m_new)
                alpha = jnp.exp(m_prev - m_new)
                l_ref[h] = alpha * l_ref[h] + jnp.sum(pm, axis=1, keepdims=True)
                p_hi = pm.astype(BF16)
                p_lo = (pm - p_hi.astype(F32)).astype(BF16)
                acc_ref[h] = (alpha * acc_ref[h] + jnp.dot(p_hi, vv, preferred_element_type=F32)
                              + jnp.dot(p_lo, vv, preferred_element_type=F32))
                m_ref[h] = m_new

        @pl.when(ki == qi)
        def _():
            m0 = _head_mask(0)
            o_ref[...] = jnp.where(m0, acc_ref[0] / l_ref[0], acc_ref[1] / l_ref[1])
            lse_ref[...] = jnp.where(m0, m_ref[0] + jnp.log(l_ref[0]), m_ref[1] + jnp.log(l_ref[1]))

    return pl.pallas_call(
        body, out_shape=(SDS((bsz, s_len, d), F32), SDS((bsz, nj, s_len, LANES), F32)), grid=(bsz, nj, nq, nq),
        name="flash_fwd",
        in_specs=[pl.BlockSpec((None, tq, LANES), lambda bi, j, qi, ki: (bi, qi, j)),
                  pl.BlockSpec((None, tk, LANES), lambda bi, j, qi, ki: (bi, jnp.minimum(ki, qi), j)),
                  pl.BlockSpec((None, tk, LANES), lambda bi, j, qi, ki: (bi, jnp.minimum(ki, qi), j)),
                  pl.BlockSpec((None, None, 2, tk), lambda bi, j, qi, ki: (bi, j, 0, jnp.minimum(ki, qi)))],
        out_specs=(pl.BlockSpec((None, tq, LANES), lambda bi, j, qi, ki: (bi, qi, j)),
                   pl.BlockSpec((None, None, tq, LANES), lambda bi, j, qi, ki: (bi, j, qi, 0))),
        scratch_shapes=[pltpu.VMEM((2, tq, 1), F32), pltpu.VMEM((2, tq, 1), F32), pltpu.VMEM((2, tq, LANES), F32)],
        compiler_params=_cp("parallel", "parallel", "arbitrary", "arbitrary"))(q, k, v, crow)


def _flash_probs(qv, kv, vv, dov, ov, lse, c_ref, h, mask):
    hm = _head_mask(h)
    qh = jnp.where(hm, qv, jnp.zeros_like(qv))
    s = lax.dot_general(qh, kv, _NT, preferred_element_type=F32) - c_ref[h:h + 1, :]
    pm = jnp.where(mask, jnp.exp(s - lse[:, h * HEAD_DIM:h * HEAD_DIM + 1]), 0.0)
    doh = jnp.where(hm, dov, jnp.zeros_like(dov))
    dpm = lax.dot_general(doh, vv, _NT, preferred_element_type=F32)
    delta = jnp.sum(jnp.where(hm, dov.astype(F32) * ov, 0.0), axis=1, keepdims=True)
    return pm, pm * (dpm - delta)


def _flash_bwd_dkv(q, k, v, do, o, lse, crow):
    bsz, s_len, d = q.shape
    nj = d // LANES
    tq = tk = _tile(s_len, ATT_BLOCK, LANES)
    nq = s_len // tq

    def body(q_ref, k_ref, v_ref, do_ref, o_ref, lse_ref, c_ref, dk_ref, dv_ref, dc_ref, dka_ref, dva_ref, dca_ref):
        ki, qi = pl.program_id(2), pl.program_id(3)

        @pl.when(qi == ki)
        def _():
            dka_ref[...] = jnp.zeros_like(dka_ref)
            dva_ref[...] = jnp.zeros_like(dva_ref)
            dca_ref[...] = jnp.zeros_like(dca_ref)

        @pl.when(qi >= ki)
        def _():
            qv, kv, vv, dov, ov, lse = q_ref[...], k_ref[...], v_ref[...], do_ref[...], o_ref[...], lse_ref[...]
            mask = _causal(qi, ki, tq, tk)
            for h in range(2):
                pm, ds = _flash_probs(qv, kv, vv, dov, ov, lse, c_ref, h, mask)
                dva_ref[h] += lax.dot_general(pm.astype(BF16), dov, _TN, preferred_element_type=F32)
                dka_ref[h] += lax.dot_general(ds.astype(BF16), qv, _TN, preferred_element_type=F32)
                dca_ref[h:h + 1, :] -= jnp.sum(ds, axis=0, keepdims=True)

        @pl.when(qi == nq - 1)
        def _():
            m0 = _head_mask(0)
            dk_ref[...] = jnp.where(m0, dka_ref[0], dka_ref[1])
            dv_ref[...] = jnp.where(m0, dva_ref[0], dva_ref[1]).astype(BF16)
            dc_ref[...] = dca_ref[0:2, :]

    def qside(bi, j, ki, qi):
        return (bi, jnp.maximum(qi, ki), j)

    def kside(bi, j, ki, qi):
        return (bi, ki, j)

    return pl.pallas_call(
        body, out_shape=(SDS((bsz, s_len, d), F32), SDS((bsz, s_len, d), BF16), SDS((bsz, nj, 2, s_len), F32)),
        grid=(bsz, nj, nq, nq), name="flash_bwd_dkv",
        in_specs=[pl.BlockSpec((None, tq, LANES), qside), pl.BlockSpec((None, tk, LANES), kside),
                  pl.BlockSpec((None, tk, LANES), kside), pl.BlockSpec((None, tq, LANES), qside),
                  pl.BlockSpec((None, tq, LANES), qside),
                  pl.BlockSpec((None, None, tq, LANES), lambda bi, j, ki, qi: (bi, j, jnp.maximum(qi, ki), 0)),
                  pl.BlockSpec((None, None, 2, tk), lambda bi, j, ki, qi: (bi, j, 0, ki))],
        out_specs=(pl.BlockSpec((None, tk, LANES), kside), pl.BlockSpec((None, tk, LANES), kside),
                   pl.BlockSpec((None, None, 2, tk), lambda bi, j, ki, qi: (bi, j, 0, ki))),
        scratch_shapes=[pltpu.VMEM((2, tk, LANES), F32), pltpu.VMEM((2, tk, LANES), F32), pltpu.VMEM((8, tk), F32)],
        compiler_params=_cp("parallel", "parallel", "arbitrary", "arbitrary"))(q, k, v, do, o, lse, crow)


def _flash_bwd_dq(q, k, v, do, o, lse, crow):
    bsz, s_len, d = q.shape
    nj = d // LANES
    tq = tk = _tile(s_len, ATT_BLOCK, LANES)
    nq = s_len // tq

    def body(q_ref, k_ref, v_ref, do_ref, o_ref, lse_ref, c_ref, dq_ref, dqa_ref):
        qi, ki = pl.program_id(2), pl.program_id(3)

        @pl.when(ki == 0)
        def _():
            dqa_ref[...] = jnp.zeros_like(dqa_ref)

        @pl.when(ki <= qi)
        def _():
            qv, kv, vv, dov, ov, lse = q_ref[...], k_ref[...], v_ref[...], do_ref[...], o_ref[...], lse_ref[...]
            mask = _causal(qi, ki, tq, tk)
            for h in range(2):
                _, ds = _flash_probs(qv, kv, vv, dov, ov, lse, c_ref, h, mask)
                dqa_ref[h] += jnp.dot(ds.astype(BF16), kv, preferred_element_type=F32)

        @pl.when(ki == qi)
        def _():
            dq_ref[...] = jnp.where(_head_mask(0), dqa_ref[0], dqa_ref[1])

    def qside(bi, j, qi, ki):
        return (bi, qi, j)

    def kside(bi, j, qi, ki):
        return (bi, jnp.minimum(ki, qi), j)

    return pl.pallas_call(
        body, out_shape=SDS((bsz, s_len, d), F32), grid=(bsz, nj, nq, nq), name="flash_bwd_dq",
        in_specs=[pl.BlockSpec((None, tq, LANES), qside), pl.BlockSpec((None, tk, LANES), kside),
                  pl.BlockSpec((None, tk, LANES), kside), pl.BlockSpec((None, tq, LANES), qside),
                  pl.BlockSpec((None, tq, LANES), qside),
                  pl.BlockSpec((None, None, tq, LANES), lambda bi, j, qi, ki: (bi, j, qi, 0)),
                  pl.BlockSpec((None, None, 2, tk), lambda bi, j, qi, ki: (bi, j, 0, jnp.minimum(ki, qi)))],
        out_specs=pl.BlockSpec((None, tq, LANES), qside),
        scratch_shapes=[pltpu.VMEM((2, tq, LANES), F32)],
        compiler_params=_cp("parallel", "parallel", "arbitrary", "arbitrary"))(q, k, v, do, o, lse, crow)


def _loss_head(y, target):
    t, d = y.shape
    tm = _tile(t, 512)

    def body(y_ref, t_ref, dy_ref, acc_ref):
        @pl.when(pl.program_id(0) == 0)
        def _():
            acc_ref[...] = jnp.zeros_like(acc_ref)

        err = y_ref[...] - t_ref[...]
        dy_ref[...] = err / d
        acc_ref[...] += _fold8(err * err)

    row = pl.BlockSpec((tm, d), lambda i: (i, 0))
    return pl.pallas_call(
        body, out_shape=(SDS((t, d), F32), SDS((8, d), F32)), grid=(t // tm,), name="loss_head",
        in_specs=[row, row], out_specs=(row, pl.BlockSpec((8, d), lambda i: (0, 0))),
        compiler_params=_cp("arbitrary"))(y, target)


ADAM_COLS = 1024


def _adamw(g8, w, m, v):
    _, rows, cols = g8.shape
    tr = _tile(rows, 256)
    c1 = 1.0 - ADAM_B1 ** ADAM_STEP
    c2 = 1.0 - ADAM_B2 ** ADAM_STEP

    def body(g8_ref, w_ref, m_ref, v_ref, g_ref, d_ref, nm_ref, nv_ref):
        g = g8_ref[0]
        for i in range(1, N_DEV):
            g = g + g8_ref[i]
        mn = ADAM_B1 * m_ref[...] + (1.0 - ADAM_B1) * g
        vn = ADAM_B2 * v_ref[...] + (1.0 - ADAM_B2) * (g * g)
        g_ref[...] = g
        nm_ref[...] = mn
        nv_ref[...] = vn
        d_ref[...] = -ADAM_LR * ((mn / c1) / (jnp.sqrt(vn / c2) + ADAM_EPS) + ADAM_WD * w_ref[...])

    blk = pl.BlockSpec((tr, cols), lambda i: (i, 0))
    return pl.pallas_call(
        body, out_shape=(SDS((rows, cols), F32),) * 4, grid=(rows // tr,), name="adamw",
        in_specs=[pl.BlockSpec((N_DEV, tr, cols), lambda i: (0, i, 0)), blk, blk, blk], out_specs=(blk,) * 4,
        compiler_params=_cp("parallel"))(g8, w, m, v)


def _exchange(src, *, all_to_all, name):
    shape = src.shape[1:] if all_to_all else src.shape

    def body(src_ref, dst_ref, send_sems, recv_sems, local_sem):
        x, y, c = lax.axis_index("x"), lax.axis_index("y"), lax.axis_index("c")
        me = 4 * x + 2 * y + c

        def row(p):
            return src_ref.at[p] if all_to_all else src_ref

        own = pltpu.make_async_copy(row(me), dst_ref.at[me], local_sem)
        own.start()
        copies = []
        for kbits in range(1, N_DEV):
            px = 1 - x if kbits & 4 else x
            py = 1 - y if kbits & 2 else y
            pc = 1 - c if kbits & 1 else c
            cp = pltpu.make_async_remote_copy(
                src_ref=row(4 * px + 2 * py + pc), dst_ref=dst_ref.at[me], send_sem=send_sems.at[kbits - 1],
                recv_sem=recv_sems.at[kbits - 1], device_id=(px, py, pc), device_id_type=pl.DeviceIdType.MESH)
            cp.start()
            copies.append(cp)
        for cp in copies:
            cp.wait()
        own.wait()

    return pl.pallas_call(
        body, out_shape=SDS((N_DEV,) + tuple(shape), src.dtype), name=name,
        in_specs=[pl.BlockSpec(memory_space=pl.ANY)], out_specs=pl.BlockSpec(memory_space=pl.ANY),
        scratch_shapes=[pltpu.SemaphoreType.DMA((N_DEV - 1,)), pltpu.SemaphoreType.DMA((N_DEV - 1,)),
                        pltpu.SemaphoreType.DMA])(src)


def _flat_rows(parts, dtype, row_align):
    flat = jnp.concatenate([p.reshape(-1).astype(dtype) for p in parts])
    chunk = row_align * ADAM_COLS
    n = -(-flat.shape[0] // chunk) * chunk
    return jnp.pad(flat, (0, n - flat.shape[0])).reshape(n // ADAM_COLS, ADAM_COLS)


def _unshard(g8, shard_shape, axis):
    full = jnp.moveaxis(g8.reshape((N_DEV,) + tuple(shard_shape)), 0, axis)
    shape = list(shard_shape)
    shape[axis] *= N_DEV
    return full.reshape(shape)


def _to_shards(full, axis):
    shape = list(full.shape)
    shape[axis:axis + 1] = [N_DEV, shape[axis] // N_DEV]
    return jnp.moveaxis(full.reshape(shape), axis, 0).reshape(N_DEV, -1)


def _gather_weights(shards):
    out = {}
    for names, dtype, align in ((MATRICES, BF16, 16), ([n for n in SHARDED if n not in MATRICES], F32, 8)):
        packed = _flat_rows([shards[n] for n in names], dtype, align)
        got = _exchange(packed, all_to_all=False, name="gather_" + jnp.dtype(dtype).name)
        got = got.reshape(N_DEV, -1)
        off = 0
        for n in names:
            size = shards[n].size
            out[n] = _unshard(got[:, off:off + size], shards[n].shape, SHARD_AXIS[n])
            off += size
    return out


def _pad_rows(w, rows):
    return jnp.pad(w, ((0, rows - w.shape[0]), (0, 0)))


def _fold(acc):
    return acc.sum(axis=0)


def _local_step(x, target, wt):
    bsz, s_len, d = x.shape
    t = bsz * s_len
    depth = wt['norm_mix'].shape[0]
    n_heads = d // HEAD_DIM
    f2 = wt['ffn_w_up'].shape[2]
    row = lambda a: a.reshape(1, -1)
    grads = {n: [None] * wt[n].shape[0] for n in WEIGHTS}
    saved = []

    xc = x.reshape(t, d)
    for i in range(depth):
        j = i // 3
        kind = i % 3
        sv = {'x_mix': xc}
        gm = row(wt['norm_mix'][i])
        if kind == 0:
            hn = _rmsnorm_fwd(xc, gm)
            p = _mm(hn, wt['conv_w_in'][j], bias=row(wt['conv_b_in'][j]), name="conv_in")
            u, sact = _conv_act_fwd(p.reshape(bsz, s_len, 2 * d), _pad_rows(wt['conv_dw'][j], 32),
                                    row(wt['conv_dw_b'][j]), row(wt['conv_ln_g'][j]), row(wt['conv_ln_b'][j]))
            sact = sact.reshape(t, d)
            xn = _mm(sact, wt['conv_w_out'][j], bias=row(wt['conv_b_out'][j]), residual=xc, out_dtype=F32,
                     name="conv_out")
            sv.update(hn=hn, p=p, u=u.reshape(t, d), sact=sact)
        elif kind == 1:
            xn, pp = _pool_fwd(xc.reshape(bsz, s_len, d), gm, wt['pool_w'][j], row(wt['pool_b'][j]),
                               row(wt['pool_scale'][j]))
            xn = xn.reshape(t, d)
            sv.update(p=pp)
        else:
            hn = _rmsnorm_fwd(xc, gm)
            w_in = wt['fox_w_in'][j]
            wp = jnp.pad(w_in, ((0, 0), (0, 3 * d + LANES - w_in.shape[1])))
            bf = jnp.pad(wt['fox_b_f'][j], (0, LANES - n_heads)).reshape(1, LANES)
            gq = jnp.tile(wt['fox_q_gain'][j], n_heads).reshape(1, d)
            gk = jnp.tile(wt['fox_k_gain'][j], n_heads).reshape(1, d)
            proj = _mm(hn, wp, out_dtype=F32, name="fox_in")
            c = _fox_gate_fwd(proj.reshape(bsz, s_len, -1), bf, n_heads)
            crow = jnp.swapaxes(c, 1, 2)[:, :n_heads].reshape(bsz, n_heads // 2, 2, s_len)
            qn, kn, vb = _fox_qknorm_fwd(proj, gq, gk, d)
            shp = (bsz, s_len, d)
            o, lse = _flash_fwd(qn.reshape(shp), kn.reshape(shp), vb.reshape(shp), crow)
            o = o.reshape(t, d)
            xn = _mm(o, wt['fox_w_o'][j], residual=xc, out_dtype=F32, name="fox_out")
            sv.update(hn=hn, wp=wp, bf=bf, gq=gq, gk=gk, proj=proj, crow=crow, qn=qn, kn=kn, vb=vb, o=o, lse=lse)
        xc = xn
        sv['x_ffn'] = xc
        hf = _rmsnorm_fwd(xc, row(wt['norm_ffn'][i]))
        uf = _mm(hf, wt['ffn_w_up'][i], name="ffn_up")
        dw8 = _pad_rows(wt['ffn_dw'][i], 8)
        af = _ffn_act_fwd(uf.reshape(bsz, s_len, f2), dw8, row(wt['ffn_dw_b'][i])).reshape(t, f2 // 2)
        xc = _mm(af, wt['ffn_w_down'][i], residual=xc, out_dtype=F32, name="ffn_down")
        sv.update(hf=hf, uf=uf, af=af, dw8=dw8)
        saved.append(sv)

    dx, sq = _loss_head(xc, target.reshape(t, d))

    for i in reversed(range(depth)):
        j = i // 3
        kind = i % 3
        sv = saved[i]
        da = _mm(dx, wt['ffn_w_down'][i], trans_b=True, name="ffn_down_dgrad")
        grads['ffn_w_down'][i], _ = _wgrad(sv['af'], dx, name="ffn_down_wgrad")
        dvf, ddw, db = _ffn_act_bwd1(sv['uf'].reshape(bsz, s_len, f2), da.reshape(bsz, s_len, f2 // 2), sv['dw8'],
                                     row(wt['ffn_dw_b'][i]))
        grads['ffn_dw'][i] = ddw.sum(axis=1)
        grads['ffn_dw_b'][i] = _fold(db)
        duf = _ffn_act_bwd2(dvf, sv['dw8']).reshape(t, f2)
        grads['ffn_w_up'][i], _ = _wgrad(sv['hf'], duf, name="ffn_up_wgrad")
        dhf = _mm(duf, wt['ffn_w_up'][i], trans_b=True, name="ffn_up_dgrad")
        dx, dg = _rmsnorm_bwd(sv['x_ffn'], row(wt['norm_ffn'][i]), dhf, dx)
        grads['norm_ffn'][i] = _fold(dg)
        gm = row(wt['norm_mix'][i])
        if kind == 0:
            dsact = _mm(dx, wt['conv_w_out'][j], trans_b=True, name="conv_out_dgrad")
            grads['conv_w_out'][j], cs = _wgrad(sv['sact'], dx, name="conv_out_wgrad")
            grads['conv_b_out'][j] = _fold(cs)
            du, dlg, dlb, dwb = _conv_act_bwd1(sv['u'], dsact, row(wt['conv_ln_g'][j]), row(wt['conv_ln_b'][j]))
            grads['conv_ln_g'][j], grads['conv_ln_b'][j], grads['conv_dw_b'][j] = _fold(dlg), _fold(dlb), _fold(dwb)
            dp, ddw = _conv_act_bwd2(du.reshape(bsz, s_len, d), sv['p'].reshape(bsz, s_len, 2 * d),
                                     _pad_rows(wt['conv_dw'][j], 32))
            grads['conv_dw'][j] = ddw.sum(axis=1)[:wt['conv_dw'].shape[1]]
            dp = dp.reshape(t, 2 * d)
            grads['conv_w_in'][j], cs = _wgrad(sv['hn'], dp, name="conv_in_wgrad")
            grads['conv_b_in'][j] = _fold(cs)
            dhn = _mm(dp, wt['conv_w_in'][j], trans_b=True, name="conv_in_dgrad")
            dx, dg = _rmsnorm_bwd(sv['x_mix'], gm, dhn, dx)
            grads['norm_mix'][i] = _fold(dg)
        elif kind == 1:
            shp = (bsz, s_len, d)
            dxn, dwp, dbp, dsc, dg = _pool_bwd(sv['x_mix'].reshape(shp), dx.reshape(shp), sv['p'], gm, wt['pool_w'][j],
                                               row(wt['pool_b'][j]), row(wt['pool_scale'][j]))
            dx = dxn.reshape(t, d)
            grads['pool_w'][j] = dwp
            grads['pool_b'][j] = _fold(dbp).reshape(wt['pool_b'].shape[1:])
            grads['pool_scale'][j] = _fold(dsc)
            grads['norm_mix'][i] = _fold(dg)
        else:
            shp = (bsz, s_len, d)
            do = _mm(dx, wt['fox_w_o'][j], trans_b=True, name="fox_out_dgrad")
            grads['fox_w_o'][j], _ = _wgrad(sv['o'], dx, name="fox_out_wgrad")
            fl_args = (sv['qn'].reshape(shp), sv['kn'].reshape(shp), sv['vb'].reshape(shp), do.reshape(shp),
                       sv['o'].reshape(shp), sv['lse'], sv['crow'])
            dk, dv, dcrow = _flash_bwd_dkv(*fl_args)
            dq = _flash_bwd_dq(*fl_args)
            dc = jnp.swapaxes(dcrow.reshape(bsz, n_heads, s_len), 1, 2)
            dc = jnp.pad(dc, ((0, 0), (0, 0), (0, LANES - n_heads)))
            dfl, dbf = _fox_gate_bwd(dc, sv['proj'].reshape(bsz, s_len, -1), sv['bf'], n_heads)
            grads['fox_b_f'][j] = _fold(dbf)[:n_heads]
            dqkv, dgq, dgk = _fox_qknorm_bwd(sv['proj'], dq.reshape(t, d), dk.reshape(t, d), dv.reshape(t, d),
                                             sv['gq'], sv['gk'], d)
            grads['fox_q_gain'][j] = _fold(dgq).reshape(n_heads, HEAD_DIM).sum(axis=0)
            grads['fox_k_gain'][j] = _fold(dgk).reshape(n_heads, HEAD_DIM).sum(axis=0)
            dproj = jnp.concatenate([dqkv, dfl.reshape(t, LANES)], axis=1)
            dwp, _ = _wgrad(sv['hn'], dproj, name="fox_in_wgrad")
            grads['fox_w_in'][j] = dwp[:, :wt['fox_w_in'].shape[2]]
            dhn = _mm(dproj, sv['wp'], trans_b=True, name="fox_in_dgrad")
            dx, dg = _rmsnorm_bwd(sv['x_mix'], gm, dhn, dx)
            grads['norm_mix'][i] = _fold(dg)

    return sq.sum(), dx.reshape(bsz, s_len, d), {n: jnp.stack(g) for n, g in grads.items()}


def _train_step(x, target, w, m, v):
    wt = {n: w[n] for n in REPLICATED}
    wt.update(_gather_weights({n: w[n] for n in SHARDED}))
    sq, grad_x, grads = _local_step(x, target, wt)
    d = x.shape[-1]

    shard_rows = jnp.concatenate([_to_shards(grads[n], SHARD_AXIS[n]) for n in SHARDED], axis=1)
    rep = jnp.concatenate([grads[n].reshape(-1) for n in REPLICATED] + [(0.5 / d) * sq.reshape(1)])
    rows = jnp.concatenate([shard_rows, jnp.broadcast_to(rep, (N_DEV, rep.shape[0]))], axis=1)
    chunk = 8 * ADAM_COLS
    n_all = rows.shape[1]
    n_pad = -(-n_all // chunk) * chunk
    rows = jnp.pad(rows, ((0, 0), (0, n_pad - n_all))).reshape(N_DEV, n_pad // ADAM_COLS, ADAM_COLS)
    got = _exchange(rows, all_to_all=True, name="grad_exchange")

    order = SHARDED + REPLICATED

    def flat(tree):
        parts = jnp.concatenate([tree[n].reshape(-1) for n in order])
        return jnp.pad(parts, (0, n_pad - parts.shape[0])).reshape(n_pad // ADAM_COLS, ADAM_COLS)

    outs = [o.reshape(-1) for o in _adamw(got, flat(w), flat(m), flat(v))]
    res = [{}, {}, {}, {}]
    off = 0
    for n in order:
        size = w[n].size
        for k in range(4):
            res[k][n] = outs[k][off:off + size].reshape(w[n].shape)
        off += size
    loss = outs[0][n_all - 1]
    return (loss, grad_x, *[res[0][n] for n in WEIGHTS], *[res[1][n] for n in WEIGHTS],
            *[res[2][n] for n in WEIGHTS], *[res[3][n] for n in WEIGHTS])


def kernel(x, norm_mix, norm_ffn, conv_w_in, conv_b_in, conv_dw, conv_dw_b, conv_ln_g, conv_ln_b, conv_w_out, conv_b_out, pool_w, pool_b, pool_scale, fox_w_in, fox_b_f, fox_q_gain, fox_k_gain, fox_w_o, ffn_w_up, ffn_dw, ffn_dw_b, ffn_w_down, loss_target, m_norm_mix, m_norm_ffn, m_conv_w_in, m_conv_b_in, m_conv_dw, m_conv_dw_b, m_conv_ln_g, m_conv_ln_b, m_conv_w_out, m_conv_b_out, m_pool_w, m_pool_b, m_pool_scale, m_fox_w_in, m_fox_b_f, m_fox_q_gain, m_fox_k_gain, m_fox_w_o, m_ffn_w_up, m_ffn_dw, m_ffn_dw_b, m_ffn_w_down, v_norm_mix, v_norm_ffn, v_conv_w_in, v_conv_b_in, v_conv_dw, v_conv_dw_b, v_conv_ln_g, v_conv_ln_b, v_conv_w_out, v_conv_b_out, v_pool_w, v_pool_b, v_pool_scale, v_fox_w_in, v_fox_b_f, v_fox_q_gain, v_fox_k_gain, v_fox_w_o, v_ffn_w_up, v_ffn_dw, v_ffn_dw_b, v_ffn_w_down):
    w = dict(zip(WEIGHTS, (norm_mix, norm_ffn, conv_w_in, conv_b_in, conv_dw, conv_dw_b, conv_ln_g, conv_ln_b, conv_w_out, conv_b_out, pool_w, pool_b, pool_scale, fox_w_in, fox_b_f, fox_q_gain, fox_k_gain, fox_w_o, ffn_w_up, ffn_dw, ffn_dw_b, ffn_w_down)))
    m = dict(zip(WEIGHTS, (m_norm_mix, m_norm_ffn, m_conv_w_in, m_conv_b_in, m_conv_dw, m_conv_dw_b, m_conv_ln_g, m_conv_ln_b, m_conv_w_out, m_conv_b_out, m_pool_w, m_pool_b, m_pool_scale, m_fox_w_in, m_fox_b_f, m_fox_q_gain, m_fox_k_gain, m_fox_w_o, m_ffn_w_up, m_ffn_dw, m_ffn_dw_b, m_ffn_w_down)))
    v = dict(zip(WEIGHTS, (v_norm_mix, v_norm_ffn, v_conv_w_in, v_conv_b_in, v_conv_dw, v_conv_dw_b, v_conv_ln_g, v_conv_ln_b, v_conv_w_out, v_conv_b_out, v_pool_w, v_pool_b, v_pool_scale, v_fox_w_in, v_fox_b_f, v_fox_q_gain, v_fox_k_gain, v_fox_w_o, v_ffn_w_up, v_ffn_dw, v_ffn_dw_b, v_ffn_w_down)))
    return _train_step(x, loss_target, w, m, v)
```

```python
import functools
import math

import jax
import jax.numpy as jnp
from jax import lax
from jax.experimental import pallas as pl
from jax.experimental.pallas import tpu as pltpu

F32, BF16 = jnp.float32, jnp.bfloat16
SDS = jax.ShapeDtypeStruct

N_DEV = 8
EPS = 1e-6
POOL_WINDOWS = (2, 4, 8, 16)
HEAD_DIM = 64
ADAM_LR, ADAM_B1, ADAM_B2, ADAM_EPS, ADAM_WD, ADAM_STEP = 0.001, 0.9, 0.999, 1e-08, 0.01, 10
LANES = 128
VMEM_LIMIT_BYTES = 48 * 1024 * 1024
NEG = -1e30

WEIGHTS = ['norm_mix', 'norm_ffn', 'conv_w_in', 'conv_b_in', 'conv_dw', 'conv_dw_b', 'conv_ln_g', 'conv_ln_b',
           'conv_w_out', 'conv_b_out', 'pool_w', 'pool_b', 'pool_scale', 'fox_w_in', 'fox_b_f', 'fox_q_gain',
           'fox_k_gain', 'fox_w_o', 'ffn_w_up', 'ffn_dw', 'ffn_dw_b', 'ffn_w_down']
SHARD_AXIS = {'conv_w_in': 2, 'conv_b_in': 1, 'conv_dw': 2, 'conv_dw_b': 1, 'conv_ln_g': 1, 'conv_ln_b': 1,
              'conv_w_out': 1, 'conv_b_out': 1, 'pool_w': 2, 'pool_b': 2, 'fox_w_in': 2, 'fox_w_o': 1,
              'ffn_w_up': 2, 'ffn_dw': 2, 'ffn_w_down': 1}
MATRICES = ('conv_w_in', 'conv_w_out', 'pool_w', 'fox_w_in', 'fox_w_o', 'ffn_w_up', 'ffn_w_down')
SHARDED = [n for n in WEIGHTS if n in SHARD_AXIS]
REPLICATED = [n for n in WEIGHTS if n not in SHARD_AXIS]


def _cp(*sem):
    return pltpu.CompilerParams(dimension_semantics=sem, vmem_limit_bytes=VMEM_LIMIT_BYTES)


def _tile(n, pref, align=8):
    if n <= pref:
        return n
    t = (pref // align) * align
    while t >= align:
        if n % t == 0:
            return t
        t -= align
    return n


def _fold8(x):
    r, c = x.shape
    return x.reshape(r // 8, 8, c).sum(axis=0)


def _sigmoid(x):
    return 1.0 / (1.0 + jnp.exp(-x))


def _shifts_back(cur, tail, n):
    hb = tail.shape[0]
    xe = jnp.concatenate([tail, cur], axis=0)
    return [cur] + [pltpu.roll(xe, j, axis=0)[hb:] for j in range(1, n)]


def _shifts_fwd(cur, head, n):
    ts = cur.shape[0]
    xe = jnp.concatenate([cur, head], axis=0)
    ln = xe.shape[0]
    return [cur] + [pltpu.roll(xe, ln - j, axis=0)[:ts] for j in range(1, n)]


def _dot_hi(a, b):
    return jnp.dot(a, b, preferred_element_type=F32, precision=lax.Precision.HIGHEST)


def _rmsnorm_fwd(x, g):
    t, d = x.shape
    tm = _tile(t, 512)

    def body(x_ref, g_ref, h_ref):
        xv = x_ref[...]
        r = lax.rsqrt(jnp.mean(xv * xv, axis=-1, keepdims=True) + EPS)
        h_ref[...] = (xv * r * g_ref[...]).astype(BF16)

    return pl.pallas_call(
        body, out_shape=SDS((t, d), BF16), grid=(t // tm,), name="rmsnorm_fwd",
        in_specs=[pl.BlockSpec((tm, d), lambda i: (i, 0)), pl.BlockSpec((1, d), lambda i: (0, 0))],
        out_specs=pl.BlockSpec((tm, d), lambda i: (i, 0)), compiler_params=_cp("parallel"))(x, g)


def _rmsnorm_bwd(x, g, dh, dres):
    t, d = x.shape
    tm = _tile(t, 512)

    def body(x_ref, g_ref, dh_ref, dres_ref, dx_ref, dg_ref):
        @pl.when(pl.program_id(0) == 0)
        def _():
            dg_ref[...] = jnp.zeros_like(dg_ref)

        xv = x_ref[...]
        r = lax.rsqrt(jnp.mean(xv * xv, axis=-1, keepdims=True) + EPS)
        xh = xv * r
        dhv = dh_ref[...].astype(F32)
        u = dhv * g_ref[...]
        dx_ref[...] = dres_ref[...] + r * (u - xh * jnp.mean(u * xh, axis=-1, keepdims=True))
        dg_ref[...] += _fold8(dhv * xh)

    row = pl.BlockSpec((tm, d), lambda i: (i, 0))
    return pl.pallas_call(
        body, out_shape=(SDS((t, d), F32), SDS((8, d), F32)), grid=(t // tm,), name="rmsnorm_bwd",
        in_specs=[row, pl.BlockSpec((1, d), lambda i: (0, 0)), row, row],
        out_specs=(row, pl.BlockSpec((8, d), lambda i: (0, 0))), compiler_params=_cp("arbitrary"))(x, g, dh, dres)


def _mm(a, b, *, trans_b=False, bias=None, residual=None, a2=None, b2=None, out_dtype=BF16, name="mm"):
    m, k = a.shape
    n = b.shape[0] if trans_b else b.shape[1]
    tm, tn, tk = _tile(m, 512, 16), _tile(n, 1024, LANES), _tile(k, 1536, LANES)
    nk = k // tk
    two = a2 is not None
    steps = 2 * nk if two else nk
    dims = (((1,), (1,)), ((), ())) if trans_b else (((1,), (0,)), ((), ()))
    has_bias, has_res = bias is not None, residual is not None

    def body(*refs):
        n_in = 4 if two else 2
        bias_ref = refs[n_in] if has_bias else None
        res_ref = refs[n_in + has_bias] if has_res else None

        def finish(r):
            if has_bias:
                r = r + bias_ref[...]
            if has_res:
                r = r + res_ref[...]
            return r.astype(out_dtype)

        def dot(a_ref, b_ref):
            return lax.dot_general(a_ref[...].astype(BF16), b_ref[...].astype(BF16), dims, preferred_element_type=F32)

        if steps == 1:
            refs[-1][...] = finish(dot(refs[0], refs[1]))
            return
        o_ref, acc_ref = refs[-2], refs[-1]
        kk = pl.program_id(2)

        @pl.when(kk == 0)
        def _():
            acc_ref[...] = jnp.zeros_like(acc_ref)

        @pl.when(kk < nk)
        def _():
            acc_ref[...] += dot(refs[0], refs[1])

        if two:
            @pl.when(kk >= nk)
            def _():
                acc_ref[...] += dot(refs[2], refs[3])

        @pl.when(kk == steps - 1)
        def _():
            o_ref[...] = finish(acc_ref[...])

    def pair(first):
        kmap = (lambda kk: jnp.minimum(kk, nk - 1)) if first else (lambda kk: jnp.maximum(kk - nk, 0))
        a_spec = pl.BlockSpec((tm, tk), lambda i, j, kk: (i, kmap(kk)))
        if trans_b:
            b_spec = pl.BlockSpec((tn, tk), lambda i, j, kk: (j, kmap(kk)))
        else:
            b_spec = pl.BlockSpec((tk, tn), lambda i, j, kk: (kmap(kk), j))
        return [a_spec, b_spec]

    in_specs, args = pair(True), [a, b]
    if two:
        in_specs += pair(False)
        args += [a2, b2]
    if has_bias:
        in_specs.append(pl.BlockSpec((1, tn), lambda i, j, kk: (0, j)))
        args.append(bias)
    if has_res:
        in_specs.append(pl.BlockSpec((tm, tn), lambda i, j, kk: (i, j)))
        args.append(residual)
    return pl.pallas_call(
        body, out_shape=SDS((m, n), out_dtype), grid=(m // tm, n // tn, steps), name=name,
        in_specs=in_specs, out_specs=pl.BlockSpec((tm, tn), lambda i, j, kk: (i, j)),
        scratch_shapes=[] if steps == 1 else [pltpu.VMEM((tm, tn), F32)],
        compiler_params=_cp("parallel", "parallel", "arbitrary"))(*args)


def _wgrad(a, g, *, out_dtype=BF16, name="wgrad"):
    m, ka = a.shape
    n = g.shape[1]
    ta, tn, tm = _tile(ka, 512, LANES), _tile(n, 1024, LANES), _tile(m, 1024)
    nm = m // tm

    def body(a_ref, g_ref, o_ref, cs_ref, acc_ref):
        i, mm = pl.program_id(1), pl.program_id(2)

        @pl.when(mm == 0)
        def _():
            acc_ref[...] = jnp.zeros_like(acc_ref)

        @pl.when((mm == 0) & (i == 0))
        def _():
            cs_ref[...] = jnp.zeros_like(cs_ref)

        gv = g_ref[...]
        acc_ref[...] += lax.dot_general(a_ref[...].astype(BF16), gv.astype(BF16), (((0,), (0,)), ((), ())),
                                        preferred_element_type=F32)

        @pl.when(i == 0)
        def _():
            cs_ref[...] += _fold8(gv.astype(F32))

        @pl.when(mm == nm - 1)
        def _():
            o_ref[...] = acc_ref[...].astype(out_dtype)

    return pl.pallas_call(
        body, out_shape=(SDS((ka, n), out_dtype), SDS((8, n), F32)), grid=(n // tn, ka // ta, nm), name=name,
        in_specs=[pl.BlockSpec((tm, ta), lambda j, i, mm: (mm, i)), pl.BlockSpec((tm, tn), lambda j, i, mm: (mm, j))],
        out_specs=(pl.BlockSpec((ta, tn), lambda j, i, mm: (i, j)), pl.BlockSpec((8, tn), lambda j, i, mm: (0, j))),
        scratch_shapes=[pltpu.VMEM((ta, tn), F32)],
        compiler_params=_cp("arbitrary", "arbitrary", "arbitrary"))(a, g)


FFN_HALO = 16


def _ffn_conv(uc_ref, up_ref, w_ref, b_ref, s):
    u = uc_ref[...].astype(F32)
    tail = jnp.where(s > 0, up_ref[...].astype(F32), 0.0)
    sh = _shifts_back(u, tail, 3)
    return sh, sh[2] * w_ref[0:1, :] + sh[1] * w_ref[1:2, :] + sh[0] * w_ref[2:3, :] + b_ref[...]


def _ffn_act_fwd(uv, ug, dw8, b):
    bsz, s_len, f = uv.shape
    tc, ts = _tile(f, 256, LANES), _tile(s_len, 512, FFN_HALO)
    nf, r = f // tc, ts // FFN_HALO

    def body(uv_ref, uvp_ref, ug_ref, ugp_ref, wv_ref, wg_ref, bv_ref, bg_ref, a_ref):
        s = pl.program_id(2)
        _, val = _ffn_conv(uv_ref, uvp_ref, wv_ref, bv_ref, s)
        _, gate = _ffn_conv(ug_ref, ugp_ref, wg_ref, bg_ref, s)
        a_ref[...] = (gate * _sigmoid(gate) * val).astype(BF16)

    cur = pl.BlockSpec((None, ts, tc), lambda bi, j, s: (bi, s, j))
    prev = pl.BlockSpec((None, FFN_HALO, tc), lambda bi, j, s: (bi, jnp.maximum(s * r - 1, 0), j))

    def par(rows, off):
        return pl.BlockSpec((rows, tc), lambda bi, j, s: (0, j + off))

    return pl.pallas_call(
        body, out_shape=SDS((bsz, s_len, f), BF16), grid=(bsz, nf, s_len // ts), name="ffn_act_fwd",
        in_specs=[cur, prev, cur, prev, par(8, 0), par(8, nf), par(1, 0), par(1, nf)], out_specs=cur,
        compiler_params=_cp("parallel", "parallel", "arbitrary"))(uv, uv, ug, ug, dw8, dw8, b, b)


def _ffn_act_bwd1(uv, ug, da, dw8, b):
    bsz, s_len, f = uv.shape
    tc, ts = _tile(f, 256, LANES), _tile(s_len, 512, FFN_HALO)
    nf, r = f // tc, ts // FFN_HALO

    def body(uv_ref, uvp_ref, ug_ref, ugp_ref, da_ref, wv_ref, wg_ref, bv_ref, bg_ref,
             dvv_ref, dvg_ref, ddwv_ref, ddwg_ref, dbv_ref, dbg_ref):
        bi, s = pl.program_id(1), pl.program_id(2)

        @pl.when((bi == 0) & (s == 0))
        def _():
            for ref in (ddwv_ref, ddwg_ref, dbv_ref, dbg_ref):
                ref[...] = jnp.zeros_like(ref)

        shv, val = _ffn_conv(uv_ref, uvp_ref, wv_ref, bv_ref, s)
        shg, gate = _ffn_conv(ug_ref, ugp_ref, wg_ref, bg_ref, s)
        sg = _sigmoid(gate)
        dav = da_ref[...].astype(F32)
        for dv, sh, dv_ref, ddw_ref, db_ref in (
                (dav * gate * sg, shv, dvv_ref, ddwv_ref, dbv_ref),
                (dav * val * (sg * (1.0 + gate * (1.0 - sg))), shg, dvg_ref, ddwg_ref, dbg_ref)):
            dv_ref[...] = dv.astype(BF16)
            db_ref[...] += _fold8(dv)
            for k in range(3):
                ddw_ref[k] += _fold8(dv * sh[2 - k])

    cur = pl.BlockSpec((None, ts, tc), lambda j, bi, s: (bi, s, j))
    prev = pl.BlockSpec((None, FFN_HALO, tc), lambda j, bi, s: (bi, jnp.maximum(s * r - 1, 0), j))

    def par(rows, off):
        return pl.BlockSpec((rows, tc), lambda j, bi, s: (0, j + off))

    acc3 = pl.BlockSpec((3, 8, tc), lambda j, bi, s: (0, 0, j))
    acc1 = pl.BlockSpec((8, tc), lambda j, bi, s: (0, j))
    return pl.pallas_call(
        body, out_shape=(SDS((bsz, s_len, f), BF16), SDS((bsz, s_len, f), BF16), SDS((3, 8, f), F32),
                         SDS((3, 8, f), F32), SDS((8, f), F32), SDS((8, f), F32)),
        grid=(nf, bsz, s_len // ts), name="ffn_act_bwd1",
        in_specs=[cur, prev, cur, prev, cur, par(8, 0), par(8, nf), par(1, 0), par(1, nf)],
        out_specs=(cur, cur, acc3, acc3, acc1, acc1),
        compiler_params=_cp("arbitrary", "arbitrary", "arbitrary"))(uv, uv, ug, ug, da, dw8, dw8, b, b)


def _ffn_act_bwd2(dvv, dvg, dw8):
    bsz, s_len, f = dvv.shape
    tc, ts = _tile(f, 256, LANES), _tile(s_len, 512, FFN_HALO)
    nf, r, ns = f // tc, ts // FFN_HALO, s_len // ts

    def body(vc_ref, vn_ref, gc_ref, gn_ref, wv_ref, wg_ref, duv_ref, dug_ref):
        s = pl.program_id(2)
        for dc_ref, dn_ref, w_ref, du_ref in ((vc_ref, vn_ref, wv_ref, duv_ref), (gc_ref, gn_ref, wg_ref, dug_ref)):
            d = dc_ref[...].astype(F32)
            head = jnp.where(s < ns - 1, dn_ref[...].astype(F32), 0.0)
            sh = _shifts_fwd(d, head, 3)
            du_ref[...] = (sh[0] * w_ref[2:3, :] + sh[1] * w_ref[1:2, :] + sh[2] * w_ref[0:1, :]).astype(BF16)

    cur = pl.BlockSpec((None, ts, tc), lambda bi, j, s: (bi, s, j))
    nxt = pl.BlockSpec((None, FFN_HALO, tc),
                       lambda bi, j, s: (bi, jnp.minimum((s + 1) * r, s_len // FFN_HALO - 1), j))

    def par(off):
        return pl.BlockSpec((8, tc), lambda bi, j, s: (0, j + off))

    return pl.pallas_call(
        body, out_shape=(SDS((bsz, s_len, f), BF16),) * 2, grid=(bsz, nf, ns), name="ffn_act_bwd2",
        in_specs=[cur, nxt, cur, nxt, par(0), par(nf)], out_specs=(cur, cur),
        compiler_params=_cp("parallel", "parallel", "arbitrary"))(dvv, dvv, dvg, dvg, dw8, dw8)


CONV_HALO = 32
CONV_CHUNK = 256


def _conv_act_fwd(p, dw32, dwb, ln_g, ln_b):
    bsz, s_len, d2 = p.shape
    d = d2 // 2
    kw = 31
    ts = _tile(s_len, 256, CONV_HALO)
    r = ts // CONV_HALO
    cc = min(CONV_CHUNK, d)

    def body(pc_ref, pp_ref, w_ref, wb_ref, g_ref, b_ref, u_ref, s_ref):
        s = pl.program_id(1)
        tot = jnp.zeros((ts, 1), F32)
        for c0 in range(0, d, cc):
            a = pc_ref[:, c0:c0 + cc].astype(F32)
            g = pc_ref[:, d + c0:d + c0 + cc].astype(F32)
            z = a * _sigmoid(g)
            ap = pp_ref[:, c0:c0 + cc].astype(F32)
            gp = pp_ref[:, d + c0:d + c0 + cc].astype(F32)
            tail = jnp.where(s > 0, ap * _sigmoid(gp), 0.0)
            sh = _shifts_back(z, tail, kw)
            acc = wb_ref[:, c0:c0 + cc] + sh[0] * w_ref[kw - 1:kw, c0:c0 + cc]
            for j in range(1, kw):
                acc = acc + sh[j] * w_ref[kw - 1 - j:kw - j, c0:c0 + cc]
            u_ref[:, c0:c0 + cc] = acc
            tot = tot + jnp.sum(acc, axis=-1, keepdims=True)
        u = u_ref[...]
        mu = tot / d
        uc = u - mu
        var = jnp.mean(uc * uc, axis=-1, keepdims=True)
        ul = uc * lax.rsqrt(var + EPS) * g_ref[...] + b_ref[...]
        s_ref[...] = (ul * _sigmoid(ul)).astype(BF16)

    vec = pl.BlockSpec((1, d), lambda bi, s: (0, 0))
    return pl.pallas_call(
        body, out_shape=(SDS((bsz, s_len, d), F32), SDS((bsz, s_len, d), BF16)), grid=(bsz, s_len // ts),
        name="conv_act_fwd",
        in_specs=[pl.BlockSpec((None, ts, d2), lambda bi, s: (bi, s, 0)),
                  pl.BlockSpec((None, CONV_HALO, d2), lambda bi, s: (bi, jnp.maximum(s * r - 1, 0), 0)),
                  pl.BlockSpec((32, d), lambda bi, s: (0, 0)), vec, vec, vec],
        out_specs=(pl.BlockSpec((None, ts, d), lambda bi, s: (bi, s, 0)),
                   pl.BlockSpec((None, ts, d), lambda bi, s: (bi, s, 0))),
        compiler_params=_cp("parallel", "arbitrary"))(p, p, dw32, dwb, ln_g, ln_b)


def _conv_act_bwd1(u, ds, ln_g, ln_b):
    t, d = u.shape
    ts = _tile(t, 256)

    def body(u_ref, ds_ref, g_ref, b_ref, du_ref, dg_ref, db_ref, dwb_ref):
        @pl.when(pl.program_id(0) == 0)
        def _():
            dg_ref[...] = jnp.zeros_like(dg_ref)
            db_ref[...] = jnp.zeros_like(db_ref)
            dwb_ref[...] = jnp.zeros_like(dwb_ref)

        uv = u_ref[...]
        uc = uv - jnp.mean(uv, axis=-1, keepdims=True)
        rstd = lax.rsqrt(jnp.mean(uc * uc, axis=-1, keepdims=True) + EPS)
        uh = uc * rstd
        ul = uh * g_ref[...] + b_ref[...]
        sg = _sigmoid(ul)
        dul = ds_ref[...].astype(F32) * (sg * (1.0 + ul * (1.0 - sg)))
        duh = dul * g_ref[...]
        du = rstd * (duh - jnp.mean(duh, axis=-1, keepdims=True) - uh * jnp.mean(duh * uh, axis=-1, keepdims=True))
        du_ref[...] = du
        dg_ref[...] += _fold8(dul * uh)
        db_ref[...] += _fold8(dul)
        dwb_ref[...] += _fold8(du)

    row = pl.BlockSpec((ts, d), lambda i: (i, 0))
    vec = pl.BlockSpec((1, d), lambda i: (0, 0))
    acc = pl.BlockSpec((8, d), lambda i: (0, 0))
    return pl.pallas_call(
        body, out_shape=(SDS((t, d), F32), SDS((8, d), F32), SDS((8, d), F32), SDS((8, d), F32)), grid=(t // ts,),
        name="conv_act_bwd1", in_specs=[row, row, vec, vec], out_specs=(row, acc, acc, acc),
        compiler_params=_cp("arbitrary"))(u, ds, ln_g, ln_b)


def _conv_act_bwd2(du, p, dw32):
    bsz, s_len, d2 = p.shape
    d = d2 // 2
    kw = 31
    ts = _tile(s_len, 256, CONV_HALO)
    r, ns = ts // CONV_HALO, s_len // ts
    cc = min(CONV_CHUNK, d)

    def body(dc_ref, dn_ref, pc_ref, pp_ref, w_ref, dp_ref, ddw_ref):
        bi, s = pl.program_id(0), pl.program_id(1)

        @pl.when((bi == 0) & (s == 0))
        def _():
            ddw_ref[...] = jnp.zeros_like(ddw_ref)

        for c0 in range(0, d, cc):
            a = pc_ref[:, c0:c0 + cc].astype(F32)
            g = pc_ref[:, d + c0:d + c0 + cc].astype(F32)
            sg = _sigmoid(g)
            z = a * sg
            ap = pp_ref[:, c0:c0 + cc].astype(F32)
            gp = pp_ref[:, d + c0:d + c0 + cc].astype(F32)
            tail = jnp.where(s > 0, ap * _sigmoid(gp), 0.0)
            duv = dc_ref[:, c0:c0 + cc]
            head = jnp.where(s < ns - 1, dn_ref[:, c0:c0 + cc], 0.0)
            zb = _shifts_back(z, tail, kw)
            for k in range(kw):
                ddw_ref[k, :, c0:c0 + cc] += _fold8(duv * zb[kw - 1 - k])
            df = _shifts_fwd(duv, head, kw)
            dz = df[0] * w_ref[kw - 1:kw, c0:c0 + cc]
            for j in range(1, kw):
                dz = dz + df[j] * w_ref[kw - 1 - j:kw - j, c0:c0 + cc]
            dp_ref[:, c0:c0 + cc] = (dz * sg).astype(BF16)
            dp_ref[:, d + c0:d + c0 + cc] = (dz * a * sg * (1.0 - sg)).astype(BF16)

    return pl.pallas_call(
        body, out_shape=(SDS((bsz, s_len, d2), BF16), SDS((32, 8, d), F32)), grid=(bsz, ns), name="conv_act_bwd2",
        in_specs=[pl.BlockSpec((None, ts, d), lambda bi, s: (bi, s, 0)),
                  pl.BlockSpec((None, CONV_HALO, d),
                               lambda bi, s: (bi, jnp.minimum((s + 1) * r, s_len // CONV_HALO - 1), 0)),
                  pl.BlockSpec((None, ts, d2), lambda bi, s: (bi, s, 0)),
                  pl.BlockSpec((None, CONV_HALO, d2), lambda bi, s: (bi, jnp.maximum(s * r - 1, 0), 0)),
                  pl.BlockSpec((32, d), lambda bi, s: (0, 0))],
        out_specs=(pl.BlockSpec((None, ts, d2), lambda bi, s: (bi, s, 0)),
                   pl.BlockSpec((32, 8, d), lambda bi, s: (0, 0, 0))),
        compiler_params=_cp("arbitrary", "arbitrary"))(du, du, p, p, dw32)


POOL_HALO = 16


def _pool_counts(s, ts, rows, w):
    t = s * ts + lax.broadcasted_iota(jnp.int32, (rows, 1), 0)
    return jnp.minimum(t + 1, w).astype(F32)


def _pool_fwd(x, gmix, w, b, scale):
    bsz, s_len, d = x.shape
    ng = len(POOL_WINDOWS)
    cg = d // ng
    ts = _tile(s_len, 512, POOL_HALO)
    r = ts // POOL_HALO

    def body(xc_ref, xp_ref, g_ref, w_ref, b_ref, sc_ref, y_ref, p_ref):
        s = pl.program_id(1)

        def norm(v):
            return v * lax.rsqrt(jnp.mean(v * v, axis=-1, keepdims=True) + EPS) * g_ref[...]

        xc = xc_ref[...]
        h = norm(xc)
        tail = jnp.where(s > 0, norm(xp_ref[...]), 0.0)
        for gi, win in enumerate(POOL_WINDOWS):
            lo, hi = gi * cg, (gi + 1) * cg
            hg = h[:, lo:hi]
            acc = jnp.concatenate([tail[:, lo:hi], hg], axis=0)
            step = 1
            while step < win:
                acc = acc + pltpu.roll(acc, step, axis=0)
                step *= 2
            pg = acc[POOL_HALO:] / _pool_counts(s, ts, ts, win) - hg
            pb = pg.astype(BF16)
            p_ref[:, lo:hi] = pb
            yg = jnp.dot(pb, w_ref[gi], preferred_element_type=F32) + b_ref[:, lo:hi]
            y_ref[:, lo:hi] = xc[:, lo:hi] + yg * sc_ref[:, lo:hi]

    vec = pl.BlockSpec((1, d), lambda bi, s: (0, 0))
    blk = pl.BlockSpec((None, ts, d), lambda bi, s: (bi, s, 0))
    return pl.pallas_call(
        body, out_shape=(SDS((bsz, s_len, d), F32), SDS((bsz, s_len, d), BF16)), grid=(bsz, s_len // ts),
        name="pool_fwd",
        in_specs=[blk, pl.BlockSpec((None, POOL_HALO, d), lambda bi, s: (bi, jnp.maximum(s * r - 1, 0), 0)),
                  vec, pl.BlockSpec((ng, cg, cg), lambda bi, s: (0, 0, 0)), vec, vec],
        out_specs=(blk, blk), compiler_params=_cp("parallel", "arbitrary"))(x, x, gmix, w, b, scale)


def _pool_bwd(x, dy, p, gmix, w, b, scale):
    bsz, s_len, d = x.shape
    ng = len(POOL_WINDOWS)
    cg = d // ng
    ts = _tile(s_len, 512, POOL_HALO)
    r, ns = ts // POOL_HALO, s_len // ts
    nt = (((1,), (1,)), ((), ()))
    tn = (((0,), (0,)), ((), ()))

    def body(x_ref, dy_ref, dyn_ref, p_ref, g_ref, w_ref, b_ref, sc_ref, dx_ref, dw_ref, db_ref, dsc_ref, dg_ref):
        bi, s = pl.program_id(0), pl.program_id(1)

        @pl.when((bi == 0) & (s == 0))
        def _():
            dw_ref[...] = jnp.zeros_like(dw_ref)
            db_ref[...] = jnp.zeros_like(db_ref)
            dsc_ref[...] = jnp.zeros_like(dsc_ref)
            dg_ref[...] = jnp.zeros_like(dg_ref)

        dy = dy_ref[...]
        dyy = dy * sc_ref[...]
        dyy_n = jnp.where(s < ns - 1, dyn_ref[...] * sc_ref[...], 0.0)
        db_ref[...] += _fold8(dyy)
        xv = x_ref[...]
        rr = lax.rsqrt(jnp.mean(xv * xv, axis=-1, keepdims=True) + EPS)
        xh = xv * rr
        for gi, win in enumerate(POOL_WINDOWS):
            lo, hi = gi * cg, (gi + 1) * cg
            pb = p_ref[:, lo:hi]
            wg = w_ref[gi]
            pre = jnp.dot(pb, wg, preferred_element_type=F32) + b_ref[:, lo:hi]
            dsc_ref[:, lo:hi] += _fold8(dy[:, lo:hi] * pre)
            dyb = dyy[:, lo:hi].astype(BF16)
            dw_ref[gi] += lax.dot_general(pb, dyb, tn, preferred_element_type=F32)
            dp = lax.dot_general(dyb, wg, nt, preferred_element_type=F32)
            dp_n = lax.dot_general(dyy_n[:, lo:hi].astype(BF16), wg, nt, preferred_element_type=F32)
            q = dp / _pool_counts(s, ts, ts, win)
            q_n = dp_n / _pool_counts(s + 1, ts, POOL_HALO, win)
            acc = jnp.concatenate([q, q_n], axis=0)
            ln = ts + POOL_HALO
            step = 1
            while step < win:
                acc = acc + pltpu.roll(acc, ln - step, axis=0)
                step *= 2
            dh = acc[:ts] - dp
            xhg = xh[:, lo:hi]
            dg_ref[:, lo:hi] += _fold8(dh * xhg)
            dx_ref[:, lo:hi] = dh * g_ref[:, lo:hi]
        u = dx_ref[...]
        dx_ref[...] = dy + rr * (u - xh * jnp.mean(u * xh, axis=-1, keepdims=True))

    vec = pl.BlockSpec((1, d), lambda bi, s: (0, 0))
    acc8 = pl.BlockSpec((8, d), lambda bi, s: (0, 0))
    blk = pl.BlockSpec((None, ts, d), lambda bi, s: (bi, s, 0))
    wspec = pl.BlockSpec((ng, cg, cg), lambda bi, s: (0, 0, 0))
    return pl.pallas_call(
        body, out_shape=(SDS((bsz, s_len, d), F32), SDS((ng, cg, cg), F32), SDS((8, d), F32), SDS((8, d), F32),
                         SDS((8, d), F32)),
        grid=(bsz, ns), name="pool_bwd",
        in_specs=[blk, blk,
                  pl.BlockSpec((None, POOL_HALO, d),
                               lambda bi, s: (bi, jnp.minimum((s + 1) * r, s_len // POOL_HALO - 1), 0)),
                  blk, vec, wspec, vec, vec],
        out_specs=(blk, wspec, acc8, acc8, acc8),
        compiler_params=_cp("arbitrary", "arbitrary"))(x, dy, dy, p, gmix, w, b, scale)


def _tri(n, upper):
    row = lax.broadcasted_iota(jnp.int32, (n, n), 0)
    col = lax.broadcasted_iota(jnp.int32, (n, n), 1)
    return jnp.where((col >= row) if upper else (col <= row), 1.0, 0.0).astype(F32)


def _fox_gate_fwd(proj, bf, n_heads):
    bsz, s_len, width = proj.shape
    col = width // LANES - 1
    ts = _tile(s_len, 512)

    def body(fl_ref, b_ref, c_ref, carry_ref):
        @pl.when(pl.program_id(1) == 0)
        def _():
            carry_ref[...] = jnp.zeros_like(carry_ref)

        xv = fl_ref[...] + b_ref[...]
        logf = jnp.minimum(xv, 0.0) - jnp.log(1.0 + jnp.exp(-jnp.abs(xv)))
        lane = lax.broadcasted_iota(jnp.int32, (1, LANES), 1)
        logf = jnp.where(lane < n_heads, logf, 0.0)
        c = _dot_hi(_tri(ts, False), logf) + carry_ref[0:1, :]
        c_ref[...] = c
        carry_ref[0:1, :] = c[ts - 1:ts, :]

    return pl.pallas_call(
        body, out_shape=SDS((bsz, s_len, LANES), F32), grid=(bsz, s_len // ts), name="fox_gate_fwd",
        in_specs=[pl.BlockSpec((None, ts, LANES), lambda bi, s: (bi, s, col)),
                  pl.BlockSpec((1, LANES), lambda bi, s: (0, 0))],
        out_specs=pl.BlockSpec((None, ts, LANES), lambda bi, s: (bi, s, 0)),
        scratch_shapes=[pltpu.VMEM((8, LANES), F32)],
        compiler_params=_cp("arbitrary", "arbitrary"))(proj, bf)


def _fox_gate_bwd(dc, proj, bf, n_heads):
    bsz, s_len, width = proj.shape
    col = width // LANES - 1
    ts = _tile(s_len, 512)
    ns = s_len // ts

    def body(dc_ref, fl_ref, b_ref, dfl_ref, db_ref, carry_ref):
        bi, s = pl.program_id(0), pl.program_id(1)

        @pl.when((bi == 0) & (s == 0))
        def _():
            db_ref[...] = jnp.zeros_like(db_ref)

        @pl.when(s == 0)
        def _():
            carry_ref[...] = jnp.zeros_like(carry_ref)

        dlogf = _dot_hi(_tri(ts, True), dc_ref[...]) + carry_ref[0:1, :]
        carry_ref[0:1, :] = dlogf[0:1, :]
        lane = lax.broadcasted_iota(jnp.int32, (1, LANES), 1)
        dfl = jnp.where(lane < n_heads, dlogf * (1.0 - _sigmoid(fl_ref[...] + b_ref[...])), 0.0)
        dfl_ref[...] = dfl.astype(BF16)
        db_ref[...] += _fold8(dfl)

    return pl.pallas_call(
        body, out_shape=(SDS((bsz, s_len, LANES), BF16), SDS((8, LANES), F32)), grid=(bsz, ns), name="fox_gate_bwd",
        in_specs=[pl.BlockSpec((None, ts, LANES), lambda bi, s: (bi, ns - 1 - s, 0)),
                  pl.BlockSpec((None, ts, LANES), lambda bi, s: (bi, ns - 1 - s, col)),
                  pl.BlockSpec((1, LANES), lambda bi, s: (0, 0))],
        out_specs=(pl.BlockSpec((None, ts, LANES), lambda bi, s: (bi, ns - 1 - s, 0)),
                   pl.BlockSpec((8, LANES), lambda bi, s: (0, 0))),
        scratch_shapes=[pltpu.VMEM((8, LANES), F32)],
        compiler_params=_cp("arbitrary", "arbitrary"))(dc, proj, bf)


def _head_maps(d):
    ch = lax.broadcasted_iota(jnp.int32, (d, LANES), 0) // HEAD_DIM
    hd = lax.broadcasted_iota(jnp.int32, (d, LANES), 1)
    e = jnp.where(ch == hd, 1.0, 0.0).astype(F32)
    cht = lax.broadcasted_iota(jnp.int32, (LANES, d), 1) // HEAD_DIM
    hdt = lax.broadcasted_iota(jnp.int32, (LANES, d), 0)
    et = jnp.where(cht == hdt, 1.0, 0.0).astype(F32)
    return e, et


def _fox_qknorm_fwd(proj, gq, gk, d):
    t = proj.shape[0]
    ts = _tile(t, 256)
    scale = 1.0 / math.sqrt(HEAD_DIM)

    def body(q_ref, k_ref, v_ref, gq_ref, gk_ref, qn_ref, kn_ref, vb_ref):
        e, et = _head_maps(d)

        def norm(v, g):
            r = lax.rsqrt(_dot_hi(v * v, e) / HEAD_DIM + EPS)
            return v * _dot_hi(r, et) * g

        qn_ref[...] = (norm(q_ref[...], gq_ref[...]) * scale).astype(BF16)
        kn_ref[...] = norm(k_ref[...], gk_ref[...]).astype(BF16)
        vb_ref[...] = v_ref[...].astype(BF16)

    def colblk(j):
        return pl.BlockSpec((ts, d), lambda i: (i, j))

    vec = pl.BlockSpec((1, d), lambda i: (0, 0))
    out = pl.BlockSpec((ts, d), lambda i: (i, 0))
    return pl.pallas_call(
        body, out_shape=(SDS((t, d), BF16),) * 3, grid=(t // ts,), name="fox_qknorm_fwd",
        in_specs=[colblk(0), colblk(1), colblk(2), vec, vec], out_specs=(out, out, out),
        compiler_params=_cp("parallel"))(proj, proj, proj, gq, gk)


def _fox_qknorm_bwd(proj, dq, dk, dv, gq, gk, d):
    t = proj.shape[0]
    ts = _tile(t, 256)
    scale = 1.0 / math.sqrt(HEAD_DIM)

    def body(q_ref, k_ref, dq_ref, dk_ref, dv_ref, gq_ref, gk_ref, dp_ref, dgq_ref, dgk_ref):
        @pl.when(pl.program_id(0) == 0)
        def _():
            dgq_ref[...] = jnp.zeros_like(dgq_ref)
            dgk_ref[...] = jnp.zeros_like(dgk_ref)

        e, et = _head_maps(d)

        def back(v, g, dn, dg_ref):
            r = _dot_hi(lax.rsqrt(_dot_hi(v * v, e) / HEAD_DIM + EPS), et)
            vh = v * r
            dg_ref[...] += _fold8(dn * vh)
            u = dn * g
            mh = _dot_hi(_dot_hi(u * vh, e) / HEAD_DIM, et)
            return r * (u - vh * mh)

        dp_ref[:, 0:d] = back(q_ref[...], gq_ref[...], dq_ref[...] * scale, dgq_ref).astype(BF16)
        dp_ref[:, d:2 * d] = back(k_ref[...], gk_ref[...], dk_ref[...], dgk_ref).astype(BF16)
        dp_ref[:, 2 * d:3 * d] = dv_ref[...]

    def colblk(j):
        return pl.BlockSpec((ts, d), lambda i: (i, j))

    row = pl.BlockSpec((ts, d), lambda i: (i, 0))
    vec = pl.BlockSpec((1, d), lambda i: (0, 0))
    acc = pl.BlockSpec((8, d), lambda i: (0, 0))
    return pl.pallas_call(
        body, out_shape=(SDS((t, 3 * d), BF16), SDS((8, d), F32), SDS((8, d), F32)), grid=(t // ts,),
        name="fox_qknorm_bwd", in_specs=[colblk(0), colblk(1), row, row, row, vec, vec],
        out_specs=(pl.BlockSpec((ts, 3 * d), lambda i: (i, 0)), acc, acc),
        compiler_params=_cp("arbitrary"))(proj, proj, dq, dk, dv, gq, gk)


ATT_BLOCK = 512
_NT = (((1,), (1,)), ((), ()))
_TN = (((0,), (0,)), ((), ()))


def _head_mask(h):
    return (lax.broadcasted_iota(jnp.int32, (1, LANES), 1) // HEAD_DIM) == h


def _causal(qi, ki, tq, tk):
    row = qi * tq + lax.broadcasted_iota(jnp.int32, (tq, 1), 0)
    col = ki * tk + lax.broadcasted_iota(jnp.int32, (1, tk), 1)
    return col <= row


def _flash_fwd(q, k, v, crow):
    bsz, s_len, d = q.shape
    nj = d // LANES
    tq = tk = _tile(s_len, ATT_BLOCK, LANES)
    nq = s_len // tq

    def body(q_ref, k_ref, v_ref, c_ref, o_ref, lse_ref, m_ref, l_ref, acc_ref):
        qi, ki = pl.program_id(2), pl.program_id(3)

        @pl.when(ki == 0)
        def _():
            m_ref[...] = jnp.full_like(m_ref, NEG)
            l_ref[...] = jnp.zeros_like(l_ref)
            acc_ref[...] = jnp.zeros_like(acc_ref)

        @pl.when(ki <= qi)
        def _():
            qv, kv, vv = q_ref[...], k_ref[...], v_ref[...]
            mask = _causal(qi, ki, tq, tk)
            for h in range(2):
                qh = jnp.where(_head_mask(h), qv, jnp.zeros_like(qv))
                s = lax.dot_general(qh, kv, _NT, preferred_element_type=F32) - c_ref[h:h + 1, :]
                s = jnp.where(mask, s, NEG)
                m_prev = m_ref[h]
                m_new = jnp.maximum(m_prev, jnp.max(s, axis=1, keepdims=True))
                pm = jnp.exp(s - m_new)
                alpha = jnp.exp(m_prev - m_new)
                l_ref[h] = alpha * l_ref[h] + jnp.sum(pm, axis=1, keepdims=True)
                p_hi = pm.astype(BF16)
                p_lo = (pm - p_hi.astype(F32)).astype(BF16)
                acc_ref[h] = (alpha * acc_ref[h] + jnp.dot(p_hi, vv, preferred_element_type=F32)
                              + jnp.dot(p_lo, vv, preferred_element_type=F32))
                m_ref[h] = m_new

        @pl.when(ki == qi)
        def _():
            m0 = _head_mask(0)
            o_ref[...] = jnp.where(m0, acc_ref[0] / l_ref[0], acc_ref[1] / l_ref[1])
            lse_ref[...] = jnp.where(m0, m_ref[0] + jnp.log(l_ref[0]), m_ref[1] + jnp.log(l_ref[1]))

    return pl.pallas_call(
        body, out_shape=(SDS((bsz, s_len, d), F32), SDS((bsz, nj, s_len, LANES), F32)), grid=(bsz, nj, nq, nq),
        name="flash_fwd",
        in_specs=[pl.BlockSpec((None, tq, LANES), lambda bi, j, qi, ki: (bi, qi, j)),
                  pl.BlockSpec((None, tk, LANES), lambda bi, j, qi, ki: (bi, jnp.minimum(ki, qi), j)),
                  pl.BlockSpec((None, tk, LANES), lambda bi, j, qi, ki: (bi, jnp.minimum(ki, qi), j)),
                  pl.BlockSpec((None, None, 2, tk), lambda bi, j, qi, ki: (bi, j, 0, jnp.minimum(ki, qi)))],
        out_specs=(pl.BlockSpec((None, tq, LANES), lambda bi, j, qi, ki: (bi, qi, j)),
                   pl.BlockSpec((None, None, tq, LANES), lambda bi, j, qi, ki: (bi, j, qi, 0))),
        scratch_shapes=[pltpu.VMEM((2, tq, 1), F32), pltpu.VMEM((2, tq, 1), F32), pltpu.VMEM((2, tq, LANES), F32)],
        compiler_params=_cp("parallel", "parallel", "arbitrary", "arbitrary"))(q, k, v, crow)


def _flash_probs(qv, kv, vv, dov, ov, lse, c_ref, h, mask):
    hm = _head_mask(h)
    qh = jnp.where(hm, qv, jnp.zeros_like(qv))
    s = lax.dot_general(qh, kv, _NT, preferred_element_type=F32) - c_ref[h:h + 1, :]
    pm = jnp.where(mask, jnp.exp(s - lse[:, h * HEAD_DIM:h * HEAD_DIM + 1]), 0.0)
    doh = jnp.where(hm, dov, jnp.zeros_like(dov))
    dpm = lax.dot_general(doh, vv, _NT, preferred_element_type=F32)
    delta = jnp.sum(jnp.where(hm, dov.astype(F32) * ov, 0.0), axis=1, keepdims=True)
    return pm, pm * (dpm - delta)


def _flash_bwd_dkv(q, k, v, do, o, lse, crow):
    bsz, s_len, d = q.shape
    nj = d // LANES
    tq = tk = _tile(s_len, ATT_BLOCK, LANES)
    nq = s_len // tq

    def body(q_ref, k_ref, v_ref, do_ref, o_ref, lse_ref, c_ref, dk_ref, dv_ref, dc_ref, dka_ref, dva_ref, dca_ref):
        ki, qi = pl.program_id(2), pl.program_id(3)

        @pl.when(qi == ki)
        def _():
            dka_ref[...] = jnp.zeros_like(dka_ref)
            dva_ref[...] = jnp.zeros_like(dva_ref)
            dca_ref[...] = jnp.zeros_like(dca_ref)

        @pl.when(qi >= ki)
        def _():
            qv, kv, vv, dov, ov, lse = q_ref[...], k_ref[...], v_ref[...], do_ref[...], o_ref[...], lse_ref[...]
            mask = _causal(qi, ki, tq, tk)
            for h in range(2):
                pm, ds = _flash_probs(qv, kv, vv, dov, ov, lse, c_ref, h, mask)
                dva_ref[h] += lax.dot_general(pm.astype(BF16), dov, _TN, preferred_element_type=F32)
                dka_ref[h] += lax.dot_general(ds.astype(BF16), qv, _TN, preferred_element_type=F32)
                dca_ref[h:h + 1, :] -= jnp.sum(ds, axis=0, keepdims=True)

        @pl.when(qi == nq - 1)
        def _():
            m0 = _head_mask(0)
            dk_ref[...] = jnp.where(m0, dka_ref[0], dka_ref[1])
            dv_ref[...] = jnp.where(m0, dva_ref[0], dva_ref[1]).astype(BF16)
            dc_ref[...] = dca_ref[0:2, :]

    def qside(bi, j, ki, qi):
        return (bi, jnp.maximum(qi, ki), j)

    def kside(bi, j, ki, qi):
        return (bi, ki, j)

    return pl.pallas_call(
        body, out_shape=(SDS((bsz, s_len, d), F32), SDS((bsz, s_len, d), BF16), SDS((bsz, nj, 2, s_len), F32)),
        grid=(bsz, nj, nq, nq), name="flash_bwd_dkv",
        in_specs=[pl.BlockSpec((None, tq, LANES), qside), pl.BlockSpec((None, tk, LANES), kside),
                  pl.BlockSpec((None, tk, LANES), kside), pl.BlockSpec((None, tq, LANES), qside),
                  pl.BlockSpec((None, tq, LANES), qside),
                  pl.BlockSpec((None, None, tq, LANES), lambda bi, j, ki, qi: (bi, j, jnp.maximum(qi, ki), 0)),
                  pl.BlockSpec((None, None, 2, tk), lambda bi, j, ki, qi: (bi, j, 0, ki))],
        out_specs=(pl.BlockSpec((None, tk, LANES), kside), pl.BlockSpec((None, tk, LANES), kside),
                   pl.BlockSpec((None, None, 2, tk), lambda bi, j, ki, qi: (bi, j, 0, ki))),
        scratch_shapes=[pltpu.VMEM((2, tk, LANES), F32), pltpu.VMEM((2, tk, LANES), F32), pltpu.VMEM((8, tk), F32)],
        compiler_params=_cp("parallel", "parallel", "arbitrary", "arbitrary"))(q, k, v, do, o, lse, crow)


def _flash_bwd_dq(q, k, v, do, o, lse, crow):
    bsz, s_len, d = q.shape
    nj = d // LANES
    tq = tk = _tile(s_len, ATT_BLOCK, LANES)
    nq = s_len // tq

    def body(q_ref, k_ref, v_ref, do_ref, o_ref, lse_ref, c_ref, dq_ref, dqa_ref):
        qi, ki = pl.program_id(2), pl.program_id(3)

        @pl.when(ki == 0)
        def _():
            dqa_ref[...] = jnp.zeros_like(dqa_ref)

        @pl.when(ki <= qi)
        def _():
            qv, kv, vv, dov, ov, lse = q_ref[...], k_ref[...], v_ref[...], do_ref[...], o_ref[...], lse_ref[...]
            mask = _causal(qi, ki, tq, tk)
            for h in range(2):
                _, ds = _flash_probs(qv, kv, vv, dov, ov, lse, c_ref, h, mask)
                dqa_ref[h] += jnp.dot(ds.astype(BF16), kv, preferred_element_type=F32)

        @pl.when(ki == qi)
        def _():
            dq_ref[...] = jnp.where(_head_mask(0), dqa_ref[0], dqa_ref[1])

    def qside(bi, j, qi, ki):
        return (bi, qi, j)

    def kside(bi, j, qi, ki):
        return (bi, jnp.minimum(ki, qi), j)

    return pl.pallas_call(
        body, out_shape=SDS((bsz, s_len, d), F32), grid=(bsz, nj, nq, nq), name="flash_bwd_dq",
        in_specs=[pl.BlockSpec((None, tq, LANES), qside), pl.BlockSpec((None, tk, LANES), kside),
                  pl.BlockSpec((None, tk, LANES), kside), pl.BlockSpec((None, tq, LANES), qside),
                  pl.BlockSpec((None, tq, LANES), qside),
                  pl.BlockSpec((None, None, tq, LANES), lambda bi, j, qi, ki: (bi, j, qi, 0)),
                  pl.BlockSpec((None, None, 2, tk), lambda bi, j, qi, ki: (bi, j, 0, jnp.minimum(ki, qi)))],
        out_specs=pl.BlockSpec((None, tq, LANES), qside),
        scratch_shapes=[pltpu.VMEM((2, tq, LANES), F32)],
        compiler_params=_cp("parallel", "parallel", "arbitrary", "arbitrary"))(q, k, v, do, o, lse, crow)


def _loss_head(y, target):
    t, d = y.shape
    tm = _tile(t, 512)

    def body(y_ref, t_ref, dy_ref, acc_ref):
        @pl.when(pl.program_id(0) == 0)
        def _():
            acc_ref[...] = jnp.zeros_like(acc_ref)

        err = y_ref[...] - t_ref[...]
        dy_ref[...] = err / d
        acc_ref[...] += _fold8(err * err)

    row = pl.BlockSpec((tm, d), lambda i: (i, 0))
    return pl.pallas_call(
        body, out_shape=(SDS((t, d), F32), SDS((8, d), F32)), grid=(t // tm,), name="loss_head",
        in_specs=[row, row], out_specs=(row, pl.BlockSpec((8, d), lambda i: (0, 0))),
        compiler_params=_cp("arbitrary"))(y, target)


ADAM_COLS = 1024


def _adamw(g8, w, m, v):
    shape = w.shape
    cols = shape[-1]
    rows = w.size // cols
    g8, w, m, v = g8.reshape(N_DEV, rows, cols), w.reshape(rows, cols), m.reshape(rows, cols), v.reshape(rows, cols)
    tr = _tile(rows, 256, 16)
    c1 = 1.0 - ADAM_B1 ** ADAM_STEP
    c2 = 1.0 - ADAM_B2 ** ADAM_STEP

    def body(g8_ref, w_ref, m_ref, v_ref, g_ref, d_ref, nm_ref, nv_ref):
        g = g8_ref[0].astype(F32)
        for i in range(1, N_DEV):
            g = g + g8_ref[i].astype(F32)
        mn = ADAM_B1 * m_ref[...] + (1.0 - ADAM_B1) * g
        vn = ADAM_B2 * v_ref[...] + (1.0 - ADAM_B2) * (g * g)
        g_ref[...] = g
        nm_ref[...] = mn
        nv_ref[...] = vn
        d_ref[...] = -ADAM_LR * ((mn / c1) / (jnp.sqrt(vn / c2) + ADAM_EPS) + ADAM_WD * w_ref[...])

    blk = pl.BlockSpec((tr, cols), lambda i: (i, 0))
    outs = pl.pallas_call(
        body, out_shape=(SDS((rows, cols), F32),) * 4, grid=(rows // tr,), name="adamw",
        in_specs=[pl.BlockSpec((N_DEV, tr, cols), lambda i: (0, i, 0)), blk, blk, blk], out_specs=(blk,) * 4,
        compiler_params=_cp("parallel"))(g8, w, m, v)
    return [o.reshape(shape) for o in outs]


def _mesh_place():
    x, y, c = lax.axis_index("x"), lax.axis_index("y"), lax.axis_index("c")
    return x, y, c, 4 * x + 2 * y + c


def _gather(shards):
    n = len(shards)

    def body(*refs):
        ins, outs = refs[:n], refs[n:2 * n]
        send_sems, recv_sems, local_sems = refs[2 * n:]
        x, y, c, me = _mesh_place()
        sibling = (x, y, 1 - c)
        chips = [(1 - x, y), (x, 1 - y), (1 - x, 1 - y)]

        def block(px, py, pc):
            return 4 * px + 2 * py + pc

        def copy(t, k, blk, to, src=None):
            return pltpu.make_async_remote_copy(
                src_ref=outs[t].at[blk] if src is None else src, dst_ref=outs[t].at[blk],
                send_sem=send_sems.at[t, k], recv_sem=recv_sems.at[t, k], device_id=to,
                device_id_type=pl.DeviceIdType.MESH)

        own = [pltpu.make_async_copy(ins[t], outs[t].at[me], local_sems.at[t]) for t in range(n)]
        first = []
        for t in range(n):
            own[t].start()
            first.append(copy(t, 0, me, sibling, src=ins[t]))
            first += [copy(t, 1 + j, me, (*chip, c), src=ins[t]) for j, chip in enumerate(chips)]
        for cp in first:
            cp.start()
        passed = []
        for j, chip in enumerate(chips):
            for t in range(n):
                copy(t, 1 + j, block(*chip, c), (x, y, c)).wait_recv()
                cp = copy(t, 4 + j, block(*chip, c), sibling)
                cp.start()
                passed.append(cp)
        for t in range(n):
            copy(t, 0, block(x, y, 1 - c), (x, y, c)).wait_recv()
            for j, chip in enumerate(chips):
                copy(t, 4 + j, block(*chip, 1 - c), (x, y, c)).wait_recv()
        for cp in first + passed:
            cp.wait_send()
        for cp in own:
            cp.wait()

    hbm = pl.BlockSpec(memory_space=pl.ANY)
    return pl.pallas_call(
        body, out_shape=[SDS((N_DEV,) + tuple(s.shape), s.dtype) for s in shards], name="gather",
        in_specs=[hbm] * n, out_specs=[hbm] * n,
        scratch_shapes=[pltpu.SemaphoreType.DMA((n, N_DEV - 1)), pltpu.SemaphoreType.DMA((n, N_DEV - 1)),
                        pltpu.SemaphoreType.DMA((n,))])(*shards)


def _scatter(items, groups):
    n = len(items)
    place = {it: (g, l) for g, members in enumerate(groups) for l, it in enumerate(members)}

    def body(*refs):
        ins, outs = refs[:n], refs[n:n + len(groups)]
        send_sems, recv_sems, local_sems = refs[n + len(groups):]
        x, y, c, me = _mesh_place()
        copies = []
        for it in range(n):
            g, l = place[it]
            own = pltpu.make_async_copy(ins[it].at[me], outs[g].at[me, l], local_sems.at[it])
            own.start()
            copies.append(own)
            for kbits in range(1, N_DEV):
                px = 1 - x if kbits & 4 else x
                py = 1 - y if kbits & 2 else y
                pc = 1 - c if kbits & 1 else c
                cp = pltpu.make_async_remote_copy(
                    src_ref=ins[it].at[4 * px + 2 * py + pc], dst_ref=outs[g].at[me, l],
                    send_sem=send_sems.at[it, kbits - 1], recv_sem=recv_sems.at[it, kbits - 1],
                    device_id=(px, py, pc), device_id_type=pl.DeviceIdType.MESH)
                cp.start()
                copies.append(cp)
        for cp in copies:
            cp.wait()

    hbm = pl.BlockSpec(memory_space=pl.ANY)
    out_shape = [SDS((N_DEV, len(members)) + tuple(items[members[0]].shape[1:]), items[members[0]].dtype)
                 for members in groups]
    return pl.pallas_call(
        body, out_shape=out_shape, name="scatter", in_specs=[hbm] * n, out_specs=[hbm] * len(groups),
        scratch_shapes=[pltpu.SemaphoreType.DMA((n, N_DEV - 1)), pltpu.SemaphoreType.DMA((n, N_DEV - 1)),
                        pltpu.SemaphoreType.DMA((n,))])(*items)


def _cat_lanes(g, layer, nb, blk, width):
    _, _, rows, c = g.shape
    tr = _tile(rows, 256, 16)

    def body(g_ref, o_ref):
        for p in range(nb):
            o_ref[:, p * c:(p + 1) * c] = g_ref[p]
        if width > nb * c:
            o_ref[:, nb * c:] = jnp.zeros((tr, width - nb * c), g.dtype)

    return pl.pallas_call(
        body, out_shape=SDS((rows, width), g.dtype), grid=(rows // tr,), name="cat_lanes",
        in_specs=[pl.BlockSpec((nb, None, tr, c), lambda i: (blk, layer, i, 0))],
        out_specs=pl.BlockSpec((tr, width), lambda i: (i, 0)),
        compiler_params=_cp("parallel"))(g)


def _split_lanes(parts, c):
    rows = parts[0].shape[0]
    counts = [p.shape[1] // c for p in parts]
    tr = _tile(rows, 256, 16)

    def body(*refs):
        o_ref = refs[-1]
        q = 0
        for x_ref, cnt in zip(refs[:-1], counts):
            for p in range(cnt):
                o_ref[q] = x_ref[:, p * c:(p + 1) * c]
                q += 1

    return pl.pallas_call(
        body, out_shape=SDS((sum(counts), rows, c), parts[0].dtype), grid=(rows // tr,), name="split_lanes",
        in_specs=[pl.BlockSpec((tr, p.shape[1]), lambda i: (i, 0)) for p in parts],
        out_specs=pl.BlockSpec((sum(counts), tr, c), lambda i: (0, i, 0)),
        compiler_params=_cp("parallel"))(*parts)


def _unshard(g8, shard_shape, axis):
    full = jnp.moveaxis(g8.reshape((N_DEV,) + tuple(shard_shape)), 0, axis)
    shape = list(shard_shape)
    shape[axis] *= N_DEV
    return full.reshape(shape)


def _to_shards(full, axis):
    shape = list(full.shape)
    shape[axis:axis + 1] = [N_DEV, shape[axis] // N_DEV]
    return jnp.moveaxis(full.reshape(shape), axis, 0).reshape(N_DEV, -1)


SMALL = [n for n in SHARDED if n not in MATRICES]


def _flat_rows(parts):
    flat = jnp.concatenate([p.reshape(-1) for p in parts])
    chunk = 8 * ADAM_COLS
    n = -(-flat.shape[0] // chunk) * chunk
    return jnp.pad(flat, (0, n - flat.shape[0])).reshape(n // ADAM_COLS, ADAM_COLS)


def _prepare_weights(gathered, small, shards):
    wt = {}
    flat = small.reshape(N_DEV, -1)
    off = 0
    for n in SMALL:
        size = shards[n].size
        wt[n] = _unshard(flat[:, off:off + size], shards[n].shape, SHARD_AXIS[n])
        off += size
    for n in ('conv_w_out', 'fox_w_o', 'ffn_w_down'):
        g = gathered[n]
        wt[n] = [g[:, l].reshape(N_DEV * g.shape[2], g.shape[3]) for l in range(g.shape[1])]
    g = gathered['pool_w']
    wt['pool_w'] = [jnp.moveaxis(g[:, l], 0, 1).reshape(g.shape[2], N_DEV * g.shape[3], g.shape[4])
                    for l in range(g.shape[1])]
    g = gathered['conv_w_in']
    wt['conv_w_in'] = [_cat_lanes(g, l, N_DEV, 0, N_DEV * g.shape[3]) for l in range(g.shape[1])]
    g = gathered['fox_w_in']
    wt['fox_w_in'] = [_cat_lanes(g, l, N_DEV, 0, 3 * g.shape[2] + LANES) for l in range(g.shape[1])]
    g = gathered['ffn_w_up']
    half = N_DEV // 2
    wt['ffn_w_up_v'] = [_cat_lanes(g, l, half, 0, half * g.shape[3]) for l in range(g.shape[1])]
    wt['ffn_w_up_g'] = [_cat_lanes(g, l, half, 1, half * g.shape[3]) for l in range(g.shape[1])]
    return wt


def _pad_rows(w, rows):
    return jnp.pad(w, ((0, rows - w.shape[0]), (0, 0)))


def _fold(acc):
    return acc.sum(axis=0)


def _local_step(x, target, wt):
    bsz, s_len, d = x.shape
    t = bsz * s_len
    depth = wt['norm_mix'].shape[0]
    n_heads = d // HEAD_DIM
    f = wt['ffn_w_up_v'][0].shape[1]
    row = lambda a: a.reshape(1, -1)
    grads = {n: {} for n in WEIGHTS}
    saved = []

    xc = x.reshape(t, d)
    for i in range(depth):
        j = i // 3
        kind = i % 3
        sv = {'x_mix': xc}
        gm = row(wt['norm_mix'][i])
        if kind == 0:
            hn = _rmsnorm_fwd(xc, gm)
            p = _mm(hn, wt['conv_w_in'][j], bias=row(wt['conv_b_in'][j]), name="conv_in")
            u, sact = _conv_act_fwd(p.reshape(bsz, s_len, 2 * d), _pad_rows(wt['conv_dw'][j], 32),
                                    row(wt['conv_dw_b'][j]), row(wt['conv_ln_g'][j]), row(wt['conv_ln_b'][j]))
            sact = sact.reshape(t, d)
            xn = _mm(sact, wt['conv_w_out'][j], bias=row(wt['conv_b_out'][j]), residual=xc, out_dtype=F32,
                     name="conv_out")
            sv.update(hn=hn, p=p, u=u.reshape(t, d), sact=sact)
        elif kind == 1:
            xn, pp = _pool_fwd(xc.reshape(bsz, s_len, d), gm, wt['pool_w'][j], row(wt['pool_b'][j]),
                               row(wt['pool_scale'][j]))
            xn = xn.reshape(t, d)
            sv.update(p=pp)
        else:
            hn = _rmsnorm_fwd(xc, gm)
            wp = wt['fox_w_in'][j]
            bf = jnp.pad(wt['fox_b_f'][j], (0, LANES - n_heads)).reshape(1, LANES)
            gq = jnp.tile(wt['fox_q_gain'][j], n_heads).reshape(1, d)
            gk = jnp.tile(wt['fox_k_gain'][j], n_heads).reshape(1, d)
            proj = _mm(hn, wp, out_dtype=F32, name="fox_in")
            c = _fox_gate_fwd(proj.reshape(bsz, s_len, -1), bf, n_heads)
            crow = jnp.swapaxes(c, 1, 2)[:, :n_heads].reshape(bsz, n_heads // 2, 2, s_len)
            qn, kn, vb = _fox_qknorm_fwd(proj, gq, gk, d)
            shp = (bsz, s_len, d)
            o, lse = _flash_fwd(qn.reshape(shp), kn.reshape(shp), vb.reshape(shp), crow)
            o = o.reshape(t, d)
            xn = _mm(o, wt['fox_w_o'][j], residual=xc, out_dtype=F32, name="fox_out")
            sv.update(hn=hn, wp=wp, bf=bf, gq=gq, gk=gk, proj=proj, crow=crow, qn=qn, kn=kn, vb=vb, o=o, lse=lse)
        xc = xn
        sv['x_ffn'] = xc
        hf = _rmsnorm_fwd(xc, row(wt['norm_ffn'][i]))
        shf = (bsz, s_len, f)
        uv = _mm(hf, wt['ffn_w_up_v'][i], name="ffn_up").reshape(shf)
        ug = _mm(hf, wt['ffn_w_up_g'][i], name="ffn_up").reshape(shf)
        dw8 = _pad_rows(wt['ffn_dw'][i], 8)
        af = _ffn_act_fwd(uv, ug, dw8, row(wt['ffn_dw_b'][i])).reshape(t, f)
        xc = _mm(af, wt['ffn_w_down'][i], residual=xc, out_dtype=F32, name="ffn_down")
        sv.update(hf=hf, uv=uv, ug=ug, af=af, dw8=dw8)
        saved.append(sv)

    dx, sq = _loss_head(xc, target.reshape(t, d))

    for i in reversed(range(depth)):
        j = i // 3
        kind = i % 3
        sv = saved[i]
        shf = (bsz, s_len, f)
        da = _mm(dx, wt['ffn_w_down'][i], trans_b=True, name="ffn_down_dgrad")
        gw, _ = _wgrad(sv['af'], dx, name="ffn_down_wgrad")
        grads['ffn_w_down'][i] = gw.reshape(N_DEV, f // N_DEV, d)
        dvv, dvg, ddwv, ddwg, dbv, dbg = _ffn_act_bwd1(sv['uv'], sv['ug'], da.reshape(shf), sv['dw8'],
                                                       row(wt['ffn_dw_b'][i]))
        grads['ffn_dw'][i] = jnp.concatenate([ddwv.sum(axis=1), ddwg.sum(axis=1)], axis=1)
        grads['ffn_dw_b'][i] = jnp.concatenate([_fold(dbv), _fold(dbg)])
        duv, dug = _ffn_act_bwd2(dvv, dvg, sv['dw8'])
        duv, dug = duv.reshape(t, f), dug.reshape(t, f)
        gv, _ = _wgrad(sv['hf'], duv, name="ffn_up_wgrad")
        gg, _ = _wgrad(sv['hf'], dug, name="ffn_up_wgrad")
        grads['ffn_w_up'][i] = _split_lanes([gv, gg], 2 * f // N_DEV)
        dhf = _mm(duv, wt['ffn_w_up_v'][i], trans_b=True, a2=dug, b2=wt['ffn_w_up_g'][i], name="ffn_up_dgrad")
        dx, dg = _rmsnorm_bwd(sv['x_ffn'], row(wt['norm_ffn'][i]), dhf, dx)
        grads['norm_ffn'][i] = _fold(dg)
        gm = row(wt['norm_mix'][i])
        if kind == 0:
            dsact = _mm(dx, wt['conv_w_out'][j], trans_b=True, name="conv_out_dgrad")
            gw, cs = _wgrad(sv['sact'], dx, name="conv_out_wgrad")
            grads['conv_w_out'][j] = gw.reshape(N_DEV, d // N_DEV, d)
            grads['conv_b_out'][j] = _fold(cs)
            du, dlg, dlb, dwb = _conv_act_bwd1(sv['u'], dsact, row(wt['conv_ln_g'][j]), row(wt['conv_ln_b'][j]))
            grads['conv_ln_g'][j], grads['conv_ln_b'][j], grads['conv_dw_b'][j] = _fold(dlg), _fold(dlb), _fold(dwb)
            dp, ddw = _conv_act_bwd2(du.reshape(bsz, s_len, d), sv['p'].reshape(bsz, s_len, 2 * d),
                                     _pad_rows(wt['conv_dw'][j], 32))
            grads['conv_dw'][j] = ddw.sum(axis=1)[:wt['conv_dw'].shape[1]]
            dp = dp.reshape(t, 2 * d)
            gw, cs = _wgrad(sv['hn'], dp, name="conv_in_wgrad")
            grads['conv_w_in'][j] = _split_lanes([gw], 2 * d // N_DEV)
            grads['conv_b_in'][j] = _fold(cs)
            dhn = _mm(dp, wt['conv_w_in'][j], trans_b=True, name="conv_in_dgrad")
            dx, dg = _rmsnorm_bwd(sv['x_mix'], gm, dhn, dx)
            grads['norm_mix'][i] = _fold(dg)
        elif kind == 1:
            shp = (bsz, s_len, d)
            dxn, dwp, dbp, dsc, dg = _pool_bwd(sv['x_mix'].reshape(shp), dx.reshape(shp), sv['p'], gm, wt['pool_w'][j],
                                               row(wt['pool_b'][j]), row(wt['pool_scale'][j]))
            dx = dxn.reshape(t, d)
            ng, cg = dwp.shape[0], dwp.shape[1]
            grads['pool_w'][j] = jnp.moveaxis(dwp.reshape(ng, N_DEV, cg // N_DEV, cg), 1, 0).astype(BF16)
            grads['pool_b'][j] = _fold(dbp).reshape(wt['pool_b'].shape[1:])
            grads['pool_scale'][j] = _fold(dsc)
            grads['norm_mix'][i] = _fold(dg)
        else:
            shp = (bsz, s_len, d)
            do = _mm(dx, wt['fox_w_o'][j], trans_b=True, name="fox_out_dgrad")
            gw, _ = _wgrad(sv['o'], dx, name="fox_out_wgrad")
            grads['fox_w_o'][j] = gw.reshape(N_DEV, d // N_DEV, d)
            fl_args = (sv['qn'].reshape(shp), sv['kn'].reshape(shp), sv['vb'].reshape(shp), do.reshape(shp),
                       sv['o'].reshape(shp), sv['lse'], sv['crow'])
            dk, dv, dcrow = _flash_bwd_dkv(*fl_args)
            dq = _flash_bwd_dq(*fl_args)
            dc = jnp.swapaxes(dcrow.reshape(bsz, n_heads, s_len), 1, 2)
            dc = jnp.pad(dc, ((0, 0), (0, 0), (0, LANES - n_heads)))
            dfl, dbf = _fox_gate_bwd(dc, sv['proj'].reshape(bsz, s_len, -1), sv['bf'], n_heads)
            grads['fox_b_f'][j] = _fold(dbf)[:n_heads]
            dqkv, dgq, dgk = _fox_qknorm_bwd(sv['proj'], dq.reshape(t, d), dk.reshape(t, d), dv.reshape(t, d),
                                             sv['gq'], sv['gk'], d)
            grads['fox_q_gain'][j] = _fold(dgq).reshape(n_heads, HEAD_DIM).sum(axis=0)
            grads['fox_k_gain'][j] = _fold(dgk).reshape(n_heads, HEAD_DIM).sum(axis=0)
            dproj = jnp.concatenate([dqkv, dfl.reshape(t, LANES)], axis=1)
            dwp, _ = _wgrad(sv['hn'], dproj, name="fox_in_wgrad")
            grads['fox_w_in'][j] = _split_lanes([dwp], (3 * d + n_heads) // N_DEV)
            dhn = _mm(dproj, sv['wp'], trans_b=True, name="fox_in_dgrad")
            dx, dg = _rmsnorm_bwd(sv['x_mix'], gm, dhn, dx)
            grads['norm_mix'][i] = _fold(dg)

    listed = {n: [g[k] for k in sorted(g)] for n, g in grads.items()}
    small = {n: jnp.stack(g) for n, g in listed.items() if n not in MATRICES}
    return sq.sum(), dx.reshape(bsz, s_len, d), small, {n: listed[n] for n in MATRICES}


def _train_step(x, target, w, m, v):
    got = _gather([w[n].astype(BF16) for n in MATRICES] + [_flat_rows([w[n] for n in SMALL])])
    wt = _prepare_weights(dict(zip(MATRICES, got[:-1])), got[-1], w)
    wt.update({n: w[n] for n in REPLICATED})
    sq, grad_x, gsmall, gbig = _local_step(x, target, wt)
    d = x.shape[-1]

    shard_rows = jnp.concatenate([_to_shards(gsmall[n], SHARD_AXIS[n]) for n in SMALL], axis=1)
    rep = jnp.concatenate([gsmall[n].reshape(-1) for n in REPLICATED] + [(0.5 / d) * sq.reshape(1)])
    rows = jnp.concatenate([shard_rows, jnp.broadcast_to(rep, (N_DEV, rep.shape[0]))], axis=1)
    chunk = 8 * ADAM_COLS
    n_all = rows.shape[1]
    n_pad = -(-n_all // chunk) * chunk
    rows = jnp.pad(rows, ((0, 0), (0, n_pad - n_all))).reshape(N_DEV, n_pad // ADAM_COLS, ADAM_COLS)

    items, groups = [], []
    for n in MATRICES:
        groups.append(list(range(len(items), len(items) + len(gbig[n]))))
        items += gbig[n]
    groups.append([len(items)])
    items.append(rows)
    recv = _scatter(items, groups)

    res = [{}, {}, {}, {}]
    for n, r in zip(MATRICES, recv[:-1]):
        for k, o in enumerate(_adamw(r, w[n], m[n], v[n])):
            res[k][n] = o
    order = SMALL + REPLICATED

    def flat(tree):
        parts = jnp.concatenate([tree[n].reshape(-1) for n in order])
        return jnp.pad(parts, (0, n_pad - parts.shape[0])).reshape(n_pad // ADAM_COLS, ADAM_COLS)

    outs = [o.reshape(-1) for o in _adamw(recv[-1].reshape(rows.shape), flat(w), flat(m), flat(v))]
    off = 0
    for n in order:
        size = w[n].size
        for k in range(4):
            res[k][n] = outs[k][off:off + size].reshape(w[n].shape)
        off += size
    loss = outs[0][n_all - 1]
    return (loss, grad_x, *[res[0][n] for n in WEIGHTS], *[res[1][n] for n in WEIGHTS],
            *[res[2][n] for n in WEIGHTS], *[res[3][n] for n in WEIGHTS])


def kernel(x, norm_mix, norm_ffn, conv_w_in, conv_b_in, conv_dw, conv_dw_b, conv_ln_g, conv_ln_b, conv_w_out, conv_b_out, pool_w, pool_b, pool_scale, fox_w_in, fox_b_f, fox_q_gain, fox_k_gain, fox_w_o, ffn_w_up, ffn_dw, ffn_dw_b, ffn_w_down, loss_target, m_norm_mix, m_norm_ffn, m_conv_w_in, m_conv_b_in, m_conv_dw, m_conv_dw_b, m_conv_ln_g, m_conv_ln_b, m_conv_w_out, m_conv_b_out, m_pool_w, m_pool_b, m_pool_scale, m_fox_w_in, m_fox_b_f, m_fox_q_gain, m_fox_k_gain, m_fox_w_o, m_ffn_w_up, m_ffn_dw, m_ffn_dw_b, m_ffn_w_down, v_norm_mix, v_norm_ffn, v_conv_w_in, v_conv_b_in, v_conv_dw, v_conv_dw_b, v_conv_ln_g, v_conv_ln_b, v_conv_w_out, v_conv_b_out, v_pool_w, v_pool_b, v_pool_scale, v_fox_w_in, v_fox_b_f, v_fox_q_gain, v_fox_k_gain, v_fox_w_o, v_ffn_w_up, v_ffn_dw, v_ffn_dw_b, v_ffn_w_down):
    w = dict(zip(WEIGHTS, (norm_mix, norm_ffn, conv_w_in, conv_b_in, conv_dw, conv_dw_b, conv_ln_g, conv_ln_b, conv_w_out, conv_b_out, pool_w, pool_b, pool_scale, fox_w_in, fox_b_f, fox_q_gain, fox_k_gain, fox_w_o, ffn_w_up, ffn_dw, ffn_dw_b, ffn_w_down)))
    m = dict(zip(WEIGHTS, (m_norm_mix, m_norm_ffn, m_conv_w_in, m_conv_b_in, m_conv_dw, m_conv_dw_b, m_conv_ln_g, m_conv_ln_b, m_conv_w_out, m_conv_b_out, m_pool_w, m_pool_b, m_pool_scale, m_fox_w_in, m_fox_b_f, m_fox_q_gain, m_fox_k_gain, m_fox_w_o, m_ffn_w_up, m_ffn_dw, m_ffn_dw_b, m_ffn_w_down)))
    v = dict(zip(WEIGHTS, (v_norm_mix, v_norm_ffn, v_conv_w_in, v_conv_b_in, v_conv_dw, v_conv_dw_b, v_conv_ln_g, v_conv_ln_b, v_conv_w_out, v_conv_b_out, v_pool_w, v_pool_b, v_pool_scale, v_fox_w_in, v_fox_b_f, v_fox_q_gain, v_fox_k_gain, v_fox_w_o, v_ffn_w_up, v_ffn_dw, v_ffn_dw_b, v_ffn_w_down)))
    return _train_step(x, loss_target, w, m, v)
```

```python
import functools
import math

import jax
import jax.numpy as jnp
from jax import lax
from jax.experimental import pallas as pl
from jax.experimental.pallas import tpu as pltpu

F32, BF16 = jnp.float32, jnp.bfloat16
SDS = jax.ShapeDtypeStruct

N_DEV = 8
EPS = 1e-6
POOL_WINDOWS = (2, 4, 8, 16)
HEAD_DIM = 64
ADAM_LR, ADAM_B1, ADAM_B2, ADAM_EPS, ADAM_WD, ADAM_STEP = 0.001, 0.9, 0.999, 1e-08, 0.01, 10
LANES = 128
VMEM_LIMIT_BYTES = 48 * 1024 * 1024
NEG = -1e30

WEIGHTS = ['norm_mix', 'norm_ffn', 'conv_w_in', 'conv_b_in', 'conv_dw', 'conv_dw_b', 'conv_ln_g', 'conv_ln_b',
           'conv_w_out', 'conv_b_out', 'pool_w', 'pool_b', 'pool_scale', 'fox_w_in', 'fox_b_f', 'fox_q_gain',
           'fox_k_gain', 'fox_w_o', 'ffn_w_up', 'ffn_dw', 'ffn_dw_b', 'ffn_w_down']
SHARD_AXIS = {'conv_w_in': 2, 'conv_b_in': 1, 'conv_dw': 2, 'conv_dw_b': 1, 'conv_ln_g': 1, 'conv_ln_b': 1,
              'conv_w_out': 1, 'conv_b_out': 1, 'pool_w': 2, 'pool_b': 2, 'fox_w_in': 2, 'fox_w_o': 1,
              'ffn_w_up': 2, 'ffn_dw': 2, 'ffn_w_down': 1}
MATRICES = ('conv_w_in', 'conv_w_out', 'pool_w', 'fox_w_in', 'fox_w_o', 'ffn_w_up', 'ffn_w_down')
SHARDED = [n for n in WEIGHTS if n in SHARD_AXIS]
REPLICATED = [n for n in WEIGHTS if n not in SHARD_AXIS]


def _cp(*sem):
    return pltpu.CompilerParams(dimension_semantics=sem, vmem_limit_bytes=VMEM_LIMIT_BYTES)


def _tile(n, pref, align=8):
    if n <= pref:
        return n
    t = (pref // align) * align
    while t >= align:
        if n % t == 0:
            return t
        t -= align
    return n


def _fold8(x):
    r, c = x.shape
    return x.reshape(r // 8, 8, c).sum(axis=0)


def _sigmoid(x):
    return 0.5 * jnp.tanh(0.5 * x) + 0.5


def _shifts_back(cur, tail, n):
    hb = tail.shape[0]
    xe = jnp.concatenate([tail, cur], axis=0)
    return [cur] + [pltpu.roll(xe, j, axis=0)[hb:] for j in range(1, n)]


def _shifts_fwd(cur, head, n):
    ts = cur.shape[0]
    xe = jnp.concatenate([cur, head], axis=0)
    ln = xe.shape[0]
    return [cur] + [pltpu.roll(xe, ln - j, axis=0)[:ts] for j in range(1, n)]


def _dot_hi(a, b):
    return jnp.dot(a, b, preferred_element_type=F32, precision=lax.Precision.HIGHEST)


def _rmsnorm_fwd(x, g):
    t, d = x.shape
    tm = _tile(t, 512)

    def body(x_ref, g_ref, h_ref):
        xv = x_ref[...]
        r = lax.rsqrt(jnp.mean(xv * xv, axis=-1, keepdims=True) + EPS)
        h_ref[...] = (xv * r * g_ref[...]).astype(BF16)

    return pl.pallas_call(
        body, out_shape=SDS((t, d), BF16), grid=(t // tm,), name="rmsnorm_fwd",
        in_specs=[pl.BlockSpec((tm, d), lambda i: (i, 0)), pl.BlockSpec((1, d), lambda i: (0, 0))],
        out_specs=pl.BlockSpec((tm, d), lambda i: (i, 0)), compiler_params=_cp("parallel"))(x, g)


def _rmsnorm_bwd(x, g, dh, dres):
    t, d = x.shape
    tm = _tile(t, 512)

    def body(x_ref, g_ref, dh_ref, dres_ref, dx_ref, dg_ref):
        @pl.when(pl.program_id(0) == 0)
        def _():
            dg_ref[...] = jnp.zeros_like(dg_ref)

        xv = x_ref[...]
        r = lax.rsqrt(jnp.mean(xv * xv, axis=-1, keepdims=True) + EPS)
        xh = xv * r
        dhv = dh_ref[...].astype(F32)
        u = dhv * g_ref[...]
        dx_ref[...] = dres_ref[...] + r * (u - xh * jnp.mean(u * xh, axis=-1, keepdims=True))
        dg_ref[...] += _fold8(dhv * xh)

    row = pl.BlockSpec((tm, d), lambda i: (i, 0))
    return pl.pallas_call(
        body, out_shape=(SDS((t, d), F32), SDS((8, d), F32)), grid=(t // tm,), name="rmsnorm_bwd",
        in_specs=[row, pl.BlockSpec((1, d), lambda i: (0, 0)), row, row],
        out_specs=(row, pl.BlockSpec((8, d), lambda i: (0, 0))), compiler_params=_cp("arbitrary"))(x, g, dh, dres)


def _mm(a, b, *, trans_b=False, bias=None, residual=None, a2=None, b2=None, out_dtype=BF16, name="mm"):
    m, k = a.shape
    n = b.shape[0] if trans_b else b.shape[1]
    tm, tn, tk = _tile(m, 512, 16), _tile(n, 1536, LANES), _tile(k, 1536, LANES)
    nk = k // tk
    two = a2 is not None
    steps = 2 * nk if two else nk
    dims = (((1,), (1,)), ((), ())) if trans_b else (((1,), (0,)), ((), ()))
    has_bias, has_res = bias is not None, residual is not None

    def body(*refs):
        n_in = 4 if two else 2
        bias_ref = refs[n_in] if has_bias else None
        res_ref = refs[n_in + has_bias] if has_res else None

        def finish(r):
            if has_bias:
                r = r + bias_ref[...]
            if has_res:
                r = r + res_ref[...]
            return r.astype(out_dtype)

        def dot(a_ref, b_ref):
            return lax.dot_general(a_ref[...].astype(BF16), b_ref[...].astype(BF16), dims, preferred_element_type=F32)

        if steps == 1:
            refs[-1][...] = finish(dot(refs[0], refs[1]))
            return
        o_ref, acc_ref = refs[-2], refs[-1]
        kk = pl.program_id(2)

        @pl.when(kk == 0)
        def _():
            acc_ref[...] = jnp.zeros_like(acc_ref)

        @pl.when(kk < nk)
        def _():
            acc_ref[...] += dot(refs[0], refs[1])

        if two:
            @pl.when(kk >= nk)
            def _():
                acc_ref[...] += dot(refs[2], refs[3])

        @pl.when(kk == steps - 1)
        def _():
            o_ref[...] = finish(acc_ref[...])

    def pair(first):
        kmap = (lambda kk: jnp.minimum(kk, nk - 1)) if first else (lambda kk: jnp.maximum(kk - nk, 0))
        a_spec = pl.BlockSpec((tm, tk), lambda i, j, kk: (i, kmap(kk)))
        if trans_b:
            b_spec = pl.BlockSpec((tn, tk), lambda i, j, kk: (j, kmap(kk)))
        else:
            b_spec = pl.BlockSpec((tk, tn), lambda i, j, kk: (kmap(kk), j))
        return [a_spec, b_spec]

    in_specs, args = pair(True), [a, b]
    if two:
        in_specs += pair(False)
        args += [a2, b2]
    if has_bias:
        in_specs.append(pl.BlockSpec((1, tn), lambda i, j, kk: (0, j)))
        args.append(bias)
    if has_res:
        in_specs.append(pl.BlockSpec((tm, tn), lambda i, j, kk: (i, j)))
        args.append(residual)
    return pl.pallas_call(
        body, out_shape=SDS((m, n), out_dtype), grid=(m // tm, n // tn, steps), name=name,
        in_specs=in_specs, out_specs=pl.BlockSpec((tm, tn), lambda i, j, kk: (i, j)),
        scratch_shapes=[] if steps == 1 else [pltpu.VMEM((tm, tn), F32)],
        compiler_params=_cp("parallel", "parallel", "arbitrary"))(*args)


def _wgrad(a, g, *, out_dtype=BF16, name="wgrad"):
    m, ka = a.shape
    n = g.shape[1]
    ta, tn, tm = _tile(ka, 1536, LANES), _tile(n, 1536, LANES), _tile(m, 1024)
    nm = m // tm

    def body(a_ref, g_ref, o_ref, cs_ref, acc_ref):
        i, mm = pl.program_id(1), pl.program_id(2)

        @pl.when(mm == 0)
        def _():
            acc_ref[...] = jnp.zeros_like(acc_ref)

        @pl.when((mm == 0) & (i == 0))
        def _():
            cs_ref[...] = jnp.zeros_like(cs_ref)

        gv = g_ref[...]
        acc_ref[...] += lax.dot_general(a_ref[...].astype(BF16), gv.astype(BF16), (((0,), (0,)), ((), ())),
                                        preferred_element_type=F32)

        @pl.when(i == 0)
        def _():
            cs_ref[...] += _fold8(gv.astype(F32))

        @pl.when(mm == nm - 1)
        def _():
            o_ref[...] = acc_ref[...].astype(out_dtype)

    return pl.pallas_call(
        body, out_shape=(SDS((ka, n), out_dtype), SDS((8, n), F32)), grid=(n // tn, ka // ta, nm), name=name,
        in_specs=[pl.BlockSpec((tm, ta), lambda j, i, mm: (mm, i)), pl.BlockSpec((tm, tn), lambda j, i, mm: (mm, j))],
        out_specs=(pl.BlockSpec((ta, tn), lambda j, i, mm: (i, j)), pl.BlockSpec((8, tn), lambda j, i, mm: (0, j))),
        scratch_shapes=[pltpu.VMEM((ta, tn), F32)],
        compiler_params=_cp("arbitrary", "arbitrary", "arbitrary"))(a, g)


FFN_HALO = 16


def _ffn_conv(uc_ref, up_ref, w_ref, b_ref, s):
    u = uc_ref[...].astype(F32)
    tail = jnp.where(s > 0, up_ref[...].astype(F32), 0.0)
    sh = _shifts_back(u, tail, 3)
    return sh, sh[2] * w_ref[0:1, :] + sh[1] * w_ref[1:2, :] + sh[0] * w_ref[2:3, :] + b_ref[...]


def _ffn_act_fwd(uv, ug, dw8, b):
    bsz, s_len, f = uv.shape
    tc, ts = _tile(f, 256, LANES), _tile(s_len, 1024, FFN_HALO)
    nf, r = f // tc, ts // FFN_HALO

    def body(uv_ref, uvp_ref, ug_ref, ugp_ref, wv_ref, wg_ref, bv_ref, bg_ref, a_ref):
        s = pl.program_id(2)
        _, val = _ffn_conv(uv_ref, uvp_ref, wv_ref, bv_ref, s)
        _, gate = _ffn_conv(ug_ref, ugp_ref, wg_ref, bg_ref, s)
        a_ref[...] = (gate * _sigmoid(gate) * val).astype(BF16)

    cur = pl.BlockSpec((None, ts, tc), lambda bi, j, s: (bi, s, j))
    prev = pl.BlockSpec((None, FFN_HALO, tc), lambda bi, j, s: (bi, jnp.maximum(s * r - 1, 0), j))

    def par(rows, off):
        return pl.BlockSpec((rows, tc), lambda bi, j, s: (0, j + off))

    return pl.pallas_call(
        body, out_shape=SDS((bsz, s_len, f), BF16), grid=(bsz, nf, s_len // ts), name="ffn_act_fwd",
        in_specs=[cur, prev, cur, prev, par(8, 0), par(8, nf), par(1, 0), par(1, nf)], out_specs=cur,
        compiler_params=_cp("parallel", "parallel", "arbitrary"))(uv, uv, ug, ug, dw8, dw8, b, b)


def _ffn_act_bwd1(uv, ug, da, dw8, b):
    bsz, s_len, f = uv.shape
    tc, ts = _tile(f, 256, LANES), _tile(s_len, 1024, FFN_HALO)
    nf, r = f // tc, ts // FFN_HALO

    def body(uv_ref, uvp_ref, ug_ref, ugp_ref, da_ref, wv_ref, wg_ref, bv_ref, bg_ref,
             dvv_ref, dvg_ref, ddwv_ref, ddwg_ref, dbv_ref, dbg_ref):
        bi, s = pl.program_id(1), pl.program_id(2)

        @pl.when((bi == 0) & (s == 0))
        def _():
            for ref in (ddwv_ref, ddwg_ref, dbv_ref, dbg_ref):
                ref[...] = jnp.zeros_like(ref)

        shv, val = _ffn_conv(uv_ref, uvp_ref, wv_ref, bv_ref, s)
        shg, gate = _ffn_conv(ug_ref, ugp_ref, wg_ref, bg_ref, s)
        sg = _sigmoid(gate)
        dav = da_ref[...].astype(F32)
        for dv, sh, dv_ref, ddw_ref, db_ref in (
                (dav * gate * sg, shv, dvv_ref, ddwv_ref, dbv_ref),
                (dav * val * (sg * (1.0 + gate * (1.0 - sg))), shg, dvg_ref, ddwg_ref, dbg_ref)):
            dv_ref[...] = dv.astype(BF16)
            db_ref[...] += _fold8(dv)
            for k in range(3):
                ddw_ref[k] += _fold8(dv * sh[2 - k])

    cur = pl.BlockSpec((None, ts, tc), lambda j, bi, s: (bi, s, j))
    prev = pl.BlockSpec((None, FFN_HALO, tc), lambda j, bi, s: (bi, jnp.maximum(s * r - 1, 0), j))

    def par(rows, off):
        return pl.BlockSpec((rows, tc), lambda j, bi, s: (0, j + off))

    acc3 = pl.BlockSpec((3, 8, tc), lambda j, bi, s: (0, 0, j))
    acc1 = pl.BlockSpec((8, tc), lambda j, bi, s: (0, j))
    return pl.pallas_call(
        body, out_shape=(SDS((bsz, s_len, f), BF16), SDS((bsz, s_len, f), BF16), SDS((3, 8, f), F32),
                         SDS((3, 8, f), F32), SDS((8, f), F32), SDS((8, f), F32)),
        grid=(nf, bsz, s_len // ts), name="ffn_act_bwd1",
        in_specs=[cur, prev, cur, prev, cur, par(8, 0), par(8, nf), par(1, 0), par(1, nf)],
        out_specs=(cur, cur, acc3, acc3, acc1, acc1),
        compiler_params=_cp("arbitrary", "arbitrary", "arbitrary"))(uv, uv, ug, ug, da, dw8, dw8, b, b)


def _ffn_act_bwd2(dvv, dvg, dw8):
    bsz, s_len, f = dvv.shape
    tc, ts = _tile(f, 256, LANES), _tile(s_len, 1024, FFN_HALO)
    nf, r, ns = f // tc, ts // FFN_HALO, s_len // ts

    def body(vc_ref, vn_ref, gc_ref, gn_ref, wv_ref, wg_ref, duv_ref, dug_ref):
        s = pl.program_id(2)
        for dc_ref, dn_ref, w_ref, du_ref in ((vc_ref, vn_ref, wv_ref, duv_ref), (gc_ref, gn_ref, wg_ref, dug_ref)):
            d = dc_ref[...].astype(F32)
            head = jnp.where(s < ns - 1, dn_ref[...].astype(F32), 0.0)
            sh = _shifts_fwd(d, head, 3)
            du_ref[...] = (sh[0] * w_ref[2:3, :] + sh[1] * w_ref[1:2, :] + sh[2] * w_ref[0:1, :]).astype(BF16)

    cur = pl.BlockSpec((None, ts, tc), lambda bi, j, s: (bi, s, j))
    nxt = pl.BlockSpec((None, FFN_HALO, tc),
                       lambda bi, j, s: (bi, jnp.minimum((s + 1) * r, s_len // FFN_HALO - 1), j))

    def par(off):
        return pl.BlockSpec((8, tc), lambda bi, j, s: (0, j + off))

    return pl.pallas_call(
        body, out_shape=(SDS((bsz, s_len, f), BF16),) * 2, grid=(bsz, nf, ns), name="ffn_act_bwd2",
        in_specs=[cur, nxt, cur, nxt, par(0), par(nf)], out_specs=(cur, cur),
        compiler_params=_cp("parallel", "parallel", "arbitrary"))(dvv, dvv, dvg, dvg, dw8, dw8)


CONV_HALO = 32
CONV_CHUNK = 256


def _conv_act_fwd(p, dw32, dwb, ln_g, ln_b):
    bsz, s_len, d2 = p.shape
    d = d2 // 2
    kw = 31
    ts = _tile(s_len, 256, CONV_HALO)
    r = ts // CONV_HALO
    cc = min(CONV_CHUNK, d)

    def body(pc_ref, pp_ref, w_ref, wb_ref, g_ref, b_ref, u_ref, s_ref):
        s = pl.program_id(1)
        tot = jnp.zeros((ts, 1), F32)
        for c0 in range(0, d, cc):
            a = pc_ref[:, c0:c0 + cc].astype(F32)
            g = pc_ref[:, d + c0:d + c0 + cc].astype(F32)
            z = a * _sigmoid(g)
            ap = pp_ref[:, c0:c0 + cc].astype(F32)
            gp = pp_ref[:, d + c0:d + c0 + cc].astype(F32)
            tail = jnp.where(s > 0, ap * _sigmoid(gp), 0.0)
            sh = _shifts_back(z, tail, kw)
            acc = wb_ref[:, c0:c0 + cc] + sh[0] * w_ref[kw - 1:kw, c0:c0 + cc]
            for j in range(1, kw):
                acc = acc + sh[j] * w_ref[kw - 1 - j:kw - j, c0:c0 + cc]
            u_ref[:, c0:c0 + cc] = acc
            tot = tot + jnp.sum(acc, axis=-1, keepdims=True)
        u = u_ref[...]
        mu = tot / d
        uc = u - mu
        var = jnp.mean(uc * uc, axis=-1, keepdims=True)
        ul = uc * lax.rsqrt(var + EPS) * g_ref[...] + b_ref[...]
        s_ref[...] = (ul * _sigmoid(ul)).astype(BF16)

    vec = pl.BlockSpec((1, d), lambda bi, s: (0, 0))
    return pl.pallas_call(
        body, out_shape=(SDS((bsz, s_len, d), F32), SDS((bsz, s_len, d), BF16)), grid=(bsz, s_len // ts),
        name="conv_act_fwd",
        in_specs=[pl.BlockSpec((None, ts, d2), lambda bi, s: (bi, s, 0)),
                  pl.BlockSpec((None, CONV_HALO, d2), lambda bi, s: (bi, jnp.maximum(s * r - 1, 0), 0)),
                  pl.BlockSpec((32, d), lambda bi, s: (0, 0)), vec, vec, vec],
        out_specs=(pl.BlockSpec((None, ts, d), lambda bi, s: (bi, s, 0)),
                   pl.BlockSpec((None, ts, d), lambda bi, s: (bi, s, 0))),
        compiler_params=_cp("parallel", "arbitrary"))(p, p, dw32, dwb, ln_g, ln_b)


def _conv_act_bwd1(u, ds, ln_g, ln_b):
    t, d = u.shape
    ts = _tile(t, 256)

    def body(u_ref, ds_ref, g_ref, b_ref, du_ref, dg_ref, db_ref, dwb_ref):
        @pl.when(pl.program_id(0) == 0)
        def _():
            dg_ref[...] = jnp.zeros_like(dg_ref)
            db_ref[...] = jnp.zeros_like(db_ref)
            dwb_ref[...] = jnp.zeros_like(dwb_ref)

        uv = u_ref[...]
        uc = uv - jnp.mean(uv, axis=-1, keepdims=True)
        rstd = lax.rsqrt(jnp.mean(uc * uc, axis=-1, keepdims=True) + EPS)
        uh = uc * rstd
        ul = uh * g_ref[...] + b_ref[...]
        sg = _sigmoid(ul)
        dul = ds_ref[...].astype(F32) * (sg * (1.0 + ul * (1.0 - sg)))
        duh = dul * g_ref[...]
        du = rstd * (duh - jnp.mean(duh, axis=-1, keepdims=True) - uh * jnp.mean(duh * uh, axis=-1, keepdims=True))
        du_ref[...] = du
        dg_ref[...] += _fold8(dul * uh)
        db_ref[...] += _fold8(dul)
        dwb_ref[...] += _fold8(du)

    row = pl.BlockSpec((ts, d), lambda i: (i, 0))
    vec = pl.BlockSpec((1, d), lambda i: (0, 0))
    acc = pl.BlockSpec((8, d), lambda i: (0, 0))
    return pl.pallas_call(
        body, out_shape=(SDS((t, d), F32), SDS((8, d), F32), SDS((8, d), F32), SDS((8, d), F32)), grid=(t // ts,),
        name="conv_act_bwd1", in_specs=[row, row, vec, vec], out_specs=(row, acc, acc, acc),
        compiler_params=_cp("arbitrary"))(u, ds, ln_g, ln_b)


def _conv_act_bwd2(du, p, dw32):
    bsz, s_len, d2 = p.shape
    d = d2 // 2
    kw = 31
    ts = _tile(s_len, 256, CONV_HALO)
    r, ns = ts // CONV_HALO, s_len // ts
    cc = min(CONV_CHUNK, d)

    def body(dc_ref, dn_ref, pc_ref, pp_ref, w_ref, dp_ref, ddw_ref):
        bi, s = pl.program_id(0), pl.program_id(1)

        @pl.when((bi == 0) & (s == 0))
        def _():
            ddw_ref[...] = jnp.zeros_like(ddw_ref)

        for c0 in range(0, d, cc):
            a = pc_ref[:, c0:c0 + cc].astype(F32)
            g = pc_ref[:, d + c0:d + c0 + cc].astype(F32)
            sg = _sigmoid(g)
            z = a * sg
            ap = pp_ref[:, c0:c0 + cc].astype(F32)
            gp = pp_ref[:, d + c0:d + c0 + cc].astype(F32)
            tail = jnp.where(s > 0, ap * _sigmoid(gp), 0.0)
            duv = dc_ref[:, c0:c0 + cc]
            head = jnp.where(s < ns - 1, dn_ref[:, c0:c0 + cc], 0.0)
            zb = _shifts_back(z, tail, kw)
            for k in range(kw):
                ddw_ref[k, :, c0:c0 + cc] += _fold8(duv * zb[kw - 1 - k])
            df = _shifts_fwd(duv, head, kw)
            dz = df[0] * w_ref[kw - 1:kw, c0:c0 + cc]
            for j in range(1, kw):
                dz = dz + df[j] * w_ref[kw - 1 - j:kw - j, c0:c0 + cc]
            dp_ref[:, c0:c0 + cc] = (dz * sg).astype(BF16)
            dp_ref[:, d + c0:d + c0 + cc] = (dz * a * sg * (1.0 - sg)).astype(BF16)

    return pl.pallas_call(
        body, out_shape=(SDS((bsz, s_len, d2), BF16), SDS((32, 8, d), F32)), grid=(bsz, ns), name="conv_act_bwd2",
        in_specs=[pl.BlockSpec((None, ts, d), lambda bi, s: (bi, s, 0)),
                  pl.BlockSpec((None, CONV_HALO, d),
                               lambda bi, s: (bi, jnp.minimum((s + 1) * r, s_len // CONV_HALO - 1), 0)),
                  pl.BlockSpec((None, ts, d2), lambda bi, s: (bi, s, 0)),
                  pl.BlockSpec((None, CONV_HALO, d2), lambda bi, s: (bi, jnp.maximum(s * r - 1, 0), 0)),
                  pl.BlockSpec((32, d), lambda bi, s: (0, 0))],
        out_specs=(pl.BlockSpec((None, ts, d2), lambda bi, s: (bi, s, 0)),
                   pl.BlockSpec((32, 8, d), lambda bi, s: (0, 0, 0))),
        compiler_params=_cp("arbitrary", "arbitrary"))(du, du, p, p, dw32)


POOL_HALO = 16


def _pool_counts(s, ts, rows, w):
    t = s * ts + lax.broadcasted_iota(jnp.int32, (rows, 1), 0)
    return jnp.minimum(t + 1, w).astype(F32)


def _pool_fwd(x, gmix, w, b, scale):
    bsz, s_len, d = x.shape
    ng = len(POOL_WINDOWS)
    cg = d // ng
    ts = _tile(s_len, 512, POOL_HALO)
    r = ts // POOL_HALO

    def body(xc_ref, xp_ref, g_ref, w_ref, b_ref, sc_ref, y_ref, p_ref):
        s = pl.program_id(1)

        def norm(v):
            return v * lax.rsqrt(jnp.mean(v * v, axis=-1, keepdims=True) + EPS) * g_ref[...]

        xc = xc_ref[...]
        h = norm(xc)
        tail = jnp.where(s > 0, norm(xp_ref[...]), 0.0)
        for gi, win in enumerate(POOL_WINDOWS):
            lo, hi = gi * cg, (gi + 1) * cg
            hg = h[:, lo:hi]
            acc = jnp.concatenate([tail[:, lo:hi], hg], axis=0)
            step = 1
            while step < win:
                acc = acc + pltpu.roll(acc, step, axis=0)
                step *= 2
            pg = acc[POOL_HALO:] / _pool_counts(s, ts, ts, win) - hg
            pb = pg.astype(BF16)
            p_ref[:, lo:hi] = pb
            yg = jnp.dot(pb, w_ref[gi], preferred_element_type=F32) + b_ref[:, lo:hi]
            y_ref[:, lo:hi] = xc[:, lo:hi] + yg * sc_ref[:, lo:hi]

    vec = pl.BlockSpec((1, d), lambda bi, s: (0, 0))
    blk = pl.BlockSpec((None, ts, d), lambda bi, s: (bi, s, 0))
    return pl.pallas_call(
        body, out_shape=(SDS((bsz, s_len, d), F32), SDS((bsz, s_len, d), BF16)), grid=(bsz, s_len // ts),
        name="pool_fwd",
        in_specs=[blk, pl.BlockSpec((None, POOL_HALO, d), lambda bi, s: (bi, jnp.maximum(s * r - 1, 0), 0)),
                  vec, pl.BlockSpec((ng, cg, cg), lambda bi, s: (0, 0, 0)), vec, vec],
        out_specs=(blk, blk), compiler_params=_cp("parallel", "arbitrary"))(x, x, gmix, w, b, scale)


def _pool_bwd(x, dy, p, gmix, w, b, scale):
    bsz, s_len, d = x.shape
    ng = len(POOL_WINDOWS)
    cg = d // ng
    ts = _tile(s_len, 512, POOL_HALO)
    r, ns = ts // POOL_HALO, s_len // ts
    nt = (((1,), (1,)), ((), ()))
    tn = (((0,), (0,)), ((), ()))

    def body(x_ref, dy_ref, dyn_ref, p_ref, g_ref, w_ref, b_ref, sc_ref, dx_ref, dw_ref, db_ref, dsc_ref, dg_ref):
        bi, s = pl.program_id(0), pl.program_id(1)

        @pl.when((bi == 0) & (s == 0))
        def _():
            dw_ref[...] = jnp.zeros_like(dw_ref)
            db_ref[...] = jnp.zeros_like(db_ref)
            dsc_ref[...] = jnp.zeros_like(dsc_ref)
            dg_ref[...] = jnp.zeros_like(dg_ref)

        dy = dy_ref[...]
        dyy = dy * sc_ref[...]
        dyy_n = jnp.where(s < ns - 1, dyn_ref[...] * sc_ref[...], 0.0)
        db_ref[...] += _fold8(dyy)
        xv = x_ref[...]
        rr = lax.rsqrt(jnp.mean(xv * xv, axis=-1, keepdims=True) + EPS)
        xh = xv * rr
        for gi, win in enumerate(POOL_WINDOWS):
            lo, hi = gi * cg, (gi + 1) * cg
            pb = p_ref[:, lo:hi]
            wg = w_ref[gi]
            pre = jnp.dot(pb, wg, preferred_element_type=F32) + b_ref[:, lo:hi]
            dsc_ref[:, lo:hi] += _fold8(dy[:, lo:hi] * pre)
            dyb = dyy[:, lo:hi].astype(BF16)
            dw_ref[gi] += lax.dot_general(pb, dyb, tn, preferred_element_type=F32)
            dp = lax.dot_general(dyb, wg, nt, preferred_element_type=F32)
            dp_n = lax.dot_general(dyy_n[:, lo:hi].astype(BF16), wg, nt, preferred_element_type=F32)
            q = dp / _pool_counts(s, ts, ts, win)
            q_n = dp_n / _pool_counts(s + 1, ts, POOL_HALO, win)
            acc = jnp.concatenate([q, q_n], axis=0)
            ln = ts + POOL_HALO
            step = 1
            while step < win:
                acc = acc + pltpu.roll(acc, ln - step, axis=0)
                step *= 2
            dh = acc[:ts] - dp
            xhg = xh[:, lo:hi]
            dg_ref[:, lo:hi] += _fold8(dh * xhg)
            dx_ref[:, lo:hi] = dh * g_ref[:, lo:hi]
        u = dx_ref[...]
        dx_ref[...] = dy + rr * (u - xh * jnp.mean(u * xh, axis=-1, keepdims=True))

    vec = pl.BlockSpec((1, d), lambda bi, s: (0, 0))
    acc8 = pl.BlockSpec((8, d), lambda bi, s: (0, 0))
    blk = pl.BlockSpec((None, ts, d), lambda bi, s: (bi, s, 0))
    wspec = pl.BlockSpec((ng, cg, cg), lambda bi, s: (0, 0, 0))
    return pl.pallas_call(
        body, out_shape=(SDS((bsz, s_len, d), F32), SDS((ng, cg, cg), F32), SDS((8, d), F32), SDS((8, d), F32),
                         SDS((8, d), F32)),
        grid=(bsz, ns), name="pool_bwd",
        in_specs=[blk, blk,
                  pl.BlockSpec((None, POOL_HALO, d),
                               lambda bi, s: (bi, jnp.minimum((s + 1) * r, s_len // POOL_HALO - 1), 0)),
                  blk, vec, wspec, vec, vec],
        out_specs=(blk, wspec, acc8, acc8, acc8),
        compiler_params=_cp("arbitrary", "arbitrary"))(x, dy, dy, p, gmix, w, b, scale)


def _tri(n, upper):
    row = lax.broadcasted_iota(jnp.int32, (n, n), 0)
    col = lax.broadcasted_iota(jnp.int32, (n, n), 1)
    return jnp.where((col >= row) if upper else (col <= row), 1.0, 0.0).astype(F32)


def _fox_gate_fwd(proj, bf, n_heads):
    bsz, s_len, width = proj.shape
    col = width // LANES - 1
    ts = _tile(s_len, 512)

    def body(fl_ref, b_ref, c_ref, carry_ref):
        @pl.when(pl.program_id(1) == 0)
        def _():
            carry_ref[...] = jnp.zeros_like(carry_ref)

        xv = fl_ref[...] + b_ref[...]
        logf = jnp.minimum(xv, 0.0) - jnp.log(1.0 + jnp.exp(-jnp.abs(xv)))
        lane = lax.broadcasted_iota(jnp.int32, (1, LANES), 1)
        logf = jnp.where(lane < n_heads, logf, 0.0)
        c = _dot_hi(_tri(ts, False), logf) + carry_ref[0:1, :]
        c_ref[...] = c
        carry_ref[0:1, :] = c[ts - 1:ts, :]

    return pl.pallas_call(
        body, out_shape=SDS((bsz, s_len, LANES), F32), grid=(bsz, s_len // ts), name="fox_gate_fwd",
        in_specs=[pl.BlockSpec((None, ts, LANES), lambda bi, s: (bi, s, col)),
                  pl.BlockSpec((1, LANES), lambda bi, s: (0, 0))],
        out_specs=pl.BlockSpec((None, ts, LANES), lambda bi, s: (bi, s, 0)),
        scratch_shapes=[pltpu.VMEM((8, LANES), F32)],
        compiler_params=_cp("arbitrary", "arbitrary"))(proj, bf)


def _fox_gate_bwd(dc, proj, bf, n_heads):
    bsz, s_len, width = proj.shape
    col = width // LANES - 1
    ts = _tile(s_len, 512)
    ns = s_len // ts

    def body(dc_ref, fl_ref, b_ref, dfl_ref, db_ref, carry_ref):
        bi, s = pl.program_id(0), pl.program_id(1)

        @pl.when((bi == 0) & (s == 0))
        def _():
            db_ref[...] = jnp.zeros_like(db_ref)

        @pl.when(s == 0)
        def _():
            carry_ref[...] = jnp.zeros_like(carry_ref)

        dlogf = _dot_hi(_tri(ts, True), dc_ref[...]) + carry_ref[0:1, :]
        carry_ref[0:1, :] = dlogf[0:1, :]
        lane = lax.broadcasted_iota(jnp.int32, (1, LANES), 1)
        dfl = jnp.where(lane < n_heads, dlogf * (1.0 - _sigmoid(fl_ref[...] + b_ref[...])), 0.0)
        dfl_ref[...] = dfl.astype(BF16)
        db_ref[...] += _fold8(dfl)

    return pl.pallas_call(
        body, out_shape=(SDS((bsz, s_len, LANES), BF16), SDS((8, LANES), F32)), grid=(bsz, ns), name="fox_gate_bwd",
        in_specs=[pl.BlockSpec((None, ts, LANES), lambda bi, s: (bi, ns - 1 - s, 0)),
                  pl.BlockSpec((None, ts, LANES), lambda bi, s: (bi, ns - 1 - s, col)),
                  pl.BlockSpec((1, LANES), lambda bi, s: (0, 0))],
        out_specs=(pl.BlockSpec((None, ts, LANES), lambda bi, s: (bi, ns - 1 - s, 0)),
                   pl.BlockSpec((8, LANES), lambda bi, s: (0, 0))),
        scratch_shapes=[pltpu.VMEM((8, LANES), F32)],
        compiler_params=_cp("arbitrary", "arbitrary"))(dc, proj, bf)


def _head_maps(d):
    ch = lax.broadcasted_iota(jnp.int32, (d, LANES), 0) // HEAD_DIM
    hd = lax.broadcasted_iota(jnp.int32, (d, LANES), 1)
    e = jnp.where(ch == hd, 1.0, 0.0).astype(F32)
    cht = lax.broadcasted_iota(jnp.int32, (LANES, d), 1) // HEAD_DIM
    hdt = lax.broadcasted_iota(jnp.int32, (LANES, d), 0)
    et = jnp.where(cht == hdt, 1.0, 0.0).astype(F32)
    return e, et


def _fox_qknorm_fwd(proj, gq, gk, d):
    t = proj.shape[0]
    ts = _tile(t, 256)
    scale = 1.0 / math.sqrt(HEAD_DIM)

    def body(q_ref, k_ref, v_ref, gq_ref, gk_ref, qn_ref, kn_ref, vb_ref):
        e, et = _head_maps(d)

        def norm(v, g):
            r = lax.rsqrt(_dot_hi(v * v, e) / HEAD_DIM + EPS)
            return v * _dot_hi(r, et) * g

        qn_ref[...] = (norm(q_ref[...], gq_ref[...]) * scale).astype(BF16)
        kn_ref[...] = norm(k_ref[...], gk_ref[...]).astype(BF16)
        vb_ref[...] = v_ref[...].astype(BF16)

    def colblk(j):
        return pl.BlockSpec((ts, d), lambda i: (i, j))

    vec = pl.BlockSpec((1, d), lambda i: (0, 0))
    out = pl.BlockSpec((ts, d), lambda i: (i, 0))
    return pl.pallas_call(
        body, out_shape=(SDS((t, d), BF16),) * 3, grid=(t // ts,), name="fox_qknorm_fwd",
        in_specs=[colblk(0), colblk(1), colblk(2), vec, vec], out_specs=(out, out, out),
        compiler_params=_cp("parallel"))(proj, proj, proj, gq, gk)


def _fox_qknorm_bwd(proj, dq, dk, dv, gq, gk, d):
    t = proj.shape[0]
    ts = _tile(t, 256)
    scale = 1.0 / math.sqrt(HEAD_DIM)

    def body(q_ref, k_ref, dq_ref, dk_ref, dv_ref, gq_ref, gk_ref, dp_ref, dgq_ref, dgk_ref):
        @pl.when(pl.program_id(0) == 0)
        def _():
            dgq_ref[...] = jnp.zeros_like(dgq_ref)
            dgk_ref[...] = jnp.zeros_like(dgk_ref)

        e, et = _head_maps(d)

        def back(v, g, dn, dg_ref):
            r = _dot_hi(lax.rsqrt(_dot_hi(v * v, e) / HEAD_DIM + EPS), et)
            vh = v * r
            dg_ref[...] += _fold8(dn * vh)
            u = dn * g
            mh = _dot_hi(_dot_hi(u * vh, e) / HEAD_DIM, et)
            return r * (u - vh * mh)

        dp_ref[:, 0:d] = back(q_ref[...], gq_ref[...], dq_ref[...] * scale, dgq_ref).astype(BF16)
        dp_ref[:, d:2 * d] = back(k_ref[...], gk_ref[...], dk_ref[...], dgk_ref).astype(BF16)
        dp_ref[:, 2 * d:3 * d] = dv_ref[...]

    def colblk(j):
        return pl.BlockSpec((ts, d), lambda i: (i, j))

    row = pl.BlockSpec((ts, d), lambda i: (i, 0))
    vec = pl.BlockSpec((1, d), lambda i: (0, 0))
    acc = pl.BlockSpec((8, d), lambda i: (0, 0))
    return pl.pallas_call(
        body, out_shape=(SDS((t, 3 * d), BF16), SDS((8, d), F32), SDS((8, d), F32)), grid=(t // ts,),
        name="fox_qknorm_bwd", in_specs=[colblk(0), colblk(1), row, row, row, vec, vec],
        out_specs=(pl.BlockSpec((ts, 3 * d), lambda i: (i, 0)), acc, acc),
        compiler_params=_cp("arbitrary"))(proj, proj, dq, dk, dv, gq, gk)


ATT_BLOCK = 512
_NT = (((1,), (1,)), ((), ()))
_TN = (((0,), (0,)), ((), ()))


def _head_mask(h):
    return (lax.broadcasted_iota(jnp.int32, (1, LANES), 1) // HEAD_DIM) == h


def _causal(qi, ki, tq, tk):
    row = qi * tq + lax.broadcasted_iota(jnp.int32, (tq, 1), 0)
    col = ki * tk + lax.broadcasted_iota(jnp.int32, (1, tk), 1)
    return col <= row


def _flash_fwd(q, k, v, crow):
    bsz, s_len, d = q.shape
    nj = d // LANES
    tq = tk = _tile(s_len, ATT_BLOCK, LANES)
    nq = s_len // tq

    def body(q_ref, k_ref, v_ref, c_ref, o_ref, lse_ref, m_ref, l_ref, acc_ref):
        qi, ki = pl.program_id(2), pl.program_id(3)

        @pl.when(ki == 0)
        def _():
            m_ref[...] = jnp.full_like(m_ref, NEG)
            l_ref[...] = jnp.zeros_like(l_ref)
            acc_ref[...] = jnp.zeros_like(acc_ref)

        @pl.when(ki <= qi)
        def _():
            qv, kv, vv = q_ref[...], k_ref[...], v_ref[...]
            mask = _causal(qi, ki, tq, tk)
            for h in range(2):
                qh = jnp.where(_head_mask(h), qv, jnp.zeros_like(qv))
                s = lax.dot_general(qh, kv, _NT, preferred_element_type=F32) - c_ref[h:h + 1, :]
                s = jnp.where(mask, s, NEG)
                m_prev = m_ref[h]
                m_new = jnp.maximum(m_prev, jnp.max(s, axis=1, keepdims=True))
                pm = jnp.exp(s - m_new)
                alpha = jnp.exp(m_prev - m_new)
                l_ref[h] = alpha * l_ref[h] + jnp.sum(pm, axis=1, keepdims=True)
                p_hi = pm.astype(BF16)
                p_lo = (pm - p_hi.astype(F32)).astype(BF16)
                acc_ref[h] = (alpha * acc_ref[h] + jnp.dot(p_hi, vv, preferred_element_type=F32)
                              + jnp.dot(p_lo, vv, preferred_element_type=F32))
                m_ref[h] = m_new

        @pl.when(ki == qi)
        def _():
            m0 = _head_mask(0)
            o_ref[...] = jnp.where(m0, acc_ref[0] / l_ref[0], acc_ref[1] / l_ref[1])
            lse_ref[...] = jnp.where(m0, m_ref[0] + jnp.log(l_ref[0]), m_ref[1] + jnp.log(l_ref[1]))

    return pl.pallas_call(
        body, out_shape=(SDS((bsz, s_len, d), F32), SDS((bsz, nj, s_len, LANES), F32)), grid=(bsz, nj, nq, nq),
        name="flash_fwd",
        in_specs=[pl.BlockSpec((None, tq, LANES), lambda bi, j, qi, ki: (bi, qi, j)),
                  pl.BlockSpec((None, tk, LANES), lambda bi, j, qi, ki: (bi, jnp.minimum(ki, qi), j)),
                  pl.BlockSpec((None, tk, LANES), lambda bi, j, qi, ki: (bi, jnp.minimum(ki, qi), j)),
                  pl.BlockSpec((None, None, 2, tk), lambda bi, j, qi, ki: (bi, j, 0, jnp.minimum(ki, qi)))],
        out_specs=(pl.BlockSpec((None, tq, LANES), lambda bi, j, qi, ki: (bi, qi, j)),
                   pl.BlockSpec((None, None, tq, LANES), lambda bi, j, qi, ki: (bi, j, qi, 0))),
        scratch_shapes=[pltpu.VMEM((2, tq, 1), F32), pltpu.VMEM((2, tq, 1), F32), pltpu.VMEM((2, tq, LANES), F32)],
        compiler_params=_cp("parallel", "parallel", "arbitrary", "arbitrary"))(q, k, v, crow)


def _flash_probs(qv, kv, vv, dov, ov, lse, c_ref, h, mask):
    hm = _head_mask(h)
    qh = jnp.where(hm, qv, jnp.zeros_like(qv))
    s = lax.dot_general(qh, kv, _NT, preferred_element_type=F32) - c_ref[h:h + 1, :]
    pm = jnp.where(mask, jnp.exp(s - lse[:, h * HEAD_DIM:h * HEAD_DIM + 1]), 0.0)
    doh = jnp.where(hm, dov, jnp.zeros_like(dov))
    dpm = lax.dot_general(doh, vv, _NT, preferred_element_type=F32)
    delta = jnp.sum(jnp.where(hm, dov.astype(F32) * ov, 0.0), axis=1, keepdims=True)
    return pm, pm * (dpm - delta)


def _flash_bwd(q, k, v, do, o, lse, crow):
    bsz, s_len, d = q.shape
    nj = d // LANES
    tq = tk = _tile(s_len, ATT_BLOCK, LANES)
    nq = s_len // tq

    def body(q_ref, k_ref, v_ref, do_ref, o_ref, lse_ref, c_ref, dq_ref, dk_ref, dv_ref, dc_ref,
             dqa_ref, dka_ref, dva_ref, dca_ref):
        ki, qi = pl.program_id(2), pl.program_id(3)
        rows = pl.ds(pl.multiple_of(qi * tq, tq), tq)

        @pl.when((ki == 0) & (qi == 0))
        def _():
            dqa_ref[...] = jnp.zeros_like(dqa_ref)

        @pl.when(qi == ki)
        def _():
            dka_ref[...] = jnp.zeros_like(dka_ref)
            dva_ref[...] = jnp.zeros_like(dva_ref)
            dca_ref[...] = jnp.zeros_like(dca_ref)

        @pl.when(qi >= ki)
        def _():
            qv, kv, vv, dov, ov, lse = q_ref[...], k_ref[...], v_ref[...], do_ref[...], o_ref[...], lse_ref[...]
            mask = _causal(qi, ki, tq, tk)
            for h in range(2):
                pm, ds = _flash_probs(qv, kv, vv, dov, ov, lse, c_ref, h, mask)
                dsb = ds.astype(BF16)
                dva_ref[h] += lax.dot_general(pm.astype(BF16), dov, _TN, preferred_element_type=F32)
                dka_ref[h] += lax.dot_general(dsb, qv, _TN, preferred_element_type=F32)
                dqa_ref[h, rows, :] += jnp.dot(dsb, kv, preferred_element_type=F32)
                dca_ref[h:h + 1, :] -= jnp.sum(ds, axis=0, keepdims=True)

        @pl.when(qi == nq - 1)
        def _():
            m0 = _head_mask(0)
            dk_ref[...] = jnp.where(m0, dka_ref[0], dka_ref[1])
            dv_ref[...] = jnp.where(m0, dva_ref[0], dva_ref[1]).astype(BF16)
            dc_ref[...] = dca_ref[0:2, :]

        @pl.when(ki == nq - 1)
        def _():
            dq_ref[...] = jnp.where(_head_mask(0), dqa_ref[0, rows, :], dqa_ref[1, rows, :])

    def qside(bi, j, ki, qi):
        return (bi, jnp.maximum(qi, ki), j)

    def kside(bi, j, ki, qi):
        return (bi, ki, j)

    def dqside(bi, j, ki, qi):
        return (bi, jnp.where(ki == nq - 1, qi, 0), j)

    return pl.pallas_call(
        body, out_shape=(SDS((bsz, s_len, d), F32), SDS((bsz, s_len, d), F32), SDS((bsz, s_len, d), BF16),
                         SDS((bsz, nj, 2, s_len), F32)),
        grid=(bsz, nj, nq, nq), name="flash_bwd",
        in_specs=[pl.BlockSpec((None, tq, LANES), qside), pl.BlockSpec((None, tk, LANES), kside),
                  pl.BlockSpec((None, tk, LANES), kside), pl.BlockSpec((None, tq, LANES), qside),
                  pl.BlockSpec((None, tq, LANES), qside),
                  pl.BlockSpec((None, None, tq, LANES), lambda bi, j, ki, qi: (bi, j, jnp.maximum(qi, ki), 0)),
                  pl.BlockSpec((None, None, 2, tk), lambda bi, j, ki, qi: (bi, j, 0, ki))],
        out_specs=(pl.BlockSpec((None, tq, LANES), dqside), pl.BlockSpec((None, tk, LANES), kside),
                   pl.BlockSpec((None, tk, LANES), kside),
                   pl.BlockSpec((None, None, 2, tk), lambda bi, j, ki, qi: (bi, j, 0, ki))),
        scratch_shapes=[pltpu.VMEM((2, s_len, LANES), F32), pltpu.VMEM((2, tk, LANES), F32),
                        pltpu.VMEM((2, tk, LANES), F32), pltpu.VMEM((8, tk), F32)],
        compiler_params=_cp("parallel", "parallel", "arbitrary", "arbitrary"))(q, k, v, do, o, lse, crow)


def _loss_head(y, target):
    t, d = y.shape
    tm = _tile(t, 512)

    def body(y_ref, t_ref, dy_ref, acc_ref):
        @pl.when(pl.program_id(0) == 0)
        def _():
            acc_ref[...] = jnp.zeros_like(acc_ref)

        err = y_ref[...] - t_ref[...]
        dy_ref[...] = err / d
        acc_ref[...] += _fold8(err * err)

    row = pl.BlockSpec((tm, d), lambda i: (i, 0))
    return pl.pallas_call(
        body, out_shape=(SDS((t, d), F32), SDS((8, d), F32)), grid=(t // tm,), name="loss_head",
        in_specs=[row, row], out_specs=(row, pl.BlockSpec((8, d), lambda i: (0, 0))),
        compiler_params=_cp("arbitrary"))(y, target)


ADAM_COLS = 1024


def _adamw(g8, w, m, v):
    shape = w.shape
    cols = shape[-1]
    rows = w.size // cols
    g8, w, m, v = g8.reshape(N_DEV, rows, cols), w.reshape(rows, cols), m.reshape(rows, cols), v.reshape(rows, cols)
    tr = _tile(rows, 256, 16)
    c1 = 1.0 - ADAM_B1 ** ADAM_STEP
    c2 = 1.0 - ADAM_B2 ** ADAM_STEP

    def body(g8_ref, w_ref, m_ref, v_ref, g_ref, d_ref, nm_ref, nv_ref):
        g = g8_ref[0].astype(F32)
        for i in range(1, N_DEV):
            g = g + g8_ref[i].astype(F32)
        mn = ADAM_B1 * m_ref[...] + (1.0 - ADAM_B1) * g
        vn = ADAM_B2 * v_ref[...] + (1.0 - ADAM_B2) * (g * g)
        g_ref[...] = g
        nm_ref[...] = mn
        nv_ref[...] = vn
        d_ref[...] = -ADAM_LR * ((mn / c1) / (jnp.sqrt(vn / c2) + ADAM_EPS) + ADAM_WD * w_ref[...])

    blk = pl.BlockSpec((tr, cols), lambda i: (i, 0))
    outs = pl.pallas_call(
        body, out_shape=(SDS((rows, cols), F32),) * 4, grid=(rows // tr,), name="adamw",
        in_specs=[pl.BlockSpec((N_DEV, tr, cols), lambda i: (0, i, 0)), blk, blk, blk], out_specs=(blk,) * 4,
        compiler_params=_cp("parallel"))(g8, w, m, v)
    return [o.reshape(shape) for o in outs]


def _mesh_place():
    x, y, c = lax.axis_index("x"), lax.axis_index("y"), lax.axis_index("c")
    return x, y, c, 4 * x + 2 * y + c


def _gather(shards):
    n = len(shards)

    def body(*refs):
        ins, outs = refs[:n], refs[n:2 * n]
        send_sems, recv_sems, local_sems = refs[2 * n:]
        x, y, c, me = _mesh_place()
        sibling = (x, y, 1 - c)
        chips = [(1 - x, y), (x, 1 - y), (1 - x, 1 - y)]

        def block(px, py, pc):
            return 4 * px + 2 * py + pc

        def copy(t, k, blk, to, src=None):
            return pltpu.make_async_remote_copy(
                src_ref=outs[t].at[blk] if src is None else src, dst_ref=outs[t].at[blk],
                send_sem=send_sems.at[t, k], recv_sem=recv_sems.at[t, k], device_id=to,
                device_id_type=pl.DeviceIdType.MESH)

        own = [pltpu.make_async_copy(ins[t], outs[t].at[me], local_sems.at[t]) for t in range(n)]
        first = []
        for t in range(n):
            own[t].start()
            first.append(copy(t, 0, me, sibling, src=ins[t]))
            first += [copy(t, 1 + j, me, (*chip, c), src=ins[t]) for j, chip in enumerate(chips)]
        for cp in first:
            cp.start()
        passed = []
        for j, chip in enumerate(chips):
            for t in range(n):
                copy(t, 1 + j, block(*chip, c), (x, y, c)).wait_recv()
                cp = copy(t, 4 + j, block(*chip, c), sibling)
                cp.start()
                passed.append(cp)
        for t in range(n):
            copy(t, 0, block(x, y, 1 - c), (x, y, c)).wait_recv()
            for j, chip in enumerate(chips):
                copy(t, 4 + j, block(*chip, 1 - c), (x, y, c)).wait_recv()
        for cp in first + passed:
            cp.wait_send()
        for cp in own:
            cp.wait()

    hbm = pl.BlockSpec(memory_space=pl.ANY)
    return pl.pallas_call(
        body, out_shape=[SDS((N_DEV,) + tuple(s.shape), s.dtype) for s in shards], name="gather",
        in_specs=[hbm] * n, out_specs=[hbm] * n,
        scratch_shapes=[pltpu.SemaphoreType.DMA((n, N_DEV - 1)), pltpu.SemaphoreType.DMA((n, N_DEV - 1)),
                        pltpu.SemaphoreType.DMA((n,))])(*shards)


def _scatter(items, groups):
    n = len(items)
    place = {it: (g, l) for g, members in enumerate(groups) for l, it in enumerate(members)}

    def body(*refs):
        ins, outs = refs[:n], refs[n:n + len(groups)]
        send_sems, recv_sems, local_sems = refs[n + len(groups):]
        x, y, c, me = _mesh_place()
        copies = []
        for it in range(n):
            g, l = place[it]
            own = pltpu.make_async_copy(ins[it].at[me], outs[g].at[me, l], local_sems.at[it])
            own.start()
            copies.append(own)
            for kbits in range(1, N_DEV):
                px = 1 - x if kbits & 4 else x
                py = 1 - y if kbits & 2 else y
                pc = 1 - c if kbits & 1 else c
                cp = pltpu.make_async_remote_copy(
                    src_ref=ins[it].at[4 * px + 2 * py + pc], dst_ref=outs[g].at[me, l],
                    send_sem=send_sems.at[it, kbits - 1], recv_sem=recv_sems.at[it, kbits - 1],
                    device_id=(px, py, pc), device_id_type=pl.DeviceIdType.MESH)
                cp.start()
                copies.append(cp)
        for cp in copies:
            cp.wait()

    hbm = pl.BlockSpec(memory_space=pl.ANY)
    out_shape = [SDS((N_DEV, len(members)) + tuple(items[members[0]].shape[1:]), items[members[0]].dtype)
                 for members in groups]
    return pl.pallas_call(
        body, out_shape=out_shape, name="scatter", in_specs=[hbm] * n, out_specs=[hbm] * len(groups),
        scratch_shapes=[pltpu.SemaphoreType.DMA((n, N_DEV - 1)), pltpu.SemaphoreType.DMA((n, N_DEV - 1)),
                        pltpu.SemaphoreType.DMA((n,))])(*items)


def _cat_lanes(g, layer, nb, blk, width):
    _, _, rows, c = g.shape
    tr = _tile(rows, 256, 16)

    def body(g_ref, o_ref):
        for p in range(nb):
            o_ref[:, p * c:(p + 1) * c] = g_ref[p]
        if width > nb * c:
            o_ref[:, nb * c:] = jnp.zeros((tr, width - nb * c), g.dtype)

    return pl.pallas_call(
        body, out_shape=SDS((rows, width), g.dtype), grid=(rows // tr,), name="cat_lanes",
        in_specs=[pl.BlockSpec((nb, None, tr, c), lambda i: (blk, layer, i, 0))],
        out_specs=pl.BlockSpec((tr, width), lambda i: (i, 0)),
        compiler_params=_cp("parallel"))(g)


def _split_lanes(parts, c):
    rows = parts[0].shape[0]
    counts = [p.shape[1] // c for p in parts]
    tr = _tile(rows, 256, 16)

    def body(*refs):
        o_ref = refs[-1]
        q = 0
        for x_ref, cnt in zip(refs[:-1], counts):
            for p in range(cnt):
                o_ref[q] = x_ref[:, p * c:(p + 1) * c]
                q += 1

    return pl.pallas_call(
        body, out_shape=SDS((sum(counts), rows, c), parts[0].dtype), grid=(rows // tr,), name="split_lanes",
        in_specs=[pl.BlockSpec((tr, p.shape[1]), lambda i: (i, 0)) for p in parts],
        out_specs=pl.BlockSpec((sum(counts), tr, c), lambda i: (0, i, 0)),
        compiler_params=_cp("parallel"))(*parts)


def _unshard(g8, shard_shape, axis):
    full = jnp.moveaxis(g8.reshape((N_DEV,) + tuple(shard_shape)), 0, axis)
    shape = list(shard_shape)
    shape[axis] *= N_DEV
    return full.reshape(shape)


def _to_shards(full, axis):
    shape = list(full.shape)
    shape[axis:axis + 1] = [N_DEV, shape[axis] // N_DEV]
    return jnp.moveaxis(full.reshape(shape), axis, 0).reshape(N_DEV, -1)


SMALL = [n for n in SHARDED if n not in MATRICES]


def _flat_rows(parts):
    flat = jnp.concatenate([p.reshape(-1) for p in parts])
    chunk = 8 * ADAM_COLS
    n = -(-flat.shape[0] // chunk) * chunk
    return jnp.pad(flat, (0, n - flat.shape[0])).reshape(n // ADAM_COLS, ADAM_COLS)


def _prepare_weights(gathered, small, shards):
    wt = {}
    flat = small.reshape(N_DEV, -1)
    off = 0
    for n in SMALL:
        size = shards[n].size
        wt[n] = _unshard(flat[:, off:off + size], shards[n].shape, SHARD_AXIS[n])
        off += size
    for n in ('conv_w_out', 'fox_w_o', 'ffn_w_down'):
        g = gathered[n]
        wt[n] = [g[:, l].reshape(N_DEV * g.shape[2], g.shape[3]) for l in range(g.shape[1])]
    g = gathered['pool_w']
    wt['pool_w'] = [jnp.moveaxis(g[:, l], 0, 1).reshape(g.shape[2], N_DEV * g.shape[3], g.shape[4])
                    for l in range(g.shape[1])]
    g = gathered['conv_w_in']
    wt['conv_w_in'] = [_cat_lanes(g, l, N_DEV, 0, N_DEV * g.shape[3]) for l in range(g.shape[1])]
    g = gathered['fox_w_in']
    wt['fox_w_in'] = [_cat_lanes(g, l, N_DEV, 0, 3 * g.shape[2] + LANES) for l in range(g.shape[1])]
    g = gathered['ffn_w_up']
    half = N_DEV // 2
    wt['ffn_w_up_v'] = [_cat_lanes(g, l, half, 0, half * g.shape[3]) for l in range(g.shape[1])]
    wt['ffn_w_up_g'] = [_cat_lanes(g, l, half, 1, half * g.shape[3]) for l in range(g.shape[1])]
    return wt


def _pad_rows(w, rows):
    return jnp.pad(w, ((0, rows - w.shape[0]), (0, 0)))


def _fold(acc):
    return acc.sum(axis=0)


def _local_step(x, target, wt):
    bsz, s_len, d = x.shape
    t = bsz * s_len
    depth = wt['norm_mix'].shape[0]
    n_heads = d // HEAD_DIM
    f = wt['ffn_w_up_v'][0].shape[1]
    row = lambda a: a.reshape(1, -1)
    grads = {n: {} for n in WEIGHTS}
    saved = []

    xc = x.reshape(t, d)
    for i in range(depth):
        j = i // 3
        kind = i % 3
        sv = {'x_mix': xc}
        gm = row(wt['norm_mix'][i])
        if kind == 0:
            hn = _rmsnorm_fwd(xc, gm)
            p = _mm(hn, wt['conv_w_in'][j], bias=row(wt['conv_b_in'][j]), name="conv_in")
            u, sact = _conv_act_fwd(p.reshape(bsz, s_len, 2 * d), _pad_rows(wt['conv_dw'][j], 32),
                                    row(wt['conv_dw_b'][j]), row(wt['conv_ln_g'][j]), row(wt['conv_ln_b'][j]))
            sact = sact.reshape(t, d)
            xn = _mm(sact, wt['conv_w_out'][j], bias=row(wt['conv_b_out'][j]), residual=xc, out_dtype=F32,
                     name="conv_out")
            sv.update(hn=hn, p=p, u=u.reshape(t, d), sact=sact)
        elif kind == 1:
            xn, pp = _pool_fwd(xc.reshape(bsz, s_len, d), gm, wt['pool_w'][j], row(wt['pool_b'][j]),
                               row(wt['pool_scale'][j]))
            xn = xn.reshape(t, d)
            sv.update(p=pp)
        else:
            hn = _rmsnorm_fwd(xc, gm)
            wp = wt['fox_w_in'][j]
            bf = jnp.pad(wt['fox_b_f'][j], (0, LANES - n_heads)).reshape(1, LANES)
            gq = jnp.tile(wt['fox_q_gain'][j], n_heads).reshape(1, d)
            gk = jnp.tile(wt['fox_k_gain'][j], n_heads).reshape(1, d)
            proj = _mm(hn, wp, out_dtype=F32, name="fox_in")
            c = _fox_gate_fwd(proj.reshape(bsz, s_len, -1), bf, n_heads)
            crow = jnp.swapaxes(c, 1, 2)[:, :n_heads].reshape(bsz, n_heads // 2, 2, s_len)
            qn, kn, vb = _fox_qknorm_fwd(proj, gq, gk, d)
            shp = (bsz, s_len, d)
            o, lse = _flash_fwd(qn.reshape(shp), kn.reshape(shp), vb.reshape(shp), crow)
            o = o.reshape(t, d)
            xn = _mm(o, wt['fox_w_o'][j], residual=xc, out_dtype=F32, name="fox_out")
            sv.update(hn=hn, wp=wp, bf=bf, gq=gq, gk=gk, proj=proj, crow=crow, qn=qn, kn=kn, vb=vb, o=o, lse=lse)
        xc = xn
        sv['x_ffn'] = xc
        hf = _rmsnorm_fwd(xc, row(wt['norm_ffn'][i]))
        shf = (bsz, s_len, f)
        uv = _mm(hf, wt['ffn_w_up_v'][i], name="ffn_up").reshape(shf)
        ug = _mm(hf, wt['ffn_w_up_g'][i], name="ffn_up").reshape(shf)
        dw8 = _pad_rows(wt['ffn_dw'][i], 8)
        af = _ffn_act_fwd(uv, ug, dw8, row(wt['ffn_dw_b'][i])).reshape(t, f)
        xc = _mm(af, wt['ffn_w_down'][i], residual=xc, out_dtype=F32, name="ffn_down")
        sv.update(hf=hf, uv=uv, ug=ug, af=af, dw8=dw8)
        saved.append(sv)

    dx, sq = _loss_head(xc, target.reshape(t, d))

    for i in reversed(range(depth)):
        j = i // 3
        kind = i % 3
        sv = saved[i]
        shf = (bsz, s_len, f)
        da = _mm(dx, wt['ffn_w_down'][i], trans_b=True, name="ffn_down_dgrad")
        gw, _ = _wgrad(sv['af'], dx, name="ffn_down_wgrad")
        grads['ffn_w_down'][i] = gw.reshape(N_DEV, f // N_DEV, d)
        dvv, dvg, ddwv, ddwg, dbv, dbg = _ffn_act_bwd1(sv['uv'], sv['ug'], da.reshape(shf), sv['dw8'],
                                                       row(wt['ffn_dw_b'][i]))
        grads['ffn_dw'][i] = jnp.concatenate([ddwv.sum(axis=1), ddwg.sum(axis=1)], axis=1)
        grads['ffn_dw_b'][i] = jnp.concatenate([_fold(dbv), _fold(dbg)])
        duv, dug = _ffn_act_bwd2(dvv, dvg, sv['dw8'])
        duv, dug = duv.reshape(t, f), dug.reshape(t, f)
        gv, _ = _wgrad(sv['hf'], duv, name="ffn_up_wgrad")
        gg, _ = _wgrad(sv['hf'], dug, name="ffn_up_wgrad")
        grads['ffn_w_up'][i] = _split_lanes([gv, gg], 2 * f // N_DEV)
        dhf = _mm(duv, wt['ffn_w_up_v'][i], trans_b=True, a2=dug, b2=wt['ffn_w_up_g'][i], name="ffn_up_dgrad")
        dx, dg = _rmsnorm_bwd(sv['x_ffn'], row(wt['norm_ffn'][i]), dhf, dx)
        grads['norm_ffn'][i] = _fold(dg)
        gm = row(wt['norm_mix'][i])
        if kind == 0:
            dsact = _mm(dx, wt['conv_w_out'][j], trans_b=True, name="conv_out_dgrad")
            gw, cs = _wgrad(sv['sact'], dx, name="conv_out_wgrad")
            grads['conv_w_out'][j] = gw.reshape(N_DEV, d // N_DEV, d)
            grads['conv_b_out'][j] = _fold(cs)
            du, dlg, dlb, dwb = _conv_act_bwd1(sv['u'], dsact, row(wt['conv_ln_g'][j]), row(wt['conv_ln_b'][j]))
            grads['conv_ln_g'][j], grads['conv_ln_b'][j], grads['conv_dw_b'][j] = _fold(dlg), _fold(dlb), _fold(dwb)
            dp, ddw = _conv_act_bwd2(du.reshape(bsz, s_len, d), sv['p'].reshape(bsz, s_len, 2 * d),
                                     _pad_rows(wt['conv_dw'][j], 32))
            grads['conv_dw'][j] = ddw.sum(axis=1)[:wt['conv_dw'].shape[1]]
            dp = dp.reshape(t, 2 * d)
            gw, cs = _wgrad(sv['hn'], dp, name="conv_in_wgrad")
            grads['conv_w_in'][j] = _split_lanes([gw], 2 * d // N_DEV)
            grads['conv_b_in'][j] = _fold(cs)
            dhn = _mm(dp, wt['conv_w_in'][j], trans_b=True, name="conv_in_dgrad")
            dx, dg = _rmsnorm_bwd(sv['x_mix'], gm, dhn, dx)
            grads['norm_mix'][i] = _fold(dg)
        elif kind == 1:
            shp = (bsz, s_len, d)
            dxn, dwp, dbp, dsc, dg = _pool_bwd(sv['x_mix'].reshape(shp), dx.reshape(shp), sv['p'], gm, wt['pool_w'][j],
                                               row(wt['pool_b'][j]), row(wt['pool_scale'][j]))
            dx = dxn.reshape(t, d)
            ng, cg = dwp.shape[0], dwp.shape[1]
            grads['pool_w'][j] = jnp.moveaxis(dwp.reshape(ng, N_DEV, cg // N_DEV, cg), 1, 0).astype(BF16)
            grads['pool_b'][j] = _fold(dbp).reshape(wt['pool_b'].shape[1:])
            grads['pool_scale'][j] = _fold(dsc)
            grads['norm_mix'][i] = _fold(dg)
        else:
            shp = (bsz, s_len, d)
            do = _mm(dx, wt['fox_w_o'][j], trans_b=True, name="fox_out_dgrad")
            gw, _ = _wgrad(sv['o'], dx, name="fox_out_wgrad")
            grads['fox_w_o'][j] = gw.reshape(N_DEV, d // N_DEV, d)
            fl_args = (sv['qn'].reshape(shp), sv['kn'].reshape(shp), sv['vb'].reshape(shp), do.reshape(shp),
                       sv['o'].reshape(shp), sv['lse'], sv['crow'])
            dq, dk, dv, dcrow = _flash_bwd(*fl_args)
            dc = jnp.swapaxes(dcrow.reshape(bsz, n_heads, s_len), 1, 2)
            dc = jnp.pad(dc, ((0, 0), (0, 0), (0, LANES - n_heads)))
            dfl, dbf = _fox_gate_bwd(dc, sv['proj'].reshape(bsz, s_len, -1), sv['bf'], n_heads)
            grads['fox_b_f'][j] = _fold(dbf)[:n_heads]
            dqkv, dgq, dgk = _fox_qknorm_bwd(sv['proj'], dq.reshape(t, d), dk.reshape(t, d), dv.reshape(t, d),
                                             sv['gq'], sv['gk'], d)
            grads['fox_q_gain'][j] = _fold(dgq).reshape(n_heads, HEAD_DIM).sum(axis=0)
            grads['fox_k_gain'][j] = _fold(dgk).reshape(n_heads, HEAD_DIM).sum(axis=0)
            dproj = jnp.concatenate([dqkv, dfl.reshape(t, LANES)], axis=1)
            dwp, _ = _wgrad(sv['hn'], dproj, name="fox_in_wgrad")
            grads['fox_w_in'][j] = _split_lanes([dwp], (3 * d + n_heads) // N_DEV)
            dhn = _mm(dproj, sv['wp'], trans_b=True, name="fox_in_dgrad")
            dx, dg = _rmsnorm_bwd(sv['x_mix'], gm, dhn, dx)
            grads['norm_mix'][i] = _fold(dg)

    listed = {n: [g[k] for k in sorted(g)] for n, g in grads.items()}
    small = {n: jnp.stack(g) for n, g in listed.items() if n not in MATRICES}
    return sq.sum(), dx.reshape(bsz, s_len, d), small, {n: listed[n] for n in MATRICES}


def _train_step(x, target, w, m, v):
    got = _gather([w[n].astype(BF16) for n in MATRICES] + [_flat_rows([w[n] for n in SMALL])])
    wt = _prepare_weights(dict(zip(MATRICES, got[:-1])), got[-1], w)
    wt.update({n: w[n] for n in REPLICATED})
    sq, grad_x, gsmall, gbig = _local_step(x, target, wt)
    d = x.shape[-1]

    shard_rows = jnp.concatenate([_to_shards(gsmall[n], SHARD_AXIS[n]) for n in SMALL], axis=1)
    rep = jnp.concatenate([gsmall[n].reshape(-1) for n in REPLICATED] + [(0.5 / d) * sq.reshape(1)])
    rows = jnp.concatenate([shard_rows, jnp.broadcast_to(rep, (N_DEV, rep.shape[0]))], axis=1)
    chunk = 8 * ADAM_COLS
    n_all = rows.shape[1]
    n_pad = -(-n_all // chunk) * chunk
    rows = jnp.pad(rows, ((0, 0), (0, n_pad - n_all))).reshape(N_DEV, n_pad // ADAM_COLS, ADAM_COLS)

    items, groups = [], []
    for n in MATRICES:
        groups.append(list(range(len(items), len(items) + len(gbig[n]))))
        items += gbig[n]
    groups.append([len(items)])
    items.append(rows)
    recv = _scatter(items, groups)

    res = [{}, {}, {}, {}]
    for n, r in zip(MATRICES, recv[:-1]):
        for k, o in enumerate(_adamw(r, w[n], m[n], v[n])):
            res[k][n] = o
    order = SMALL + REPLICATED

    def flat(tree):
        parts = jnp.concatenate([tree[n].reshape(-1) for n in order])
        return jnp.pad(parts, (0, n_pad - parts.shape[0])).reshape(n_pad // ADAM_COLS, ADAM_COLS)

    outs = [o.reshape(-1) for o in _adamw(recv[-1].reshape(rows.shape), flat(w), flat(m), flat(v))]
    off = 0
    for n in order:
        size = w[n].size
        for k in range(4):
            res[k][n] = outs[k][off:off + size].reshape(w[n].shape)
        off += size
    loss = outs[0][n_all - 1]
    return (loss, grad_x, *[res[0][n] for n in WEIGHTS], *[res[1][n] for n in WEIGHTS],
            *[res[2][n] for n in WEIGHTS], *[res[3][n] for n in WEIGHTS])


def kernel(x, norm_mix, norm_ffn, conv_w_in, conv_b_in, conv_dw, conv_dw_b, conv_ln_g, conv_ln_b, conv_w_out, conv_b_out, pool_w, pool_b, pool_scale, fox_w_in, fox_b_f, fox_q_gain, fox_k_gain, fox_w_o, ffn_w_up, ffn_dw, ffn_dw_b, ffn_w_down, loss_target, m_norm_mix, m_norm_ffn, m_conv_w_in, m_conv_b_in, m_conv_dw, m_conv_dw_b, m_conv_ln_g, m_conv_ln_b, m_conv_w_out, m_conv_b_out, m_pool_w, m_pool_b, m_pool_scale, m_fox_w_in, m_fox_b_f, m_fox_q_gain, m_fox_k_gain, m_fox_w_o, m_ffn_w_up, m_ffn_dw, m_ffn_dw_b, m_ffn_w_down, v_norm_mix, v_norm_ffn, v_conv_w_in, v_conv_b_in, v_conv_dw, v_conv_dw_b, v_conv_ln_g, v_conv_ln_b, v_conv_w_out, v_conv_b_out, v_pool_w, v_pool_b, v_pool_scale, v_fox_w_in, v_fox_b_f, v_fox_q_gain, v_fox_k_gain, v_fox_w_o, v_ffn_w_up, v_ffn_dw, v_ffn_dw_b, v_ffn_w_down):
    w = dict(zip(WEIGHTS, (norm_mix, norm_ffn, conv_w_in, conv_b_in, conv_dw, conv_dw_b, conv_ln_g, conv_ln_b, conv_w_out, conv_b_out, pool_w, pool_b, pool_scale, fox_w_in, fox_b_f, fox_q_gain, fox_k_gain, fox_w_o, ffn_w_up, ffn_dw, ffn_dw_b, ffn_w_down)))
    m = dict(zip(WEIGHTS, (m_norm_mix, m_norm_ffn, m_conv_w_in, m_conv_b_in, m_conv_dw, m_conv_dw_b, m_conv_ln_g, m_conv_ln_b, m_conv_w_out, m_conv_b_out, m_pool_w, m_pool_b, m_pool_scale, m_fox_w_in, m_fox_b_f, m_fox_q_gain, m_fox_k_gain, m_fox_w_o, m_ffn_w_up, m_ffn_dw, m_ffn_dw_b, m_ffn_w_down)))
    v = dict(zip(WEIGHTS, (v_norm_mix, v_norm_ffn, v_conv_w_in, v_conv_b_in, v_conv_dw, v_conv_dw_b, v_conv_ln_g, v_conv_ln_b, v_conv_w_out, v_conv_b_out, v_pool_w, v_pool_b, v_pool_scale, v_fox_w_in, v_fox_b_f, v_fox_q_gain, v_fox_k_gain, v_fox_w_o, v_ffn_w_up, v_ffn_dw, v_ffn_dw_b, v_ffn_w_down)))
    return _train_step(x, loss_target, w, m, v)
```

```python
import functools
import math

import jax
import jax.numpy as jnp
from jax import lax
from jax.experimental import pallas as pl
from jax.experimental.pallas import tpu as pltpu

F32, BF16 = jnp.float32, jnp.bfloat16
SDS = jax.ShapeDtypeStruct

N_DEV = 8
EPS = 1e-6
POOL_WINDOWS = (2, 4, 8, 16)
HEAD_DIM = 64
ADAM_LR, ADAM_B1, ADAM_B2, ADAM_EPS, ADAM_WD, ADAM_STEP = 0.001, 0.9, 0.999, 1e-08, 0.01, 10
LANES = 128
VMEM_LIMIT_BYTES = 48 * 1024 * 1024
NEG = -1e30

WEIGHTS = ['norm_mix', 'norm_ffn', 'conv_w_in', 'conv_b_in', 'conv_dw', 'conv_dw_b', 'conv_ln_g', 'conv_ln_b',
           'conv_w_out', 'conv_b_out', 'pool_w', 'pool_b', 'pool_scale', 'fox_w_in', 'fox_b_f', 'fox_q_gain',
           'fox_k_gain', 'fox_w_o', 'ffn_w_up', 'ffn_dw', 'ffn_dw_b', 'ffn_w_down']
SHARD_AXIS = {'conv_w_in': 2, 'conv_b_in': 1, 'conv_dw': 2, 'conv_dw_b': 1, 'conv_ln_g': 1, 'conv_ln_b': 1,
              'conv_w_out': 1, 'conv_b_out': 1, 'pool_w': 2, 'pool_b': 2, 'fox_w_in': 2, 'fox_w_o': 1,
              'ffn_w_up': 2, 'ffn_dw': 2, 'ffn_w_down': 1}
MATRICES = ('conv_w_in', 'conv_w_out', 'pool_w', 'fox_w_in', 'fox_w_o', 'ffn_w_up', 'ffn_w_down')
SHARDED = [n for n in WEIGHTS if n in SHARD_AXIS]
REPLICATED = [n for n in WEIGHTS if n not in SHARD_AXIS]


def _cp(*sem):
    return pltpu.CompilerParams(dimension_semantics=sem, vmem_limit_bytes=VMEM_LIMIT_BYTES)


def _tile(n, pref, align=8):
    if n <= pref:
        return n
    t = (pref // align) * align
    while t >= align:
        if n % t == 0:
            return t
        t -= align
    return n


def _fold8(x):
    r, c = x.shape
    return x.reshape(r // 8, 8, c).sum(axis=0)


def _sigmoid(x):
    return 0.5 * jnp.tanh(0.5 * x) + 0.5


def _shifts_back(cur, tail, n):
    hb = tail.shape[0]
    xe = jnp.concatenate([tail, cur], axis=0)
    return [cur] + [pltpu.roll(xe, j, axis=0)[hb:] for j in range(1, n)]


def _shifts_fwd(cur, head, n):
    ts = cur.shape[0]
    xe = jnp.concatenate([cur, head], axis=0)
    ln = xe.shape[0]
    return [cur] + [pltpu.roll(xe, ln - j, axis=0)[:ts] for j in range(1, n)]


def _dot_hi(a, b):
    return jnp.dot(a, b, preferred_element_type=F32, precision=lax.Precision.HIGHEST)


def _rmsnorm_fwd(x, g):
    t, d = x.shape
    tm = _tile(t, 512)

    def body(x_ref, g_ref, h_ref):
        xv = x_ref[...]
        r = lax.rsqrt(jnp.mean(xv * xv, axis=-1, keepdims=True) + EPS)
        h_ref[...] = (xv * r * g_ref[...]).astype(BF16)

    return pl.pallas_call(
        body, out_shape=SDS((t, d), BF16), grid=(t // tm,), name="rmsnorm_fwd",
        in_specs=[pl.BlockSpec((tm, d), lambda i: (i, 0)), pl.BlockSpec((1, d), lambda i: (0, 0))],
        out_specs=pl.BlockSpec((tm, d), lambda i: (i, 0)), compiler_params=_cp("parallel"))(x, g)


def _rmsnorm_bwd(x, g, dh, dres):
    t, d = x.shape
    tm = _tile(t, 512)

    def body(x_ref, g_ref, dh_ref, dres_ref, dx_ref, dg_ref):
        @pl.when(pl.program_id(0) == 0)
        def _():
            dg_ref[...] = jnp.zeros_like(dg_ref)

        xv = x_ref[...]
        r = lax.rsqrt(jnp.mean(xv * xv, axis=-1, keepdims=True) + EPS)
        xh = xv * r
        dhv = dh_ref[...].astype(F32)
        u = dhv * g_ref[...]
        dx_ref[...] = dres_ref[...] + r * (u - xh * jnp.mean(u * xh, axis=-1, keepdims=True))
        dg_ref[...] += _fold8(dhv * xh)

    row = pl.BlockSpec((tm, d), lambda i: (i, 0))
    return pl.pallas_call(
        body, out_shape=(SDS((t, d), F32), SDS((8, d), F32)), grid=(t // tm,), name="rmsnorm_bwd",
        in_specs=[row, pl.BlockSpec((1, d), lambda i: (0, 0)), row, row],
        out_specs=(row, pl.BlockSpec((8, d), lambda i: (0, 0))), compiler_params=_cp("arbitrary"))(x, g, dh, dres)


def _mm(a, b, *, trans_b=False, bias=None, residual=None, a2=None, b2=None, out_dtype=BF16, name="mm"):
    m, k = a.shape
    n = b.shape[0] if trans_b else b.shape[1]
    tm, tn, tk = _tile(m, 512, 16), _tile(n, 1536, LANES), _tile(k, 1536, LANES)
    nk = k // tk
    two = a2 is not None
    steps = 2 * nk if two else nk
    dims = (((1,), (1,)), ((), ())) if trans_b else (((1,), (0,)), ((), ()))
    has_bias, has_res = bias is not None, residual is not None

    def body(*refs):
        n_in = 4 if two else 2
        bias_ref = refs[n_in] if has_bias else None
        res_ref = refs[n_in + has_bias] if has_res else None

        def finish(r):
            if has_bias:
                r = r + bias_ref[...]
            if has_res:
                r = r + res_ref[...]
            return r.astype(out_dtype)

        def dot(a_ref, b_ref):
            return lax.dot_general(a_ref[...].astype(BF16), b_ref[...].astype(BF16), dims, preferred_element_type=F32)

        if steps == 1:
            refs[-1][...] = finish(dot(refs[0], refs[1]))
            return
        o_ref, acc_ref = refs[-2], refs[-1]
        kk = pl.program_id(2)

        @pl.when(kk == 0)
        def _():
            acc_ref[...] = jnp.zeros_like(acc_ref)

        @pl.when(kk < nk)
        def _():
            acc_ref[...] += dot(refs[0], refs[1])

        if two:
            @pl.when(kk >= nk)
            def _():
                acc_ref[...] += dot(refs[2], refs[3])

        @pl.when(kk == steps - 1)
        def _():
            o_ref[...] = finish(acc_ref[...])

    def pair(first):
        kmap = (lambda kk: jnp.minimum(kk, nk - 1)) if first else (lambda kk: jnp.maximum(kk - nk, 0))
        a_spec = pl.BlockSpec((tm, tk), lambda i, j, kk: (i, kmap(kk)))
        if trans_b:
            b_spec = pl.BlockSpec((tn, tk), lambda i, j, kk: (j, kmap(kk)))
        else:
            b_spec = pl.BlockSpec((tk, tn), lambda i, j, kk: (kmap(kk), j))
        return [a_spec, b_spec]

    in_specs, args = pair(True), [a, b]
    if two:
        in_specs += pair(False)
        args += [a2, b2]
    if has_bias:
        in_specs.append(pl.BlockSpec((1, tn), lambda i, j, kk: (0, j)))
        args.append(bias)
    if has_res:
        in_specs.append(pl.BlockSpec((tm, tn), lambda i, j, kk: (i, j)))
        args.append(residual)
    return pl.pallas_call(
        body, out_shape=SDS((m, n), out_dtype), grid=(m // tm, n // tn, steps), name=name,
        in_specs=in_specs, out_specs=pl.BlockSpec((tm, tn), lambda i, j, kk: (i, j)),
        scratch_shapes=[] if steps == 1 else [pltpu.VMEM((tm, tn), F32)],
        compiler_params=_cp("parallel", "parallel", "arbitrary"))(*args)


def _wgrad(a, g, *, out_dtype=BF16, name="wgrad"):
    m, ka = a.shape
    n = g.shape[1]
    ta, tn, tm = _tile(ka, 1536, LANES), _tile(n, 1536, LANES), _tile(m, 1024)
    nm = m // tm

    def body(a_ref, g_ref, o_ref, cs_ref, acc_ref):
        i, mm = pl.program_id(1), pl.program_id(2)

        @pl.when(mm == 0)
        def _():
            acc_ref[...] = jnp.zeros_like(acc_ref)

        @pl.when((mm == 0) & (i == 0))
        def _():
            cs_ref[...] = jnp.zeros_like(cs_ref)

        gv = g_ref[...]
        acc_ref[...] += lax.dot_general(a_ref[...].astype(BF16), gv.astype(BF16), (((0,), (0,)), ((), ())),
                                        preferred_element_type=F32)

        @pl.when(i == 0)
        def _():
            cs_ref[...] += _fold8(gv.astype(F32))

        @pl.when(mm == nm - 1)
        def _():
            o_ref[...] = acc_ref[...].astype(out_dtype)

    return pl.pallas_call(
        body, out_shape=(SDS((ka, n), out_dtype), SDS((8, n), F32)), grid=(n // tn, ka // ta, nm), name=name,
        in_specs=[pl.BlockSpec((tm, ta), lambda j, i, mm: (mm, i)), pl.BlockSpec((tm, tn), lambda j, i, mm: (mm, j))],
        out_specs=(pl.BlockSpec((ta, tn), lambda j, i, mm: (i, j)), pl.BlockSpec((8, tn), lambda j, i, mm: (0, j))),
        scratch_shapes=[pltpu.VMEM((ta, tn), F32)],
        compiler_params=_cp("arbitrary", "arbitrary", "arbitrary"))(a, g)


FFN_HALO = 16


def _ffn_conv(uc_ref, up_ref, w_ref, b_ref, s):
    u = uc_ref[...].astype(F32)
    tail = jnp.where(s > 0, up_ref[...].astype(F32), 0.0)
    sh = _shifts_back(u, tail, 3)
    return sh, sh[2] * w_ref[0:1, :] + sh[1] * w_ref[1:2, :] + sh[0] * w_ref[2:3, :] + b_ref[...]


def _ffn_act_fwd(uv, ug, dw8, b):
    bsz, s_len, f = uv.shape
    tc, ts = _tile(f, 256, LANES), _tile(s_len, 1024, FFN_HALO)
    nf, r = f // tc, ts // FFN_HALO

    def body(uv_ref, uvp_ref, ug_ref, ugp_ref, wv_ref, wg_ref, bv_ref, bg_ref, a_ref):
        s = pl.program_id(2)
        _, val = _ffn_conv(uv_ref, uvp_ref, wv_ref, bv_ref, s)
        _, gate = _ffn_conv(ug_ref, ugp_ref, wg_ref, bg_ref, s)
        a_ref[...] = (gate * _sigmoid(gate) * val).astype(BF16)

    cur = pl.BlockSpec((None, ts, tc), lambda bi, j, s: (bi, s, j))
    prev = pl.BlockSpec((None, FFN_HALO, tc), lambda bi, j, s: (bi, jnp.maximum(s * r - 1, 0), j))

    def par(rows, off):
        return pl.BlockSpec((rows, tc), lambda bi, j, s: (0, j + off))

    return pl.pallas_call(
        body, out_shape=SDS((bsz, s_len, f), BF16), grid=(bsz, nf, s_len // ts), name="ffn_act_fwd",
        in_specs=[cur, prev, cur, prev, par(8, 0), par(8, nf), par(1, 0), par(1, nf)], out_specs=cur,
        compiler_params=_cp("parallel", "parallel", "arbitrary"))(uv, uv, ug, ug, dw8, dw8, b, b)


def _ffn_act_bwd1(uv, ug, da, dw8, b):
    bsz, s_len, f = uv.shape
    tc, ts = _tile(f, 256, LANES), _tile(s_len, 1024, FFN_HALO)
    nf, r = f // tc, ts // FFN_HALO

    def body(uv_ref, uvp_ref, ug_ref, ugp_ref, da_ref, wv_ref, wg_ref, bv_ref, bg_ref,
             dvv_ref, dvg_ref, ddwv_ref, ddwg_ref, dbv_ref, dbg_ref):
        bi, s = pl.program_id(1), pl.program_id(2)

        @pl.when((bi == 0) & (s == 0))
        def _():
            for ref in (ddwv_ref, ddwg_ref, dbv_ref, dbg_ref):
                ref[...] = jnp.zeros_like(ref)

        shv, val = _ffn_conv(uv_ref, uvp_ref, wv_ref, bv_ref, s)
        shg, gate = _ffn_conv(ug_ref, ugp_ref, wg_ref, bg_ref, s)
        sg = _sigmoid(gate)
        dav = da_ref[...].astype(F32)
        for dv, sh, dv_ref, ddw_ref, db_ref in (
                (dav * gate * sg, shv, dvv_ref, ddwv_ref, dbv_ref),
                (dav * val * (sg * (1.0 + gate * (1.0 - sg))), shg, dvg_ref, ddwg_ref, dbg_ref)):
            dv_ref[...] = dv.astype(BF16)
            db_ref[...] += _fold8(dv)
            for k in range(3):
                ddw_ref[k] += _fold8(dv * sh[2 - k])

    cur = pl.BlockSpec((None, ts, tc), lambda j, bi, s: (bi, s, j))
    prev = pl.BlockSpec((None, FFN_HALO, tc), lambda j, bi, s: (bi, jnp.maximum(s * r - 1, 0), j))

    def par(rows, off):
        return pl.BlockSpec((rows, tc), lambda j, bi, s: (0, j + off))

    acc3 = pl.BlockSpec((3, 8, tc), lambda j, bi, s: (0, 0, j))
    acc1 = pl.BlockSpec((8, tc), lambda j, bi, s: (0, j))
    return pl.pallas_call(
        body, out_shape=(SDS((bsz, s_len, f), BF16), SDS((bsz, s_len, f), BF16), SDS((3, 8, f), F32),
                         SDS((3, 8, f), F32), SDS((8, f), F32), SDS((8, f), F32)),
        grid=(nf, bsz, s_len // ts), name="ffn_act_bwd1",
        in_specs=[cur, prev, cur, prev, cur, par(8, 0), par(8, nf), par(1, 0), par(1, nf)],
        out_specs=(cur, cur, acc3, acc3, acc1, acc1),
        compiler_params=_cp("arbitrary", "arbitrary", "arbitrary"))(uv, uv, ug, ug, da, dw8, dw8, b, b)


def _ffn_act_bwd2(dvv, dvg, dw8):
    bsz, s_len, f = dvv.shape
    tc, ts = _tile(f, 256, LANES), _tile(s_len, 1024, FFN_HALO)
    nf, r, ns = f // tc, ts // FFN_HALO, s_len // ts

    def body(vc_ref, vn_ref, gc_ref, gn_ref, wv_ref, wg_ref, duv_ref, dug_ref):
        s = pl.program_id(2)
        for dc_ref, dn_ref, w_ref, du_ref in ((vc_ref, vn_ref, wv_ref, duv_ref), (gc_ref, gn_ref, wg_ref, dug_ref)):
            d = dc_ref[...].astype(F32)
            head = jnp.where(s < ns - 1, dn_ref[...].astype(F32), 0.0)
            sh = _shifts_fwd(d, head, 3)
            du_ref[...] = (sh[0] * w_ref[2:3, :] + sh[1] * w_ref[1:2, :] + sh[2] * w_ref[0:1, :]).astype(BF16)

    cur = pl.BlockSpec((None, ts, tc), lambda bi, j, s: (bi, s, j))
    nxt = pl.BlockSpec((None, FFN_HALO, tc),
                       lambda bi, j, s: (bi, jnp.minimum((s + 1) * r, s_len // FFN_HALO - 1), j))

    def par(off):
        return pl.BlockSpec((8, tc), lambda bi, j, s: (0, j + off))

    return pl.pallas_call(
        body, out_shape=(SDS((bsz, s_len, f), BF16),) * 2, grid=(bsz, nf, ns), name="ffn_act_bwd2",
        in_specs=[cur, nxt, cur, nxt, par(0), par(nf)], out_specs=(cur, cur),
        compiler_params=_cp("parallel", "parallel", "arbitrary"))(dvv, dvv, dvg, dvg, dw8, dw8)


CONV_HALO = 32
CONV_CHUNK = 256


def _conv_act_fwd(p, dw32, dwb, ln_g, ln_b):
    bsz, s_len, d2 = p.shape
    d = d2 // 2
    kw = 31
    ts = _tile(s_len, 256, CONV_HALO)
    r = ts // CONV_HALO
    cc = min(CONV_CHUNK, d)

    def body(pc_ref, pp_ref, w_ref, wb_ref, g_ref, b_ref, u_ref, s_ref):
        s = pl.program_id(1)
        tot = jnp.zeros((ts, 1), F32)
        for c0 in range(0, d, cc):
            a = pc_ref[:, c0:c0 + cc].astype(F32)
            g = pc_ref[:, d + c0:d + c0 + cc].astype(F32)
            z = a * _sigmoid(g)
            ap = pp_ref[:, c0:c0 + cc].astype(F32)
            gp = pp_ref[:, d + c0:d + c0 + cc].astype(F32)
            tail = jnp.where(s > 0, ap * _sigmoid(gp), 0.0)
            sh = _shifts_back(z, tail, kw)
            acc = wb_ref[:, c0:c0 + cc] + sh[0] * w_ref[kw - 1:kw, c0:c0 + cc]
            for j in range(1, kw):
                acc = acc + sh[j] * w_ref[kw - 1 - j:kw - j, c0:c0 + cc]
            u_ref[:, c0:c0 + cc] = acc
            tot = tot + jnp.sum(acc, axis=-1, keepdims=True)
        u = u_ref[...]
        mu = tot / d
        uc = u - mu
        var = jnp.mean(uc * uc, axis=-1, keepdims=True)
        ul = uc * lax.rsqrt(var + EPS) * g_ref[...] + b_ref[...]
        s_ref[...] = (ul * _sigmoid(ul)).astype(BF16)

    vec = pl.BlockSpec((1, d), lambda bi, s: (0, 0))
    return pl.pallas_call(
        body, out_shape=(SDS((bsz, s_len, d), F32), SDS((bsz, s_len, d), BF16)), grid=(bsz, s_len // ts),
        name="conv_act_fwd",
        in_specs=[pl.BlockSpec((None, ts, d2), lambda bi, s: (bi, s, 0)),
                  pl.BlockSpec((None, CONV_HALO, d2), lambda bi, s: (bi, jnp.maximum(s * r - 1, 0), 0)),
                  pl.BlockSpec((32, d), lambda bi, s: (0, 0)), vec, vec, vec],
        out_specs=(pl.BlockSpec((None, ts, d), lambda bi, s: (bi, s, 0)),
                   pl.BlockSpec((None, ts, d), lambda bi, s: (bi, s, 0))),
        compiler_params=_cp("parallel", "arbitrary"))(p, p, dw32, dwb, ln_g, ln_b)


def _conv_act_bwd1(u, ds, ln_g, ln_b):
    t, d = u.shape
    ts = _tile(t, 256)

    def body(u_ref, ds_ref, g_ref, b_ref, du_ref, dg_ref, db_ref, dwb_ref):
        @pl.when(pl.program_id(0) == 0)
        def _():
            dg_ref[...] = jnp.zeros_like(dg_ref)
            db_ref[...] = jnp.zeros_like(db_ref)
            dwb_ref[...] = jnp.zeros_like(dwb_ref)

        uv = u_ref[...]
        uc = uv - jnp.mean(uv, axis=-1, keepdims=True)
        rstd = lax.rsqrt(jnp.mean(uc * uc, axis=-1, keepdims=True) + EPS)
        uh = uc * rstd
        ul = uh * g_ref[...] + b_ref[...]
        sg = _sigmoid(ul)
        dul = ds_ref[...].astype(F32) * (sg * (1.0 + ul * (1.0 - sg)))
        duh = dul * g_ref[...]
        du = rstd * (duh - jnp.mean(duh, axis=-1, keepdims=True) - uh * jnp.mean(duh * uh, axis=-1, keepdims=True))
        du_ref[...] = du
        dg_ref[...] += _fold8(dul * uh)
        db_ref[...] += _fold8(dul)
        dwb_ref[...] += _fold8(du)

    row = pl.BlockSpec((ts, d), lambda i: (i, 0))
    vec = pl.BlockSpec((1, d), lambda i: (0, 0))
    acc = pl.BlockSpec((8, d), lambda i: (0, 0))
    return pl.pallas_call(
        body, out_shape=(SDS((t, d), F32), SDS((8, d), F32), SDS((8, d), F32), SDS((8, d), F32)), grid=(t // ts,),
        name="conv_act_bwd1", in_specs=[row, row, vec, vec], out_specs=(row, acc, acc, acc),
        compiler_params=_cp("arbitrary"))(u, ds, ln_g, ln_b)


def _conv_act_bwd2(du, p, dw32):
    bsz, s_len, d2 = p.shape
    d = d2 // 2
    kw = 31
    ts = _tile(s_len, 256, CONV_HALO)
    r, ns = ts // CONV_HALO, s_len // ts
    cc = min(CONV_CHUNK, d)

    def body(dc_ref, dn_ref, pc_ref, pp_ref, w_ref, dp_ref, ddw_ref):
        bi, s = pl.program_id(0), pl.program_id(1)

        @pl.when((bi == 0) & (s == 0))
        def _():
            ddw_ref[...] = jnp.zeros_like(ddw_ref)

        for c0 in range(0, d, cc):
            a = pc_ref[:, c0:c0 + cc].astype(F32)
            g = pc_ref[:, d + c0:d + c0 + cc].astype(F32)
            sg = _sigmoid(g)
            z = a * sg
            ap = pp_ref[:, c0:c0 + cc].astype(F32)
            gp = pp_ref[:, d + c0:d + c0 + cc].astype(F32)
            tail = jnp.where(s > 0, ap * _sigmoid(gp), 0.0)
            duv = dc_ref[:, c0:c0 + cc]
            head = jnp.where(s < ns - 1, dn_ref[:, c0:c0 + cc], 0.0)
            zb = _shifts_back(z, tail, kw)
            for k in range(kw):
                ddw_ref[k, :, c0:c0 + cc] += _fold8(duv * zb[kw - 1 - k])
            df = _shifts_fwd(duv, head, kw)
            dz = df[0] * w_ref[kw - 1:kw, c0:c0 + cc]
            for j in range(1, kw):
                dz = dz + df[j] * w_ref[kw - 1 - j:kw - j, c0:c0 + cc]
            dp_ref[:, c0:c0 + cc] = (dz * sg).astype(BF16)
            dp_ref[:, d + c0:d + c0 + cc] = (dz * a * sg * (1.0 - sg)).astype(BF16)

    return pl.pallas_call(
        body, out_shape=(SDS((bsz, s_len, d2), BF16), SDS((32, 8, d), F32)), grid=(bsz, ns), name="conv_act_bwd2",
        in_specs=[pl.BlockSpec((None, ts, d), lambda bi, s: (bi, s, 0)),
                  pl.BlockSpec((None, CONV_HALO, d),
                               lambda bi, s: (bi, jnp.minimum((s + 1) * r, s_len // CONV_HALO - 1), 0)),
                  pl.BlockSpec((None, ts, d2), lambda bi, s: (bi, s, 0)),
                  pl.BlockSpec((None, CONV_HALO, d2), lambda bi, s: (bi, jnp.maximum(s * r - 1, 0), 0)),
                  pl.BlockSpec((32, d), lambda bi, s: (0, 0))],
        out_specs=(pl.BlockSpec((None, ts, d2), lambda bi, s: (bi, s, 0)),
                   pl.BlockSpec((32, 8, d), lambda bi, s: (0, 0, 0))),
        compiler_params=_cp("arbitrary", "arbitrary"))(du, du, p, p, dw32)


POOL_HALO = 16


def _pool_counts(s, ts, rows, w):
    t = s * ts + lax.broadcasted_iota(jnp.int32, (rows, 1), 0)
    return jnp.minimum(t + 1, w).astype(F32)


def _pool_fwd(x, gmix, w, b, scale):
    bsz, s_len, d = x.shape
    ng = len(POOL_WINDOWS)
    cg = d // ng
    ts = _tile(s_len, 512, POOL_HALO)
    r = ts // POOL_HALO

    def body(xc_ref, xp_ref, g_ref, w_ref, b_ref, sc_ref, y_ref, p_ref):
        s = pl.program_id(1)

        def norm(v):
            return v * lax.rsqrt(jnp.mean(v * v, axis=-1, keepdims=True) + EPS) * g_ref[...]

        xc = xc_ref[...]
        h = norm(xc)
        tail = jnp.where(s > 0, norm(xp_ref[...]), 0.0)
        for gi, win in enumerate(POOL_WINDOWS):
            lo, hi = gi * cg, (gi + 1) * cg
            hg = h[:, lo:hi]
            acc = jnp.concatenate([tail[:, lo:hi], hg], axis=0)
            step = 1
            while step < win:
                acc = acc + pltpu.roll(acc, step, axis=0)
                step *= 2
            pg = acc[POOL_HALO:] / _pool_counts(s, ts, ts, win) - hg
            pb = pg.astype(BF16)
            p_ref[:, lo:hi] = pb
            yg = jnp.dot(pb, w_ref[gi], preferred_element_type=F32) + b_ref[:, lo:hi]
            y_ref[:, lo:hi] = xc[:, lo:hi] + yg * sc_ref[:, lo:hi]

    vec = pl.BlockSpec((1, d), lambda bi, s: (0, 0))
    blk = pl.BlockSpec((None, ts, d), lambda bi, s: (bi, s, 0))
    return pl.pallas_call(
        body, out_shape=(SDS((bsz, s_len, d), F32), SDS((bsz, s_len, d), BF16)), grid=(bsz, s_len // ts),
        name="pool_fwd",
        in_specs=[blk, pl.BlockSpec((None, POOL_HALO, d), lambda bi, s: (bi, jnp.maximum(s * r - 1, 0), 0)),
                  vec, pl.BlockSpec((ng, cg, cg), lambda bi, s: (0, 0, 0)), vec, vec],
        out_specs=(blk, blk), compiler_params=_cp("parallel", "arbitrary"))(x, x, gmix, w, b, scale)


def _pool_bwd(x, dy, p, gmix, w, b, scale):
    bsz, s_len, d = x.shape
    ng = len(POOL_WINDOWS)
    cg = d // ng
    ts = _tile(s_len, 512, POOL_HALO)
    r, ns = ts // POOL_HALO, s_len // ts
    nt = (((1,), (1,)), ((), ()))
    tn = (((0,), (0,)), ((), ()))

    def body(x_ref, dy_ref, dyn_ref, p_ref, g_ref, w_ref, b_ref, sc_ref, dx_ref, dw_ref, db_ref, dsc_ref, dg_ref):
        bi, s = pl.program_id(0), pl.program_id(1)

        @pl.when((bi == 0) & (s == 0))
        def _():
            dw_ref[...] = jnp.zeros_like(dw_ref)
            db_ref[...] = jnp.zeros_like(db_ref)
            dsc_ref[...] = jnp.zeros_like(dsc_ref)
            dg_ref[...] = jnp.zeros_like(dg_ref)

        dy = dy_ref[...]
        dyy = dy * sc_ref[...]
        dyy_n = jnp.where(s < ns - 1, dyn_ref[...] * sc_ref[...], 0.0)
        db_ref[...] += _fold8(dyy)
        xv = x_ref[...]
        rr = lax.rsqrt(jnp.mean(xv * xv, axis=-1, keepdims=True) + EPS)
        xh = xv * rr
        for gi, win in enumerate(POOL_WINDOWS):
            lo, hi = gi * cg, (gi + 1) * cg
            pb = p_ref[:, lo:hi]
            wg = w_ref[gi]
            pre = jnp.dot(pb, wg, preferred_element_type=F32) + b_ref[:, lo:hi]
            dsc_ref[:, lo:hi] += _fold8(dy[:, lo:hi] * pre)
            dyb = dyy[:, lo:hi].astype(BF16)
            dw_ref[gi] += lax.dot_general(pb, dyb, tn, preferred_element_type=F32)
            dp = lax.dot_general(dyb, wg, nt, preferred_element_type=F32)
            dp_n = lax.dot_general(dyy_n[:, lo:hi].astype(BF16), wg, nt, preferred_element_type=F32)
            q = dp / _pool_counts(s, ts, ts, win)
            q_n = dp_n / _pool_counts(s + 1, ts, POOL_HALO, win)
            acc = jnp.concatenate([q, q_n], axis=0)
            ln = ts + POOL_HALO
            step = 1
            while step < win:
                acc = acc + pltpu.roll(acc, ln - step, axis=0)
                step *= 2
            dh = acc[:ts] - dp
            xhg = xh[:, lo:hi]
            dg_ref[:, lo:hi] += _fold8(dh * xhg)
            dx_ref[:, lo:hi] = dh * g_ref[:, lo:hi]
        u = dx_ref[...]
        dx_ref[...] = dy + rr * (u - xh * jnp.mean(u * xh, axis=-1, keepdims=True))

    vec = pl.BlockSpec((1, d), lambda bi, s: (0, 0))
    acc8 = pl.BlockSpec((8, d), lambda bi, s: (0, 0))
    blk = pl.BlockSpec((None, ts, d), lambda bi, s: (bi, s, 0))
    wspec = pl.BlockSpec((ng, cg, cg), lambda bi, s: (0, 0, 0))
    return pl.pallas_call(
        body, out_shape=(SDS((bsz, s_len, d), F32), SDS((ng, cg, cg), F32), SDS((8, d), F32), SDS((8, d), F32),
                         SDS((8, d), F32)),
        grid=(bsz, ns), name="pool_bwd",
        in_specs=[blk, blk,
                  pl.BlockSpec((None, POOL_HALO, d),
                               lambda bi, s: (bi, jnp.minimum((s + 1) * r, s_len // POOL_HALO - 1), 0)),
                  blk, vec, wspec, vec, vec],
        out_specs=(blk, wspec, acc8, acc8, acc8),
        compiler_params=_cp("arbitrary", "arbitrary"))(x, dy, dy, p, gmix, w, b, scale)


def _tri(n, upper):
    row = lax.broadcasted_iota(jnp.int32, (n, n), 0)
    col = lax.broadcasted_iota(jnp.int32, (n, n), 1)
    return jnp.where((col >= row) if upper else (col <= row), 1.0, 0.0).astype(F32)


def _fox_gate_fwd(proj, bf, n_heads):
    bsz, s_len, width = proj.shape
    col = width // LANES - 1
    ts = _tile(s_len, 512)

    def body(fl_ref, b_ref, c_ref, carry_ref):
        @pl.when(pl.program_id(1) == 0)
        def _():
            carry_ref[...] = jnp.zeros_like(carry_ref)

        xv = fl_ref[...] + b_ref[...]
        logf = jnp.minimum(xv, 0.0) - jnp.log(1.0 + jnp.exp(-jnp.abs(xv)))
        lane = lax.broadcasted_iota(jnp.int32, (1, LANES), 1)
        logf = jnp.where(lane < n_heads, logf, 0.0)
        c = _dot_hi(_tri(ts, False), logf) + carry_ref[0:1, :]
        c_ref[...] = c
        carry_ref[0:1, :] = c[ts - 1:ts, :]

    return pl.pallas_call(
        body, out_shape=SDS((bsz, s_len, LANES), F32), grid=(bsz, s_len // ts), name="fox_gate_fwd",
        in_specs=[pl.BlockSpec((None, ts, LANES), lambda bi, s: (bi, s, col)),
                  pl.BlockSpec((1, LANES), lambda bi, s: (0, 0))],
        out_specs=pl.BlockSpec((None, ts, LANES), lambda bi, s: (bi, s, 0)),
        scratch_shapes=[pltpu.VMEM((8, LANES), F32)],
        compiler_params=_cp("arbitrary", "arbitrary"))(proj, bf)


def _fox_gate_bwd(dc, proj, bf, n_heads):
    bsz, s_len, width = proj.shape
    col = width // LANES - 1
    ts = _tile(s_len, 512)
    ns = s_len // ts

    def body(dc_ref, fl_ref, b_ref, dfl_ref, db_ref, carry_ref):
        bi, s = pl.program_id(0), pl.program_id(1)

        @pl.when((bi == 0) & (s == 0))
        def _():
            db_ref[...] = jnp.zeros_like(db_ref)

        @pl.when(s == 0)
        def _():
            carry_ref[...] = jnp.zeros_like(carry_ref)

        dlogf = _dot_hi(_tri(ts, True), dc_ref[...]) + carry_ref[0:1, :]
        carry_ref[0:1, :] = dlogf[0:1, :]
        lane = lax.broadcasted_iota(jnp.int32, (1, LANES), 1)
        dfl = jnp.where(lane < n_heads, dlogf * (1.0 - _sigmoid(fl_ref[...] + b_ref[...])), 0.0)
        dfl_ref[...] = dfl.astype(BF16)
        db_ref[...] += _fold8(dfl)

    return pl.pallas_call(
        body, out_shape=(SDS((bsz, s_len, LANES), BF16), SDS((8, LANES), F32)), grid=(bsz, ns), name="fox_gate_bwd",
        in_specs=[pl.BlockSpec((None, ts, LANES), lambda bi, s: (bi, ns - 1 - s, 0)),
                  pl.BlockSpec((None, ts, LANES), lambda bi, s: (bi, ns - 1 - s, col)),
                  pl.BlockSpec((1, LANES), lambda bi, s: (0, 0))],
        out_specs=(pl.BlockSpec((None, ts, LANES), lambda bi, s: (bi, ns - 1 - s, 0)),
                   pl.BlockSpec((8, LANES), lambda bi, s: (0, 0))),
        scratch_shapes=[pltpu.VMEM((8, LANES), F32)],
        compiler_params=_cp("arbitrary", "arbitrary"))(dc, proj, bf)


def _head_maps(d):
    ch = lax.broadcasted_iota(jnp.int32, (d, LANES), 0) // HEAD_DIM
    hd = lax.broadcasted_iota(jnp.int32, (d, LANES), 1)
    e = jnp.where(ch == hd, 1.0, 0.0).astype(BF16)
    cht = lax.broadcasted_iota(jnp.int32, (LANES, d), 1) // HEAD_DIM
    hdt = lax.broadcasted_iota(jnp.int32, (LANES, d), 0)
    et = jnp.where(cht == hdt, 1.0, 0.0).astype(BF16)
    return e, et


def _dot_sel(x, e):
    a = x.astype(BF16)
    r = x - a.astype(F32)
    b = r.astype(BF16)
    c = (r - b.astype(F32)).astype(BF16)
    return (jnp.dot(a, e, preferred_element_type=F32) + jnp.dot(b, e, preferred_element_type=F32)
            + jnp.dot(c, e, preferred_element_type=F32))


def _fox_qknorm_fwd(proj, gq, gk, d):
    t = proj.shape[0]
    ts = _tile(t, 256)
    scale = 1.0 / math.sqrt(HEAD_DIM)

    def body(q_ref, k_ref, v_ref, gq_ref, gk_ref, qn_ref, kn_ref, vb_ref):
        e, et = _head_maps(d)

        def norm(v, g):
            r = lax.rsqrt(_dot_sel(v * v, e) / HEAD_DIM + EPS)
            return v * _dot_sel(r, et) * g

        qn_ref[...] = (norm(q_ref[...], gq_ref[...]) * scale).astype(BF16)
        kn_ref[...] = norm(k_ref[...], gk_ref[...]).astype(BF16)
        vb_ref[...] = v_ref[...].astype(BF16)

    def colblk(j):
        return pl.BlockSpec((ts, d), lambda i: (i, j))

    vec = pl.BlockSpec((1, d), lambda i: (0, 0))
    out = pl.BlockSpec((ts, d), lambda i: (i, 0))
    return pl.pallas_call(
        body, out_shape=(SDS((t, d), BF16),) * 3, grid=(t // ts,), name="fox_qknorm_fwd",
        in_specs=[colblk(0), colblk(1), colblk(2), vec, vec], out_specs=(out, out, out),
        compiler_params=_cp("parallel"))(proj, proj, proj, gq, gk)


def _fox_qknorm_bwd(proj, dq, dk, dv, gq, gk, d):
    t = proj.shape[0]
    ts = _tile(t, 256)
    scale = 1.0 / math.sqrt(HEAD_DIM)

    def body(q_ref, k_ref, dq_ref, dk_ref, dv_ref, gq_ref, gk_ref, dp_ref, dgq_ref, dgk_ref):
        @pl.when(pl.program_id(0) == 0)
        def _():
            dgq_ref[...] = jnp.zeros_like(dgq_ref)
            dgk_ref[...] = jnp.zeros_like(dgk_ref)

        e, et = _head_maps(d)

        def back(v, g, dn, dg_ref):
            r = _dot_sel(lax.rsqrt(_dot_sel(v * v, e) / HEAD_DIM + EPS), et)
            vh = v * r
            dg_ref[...] += _fold8(dn * vh)
            u = dn * g
            mh = _dot_sel(_dot_sel(u * vh, e) / HEAD_DIM, et)
            return r * (u - vh * mh)

        dp_ref[:, 0:d] = back(q_ref[...], gq_ref[...], dq_ref[...] * scale, dgq_ref).astype(BF16)
        dp_ref[:, d:2 * d] = back(k_ref[...], gk_ref[...], dk_ref[...], dgk_ref).astype(BF16)
        dp_ref[:, 2 * d:3 * d] = dv_ref[...]

    def colblk(j):
        return pl.BlockSpec((ts, d), lambda i: (i, j))

    row = pl.BlockSpec((ts, d), lambda i: (i, 0))
    vec = pl.BlockSpec((1, d), lambda i: (0, 0))
    acc = pl.BlockSpec((8, d), lambda i: (0, 0))
    return pl.pallas_call(
        body, out_shape=(SDS((t, 3 * d), BF16), SDS((8, d), F32), SDS((8, d), F32)), grid=(t // ts,),
        name="fox_qknorm_bwd", in_specs=[colblk(0), colblk(1), row, row, row, vec, vec],
        out_specs=(pl.BlockSpec((ts, 3 * d), lambda i: (i, 0)), acc, acc),
        compiler_params=_cp("arbitrary"))(proj, proj, dq, dk, dv, gq, gk)


ATT_BLOCK = 512
_NT = (((1,), (1,)), ((), ()))
_TN = (((0,), (0,)), ((), ()))


def _head_mask(h):
    return (lax.broadcasted_iota(jnp.int32, (1, LANES), 1) // HEAD_DIM) == h


def _causal(qi, ki, tq, tk):
    row = qi * tq + lax.broadcasted_iota(jnp.int32, (tq, 1), 0)
    col = ki * tk + lax.broadcasted_iota(jnp.int32, (1, tk), 1)
    return col <= row


def _flash_fwd(q, k, v, crow):
    bsz, s_len, d = q.shape
    nj = d // LANES
    tq = tk = _tile(s_len, ATT_BLOCK, LANES)
    nq = s_len // tq

    def body(q_ref, k_ref, v_ref, c_ref, o_ref, lse_ref, m_ref, l_ref, acc_ref):
        qi, ki = pl.program_id(2), pl.program_id(3)

        @pl.when(ki == 0)
        def _():
            m_ref[...] = jnp.full_like(m_ref, NEG)
            l_ref[...] = jnp.zeros_like(l_ref)
            acc_ref[...] = jnp.zeros_like(acc_ref)

        def step(masked):
            qv, kv, vv = q_ref[...], k_ref[...], v_ref[...]
            for h in range(2):
                qh = jnp.where(_head_mask(h), qv, jnp.zeros_like(qv))
                s = lax.dot_general(qh, kv, _NT, preferred_element_type=F32) - c_ref[h:h + 1, :]
                if masked:
                    s = jnp.where(_causal(qi, ki, tq, tk), s, NEG)
                m_prev = m_ref[h]
                m_new = jnp.maximum(m_prev, jnp.max(s, axis=1, keepdims=True))
                pm = jnp.exp(s - m_new)
                alpha = jnp.exp(m_prev - m_new)
                l_ref[h] = alpha * l_ref[h] + jnp.sum(pm, axis=1, keepdims=True)
                p_hi = pm.astype(BF16)
                p_lo = (pm - p_hi.astype(F32)).astype(BF16)
                acc_ref[h] = (alpha * acc_ref[h] + jnp.dot(p_hi, vv, preferred_element_type=F32)
                              + jnp.dot(p_lo, vv, preferred_element_type=F32))
                m_ref[h] = m_new

        pl.when(ki < qi)(functools.partial(step, False))
        pl.when(ki == qi)(functools.partial(step, True))

        @pl.when(ki == qi)
        def _():
            m0 = _head_mask(0)
            o_ref[...] = jnp.where(m0, acc_ref[0] / l_ref[0], acc_ref[1] / l_ref[1])
            lse_ref[...] = jnp.where(m0, m_ref[0] + jnp.log(l_ref[0]), m_ref[1] + jnp.log(l_ref[1]))

    return pl.pallas_call(
        body, out_shape=(SDS((bsz, s_len, d), F32), SDS((bsz, nj, s_len, LANES), F32)), grid=(bsz, nj, nq, nq),
        name="flash_fwd",
        in_specs=[pl.BlockSpec((None, tq, LANES), lambda bi, j, qi, ki: (bi, qi, j)),
                  pl.BlockSpec((None, tk, LANES), lambda bi, j, qi, ki: (bi, jnp.minimum(ki, qi), j)),
                  pl.BlockSpec((None, tk, LANES), lambda bi, j, qi, ki: (bi, jnp.minimum(ki, qi), j)),
                  pl.BlockSpec((None, None, 2, tk), lambda bi, j, qi, ki: (bi, j, 0, jnp.minimum(ki, qi)))],
        out_specs=(pl.BlockSpec((None, tq, LANES), lambda bi, j, qi, ki: (bi, qi, j)),
                   pl.BlockSpec((None, None, tq, LANES), lambda bi, j, qi, ki: (bi, j, qi, 0))),
        scratch_shapes=[pltpu.VMEM((2, tq, 1), F32), pltpu.VMEM((2, tq, 1), F32), pltpu.VMEM((2, tq, LANES), F32)],
        compiler_params=_cp("parallel", "parallel", "arbitrary", "arbitrary"))(q, k, v, crow)


def _flash_probs(qv, kv, vv, dov, ov, lse, c_ref, h, mask):
    hm = _head_mask(h)
    qh = jnp.where(hm, qv, jnp.zeros_like(qv))
    s = lax.dot_general(qh, kv, _NT, preferred_element_type=F32) - c_ref[h:h + 1, :]
    pm = jnp.exp(s - lse[:, h * HEAD_DIM:h * HEAD_DIM + 1])
    if mask is not None:
        pm = jnp.where(mask, pm, 0.0)
    doh = jnp.where(hm, dov, jnp.zeros_like(dov))
    dpm = lax.dot_general(doh, vv, _NT, preferred_element_type=F32)
    delta = jnp.sum(jnp.where(hm, dov.astype(F32) * ov, 0.0), axis=1, keepdims=True)
    return pm, pm * (dpm - delta)


def _flash_bwd(q, k, v, do, o, lse, crow):
    bsz, s_len, d = q.shape
    nj = d // LANES
    tq = tk = _tile(s_len, ATT_BLOCK, LANES)
    nq = s_len // tq

    def body(q_ref, k_ref, v_ref, do_ref, o_ref, lse_ref, c_ref, dq_ref, dk_ref, dv_ref, dc_ref,
             dqa_ref, dka_ref, dva_ref, dca_ref):
        ki, qi = pl.program_id(2), pl.program_id(3)
        rows = pl.ds(pl.multiple_of(qi * tq, tq), tq)

        @pl.when((ki == 0) & (qi == 0))
        def _():
            dqa_ref[...] = jnp.zeros_like(dqa_ref)

        @pl.when(qi == ki)
        def _():
            dka_ref[...] = jnp.zeros_like(dka_ref)
            dva_ref[...] = jnp.zeros_like(dva_ref)
            dca_ref[...] = jnp.zeros_like(dca_ref)

        def step(masked):
            qv, kv, vv, dov, ov, lse = q_ref[...], k_ref[...], v_ref[...], do_ref[...], o_ref[...], lse_ref[...]
            mask = _causal(qi, ki, tq, tk) if masked else None
            for h in range(2):
                pm, ds = _flash_probs(qv, kv, vv, dov, ov, lse, c_ref, h, mask)
                dsb = ds.astype(BF16)
                dva_ref[h] += lax.dot_general(pm.astype(BF16), dov, _TN, preferred_element_type=F32)
                dka_ref[h] += lax.dot_general(dsb, qv, _TN, preferred_element_type=F32)
                dqa_ref[h, rows, :] += jnp.dot(dsb, kv, preferred_element_type=F32)
                dca_ref[h:h + 1, :] -= jnp.sum(ds, axis=0, keepdims=True)

        pl.when(qi > ki)(functools.partial(step, False))
        pl.when(qi == ki)(functools.partial(step, True))

        @pl.when(qi == nq - 1)
        def _():
            m0 = _head_mask(0)
            dk_ref[...] = jnp.where(m0, dka_ref[0], dka_ref[1])
            dv_ref[...] = jnp.where(m0, dva_ref[0], dva_ref[1]).astype(BF16)
            dc_ref[...] = dca_ref[0:2, :]

        @pl.when(ki == nq - 1)
        def _():
            dq_ref[...] = jnp.where(_head_mask(0), dqa_ref[0, rows, :], dqa_ref[1, rows, :])

    def qside(bi, j, ki, qi):
        return (bi, jnp.maximum(qi, ki), j)

    def kside(bi, j, ki, qi):
        return (bi, ki, j)

    def dqside(bi, j, ki, qi):
        return (bi, jnp.where(ki == nq - 1, qi, 0), j)

    return pl.pallas_call(
        body, out_shape=(SDS((bsz, s_len, d), F32), SDS((bsz, s_len, d), F32), SDS((bsz, s_len, d), BF16),
                         SDS((bsz, nj, 2, s_len), F32)),
        grid=(bsz, nj, nq, nq), name="flash_bwd",
        in_specs=[pl.BlockSpec((None, tq, LANES), qside), pl.BlockSpec((None, tk, LANES), kside),
                  pl.BlockSpec((None, tk, LANES), kside), pl.BlockSpec((None, tq, LANES), qside),
                  pl.BlockSpec((None, tq, LANES), qside),
                  pl.BlockSpec((None, None, tq, LANES), lambda bi, j, ki, qi: (bi, j, jnp.maximum(qi, ki), 0)),
                  pl.BlockSpec((None, None, 2, tk), lambda bi, j, ki, qi: (bi, j, 0, ki))],
        out_specs=(pl.BlockSpec((None, tq, LANES), dqside), pl.BlockSpec((None, tk, LANES), kside),
                   pl.BlockSpec((None, tk, LANES), kside),
                   pl.BlockSpec((None, None, 2, tk), lambda bi, j, ki, qi: (bi, j, 0, ki))),
        scratch_shapes=[pltpu.VMEM((2, s_len, LANES), F32), pltpu.VMEM((2, tk, LANES), F32),
                        pltpu.VMEM((2, tk, LANES), F32), pltpu.VMEM((8, tk), F32)],
        compiler_params=_cp("parallel", "parallel", "arbitrary", "arbitrary"))(q, k, v, do, o, lse, crow)


def _loss_head(y, target):
    t, d = y.shape
    tm = _tile(t, 512)

    def body(y_ref, t_ref, dy_ref, acc_ref):
        @pl.when(pl.program_id(0) == 0)
        def _():
            acc_ref[...] = jnp.zeros_like(acc_ref)

        err = y_ref[...] - t_ref[...]
        dy_ref[...] = err / d
        acc_ref[...] += _fold8(err * err)

    row = pl.BlockSpec((tm, d), lambda i: (i, 0))
    return pl.pallas_call(
        body, out_shape=(SDS((t, d), F32), SDS((8, d), F32)), grid=(t // tm,), name="loss_head",
        in_specs=[row, row], out_specs=(row, pl.BlockSpec((8, d), lambda i: (0, 0))),
        compiler_params=_cp("arbitrary"))(y, target)


ADAM_COLS = 1024


def _adamw(g8, w, m, v):
    shape = w.shape
    cols = shape[-1]
    rows = w.size // cols
    n_parts = g8.shape[0]
    g8, w, m, v = g8.reshape(n_parts, rows, cols), w.reshape(rows, cols), m.reshape(rows, cols), v.reshape(rows, cols)
    tr = _tile(rows, 256, 16)
    c1 = 1.0 - ADAM_B1 ** ADAM_STEP
    c2 = 1.0 - ADAM_B2 ** ADAM_STEP

    def body(g8_ref, w_ref, m_ref, v_ref, g_ref, d_ref, nm_ref, nv_ref):
        g = g8_ref[0].astype(F32)
        for i in range(1, n_parts):
            g = g + g8_ref[i].astype(F32)
        mn = ADAM_B1 * m_ref[...] + (1.0 - ADAM_B1) * g
        vn = ADAM_B2 * v_ref[...] + (1.0 - ADAM_B2) * (g * g)
        g_ref[...] = g
        nm_ref[...] = mn
        nv_ref[...] = vn
        d_ref[...] = -ADAM_LR * ((mn / c1) / (jnp.sqrt(vn / c2) + ADAM_EPS) + ADAM_WD * w_ref[...])

    blk = pl.BlockSpec((tr, cols), lambda i: (i, 0))
    outs = pl.pallas_call(
        body, out_shape=(SDS((rows, cols), F32),) * 4, grid=(rows // tr,), name="adamw",
        in_specs=[pl.BlockSpec((n_parts, tr, cols), lambda i: (0, i, 0)), blk, blk, blk], out_specs=(blk,) * 4,
        compiler_params=_cp("parallel"))(g8, w, m, v)
    return [o.reshape(shape) for o in outs]


def _mesh_place():
    x, y, c = lax.axis_index("x"), lax.axis_index("y"), lax.axis_index("c")
    return x, y, c, 4 * x + 2 * y + c


def _gather(shards):
    n = len(shards)

    def body(*refs):
        ins, outs = refs[:n], refs[n:2 * n]
        send_sems, recv_sems, local_sems = refs[2 * n:]
        x, y, c, me = _mesh_place()
        sibling = (x, y, 1 - c)
        chips = [(1 - x, y), (x, 1 - y), (1 - x, 1 - y)]

        def block(px, py, pc):
            return 4 * px + 2 * py + pc

        def copy(t, k, blk, to, src=None):
            return pltpu.make_async_remote_copy(
                src_ref=outs[t].at[blk] if src is None else src, dst_ref=outs[t].at[blk],
                send_sem=send_sems.at[t, k], recv_sem=recv_sems.at[t, k], device_id=to,
                device_id_type=pl.DeviceIdType.MESH)

        own = [pltpu.make_async_copy(ins[t], outs[t].at[me], local_sems.at[t]) for t in range(n)]
        first = []
        for t in range(n):
            own[t].start()
            first.append(copy(t, 0, me, sibling, src=ins[t]))
            first += [copy(t, 1 + j, me, (*chip, c), src=ins[t]) for j, chip in enumerate(chips)]
        for cp in first:
            cp.start()
        passed = []
        for j, chip in enumerate(chips):
            for t in range(n):
                copy(t, 1 + j, block(*chip, c), (x, y, c)).wait_recv()
                cp = copy(t, 4 + j, block(*chip, c), sibling)
                cp.start()
                passed.append(cp)
        for t in range(n):
            copy(t, 0, block(x, y, 1 - c), (x, y, c)).wait_recv()
            for j, chip in enumerate(chips):
                copy(t, 4 + j, block(*chip, 1 - c), (x, y, c)).wait_recv()
        for cp in first + passed:
            cp.wait_send()
        for cp in own:
            cp.wait()

    hbm = pl.BlockSpec(memory_space=pl.ANY)
    return pl.pallas_call(
        body, out_shape=[SDS((N_DEV,) + tuple(s.shape), s.dtype) for s in shards], name="gather",
        in_specs=[hbm] * n, out_specs=[hbm] * n,
        scratch_shapes=[pltpu.SemaphoreType.DMA((n, N_DEV - 1)), pltpu.SemaphoreType.DMA((n, N_DEV - 1)),
                        pltpu.SemaphoreType.DMA((n,))])(*shards)


N_CHIP = N_DEV // 2


def _scatter_core(items):
    n = len(items)

    def body(*refs):
        ins, outs = refs[:n], refs[n:2 * n]
        send_sems, recv_sems = refs[2 * n:]
        x, y, c, _ = _mesh_place()
        copies = []
        for it in range(n):
            for r in range(N_CHIP):
                cp = pltpu.make_async_remote_copy(
                    src_ref=ins[it].at[2 * r + 1 - c], dst_ref=outs[it].at[r], send_sem=send_sems.at[it, r],
                    recv_sem=recv_sems.at[it, r], device_id=(x, y, 1 - c), device_id_type=pl.DeviceIdType.MESH)
                cp.start()
                copies.append(cp)
        for cp in copies:
            cp.wait()

    hbm = pl.BlockSpec(memory_space=pl.ANY)
    return pl.pallas_call(
        body, out_shape=[SDS((N_CHIP,) + tuple(a.shape[1:]), a.dtype) for a in items], name="scatter_core",
        in_specs=[hbm] * n, out_specs=[hbm] * n,
        scratch_shapes=[pltpu.SemaphoreType.DMA((n, N_CHIP)), pltpu.SemaphoreType.DMA((n, N_CHIP))])(*items)


def _pair_add(item, other):
    shape = item.shape[1:]
    cols = shape[-1]
    rows = math.prod(shape) // cols
    tr = _tile(rows, 512, 16)

    def body(x_ref, o_ref, h_ref):
        c = lax.axis_index("c")
        mine = jnp.where(c == 0, x_ref[0].astype(F32), x_ref[1].astype(F32))
        h_ref[...] = (mine + o_ref[...].astype(F32)).astype(item.dtype)

    return pl.pallas_call(
        body, out_shape=SDS((N_CHIP, rows, cols), item.dtype), grid=(N_CHIP, rows // tr), name="pair_add",
        in_specs=[pl.BlockSpec((None, 2, tr, cols), lambda r, i: (r, 0, i, 0)),
                  pl.BlockSpec((None, tr, cols), lambda r, i: (r, i, 0))],
        out_specs=pl.BlockSpec((None, tr, cols), lambda r, i: (r, i, 0)),
        compiler_params=_cp("parallel", "parallel"))(
            item.reshape(N_CHIP, 2, rows, cols), other.reshape(N_CHIP, rows, cols)).reshape((N_CHIP,) + shape)


def _scatter_chip(items, groups):
    n = len(items)
    place = {it: (g, l) for g, members in enumerate(groups) for l, it in enumerate(members)}

    def body(*refs):
        ins, outs = refs[:n], refs[n:n + len(groups)]
        send_sems, recv_sems, local_sems = refs[n + len(groups):]
        x, y, c, _ = _mesh_place()
        chip = 2 * x + y
        copies = []
        for it in range(n):
            g, l = place[it]
            own = pltpu.make_async_copy(ins[it].at[chip], outs[g].at[chip, l], local_sems.at[it])
            own.start()
            copies.append(own)
            for kbits in range(1, N_CHIP):
                px = 1 - x if kbits & 2 else x
                py = 1 - y if kbits & 1 else y
                cp = pltpu.make_async_remote_copy(
                    src_ref=ins[it].at[2 * px + py], dst_ref=outs[g].at[chip, l],
                    send_sem=send_sems.at[it, kbits - 1], recv_sem=recv_sems.at[it, kbits - 1],
                    device_id=(px, py, c), device_id_type=pl.DeviceIdType.MESH)
                cp.start()
                copies.append(cp)
        for cp in copies:
            cp.wait()

    hbm = pl.BlockSpec(memory_space=pl.ANY)
    out_shape = [SDS((N_CHIP, len(members)) + tuple(items[members[0]].shape[1:]), items[members[0]].dtype)
                 for members in groups]
    return pl.pallas_call(
        body, out_shape=out_shape, name="scatter_chip", in_specs=[hbm] * n, out_specs=[hbm] * len(groups),
        scratch_shapes=[pltpu.SemaphoreType.DMA((n, N_CHIP - 1)), pltpu.SemaphoreType.DMA((n, N_CHIP - 1)),
                        pltpu.SemaphoreType.DMA((n,))])(*items)


def _scatter(items, groups):
    halves = _scatter_core(items)
    return _scatter_chip([_pair_add(a, h) for a, h in zip(items, halves)], groups)


def _cat_lanes(g, layer, nb, blk, width):
    _, _, rows, c = g.shape
    tr = _tile(rows, 256, 16)

    def body(g_ref, o_ref):
        for p in range(nb):
            o_ref[:, p * c:(p + 1) * c] = g_ref[p]
        if width > nb * c:
            o_ref[:, nb * c:] = jnp.zeros((tr, width - nb * c), g.dtype)

    return pl.pallas_call(
        body, out_shape=SDS((rows, width), g.dtype), grid=(rows // tr,), name="cat_lanes",
        in_specs=[pl.BlockSpec((nb, None, tr, c), lambda i: (blk, layer, i, 0))],
        out_specs=pl.BlockSpec((tr, width), lambda i: (i, 0)),
        compiler_params=_cp("parallel"))(g)


def _split_lanes(parts, c):
    rows = parts[0].shape[0]
    counts = [p.shape[1] // c for p in parts]
    tr = _tile(rows, 256, 16)

    def body(*refs):
        o_ref = refs[-1]
        q = 0
        for x_ref, cnt in zip(refs[:-1], counts):
            for p in range(cnt):
                o_ref[q] = x_ref[:, p * c:(p + 1) * c]
                q += 1

    return pl.pallas_call(
        body, out_shape=SDS((sum(counts), rows, c), parts[0].dtype), grid=(rows // tr,), name="split_lanes",
        in_specs=[pl.BlockSpec((tr, p.shape[1]), lambda i: (i, 0)) for p in parts],
        out_specs=pl.BlockSpec((sum(counts), tr, c), lambda i: (0, i, 0)),
        compiler_params=_cp("parallel"))(*parts)


def _unshard(g8, shard_shape, axis):
    full = jnp.moveaxis(g8.reshape((N_DEV,) + tuple(shard_shape)), 0, axis)
    shape = list(shard_shape)
    shape[axis] *= N_DEV
    return full.reshape(shape)


def _to_shards(full, axis):
    shape = list(full.shape)
    shape[axis:axis + 1] = [N_DEV, shape[axis] // N_DEV]
    return jnp.moveaxis(full.reshape(shape), axis, 0).reshape(N_DEV, -1)


SMALL = [n for n in SHARDED if n not in MATRICES]


def _flat_rows(parts):
    flat = jnp.concatenate([p.reshape(-1) for p in parts])
    chunk = 8 * ADAM_COLS
    n = -(-flat.shape[0] // chunk) * chunk
    return jnp.pad(flat, (0, n - flat.shape[0])).reshape(n // ADAM_COLS, ADAM_COLS)


def _prepare_weights(gathered, small, shards):
    wt = {}
    flat = small.reshape(N_DEV, -1)
    off = 0
    for n in SMALL:
        size = shards[n].size
        wt[n] = _unshard(flat[:, off:off + size], shards[n].shape, SHARD_AXIS[n])
        off += size
    for n in ('conv_w_out', 'fox_w_o', 'ffn_w_down'):
        g = gathered[n]
        wt[n] = [g[:, l].reshape(N_DEV * g.shape[2], g.shape[3]) for l in range(g.shape[1])]
    g = gathered['pool_w']
    wt['pool_w'] = [jnp.moveaxis(g[:, l], 0, 1).reshape(g.shape[2], N_DEV * g.shape[3], g.shape[4])
                    for l in range(g.shape[1])]
    g = gathered['conv_w_in']
    wt['conv_w_in'] = [_cat_lanes(g, l, N_DEV, 0, N_DEV * g.shape[3]) for l in range(g.shape[1])]
    g = gathered['fox_w_in']
    wt['fox_w_in'] = [_cat_lanes(g, l, N_DEV, 0, 3 * g.shape[2] + LANES) for l in range(g.shape[1])]
    g = gathered['ffn_w_up']
    half = N_DEV // 2
    wt['ffn_w_up_v'] = [_cat_lanes(g, l, half, 0, half * g.shape[3]) for l in range(g.shape[1])]
    wt['ffn_w_up_g'] = [_cat_lanes(g, l, half, 1, half * g.shape[3]) for l in range(g.shape[1])]
    return wt


def _pad_rows(w, rows):
    return jnp.pad(w, ((0, rows - w.shape[0]), (0, 0)))


def _fold(acc):
    return acc.sum(axis=0)


def _local_step(x, target, wt):
    bsz, s_len, d = x.shape
    t = bsz * s_len
    depth = wt['norm_mix'].shape[0]
    n_heads = d // HEAD_DIM
    f = wt['ffn_w_up_v'][0].shape[1]
    row = lambda a: a.reshape(1, -1)
    grads = {n: {} for n in WEIGHTS}
    saved = []

    xc = x.reshape(t, d)
    for i in range(depth):
        j = i // 3
        kind = i % 3
        sv = {'x_mix': xc}
        gm = row(wt['norm_mix'][i])
        if kind == 0:
            hn = _rmsnorm_fwd(xc, gm)
            p = _mm(hn, wt['conv_w_in'][j], bias=row(wt['conv_b_in'][j]), name="conv_in")
            u, sact = _conv_act_fwd(p.reshape(bsz, s_len, 2 * d), _pad_rows(wt['conv_dw'][j], 32),
                                    row(wt['conv_dw_b'][j]), row(wt['conv_ln_g'][j]), row(wt['conv_ln_b'][j]))
            sact = sact.reshape(t, d)
            xn = _mm(sact, wt['conv_w_out'][j], bias=row(wt['conv_b_out'][j]), residual=xc, out_dtype=F32,
                     name="conv_out")
            sv.update(hn=hn, p=p, u=u.reshape(t, d), sact=sact)
        elif kind == 1:
            xn, pp = _pool_fwd(xc.reshape(bsz, s_len, d), gm, wt['pool_w'][j], row(wt['pool_b'][j]),
                               row(wt['pool_scale'][j]))
            xn = xn.reshape(t, d)
            sv.update(p=pp)
        else:
            hn = _rmsnorm_fwd(xc, gm)
            wp = wt['fox_w_in'][j]
            bf = jnp.pad(wt['fox_b_f'][j], (0, LANES - n_heads)).reshape(1, LANES)
            gq = jnp.tile(wt['fox_q_gain'][j], n_heads).reshape(1, d)
            gk = jnp.tile(wt['fox_k_gain'][j], n_heads).reshape(1, d)
            proj = _mm(hn, wp, out_dtype=F32, name="fox_in")
            c = _fox_gate_fwd(proj.reshape(bsz, s_len, -1), bf, n_heads)
            crow = jnp.swapaxes(c, 1, 2)[:, :n_heads].reshape(bsz, n_heads // 2, 2, s_len)
            qn, kn, vb = _fox_qknorm_fwd(proj, gq, gk, d)
            shp = (bsz, s_len, d)
            o, lse = _flash_fwd(qn.reshape(shp), kn.reshape(shp), vb.reshape(shp), crow)
            o = o.reshape(t, d)
            xn = _mm(o, wt['fox_w_o'][j], residual=xc, out_dtype=F32, name="fox_out")
            sv.update(hn=hn, wp=wp, bf=bf, gq=gq, gk=gk, proj=proj, crow=crow, qn=qn, kn=kn, vb=vb, o=o, lse=lse)
        xc = xn
        sv['x_ffn'] = xc
        hf = _rmsnorm_fwd(xc, row(wt['norm_ffn'][i]))
        shf = (bsz, s_len, f)
        uv = _mm(hf, wt['ffn_w_up_v'][i], name="ffn_up").reshape(shf)
        ug = _mm(hf, wt['ffn_w_up_g'][i], name="ffn_up").reshape(shf)
        dw8 = _pad_rows(wt['ffn_dw'][i], 8)
        af = _ffn_act_fwd(uv, ug, dw8, row(wt['ffn_dw_b'][i])).reshape(t, f)
        xc = _mm(af, wt['ffn_w_down'][i], residual=xc, out_dtype=F32, name="ffn_down")
        sv.update(hf=hf, uv=uv, ug=ug, af=af, dw8=dw8)
        saved.append(sv)

    dx, sq = _loss_head(xc, target.reshape(t, d))

    for i in reversed(range(depth)):
        j = i // 3
        kind = i % 3
        sv = saved[i]
        shf = (bsz, s_len, f)
        da = _mm(dx, wt['ffn_w_down'][i], trans_b=True, name="ffn_down_dgrad")
        gw, _ = _wgrad(sv['af'], dx, name="ffn_down_wgrad")
        grads['ffn_w_down'][i] = gw.reshape(N_DEV, f // N_DEV, d)
        dvv, dvg, ddwv, ddwg, dbv, dbg = _ffn_act_bwd1(sv['uv'], sv['ug'], da.reshape(shf), sv['dw8'],
                                                       row(wt['ffn_dw_b'][i]))
        grads['ffn_dw'][i] = jnp.concatenate([ddwv.sum(axis=1), ddwg.sum(axis=1)], axis=1)
        grads['ffn_dw_b'][i] = jnp.concatenate([_fold(dbv), _fold(dbg)])
        duv, dug = _ffn_act_bwd2(dvv, dvg, sv['dw8'])
        duv, dug = duv.reshape(t, f), dug.reshape(t, f)
        gv, _ = _wgrad(sv['hf'], duv, name="ffn_up_wgrad")
        gg, _ = _wgrad(sv['hf'], dug, name="ffn_up_wgrad")
        grads['ffn_w_up'][i] = _split_lanes([gv, gg], 2 * f // N_DEV)
        dhf = _mm(duv, wt['ffn_w_up_v'][i], trans_b=True, a2=dug, b2=wt['ffn_w_up_g'][i], name="ffn_up_dgrad")
        dx, dg = _rmsnorm_bwd(sv['x_ffn'], row(wt['norm_ffn'][i]), dhf, dx)
        grads['norm_ffn'][i] = _fold(dg)
        gm = row(wt['norm_mix'][i])
        if kind == 0:
            dsact = _mm(dx, wt['conv_w_out'][j], trans_b=True, name="conv_out_dgrad")
            gw, cs = _wgrad(sv['sact'], dx, name="conv_out_wgrad")
            grads['conv_w_out'][j] = gw.reshape(N_DEV, d // N_DEV, d)
            grads['conv_b_out'][j] = _fold(cs)
            du, dlg, dlb, dwb = _conv_act_bwd1(sv['u'], dsact, row(wt['conv_ln_g'][j]), row(wt['conv_ln_b'][j]))
            grads['conv_ln_g'][j], grads['conv_ln_b'][j], grads['conv_dw_b'][j] = _fold(dlg), _fold(dlb), _fold(dwb)
            dp, ddw = _conv_act_bwd2(du.reshape(bsz, s_len, d), sv['p'].reshape(bsz, s_len, 2 * d),
                                     _pad_rows(wt['conv_dw'][j], 32))
            grads['conv_dw'][j] = ddw.sum(axis=1)[:wt['conv_dw'].shape[1]]
            dp = dp.reshape(t, 2 * d)
            gw, cs = _wgrad(sv['hn'], dp, name="conv_in_wgrad")
            grads['conv_w_in'][j] = _split_lanes([gw], 2 * d // N_DEV)
            grads['conv_b_in'][j] = _fold(cs)
            dhn = _mm(dp, wt['conv_w_in'][j], trans_b=True, name="conv_in_dgrad")
            dx, dg = _rmsnorm_bwd(sv['x_mix'], gm, dhn, dx)
            grads['norm_mix'][i] = _fold(dg)
        elif kind == 1:
            shp = (bsz, s_len, d)
            dxn, dwp, dbp, dsc, dg = _pool_bwd(sv['x_mix'].reshape(shp), dx.reshape(shp), sv['p'], gm, wt['pool_w'][j],
                                               row(wt['pool_b'][j]), row(wt['pool_scale'][j]))
            dx = dxn.reshape(t, d)
            ng, cg = dwp.shape[0], dwp.shape[1]
            grads['pool_w'][j] = jnp.moveaxis(dwp.reshape(ng, N_DEV, cg // N_DEV, cg), 1, 0).astype(BF16)
            grads['pool_b'][j] = _fold(dbp).reshape(wt['pool_b'].shape[1:])
            grads['pool_scale'][j] = _fold(dsc)
            grads['norm_mix'][i] = _fold(dg)
        else:
            shp = (bsz, s_len, d)
            do = _mm(dx, wt['fox_w_o'][j], trans_b=True, name="fox_out_dgrad")
            gw, _ = _wgrad(sv['o'], dx, name="fox_out_wgrad")
            grads['fox_w_o'][j] = gw.reshape(N_DEV, d // N_DEV, d)
            fl_args = (sv['qn'].reshape(shp), sv['kn'].reshape(shp), sv['vb'].reshape(shp), do.reshape(shp),
                       sv['o'].reshape(shp), sv['lse'], sv['crow'])
            dq, dk, dv, dcrow = _flash_bwd(*fl_args)
            dc = jnp.swapaxes(dcrow.reshape(bsz, n_heads, s_len), 1, 2)
            dc = jnp.pad(dc, ((0, 0), (0, 0), (0, LANES - n_heads)))
            dfl, dbf = _fox_gate_bwd(dc, sv['proj'].reshape(bsz, s_len, -1), sv['bf'], n_heads)
            grads['fox_b_f'][j] = _fold(dbf)[:n_heads]
            dqkv, dgq, dgk = _fox_qknorm_bwd(sv['proj'], dq.reshape(t, d), dk.reshape(t, d), dv.reshape(t, d),
                                             sv['gq'], sv['gk'], d)
            grads['fox_q_gain'][j] = _fold(dgq).reshape(n_heads, HEAD_DIM).sum(axis=0)
            grads['fox_k_gain'][j] = _fold(dgk).reshape(n_heads, HEAD_DIM).sum(axis=0)
            dproj = jnp.concatenate([dqkv, dfl.reshape(t, LANES)], axis=1)
            dwp, _ = _wgrad(sv['hn'], dproj, name="fox_in_wgrad")
            grads['fox_w_in'][j] = _split_lanes([dwp], (3 * d + n_heads) // N_DEV)
            dhn = _mm(dproj, sv['wp'], trans_b=True, name="fox_in_dgrad")
            dx, dg = _rmsnorm_bwd(sv['x_mix'], gm, dhn, dx)
            grads['norm_mix'][i] = _fold(dg)

    listed = {n: [g[k] for k in sorted(g)] for n, g in grads.items()}
    small = {n: jnp.stack(g) for n, g in listed.items() if n not in MATRICES}
    return sq.sum(), dx.reshape(bsz, s_len, d), small, {n: listed[n] for n in MATRICES}


def _train_step(x, target, w, m, v):
    got = _gather([w[n].astype(BF16) for n in MATRICES] + [_flat_rows([w[n] for n in SMALL])])
    wt = _prepare_weights(dict(zip(MATRICES, got[:-1])), got[-1], w)
    wt.update({n: w[n] for n in REPLICATED})
    sq, grad_x, gsmall, gbig = _local_step(x, target, wt)
    d = x.shape[-1]

    shard_rows = jnp.concatenate([_to_shards(gsmall[n], SHARD_AXIS[n]) for n in SMALL], axis=1)
    rep = jnp.concatenate([gsmall[n].reshape(-1) for n in REPLICATED] + [(0.5 / d) * sq.reshape(1)])
    rows = jnp.concatenate([shard_rows, jnp.broadcast_to(rep, (N_DEV, rep.shape[0]))], axis=1)
    chunk = 8 * ADAM_COLS
    n_all = rows.shape[1]
    n_pad = -(-n_all // chunk) * chunk
    rows = jnp.pad(rows, ((0, 0), (0, n_pad - n_all))).reshape(N_DEV, n_pad // ADAM_COLS, ADAM_COLS)

    items, groups = [], []
    for n in MATRICES:
        groups.append(list(range(len(items), len(items) + len(gbig[n]))))
        items += gbig[n]
    groups.append([len(items)])
    items.append(rows)
    recv = _scatter(items, groups)

    res = [{}, {}, {}, {}]
    for n, r in zip(MATRICES, recv[:-1]):
        for k, o in enumerate(_adamw(r, w[n], m[n], v[n])):
            res[k][n] = o
    order = SMALL + REPLICATED

    def flat(tree):
        parts = jnp.concatenate([tree[n].reshape(-1) for n in order])
        return jnp.pad(parts, (0, n_pad - parts.shape[0])).reshape(n_pad // ADAM_COLS, ADAM_COLS)

    outs = [o.reshape(-1) for o in _adamw(recv[-1].reshape((N_CHIP,) + rows.shape[1:]), flat(w), flat(m), flat(v))]
    off = 0
    for n in order:
        size = w[n].size
        for k in range(4):
            res[k][n] = outs[k][off:off + size].reshape(w[n].shape)
        off += size
    loss = outs[0][n_all - 1]
    return (loss, grad_x, *[res[0][n] for n in WEIGHTS], *[res[1][n] for n in WEIGHTS],
            *[res[2][n] for n in WEIGHTS], *[res[3][n] for n in WEIGHTS])


def kernel(x, norm_mix, norm_ffn, conv_w_in, conv_b_in, conv_dw, conv_dw_b, conv_ln_g, conv_ln_b, conv_w_out, conv_b_out, pool_w, pool_b, pool_scale, fox_w_in, fox_b_f, fox_q_gain, fox_k_gain, fox_w_o, ffn_w_up, ffn_dw, ffn_dw_b, ffn_w_down, loss_target, m_norm_mix, m_norm_ffn, m_conv_w_in, m_conv_b_in, m_conv_dw, m_conv_dw_b, m_conv_ln_g, m_conv_ln_b, m_conv_w_out, m_conv_b_out, m_pool_w, m_pool_b, m_pool_scale, m_fox_w_in, m_fox_b_f, m_fox_q_gain, m_fox_k_gain, m_fox_w_o, m_ffn_w_up, m_ffn_dw, m_ffn_dw_b, m_ffn_w_down, v_norm_mix, v_norm_ffn, v_conv_w_in, v_conv_b_in, v_conv_dw, v_conv_dw_b, v_conv_ln_g, v_conv_ln_b, v_conv_w_out, v_conv_b_out, v_pool_w, v_pool_b, v_pool_scale, v_fox_w_in, v_fox_b_f, v_fox_q_gain, v_fox_k_gain, v_fox_w_o, v_ffn_w_up, v_ffn_dw, v_ffn_dw_b, v_ffn_w_down):
    w = dict(zip(WEIGHTS, (norm_mix, norm_ffn, conv_w_in, conv_b_in, conv_dw, conv_dw_b, conv_ln_g, conv_ln_b, conv_w_out, conv_b_out, pool_w, pool_b, pool_scale, fox_w_in, fox_b_f, fox_q_gain, fox_k_gain, fox_w_o, ffn_w_up, ffn_dw, ffn_dw_b, ffn_w_down)))
    m = dict(zip(WEIGHTS, (m_norm_mix, m_norm_ffn, m_conv_w_in, m_conv_b_in, m_conv_dw, m_conv_dw_b, m_conv_ln_g, m_conv_ln_b, m_conv_w_out, m_conv_b_out, m_pool_w, m_pool_b, m_pool_scale, m_fox_w_in, m_fox_b_f, m_fox_q_gain, m_fox_k_gain, m_fox_w_o, m_ffn_w_up, m_ffn_dw, m_ffn_dw_b, m_ffn_w_down)))
    v = dict(zip(WEIGHTS, (v_norm_mix, v_norm_ffn, v_conv_w_in, v_conv_b_in, v_conv_dw, v_conv_dw_b, v_conv_ln_g, v_conv_ln_b, v_conv_w_out, v_conv_b_out, v_pool_w, v_pool_b, v_pool_scale, v_fox_w_in, v_fox_b_f, v_fox_q_gain, v_fox_k_gain, v_fox_w_o, v_ffn_w_up, v_ffn_dw, v_ffn_dw_b, v_ffn_w_down)))
    return _train_step(x, loss_target, w, m, v)
```

```python
import functools
import math

import jax
import jax.numpy as jnp
from jax import lax
from jax.experimental import pallas as pl
from jax.experimental.pallas import tpu as pltpu

F32, BF16 = jnp.float32, jnp.bfloat16
SDS = jax.ShapeDtypeStruct

N_DEV = 8
EPS = 1e-6
POOL_WINDOWS = (2, 4, 8, 16)
HEAD_DIM = 64
ADAM_LR, ADAM_B1, ADAM_B2, ADAM_EPS, ADAM_WD, ADAM_STEP = 0.001, 0.9, 0.999, 1e-08, 0.01, 10
LANES = 128
VMEM_LIMIT_BYTES = 48 * 1024 * 1024
NEG = -1e30

WEIGHTS = ['norm_mix', 'norm_ffn', 'conv_w_in', 'conv_b_in', 'conv_dw', 'conv_dw_b', 'conv_ln_g', 'conv_ln_b',
           'conv_w_out', 'conv_b_out', 'pool_w', 'pool_b', 'pool_scale', 'fox_w_in', 'fox_b_f', 'fox_q_gain',
           'fox_k_gain', 'fox_w_o', 'ffn_w_up', 'ffn_dw', 'ffn_dw_b', 'ffn_w_down']
SHARD_AXIS = {'conv_w_in': 2, 'conv_b_in': 1, 'conv_dw': 2, 'conv_dw_b': 1, 'conv_ln_g': 1, 'conv_ln_b': 1,
              'conv_w_out': 1, 'conv_b_out': 1, 'pool_w': 2, 'pool_b': 2, 'fox_w_in': 2, 'fox_w_o': 1,
              'ffn_w_up': 2, 'ffn_dw': 2, 'ffn_w_down': 1}
MATRICES = ('conv_w_in', 'conv_w_out', 'pool_w', 'fox_w_in', 'fox_w_o', 'ffn_w_up', 'ffn_w_down')
SHARDED = [n for n in WEIGHTS if n in SHARD_AXIS]
REPLICATED = [n for n in WEIGHTS if n not in SHARD_AXIS]


def _cp(*sem):
    return pltpu.CompilerParams(dimension_semantics=sem, vmem_limit_bytes=VMEM_LIMIT_BYTES)


def _tile(n, pref, align=8):
    if n <= pref:
        return n
    t = (pref // align) * align
    while t >= align:
        if n % t == 0:
            return t
        t -= align
    return n


def _fold8(x):
    r, c = x.shape
    return x.reshape(r // 8, 8, c).sum(axis=0)


def _sigmoid(x):
    return 0.5 * jnp.tanh(0.5 * x) + 0.5


def _shifts_back(cur, tail, n):
    hb = tail.shape[0]
    xe = jnp.concatenate([tail, cur], axis=0)
    return [cur] + [pltpu.roll(xe, j, axis=0)[hb:] for j in range(1, n)]


def _shifts_fwd(cur, head, n):
    ts = cur.shape[0]
    xe = jnp.concatenate([cur, head], axis=0)
    ln = xe.shape[0]
    return [cur] + [pltpu.roll(xe, ln - j, axis=0)[:ts] for j in range(1, n)]


def _dot_hi(a, b):
    return jnp.dot(a, b, preferred_element_type=F32, precision=lax.Precision.HIGHEST)


def _rmsnorm_fwd(x, g):
    t, d = x.shape
    tm = _tile(t, 512)

    def body(x_ref, g_ref, h_ref):
        xv = x_ref[...]
        r = lax.rsqrt(jnp.mean(xv * xv, axis=-1, keepdims=True) + EPS)
        h_ref[...] = (xv * r * g_ref[...]).astype(BF16)

    return pl.pallas_call(
        body, out_shape=SDS((t, d), BF16), grid=(t // tm,), name="rmsnorm_fwd",
        in_specs=[pl.BlockSpec((tm, d), lambda i: (i, 0)), pl.BlockSpec((1, d), lambda i: (0, 0))],
        out_specs=pl.BlockSpec((tm, d), lambda i: (i, 0)), compiler_params=_cp("parallel"))(x, g)


def _rmsnorm_bwd(x, g, dh, dres):
    t, d = x.shape
    tm = _tile(t, 512)

    def body(x_ref, g_ref, dh_ref, dres_ref, dx_ref, dg_ref):
        @pl.when(pl.program_id(0) == 0)
        def _():
            dg_ref[...] = jnp.zeros_like(dg_ref)

        xv = x_ref[...]
        r = lax.rsqrt(jnp.mean(xv * xv, axis=-1, keepdims=True) + EPS)
        xh = xv * r
        dhv = dh_ref[...].astype(F32)
        u = dhv * g_ref[...]
        dx_ref[...] = dres_ref[...] + r * (u - xh * jnp.mean(u * xh, axis=-1, keepdims=True))
        dg_ref[...] += _fold8(dhv * xh)

    row = pl.BlockSpec((tm, d), lambda i: (i, 0))
    return pl.pallas_call(
        body, out_shape=(SDS((t, d), F32), SDS((8, d), F32)), grid=(t // tm,), name="rmsnorm_bwd",
        in_specs=[row, pl.BlockSpec((1, d), lambda i: (0, 0)), row, row],
        out_specs=(row, pl.BlockSpec((8, d), lambda i: (0, 0))), compiler_params=_cp("arbitrary"))(x, g, dh, dres)


def _mm(a, b, *, trans_b=False, bias=None, residual=None, a2=None, b2=None, out_dtype=BF16, name="mm"):
    m, k = a.shape
    n = b.shape[0] if trans_b else b.shape[1]
    tm, tn, tk = _tile(m, 1024, 16), _tile(n, 1536, LANES), _tile(k, 1536, LANES)
    nk = k // tk
    two = a2 is not None
    steps = 2 * nk if two else nk
    dims = (((1,), (1,)), ((), ())) if trans_b else (((1,), (0,)), ((), ()))
    has_bias, has_res = bias is not None, residual is not None

    def body(*refs):
        n_in = 4 if two else 2
        bias_ref = refs[n_in] if has_bias else None
        res_ref = refs[n_in + has_bias] if has_res else None

        def finish(r):
            if has_bias:
                r = r + bias_ref[...]
            if has_res:
                r = r + res_ref[...]
            return r.astype(out_dtype)

        def dot(a_ref, b_ref):
            return lax.dot_general(a_ref[...].astype(BF16), b_ref[...].astype(BF16), dims, preferred_element_type=F32)

        if steps == 1:
            refs[-1][...] = finish(dot(refs[0], refs[1]))
            return
        o_ref, acc_ref = refs[-2], refs[-1]
        kk = pl.program_id(2)

        @pl.when(kk == 0)
        def _():
            acc_ref[...] = jnp.zeros_like(acc_ref)

        @pl.when(kk < nk)
        def _():
            acc_ref[...] += dot(refs[0], refs[1])

        if two:
            @pl.when(kk >= nk)
            def _():
                acc_ref[...] += dot(refs[2], refs[3])

        @pl.when(kk == steps - 1)
        def _():
            o_ref[...] = finish(acc_ref[...])

    def pair(first):
        kmap = (lambda kk: jnp.minimum(kk, nk - 1)) if first else (lambda kk: jnp.maximum(kk - nk, 0))
        a_spec = pl.BlockSpec((tm, tk), lambda j, i, kk: (i, kmap(kk)))
        if trans_b:
            b_spec = pl.BlockSpec((tn, tk), lambda j, i, kk: (j, kmap(kk)))
        else:
            b_spec = pl.BlockSpec((tk, tn), lambda j, i, kk: (kmap(kk), j))
        return [a_spec, b_spec]

    in_specs, args = pair(True), [a, b]
    if two:
        in_specs += pair(False)
        args += [a2, b2]
    if has_bias:
        in_specs.append(pl.BlockSpec((1, tn), lambda j, i, kk: (0, j)))
        args.append(bias)
    if has_res:
        in_specs.append(pl.BlockSpec((tm, tn), lambda j, i, kk: (i, j)))
        args.append(residual)
    return pl.pallas_call(
        body, out_shape=SDS((m, n), out_dtype), grid=(n // tn, m // tm, steps), name=name,
        in_specs=in_specs, out_specs=pl.BlockSpec((tm, tn), lambda j, i, kk: (i, j)),
        scratch_shapes=[] if steps == 1 else [pltpu.VMEM((tm, tn), F32)],
        compiler_params=_cp("parallel", "parallel", "arbitrary"))(*args)


def _wgrad(a, g, *, out_dtype=BF16, name="wgrad"):
    m, ka = a.shape
    n = g.shape[1]
    ta, tn, tm = _tile(ka, 1536, LANES), _tile(n, 1536, LANES), _tile(m, 1024)
    nm = m // tm

    def body(a_ref, g_ref, o_ref, cs_ref, acc_ref):
        i, mm = pl.program_id(1), pl.program_id(2)

        @pl.when(mm == 0)
        def _():
            acc_ref[...] = jnp.zeros_like(acc_ref)

        @pl.when((mm == 0) & (i == 0))
        def _():
            cs_ref[...] = jnp.zeros_like(cs_ref)

        gv = g_ref[...]
        acc_ref[...] += lax.dot_general(a_ref[...].astype(BF16), gv.astype(BF16), (((0,), (0,)), ((), ())),
                                        preferred_element_type=F32)

        @pl.when(i == 0)
        def _():
            cs_ref[...] += _fold8(gv.astype(F32))

        @pl.when(mm == nm - 1)
        def _():
            o_ref[...] = acc_ref[...].astype(out_dtype)

    return pl.pallas_call(
        body, out_shape=(SDS((ka, n), out_dtype), SDS((8, n), F32)), grid=(n // tn, ka // ta, nm), name=name,
        in_specs=[pl.BlockSpec((tm, ta), lambda j, i, mm: (mm, i)), pl.BlockSpec((tm, tn), lambda j, i, mm: (mm, j))],
        out_specs=(pl.BlockSpec((ta, tn), lambda j, i, mm: (i, j)), pl.BlockSpec((8, tn), lambda j, i, mm: (0, j))),
        scratch_shapes=[pltpu.VMEM((ta, tn), F32)],
        compiler_params=_cp("arbitrary", "arbitrary", "arbitrary"))(a, g)


FFN_HALO = 16


def _ffn_conv(uc_ref, up_ref, w_ref, b_ref, s):
    u = uc_ref[...].astype(F32)
    tail = jnp.where(s > 0, up_ref[...].astype(F32), 0.0)
    sh = _shifts_back(u, tail, 3)
    return sh, sh[2] * w_ref[0:1, :] + sh[1] * w_ref[1:2, :] + sh[0] * w_ref[2:3, :] + b_ref[...]


def _ffn_act_fwd(uv, ug, dw8, b):
    bsz, s_len, f = uv.shape
    tc, ts = _tile(f, 256, LANES), _tile(s_len, 1024, FFN_HALO)
    nf, r = f // tc, ts // FFN_HALO

    def body(uv_ref, uvp_ref, ug_ref, ugp_ref, wv_ref, wg_ref, bv_ref, bg_ref, a_ref):
        s = pl.program_id(2)
        _, val = _ffn_conv(uv_ref, uvp_ref, wv_ref, bv_ref, s)
        _, gate = _ffn_conv(ug_ref, ugp_ref, wg_ref, bg_ref, s)
        a_ref[...] = (gate * _sigmoid(gate) * val).astype(BF16)

    cur = pl.BlockSpec((None, ts, tc), lambda bi, j, s: (bi, s, j))
    prev = pl.BlockSpec((None, FFN_HALO, tc), lambda bi, j, s: (bi, jnp.maximum(s * r - 1, 0), j))

    def par(rows, off):
        return pl.BlockSpec((rows, tc), lambda bi, j, s: (0, j + off))

    return pl.pallas_call(
        body, out_shape=SDS((bsz, s_len, f), BF16), grid=(bsz, nf, s_len // ts), name="ffn_act_fwd",
        in_specs=[cur, prev, cur, prev, par(8, 0), par(8, nf), par(1, 0), par(1, nf)], out_specs=cur,
        compiler_params=_cp("parallel", "parallel", "arbitrary"))(uv, uv, ug, ug, dw8, dw8, b, b)


def _ffn_act_bwd1(uv, ug, da, dw8, b):
    bsz, s_len, f = uv.shape
    tc, ts = _tile(f, 256, LANES), _tile(s_len, 1024, FFN_HALO)
    nf, r = f // tc, ts // FFN_HALO

    def body(uv_ref, uvp_ref, ug_ref, ugp_ref, da_ref, wv_ref, wg_ref, bv_ref, bg_ref,
             dvv_ref, dvg_ref, ddwv_ref, ddwg_ref, dbv_ref, dbg_ref):
        bi, s = pl.program_id(1), pl.program_id(2)

        @pl.when((bi == 0) & (s == 0))
        def _():
            for ref in (ddwv_ref, ddwg_ref, dbv_ref, dbg_ref):
                ref[...] = jnp.zeros_like(ref)

        shv, val = _ffn_conv(uv_ref, uvp_ref, wv_ref, bv_ref, s)
        shg, gate = _ffn_conv(ug_ref, ugp_ref, wg_ref, bg_ref, s)
        sg = _sigmoid(gate)
        dav = da_ref[...].astype(F32)
        for dv, sh, dv_ref, ddw_ref, db_ref in (
                (dav * gate * sg, shv, dvv_ref, ddwv_ref, dbv_ref),
                (dav * val * (sg * (1.0 + gate * (1.0 - sg))), shg, dvg_ref, ddwg_ref, dbg_ref)):
            dv_ref[...] = dv.astype(BF16)
            db_ref[...] += _fold8(dv)
            for k in range(3):
                ddw_ref[k] += _fold8(dv * sh[2 - k])

    cur = pl.BlockSpec((None, ts, tc), lambda j, bi, s: (bi, s, j))
    prev = pl.BlockSpec((None, FFN_HALO, tc), lambda j, bi, s: (bi, jnp.maximum(s * r - 1, 0), j))

    def par(rows, off):
        return pl.BlockSpec((rows, tc), lambda j, bi, s: (0, j + off))

    acc3 = pl.BlockSpec((3, 8, tc), lambda j, bi, s: (0, 0, j))
    acc1 = pl.BlockSpec((8, tc), lambda j, bi, s: (0, j))
    return pl.pallas_call(
        body, out_shape=(SDS((bsz, s_len, f), BF16), SDS((bsz, s_len, f), BF16), SDS((3, 8, f), F32),
                         SDS((3, 8, f), F32), SDS((8, f), F32), SDS((8, f), F32)),
        grid=(nf, bsz, s_len // ts), name="ffn_act_bwd1",
        in_specs=[cur, prev, cur, prev, cur, par(8, 0), par(8, nf), par(1, 0), par(1, nf)],
        out_specs=(cur, cur, acc3, acc3, acc1, acc1),
        compiler_params=_cp("arbitrary", "arbitrary", "arbitrary"))(uv, uv, ug, ug, da, dw8, dw8, b, b)


def _ffn_act_bwd2(dvv, dvg, dw8):
    bsz, s_len, f = dvv.shape
    tc, ts = _tile(f, 256, LANES), _tile(s_len, 1024, FFN_HALO)
    nf, r, ns = f // tc, ts // FFN_HALO, s_len // ts

    def body(vc_ref, vn_ref, gc_ref, gn_ref, wv_ref, wg_ref, duv_ref, dug_ref):
        s = pl.program_id(2)
        for dc_ref, dn_ref, w_ref, du_ref in ((vc_ref, vn_ref, wv_ref, duv_ref), (gc_ref, gn_ref, wg_ref, dug_ref)):
            d = dc_ref[...].astype(F32)
            head = jnp.where(s < ns - 1, dn_ref[...].astype(F32), 0.0)
            sh = _shifts_fwd(d, head, 3)
            du_ref[...] = (sh[0] * w_ref[2:3, :] + sh[1] * w_ref[1:2, :] + sh[2] * w_ref[0:1, :]).astype(BF16)

    cur = pl.BlockSpec((None, ts, tc), lambda bi, j, s: (bi, s, j))
    nxt = pl.BlockSpec((None, FFN_HALO, tc),
                       lambda bi, j, s: (bi, jnp.minimum((s + 1) * r, s_len // FFN_HALO - 1), j))

    def par(off):
        return pl.BlockSpec((8, tc), lambda bi, j, s: (0, j + off))

    return pl.pallas_call(
        body, out_shape=(SDS((bsz, s_len, f), BF16),) * 2, grid=(bsz, nf, ns), name="ffn_act_bwd2",
        in_specs=[cur, nxt, cur, nxt, par(0), par(nf)], out_specs=(cur, cur),
        compiler_params=_cp("parallel", "parallel", "arbitrary"))(dvv, dvv, dvg, dvg, dw8, dw8)


CONV_HALO = 32
CONV_CHUNK = 256


def _conv_act_fwd(p, dw32, dwb, ln_g, ln_b):
    bsz, s_len, d2 = p.shape
    d = d2 // 2
    kw = 31
    ts = _tile(s_len, 256, CONV_HALO)
    r = ts // CONV_HALO
    cc = min(CONV_CHUNK, d)

    def body(pc_ref, pp_ref, w_ref, wb_ref, g_ref, b_ref, u_ref, s_ref):
        s = pl.program_id(1)
        tot = jnp.zeros((ts, 1), F32)
        for c0 in range(0, d, cc):
            a = pc_ref[:, c0:c0 + cc].astype(F32)
            g = pc_ref[:, d + c0:d + c0 + cc].astype(F32)
            z = a * _sigmoid(g)
            ap = pp_ref[:, c0:c0 + cc].astype(F32)
            gp = pp_ref[:, d + c0:d + c0 + cc].astype(F32)
            tail = jnp.where(s > 0, ap * _sigmoid(gp), 0.0)
            sh = _shifts_back(z, tail, kw)
            acc = wb_ref[:, c0:c0 + cc] + sh[0] * w_ref[kw - 1:kw, c0:c0 + cc]
            for j in range(1, kw):
                acc = acc + sh[j] * w_ref[kw - 1 - j:kw - j, c0:c0 + cc]
            u_ref[:, c0:c0 + cc] = acc
            tot = tot + jnp.sum(acc, axis=-1, keepdims=True)
        u = u_ref[...]
        mu = tot / d
        uc = u - mu
        var = jnp.mean(uc * uc, axis=-1, keepdims=True)
        ul = uc * lax.rsqrt(var + EPS) * g_ref[...] + b_ref[...]
        s_ref[...] = (ul * _sigmoid(ul)).astype(BF16)

    vec = pl.BlockSpec((1, d), lambda bi, s: (0, 0))
    return pl.pallas_call(
        body, out_shape=(SDS((bsz, s_len, d), F32), SDS((bsz, s_len, d), BF16)), grid=(bsz, s_len // ts),
        name="conv_act_fwd",
        in_specs=[pl.BlockSpec((None, ts, d2), lambda bi, s: (bi, s, 0)),
                  pl.BlockSpec((None, CONV_HALO, d2), lambda bi, s: (bi, jnp.maximum(s * r - 1, 0), 0)),
                  pl.BlockSpec((32, d), lambda bi, s: (0, 0)), vec, vec, vec],
        out_specs=(pl.BlockSpec((None, ts, d), lambda bi, s: (bi, s, 0)),
                   pl.BlockSpec((None, ts, d), lambda bi, s: (bi, s, 0))),
        compiler_params=_cp("parallel", "arbitrary"))(p, p, dw32, dwb, ln_g, ln_b)


def _conv_act_bwd1(u, ds, ln_g, ln_b):
    t, d = u.shape
    ts = _tile(t, 256)

    def body(u_ref, ds_ref, g_ref, b_ref, du_ref, dg_ref, db_ref, dwb_ref):
        @pl.when(pl.program_id(0) == 0)
        def _():
            dg_ref[...] = jnp.zeros_like(dg_ref)
            db_ref[...] = jnp.zeros_like(db_ref)
            dwb_ref[...] = jnp.zeros_like(dwb_ref)

        uv = u_ref[...]
        uc = uv - jnp.mean(uv, axis=-1, keepdims=True)
        rstd = lax.rsqrt(jnp.mean(uc * uc, axis=-1, keepdims=True) + EPS)
        uh = uc * rstd
        ul = uh * g_ref[...] + b_ref[...]
        sg = _sigmoid(ul)
        dul = ds_ref[...].astype(F32) * (sg * (1.0 + ul * (1.0 - sg)))
        duh = dul * g_ref[...]
        du = rstd * (duh - jnp.mean(duh, axis=-1, keepdims=True) - uh * jnp.mean(duh * uh, axis=-1, keepdims=True))
        du_ref[...] = du
        dg_ref[...] += _fold8(dul * uh)
        db_ref[...] += _fold8(dul)
        dwb_ref[...] += _fold8(du)

    row = pl.BlockSpec((ts, d), lambda i: (i, 0))
    vec = pl.BlockSpec((1, d), lambda i: (0, 0))
    acc = pl.BlockSpec((8, d), lambda i: (0, 0))
    return pl.pallas_call(
        body, out_shape=(SDS((t, d), F32), SDS((8, d), F32), SDS((8, d), F32), SDS((8, d), F32)), grid=(t // ts,),
        name="conv_act_bwd1", in_specs=[row, row, vec, vec], out_specs=(row, acc, acc, acc),
        compiler_params=_cp("arbitrary"))(u, ds, ln_g, ln_b)


def _conv_act_bwd2(du, p, dw32):
    bsz, s_len, d2 = p.shape
    d = d2 // 2
    kw = 31
    ts = _tile(s_len, 256, CONV_HALO)
    r, ns = ts // CONV_HALO, s_len // ts
    cc = min(CONV_CHUNK, d)

    def body(dc_ref, dn_ref, pc_ref, pp_ref, w_ref, dp_ref, ddw_ref):
        bi, s = pl.program_id(0), pl.program_id(1)

        @pl.when((bi == 0) & (s == 0))
        def _():
            ddw_ref[...] = jnp.zeros_like(ddw_ref)

        for c0 in range(0, d, cc):
            a = pc_ref[:, c0:c0 + cc].astype(F32)
            g = pc_ref[:, d + c0:d + c0 + cc].astype(F32)
            sg = _sigmoid(g)
            z = a * sg
            ap = pp_ref[:, c0:c0 + cc].astype(F32)
            gp = pp_ref[:, d + c0:d + c0 + cc].astype(F32)
            tail = jnp.where(s > 0, ap * _sigmoid(gp), 0.0)
            duv = dc_ref[:, c0:c0 + cc]
            head = jnp.where(s < ns - 1, dn_ref[:, c0:c0 + cc], 0.0)
            zb = _shifts_back(z, tail, kw)
            for k in range(kw):
                ddw_ref[k, :, c0:c0 + cc] += _fold8(duv * zb[kw - 1 - k])
            df = _shifts_fwd(duv, head, kw)
            dz = df[0] * w_ref[kw - 1:kw, c0:c0 + cc]
            for j in range(1, kw):
                dz = dz + df[j] * w_ref[kw - 1 - j:kw - j, c0:c0 + cc]
            dp_ref[:, c0:c0 + cc] = (dz * sg).astype(BF16)
            dp_ref[:, d + c0:d + c0 + cc] = (dz * a * sg * (1.0 - sg)).astype(BF16)

    return pl.pallas_call(
        body, out_shape=(SDS((bsz, s_len, d2), BF16), SDS((32, 8, d), F32)), grid=(bsz, ns), name="conv_act_bwd2",
        in_specs=[pl.BlockSpec((None, ts, d), lambda bi, s: (bi, s, 0)),
                  pl.BlockSpec((None, CONV_HALO, d),
                               lambda bi, s: (bi, jnp.minimum((s + 1) * r, s_len // CONV_HALO - 1), 0)),
                  pl.BlockSpec((None, ts, d2), lambda bi, s: (bi, s, 0)),
                  pl.BlockSpec((None, CONV_HALO, d2), lambda bi, s: (bi, jnp.maximum(s * r - 1, 0), 0)),
                  pl.BlockSpec((32, d), lambda bi, s: (0, 0))],
        out_specs=(pl.BlockSpec((None, ts, d2), lambda bi, s: (bi, s, 0)),
                   pl.BlockSpec((32, 8, d), lambda bi, s: (0, 0, 0))),
        compiler_params=_cp("arbitrary", "arbitrary"))(du, du, p, p, dw32)


POOL_HALO = 16


def _pool_counts(s, ts, rows, w):
    t = s * ts + lax.broadcasted_iota(jnp.int32, (rows, 1), 0)
    return jnp.minimum(t + 1, w).astype(F32)


def _pool_fwd(x, gmix, w, b, scale):
    bsz, s_len, d = x.shape
    ng = len(POOL_WINDOWS)
    cg = d // ng
    ts = _tile(s_len, 512, POOL_HALO)
    r = ts // POOL_HALO

    def body(xc_ref, xp_ref, g_ref, w_ref, b_ref, sc_ref, y_ref, p_ref):
        s = pl.program_id(1)

        def norm(v):
            return v * lax.rsqrt(jnp.mean(v * v, axis=-1, keepdims=True) + EPS) * g_ref[...]

        xc = xc_ref[...]
        h = norm(xc)
        tail = jnp.where(s > 0, norm(xp_ref[...]), 0.0)
        for gi, win in enumerate(POOL_WINDOWS):
            lo, hi = gi * cg, (gi + 1) * cg
            hg = h[:, lo:hi]
            acc = jnp.concatenate([tail[:, lo:hi], hg], axis=0)
            step = 1
            while step < win:
                acc = acc + pltpu.roll(acc, step, axis=0)
                step *= 2
            pg = acc[POOL_HALO:] / _pool_counts(s, ts, ts, win) - hg
            pb = pg.astype(BF16)
            p_ref[:, lo:hi] = pb
            yg = jnp.dot(pb, w_ref[gi], preferred_element_type=F32) + b_ref[:, lo:hi]
            y_ref[:, lo:hi] = xc[:, lo:hi] + yg * sc_ref[:, lo:hi]

    vec = pl.BlockSpec((1, d), lambda bi, s: (0, 0))
    blk = pl.BlockSpec((None, ts, d), lambda bi, s: (bi, s, 0))
    return pl.pallas_call(
        body, out_shape=(SDS((bsz, s_len, d), F32), SDS((bsz, s_len, d), BF16)), grid=(bsz, s_len // ts),
        name="pool_fwd",
        in_specs=[blk, pl.BlockSpec((None, POOL_HALO, d), lambda bi, s: (bi, jnp.maximum(s * r - 1, 0), 0)),
                  vec, pl.BlockSpec((ng, cg, cg), lambda bi, s: (0, 0, 0)), vec, vec],
        out_specs=(blk, blk), compiler_params=_cp("parallel", "arbitrary"))(x, x, gmix, w, b, scale)


def _pool_bwd(x, dy, p, gmix, w, b, scale):
    bsz, s_len, d = x.shape
    ng = len(POOL_WINDOWS)
    cg = d // ng
    ts = _tile(s_len, 512, POOL_HALO)
    r, ns = ts // POOL_HALO, s_len // ts
    nt = (((1,), (1,)), ((), ()))
    tn = (((0,), (0,)), ((), ()))

    def body(x_ref, dy_ref, dyn_ref, p_ref, g_ref, w_ref, b_ref, sc_ref, dx_ref, dw_ref, db_ref, dsc_ref, dg_ref):
        bi, s = pl.program_id(0), pl.program_id(1)

        @pl.when((bi == 0) & (s == 0))
        def _():
            dw_ref[...] = jnp.zeros_like(dw_ref)
            db_ref[...] = jnp.zeros_like(db_ref)
            dsc_ref[...] = jnp.zeros_like(dsc_ref)
            dg_ref[...] = jnp.zeros_like(dg_ref)

        dy = dy_ref[...]
        dyy = dy * sc_ref[...]
        dyy_n = jnp.where(s < ns - 1, dyn_ref[...] * sc_ref[...], 0.0)
        db_ref[...] += _fold8(dyy)
        xv = x_ref[...]
        rr = lax.rsqrt(jnp.mean(xv * xv, axis=-1, keepdims=True) + EPS)
        xh = xv * rr
        for gi, win in enumerate(POOL_WINDOWS):
            lo, hi = gi * cg, (gi + 1) * cg
            pb = p_ref[:, lo:hi]
            wg = w_ref[gi]
            pre = jnp.dot(pb, wg, preferred_element_type=F32) + b_ref[:, lo:hi]
            dsc_ref[:, lo:hi] += _fold8(dy[:, lo:hi] * pre)
            dyb = dyy[:, lo:hi].astype(BF16)
            dw_ref[gi] += lax.dot_general(pb, dyb, tn, preferred_element_type=F32)
            dp = lax.dot_general(dyb, wg, nt, preferred_element_type=F32)
            dp_n = lax.dot_general(dyy_n[:, lo:hi].astype(BF16), wg, nt, preferred_element_type=F32)
            q = dp / _pool_counts(s, ts, ts, win)
            q_n = dp_n / _pool_counts(s + 1, ts, POOL_HALO, win)
            acc = jnp.concatenate([q, q_n], axis=0)
            ln = ts + POOL_HALO
            step = 1
            while step < win:
                acc = acc + pltpu.roll(acc, ln - step, axis=0)
                step *= 2
            dh = acc[:ts] - dp
            xhg = xh[:, lo:hi]
            dg_ref[:, lo:hi] += _fold8(dh * xhg)
            dx_ref[:, lo:hi] = dh * g_ref[:, lo:hi]
        u = dx_ref[...]
        dx_ref[...] = dy + rr * (u - xh * jnp.mean(u * xh, axis=-1, keepdims=True))

    vec = pl.BlockSpec((1, d), lambda bi, s: (0, 0))
    acc8 = pl.BlockSpec((8, d), lambda bi, s: (0, 0))
    blk = pl.BlockSpec((None, ts, d), lambda bi, s: (bi, s, 0))
    wspec = pl.BlockSpec((ng, cg, cg), lambda bi, s: (0, 0, 0))
    return pl.pallas_call(
        body, out_shape=(SDS((bsz, s_len, d), F32), SDS((ng, cg, cg), F32), SDS((8, d), F32), SDS((8, d), F32),
                         SDS((8, d), F32)),
        grid=(bsz, ns), name="pool_bwd",
        in_specs=[blk, blk,
                  pl.BlockSpec((None, POOL_HALO, d),
                               lambda bi, s: (bi, jnp.minimum((s + 1) * r, s_len // POOL_HALO - 1), 0)),
                  blk, vec, wspec, vec, vec],
        out_specs=(blk, wspec, acc8, acc8, acc8),
        compiler_params=_cp("arbitrary", "arbitrary"))(x, dy, dy, p, gmix, w, b, scale)


def _tri(n, upper):
    row = lax.broadcasted_iota(jnp.int32, (n, n), 0)
    col = lax.broadcasted_iota(jnp.int32, (n, n), 1)
    return jnp.where((col >= row) if upper else (col <= row), 1.0, 0.0).astype(F32)


def _fox_gate_fwd(proj, bf, n_heads):
    bsz, s_len, width = proj.shape
    col = width // LANES - 1
    ts = _tile(s_len, 512)

    def body(fl_ref, b_ref, c_ref, carry_ref):
        @pl.when(pl.program_id(1) == 0)
        def _():
            carry_ref[...] = jnp.zeros_like(carry_ref)

        xv = fl_ref[...] + b_ref[...]
        logf = jnp.minimum(xv, 0.0) - jnp.log(1.0 + jnp.exp(-jnp.abs(xv)))
        lane = lax.broadcasted_iota(jnp.int32, (1, LANES), 1)
        logf = jnp.where(lane < n_heads, logf, 0.0)
        c = _dot_hi(_tri(ts, False), logf) + carry_ref[0:1, :]
        c_ref[...] = c
        carry_ref[0:1, :] = c[ts - 1:ts, :]

    return pl.pallas_call(
        body, out_shape=SDS((bsz, s_len, LANES), F32), grid=(bsz, s_len // ts), name="fox_gate_fwd",
        in_specs=[pl.BlockSpec((None, ts, LANES), lambda bi, s: (bi, s, col)),
                  pl.BlockSpec((1, LANES), lambda bi, s: (0, 0))],
        out_specs=pl.BlockSpec((None, ts, LANES), lambda bi, s: (bi, s, 0)),
        scratch_shapes=[pltpu.VMEM((8, LANES), F32)],
        compiler_params=_cp("arbitrary", "arbitrary"))(proj, bf)


def _fox_gate_bwd(dc, proj, bf, n_heads):
    bsz, s_len, width = proj.shape
    col = width // LANES - 1
    ts = _tile(s_len, 512)
    ns = s_len // ts

    def body(dc_ref, fl_ref, b_ref, dfl_ref, db_ref, carry_ref):
        bi, s = pl.program_id(0), pl.program_id(1)

        @pl.when((bi == 0) & (s == 0))
        def _():
            db_ref[...] = jnp.zeros_like(db_ref)

        @pl.when(s == 0)
        def _():
            carry_ref[...] = jnp.zeros_like(carry_ref)

        dlogf = _dot_hi(_tri(ts, True), dc_ref[...]) + carry_ref[0:1, :]
        carry_ref[0:1, :] = dlogf[0:1, :]
        lane = lax.broadcasted_iota(jnp.int32, (1, LANES), 1)
        dfl = jnp.where(lane < n_heads, dlogf * (1.0 - _sigmoid(fl_ref[...] + b_ref[...])), 0.0)
        dfl_ref[...] = dfl.astype(BF16)
        db_ref[...] += _fold8(dfl)

    return pl.pallas_call(
        body, out_shape=(SDS((bsz, s_len, LANES), BF16), SDS((8, LANES), F32)), grid=(bsz, ns), name="fox_gate_bwd",
        in_specs=[pl.BlockSpec((None, ts, LANES), lambda bi, s: (bi, ns - 1 - s, 0)),
                  pl.BlockSpec((None, ts, LANES), lambda bi, s: (bi, ns - 1 - s, col)),
                  pl.BlockSpec((1, LANES), lambda bi, s: (0, 0))],
        out_specs=(pl.BlockSpec((None, ts, LANES), lambda bi, s: (bi, ns - 1 - s, 0)),
                   pl.BlockSpec((8, LANES), lambda bi, s: (0, 0))),
        scratch_shapes=[pltpu.VMEM((8, LANES), F32)],
        compiler_params=_cp("arbitrary", "arbitrary"))(dc, proj, bf)


def _head_maps(d):
    ch = lax.broadcasted_iota(jnp.int32, (d, LANES), 0) // HEAD_DIM
    hd = lax.broadcasted_iota(jnp.int32, (d, LANES), 1)
    e = jnp.where(ch == hd, 1.0, 0.0).astype(BF16)
    cht = lax.broadcasted_iota(jnp.int32, (LANES, d), 1) // HEAD_DIM
    hdt = lax.broadcasted_iota(jnp.int32, (LANES, d), 0)
    et = jnp.where(cht == hdt, 1.0, 0.0).astype(BF16)
    return e, et


def _dot_sel(x, e):
    a = x.astype(BF16)
    r = x - a.astype(F32)
    b = r.astype(BF16)
    c = (r - b.astype(F32)).astype(BF16)
    return (jnp.dot(a, e, preferred_element_type=F32) + jnp.dot(b, e, preferred_element_type=F32)
            + jnp.dot(c, e, preferred_element_type=F32))


def _fox_qknorm_fwd(proj, gq, gk, d):
    t = proj.shape[0]
    ts = _tile(t, 256)
    scale = 1.0 / math.sqrt(HEAD_DIM)

    def body(q_ref, k_ref, v_ref, gq_ref, gk_ref, qn_ref, kn_ref, vb_ref):
        e, et = _head_maps(d)

        def norm(v, g):
            r = lax.rsqrt(_dot_sel(v * v, e) / HEAD_DIM + EPS)
            return v * _dot_sel(r, et) * g

        qn_ref[...] = (norm(q_ref[...], gq_ref[...]) * scale).astype(BF16)
        kn_ref[...] = norm(k_ref[...], gk_ref[...]).astype(BF16)
        vb_ref[...] = v_ref[...].astype(BF16)

    def colblk(j):
        return pl.BlockSpec((ts, d), lambda i: (i, j))

    vec = pl.BlockSpec((1, d), lambda i: (0, 0))
    out = pl.BlockSpec((ts, d), lambda i: (i, 0))
    return pl.pallas_call(
        body, out_shape=(SDS((t, d), BF16),) * 3, grid=(t // ts,), name="fox_qknorm_fwd",
        in_specs=[colblk(0), colblk(1), colblk(2), vec, vec], out_specs=(out, out, out),
        compiler_params=_cp("parallel"))(proj, proj, proj, gq, gk)


def _fox_qknorm_bwd(proj, dq, dk, dv, gq, gk, d):
    t = proj.shape[0]
    ts = _tile(t, 256)
    scale = 1.0 / math.sqrt(HEAD_DIM)

    def body(q_ref, k_ref, dq_ref, dk_ref, dv_ref, gq_ref, gk_ref, dp_ref, dgq_ref, dgk_ref):
        @pl.when(pl.program_id(0) == 0)
        def _():
            dgq_ref[...] = jnp.zeros_like(dgq_ref)
            dgk_ref[...] = jnp.zeros_like(dgk_ref)

        e, et = _head_maps(d)

        def back(v, g, dn, dg_ref):
            r = _dot_sel(lax.rsqrt(_dot_sel(v * v, e) / HEAD_DIM + EPS), et)
            vh = v * r
            dg_ref[...] += _fold8(dn * vh)
            u = dn * g
            mh = _dot_sel(_dot_sel(u * vh, e) / HEAD_DIM, et)
            return r * (u - vh * mh)

        dp_ref[:, 0:d] = back(q_ref[...], gq_ref[...], dq_ref[...] * scale, dgq_ref).astype(BF16)
        dp_ref[:, d:2 * d] = back(k_ref[...], gk_ref[...], dk_ref[...], dgk_ref).astype(BF16)
        dp_ref[:, 2 * d:3 * d] = dv_ref[...]

    def colblk(j):
        return pl.BlockSpec((ts, d), lambda i: (i, j))

    row = pl.BlockSpec((ts, d), lambda i: (i, 0))
    vec = pl.BlockSpec((1, d), lambda i: (0, 0))
    acc = pl.BlockSpec((8, d), lambda i: (0, 0))
    return pl.pallas_call(
        body, out_shape=(SDS((t, 3 * d), BF16), SDS((8, d), F32), SDS((8, d), F32)), grid=(t // ts,),
        name="fox_qknorm_bwd", in_specs=[colblk(0), colblk(1), row, row, row, vec, vec],
        out_specs=(pl.BlockSpec((ts, 3 * d), lambda i: (i, 0)), acc, acc),
        compiler_params=_cp("arbitrary"))(proj, proj, dq, dk, dv, gq, gk)


ATT_BLOCK = 512
_NT = (((1,), (1,)), ((), ()))
_TN = (((0,), (0,)), ((), ()))


def _head_mask(h):
    return (lax.broadcasted_iota(jnp.int32, (1, LANES), 1) // HEAD_DIM) == h


def _causal(qi, ki, tq, tk):
    row = qi * tq + lax.broadcasted_iota(jnp.int32, (tq, 1), 0)
    col = ki * tk + lax.broadcasted_iota(jnp.int32, (1, tk), 1)
    return col <= row


def _flash_fwd(q, k, v, crow):
    bsz, s_len, d = q.shape
    nj = d // LANES
    tq = tk = _tile(s_len, ATT_BLOCK, LANES)
    nq = s_len // tq

    pairs = [(a, b) for a in range(nq) for b in range(a + 1)]
    qtab = jnp.asarray([a for a, _ in pairs], jnp.int32)
    ktab = jnp.asarray([b for _, b in pairs], jnp.int32)

    def body(qtab_ref, ktab_ref, q_ref, k_ref, v_ref, c_ref, o_ref, lse_ref, m_ref, l_ref, acc_ref):
        step_id = pl.program_id(2)
        qi, ki = qtab_ref[step_id], ktab_ref[step_id]

        @pl.when(ki == 0)
        def _():
            m_ref[...] = jnp.full_like(m_ref, NEG)
            l_ref[...] = jnp.zeros_like(l_ref)
            acc_ref[...] = jnp.zeros_like(acc_ref)

        def step(masked):
            qv, kv, vv = q_ref[...], k_ref[...], v_ref[...]
            for h in range(2):
                qh = jnp.where(_head_mask(h), qv, jnp.zeros_like(qv))
                s = lax.dot_general(qh, kv, _NT, preferred_element_type=F32) - c_ref[h:h + 1, :]
                if masked:
                    s = jnp.where(_causal(qi, ki, tq, tk), s, NEG)
                m_prev = m_ref[h]
                m_new = jnp.maximum(m_prev, jnp.max(s, axis=1, keepdims=True))
                pm = jnp.exp(s - m_new)
                alpha = jnp.exp(m_prev - m_new)
                l_ref[h] = alpha * l_ref[h] + jnp.sum(pm, axis=1, keepdims=True)
                p_hi = pm.astype(BF16)
                p_lo = (pm - p_hi.astype(F32)).astype(BF16)
                acc_ref[h] = (alpha * acc_ref[h] + jnp.dot(p_hi, vv, preferred_element_type=F32)
                              + jnp.dot(p_lo, vv, preferred_element_type=F32))
                m_ref[h] = m_new

        pl.when(ki < qi)(functools.partial(step, False))
        pl.when(ki == qi)(functools.partial(step, True))

        @pl.when(ki == qi)
        def _():
            m0 = _head_mask(0)
            o_ref[...] = jnp.where(m0, acc_ref[0] / l_ref[0], acc_ref[1] / l_ref[1])
            lse_ref[...] = jnp.where(m0, m_ref[0] + jnp.log(l_ref[0]), m_ref[1] + jnp.log(l_ref[1]))

    qblk = pl.BlockSpec((None, tq, LANES), lambda bi, j, t, qt, kt: (bi, qt[t], j))
    kblk = pl.BlockSpec((None, tk, LANES), lambda bi, j, t, qt, kt: (bi, kt[t], j))
    return pl.pallas_call(
        body, out_shape=(SDS((bsz, s_len, d), F32), SDS((bsz, nj, s_len, LANES), F32)), name="flash_fwd",
        grid_spec=pltpu.PrefetchScalarGridSpec(
            num_scalar_prefetch=2, grid=(bsz, nj, len(pairs)),
            in_specs=[qblk, kblk, kblk,
                      pl.BlockSpec((None, None, 2, tk), lambda bi, j, t, qt, kt: (bi, j, 0, kt[t]))],
            out_specs=(qblk, pl.BlockSpec((None, None, tq, LANES), lambda bi, j, t, qt, kt: (bi, j, qt[t], 0))),
            scratch_shapes=[pltpu.VMEM((2, tq, 1), F32), pltpu.VMEM((2, tq, 1), F32),
                            pltpu.VMEM((2, tq, LANES), F32)]),
        compiler_params=_cp("parallel", "parallel", "arbitrary"))(qtab, ktab, q, k, v, crow)


def _flash_probs(qv, kv, vv, dov, ov, lse, c_ref, h, mask):
    hm = _head_mask(h)
    qh = jnp.where(hm, qv, jnp.zeros_like(qv))
    s = lax.dot_general(qh, kv, _NT, preferred_element_type=F32) - c_ref[h:h + 1, :]
    pm = jnp.exp(s - lse[:, h * HEAD_DIM:h * HEAD_DIM + 1])
    if mask is not None:
        pm = jnp.where(mask, pm, 0.0)
    doh = jnp.where(hm, dov, jnp.zeros_like(dov))
    dpm = lax.dot_general(doh, vv, _NT, preferred_element_type=F32)
    delta = jnp.sum(jnp.where(hm, dov.astype(F32) * ov, 0.0), axis=1, keepdims=True)
    return pm, pm * (dpm - delta)


def _flash_bwd(q, k, v, do, o, lse, crow):
    bsz, s_len, d = q.shape
    nj = d // LANES
    tq = tk = _tile(s_len, ATT_BLOCK, LANES)
    nq = s_len // tq

    pairs = [(b, a) for b in range(nq) for a in range(b, nq)]
    n_live = len(pairs)
    ktab = jnp.asarray([b for b, _ in pairs] + [nq - 1] * nq, jnp.int32)
    qtab = jnp.asarray([a for _, a in pairs] + list(range(nq)), jnp.int32)

    def body(ktab_ref, qtab_ref, q_ref, k_ref, v_ref, do_ref, o_ref, lse_ref, c_ref, dq_ref, dk_ref, dv_ref, dc_ref,
             dqa_ref, dka_ref, dva_ref, dca_ref):
        step_id = pl.program_id(2)
        ki, qi = ktab_ref[step_id], qtab_ref[step_id]
        live = step_id < n_live
        rows = pl.ds(pl.multiple_of(qi * tq, tq), tq)

        @pl.when(step_id == 0)
        def _():
            dqa_ref[...] = jnp.zeros_like(dqa_ref)

        @pl.when(live & (qi == ki))
        def _():
            dka_ref[...] = jnp.zeros_like(dka_ref)
            dva_ref[...] = jnp.zeros_like(dva_ref)
            dca_ref[...] = jnp.zeros_like(dca_ref)

        def step(masked):
            qv, kv, vv, dov, ov, lse = q_ref[...], k_ref[...], v_ref[...], do_ref[...], o_ref[...], lse_ref[...]
            mask = _causal(qi, ki, tq, tk) if masked else None
            for h in range(2):
                pm, ds = _flash_probs(qv, kv, vv, dov, ov, lse, c_ref, h, mask)
                dsb = ds.astype(BF16)
                dva_ref[h] += lax.dot_general(pm.astype(BF16), dov, _TN, preferred_element_type=F32)
                dka_ref[h] += lax.dot_general(dsb, qv, _TN, preferred_element_type=F32)
                dqa_ref[h, rows, :] += jnp.dot(dsb, kv, preferred_element_type=F32)
                dca_ref[h:h + 1, :] -= jnp.sum(ds, axis=0, keepdims=True)

        pl.when(live & (qi > ki))(functools.partial(step, False))
        pl.when(live & (qi == ki))(functools.partial(step, True))

        @pl.when(live & (qi == nq - 1))
        def _():
            m0 = _head_mask(0)
            dk_ref[...] = jnp.where(m0, dka_ref[0], dka_ref[1])
            dv_ref[...] = jnp.where(m0, dva_ref[0], dva_ref[1]).astype(BF16)
            dc_ref[...] = dca_ref[0:2, :]

        @pl.when(jnp.logical_not(live))
        def _():
            dq_ref[...] = jnp.where(_head_mask(0), dqa_ref[0, rows, :], dqa_ref[1, rows, :])

    def qside(bi, j, t, kt, qt):
        return (bi, jnp.where(t < n_live, qt[t], nq - 1), j)

    def kside(bi, j, t, kt, qt):
        return (bi, kt[t], j)

    def dqside(bi, j, t, kt, qt):
        return (bi, jnp.where(t < n_live, 0, qt[t]), j)

    qblk, kblk = pl.BlockSpec((None, tq, LANES), qside), pl.BlockSpec((None, tk, LANES), kside)
    cblk = pl.BlockSpec((None, None, 2, tk), lambda bi, j, t, kt, qt: (bi, j, 0, kt[t]))
    return pl.pallas_call(
        body, out_shape=(SDS((bsz, s_len, d), F32), SDS((bsz, s_len, d), F32), SDS((bsz, s_len, d), BF16),
                         SDS((bsz, nj, 2, s_len), F32)), name="flash_bwd",
        grid_spec=pltpu.PrefetchScalarGridSpec(
            num_scalar_prefetch=2, grid=(bsz, nj, n_live + nq),
            in_specs=[qblk, kblk, kblk, qblk, qblk,
                      pl.BlockSpec((None, None, tq, LANES),
                                   lambda bi, j, t, kt, qt: (bi, j, jnp.where(t < n_live, qt[t], nq - 1), 0)),
                      cblk],
            out_specs=(pl.BlockSpec((None, tq, LANES), dqside), kblk, kblk, cblk),
            scratch_shapes=[pltpu.VMEM((2, s_len, LANES), F32), pltpu.VMEM((2, tk, LANES), F32),
                            pltpu.VMEM((2, tk, LANES), F32), pltpu.VMEM((8, tk), F32)]),
        compiler_params=_cp("parallel", "parallel", "arbitrary"))(ktab, qtab, q, k, v, do, o, lse, crow)


def _loss_head(y, target):
    t, d = y.shape
    tm = _tile(t, 512)

    def body(y_ref, t_ref, dy_ref, acc_ref):
        @pl.when(pl.program_id(0) == 0)
        def _():
            acc_ref[...] = jnp.zeros_like(acc_ref)

        err = y_ref[...] - t_ref[...]
        dy_ref[...] = err / d
        acc_ref[...] += _fold8(err * err)

    row = pl.BlockSpec((tm, d), lambda i: (i, 0))
    return pl.pallas_call(
        body, out_shape=(SDS((t, d), F32), SDS((8, d), F32)), grid=(t // tm,), name="loss_head",
        in_specs=[row, row], out_specs=(row, pl.BlockSpec((8, d), lambda i: (0, 0))),
        compiler_params=_cp("arbitrary"))(y, target)


ADAM_COLS = 1024


def _adamw(g8, w, m, v):
    shape = w.shape
    cols = shape[-1]
    rows = w.size // cols
    n_parts = g8.shape[0]
    g8, w, m, v = g8.reshape(n_parts, rows, cols), w.reshape(rows, cols), m.reshape(rows, cols), v.reshape(rows, cols)
    tr = _tile(rows, 256, 16)
    c1 = 1.0 - ADAM_B1 ** ADAM_STEP
    c2 = 1.0 - ADAM_B2 ** ADAM_STEP

    def body(g8_ref, w_ref, m_ref, v_ref, g_ref, d_ref, nm_ref, nv_ref):
        g = g8_ref[0].astype(F32)
        for i in range(1, n_parts):
            g = g + g8_ref[i].astype(F32)
        mn = ADAM_B1 * m_ref[...] + (1.0 - ADAM_B1) * g
        vn = ADAM_B2 * v_ref[...] + (1.0 - ADAM_B2) * (g * g)
        g_ref[...] = g
        nm_ref[...] = mn
        nv_ref[...] = vn
        d_ref[...] = -ADAM_LR * ((mn / c1) / (jnp.sqrt(vn / c2) + ADAM_EPS) + ADAM_WD * w_ref[...])

    blk = pl.BlockSpec((tr, cols), lambda i: (i, 0))
    outs = pl.pallas_call(
        body, out_shape=(SDS((rows, cols), F32),) * 4, grid=(rows // tr,), name="adamw",
        in_specs=[pl.BlockSpec((n_parts, tr, cols), lambda i: (0, i, 0)), blk, blk, blk], out_specs=(blk,) * 4,
        compiler_params=_cp("parallel"))(g8, w, m, v)
    return [o.reshape(shape) for o in outs]


def _mesh_place():
    x, y, c = lax.axis_index("x"), lax.axis_index("y"), lax.axis_index("c")
    return x, y, c, 4 * x + 2 * y + c


def _gather(shards):
    n = len(shards)

    def body(*refs):
        ins, outs = refs[:n], refs[n:2 * n]
        send_sems, recv_sems, local_sems = refs[2 * n:]
        x, y, c, me = _mesh_place()
        sibling = (x, y, 1 - c)
        chips = [(1 - x, y), (x, 1 - y), (1 - x, 1 - y)]

        def block(px, py, pc):
            return 4 * px + 2 * py + pc

        def copy(t, k, blk, to, src=None):
            return pltpu.make_async_remote_copy(
                src_ref=outs[t].at[blk] if src is None else src, dst_ref=outs[t].at[blk],
                send_sem=send_sems.at[t, k], recv_sem=recv_sems.at[t, k], device_id=to,
                device_id_type=pl.DeviceIdType.MESH)

        own = [pltpu.make_async_copy(ins[t], outs[t].at[me], local_sems.at[t]) for t in range(n)]
        first = []
        for t in range(n):
            own[t].start()
            first.append(copy(t, 0, me, sibling, src=ins[t]))
            first += [copy(t, 1 + j, me, (*chip, c), src=ins[t]) for j, chip in enumerate(chips)]
        for cp in first:
            cp.start()
        passed = []
        for j, chip in enumerate(chips):
            for t in range(n):
                copy(t, 1 + j, block(*chip, c), (x, y, c)).wait_recv()
                cp = copy(t, 4 + j, block(*chip, c), sibling)
                cp.start()
                passed.append(cp)
        for t in range(n):
            copy(t, 0, block(x, y, 1 - c), (x, y, c)).wait_recv()
            for j, chip in enumerate(chips):
                copy(t, 4 + j, block(*chip, 1 - c), (x, y, c)).wait_recv()
        for cp in first + passed:
            cp.wait_send()
        for cp in own:
            cp.wait()

    hbm = pl.BlockSpec(memory_space=pl.ANY)
    return pl.pallas_call(
        body, out_shape=[SDS((N_DEV,) + tuple(s.shape), s.dtype) for s in shards], name="gather",
        in_specs=[hbm] * n, out_specs=[hbm] * n,
        scratch_shapes=[pltpu.SemaphoreType.DMA((n, N_DEV - 1)), pltpu.SemaphoreType.DMA((n, N_DEV - 1)),
                        pltpu.SemaphoreType.DMA((n,))])(*shards)


N_CHIP = N_DEV // 2


def _scatter_core(items):
    n = len(items)

    def body(*refs):
        ins, outs = refs[:n], refs[n:2 * n]
        send_sems, recv_sems = refs[2 * n:]
        x, y, c, _ = _mesh_place()
        copies = []
        for it in range(n):
            for r in range(N_CHIP):
                cp = pltpu.make_async_remote_copy(
                    src_ref=ins[it].at[2 * r + 1 - c], dst_ref=outs[it].at[r], send_sem=send_sems.at[it, r],
                    recv_sem=recv_sems.at[it, r], device_id=(x, y, 1 - c), device_id_type=pl.DeviceIdType.MESH)
                cp.start()
                copies.append(cp)
        for cp in copies:
            cp.wait()

    hbm = pl.BlockSpec(memory_space=pl.ANY)
    return pl.pallas_call(
        body, out_shape=[SDS((N_CHIP,) + tuple(a.shape[1:]), a.dtype) for a in items], name="scatter_core",
        in_specs=[hbm] * n, out_specs=[hbm] * n,
        scratch_shapes=[pltpu.SemaphoreType.DMA((n, N_CHIP)), pltpu.SemaphoreType.DMA((n, N_CHIP))])(*items)


def _pair_add(item, other):
    shape = item.shape[1:]
    cols = shape[-1]
    rows = math.prod(shape) // cols
    tr = _tile(rows, 512, 16)

    def body(x_ref, o_ref, h_ref):
        c = lax.axis_index("c")
        mine = jnp.where(c == 0, x_ref[0].astype(F32), x_ref[1].astype(F32))
        h_ref[...] = (mine + o_ref[...].astype(F32)).astype(item.dtype)

    return pl.pallas_call(
        body, out_shape=SDS((N_CHIP, rows, cols), item.dtype), grid=(N_CHIP, rows // tr), name="pair_add",
        in_specs=[pl.BlockSpec((None, 2, tr, cols), lambda r, i: (r, 0, i, 0)),
                  pl.BlockSpec((None, tr, cols), lambda r, i: (r, i, 0))],
        out_specs=pl.BlockSpec((None, tr, cols), lambda r, i: (r, i, 0)),
        compiler_params=_cp("parallel", "parallel"))(
            item.reshape(N_CHIP, 2, rows, cols), other.reshape(N_CHIP, rows, cols)).reshape((N_CHIP,) + shape)


def _scatter_chip(items, groups):
    n = len(items)
    place = {it: (g, l) for g, members in enumerate(groups) for l, it in enumerate(members)}

    def body(*refs):
        ins, outs = refs[:n], refs[n:n + len(groups)]
        send_sems, recv_sems, local_sems = refs[n + len(groups):]
        x, y, c, _ = _mesh_place()
        chip = 2 * x + y
        copies = []
        for it in range(n):
            g, l = place[it]
            own = pltpu.make_async_copy(ins[it].at[chip], outs[g].at[chip, l], local_sems.at[it])
            own.start()
            copies.append(own)
            for kbits in range(1, N_CHIP):
                px = 1 - x if kbits & 2 else x
                py = 1 - y if kbits & 1 else y
                cp = pltpu.make_async_remote_copy(
                    src_ref=ins[it].at[2 * px + py], dst_ref=outs[g].at[chip, l],
                    send_sem=send_sems.at[it, kbits - 1], recv_sem=recv_sems.at[it, kbits - 1],
                    device_id=(px, py, c), device_id_type=pl.DeviceIdType.MESH)
                cp.start()
                copies.append(cp)
        for cp in copies:
            cp.wait()

    hbm = pl.BlockSpec(memory_space=pl.ANY)
    out_shape = [SDS((N_CHIP, len(members)) + tuple(items[members[0]].shape[1:]), items[members[0]].dtype)
                 for members in groups]
    return pl.pallas_call(
        body, out_shape=out_shape, name="scatter_chip", in_specs=[hbm] * n, out_specs=[hbm] * len(groups),
        scratch_shapes=[pltpu.SemaphoreType.DMA((n, N_CHIP - 1)), pltpu.SemaphoreType.DMA((n, N_CHIP - 1)),
                        pltpu.SemaphoreType.DMA((n,))])(*items)


def _scatter(items, groups):
    halves = _scatter_core(items)
    return _scatter_chip([_pair_add(a, h) for a, h in zip(items, halves)], groups)


def _cat_lanes(g, layer, nb, blk, width):
    _, _, rows, c = g.shape
    tr = _tile(rows, 256, 16)

    def body(g_ref, o_ref):
        for p in range(nb):
            o_ref[:, p * c:(p + 1) * c] = g_ref[p]
        if width > nb * c:
            o_ref[:, nb * c:] = jnp.zeros((tr, width - nb * c), g.dtype)

    return pl.pallas_call(
        body, out_shape=SDS((rows, width), g.dtype), grid=(rows // tr,), name="cat_lanes",
        in_specs=[pl.BlockSpec((nb, None, tr, c), lambda i: (blk, layer, i, 0))],
        out_specs=pl.BlockSpec((tr, width), lambda i: (i, 0)),
        compiler_params=_cp("parallel"))(g)


def _split_lanes(parts, c):
    rows = parts[0].shape[0]
    counts = [p.shape[1] // c for p in parts]
    tr = _tile(rows, 256, 16)

    def body(*refs):
        o_ref = refs[-1]
        q = 0
        for x_ref, cnt in zip(refs[:-1], counts):
            for p in range(cnt):
                o_ref[q] = x_ref[:, p * c:(p + 1) * c]
                q += 1

    return pl.pallas_call(
        body, out_shape=SDS((sum(counts), rows, c), parts[0].dtype), grid=(rows // tr,), name="split_lanes",
        in_specs=[pl.BlockSpec((tr, p.shape[1]), lambda i: (i, 0)) for p in parts],
        out_specs=pl.BlockSpec((sum(counts), tr, c), lambda i: (0, i, 0)),
        compiler_params=_cp("parallel"))(*parts)


def _unshard(g8, shard_shape, axis):
    full = jnp.moveaxis(g8.reshape((N_DEV,) + tuple(shard_shape)), 0, axis)
    shape = list(shard_shape)
    shape[axis] *= N_DEV
    return full.reshape(shape)


def _to_shards(full, axis):
    shape = list(full.shape)
    shape[axis:axis + 1] = [N_DEV, shape[axis] // N_DEV]
    return jnp.moveaxis(full.reshape(shape), axis, 0).reshape(N_DEV, -1)


SMALL = [n for n in SHARDED if n not in MATRICES]


def _flat_rows(parts):
    flat = jnp.concatenate([p.reshape(-1) for p in parts])
    chunk = 8 * ADAM_COLS
    n = -(-flat.shape[0] // chunk) * chunk
    return jnp.pad(flat, (0, n - flat.shape[0])).reshape(n // ADAM_COLS, ADAM_COLS)


def _prepare_weights(gathered, small, shards):
    wt = {}
    flat = small.reshape(N_DEV, -1)
    off = 0
    for n in SMALL:
        size = shards[n].size
        wt[n] = _unshard(flat[:, off:off + size], shards[n].shape, SHARD_AXIS[n])
        off += size
    for n in ('conv_w_out', 'fox_w_o', 'ffn_w_down'):
        g = gathered[n]
        wt[n] = [g[:, l].reshape(N_DEV * g.shape[2], g.shape[3]) for l in range(g.shape[1])]
    g = gathered['pool_w']
    wt['pool_w'] = [jnp.moveaxis(g[:, l], 0, 1).reshape(g.shape[2], N_DEV * g.shape[3], g.shape[4])
                    for l in range(g.shape[1])]
    g = gathered['conv_w_in']
    wt['conv_w_in'] = [_cat_lanes(g, l, N_DEV, 0, N_DEV * g.shape[3]) for l in range(g.shape[1])]
    g = gathered['fox_w_in']
    wt['fox_w_in'] = [_cat_lanes(g, l, N_DEV, 0, 3 * g.shape[2] + LANES) for l in range(g.shape[1])]
    g = gathered['ffn_w_up']
    half = N_DEV // 2
    wt['ffn_w_up_v'] = [_cat_lanes(g, l, half, 0, half * g.shape[3]) for l in range(g.shape[1])]
    wt['ffn_w_up_g'] = [_cat_lanes(g, l, half, 1, half * g.shape[3]) for l in range(g.shape[1])]
    return wt


def _pad_rows(w, rows):
    return jnp.pad(w, ((0, rows - w.shape[0]), (0, 0)))


def _fold(acc):
    return acc.sum(axis=0)


def _local_step(x, target, wt):
    bsz, s_len, d = x.shape
    t = bsz * s_len
    depth = wt['norm_mix'].shape[0]
    n_heads = d // HEAD_DIM
    f = wt['ffn_w_up_v'][0].shape[1]
    row = lambda a: a.reshape(1, -1)
    grads = {n: {} for n in WEIGHTS}
    saved = []

    xc = x.reshape(t, d)
    for i in range(depth):
        j = i // 3
        kind = i % 3
        sv = {'x_mix': xc}
        gm = row(wt['norm_mix'][i])
        if kind == 0:
            hn = _rmsnorm_fwd(xc, gm)
            p = _mm(hn, wt['conv_w_in'][j], bias=row(wt['conv_b_in'][j]), name="conv_in")
            u, sact = _conv_act_fwd(p.reshape(bsz, s_len, 2 * d), _pad_rows(wt['conv_dw'][j], 32),
                                    row(wt['conv_dw_b'][j]), row(wt['conv_ln_g'][j]), row(wt['conv_ln_b'][j]))
            sact = sact.reshape(t, d)
            xn = _mm(sact, wt['conv_w_out'][j], bias=row(wt['conv_b_out'][j]), residual=xc, out_dtype=F32,
                     name="conv_out")
            sv.update(hn=hn, p=p, u=u.reshape(t, d), sact=sact)
        elif kind == 1:
            xn, pp = _pool_fwd(xc.reshape(bsz, s_len, d), gm, wt['pool_w'][j], row(wt['pool_b'][j]),
                               row(wt['pool_scale'][j]))
            xn = xn.reshape(t, d)
            sv.update(p=pp)
        else:
            hn = _rmsnorm_fwd(xc, gm)
            wp = wt['fox_w_in'][j]
            bf = jnp.pad(wt['fox_b_f'][j], (0, LANES - n_heads)).reshape(1, LANES)
            gq = jnp.tile(wt['fox_q_gain'][j], n_heads).reshape(1, d)
            gk = jnp.tile(wt['fox_k_gain'][j], n_heads).reshape(1, d)
            proj = _mm(hn, wp, out_dtype=F32, name="fox_in")
            c = _fox_gate_fwd(proj.reshape(bsz, s_len, -1), bf, n_heads)
            crow = jnp.swapaxes(c, 1, 2)[:, :n_heads].reshape(bsz, n_heads // 2, 2, s_len)
            qn, kn, vb = _fox_qknorm_fwd(proj, gq, gk, d)
            shp = (bsz, s_len, d)
            o, lse = _flash_fwd(qn.reshape(shp), kn.reshape(shp), vb.reshape(shp), crow)
            o = o.reshape(t, d)
            xn = _mm(o, wt['fox_w_o'][j], residual=xc, out_dtype=F32, name="fox_out")
            sv.update(hn=hn, wp=wp, bf=bf, gq=gq, gk=gk, proj=proj, crow=crow, qn=qn, kn=kn, vb=vb, o=o, lse=lse)
        xc = xn
        sv['x_ffn'] = xc
        hf = _rmsnorm_fwd(xc, row(wt['norm_ffn'][i]))
        shf = (bsz, s_len, f)
        uv = _mm(hf, wt['ffn_w_up_v'][i], name="ffn_up").reshape(shf)
        ug = _mm(hf, wt['ffn_w_up_g'][i], name="ffn_up").reshape(shf)
        dw8 = _pad_rows(wt['ffn_dw'][i], 8)
        af = _ffn_act_fwd(uv, ug, dw8, row(wt['ffn_dw_b'][i])).reshape(t, f)
        xc = _mm(af, wt['ffn_w_down'][i], residual=xc, out_dtype=F32, name="ffn_down")
        sv.update(hf=hf, uv=uv, ug=ug, af=af, dw8=dw8)
        saved.append(sv)

    dx, sq = _loss_head(xc, target.reshape(t, d))

    for i in reversed(range(depth)):
        j = i // 3
        kind = i % 3
        sv = saved[i]
        shf = (bsz, s_len, f)
        da = _mm(dx, wt['ffn_w_down'][i], trans_b=True, name="ffn_down_dgrad")
        gw, _ = _wgrad(sv['af'], dx, name="ffn_down_wgrad")
        grads['ffn_w_down'][i] = gw.reshape(N_DEV, f // N_DEV, d)
        dvv, dvg, ddwv, ddwg, dbv, dbg = _ffn_act_bwd1(sv['uv'], sv['ug'], da.reshape(shf), sv['dw8'],
                                                       row(wt['ffn_dw_b'][i]))
        grads['ffn_dw'][i] = jnp.concatenate([ddwv.sum(axis=1), ddwg.sum(axis=1)], axis=1)
        grads['ffn_dw_b'][i] = jnp.concatenate([_fold(dbv), _fold(dbg)])
        duv, dug = _ffn_act_bwd2(dvv, dvg, sv['dw8'])
        duv, dug = duv.reshape(t, f), dug.reshape(t, f)
        gv, _ = _wgrad(sv['hf'], duv, name="ffn_up_wgrad")
        gg, _ = _wgrad(sv['hf'], dug, name="ffn_up_wgrad")
        grads['ffn_w_up'][i] = _split_lanes([gv, gg], 2 * f // N_DEV)
        dhf = _mm(duv, wt['ffn_w_up_v'][i], trans_b=True, a2=dug, b2=wt['ffn_w_up_g'][i], name="ffn_up_dgrad")
        dx, dg = _rmsnorm_bwd(sv['x_ffn'], row(wt['norm_ffn'][i]), dhf, dx)
        grads['norm_ffn'][i] = _fold(dg)
        gm = row(wt['norm_mix'][i])
        if kind == 0:
            dsact = _mm(dx, wt['conv_w_out'][j], trans_b=True, name="conv_out_dgrad")
            gw, cs = _wgrad(sv['sact'], dx, name="conv_out_wgrad")
            grads['conv_w_out'][j] = gw.reshape(N_DEV, d // N_DEV, d)
            grads['conv_b_out'][j] = _fold(cs)
            du, dlg, dlb, dwb = _conv_act_bwd1(sv['u'], dsact, row(wt['conv_ln_g'][j]), row(wt['conv_ln_b'][j]))
            grads['conv_ln_g'][j], grads['conv_ln_b'][j], grads['conv_dw_b'][j] = _fold(dlg), _fold(dlb), _fold(dwb)
            dp, ddw = _conv_act_bwd2(du.reshape(bsz, s_len, d), sv['p'].reshape(bsz, s_len, 2 * d),
                                     _pad_rows(wt['conv_dw'][j], 32))
            grads['conv_dw'][j] = ddw.sum(axis=1)[:wt['conv_dw'].shape[1]]
            dp = dp.reshape(t, 2 * d)
            gw, cs = _wgrad(sv['hn'], dp, name="conv_in_wgrad")
            grads['conv_w_in'][j] = _split_lanes([gw], 2 * d // N_DEV)
            grads['conv_b_in'][j] = _fold(cs)
            dhn = _mm(dp, wt['conv_w_in'][j], trans_b=True, name="conv_in_dgrad")
            dx, dg = _rmsnorm_bwd(sv['x_mix'], gm, dhn, dx)
            grads['norm_mix'][i] = _fold(dg)
        elif kind == 1:
            shp = (bsz, s_len, d)
            dxn, dwp, dbp, dsc, dg = _pool_bwd(sv['x_mix'].reshape(shp), dx.reshape(shp), sv['p'], gm, wt['pool_w'][j],
                                               row(wt['pool_b'][j]), row(wt['pool_scale'][j]))
            dx = dxn.reshape(t, d)
            ng, cg = dwp.shape[0], dwp.shape[1]
            grads['pool_w'][j] = jnp.moveaxis(dwp.reshape(ng, N_DEV, cg // N_DEV, cg), 1, 0).astype(BF16)
            grads['pool_b'][j] = _fold(dbp).reshape(wt['pool_b'].shape[1:])
            grads['pool_scale'][j] = _fold(dsc)
            grads['norm_mix'][i] = _fold(dg)
        else:
            shp = (bsz, s_len, d)
            do = _mm(dx, wt['fox_w_o'][j], trans_b=True, name="fox_out_dgrad")
            gw, _ = _wgrad(sv['o'], dx, name="fox_out_wgrad")
            grads['fox_w_o'][j] = gw.reshape(N_DEV, d // N_DEV, d)
            fl_args = (sv['qn'].reshape(shp), sv['kn'].reshape(shp), sv['vb'].reshape(shp), do.reshape(shp),
                       sv['o'].reshape(shp), sv['lse'], sv['crow'])
            dq, dk, dv, dcrow = _flash_bwd(*fl_args)
            dc = jnp.swapaxes(dcrow.reshape(bsz, n_heads, s_len), 1, 2)
            dc = jnp.pad(dc, ((0, 0), (0, 0), (0, LANES - n_heads)))
            dfl, dbf = _fox_gate_bwd(dc, sv['proj'].reshape(bsz, s_len, -1), sv['bf'], n_heads)
            grads['fox_b_f'][j] = _fold(dbf)[:n_heads]
            dqkv, dgq, dgk = _fox_qknorm_bwd(sv['proj'], dq.reshape(t, d), dk.reshape(t, d), dv.reshape(t, d),
                                             sv['gq'], sv['gk'], d)
            grads['fox_q_gain'][j] = _fold(dgq).reshape(n_heads, HEAD_DIM).sum(axis=0)
            grads['fox_k_gain'][j] = _fold(dgk).reshape(n_heads, HEAD_DIM).sum(axis=0)
            dproj = jnp.concatenate([dqkv, dfl.reshape(t, LANES)], axis=1)
            dwp, _ = _wgrad(sv['hn'], dproj, name="fox_in_wgrad")
            grads['fox_w_in'][j] = _split_lanes([dwp], (3 * d + n_heads) // N_DEV)
            dhn = _mm(dproj, sv['wp'], trans_b=True, name="fox_in_dgrad")
            dx, dg = _rmsnorm_bwd(sv['x_mix'], gm, dhn, dx)
            grads['norm_mix'][i] = _fold(dg)

    listed = {n: [g[k] for k in sorted(g)] for n, g in grads.items()}
    small = {n: jnp.stack(g) for n, g in listed.items() if n not in MATRICES}
    return sq.sum(), dx.reshape(bsz, s_len, d), small, {n: listed[n] for n in MATRICES}


def _train_step(x, target, w, m, v):
    got = _gather([w[n].astype(BF16) for n in MATRICES] + [_flat_rows([w[n] for n in SMALL])])
    wt = _prepare_weights(dict(zip(MATRICES, got[:-1])), got[-1], w)
    wt.update({n: w[n] for n in REPLICATED})
    sq, grad_x, gsmall, gbig = _local_step(x, target, wt)
    d = x.shape[-1]

    shard_rows = jnp.concatenate([_to_shards(gsmall[n], SHARD_AXIS[n]) for n in SMALL], axis=1)
    rep = jnp.concatenate([gsmall[n].reshape(-1) for n in REPLICATED] + [(0.5 / d) * sq.reshape(1)])
    rows = jnp.concatenate([shard_rows, jnp.broadcast_to(rep, (N_DEV, rep.shape[0]))], axis=1)
    chunk = 8 * ADAM_COLS
    n_all = rows.shape[1]
    n_pad = -(-n_all // chunk) * chunk
    rows = jnp.pad(rows, ((0, 0), (0, n_pad - n_all))).reshape(N_DEV, n_pad // ADAM_COLS, ADAM_COLS)

    items, groups = [], []
    for n in MATRICES:
        groups.append(list(range(len(items), len(items) + len(gbig[n]))))
        items += gbig[n]
    groups.append([len(items)])
    items.append(rows)
    recv = _scatter(items, groups)

    res = [{}, {}, {}, {}]
    for n, r in zip(MATRICES, recv[:-1]):
        for k, o in enumerate(_adamw(r, w[n], m[n], v[n])):
            res[k][n] = o
    order = SMALL + REPLICATED

    def flat(tree):
        parts = jnp.concatenate([tree[n].reshape(-1) for n in order])
        return jnp.pad(parts, (0, n_pad - parts.shape[0])).reshape(n_pad // ADAM_COLS, ADAM_COLS)

    outs = [o.reshape(-1) for o in _adamw(recv[-1].reshape((N_CHIP,) + rows.shape[1:]), flat(w), flat(m), flat(v))]
    off = 0
    for n in order:
        size = w[n].size
        for k in range(4):
            res[k][n] = outs[k][off:off + size].reshape(w[n].shape)
        off += size
    loss = outs[0][n_all - 1]
    return (loss, grad_x, *[res[0][n] for n in WEIGHTS], *[res[1][n] for n in WEIGHTS],
            *[res[2][n] for n in WEIGHTS], *[res[3][n] for n in WEIGHTS])


def kernel(x, norm_mix, norm_ffn, conv_w_in, conv_b_in, conv_dw, conv_dw_b, conv_ln_g, conv_ln_b, conv_w_out, conv_b_out, pool_w, pool_b, pool_scale, fox_w_in, fox_b_f, fox_q_gain, fox_k_gain, fox_w_o, ffn_w_up, ffn_dw, ffn_dw_b, ffn_w_down, loss_target, m_norm_mix, m_norm_ffn, m_conv_w_in, m_conv_b_in, m_conv_dw, m_conv_dw_b, m_conv_ln_g, m_conv_ln_b, m_conv_w_out, m_conv_b_out, m_pool_w, m_pool_b, m_pool_scale, m_fox_w_in, m_fox_b_f, m_fox_q_gain, m_fox_k_gain, m_fox_w_o, m_ffn_w_up, m_ffn_dw, m_ffn_dw_b, m_ffn_w_down, v_norm_mix, v_norm_ffn, v_conv_w_in, v_conv_b_in, v_conv_dw, v_conv_dw_b, v_conv_ln_g, v_conv_ln_b, v_conv_w_out, v_conv_b_out, v_pool_w, v_pool_b, v_pool_scale, v_fox_w_in, v_fox_b_f, v_fox_q_gain, v_fox_k_gain, v_fox_w_o, v_ffn_w_up, v_ffn_dw, v_ffn_dw_b, v_ffn_w_down):
    w = dict(zip(WEIGHTS, (norm_mix, norm_ffn, conv_w_in, conv_b_in, conv_dw, conv_dw_b, conv_ln_g, conv_ln_b, conv_w_out, conv_b_out, pool_w, pool_b, pool_scale, fox_w_in, fox_b_f, fox_q_gain, fox_k_gain, fox_w_o, ffn_w_up, ffn_dw, ffn_dw_b, ffn_w_down)))
    m = dict(zip(WEIGHTS, (m_norm_mix, m_norm_ffn, m_conv_w_in, m_conv_b_in, m_conv_dw, m_conv_dw_b, m_conv_ln_g, m_conv_ln_b, m_conv_w_out, m_conv_b_out, m_pool_w, m_pool_b, m_pool_scale, m_fox_w_in, m_fox_b_f, m_fox_q_gain, m_fox_k_gain, m_fox_w_o, m_ffn_w_up, m_ffn_dw, m_ffn_dw_b, m_ffn_w_down)))
    v = dict(zip(WEIGHTS, (v_norm_mix, v_norm_ffn, v_conv_w_in, v_conv_b_in, v_conv_dw, v_conv_dw_b, v_conv_ln_g, v_conv_ln_b, v_conv_w_out, v_conv_b_out, v_pool_w, v_pool_b, v_pool_scale, v_fox_w_in, v_fox_b_f, v_fox_q_gain, v_fox_k_gain, v_fox_w_o, v_ffn_w_up, v_ffn_dw, v_ffn_dw_b, v_ffn_w_down)))
    return _train_step(x, loss_target, w, m, v)
```

```python
import functools
import math

import jax
import jax.numpy as jnp
from jax import lax
from jax.experimental import pallas as pl
from jax.experimental.pallas import tpu as pltpu

F32, BF16 = jnp.float32, jnp.bfloat16
SDS = jax.ShapeDtypeStruct

N_DEV = 8
EPS = 1e-6
POOL_WINDOWS = (2, 4, 8, 16)
HEAD_DIM = 64
ADAM_LR, ADAM_B1, ADAM_B2, ADAM_EPS, ADAM_WD, ADAM_STEP = 0.001, 0.9, 0.999, 1e-08, 0.01, 10
LANES = 128
VMEM_LIMIT_BYTES = 48 * 1024 * 1024
NEG = -1e30

WEIGHTS = ['norm_mix', 'norm_ffn', 'conv_w_in', 'conv_b_in', 'conv_dw', 'conv_dw_b', 'conv_ln_g', 'conv_ln_b',
           'conv_w_out', 'conv_b_out', 'pool_w', 'pool_b', 'pool_scale', 'fox_w_in', 'fox_b_f', 'fox_q_gain',
           'fox_k_gain', 'fox_w_o', 'ffn_w_up', 'ffn_dw', 'ffn_dw_b', 'ffn_w_down']
SHARD_AXIS = {'conv_w_in': 2, 'conv_b_in': 1, 'conv_dw': 2, 'conv_dw_b': 1, 'conv_ln_g': 1, 'conv_ln_b': 1,
              'conv_w_out': 1, 'conv_b_out': 1, 'pool_w': 2, 'pool_b': 2, 'fox_w_in': 2, 'fox_w_o': 1,
              'ffn_w_up': 2, 'ffn_dw': 2, 'ffn_w_down': 1}
MATRICES = ('conv_w_in', 'conv_w_out', 'pool_w', 'fox_w_in', 'fox_w_o', 'ffn_w_up', 'ffn_w_down')
SHARDED = [n for n in WEIGHTS if n in SHARD_AXIS]
REPLICATED = [n for n in WEIGHTS if n not in SHARD_AXIS]


def _cp(*sem):
    return pltpu.CompilerParams(dimension_semantics=sem, vmem_limit_bytes=VMEM_LIMIT_BYTES)


def _tile(n, pref, align=8):
    if n <= pref:
        return n
    t = (pref // align) * align
    while t >= align:
        if n % t == 0:
            return t
        t -= align
    return n


def _fold8(x):
    r, c = x.shape
    return x.reshape(r // 8, 8, c).sum(axis=0)


def _sigmoid(x):
    return 0.5 * jnp.tanh(0.5 * x) + 0.5


def _shifts_back(cur, tail, n):
    hb = tail.shape[0]
    xe = jnp.concatenate([tail, cur], axis=0)
    return [cur] + [pltpu.roll(xe, j, axis=0)[hb:] for j in range(1, n)]


def _shifts_fwd(cur, head, n):
    ts = cur.shape[0]
    xe = jnp.concatenate([cur, head], axis=0)
    ln = xe.shape[0]
    return [cur] + [pltpu.roll(xe, ln - j, axis=0)[:ts] for j in range(1, n)]


def _dot_hi(a, b):
    return jnp.dot(a, b, preferred_element_type=F32, precision=lax.Precision.HIGHEST)


def _rmsnorm_fwd(x, g):
    t, d = x.shape
    tm = _tile(t, 512)

    def body(x_ref, g_ref, h_ref):
        xv = x_ref[...]
        r = lax.rsqrt(jnp.mean(xv * xv, axis=-1, keepdims=True) + EPS)
        h_ref[...] = (xv * r * g_ref[...]).astype(BF16)

    return pl.pallas_call(
        body, out_shape=SDS((t, d), BF16), grid=(t // tm,), name="rmsnorm_fwd",
        in_specs=[pl.BlockSpec((tm, d), lambda i: (i, 0)), pl.BlockSpec((1, d), lambda i: (0, 0))],
        out_specs=pl.BlockSpec((tm, d), lambda i: (i, 0)), compiler_params=_cp("parallel"))(x, g)


def _rmsnorm_bwd(x, g, dh, dres):
    t, d = x.shape
    tm = _tile(t, 512)

    def body(x_ref, g_ref, dh_ref, dres_ref, dx_ref, dg_ref):
        @pl.when(pl.program_id(0) == 0)
        def _():
            dg_ref[...] = jnp.zeros_like(dg_ref)

        xv = x_ref[...]
        r = lax.rsqrt(jnp.mean(xv * xv, axis=-1, keepdims=True) + EPS)
        xh = xv * r
        dhv = dh_ref[...].astype(F32)
        u = dhv * g_ref[...]
        dx_ref[...] = dres_ref[...] + r * (u - xh * jnp.mean(u * xh, axis=-1, keepdims=True))
        dg_ref[...] += _fold8(dhv * xh)

    row = pl.BlockSpec((tm, d), lambda i: (i, 0))
    return pl.pallas_call(
        body, out_shape=(SDS((t, d), F32), SDS((8, d), F32)), grid=(t // tm,), name="rmsnorm_bwd",
        in_specs=[row, pl.BlockSpec((1, d), lambda i: (0, 0)), row, row],
        out_specs=(row, pl.BlockSpec((8, d), lambda i: (0, 0))), compiler_params=_cp("arbitrary"))(x, g, dh, dres)


def _mm(a, b, *, trans_b=False, bias=None, residual=None, a2=None, b2=None, out_dtype=BF16, name="mm"):
    m, k = a.shape
    n = b.shape[0] if trans_b else b.shape[1]
    tm, tn, tk = _tile(m, 1024, 16), _tile(n, 1536, LANES), _tile(k, 1536, LANES)
    nk = k // tk
    two = a2 is not None
    steps = 2 * nk if two else nk
    dims = (((1,), (1,)), ((), ())) if trans_b else (((1,), (0,)), ((), ()))
    has_bias, has_res = bias is not None, residual is not None

    def body(*refs):
        n_in = 4 if two else 2
        bias_ref = refs[n_in] if has_bias else None
        res_ref = refs[n_in + has_bias] if has_res else None

        def finish(r):
            if has_bias:
                r = r + bias_ref[...]
            if has_res:
                r = r + res_ref[...]
            return r.astype(out_dtype)

        def dot(a_ref, b_ref):
            return lax.dot_general(a_ref[...].astype(BF16), b_ref[...].astype(BF16), dims, preferred_element_type=F32)

        if steps == 1:
            refs[-1][...] = finish(dot(refs[0], refs[1]))
            return
        o_ref, acc_ref = refs[-2], refs[-1]
        kk = pl.program_id(2)

        @pl.when(kk == 0)
        def _():
            acc_ref[...] = jnp.zeros_like(acc_ref)

        @pl.when(kk < nk)
        def _():
            acc_ref[...] += dot(refs[0], refs[1])

        if two:
            @pl.when(kk >= nk)
            def _():
                acc_ref[...] += dot(refs[2], refs[3])

        @pl.when(kk == steps - 1)
        def _():
            o_ref[...] = finish(acc_ref[...])

    def pair(first):
        kmap = (lambda kk: jnp.minimum(kk, nk - 1)) if first else (lambda kk: jnp.maximum(kk - nk, 0))
        a_spec = pl.BlockSpec((tm, tk), lambda j, i, kk: (i, kmap(kk)))
        if trans_b:
            b_spec = pl.BlockSpec((tn, tk), lambda j, i, kk: (j, kmap(kk)))
        else:
            b_spec = pl.BlockSpec((tk, tn), lambda j, i, kk: (kmap(kk), j))
        return [a_spec, b_spec]

    in_specs, args = pair(True), [a, b]
    if two:
        in_specs += pair(False)
        args += [a2, b2]
    if has_bias:
        in_specs.append(pl.BlockSpec((1, tn), lambda j, i, kk: (0, j)))
        args.append(bias)
    if has_res:
        in_specs.append(pl.BlockSpec((tm, tn), lambda j, i, kk: (i, j)))
        args.append(residual)
    return pl.pallas_call(
        body, out_shape=SDS((m, n), out_dtype), grid=(n // tn, m // tm, steps), name=name,
        in_specs=in_specs, out_specs=pl.BlockSpec((tm, tn), lambda j, i, kk: (i, j)),
        scratch_shapes=[] if steps == 1 else [pltpu.VMEM((tm, tn), F32)],
        compiler_params=_cp("parallel", "parallel", "arbitrary"))(*args)


def _wgrad(a, g, *, out_dtype=BF16, name="wgrad"):
    m, ka = a.shape
    n = g.shape[1]
    ta, tn, tm = _tile(ka, 1536, LANES), _tile(n, 1536, LANES), _tile(m, 1024)
    nm = m // tm

    def body(a_ref, g_ref, o_ref, cs_ref, acc_ref):
        i, mm = pl.program_id(1), pl.program_id(2)

        @pl.when(mm == 0)
        def _():
            acc_ref[...] = jnp.zeros_like(acc_ref)

        @pl.when((mm == 0) & (i == 0))
        def _():
            cs_ref[...] = jnp.zeros_like(cs_ref)

        gv = g_ref[...]
        acc_ref[...] += lax.dot_general(a_ref[...].astype(BF16), gv.astype(BF16), (((0,), (0,)), ((), ())),
                                        preferred_element_type=F32)

        @pl.when(i == 0)
        def _():
            cs_ref[...] += _fold8(gv.astype(F32))

        @pl.when(mm == nm - 1)
        def _():
            o_ref[...] = acc_ref[...].astype(out_dtype)

    return pl.pallas_call(
        body, out_shape=(SDS((ka, n), out_dtype), SDS((8, n), F32)), grid=(n // tn, ka // ta, nm), name=name,
        in_specs=[pl.BlockSpec((tm, ta), lambda j, i, mm: (mm, i)), pl.BlockSpec((tm, tn), lambda j, i, mm: (mm, j))],
        out_specs=(pl.BlockSpec((ta, tn), lambda j, i, mm: (i, j)), pl.BlockSpec((8, tn), lambda j, i, mm: (0, j))),
        scratch_shapes=[pltpu.VMEM((ta, tn), F32)],
        compiler_params=_cp("arbitrary", "arbitrary", "arbitrary"))(a, g)


FFN_HALO = 16


def _ffn_conv(uc_ref, up_ref, w_ref, b_ref, s):
    u = uc_ref[...].astype(F32)
    tail = jnp.where(s > 0, up_ref[...].astype(F32), 0.0)
    sh = _shifts_back(u, tail, 3)
    return sh, sh[2] * w_ref[0:1, :] + sh[1] * w_ref[1:2, :] + sh[0] * w_ref[2:3, :] + b_ref[...]


def _ffn_act_fwd(uv, ug, dw8, b):
    bsz, s_len, f = uv.shape
    tc, ts = _tile(f, 256, LANES), _tile(s_len, 1024, FFN_HALO)
    nf, r = f // tc, ts // FFN_HALO

    def body(uv_ref, uvp_ref, ug_ref, ugp_ref, wv_ref, wg_ref, bv_ref, bg_ref, a_ref):
        s = pl.program_id(2)
        _, val = _ffn_conv(uv_ref, uvp_ref, wv_ref, bv_ref, s)
        _, gate = _ffn_conv(ug_ref, ugp_ref, wg_ref, bg_ref, s)
        a_ref[...] = (gate * _sigmoid(gate) * val).astype(BF16)

    cur = pl.BlockSpec((None, ts, tc), lambda bi, j, s: (bi, s, j))
    prev = pl.BlockSpec((None, FFN_HALO, tc), lambda bi, j, s: (bi, jnp.maximum(s * r - 1, 0), j))

    def par(rows, off):
        return pl.BlockSpec((rows, tc), lambda bi, j, s: (0, j + off))

    return pl.pallas_call(
        body, out_shape=SDS((bsz, s_len, f), BF16), grid=(bsz, nf, s_len // ts), name="ffn_act_fwd",
        in_specs=[cur, prev, cur, prev, par(8, 0), par(8, nf), par(1, 0), par(1, nf)], out_specs=cur,
        compiler_params=_cp("parallel", "parallel", "arbitrary"))(uv, uv, ug, ug, dw8, dw8, b, b)


def _ffn_act_bwd1(uv, ug, da, dw8, b):
    bsz, s_len, f = uv.shape
    tc, ts = _tile(f, 256, LANES), _tile(s_len, 1024, FFN_HALO)
    nf, r = f // tc, ts // FFN_HALO

    def body(uv_ref, uvp_ref, ug_ref, ugp_ref, da_ref, wv_ref, wg_ref, bv_ref, bg_ref,
             dvv_ref, dvg_ref, ddwv_ref, ddwg_ref, dbv_ref, dbg_ref):
        bi, s = pl.program_id(1), pl.program_id(2)

        @pl.when((bi == 0) & (s == 0))
        def _():
            for ref in (ddwv_ref, ddwg_ref, dbv_ref, dbg_ref):
                ref[...] = jnp.zeros_like(ref)

        shv, val = _ffn_conv(uv_ref, uvp_ref, wv_ref, bv_ref, s)
        shg, gate = _ffn_conv(ug_ref, ugp_ref, wg_ref, bg_ref, s)
        sg = _sigmoid(gate)
        dav = da_ref[...].astype(F32)
        for dv, sh, dv_ref, ddw_ref, db_ref in (
                (dav * gate * sg, shv, dvv_ref, ddwv_ref, dbv_ref),
                (dav * val * (sg * (1.0 + gate * (1.0 - sg))), shg, dvg_ref, ddwg_ref, dbg_ref)):
            dv_ref[...] = dv.astype(BF16)
            db_ref[...] += _fold8(dv)
            for k in range(3):
                ddw_ref[k] += _fold8(dv * sh[2 - k])

    cur = pl.BlockSpec((None, ts, tc), lambda j, bi, s: (bi, s, j))
    prev = pl.BlockSpec((None, FFN_HALO, tc), lambda j, bi, s: (bi, jnp.maximum(s * r - 1, 0), j))

    def par(rows, off):
        return pl.BlockSpec((rows, tc), lambda j, bi, s: (0, j + off))

    acc3 = pl.BlockSpec((3, 8, tc), lambda j, bi, s: (0, 0, j))
    acc1 = pl.BlockSpec((8, tc), lambda j, bi, s: (0, j))
    return pl.pallas_call(
        body, out_shape=(SDS((bsz, s_len, f), BF16), SDS((bsz, s_len, f), BF16), SDS((3, 8, f), F32),
                         SDS((3, 8, f), F32), SDS((8, f), F32), SDS((8, f), F32)),
        grid=(nf, bsz, s_len // ts), name="ffn_act_bwd1",
        in_specs=[cur, prev, cur, prev, cur, par(8, 0), par(8, nf), par(1, 0), par(1, nf)],
        out_specs=(cur, cur, acc3, acc3, acc1, acc1),
        compiler_params=_cp("arbitrary", "arbitrary", "arbitrary"))(uv, uv, ug, ug, da, dw8, dw8, b, b)


def _ffn_act_bwd2(dvv, dvg, dw8):
    bsz, s_len, f = dvv.shape
    tc, ts = _tile(f, 256, LANES), _tile(s_len, 1024, FFN_HALO)
    nf, r, ns = f // tc, ts // FFN_HALO, s_len // ts

    def body(vc_ref, vn_ref, gc_ref, gn_ref, wv_ref, wg_ref, duv_ref, dug_ref):
        s = pl.program_id(2)
        for dc_ref, dn_ref, w_ref, du_ref in ((vc_ref, vn_ref, wv_ref, duv_ref), (gc_ref, gn_ref, wg_ref, dug_ref)):
            d = dc_ref[...].astype(F32)
            head = jnp.where(s < ns - 1, dn_ref[...].astype(F32), 0.0)
            sh = _shifts_fwd(d, head, 3)
            du_ref[...] = (sh[0] * w_ref[2:3, :] + sh[1] * w_ref[1:2, :] + sh[2] * w_ref[0:1, :]).astype(BF16)

    cur = pl.BlockSpec((None, ts, tc), lambda bi, j, s: (bi, s, j))
    nxt = pl.BlockSpec((None, FFN_HALO, tc),
                       lambda bi, j, s: (bi, jnp.minimum((s + 1) * r, s_len // FFN_HALO - 1), j))

    def par(off):
        return pl.BlockSpec((8, tc), lambda bi, j, s: (0, j + off))

    return pl.pallas_call(
        body, out_shape=(SDS((bsz, s_len, f), BF16),) * 2, grid=(bsz, nf, ns), name="ffn_act_bwd2",
        in_specs=[cur, nxt, cur, nxt, par(0), par(nf)], out_specs=(cur, cur),
        compiler_params=_cp("parallel", "parallel", "arbitrary"))(dvv, dvv, dvg, dvg, dw8, dw8)


CONV_HALO = 32
CONV_CHUNK = 256


def _conv_act_fwd(p, dw32, dwb, ln_g, ln_b):
    bsz, s_len, d2 = p.shape
    d = d2 // 2
    kw = 31
    ts = _tile(s_len, 256, CONV_HALO)
    r = ts // CONV_HALO
    cc = min(CONV_CHUNK, d)

    def body(pc_ref, pp_ref, w_ref, wb_ref, g_ref, b_ref, u_ref, s_ref):
        s = pl.program_id(1)
        tot = jnp.zeros((ts, 1), F32)
        for c0 in range(0, d, cc):
            a = pc_ref[:, c0:c0 + cc].astype(F32)
            g = pc_ref[:, d + c0:d + c0 + cc].astype(F32)
            z = a * _sigmoid(g)
            ap = pp_ref[:, c0:c0 + cc].astype(F32)
            gp = pp_ref[:, d + c0:d + c0 + cc].astype(F32)
            tail = jnp.where(s > 0, ap * _sigmoid(gp), 0.0)
            sh = _shifts_back(z, tail, kw)
            acc = wb_ref[:, c0:c0 + cc] + sh[0] * w_ref[kw - 1:kw, c0:c0 + cc]
            for j in range(1, kw):
                acc = acc + sh[j] * w_ref[kw - 1 - j:kw - j, c0:c0 + cc]
            u_ref[:, c0:c0 + cc] = acc
            tot = tot + jnp.sum(acc, axis=-1, keepdims=True)
        u = u_ref[...]
        mu = tot / d
        uc = u - mu
        var = jnp.mean(uc * uc, axis=-1, keepdims=True)
        ul = uc * lax.rsqrt(var + EPS) * g_ref[...] + b_ref[...]
        s_ref[...] = (ul * _sigmoid(ul)).astype(BF16)

    vec = pl.BlockSpec((1, d), lambda bi, s: (0, 0))
    return pl.pallas_call(
        body, out_shape=(SDS((bsz, s_len, d), F32), SDS((bsz, s_len, d), BF16)), grid=(bsz, s_len // ts),
        name="conv_act_fwd",
        in_specs=[pl.BlockSpec((None, ts, d2), lambda bi, s: (bi, s, 0)),
                  pl.BlockSpec((None, CONV_HALO, d2), lambda bi, s: (bi, jnp.maximum(s * r - 1, 0), 0)),
                  pl.BlockSpec((32, d), lambda bi, s: (0, 0)), vec, vec, vec],
        out_specs=(pl.BlockSpec((None, ts, d), lambda bi, s: (bi, s, 0)),
                   pl.BlockSpec((None, ts, d), lambda bi, s: (bi, s, 0))),
        compiler_params=_cp("parallel", "arbitrary"))(p, p, dw32, dwb, ln_g, ln_b)


def _conv_act_bwd1(u, ds, ln_g, ln_b):
    t, d = u.shape
    ts = _tile(t, 256)

    def body(u_ref, ds_ref, g_ref, b_ref, du_ref, dg_ref, db_ref, dwb_ref):
        @pl.when(pl.program_id(0) == 0)
        def _():
            dg_ref[...] = jnp.zeros_like(dg_ref)
            db_ref[...] = jnp.zeros_like(db_ref)
            dwb_ref[...] = jnp.zeros_like(dwb_ref)

        uv = u_ref[...]
        uc = uv - jnp.mean(uv, axis=-1, keepdims=True)
        rstd = lax.rsqrt(jnp.mean(uc * uc, axis=-1, keepdims=True) + EPS)
        uh = uc * rstd
        ul = uh * g_ref[...] + b_ref[...]
        sg = _sigmoid(ul)
        dul = ds_ref[...].astype(F32) * (sg * (1.0 + ul * (1.0 - sg)))
        duh = dul * g_ref[...]
        du = rstd * (duh - jnp.mean(duh, axis=-1, keepdims=True) - uh * jnp.mean(duh * uh, axis=-1, keepdims=True))
        du_ref[...] = du
        dg_ref[...] += _fold8(dul * uh)
        db_ref[...] += _fold8(dul)
        dwb_ref[...] += _fold8(du)

    row = pl.BlockSpec((ts, d), lambda i: (i, 0))
    vec = pl.BlockSpec((1, d), lambda i: (0, 0))
    acc = pl.BlockSpec((8, d), lambda i: (0, 0))
    return pl.pallas_call(
        body, out_shape=(SDS((t, d), F32), SDS((8, d), F32), SDS((8, d), F32), SDS((8, d), F32)), grid=(t // ts,),
        name="conv_act_bwd1", in_specs=[row, row, vec, vec], out_specs=(row, acc, acc, acc),
        compiler_params=_cp("arbitrary"))(u, ds, ln_g, ln_b)


def _conv_act_bwd2(du, p, dw32):
    bsz, s_len, d2 = p.shape
    d = d2 // 2
    kw = 31
    ts = _tile(s_len, 256, CONV_HALO)
    r, ns = ts // CONV_HALO, s_len // ts
    cc = min(CONV_CHUNK, d)

    def body(dc_ref, dn_ref, pc_ref, pp_ref, w_ref, dp_ref, ddw_ref):
        bi, s = pl.program_id(0), pl.program_id(1)

        @pl.when((bi == 0) & (s == 0))
        def _():
            ddw_ref[...] = jnp.zeros_like(ddw_ref)

        for c0 in range(0, d, cc):
            a = pc_ref[:, c0:c0 + cc].astype(F32)
            g = pc_ref[:, d + c0:d + c0 + cc].astype(F32)
            sg = _sigmoid(g)
            z = a * sg
            ap = pp_ref[:, c0:c0 + cc].astype(F32)
            gp = pp_ref[:, d + c0:d + c0 + cc].astype(F32)
            tail = jnp.where(s > 0, ap * _sigmoid(gp), 0.0)
            duv = dc_ref[:, c0:c0 + cc]
            head = jnp.where(s < ns - 1, dn_ref[:, c0:c0 + cc], 0.0)
            zb = _shifts_back(z, tail, kw)
            for k in range(kw):
                ddw_ref[k, :, c0:c0 + cc] += _fold8(duv * zb[kw - 1 - k])
            df = _shifts_fwd(duv, head, kw)
            dz = df[0] * w_ref[kw - 1:kw, c0:c0 + cc]
            for j in range(1, kw):
                dz = dz + df[j] * w_ref[kw - 1 - j:kw - j, c0:c0 + cc]
            dp_ref[:, c0:c0 + cc] = (dz * sg).astype(BF16)
            dp_ref[:, d + c0:d + c0 + cc] = (dz * a * sg * (1.0 - sg)).astype(BF16)

    return pl.pallas_call(
        body, out_shape=(SDS((bsz, s_len, d2), BF16), SDS((32, 8, d), F32)), grid=(bsz, ns), name="conv_act_bwd2",
        in_specs=[pl.BlockSpec((None, ts, d), lambda bi, s: (bi, s, 0)),
                  pl.BlockSpec((None, CONV_HALO, d),
                               lambda bi, s: (bi, jnp.minimum((s + 1) * r, s_len // CONV_HALO - 1), 0)),
                  pl.BlockSpec((None, ts, d2), lambda bi, s: (bi, s, 0)),
                  pl.BlockSpec((None, CONV_HALO, d2), lambda bi, s: (bi, jnp.maximum(s * r - 1, 0), 0)),
                  pl.BlockSpec((32, d), lambda bi, s: (0, 0))],
        out_specs=(pl.BlockSpec((None, ts, d2), lambda bi, s: (bi, s, 0)),
                   pl.BlockSpec((32, 8, d), lambda bi, s: (0, 0, 0))),
        compiler_params=_cp("arbitrary", "arbitrary"))(du, du, p, p, dw32)


POOL_HALO = 16


def _pool_counts(s, ts, rows, w):
    t = s * ts + lax.broadcasted_iota(jnp.int32, (rows, 1), 0)
    return jnp.minimum(t + 1, w).astype(F32)


def _pool_fwd(x, gmix, w, b, scale):
    bsz, s_len, d = x.shape
    ng = len(POOL_WINDOWS)
    cg = d // ng
    ts = _tile(s_len, 512, POOL_HALO)
    r = ts // POOL_HALO

    def body(xc_ref, xp_ref, g_ref, w_ref, b_ref, sc_ref, y_ref, p_ref):
        s = pl.program_id(1)

        def norm(v):
            return v * lax.rsqrt(jnp.mean(v * v, axis=-1, keepdims=True) + EPS) * g_ref[...]

        xc = xc_ref[...]
        h = norm(xc)
        tail = jnp.where(s > 0, norm(xp_ref[...]), 0.0)
        for gi, win in enumerate(POOL_WINDOWS):
            lo, hi = gi * cg, (gi + 1) * cg
            hg = h[:, lo:hi]
            acc = jnp.concatenate([tail[:, lo:hi], hg], axis=0)
            step = 1
            while step < win:
                acc = acc + pltpu.roll(acc, step, axis=0)
                step *= 2
            pg = acc[POOL_HALO:] / _pool_counts(s, ts, ts, win) - hg
            pb = pg.astype(BF16)
            p_ref[:, lo:hi] = pb
            yg = jnp.dot(pb, w_ref[gi], preferred_element_type=F32) + b_ref[:, lo:hi]
            y_ref[:, lo:hi] = xc[:, lo:hi] + yg * sc_ref[:, lo:hi]

    vec = pl.BlockSpec((1, d), lambda bi, s: (0, 0))
    blk = pl.BlockSpec((None, ts, d), lambda bi, s: (bi, s, 0))
    return pl.pallas_call(
        body, out_shape=(SDS((bsz, s_len, d), F32), SDS((bsz, s_len, d), BF16)), grid=(bsz, s_len // ts),
        name="pool_fwd",
        in_specs=[blk, pl.BlockSpec((None, POOL_HALO, d), lambda bi, s: (bi, jnp.maximum(s * r - 1, 0), 0)),
                  vec, pl.BlockSpec((ng, cg, cg), lambda bi, s: (0, 0, 0)), vec, vec],
        out_specs=(blk, blk), compiler_params=_cp("parallel", "arbitrary"))(x, x, gmix, w, b, scale)


def _pool_bwd(x, dy, p, gmix, w, b, scale):
    bsz, s_len, d = x.shape
    ng = len(POOL_WINDOWS)
    cg = d // ng
    ts = _tile(s_len, 512, POOL_HALO)
    r, ns = ts // POOL_HALO, s_len // ts
    nt = (((1,), (1,)), ((), ()))
    tn = (((0,), (0,)), ((), ()))

    def body(x_ref, dy_ref, dyn_ref, p_ref, g_ref, w_ref, b_ref, sc_ref, dx_ref, dw_ref, db_ref, dsc_ref, dg_ref):
        bi, s = pl.program_id(0), pl.program_id(1)

        @pl.when((bi == 0) & (s == 0))
        def _():
            dw_ref[...] = jnp.zeros_like(dw_ref)
            db_ref[...] = jnp.zeros_like(db_ref)
            dsc_ref[...] = jnp.zeros_like(dsc_ref)
            dg_ref[...] = jnp.zeros_like(dg_ref)

        dy = dy_ref[...]
        dyy = dy * sc_ref[...]
        dyy_n = jnp.where(s < ns - 1, dyn_ref[...] * sc_ref[...], 0.0)
        db_ref[...] += _fold8(dyy)
        xv = x_ref[...]
        rr = lax.rsqrt(jnp.mean(xv * xv, axis=-1, keepdims=True) + EPS)
        xh = xv * rr
        for gi, win in enumerate(POOL_WINDOWS):
            lo, hi = gi * cg, (gi + 1) * cg
            pb = p_ref[:, lo:hi]
            wg = w_ref[gi]
            pre = jnp.dot(pb, wg, preferred_element_type=F32) + b_ref[:, lo:hi]
            dsc_ref[:, lo:hi] += _fold8(dy[:, lo:hi] * pre)
            dyb = dyy[:, lo:hi].astype(BF16)
            dw_ref[gi] += lax.dot_general(pb, dyb, tn, preferred_element_type=F32)
            dp = lax.dot_general(dyb, wg, nt, preferred_element_type=F32)
            dp_n = lax.dot_general(dyy_n[:, lo:hi].astype(BF16), wg, nt, preferred_element_type=F32)
            q = dp / _pool_counts(s, ts, ts, win)
            q_n = dp_n / _pool_counts(s + 1, ts, POOL_HALO, win)
            acc = jnp.concatenate([q, q_n], axis=0)
            ln = ts + POOL_HALO
            step = 1
            while step < win:
                acc = acc + pltpu.roll(acc, ln - step, axis=0)
                step *= 2
            dh = acc[:ts] - dp
            xhg = xh[:, lo:hi]
            dg_ref[:, lo:hi] += _fold8(dh * xhg)
            dx_ref[:, lo:hi] = dh * g_ref[:, lo:hi]
        u = dx_ref[...]
        dx_ref[...] = dy + rr * (u - xh * jnp.mean(u * xh, axis=-1, keepdims=True))

    vec = pl.BlockSpec((1, d), lambda bi, s: (0, 0))
    acc8 = pl.BlockSpec((8, d), lambda bi, s: (0, 0))
    blk = pl.BlockSpec((None, ts, d), lambda bi, s: (bi, s, 0))
    wspec = pl.BlockSpec((ng, cg, cg), lambda bi, s: (0, 0, 0))
    return pl.pallas_call(
        body, out_shape=(SDS((bsz, s_len, d), F32), SDS((ng, cg, cg), F32), SDS((8, d), F32), SDS((8, d), F32),
                         SDS((8, d), F32)),
        grid=(bsz, ns), name="pool_bwd",
        in_specs=[blk, blk,
                  pl.BlockSpec((None, POOL_HALO, d),
                               lambda bi, s: (bi, jnp.minimum((s + 1) * r, s_len // POOL_HALO - 1), 0)),
                  blk, vec, wspec, vec, vec],
        out_specs=(blk, wspec, acc8, acc8, acc8),
        compiler_params=_cp("arbitrary", "arbitrary"))(x, dy, dy, p, gmix, w, b, scale)


def _tri(n, upper):
    row = lax.broadcasted_iota(jnp.int32, (n, n), 0)
    col = lax.broadcasted_iota(jnp.int32, (n, n), 1)
    return jnp.where((col >= row) if upper else (col <= row), 1.0, 0.0).astype(F32)


def _fox_gate_fwd(proj, bf, n_heads):
    bsz, s_len, width = proj.shape
    col = width // LANES - 1
    ts = _tile(s_len, 512)

    def body(fl_ref, b_ref, c_ref, carry_ref):
        @pl.when(pl.program_id(1) == 0)
        def _():
            carry_ref[...] = jnp.zeros_like(carry_ref)

        xv = fl_ref[...] + b_ref[...]
        logf = jnp.minimum(xv, 0.0) - jnp.log(1.0 + jnp.exp(-jnp.abs(xv)))
        lane = lax.broadcasted_iota(jnp.int32, (1, LANES), 1)
        logf = jnp.where(lane < n_heads, logf, 0.0)
        c = _dot_hi(_tri(ts, False), logf) + carry_ref[0:1, :]
        c_ref[...] = c
        carry_ref[0:1, :] = c[ts - 1:ts, :]

    return pl.pallas_call(
        body, out_shape=SDS((bsz, s_len, LANES), F32), grid=(bsz, s_len // ts), name="fox_gate_fwd",
        in_specs=[pl.BlockSpec((None, ts, LANES), lambda bi, s: (bi, s, col)),
                  pl.BlockSpec((1, LANES), lambda bi, s: (0, 0))],
        out_specs=pl.BlockSpec((None, ts, LANES), lambda bi, s: (bi, s, 0)),
        scratch_shapes=[pltpu.VMEM((8, LANES), F32)],
        compiler_params=_cp("arbitrary", "arbitrary"))(proj, bf)


def _fox_gate_bwd(dc, proj, bf, n_heads):
    bsz, s_len, width = proj.shape
    col = width // LANES - 1
    ts = _tile(s_len, 512)
    ns = s_len // ts

    def body(dc_ref, fl_ref, b_ref, dfl_ref, db_ref, carry_ref):
        bi, s = pl.program_id(0), pl.program_id(1)

        @pl.when((bi == 0) & (s == 0))
        def _():
            db_ref[...] = jnp.zeros_like(db_ref)

        @pl.when(s == 0)
        def _():
            carry_ref[...] = jnp.zeros_like(carry_ref)

        dlogf = _dot_hi(_tri(ts, True), dc_ref[...]) + carry_ref[0:1, :]
        carry_ref[0:1, :] = dlogf[0:1, :]
        lane = lax.broadcasted_iota(jnp.int32, (1, LANES), 1)
        dfl = jnp.where(lane < n_heads, dlogf * (1.0 - _sigmoid(fl_ref[...] + b_ref[...])), 0.0)
        dfl_ref[...] = dfl.astype(BF16)
        db_ref[...] += _fold8(dfl)

    return pl.pallas_call(
        body, out_shape=(SDS((bsz, s_len, LANES), BF16), SDS((8, LANES), F32)), grid=(bsz, ns), name="fox_gate_bwd",
        in_specs=[pl.BlockSpec((None, ts, LANES), lambda bi, s: (bi, ns - 1 - s, 0)),
                  pl.BlockSpec((None, ts, LANES), lambda bi, s: (bi, ns - 1 - s, col)),
                  pl.BlockSpec((1, LANES), lambda bi, s: (0, 0))],
        out_specs=(pl.BlockSpec((None, ts, LANES), lambda bi, s: (bi, ns - 1 - s, 0)),
                   pl.BlockSpec((8, LANES), lambda bi, s: (0, 0))),
        scratch_shapes=[pltpu.VMEM((8, LANES), F32)],
        compiler_params=_cp("arbitrary", "arbitrary"))(dc, proj, bf)


def _head_maps(d):
    ch = lax.broadcasted_iota(jnp.int32, (d, LANES), 0) // HEAD_DIM
    hd = lax.broadcasted_iota(jnp.int32, (d, LANES), 1)
    e = jnp.where(ch == hd, 1.0, 0.0).astype(BF16)
    cht = lax.broadcasted_iota(jnp.int32, (LANES, d), 1) // HEAD_DIM
    hdt = lax.broadcasted_iota(jnp.int32, (LANES, d), 0)
    et = jnp.where(cht == hdt, 1.0, 0.0).astype(BF16)
    return e, et


def _dot_sel(x, e):
    a = x.astype(BF16)
    r = x - a.astype(F32)
    b = r.astype(BF16)
    c = (r - b.astype(F32)).astype(BF16)
    return (jnp.dot(a, e, preferred_element_type=F32) + jnp.dot(b, e, preferred_element_type=F32)
            + jnp.dot(c, e, preferred_element_type=F32))


def _fox_qknorm_fwd(proj, gq, gk, d):
    t = proj.shape[0]
    ts = _tile(t, 256)
    scale = 1.0 / math.sqrt(HEAD_DIM)

    def body(q_ref, k_ref, v_ref, gq_ref, gk_ref, qn_ref, kn_ref, vb_ref):
        e, et = _head_maps(d)

        def norm(v, g):
            r = lax.rsqrt(_dot_sel(v * v, e) / HEAD_DIM + EPS)
            return v * _dot_sel(r, et) * g

        qn_ref[...] = (norm(q_ref[...], gq_ref[...]) * scale).astype(BF16)
        kn_ref[...] = norm(k_ref[...], gk_ref[...]).astype(BF16)
        vb_ref[...] = v_ref[...].astype(BF16)

    def colblk(j):
        return pl.BlockSpec((ts, d), lambda i: (i, j))

    vec = pl.BlockSpec((1, d), lambda i: (0, 0))
    out = pl.BlockSpec((ts, d), lambda i: (i, 0))
    return pl.pallas_call(
        body, out_shape=(SDS((t, d), BF16),) * 3, grid=(t // ts,), name="fox_qknorm_fwd",
        in_specs=[colblk(0), colblk(1), colblk(2), vec, vec], out_specs=(out, out, out),
        compiler_params=_cp("parallel"))(proj, proj, proj, gq, gk)


def _fox_qknorm_bwd(proj, dq, dk, dv, gq, gk, d):
    t = proj.shape[0]
    ts = _tile(t, 256)
    scale = 1.0 / math.sqrt(HEAD_DIM)

    def body(q_ref, k_ref, dq_ref, dk_ref, dv_ref, gq_ref, gk_ref, dp_ref, dgq_ref, dgk_ref):
        @pl.when(pl.program_id(0) == 0)
        def _():
            dgq_ref[...] = jnp.zeros_like(dgq_ref)
            dgk_ref[...] = jnp.zeros_like(dgk_ref)

        e, et = _head_maps(d)

        def back(v, g, dn, dg_ref):
            r = _dot_sel(lax.rsqrt(_dot_sel(v * v, e) / HEAD_DIM + EPS), et)
            vh = v * r
            dg_ref[...] += _fold8(dn * vh)
            u = dn * g
            mh = _dot_sel(_dot_sel(u * vh, e) / HEAD_DIM, et)
            return r * (u - vh * mh)

        dp_ref[:, 0:d] = back(q_ref[...], gq_ref[...], dq_ref[...] * scale, dgq_ref).astype(BF16)
        dp_ref[:, d:2 * d] = back(k_ref[...], gk_ref[...], dk_ref[...], dgk_ref).astype(BF16)
        dp_ref[:, 2 * d:3 * d] = dv_ref[...]

    def colblk(j):
        return pl.BlockSpec((ts, d), lambda i: (i, j))

    row = pl.BlockSpec((ts, d), lambda i: (i, 0))
    vec = pl.BlockSpec((1, d), lambda i: (0, 0))
    acc = pl.BlockSpec((8, d), lambda i: (0, 0))
    return pl.pallas_call(
        body, out_shape=(SDS((t, 3 * d), BF16), SDS((8, d), F32), SDS((8, d), F32)), grid=(t // ts,),
        name="fox_qknorm_bwd", in_specs=[colblk(0), colblk(1), row, row, row, vec, vec],
        out_specs=(pl.BlockSpec((ts, 3 * d), lambda i: (i, 0)), acc, acc),
        compiler_params=_cp("arbitrary"))(proj, proj, dq, dk, dv, gq, gk)


ATT_BLOCK = 512
_NT = (((1,), (1,)), ((), ()))
_TN = (((0,), (0,)), ((), ()))


def _head_mask(h):
    return (lax.broadcasted_iota(jnp.int32, (1, LANES), 1) // HEAD_DIM) == h


def _causal(qi, ki, tq, tk):
    row = qi * tq + lax.broadcasted_iota(jnp.int32, (tq, 1), 0)
    col = ki * tk + lax.broadcasted_iota(jnp.int32, (1, tk), 1)
    return col <= row


def _direct_exchange(ins, outs, place, sems, gather):
    send_sems, recv_sems, local_sems = sems
    x, y, c, me = _mesh_place()
    copies = []
    for t in range(len(ins)):
        dst = outs[t].at[me] if gather else outs[place[t][0]].at[me, place[t][1]]
        copies.append(pltpu.make_async_copy(ins[t] if gather else ins[t].at[me], dst, local_sems.at[t]))
        for kbits in range(1, N_DEV):
            px = 1 - x if kbits & 4 else x
            py = 1 - y if kbits & 2 else y
            pc = 1 - c if kbits & 1 else c
            copies.append(pltpu.make_async_remote_copy(
                src_ref=ins[t] if gather else ins[t].at[4 * px + 2 * py + pc], dst_ref=dst,
                send_sem=send_sems.at[t, kbits - 1], recv_sem=recv_sems.at[t, kbits - 1],
                device_id=(px, py, pc), device_id_type=pl.DeviceIdType.MESH))
    return copies


def _exchange_scratch(n):
    return [pltpu.SemaphoreType.DMA((n, N_DEV - 1)), pltpu.SemaphoreType.DMA((n, N_DEV - 1)),
            pltpu.SemaphoreType.DMA((n,))]


def _flash_fwd(q, k, v, crow, gather=()):
    bsz, s_len, d = q.shape
    nj = d // LANES
    tq = tk = _tile(s_len, ATT_BLOCK, LANES)
    nq = s_len // tq
    ng = len(gather)

    pairs = [(a, b) for a in range(nq) for b in range(a + 1)]
    qtab = jnp.asarray([a for a, _ in pairs], jnp.int32)
    ktab = jnp.asarray([b for _, b in pairs], jnp.int32)

    def body(qtab_ref, ktab_ref, q_ref, k_ref, v_ref, c_ref, *rest):
        g_in, (o_ref, lse_ref), g_out = rest[:ng], rest[ng:ng + 2], rest[ng + 2:2 * ng + 2]
        m_ref, l_ref, acc_ref = rest[2 * ng + 2:2 * ng + 5]
        step_id = pl.program_id(2)
        qi, ki = qtab_ref[step_id], ktab_ref[step_id]
        if ng:
            sems = rest[2 * ng + 5:]
            outer = (pl.program_id(0), pl.program_id(1))

            @pl.when((outer[0] == 0) & (outer[1] == 0) & (step_id == 0))
            def _():
                for cp in _direct_exchange(g_in, g_out, None, sems, True):
                    cp.start()

            @pl.when((outer[0] == bsz - 1) & (outer[1] == nj - 1) & (step_id == len(pairs) - 1))
            def _():
                for cp in _direct_exchange(g_in, g_out, None, sems, True):
                    cp.wait()

        @pl.when(ki == 0)
        def _():
            m_ref[...] = jnp.full_like(m_ref, NEG)
            l_ref[...] = jnp.zeros_like(l_ref)
            acc_ref[...] = jnp.zeros_like(acc_ref)

        def step(masked):
            qv, kv, vv = q_ref[...], k_ref[...], v_ref[...]
            for h in range(2):
                qh = jnp.where(_head_mask(h), qv, jnp.zeros_like(qv))
                s = lax.dot_general(qh, kv, _NT, preferred_element_type=F32) - c_ref[h:h + 1, :]
                if masked:
                    s = jnp.where(_causal(qi, ki, tq, tk), s, NEG)
                m_prev = m_ref[h]
                m_new = jnp.maximum(m_prev, jnp.max(s, axis=1, keepdims=True))
                pm = jnp.exp(s - m_new)
                alpha = jnp.exp(m_prev - m_new)
                l_ref[h] = alpha * l_ref[h] + jnp.sum(pm, axis=1, keepdims=True)
                p_hi = pm.astype(BF16)
                p_lo = (pm - p_hi.astype(F32)).astype(BF16)
                acc_ref[h] = (alpha * acc_ref[h] + jnp.dot(p_hi, vv, preferred_element_type=F32)
                              + jnp.dot(p_lo, vv, preferred_element_type=F32))
                m_ref[h] = m_new

        pl.when(ki < qi)(functools.partial(step, False))
        pl.when(ki == qi)(functools.partial(step, True))

        @pl.when(ki == qi)
        def _():
            m0 = _head_mask(0)
            o_ref[...] = jnp.where(m0, acc_ref[0] / l_ref[0], acc_ref[1] / l_ref[1])
            lse_ref[...] = jnp.where(m0, m_ref[0] + jnp.log(l_ref[0]), m_ref[1] + jnp.log(l_ref[1]))

    qblk = pl.BlockSpec((None, tq, LANES), lambda bi, j, t, qt, kt: (bi, qt[t], j))
    kblk = pl.BlockSpec((None, tk, LANES), lambda bi, j, t, qt, kt: (bi, kt[t], j))
    hbm = pl.BlockSpec(memory_space=pl.ANY)
    outs = pl.pallas_call(
        body, out_shape=[SDS((bsz, s_len, d), F32), SDS((bsz, nj, s_len, LANES), F32)]
        + [SDS((N_DEV,) + tuple(a.shape), a.dtype) for a in gather], name="flash_fwd",
        grid_spec=pltpu.PrefetchScalarGridSpec(
            num_scalar_prefetch=2, grid=(bsz, nj, len(pairs)),
            in_specs=[qblk, kblk, kblk,
                      pl.BlockSpec((None, None, 2, tk), lambda bi, j, t, qt, kt: (bi, j, 0, kt[t]))] + [hbm] * ng,
            out_specs=[qblk, pl.BlockSpec((None, None, tq, LANES), lambda bi, j, t, qt, kt: (bi, j, qt[t], 0))]
            + [hbm] * ng,
            scratch_shapes=[pltpu.VMEM((2, tq, 1), F32), pltpu.VMEM((2, tq, 1), F32),
                            pltpu.VMEM((2, tq, LANES), F32)] + (_exchange_scratch(ng) if ng else [])),
        compiler_params=_cp("arbitrary", "arbitrary", "arbitrary"))(qtab, ktab, q, k, v, crow, *gather)
    return outs[0], outs[1], list(outs[2:])


def _flash_probs(qv, kv, vv, dov, ov, lse, c_ref, h, mask):
    hm = _head_mask(h)
    qh = jnp.where(hm, qv, jnp.zeros_like(qv))
    s = lax.dot_general(qh, kv, _NT, preferred_element_type=F32) - c_ref[h:h + 1, :]
    pm = jnp.exp(s - lse[:, h * HEAD_DIM:h * HEAD_DIM + 1])
    if mask is not None:
        pm = jnp.where(mask, pm, 0.0)
    doh = jnp.where(hm, dov, jnp.zeros_like(dov))
    dpm = lax.dot_general(doh, vv, _NT, preferred_element_type=F32)
    delta = jnp.sum(jnp.where(hm, dov.astype(F32) * ov, 0.0), axis=1, keepdims=True)
    return pm, pm * (dpm - delta)


def _flash_bwd(q, k, v, do, o, lse, crow, items=(), groups=()):
    bsz, s_len, d = q.shape
    nj = d // LANES
    tq = tk = _tile(s_len, ATT_BLOCK, LANES)
    nq = s_len // tq

    pairs = [(b, a) for b in range(nq) for a in range(b, nq)]
    n_live = len(pairs)
    ktab = jnp.asarray([b for b, _ in pairs] + [nq - 1] * nq, jnp.int32)
    qtab = jnp.asarray([a for _, a in pairs] + list(range(nq)), jnp.int32)

    n_it, n_grp = len(items), len(groups)
    place = {it: (g, l) for g, members in enumerate(groups) for l, it in enumerate(members)}

    def body(ktab_ref, qtab_ref, q_ref, k_ref, v_ref, do_ref, o_ref, lse_ref, c_ref, *rest):
        x_in, (dq_ref, dk_ref, dv_ref, dc_ref) = rest[:n_it], rest[n_it:n_it + 4]
        x_out = rest[n_it + 4:n_it + 4 + n_grp]
        dqa_ref, dka_ref, dva_ref, dca_ref = rest[n_it + 4 + n_grp:n_it + 8 + n_grp]
        step_id = pl.program_id(2)
        ki, qi = ktab_ref[step_id], qtab_ref[step_id]
        live = step_id < n_live
        rows = pl.ds(pl.multiple_of(qi * tq, tq), tq)
        if n_it:
            sems = rest[n_it + 8 + n_grp:]
            outer = (pl.program_id(0), pl.program_id(1))

            @pl.when((outer[0] == 0) & (outer[1] == 0) & (step_id == 0))
            def _():
                for cp in _direct_exchange(x_in, x_out, place, sems, False):
                    cp.start()

            @pl.when((outer[0] == bsz - 1) & (outer[1] == nj - 1) & (step_id == n_live + nq - 1))
            def _():
                for cp in _direct_exchange(x_in, x_out, place, sems, False):
                    cp.wait()

        @pl.when(step_id == 0)
        def _():
            dqa_ref[...] = jnp.zeros_like(dqa_ref)

        @pl.when(live & (qi == ki))
        def _():
            dka_ref[...] = jnp.zeros_like(dka_ref)
            dva_ref[...] = jnp.zeros_like(dva_ref)
            dca_ref[...] = jnp.zeros_like(dca_ref)

        def step(masked):
            qv, kv, vv, dov, ov, lse = q_ref[...], k_ref[...], v_ref[...], do_ref[...], o_ref[...], lse_ref[...]
            mask = _causal(qi, ki, tq, tk) if masked else None
            for h in range(2):
                pm, ds = _flash_probs(qv, kv, vv, dov, ov, lse, c_ref, h, mask)
                dsb = ds.astype(BF16)
                dva_ref[h] += lax.dot_general(pm.astype(BF16), dov, _TN, preferred_element_type=F32)
                dka_ref[h] += lax.dot_general(dsb, qv, _TN, preferred_element_type=F32)
                dqa_ref[h, rows, :] += jnp.dot(dsb, kv, preferred_element_type=F32)
                dca_ref[h:h + 1, :] -= jnp.sum(ds, axis=0, keepdims=True)

        pl.when(live & (qi > ki))(functools.partial(step, False))
        pl.when(live & (qi == ki))(functools.partial(step, True))

        @pl.when(live & (qi == nq - 1))
        def _():
            m0 = _head_mask(0)
            dk_ref[...] = jnp.where(m0, dka_ref[0], dka_ref[1])
            dv_ref[...] = jnp.where(m0, dva_ref[0], dva_ref[1]).astype(BF16)
            dc_ref[...] = dca_ref[0:2, :]

        @pl.when(jnp.logical_not(live))
        def _():
            dq_ref[...] = jnp.where(_head_mask(0), dqa_ref[0, rows, :], dqa_ref[1, rows, :])

    def qside(bi, j, t, kt, qt):
        return (bi, jnp.where(t < n_live, qt[t], nq - 1), j)

    def kside(bi, j, t, kt, qt):
        return (bi, kt[t], j)

    def dqside(bi, j, t, kt, qt):
        return (bi, jnp.where(t < n_live, 0, qt[t]), j)

    qblk, kblk = pl.BlockSpec((None, tq, LANES), qside), pl.BlockSpec((None, tk, LANES), kside)
    cblk = pl.BlockSpec((None, None, 2, tk), lambda bi, j, t, kt, qt: (bi, j, 0, kt[t]))
    hbm = pl.BlockSpec(memory_space=pl.ANY)
    outs = pl.pallas_call(
        body, out_shape=[SDS((bsz, s_len, d), F32), SDS((bsz, s_len, d), F32), SDS((bsz, s_len, d), BF16),
                         SDS((bsz, nj, 2, s_len), F32)]
        + [SDS((N_DEV, len(members)) + tuple(items[members[0]].shape[1:]), items[members[0]].dtype)
           for members in groups], name="flash_bwd",
        grid_spec=pltpu.PrefetchScalarGridSpec(
            num_scalar_prefetch=2, grid=(bsz, nj, n_live + nq),
            in_specs=[qblk, kblk, kblk, qblk, qblk,
                      pl.BlockSpec((None, None, tq, LANES),
                                   lambda bi, j, t, kt, qt: (bi, j, jnp.where(t < n_live, qt[t], nq - 1), 0)),
                      cblk] + [hbm] * n_it,
            out_specs=[pl.BlockSpec((None, tq, LANES), dqside), kblk, kblk, cblk] + [hbm] * n_grp,
            scratch_shapes=[pltpu.VMEM((2, s_len, LANES), F32), pltpu.VMEM((2, tk, LANES), F32),
                            pltpu.VMEM((2, tk, LANES), F32), pltpu.VMEM((8, tk), F32)]
            + (_exchange_scratch(n_it) if n_it else [])),
        compiler_params=_cp("arbitrary", "arbitrary", "arbitrary"))(ktab, qtab, q, k, v, do, o, lse, crow, *items)
    return outs[0], outs[1], outs[2], outs[3], list(outs[4:])


def _loss_head(y, target):
    t, d = y.shape
    tm = _tile(t, 512)

    def body(y_ref, t_ref, dy_ref, acc_ref):
        @pl.when(pl.program_id(0) == 0)
        def _():
            acc_ref[...] = jnp.zeros_like(acc_ref)

        err = y_ref[...] - t_ref[...]
        dy_ref[...] = err / d
        acc_ref[...] += _fold8(err * err)

    row = pl.BlockSpec((tm, d), lambda i: (i, 0))
    return pl.pallas_call(
        body, out_shape=(SDS((t, d), F32), SDS((8, d), F32)), grid=(t // tm,), name="loss_head",
        in_specs=[row, row], out_specs=(row, pl.BlockSpec((8, d), lambda i: (0, 0))),
        compiler_params=_cp("arbitrary"))(y, target)


ADAM_COLS = 1024


def _adamw(g8, w, m, v):
    shape = w.shape
    cols = shape[-1]
    rows = w.size // cols
    n_parts = g8.shape[0]
    g8, w, m, v = g8.reshape(n_parts, rows, cols), w.reshape(rows, cols), m.reshape(rows, cols), v.reshape(rows, cols)
    tr = _tile(rows, 256, 16)
    c1 = 1.0 - ADAM_B1 ** ADAM_STEP
    c2 = 1.0 - ADAM_B2 ** ADAM_STEP

    def body(g8_ref, w_ref, m_ref, v_ref, g_ref, d_ref, nm_ref, nv_ref):
        g = g8_ref[0].astype(F32)
        for i in range(1, n_parts):
            g = g + g8_ref[i].astype(F32)
        mn = ADAM_B1 * m_ref[...] + (1.0 - ADAM_B1) * g
        vn = ADAM_B2 * v_ref[...] + (1.0 - ADAM_B2) * (g * g)
        g_ref[...] = g
        nm_ref[...] = mn
        nv_ref[...] = vn
        d_ref[...] = -ADAM_LR * ((mn / c1) / (jnp.sqrt(vn / c2) + ADAM_EPS) + ADAM_WD * w_ref[...])

    blk = pl.BlockSpec((tr, cols), lambda i: (i, 0))
    outs = pl.pallas_call(
        body, out_shape=(SDS((rows, cols), F32),) * 4, grid=(rows // tr,), name="adamw",
        in_specs=[pl.BlockSpec((n_parts, tr, cols), lambda i: (0, i, 0)), blk, blk, blk], out_specs=(blk,) * 4,
        compiler_params=_cp("parallel"))(g8, w, m, v)
    return [o.reshape(shape) for o in outs]


def _mesh_place():
    x, y, c = lax.axis_index("x"), lax.axis_index("y"), lax.axis_index("c")
    return x, y, c, 4 * x + 2 * y + c


def _gather(shards):
    n = len(shards)

    def body(*refs):
        ins, outs = refs[:n], refs[n:2 * n]
        send_sems, recv_sems, local_sems = refs[2 * n:]
        x, y, c, me = _mesh_place()
        sibling = (x, y, 1 - c)
        chips = [(1 - x, y), (x, 1 - y), (1 - x, 1 - y)]

        def block(px, py, pc):
            return 4 * px + 2 * py + pc

        def copy(t, k, blk, to, src=None):
            return pltpu.make_async_remote_copy(
                src_ref=outs[t].at[blk] if src is None else src, dst_ref=outs[t].at[blk],
                send_sem=send_sems.at[t, k], recv_sem=recv_sems.at[t, k], device_id=to,
                device_id_type=pl.DeviceIdType.MESH)

        own = [pltpu.make_async_copy(ins[t], outs[t].at[me], local_sems.at[t]) for t in range(n)]
        first = []
        for t in range(n):
            own[t].start()
            first.append(copy(t, 0, me, sibling, src=ins[t]))
            first += [copy(t, 1 + j, me, (*chip, c), src=ins[t]) for j, chip in enumerate(chips)]
        for cp in first:
            cp.start()
        passed = []
        for j, chip in enumerate(chips):
            for t in range(n):
                copy(t, 1 + j, block(*chip, c), (x, y, c)).wait_recv()
                cp = copy(t, 4 + j, block(*chip, c), sibling)
                cp.start()
                passed.append(cp)
        for t in range(n):
            copy(t, 0, block(x, y, 1 - c), (x, y, c)).wait_recv()
            for j, chip in enumerate(chips):
                copy(t, 4 + j, block(*chip, 1 - c), (x, y, c)).wait_recv()
        for cp in first + passed:
            cp.wait_send()
        for cp in own:
            cp.wait()

    hbm = pl.BlockSpec(memory_space=pl.ANY)
    return pl.pallas_call(
        body, out_shape=[SDS((N_DEV,) + tuple(s.shape), s.dtype) for s in shards], name="gather",
        in_specs=[hbm] * n, out_specs=[hbm] * n,
        scratch_shapes=[pltpu.SemaphoreType.DMA((n, N_DEV - 1)), pltpu.SemaphoreType.DMA((n, N_DEV - 1)),
                        pltpu.SemaphoreType.DMA((n,))])(*shards)


N_CHIP = N_DEV // 2


def _scatter_core(items):
    n = len(items)

    def body(*refs):
        ins, outs = refs[:n], refs[n:2 * n]
        send_sems, recv_sems = refs[2 * n:]
        x, y, c, _ = _mesh_place()
        copies = []
        for it in range(n):
            for r in range(N_CHIP):
                cp = pltpu.make_async_remote_copy(
                    src_ref=ins[it].at[2 * r + 1 - c], dst_ref=outs[it].at[r], send_sem=send_sems.at[it, r],
                    recv_sem=recv_sems.at[it, r], device_id=(x, y, 1 - c), device_id_type=pl.DeviceIdType.MESH)
                cp.start()
                copies.append(cp)
        for cp in copies:
            cp.wait()

    hbm = pl.BlockSpec(memory_space=pl.ANY)
    return pl.pallas_call(
        body, out_shape=[SDS((N_CHIP,) + tuple(a.shape[1:]), a.dtype) for a in items], name="scatter_core",
        in_specs=[hbm] * n, out_specs=[hbm] * n,
        scratch_shapes=[pltpu.SemaphoreType.DMA((n, N_CHIP)), pltpu.SemaphoreType.DMA((n, N_CHIP))])(*items)


def _pair_add(item, other):
    shape = item.shape[1:]
    cols = shape[-1]
    rows = math.prod(shape) // cols
    tr = _tile(rows, 512, 16)

    def body(x_ref, o_ref, h_ref):
        c = lax.axis_index("c")
        mine = jnp.where(c == 0, x_ref[0].astype(F32), x_ref[1].astype(F32))
        h_ref[...] = (mine + o_ref[...].astype(F32)).astype(item.dtype)

    return pl.pallas_call(
        body, out_shape=SDS((N_CHIP, rows, cols), item.dtype), grid=(N_CHIP, rows // tr), name="pair_add",
        in_specs=[pl.BlockSpec((None, 2, tr, cols), lambda r, i: (r, 0, i, 0)),
                  pl.BlockSpec((None, tr, cols), lambda r, i: (r, i, 0))],
        out_specs=pl.BlockSpec((None, tr, cols), lambda r, i: (r, i, 0)),
        compiler_params=_cp("parallel", "parallel"))(
            item.reshape(N_CHIP, 2, rows, cols), other.reshape(N_CHIP, rows, cols)).reshape((N_CHIP,) + shape)


def _scatter_chip(items, groups):
    n = len(items)
    place = {it: (g, l) for g, members in enumerate(groups) for l, it in enumerate(members)}

    def body(*refs):
        ins, outs = refs[:n], refs[n:n + len(groups)]
        send_sems, recv_sems, local_sems = refs[n + len(groups):]
        x, y, c, _ = _mesh_place()
        chip = 2 * x + y
        copies = []
        for it in range(n):
            g, l = place[it]
            own = pltpu.make_async_copy(ins[it].at[chip], outs[g].at[chip, l], local_sems.at[it])
            own.start()
            copies.append(own)
            for kbits in range(1, N_CHIP):
                px = 1 - x if kbits & 2 else x
                py = 1 - y if kbits & 1 else y
                cp = pltpu.make_async_remote_copy(
                    src_ref=ins[it].at[2 * px + py], dst_ref=outs[g].at[chip, l],
                    send_sem=send_sems.at[it, kbits - 1], recv_sem=recv_sems.at[it, kbits - 1],
                    device_id=(px, py, c), device_id_type=pl.DeviceIdType.MESH)
                cp.start()
                copies.append(cp)
        for cp in copies:
            cp.wait()

    hbm = pl.BlockSpec(memory_space=pl.ANY)
    out_shape = [SDS((N_CHIP, len(members)) + tuple(items[members[0]].shape[1:]), items[members[0]].dtype)
                 for members in groups]
    return pl.pallas_call(
        body, out_shape=out_shape, name="scatter_chip", in_specs=[hbm] * n, out_specs=[hbm] * len(groups),
        scratch_shapes=[pltpu.SemaphoreType.DMA((n, N_CHIP - 1)), pltpu.SemaphoreType.DMA((n, N_CHIP - 1)),
                        pltpu.SemaphoreType.DMA((n,))])(*items)


def _scatter(items, groups):
    halves = _scatter_core(items)
    return _scatter_chip([_pair_add(a, h) for a, h in zip(items, halves)], groups)


def _cat_lanes(g, layer, nb, blk, width):
    _, _, rows, c = g.shape
    tr = _tile(rows, 256, 16)

    def body(g_ref, o_ref):
        for p in range(nb):
            o_ref[:, p * c:(p + 1) * c] = g_ref[p]
        if width > nb * c:
            o_ref[:, nb * c:] = jnp.zeros((tr, width - nb * c), g.dtype)

    return pl.pallas_call(
        body, out_shape=SDS((rows, width), g.dtype), grid=(rows // tr,), name="cat_lanes",
        in_specs=[pl.BlockSpec((nb, None, tr, c), lambda i: (blk, layer, i, 0))],
        out_specs=pl.BlockSpec((tr, width), lambda i: (i, 0)),
        compiler_params=_cp("parallel"))(g)


def _split_lanes(parts, c):
    rows = parts[0].shape[0]
    counts = [p.shape[1] // c for p in parts]
    tr = _tile(rows, 256, 16)

    def body(*refs):
        o_ref = refs[-1]
        q = 0
        for x_ref, cnt in zip(refs[:-1], counts):
            for p in range(cnt):
                o_ref[q] = x_ref[:, p * c:(p + 1) * c]
                q += 1

    return pl.pallas_call(
        body, out_shape=SDS((sum(counts), rows, c), parts[0].dtype), grid=(rows // tr,), name="split_lanes",
        in_specs=[pl.BlockSpec((tr, p.shape[1]), lambda i: (i, 0)) for p in parts],
        out_specs=pl.BlockSpec((sum(counts), tr, c), lambda i: (0, i, 0)),
        compiler_params=_cp("parallel"))(*parts)


def _unshard(g8, shard_shape, axis):
    full = jnp.moveaxis(g8.reshape((N_DEV,) + tuple(shard_shape)), 0, axis)
    shape = list(shard_shape)
    shape[axis] *= N_DEV
    return full.reshape(shape)


def _to_shards(full, axis):
    shape = list(full.shape)
    shape[axis:axis + 1] = [N_DEV, shape[axis] // N_DEV]
    return jnp.moveaxis(full.reshape(shape), axis, 0).reshape(N_DEV, -1)


SMALL = [n for n in SHARDED if n not in MATRICES]


def _flat_rows(parts):
    flat = jnp.concatenate([p.reshape(-1) for p in parts])
    chunk = 8 * ADAM_COLS
    n = -(-flat.shape[0] // chunk) * chunk
    return jnp.pad(flat, (0, n - flat.shape[0])).reshape(n // ADAM_COLS, ADAM_COLS)


def _prepare_vectors(small, shards):
    wt = {}
    flat = small.reshape(N_DEV, -1)
    off = 0
    for n in SMALL:
        size = shards[n].size
        wt[n] = _unshard(flat[:, off:off + size], shards[n].shape, SHARD_AXIS[n])
        off += size
    return wt


def _prepare_matrices(gathered):
    wt = {}
    for n in ('conv_w_out', 'fox_w_o', 'ffn_w_down'):
        if n in gathered:
            g = gathered[n]
            wt[n] = [g[:, l].reshape(N_DEV * g.shape[2], g.shape[3]) for l in range(g.shape[1])]
    if 'pool_w' in gathered:
        g = gathered['pool_w']
        wt['pool_w'] = [jnp.moveaxis(g[:, l], 0, 1).reshape(g.shape[2], N_DEV * g.shape[3], g.shape[4])
                        for l in range(g.shape[1])]
    if 'conv_w_in' in gathered:
        g = gathered['conv_w_in']
        wt['conv_w_in'] = [_cat_lanes(g, l, N_DEV, 0, N_DEV * g.shape[3]) for l in range(g.shape[1])]
    if 'fox_w_in' in gathered:
        g = gathered['fox_w_in']
        wt['fox_w_in'] = [_cat_lanes(g, l, N_DEV, 0, 3 * g.shape[2] + LANES) for l in range(g.shape[1])]
    if 'ffn_w_up' in gathered:
        g = gathered['ffn_w_up']
        half = N_DEV // 2
        wt['ffn_w_up_v'] = [_cat_lanes(g, l, half, 0, half * g.shape[3]) for l in range(g.shape[1])]
        wt['ffn_w_up_g'] = [_cat_lanes(g, l, half, 1, half * g.shape[3]) for l in range(g.shape[1])]
    return wt


def _pad_rows(w, rows):
    return jnp.pad(w, ((0, rows - w.shape[0]), (0, 0)))


def _fold(acc):
    return acc.sum(axis=0)


def _local_step(x, target, wt, late_shards=None, cut=None):
    wt = dict(wt)
    late_names = [n for n in MATRICES if late_shards and n in late_shards]
    late_recv = {}
    bsz, s_len, d = x.shape
    t = bsz * s_len
    depth = wt['norm_mix'].shape[0]
    n_heads = d // HEAD_DIM
    f = wt['ffn_w_up_v'][0].shape[1]
    row = lambda a: a.reshape(1, -1)
    grads = {n: {} for n in WEIGHTS}
    saved = []

    xc = x.reshape(t, d)
    for i in range(depth):
        j = i // 3
        kind = i % 3
        sv = {'x_mix': xc}
        gm = row(wt['norm_mix'][i])
        if kind == 0:
            hn = _rmsnorm_fwd(xc, gm)
            p = _mm(hn, wt['conv_w_in'][j], bias=row(wt['conv_b_in'][j]), name="conv_in")
            u, sact = _conv_act_fwd(p.reshape(bsz, s_len, 2 * d), _pad_rows(wt['conv_dw'][j], 32),
                                    row(wt['conv_dw_b'][j]), row(wt['conv_ln_g'][j]), row(wt['conv_ln_b'][j]))
            sact = sact.reshape(t, d)
            xn = _mm(sact, wt['conv_w_out'][j], bias=row(wt['conv_b_out'][j]), residual=xc, out_dtype=F32,
                     name="conv_out")
            sv.update(hn=hn, p=p, u=u.reshape(t, d), sact=sact)
        elif kind == 1:
            xn, pp = _pool_fwd(xc.reshape(bsz, s_len, d), gm, wt['pool_w'][j], row(wt['pool_b'][j]),
                               row(wt['pool_scale'][j]))
            xn = xn.reshape(t, d)
            sv.update(p=pp)
        else:
            hn = _rmsnorm_fwd(xc, gm)
            wp = wt['fox_w_in'][j]
            bf = jnp.pad(wt['fox_b_f'][j], (0, LANES - n_heads)).reshape(1, LANES)
            gq = jnp.tile(wt['fox_q_gain'][j], n_heads).reshape(1, d)
            gk = jnp.tile(wt['fox_k_gain'][j], n_heads).reshape(1, d)
            proj = _mm(hn, wp, out_dtype=F32, name="fox_in")
            c = _fox_gate_fwd(proj.reshape(bsz, s_len, -1), bf, n_heads)
            crow = jnp.swapaxes(c, 1, 2)[:, :n_heads].reshape(bsz, n_heads // 2, 2, s_len)
            qn, kn, vb = _fox_qknorm_fwd(proj, gq, gk, d)
            shp = (bsz, s_len, d)
            o, lse, got = _flash_fwd(qn.reshape(shp), kn.reshape(shp), vb.reshape(shp), crow,
                                     gather=[late_shards[n] for n in late_names])
            for key, layers in _prepare_matrices(dict(zip(late_names, got))).items():
                wt[key] = wt[key] + layers
            o = o.reshape(t, d)
            xn = _mm(o, wt['fox_w_o'][j], residual=xc, out_dtype=F32, name="fox_out")
            sv.update(hn=hn, wp=wp, bf=bf, gq=gq, gk=gk, proj=proj, crow=crow, qn=qn, kn=kn, vb=vb, o=o, lse=lse)
        xc = xn
        sv['x_ffn'] = xc
        hf = _rmsnorm_fwd(xc, row(wt['norm_ffn'][i]))
        shf = (bsz, s_len, f)
        uv = _mm(hf, wt['ffn_w_up_v'][i], name="ffn_up").reshape(shf)
        ug = _mm(hf, wt['ffn_w_up_g'][i], name="ffn_up").reshape(shf)
        dw8 = _pad_rows(wt['ffn_dw'][i], 8)
        af = _ffn_act_fwd(uv, ug, dw8, row(wt['ffn_dw_b'][i])).reshape(t, f)
        xc = _mm(af, wt['ffn_w_down'][i], residual=xc, out_dtype=F32, name="ffn_down")
        sv.update(hf=hf, uv=uv, ug=ug, af=af, dw8=dw8)
        saved.append(sv)

    dx, sq = _loss_head(xc, target.reshape(t, d))

    for i in reversed(range(depth)):
        j = i // 3
        kind = i % 3
        sv = saved[i]
        shf = (bsz, s_len, f)
        da = _mm(dx, wt['ffn_w_down'][i], trans_b=True, name="ffn_down_dgrad")
        gw, _ = _wgrad(sv['af'], dx, name="ffn_down_wgrad")
        grads['ffn_w_down'][i] = gw.reshape(N_DEV, f // N_DEV, d)
        dvv, dvg, ddwv, ddwg, dbv, dbg = _ffn_act_bwd1(sv['uv'], sv['ug'], da.reshape(shf), sv['dw8'],
                                                       row(wt['ffn_dw_b'][i]))
        grads['ffn_dw'][i] = jnp.concatenate([ddwv.sum(axis=1), ddwg.sum(axis=1)], axis=1)
        grads['ffn_dw_b'][i] = jnp.concatenate([_fold(dbv), _fold(dbg)])
        duv, dug = _ffn_act_bwd2(dvv, dvg, sv['dw8'])
        duv, dug = duv.reshape(t, f), dug.reshape(t, f)
        gv, _ = _wgrad(sv['hf'], duv, name="ffn_up_wgrad")
        gg, _ = _wgrad(sv['hf'], dug, name="ffn_up_wgrad")
        grads['ffn_w_up'][i] = _split_lanes([gv, gg], 2 * f // N_DEV)
        dhf = _mm(duv, wt['ffn_w_up_v'][i], trans_b=True, a2=dug, b2=wt['ffn_w_up_g'][i], name="ffn_up_dgrad")
        dx, dg = _rmsnorm_bwd(sv['x_ffn'], row(wt['norm_ffn'][i]), dhf, dx)
        grads['norm_ffn'][i] = _fold(dg)
        gm = row(wt['norm_mix'][i])
        if kind == 0:
            dsact = _mm(dx, wt['conv_w_out'][j], trans_b=True, name="conv_out_dgrad")
            gw, cs = _wgrad(sv['sact'], dx, name="conv_out_wgrad")
            grads['conv_w_out'][j] = gw.reshape(N_DEV, d // N_DEV, d)
            grads['conv_b_out'][j] = _fold(cs)
            du, dlg, dlb, dwb = _conv_act_bwd1(sv['u'], dsact, row(wt['conv_ln_g'][j]), row(wt['conv_ln_b'][j]))
            grads['conv_ln_g'][j], grads['conv_ln_b'][j], grads['conv_dw_b'][j] = _fold(dlg), _fold(dlb), _fold(dwb)
            dp, ddw = _conv_act_bwd2(du.reshape(bsz, s_len, d), sv['p'].reshape(bsz, s_len, 2 * d),
                                     _pad_rows(wt['conv_dw'][j], 32))
            grads['conv_dw'][j] = ddw.sum(axis=1)[:wt['conv_dw'].shape[1]]
            dp = dp.reshape(t, 2 * d)
            gw, cs = _wgrad(sv['hn'], dp, name="conv_in_wgrad")
            grads['conv_w_in'][j] = _split_lanes([gw], 2 * d // N_DEV)
            grads['conv_b_in'][j] = _fold(cs)
            dhn = _mm(dp, wt['conv_w_in'][j], trans_b=True, name="conv_in_dgrad")
            dx, dg = _rmsnorm_bwd(sv['x_mix'], gm, dhn, dx)
            grads['norm_mix'][i] = _fold(dg)
        elif kind == 1:
            shp = (bsz, s_len, d)
            dxn, dwp, dbp, dsc, dg = _pool_bwd(sv['x_mix'].reshape(shp), dx.reshape(shp), sv['p'], gm, wt['pool_w'][j],
                                               row(wt['pool_b'][j]), row(wt['pool_scale'][j]))
            dx = dxn.reshape(t, d)
            ng, cg = dwp.shape[0], dwp.shape[1]
            grads['pool_w'][j] = jnp.moveaxis(dwp.reshape(ng, N_DEV, cg // N_DEV, cg), 1, 0).astype(BF16)
            grads['pool_b'][j] = _fold(dbp).reshape(wt['pool_b'].shape[1:])
            grads['pool_scale'][j] = _fold(dsc)
            grads['norm_mix'][i] = _fold(dg)
        else:
            shp = (bsz, s_len, d)
            do = _mm(dx, wt['fox_w_o'][j], trans_b=True, name="fox_out_dgrad")
            gw, _ = _wgrad(sv['o'], dx, name="fox_out_wgrad")
            grads['fox_w_o'][j] = gw.reshape(N_DEV, d // N_DEV, d)
            fl_args = (sv['qn'].reshape(shp), sv['kn'].reshape(shp), sv['vb'].reshape(shp), do.reshape(shp),
                       sv['o'].reshape(shp), sv['lse'], sv['crow'])
            items, groups = [], []
            for n in late_names:
                members = [grads[n][l] for l in sorted(grads[n]) if l >= cut[n]]
                groups.append(list(range(len(items), len(items) + len(members))))
                items += members
            dq, dk, dv, dcrow, got = _flash_bwd(*fl_args, items=items, groups=groups)
            late_recv = dict(zip(late_names, got))
            dc = jnp.swapaxes(dcrow.reshape(bsz, n_heads, s_len), 1, 2)
            dc = jnp.pad(dc, ((0, 0), (0, 0), (0, LANES - n_heads)))
            dfl, dbf = _fox_gate_bwd(dc, sv['proj'].reshape(bsz, s_len, -1), sv['bf'], n_heads)
            grads['fox_b_f'][j] = _fold(dbf)[:n_heads]
            dqkv, dgq, dgk = _fox_qknorm_bwd(sv['proj'], dq.reshape(t, d), dk.reshape(t, d), dv.reshape(t, d),
                                             sv['gq'], sv['gk'], d)
            grads['fox_q_gain'][j] = _fold(dgq).reshape(n_heads, HEAD_DIM).sum(axis=0)
            grads['fox_k_gain'][j] = _fold(dgk).reshape(n_heads, HEAD_DIM).sum(axis=0)
            dproj = jnp.concatenate([dqkv, dfl.reshape(t, LANES)], axis=1)
            dwp, _ = _wgrad(sv['hn'], dproj, name="fox_in_wgrad")
            grads['fox_w_in'][j] = _split_lanes([dwp], (3 * d + n_heads) // N_DEV)
            dhn = _mm(dproj, sv['wp'], trans_b=True, name="fox_in_dgrad")
            dx, dg = _rmsnorm_bwd(sv['x_mix'], gm, dhn, dx)
            grads['norm_mix'][i] = _fold(dg)

    small = {n: jnp.stack([g[k] for k in sorted(g)]) for n, g in grads.items() if n not in MATRICES}
    big = {n: [grads[n][k] for k in sorted(grads[n]) if n not in late_recv or k < cut[n]] for n in MATRICES}
    return sq.sum(), dx.reshape(bsz, s_len, d), small, big, late_recv


def _train_step(x, target, w, m, v):
    depth = w['norm_mix'].shape[0]
    attn = [i for i in range(depth) if i % 3 == 2]
    cut = {n: w[n].shape[0] for n in MATRICES}
    if len(attn) == 1:
        cut['ffn_w_up'] = cut['ffn_w_down'] = attn[0]
        later_conv = [i // 3 for i in range(attn[0] + 1, depth) if i % 3 == 0]
        if later_conv:
            cut['conv_w_in'] = cut['conv_w_out'] = later_conv[0]
    late_shards = {n: w[n][cut[n]:].astype(BF16) for n in MATRICES if 0 < cut[n] < w[n].shape[0]}
    got = _gather([w[n][:cut[n]].astype(BF16) for n in MATRICES] + [_flat_rows([w[n] for n in SMALL])])
    wt = _prepare_matrices(dict(zip(MATRICES, got[:-1])))
    wt.update(_prepare_vectors(got[-1], w))
    wt.update({n: w[n] for n in REPLICATED})
    sq, grad_x, gsmall, gbig, late_recv = _local_step(x, target, wt, late_shards, cut)
    d = x.shape[-1]

    shard_rows = jnp.concatenate([_to_shards(gsmall[n], SHARD_AXIS[n]) for n in SMALL], axis=1)
    rep = jnp.concatenate([gsmall[n].reshape(-1) for n in REPLICATED] + [(0.5 / d) * sq.reshape(1)])
    rows = jnp.concatenate([shard_rows, jnp.broadcast_to(rep, (N_DEV, rep.shape[0]))], axis=1)
    chunk = 8 * ADAM_COLS
    n_all = rows.shape[1]
    n_pad = -(-n_all // chunk) * chunk
    rows = jnp.pad(rows, ((0, 0), (0, n_pad - n_all))).reshape(N_DEV, n_pad // ADAM_COLS, ADAM_COLS)

    items, groups = [], []
    for n in MATRICES:
        groups.append(list(range(len(items), len(items) + len(gbig[n]))))
        items += gbig[n]
    groups.append([len(items)])
    items.append(rows)
    recv = _scatter(items, groups)

    res = [{}, {}, {}, {}]
    for n, r in zip(MATRICES, recv[:-1]):
        if n in late_recv:
            c = cut[n]
            outs = zip(_adamw(r, w[n][:c], m[n][:c], v[n][:c]), _adamw(late_recv[n], w[n][c:], m[n][c:], v[n][c:]))
            outs = [jnp.concatenate(pair, axis=0) for pair in outs]
        else:
            outs = _adamw(r, w[n], m[n], v[n])
        for k, o in enumerate(outs):
            res[k][n] = o
    order = SMALL + REPLICATED

    def flat(tree):
        parts = jnp.concatenate([tree[n].reshape(-1) for n in order])
        return jnp.pad(parts, (0, n_pad - parts.shape[0])).reshape(n_pad // ADAM_COLS, ADAM_COLS)

    outs = [o.reshape(-1) for o in _adamw(recv[-1].reshape((N_CHIP,) + rows.shape[1:]), flat(w), flat(m), flat(v))]
    off = 0
    for n in order:
        size = w[n].size
        for k in range(4):
            res[k][n] = outs[k][off:off + size].reshape(w[n].shape)
        off += size
    loss = outs[0][n_all - 1]
    return (loss, grad_x, *[res[0][n] for n in WEIGHTS], *[res[1][n] for n in WEIGHTS],
            *[res[2][n] for n in WEIGHTS], *[res[3][n] for n in WEIGHTS])


def kernel(x, norm_mix, norm_ffn, conv_w_in, conv_b_in, conv_dw, conv_dw_b, conv_ln_g, conv_ln_b, conv_w_out, conv_b_out, pool_w, pool_b, pool_scale, fox_w_in, fox_b_f, fox_q_gain, fox_k_gain, fox_w_o, ffn_w_up, ffn_dw, ffn_dw_b, ffn_w_down, loss_target, m_norm_mix, m_norm_ffn, m_conv_w_in, m_conv_b_in, m_conv_dw, m_conv_dw_b, m_conv_ln_g, m_conv_ln_b, m_conv_w_out, m_conv_b_out, m_pool_w, m_pool_b, m_pool_scale, m_fox_w_in, m_fox_b_f, m_fox_q_gain, m_fox_k_gain, m_fox_w_o, m_ffn_w_up, m_ffn_dw, m_ffn_dw_b, m_ffn_w_down, v_norm_mix, v_norm_ffn, v_conv_w_in, v_conv_b_in, v_conv_dw, v_conv_dw_b, v_conv_ln_g, v_conv_ln_b, v_conv_w_out, v_conv_b_out, v_pool_w, v_pool_b, v_pool_scale, v_fox_w_in, v_fox_b_f, v_fox_q_gain, v_fox_k_gain, v_fox_w_o, v_ffn_w_up, v_ffn_dw, v_ffn_dw_b, v_ffn_w_down):
    w = dict(zip(WEIGHTS, (norm_mix, norm_ffn, conv_w_in, conv_b_in, conv_dw, conv_dw_b, conv_ln_g, conv_ln_b, conv_w_out, conv_b_out, pool_w, pool_b, pool_scale, fox_w_in, fox_b_f, fox_q_gain, fox_k_gain, fox_w_o, ffn_w_up, ffn_dw, ffn_dw_b, ffn_w_down)))
    m = dict(zip(WEIGHTS, (m_norm_mix, m_norm_ffn, m_conv_w_in, m_conv_b_in, m_conv_dw, m_conv_dw_b, m_conv_ln_g, m_conv_ln_b, m_conv_w_out, m_conv_b_out, m_pool_w, m_pool_b, m_pool_scale, m_fox_w_in, m_fox_b_f, m_fox_q_gain, m_fox_k_gain, m_fox_w_o, m_ffn_w_up, m_ffn_dw, m_ffn_dw_b, m_ffn_w_down)))
    v = dict(zip(WEIGHTS, (v_norm_mix, v_norm_ffn, v_conv_w_in, v_conv_b_in, v_conv_dw, v_conv_dw_b, v_conv_ln_g, v_conv_ln_b, v_conv_w_out, v_conv_b_out, v_pool_w, v_pool_b, v_pool_scale, v_fox_w_in, v_fox_b_f, v_fox_q_gain, v_fox_k_gain, v_fox_w_o, v_ffn_w_up, v_ffn_dw, v_ffn_dw_b, v_ffn_w_down)))
    return _train_step(x, loss_target, w, m, v)
```

```python
import functools
import math

import jax
import jax.numpy as jnp
from jax import lax
from jax.experimental import pallas as pl
from jax.experimental.pallas import tpu as pltpu

F32, BF16 = jnp.float32, jnp.bfloat16
SDS = jax.ShapeDtypeStruct

N_DEV = 8
EPS = 1e-6
POOL_WINDOWS = (2, 4, 8, 16)
HEAD_DIM = 64
ADAM_LR, ADAM_B1, ADAM_B2, ADAM_EPS, ADAM_WD, ADAM_STEP = 0.001, 0.9, 0.999, 1e-08, 0.01, 10
LANES = 128
VMEM_LIMIT_BYTES = 48 * 1024 * 1024
NEG = -1e30

WEIGHTS = ['norm_mix', 'norm_ffn', 'conv_w_in', 'conv_b_in', 'conv_dw', 'conv_dw_b', 'conv_ln_g', 'conv_ln_b',
           'conv_w_out', 'conv_b_out', 'pool_w', 'pool_b', 'pool_scale', 'fox_w_in', 'fox_b_f', 'fox_q_gain',
           'fox_k_gain', 'fox_w_o', 'ffn_w_up', 'ffn_dw', 'ffn_dw_b', 'ffn_w_down']
SHARD_AXIS = {'conv_w_in': 2, 'conv_b_in': 1, 'conv_dw': 2, 'conv_dw_b': 1, 'conv_ln_g': 1, 'conv_ln_b': 1,
              'conv_w_out': 1, 'conv_b_out': 1, 'pool_w': 2, 'pool_b': 2, 'fox_w_in': 2, 'fox_w_o': 1,
              'ffn_w_up': 2, 'ffn_dw': 2, 'ffn_w_down': 1}
MATRICES = ('conv_w_in', 'conv_w_out', 'pool_w', 'fox_w_in', 'fox_w_o', 'ffn_w_up', 'ffn_w_down')
SHARDED = [n for n in WEIGHTS if n in SHARD_AXIS]
REPLICATED = [n for n in WEIGHTS if n not in SHARD_AXIS]


def _cp(*sem):
    return pltpu.CompilerParams(dimension_semantics=sem, vmem_limit_bytes=VMEM_LIMIT_BYTES)


def _tile(n, pref, align=8):
    if n <= pref:
        return n
    t = (pref // align) * align
    while t >= align:
        if n % t == 0:
            return t
        t -= align
    return n


def _fold8(x):
    r, c = x.shape
    return x.reshape(r // 8, 8, c).sum(axis=0)


def _sigmoid(x):
    return 0.5 * jnp.tanh(0.5 * x) + 0.5


def _shifts_back(cur, tail, n):
    hb = tail.shape[0]
    xe = jnp.concatenate([tail, cur], axis=0)
    return [cur] + [pltpu.roll(xe, j, axis=0)[hb:] for j in range(1, n)]


def _shifts_fwd(cur, head, n):
    ts = cur.shape[0]
    xe = jnp.concatenate([cur, head], axis=0)
    ln = xe.shape[0]
    return [cur] + [pltpu.roll(xe, ln - j, axis=0)[:ts] for j in range(1, n)]


def _dot_hi(a, b):
    return jnp.dot(a, b, preferred_element_type=F32, precision=lax.Precision.HIGHEST)


def _rmsnorm_fwd(x, g):
    t, d = x.shape
    tm = _tile(t, 512)

    def body(x_ref, g_ref, h_ref):
        xv = x_ref[...]
        r = lax.rsqrt(jnp.mean(xv * xv, axis=-1, keepdims=True) + EPS)
        h_ref[...] = (xv * r * g_ref[...]).astype(BF16)

    return pl.pallas_call(
        body, out_shape=SDS((t, d), BF16), grid=(t // tm,), name="rmsnorm_fwd",
        in_specs=[pl.BlockSpec((tm, d), lambda i: (i, 0)), pl.BlockSpec((1, d), lambda i: (0, 0))],
        out_specs=pl.BlockSpec((tm, d), lambda i: (i, 0)), compiler_params=_cp("parallel"))(x, g)


def _mm(a, b, *, trans_b=False, bias=None, residual=None, a2=None, b2=None, norm_gain=None, norm_bwd=None,
        out_dtype=BF16, name="mm"):
    m, k = a.shape
    n = b.shape[0] if trans_b else b.shape[1]
    tm, tn, tk = _tile(m, 1024, 16), _tile(n, 1536, LANES), _tile(k, 1536, LANES)
    if norm_bwd is not None:
        tm = _tile(m, 512, 16)
    nk = k // tk
    two = a2 is not None
    steps = 2 * nk if two else nk
    dims = (((1,), (1,)), ((), ())) if trans_b else (((1,), (0,)), ((), ()))
    has_bias, has_res, has_norm, has_nbwd = bias is not None, residual is not None, norm_gain is not None, \
        norm_bwd is not None
    assert not (has_norm or has_nbwd) or tn == n
    n_out = 2 if (has_norm or has_nbwd) else 1

    def body(*refs):
        pos = 4 if two else 2
        bias_ref = refs[pos] if has_bias else None
        pos += has_bias
        res_ref = refs[pos] if has_res else None
        pos += has_res
        gain_ref = refs[pos] if has_norm else None
        pos += has_norm
        x_ref, g_ref, dres_ref = refs[pos:pos + 3] if has_nbwd else (None, None, None)
        pos += 3 * has_nbwd
        outs = refs[pos:pos + n_out]
        first = (pl.program_id(0) == 0) & (pl.program_id(1) == 0) & (pl.program_id(2) == 0)

        def finish(r):
            if has_bias:
                r = r + bias_ref[...]
            if has_res:
                r = r + res_ref[...]
            if has_nbwd:
                xv = x_ref[...]
                rs = lax.rsqrt(jnp.mean(xv * xv, axis=-1, keepdims=True) + EPS)
                xh = xv * rs
                u = r * g_ref[...]
                outs[0][...] = dres_ref[...] + rs * (u - xh * jnp.mean(u * xh, axis=-1, keepdims=True))
                outs[1][...] += _fold8(r * xh)
                return
            outs[0][...] = r.astype(out_dtype)
            if has_norm:
                outs[1][...] = (r * lax.rsqrt(jnp.mean(r * r, axis=-1, keepdims=True) + EPS)
                                * gain_ref[...]).astype(BF16)

        def dot(a_ref, b_ref):
            return lax.dot_general(a_ref[...].astype(BF16), b_ref[...].astype(BF16), dims, preferred_element_type=F32)

        if has_nbwd:
            @pl.when(first)
            def _():
                outs[1][...] = jnp.zeros_like(outs[1])

        if steps == 1:
            finish(dot(refs[0], refs[1]))
            return
        acc_ref = refs[-1]
        kk = pl.program_id(2)

        @pl.when(kk == 0)
        def _():
            acc_ref[...] = jnp.zeros_like(acc_ref)

        @pl.when(kk < nk)
        def _():
            acc_ref[...] += dot(refs[0], refs[1])

        if two:
            @pl.when(kk >= nk)
            def _():
                acc_ref[...] += dot(refs[2], refs[3])

        @pl.when(kk == steps - 1)
        def _():
            finish(acc_ref[...])

    def pair(first):
        kmap = (lambda kk: jnp.minimum(kk, nk - 1)) if first else (lambda kk: jnp.maximum(kk - nk, 0))
        a_spec = pl.BlockSpec((tm, tk), lambda j, i, kk: (i, kmap(kk)))
        if trans_b:
            b_spec = pl.BlockSpec((tn, tk), lambda j, i, kk: (j, kmap(kk)))
        else:
            b_spec = pl.BlockSpec((tk, tn), lambda j, i, kk: (kmap(kk), j))
        return [a_spec, b_spec]

    in_specs, args = pair(True), [a, b]
    if two:
        in_specs += pair(False)
        args += [a2, b2]
    if has_bias:
        in_specs.append(pl.BlockSpec((1, tn), lambda j, i, kk: (0, j)))
        args.append(bias)
    tile = pl.BlockSpec((tm, tn), lambda j, i, kk: (i, j))
    vec = pl.BlockSpec((1, tn), lambda j, i, kk: (0, j))
    if has_res:
        in_specs.append(tile)
        args.append(residual)
    if has_norm:
        in_specs.append(vec)
        args.append(norm_gain)
    if has_nbwd:
        in_specs += [tile, vec, tile]
        args += list(norm_bwd)
    out_shape, out_specs = [SDS((m, n), F32 if has_nbwd else out_dtype)], [tile]
    if has_norm:
        out_shape.append(SDS((m, n), BF16))
        out_specs.append(tile)
    if has_nbwd:
        out_shape.append(SDS((8, n), F32))
        out_specs.append(pl.BlockSpec((8, tn), lambda j, i, kk: (0, 0)))
    outs = pl.pallas_call(
        body, out_shape=out_shape, grid=(n // tn, m // tm, steps), name=name, in_specs=in_specs, out_specs=out_specs,
        scratch_shapes=[] if steps == 1 else [pltpu.VMEM((tm, tn), F32)],
        compiler_params=_cp("arbitrary", "arbitrary", "arbitrary"))(*args)
    return outs[0] if n_out == 1 else tuple(outs)


def _wgrad(a, g, *, out_dtype=BF16, name="wgrad"):
    m, ka = a.shape
    n = g.shape[1]
    ta, tn, tm = _tile(ka, 1536, LANES), _tile(n, 1536, LANES), _tile(m, 1024)
    nm = m // tm

    def body(a_ref, g_ref, o_ref, cs_ref, acc_ref):
        i, mm = pl.program_id(1), pl.program_id(2)

        @pl.when(mm == 0)
        def _():
            acc_ref[...] = jnp.zeros_like(acc_ref)

        @pl.when((mm == 0) & (i == 0))
        def _():
            cs_ref[...] = jnp.zeros_like(cs_ref)

        gv = g_ref[...]
        acc_ref[...] += lax.dot_general(a_ref[...].astype(BF16), gv.astype(BF16), (((0,), (0,)), ((), ())),
                                        preferred_element_type=F32)

        @pl.when(i == 0)
        def _():
            cs_ref[...] += _fold8(gv.astype(F32))

        @pl.when(mm == nm - 1)
        def _():
            o_ref[...] = acc_ref[...].astype(out_dtype)

    return pl.pallas_call(
        body, out_shape=(SDS((ka, n), out_dtype), SDS((8, n), F32)), grid=(n // tn, ka // ta, nm), name=name,
        in_specs=[pl.BlockSpec((tm, ta), lambda j, i, mm: (mm, i)), pl.BlockSpec((tm, tn), lambda j, i, mm: (mm, j))],
        out_specs=(pl.BlockSpec((ta, tn), lambda j, i, mm: (i, j)), pl.BlockSpec((8, tn), lambda j, i, mm: (0, j))),
        scratch_shapes=[pltpu.VMEM((ta, tn), F32)],
        compiler_params=_cp("arbitrary", "arbitrary", "arbitrary"))(a, g)


FFN_HALO = 16


def _ffn_conv(uc_ref, up_ref, w_ref, b_ref, s):
    u = uc_ref[...].astype(F32)
    tail = jnp.where(s > 0, up_ref[...].astype(F32), 0.0)
    sh = _shifts_back(u, tail, 3)
    return sh, sh[2] * w_ref[0:1, :] + sh[1] * w_ref[1:2, :] + sh[0] * w_ref[2:3, :] + b_ref[...]


def _ffn_act_fwd(uv, ug, dw8, b):
    bsz, s_len, f = uv.shape
    tc, ts = _tile(f, 256, LANES), _tile(s_len, 1024, FFN_HALO)
    nf, r = f // tc, ts // FFN_HALO

    def body(uv_ref, uvp_ref, ug_ref, ugp_ref, wv_ref, wg_ref, bv_ref, bg_ref, a_ref, vv_ref, vg_ref):
        s = pl.program_id(2)
        _, val = _ffn_conv(uv_ref, uvp_ref, wv_ref, bv_ref, s)
        _, gate = _ffn_conv(ug_ref, ugp_ref, wg_ref, bg_ref, s)
        a_ref[...] = (gate * _sigmoid(gate) * val).astype(BF16)
        vv_ref[...] = val.astype(BF16)
        vg_ref[...] = gate.astype(BF16)

    cur = pl.BlockSpec((None, ts, tc), lambda bi, j, s: (bi, s, j))
    prev = pl.BlockSpec((None, FFN_HALO, tc), lambda bi, j, s: (bi, jnp.maximum(s * r - 1, 0), j))

    def par(rows, off):
        return pl.BlockSpec((rows, tc), lambda bi, j, s: (0, j + off))

    return pl.pallas_call(
        body, out_shape=(SDS((bsz, s_len, f), BF16),) * 3, grid=(bsz, nf, s_len // ts), name="ffn_act_fwd",
        in_specs=[cur, prev, cur, prev, par(8, 0), par(8, nf), par(1, 0), par(1, nf)], out_specs=(cur, cur, cur),
        compiler_params=_cp("parallel", "parallel", "arbitrary"))(uv, uv, ug, ug, dw8, dw8, b, b)


def _ffn_act_bwd1(vv, vg, da):
    bsz, s_len, f = vv.shape
    tc, ts = _tile(f, 256, LANES), _tile(s_len, 1024, FFN_HALO)
    nf = f // tc

    def body(vv_ref, vg_ref, da_ref, dvv_ref, dvg_ref, dbv_ref, dbg_ref):
        @pl.when((pl.program_id(1) == 0) & (pl.program_id(2) == 0))
        def _():
            dbv_ref[...] = jnp.zeros_like(dbv_ref)
            dbg_ref[...] = jnp.zeros_like(dbg_ref)

        val, gate = vv_ref[...].astype(F32), vg_ref[...].astype(F32)
        sg = _sigmoid(gate)
        dav = da_ref[...].astype(F32)
        dval = dav * gate * sg
        dgate = dav * val * (sg * (1.0 + gate * (1.0 - sg)))
        dvv_ref[...] = dval.astype(BF16)
        dvg_ref[...] = dgate.astype(BF16)
        dbv_ref[...] += _fold8(dval)
        dbg_ref[...] += _fold8(dgate)

    cur = pl.BlockSpec((None, ts, tc), lambda j, bi, s: (bi, s, j))
    acc1 = pl.BlockSpec((8, tc), lambda j, bi, s: (0, j))
    return pl.pallas_call(
        body, out_shape=(SDS((bsz, s_len, f), BF16), SDS((bsz, s_len, f), BF16), SDS((8, f), F32), SDS((8, f), F32)),
        grid=(nf, bsz, s_len // ts), name="ffn_act_bwd1", in_specs=[cur, cur, cur], out_specs=(cur, cur, acc1, acc1),
        compiler_params=_cp("arbitrary", "arbitrary", "arbitrary"))(vv, vg, da)


def _ffn_act_bwd2(dvv, dvg, uv, ug, dw8):
    bsz, s_len, f = dvv.shape
    tc, ts = _tile(f, 256, LANES), _tile(s_len, 1024, FFN_HALO)
    nf, r, ns = f // tc, ts // FFN_HALO, s_len // ts

    def body(vc_ref, vn_ref, gc_ref, gn_ref, uv_ref, ug_ref, wv_ref, wg_ref, duv_ref, dug_ref, ddwv_ref, ddwg_ref):
        bi, s = pl.program_id(1), pl.program_id(2)

        @pl.when((bi == 0) & (s == 0))
        def _():
            ddwv_ref[...] = jnp.zeros_like(ddwv_ref)
            ddwg_ref[...] = jnp.zeros_like(ddwg_ref)

        for dc_ref, dn_ref, u_ref, w_ref, du_ref, ddw_ref in (
                (vc_ref, vn_ref, uv_ref, wv_ref, duv_ref, ddwv_ref),
                (gc_ref, gn_ref, ug_ref, wg_ref, dug_ref, ddwg_ref)):
            d = dc_ref[...].astype(F32)
            head = jnp.where(s < ns - 1, dn_ref[...].astype(F32), 0.0)
            sh = _shifts_fwd(d, head, 3)
            du_ref[...] = (sh[0] * w_ref[2:3, :] + sh[1] * w_ref[1:2, :] + sh[2] * w_ref[0:1, :]).astype(BF16)
            u = u_ref[...].astype(F32)
            for j in range(3):
                ddw_ref[2 - j] += _fold8(sh[j] * u)

    cur = pl.BlockSpec((None, ts, tc), lambda j, bi, s: (bi, s, j))
    nxt = pl.BlockSpec((None, FFN_HALO, tc),
                       lambda j, bi, s: (bi, jnp.minimum((s + 1) * r, s_len // FFN_HALO - 1), j))

    def par(off):
        return pl.BlockSpec((8, tc), lambda j, bi, s: (0, j + off))

    acc3 = pl.BlockSpec((3, 8, tc), lambda j, bi, s: (0, 0, j))
    return pl.pallas_call(
        body, out_shape=(SDS((bsz, s_len, f), BF16), SDS((bsz, s_len, f), BF16), SDS((3, 8, f), F32),
                         SDS((3, 8, f), F32)),
        grid=(nf, bsz, ns), name="ffn_act_bwd2",
        in_specs=[cur, nxt, cur, nxt, cur, cur, par(0), par(nf)], out_specs=(cur, cur, acc3, acc3),
        compiler_params=_cp("arbitrary", "arbitrary", "arbitrary"))(dvv, dvv, dvg, dvg, uv, ug, dw8, dw8)


CONV_HALO = 32
CONV_CHUNK = 256


def _conv_act_fwd(p, dw32, dwb, ln_g, ln_b):
    bsz, s_len, d2 = p.shape
    d = d2 // 2
    kw = 31
    ts = _tile(s_len, 256, CONV_HALO)
    r = ts // CONV_HALO
    cc = min(CONV_CHUNK, d)

    def body(pc_ref, pp_ref, w_ref, wb_ref, g_ref, b_ref, u_ref, s_ref):
        s = pl.program_id(1)
        tot = jnp.zeros((ts, 1), F32)
        for c0 in range(0, d, cc):
            a = pc_ref[:, c0:c0 + cc].astype(F32)
            g = pc_ref[:, d + c0:d + c0 + cc].astype(F32)
            z = a * _sigmoid(g)
            ap = pp_ref[:, c0:c0 + cc].astype(F32)
            gp = pp_ref[:, d + c0:d + c0 + cc].astype(F32)
            tail = jnp.where(s > 0, ap * _sigmoid(gp), 0.0)
            sh = _shifts_back(z, tail, kw)
            acc = wb_ref[:, c0:c0 + cc] + sh[0] * w_ref[kw - 1:kw, c0:c0 + cc]
            for j in range(1, kw):
                acc = acc + sh[j] * w_ref[kw - 1 - j:kw - j, c0:c0 + cc]
            u_ref[:, c0:c0 + cc] = acc
            tot = tot + jnp.sum(acc, axis=-1, keepdims=True)
        u = u_ref[...]
        mu = tot / d
        uc = u - mu
        var = jnp.mean(uc * uc, axis=-1, keepdims=True)
        ul = uc * lax.rsqrt(var + EPS) * g_ref[...] + b_ref[...]
        s_ref[...] = (ul * _sigmoid(ul)).astype(BF16)

    vec = pl.BlockSpec((1, d), lambda bi, s: (0, 0))
    return pl.pallas_call(
        body, out_shape=(SDS((bsz, s_len, d), F32), SDS((bsz, s_len, d), BF16)), grid=(bsz, s_len // ts),
        name="conv_act_fwd",
        in_specs=[pl.BlockSpec((None, ts, d2), lambda bi, s: (bi, s, 0)),
                  pl.BlockSpec((None, CONV_HALO, d2), lambda bi, s: (bi, jnp.maximum(s * r - 1, 0), 0)),
                  pl.BlockSpec((32, d), lambda bi, s: (0, 0)), vec, vec, vec],
        out_specs=(pl.BlockSpec((None, ts, d), lambda bi, s: (bi, s, 0)),
                   pl.BlockSpec((None, ts, d), lambda bi, s: (bi, s, 0))),
        compiler_params=_cp("parallel", "arbitrary"))(p, p, dw32, dwb, ln_g, ln_b)


def _conv_act_bwd1(u, ds, ln_g, ln_b):
    t, d = u.shape
    ts = _tile(t, 256)

    def body(u_ref, ds_ref, g_ref, b_ref, du_ref, dg_ref, db_ref, dwb_ref):
        @pl.when(pl.program_id(0) == 0)
        def _():
            dg_ref[...] = jnp.zeros_like(dg_ref)
            db_ref[...] = jnp.zeros_like(db_ref)
            dwb_ref[...] = jnp.zeros_like(dwb_ref)

        uv = u_ref[...]
        uc = uv - jnp.mean(uv, axis=-1, keepdims=True)
        rstd = lax.rsqrt(jnp.mean(uc * uc, axis=-1, keepdims=True) + EPS)
        uh = uc * rstd
        ul = uh * g_ref[...] + b_ref[...]
        sg = _sigmoid(ul)
        dul = ds_ref[...].astype(F32) * (sg * (1.0 + ul * (1.0 - sg)))
        duh = dul * g_ref[...]
        du = rstd * (duh - jnp.mean(duh, axis=-1, keepdims=True) - uh * jnp.mean(duh * uh, axis=-1, keepdims=True))
        du_ref[...] = du
        dg_ref[...] += _fold8(dul * uh)
        db_ref[...] += _fold8(dul)
        dwb_ref[...] += _fold8(du)

    row = pl.BlockSpec((ts, d), lambda i: (i, 0))
    vec = pl.BlockSpec((1, d), lambda i: (0, 0))
    acc = pl.BlockSpec((8, d), lambda i: (0, 0))
    return pl.pallas_call(
        body, out_shape=(SDS((t, d), F32), SDS((8, d), F32), SDS((8, d), F32), SDS((8, d), F32)), grid=(t // ts,),
        name="conv_act_bwd1", in_specs=[row, row, vec, vec], out_specs=(row, acc, acc, acc),
        compiler_params=_cp("arbitrary"))(u, ds, ln_g, ln_b)


def _conv_act_bwd2(du, p, dw32):
    bsz, s_len, d2 = p.shape
    d = d2 // 2
    kw = 31
    ts = _tile(s_len, 256, CONV_HALO)
    r, ns = ts // CONV_HALO, s_len // ts
    cc = min(CONV_CHUNK, d)

    def body(dc_ref, dn_ref, pc_ref, pp_ref, w_ref, dp_ref, ddw_ref):
        bi, s = pl.program_id(0), pl.program_id(1)

        @pl.when((bi == 0) & (s == 0))
        def _():
            ddw_ref[...] = jnp.zeros_like(ddw_ref)

        for c0 in range(0, d, cc):
            a = pc_ref[:, c0:c0 + cc].astype(F32)
            g = pc_ref[:, d + c0:d + c0 + cc].astype(F32)
            sg = _sigmoid(g)
            z = a * sg
            ap = pp_ref[:, c0:c0 + cc].astype(F32)
            gp = pp_ref[:, d + c0:d + c0 + cc].astype(F32)
            tail = jnp.where(s > 0, ap * _sigmoid(gp), 0.0)
            duv = dc_ref[:, c0:c0 + cc]
            head = jnp.where(s < ns - 1, dn_ref[:, c0:c0 + cc], 0.0)
            zb = _shifts_back(z, tail, kw)
            for k in range(kw):
                ddw_ref[k, :, c0:c0 + cc] += _fold8(duv * zb[kw - 1 - k])
            df = _shifts_fwd(duv, head, kw)
            dz = df[0] * w_ref[kw - 1:kw, c0:c0 + cc]
            for j in range(1, kw):
                dz = dz + df[j] * w_ref[kw - 1 - j:kw - j, c0:c0 + cc]
            dp_ref[:, c0:c0 + cc] = (dz * sg).astype(BF16)
            dp_ref[:, d + c0:d + c0 + cc] = (dz * a * sg * (1.0 - sg)).astype(BF16)

    return pl.pallas_call(
        body, out_shape=(SDS((bsz, s_len, d2), BF16), SDS((32, 8, d), F32)), grid=(bsz, ns), name="conv_act_bwd2",
        in_specs=[pl.BlockSpec((None, ts, d), lambda bi, s: (bi, s, 0)),
                  pl.BlockSpec((None, CONV_HALO, d),
                               lambda bi, s: (bi, jnp.minimum((s + 1) * r, s_len // CONV_HALO - 1), 0)),
                  pl.BlockSpec((None, ts, d2), lambda bi, s: (bi, s, 0)),
                  pl.BlockSpec((None, CONV_HALO, d2), lambda bi, s: (bi, jnp.maximum(s * r - 1, 0), 0)),
                  pl.BlockSpec((32, d), lambda bi, s: (0, 0))],
        out_specs=(pl.BlockSpec((None, ts, d2), lambda bi, s: (bi, s, 0)),
                   pl.BlockSpec((32, 8, d), lambda bi, s: (0, 0, 0))),
        compiler_params=_cp("arbitrary", "arbitrary"))(du, du, p, p, dw32)


POOL_HALO = 16


def _pool_counts(s, ts, rows, w):
    t = s * ts + lax.broadcasted_iota(jnp.int32, (rows, 1), 0)
    return jnp.minimum(t + 1, w).astype(F32)


def _pool_fwd(x, gmix, w, b, scale):
    bsz, s_len, d = x.shape
    ng = len(POOL_WINDOWS)
    cg = d // ng
    ts = _tile(s_len, 512, POOL_HALO)
    r = ts // POOL_HALO

    def body(xc_ref, xp_ref, g_ref, w_ref, b_ref, sc_ref, y_ref, p_ref):
        s = pl.program_id(1)

        def norm(v):
            return v * lax.rsqrt(jnp.mean(v * v, axis=-1, keepdims=True) + EPS) * g_ref[...]

        xc = xc_ref[...]
        h = norm(xc)
        tail = jnp.where(s > 0, norm(xp_ref[...]), 0.0)
        for gi, win in enumerate(POOL_WINDOWS):
            lo, hi = gi * cg, (gi + 1) * cg
            hg = h[:, lo:hi]
            acc = jnp.concatenate([tail[:, lo:hi], hg], axis=0)
            step = 1
            while step < win:
                acc = acc + pltpu.roll(acc, step, axis=0)
                step *= 2
            pg = acc[POOL_HALO:] / _pool_counts(s, ts, ts, win) - hg
            pb = pg.astype(BF16)
            p_ref[:, lo:hi] = pb
            yg = jnp.dot(pb, w_ref[gi], preferred_element_type=F32) + b_ref[:, lo:hi]
            y_ref[:, lo:hi] = xc[:, lo:hi] + yg * sc_ref[:, lo:hi]

    vec = pl.BlockSpec((1, d), lambda bi, s: (0, 0))
    blk = pl.BlockSpec((None, ts, d), lambda bi, s: (bi, s, 0))
    return pl.pallas_call(
        body, out_shape=(SDS((bsz, s_len, d), F32), SDS((bsz, s_len, d), BF16)), grid=(bsz, s_len // ts),
        name="pool_fwd",
        in_specs=[blk, pl.BlockSpec((None, POOL_HALO, d), lambda bi, s: (bi, jnp.maximum(s * r - 1, 0), 0)),
                  vec, pl.BlockSpec((ng, cg, cg), lambda bi, s: (0, 0, 0)), vec, vec],
        out_specs=(blk, blk), compiler_params=_cp("parallel", "arbitrary"))(x, x, gmix, w, b, scale)


def _pool_bwd(x, dy, p, gmix, w, b, scale):
    bsz, s_len, d = x.shape
    ng = len(POOL_WINDOWS)
    cg = d // ng
    ts = _tile(s_len, 512, POOL_HALO)
    r, ns = ts // POOL_HALO, s_len // ts
    nt = (((1,), (1,)), ((), ()))
    tn = (((0,), (0,)), ((), ()))

    def body(x_ref, dy_ref, dyn_ref, p_ref, g_ref, w_ref, b_ref, sc_ref, dx_ref, dw_ref, db_ref, dsc_ref, dg_ref):
        bi, s = pl.program_id(0), pl.program_id(1)

        @pl.when((bi == 0) & (s == 0))
        def _():
            dw_ref[...] = jnp.zeros_like(dw_ref)
            db_ref[...] = jnp.zeros_like(db_ref)
            dsc_ref[...] = jnp.zeros_like(dsc_ref)
            dg_ref[...] = jnp.zeros_like(dg_ref)

        dy = dy_ref[...]
        dyy = dy * sc_ref[...]
        dyy_n = jnp.where(s < ns - 1, dyn_ref[...] * sc_ref[...], 0.0)
        db_ref[...] += _fold8(dyy)
        xv = x_ref[...]
        rr = lax.rsqrt(jnp.mean(xv * xv, axis=-1, keepdims=True) + EPS)
        xh = xv * rr
        for gi, win in enumerate(POOL_WINDOWS):
            lo, hi = gi * cg, (gi + 1) * cg
            pb = p_ref[:, lo:hi]
            wg = w_ref[gi]
            pre = jnp.dot(pb, wg, preferred_element_type=F32) + b_ref[:, lo:hi]
            dsc_ref[:, lo:hi] += _fold8(dy[:, lo:hi] * pre)
            dyb = dyy[:, lo:hi].astype(BF16)
            dw_ref[gi] += lax.dot_general(pb, dyb, tn, preferred_element_type=F32)
            dp = lax.dot_general(dyb, wg, nt, preferred_element_type=F32)
            dp_n = lax.dot_general(dyy_n[:, lo:hi].astype(BF16), wg, nt, preferred_element_type=F32)
            q = dp / _pool_counts(s, ts, ts, win)
            q_n = dp_n / _pool_counts(s + 1, ts, POOL_HALO, win)
            acc = jnp.concatenate([q, q_n], axis=0)
            ln = ts + POOL_HALO
            step = 1
            while step < win:
                acc = acc + pltpu.roll(acc, ln - step, axis=0)
                step *= 2
            dh = acc[:ts] - dp
            xhg = xh[:, lo:hi]
            dg_ref[:, lo:hi] += _fold8(dh * xhg)
            dx_ref[:, lo:hi] = dh * g_ref[:, lo:hi]
        u = dx_ref[...]
        dx_ref[...] = dy + rr * (u - xh * jnp.mean(u * xh, axis=-1, keepdims=True))

    vec = pl.BlockSpec((1, d), lambda bi, s: (0, 0))
    acc8 = pl.BlockSpec((8, d), lambda bi, s: (0, 0))
    blk = pl.BlockSpec((None, ts, d), lambda bi, s: (bi, s, 0))
    wspec = pl.BlockSpec((ng, cg, cg), lambda bi, s: (0, 0, 0))
    return pl.pallas_call(
        body, out_shape=(SDS((bsz, s_len, d), F32), SDS((ng, cg, cg), F32), SDS((8, d), F32), SDS((8, d), F32),
                         SDS((8, d), F32)),
        grid=(bsz, ns), name="pool_bwd",
        in_specs=[blk, blk,
                  pl.BlockSpec((None, POOL_HALO, d),
                               lambda bi, s: (bi, jnp.minimum((s + 1) * r, s_len // POOL_HALO - 1), 0)),
                  blk, vec, wspec, vec, vec],
        out_specs=(blk, wspec, acc8, acc8, acc8),
        compiler_params=_cp("arbitrary", "arbitrary"))(x, dy, dy, p, gmix, w, b, scale)


def _tri(n, upper):
    row = lax.broadcasted_iota(jnp.int32, (n, n), 0)
    col = lax.broadcasted_iota(jnp.int32, (n, n), 1)
    return jnp.where((col >= row) if upper else (col <= row), 1.0, 0.0).astype(F32)


def _fox_gate_fwd(proj, bf, n_heads):
    bsz, s_len, width = proj.shape
    col = width // LANES - 1
    ts = _tile(s_len, 512)

    def body(fl_ref, b_ref, c_ref, carry_ref):
        @pl.when(pl.program_id(1) == 0)
        def _():
            carry_ref[...] = jnp.zeros_like(carry_ref)

        xv = fl_ref[...] + b_ref[...]
        logf = jnp.minimum(xv, 0.0) - jnp.log(1.0 + jnp.exp(-jnp.abs(xv)))
        lane = lax.broadcasted_iota(jnp.int32, (1, LANES), 1)
        logf = jnp.where(lane < n_heads, logf, 0.0)
        c = _dot_hi(_tri(ts, False), logf) + carry_ref[0:1, :]
        c_ref[...] = c
        carry_ref[0:1, :] = c[ts - 1:ts, :]

    return pl.pallas_call(
        body, out_shape=SDS((bsz, s_len, LANES), F32), grid=(bsz, s_len // ts), name="fox_gate_fwd",
        in_specs=[pl.BlockSpec((None, ts, LANES), lambda bi, s: (bi, s, col)),
                  pl.BlockSpec((1, LANES), lambda bi, s: (0, 0))],
        out_specs=pl.BlockSpec((None, ts, LANES), lambda bi, s: (bi, s, 0)),
        scratch_shapes=[pltpu.VMEM((8, LANES), F32)],
        compiler_params=_cp("arbitrary", "arbitrary"))(proj, bf)


def _fox_gate_bwd(dc, proj, bf, n_heads):
    bsz, s_len, width = proj.shape
    col = width // LANES - 1
    ts = _tile(s_len, 512)
    ns = s_len // ts

    def body(dc_ref, fl_ref, b_ref, dfl_ref, db_ref, carry_ref):
        bi, s = pl.program_id(0), pl.program_id(1)

        @pl.when((bi == 0) & (s == 0))
        def _():
            db_ref[...] = jnp.zeros_like(db_ref)

        @pl.when(s == 0)
        def _():
            carry_ref[...] = jnp.zeros_like(carry_ref)

        dlogf = _dot_hi(_tri(ts, True), dc_ref[...]) + carry_ref[0:1, :]
        carry_ref[0:1, :] = dlogf[0:1, :]
        lane = lax.broadcasted_iota(jnp.int32, (1, LANES), 1)
        dfl = jnp.where(lane < n_heads, dlogf * (1.0 - _sigmoid(fl_ref[...] + b_ref[...])), 0.0)
        dfl_ref[...] = dfl.astype(BF16)
        db_ref[...] += _fold8(dfl)

    return pl.pallas_call(
        body, out_shape=(SDS((bsz, s_len, LANES), BF16), SDS((8, LANES), F32)), grid=(bsz, ns), name="fox_gate_bwd",
        in_specs=[pl.BlockSpec((None, ts, LANES), lambda bi, s: (bi, ns - 1 - s, 0)),
                  pl.BlockSpec((None, ts, LANES), lambda bi, s: (bi, ns - 1 - s, col)),
                  pl.BlockSpec((1, LANES), lambda bi, s: (0, 0))],
        out_specs=(pl.BlockSpec((None, ts, LANES), lambda bi, s: (bi, ns - 1 - s, 0)),
                   pl.BlockSpec((8, LANES), lambda bi, s: (0, 0))),
        scratch_shapes=[pltpu.VMEM((8, LANES), F32)],
        compiler_params=_cp("arbitrary", "arbitrary"))(dc, proj, bf)


def _head_maps(d):
    ch = lax.broadcasted_iota(jnp.int32, (d, LANES), 0) // HEAD_DIM
    hd = lax.broadcasted_iota(jnp.int32, (d, LANES), 1)
    e = jnp.where(ch == hd, 1.0, 0.0).astype(BF16)
    cht = lax.broadcasted_iota(jnp.int32, (LANES, d), 1) // HEAD_DIM
    hdt = lax.broadcasted_iota(jnp.int32, (LANES, d), 0)
    et = jnp.where(cht == hdt, 1.0, 0.0).astype(BF16)
    return e, et


def _dot_sel(x, e):
    a = x.astype(BF16)
    r = x - a.astype(F32)
    b = r.astype(BF16)
    c = (r - b.astype(F32)).astype(BF16)
    return (jnp.dot(a, e, preferred_element_type=F32) + jnp.dot(b, e, preferred_element_type=F32)
            + jnp.dot(c, e, preferred_element_type=F32))


def _fox_qknorm_fwd(proj, gq, gk, d):
    t = proj.shape[0]
    ts = _tile(t, 256)
    scale = 1.0 / math.sqrt(HEAD_DIM)

    def body(q_ref, k_ref, v_ref, gq_ref, gk_ref, qn_ref, kn_ref, vb_ref):
        e, et = _head_maps(d)

        def norm(v, g):
            r = lax.rsqrt(_dot_sel(v * v, e) / HEAD_DIM + EPS)
            return v * _dot_sel(r, et) * g

        qn_ref[...] = (norm(q_ref[...], gq_ref[...]) * scale).astype(BF16)
        kn_ref[...] = norm(k_ref[...], gk_ref[...]).astype(BF16)
        vb_ref[...] = v_ref[...].astype(BF16)

    def colblk(j):
        return pl.BlockSpec((ts, d), lambda i: (i, j))

    vec = pl.BlockSpec((1, d), lambda i: (0, 0))
    out = pl.BlockSpec((ts, d), lambda i: (i, 0))
    return pl.pallas_call(
        body, out_shape=(SDS((t, d), BF16),) * 3, grid=(t // ts,), name="fox_qknorm_fwd",
        in_specs=[colblk(0), colblk(1), colblk(2), vec, vec], out_specs=(out, out, out),
        compiler_params=_cp("parallel"))(proj, proj, proj, gq, gk)


def _fox_qknorm_bwd(proj, dq, dk, dv, gq, gk, d):
    t = proj.shape[0]
    ts = _tile(t, 256)
    scale = 1.0 / math.sqrt(HEAD_DIM)

    def body(q_ref, k_ref, dq_ref, dk_ref, dv_ref, gq_ref, gk_ref, dp_ref, dgq_ref, dgk_ref):
        @pl.when(pl.program_id(0) == 0)
        def _():
            dgq_ref[...] = jnp.zeros_like(dgq_ref)
            dgk_ref[...] = jnp.zeros_like(dgk_ref)

        e, et = _head_maps(d)

        def back(v, g, dn, dg_ref):
            r = _dot_sel(lax.rsqrt(_dot_sel(v * v, e) / HEAD_DIM + EPS), et)
            vh = v * r
            dg_ref[...] += _fold8(dn * vh)
            u = dn * g
            mh = _dot_sel(_dot_sel(u * vh, e) / HEAD_DIM, et)
            return r * (u - vh * mh)

        dp_ref[:, 0:d] = back(q_ref[...], gq_ref[...], dq_ref[...] * scale, dgq_ref).astype(BF16)
        dp_ref[:, d:2 * d] = back(k_ref[...], gk_ref[...], dk_ref[...], dgk_ref).astype(BF16)
        dp_ref[:, 2 * d:3 * d] = dv_ref[...]

    def colblk(j):
        return pl.BlockSpec((ts, d), lambda i: (i, j))

    row = pl.BlockSpec((ts, d), lambda i: (i, 0))
    vec = pl.BlockSpec((1, d), lambda i: (0, 0))
    acc = pl.BlockSpec((8, d), lambda i: (0, 0))
    return pl.pallas_call(
        body, out_shape=(SDS((t, 3 * d), BF16), SDS((8, d), F32), SDS((8, d), F32)), grid=(t // ts,),
        name="fox_qknorm_bwd", in_specs=[colblk(0), colblk(1), row, row, row, vec, vec],
        out_specs=(pl.BlockSpec((ts, 3 * d), lambda i: (i, 0)), acc, acc),
        compiler_params=_cp("arbitrary"))(proj, proj, dq, dk, dv, gq, gk)


ATT_BLOCK = 512
_NT = (((1,), (1,)), ((), ()))
_TN = (((0,), (0,)), ((), ()))


def _head_mask(h):
    return (lax.broadcasted_iota(jnp.int32, (1, LANES), 1) // HEAD_DIM) == h


def _causal(qi, ki, tq, tk):
    row = qi * tq + lax.broadcasted_iota(jnp.int32, (tq, 1), 0)
    col = ki * tk + lax.broadcasted_iota(jnp.int32, (1, tk), 1)
    return col <= row


def _direct_exchange(ins, outs, place, sems, gather):
    send_sems, recv_sems, local_sems = sems
    x, y, c, me = _mesh_place()
    copies = []
    for t in range(len(ins)):
        dst = outs[t].at[me] if gather else outs[place[t][0]].at[me, place[t][1]]
        copies.append(pltpu.make_async_copy(ins[t] if gather else ins[t].at[me], dst, local_sems.at[t]))
        for kbits in range(1, N_DEV):
            px = 1 - x if kbits & 4 else x
            py = 1 - y if kbits & 2 else y
            pc = 1 - c if kbits & 1 else c
            copies.append(pltpu.make_async_remote_copy(
                src_ref=ins[t] if gather else ins[t].at[4 * px + 2 * py + pc], dst_ref=dst,
                send_sem=send_sems.at[t, kbits - 1], recv_sem=recv_sems.at[t, kbits - 1],
                device_id=(px, py, pc), device_id_type=pl.DeviceIdType.MESH))
    return copies


def _exchange_scratch(n):
    return [pltpu.SemaphoreType.DMA((n, N_DEV - 1)), pltpu.SemaphoreType.DMA((n, N_DEV - 1)),
            pltpu.SemaphoreType.DMA((n,))]


def _flash_fwd(q, k, v, crow, gather=()):
    bsz, s_len, d = q.shape
    nj = d // LANES
    tq = tk = _tile(s_len, ATT_BLOCK, LANES)
    nq = s_len // tq
    ng = len(gather)

    pairs = [(a, b) for a in range(nq) for b in range(a + 1)]
    qtab = jnp.asarray([a for a, _ in pairs], jnp.int32)
    ktab = jnp.asarray([b for _, b in pairs], jnp.int32)

    def body(qtab_ref, ktab_ref, q_ref, k_ref, v_ref, c_ref, *rest):
        g_in, (o_ref, lse_ref), g_out = rest[:ng], rest[ng:ng + 2], rest[ng + 2:2 * ng + 2]
        m_ref, l_ref, acc_ref = rest[2 * ng + 2:2 * ng + 5]
        step_id = pl.program_id(2)
        qi, ki = qtab_ref[step_id], ktab_ref[step_id]
        if ng:
            sems = rest[2 * ng + 5:]
            outer = (pl.program_id(0), pl.program_id(1))

            @pl.when((outer[0] == 0) & (outer[1] == 0) & (step_id == 0))
            def _():
                for cp in _direct_exchange(g_in, g_out, None, sems, True):
                    cp.start()

            @pl.when((outer[0] == bsz - 1) & (outer[1] == nj - 1) & (step_id == len(pairs) - 1))
            def _():
                for cp in _direct_exchange(g_in, g_out, None, sems, True):
                    cp.wait()

        @pl.when(ki == 0)
        def _():
            m_ref[...] = jnp.full_like(m_ref, NEG)
            l_ref[...] = jnp.zeros_like(l_ref)
            acc_ref[...] = jnp.zeros_like(acc_ref)

        def step(masked):
            qv, kv, vv = q_ref[...], k_ref[...], v_ref[...]
            for h in range(2):
                qh = jnp.where(_head_mask(h), qv, jnp.zeros_like(qv))
                s = lax.dot_general(qh, kv, _NT, preferred_element_type=F32) - c_ref[h:h + 1, :]
                if masked:
                    s = jnp.where(_causal(qi, ki, tq, tk), s, NEG)
                m_prev = m_ref[h]
                m_new = jnp.maximum(m_prev, jnp.max(s, axis=1, keepdims=True))
                pm = jnp.exp(s - m_new)
                alpha = jnp.exp(m_prev - m_new)
                l_ref[h] = alpha * l_ref[h] + jnp.sum(pm, axis=1, keepdims=True)
                p_hi = pm.astype(BF16)
                p_lo = (pm - p_hi.astype(F32)).astype(BF16)
                acc_ref[h] = (alpha * acc_ref[h] + jnp.dot(p_hi, vv, preferred_element_type=F32)
                              + jnp.dot(p_lo, vv, preferred_element_type=F32))
                m_ref[h] = m_new

        pl.when(ki < qi)(functools.partial(step, False))
        pl.when(ki == qi)(functools.partial(step, True))

        @pl.when(ki == qi)
        def _():
            m0 = _head_mask(0)
            o_ref[...] = jnp.where(m0, acc_ref[0] / l_ref[0], acc_ref[1] / l_ref[1])
            lse_ref[...] = jnp.where(m0, m_ref[0] + jnp.log(l_ref[0]), m_ref[1] + jnp.log(l_ref[1]))

    qblk = pl.BlockSpec((None, tq, LANES), lambda bi, j, t, qt, kt: (bi, qt[t], j))
    kblk = pl.BlockSpec((None, tk, LANES), lambda bi, j, t, qt, kt: (bi, kt[t], j))
    hbm = pl.BlockSpec(memory_space=pl.ANY)
    outs = pl.pallas_call(
        body, out_shape=[SDS((bsz, s_len, d), F32), SDS((bsz, nj, s_len, LANES), F32)]
        + [SDS((N_DEV,) + tuple(a.shape), a.dtype) for a in gather], name="flash_fwd",
        grid_spec=pltpu.PrefetchScalarGridSpec(
            num_scalar_prefetch=2, grid=(bsz, nj, len(pairs)),
            in_specs=[qblk, kblk, kblk,
                      pl.BlockSpec((None, None, 2, tk), lambda bi, j, t, qt, kt: (bi, j, 0, kt[t]))] + [hbm] * ng,
            out_specs=[qblk, pl.BlockSpec((None, None, tq, LANES), lambda bi, j, t, qt, kt: (bi, j, qt[t], 0))]
            + [hbm] * ng,
            scratch_shapes=[pltpu.VMEM((2, tq, 1), F32), pltpu.VMEM((2, tq, 1), F32),
                            pltpu.VMEM((2, tq, LANES), F32)] + (_exchange_scratch(ng) if ng else [])),
        compiler_params=_cp("arbitrary", "arbitrary", "arbitrary"))(qtab, ktab, q, k, v, crow, *gather)
    return outs[0], outs[1], list(outs[2:])


def _flash_probs(qv, kv, vv, dov, ov, lse, c_ref, h, mask):
    hm = _head_mask(h)
    qh = jnp.where(hm, qv, jnp.zeros_like(qv))
    s = lax.dot_general(qh, kv, _NT, preferred_element_type=F32) - c_ref[h:h + 1, :]
    pm = jnp.exp(s - lse[:, h * HEAD_DIM:h * HEAD_DIM + 1])
    if mask is not None:
        pm = jnp.where(mask, pm, 0.0)
    doh = jnp.where(hm, dov, jnp.zeros_like(dov))
    dpm = lax.dot_general(doh, vv, _NT, preferred_element_type=F32)
    delta = jnp.sum(jnp.where(hm, dov.astype(F32) * ov, 0.0), axis=1, keepdims=True)
    return pm, pm * (dpm - delta)


def _flash_bwd(q, k, v, do, o, lse, crow, items=(), groups=()):
    bsz, s_len, d = q.shape
    nj = d // LANES
    tq = tk = _tile(s_len, ATT_BLOCK, LANES)
    nq = s_len // tq

    pairs = [(b, a) for b in range(nq) for a in range(b, nq)]
    n_live = len(pairs)
    ktab = jnp.asarray([b for b, _ in pairs] + [nq - 1] * nq, jnp.int32)
    qtab = jnp.asarray([a for _, a in pairs] + list(range(nq)), jnp.int32)

    n_it, n_grp = len(items), len(groups)
    place = {it: (g, l) for g, members in enumerate(groups) for l, it in enumerate(members)}

    def body(ktab_ref, qtab_ref, q_ref, k_ref, v_ref, do_ref, o_ref, lse_ref, c_ref, *rest):
        x_in, (dq_ref, dk_ref, dv_ref, dc_ref) = rest[:n_it], rest[n_it:n_it + 4]
        x_out = rest[n_it + 4:n_it + 4 + n_grp]
        dqa_ref, dka_ref, dva_ref, dca_ref = rest[n_it + 4 + n_grp:n_it + 8 + n_grp]
        step_id = pl.program_id(2)
        ki, qi = ktab_ref[step_id], qtab_ref[step_id]
        live = step_id < n_live
        rows = pl.ds(pl.multiple_of(qi * tq, tq), tq)
        if n_it:
            sems = rest[n_it + 8 + n_grp:]
            outer = (pl.program_id(0), pl.program_id(1))

            @pl.when((outer[0] == 0) & (outer[1] == 0) & (step_id == 0))
            def _():
                for cp in _direct_exchange(x_in, x_out, place, sems, False):
                    cp.start()

            @pl.when((outer[0] == bsz - 1) & (outer[1] == nj - 1) & (step_id == n_live + nq - 1))
            def _():
                for cp in _direct_exchange(x_in, x_out, place, sems, False):
                    cp.wait()

        @pl.when(step_id == 0)
        def _():
            dqa_ref[...] = jnp.zeros_like(dqa_ref)

        @pl.when(live & (qi == ki))
        def _():
            dka_ref[...] = jnp.zeros_like(dka_ref)
            dva_ref[...] = jnp.zeros_like(dva_ref)
            dca_ref[...] = jnp.zeros_like(dca_ref)

        def step(masked):
            qv, kv, vv, dov, ov, lse = q_ref[...], k_ref[...], v_ref[...], do_ref[...], o_ref[...], lse_ref[...]
            mask = _causal(qi, ki, tq, tk) if masked else None
            for h in range(2):
                pm, ds = _flash_probs(qv, kv, vv, dov, ov, lse, c_ref, h, mask)
                dsb = ds.astype(BF16)
                dva_ref[h] += lax.dot_general(pm.astype(BF16), dov, _TN, preferred_element_type=F32)
                dka_ref[h] += lax.dot_general(dsb, qv, _TN, preferred_element_type=F32)
                dqa_ref[h, rows, :] += jnp.dot(dsb, kv, preferred_element_type=F32)
                dca_ref[h:h + 1, :] -= jnp.sum(ds, axis=0, keepdims=True)

        pl.when(live & (qi > ki))(functools.partial(step, False))
        pl.when(live & (qi == ki))(functools.partial(step, True))

        @pl.when(live & (qi == nq - 1))
        def _():
            m0 = _head_mask(0)
            dk_ref[...] = jnp.where(m0, dka_ref[0], dka_ref[1])
            dv_ref[...] = jnp.where(m0, dva_ref[0], dva_ref[1]).astype(BF16)
            dc_ref[...] = dca_ref[0:2, :]

        @pl.when(jnp.logical_not(live))
        def _():
            dq_ref[...] = jnp.where(_head_mask(0), dqa_ref[0, rows, :], dqa_ref[1, rows, :])

    def qside(bi, j, t, kt, qt):
        return (bi, jnp.where(t < n_live, qt[t], nq - 1), j)

    def kside(bi, j, t, kt, qt):
        return (bi, kt[t], j)

    def dqside(bi, j, t, kt, qt):
        return (bi, jnp.where(t < n_live, 0, qt[t]), j)

    qblk, kblk = pl.BlockSpec((None, tq, LANES), qside), pl.BlockSpec((None, tk, LANES), kside)
    cblk = pl.BlockSpec((None, None, 2, tk), lambda bi, j, t, kt, qt: (bi, j, 0, kt[t]))
    hbm = pl.BlockSpec(memory_space=pl.ANY)
    outs = pl.pallas_call(
        body, out_shape=[SDS((bsz, s_len, d), F32), SDS((bsz, s_len, d), F32), SDS((bsz, s_len, d), BF16),
                         SDS((bsz, nj, 2, s_len), F32)]
        + [SDS((N_DEV, len(members)) + tuple(items[members[0]].shape[1:]), items[members[0]].dtype)
           for members in groups], name="flash_bwd",
        grid_spec=pltpu.PrefetchScalarGridSpec(
            num_scalar_prefetch=2, grid=(bsz, nj, n_live + nq),
            in_specs=[qblk, kblk, kblk, qblk, qblk,
                      pl.BlockSpec((None, None, tq, LANES),
                                   lambda bi, j, t, kt, qt: (bi, j, jnp.where(t < n_live, qt[t], nq - 1), 0)),
                      cblk] + [hbm] * n_it,
            out_specs=[pl.BlockSpec((None, tq, LANES), dqside), kblk, kblk, cblk] + [hbm] * n_grp,
            scratch_shapes=[pltpu.VMEM((2, s_len, LANES), F32), pltpu.VMEM((2, tk, LANES), F32),
                            pltpu.VMEM((2, tk, LANES), F32), pltpu.VMEM((8, tk), F32)]
            + (_exchange_scratch(n_it) if n_it else [])),
        compiler_params=_cp("arbitrary", "arbitrary", "arbitrary"))(ktab, qtab, q, k, v, do, o, lse, crow, *items)
    return outs[0], outs[1], outs[2], outs[3], list(outs[4:])


def _loss_head(y, target):
    t, d = y.shape
    tm = _tile(t, 512)

    def body(y_ref, t_ref, dy_ref, acc_ref):
        @pl.when(pl.program_id(0) == 0)
        def _():
            acc_ref[...] = jnp.zeros_like(acc_ref)

        err = y_ref[...] - t_ref[...]
        dy_ref[...] = err / d
        acc_ref[...] += _fold8(err * err)

    row = pl.BlockSpec((tm, d), lambda i: (i, 0))
    return pl.pallas_call(
        body, out_shape=(SDS((t, d), F32), SDS((8, d), F32)), grid=(t // tm,), name="loss_head",
        in_specs=[row, row], out_specs=(row, pl.BlockSpec((8, d), lambda i: (0, 0))),
        compiler_params=_cp("arbitrary"))(y, target)


ADAM_COLS = 1024


def _adamw(g8, w, m, v):
    shape = w.shape
    cols = shape[-1]
    rows = w.size // cols
    n_parts = g8.shape[0]
    g8, w, m, v = g8.reshape(n_parts, rows, cols), w.reshape(rows, cols), m.reshape(rows, cols), v.reshape(rows, cols)
    tr = _tile(rows, 256, 16)
    c1 = 1.0 - ADAM_B1 ** ADAM_STEP
    c2 = 1.0 - ADAM_B2 ** ADAM_STEP

    def body(g8_ref, w_ref, m_ref, v_ref, g_ref, d_ref, nm_ref, nv_ref):
        g = g8_ref[0].astype(F32)
        for i in range(1, n_parts):
            g = g + g8_ref[i].astype(F32)
        mn = ADAM_B1 * m_ref[...] + (1.0 - ADAM_B1) * g
        vn = ADAM_B2 * v_ref[...] + (1.0 - ADAM_B2) * (g * g)
        g_ref[...] = g
        nm_ref[...] = mn
        nv_ref[...] = vn
        d_ref[...] = -ADAM_LR * ((mn / c1) / (jnp.sqrt(vn / c2) + ADAM_EPS) + ADAM_WD * w_ref[...])

    blk = pl.BlockSpec((tr, cols), lambda i: (i, 0))
    outs = pl.pallas_call(
        body, out_shape=(SDS((rows, cols), F32),) * 4, grid=(rows // tr,), name="adamw",
        in_specs=[pl.BlockSpec((n_parts, tr, cols), lambda i: (0, i, 0)), blk, blk, blk], out_specs=(blk,) * 4,
        compiler_params=_cp("parallel"))(g8, w, m, v)
    return [o.reshape(shape) for o in outs]


def _mesh_place():
    x, y, c = lax.axis_index("x"), lax.axis_index("y"), lax.axis_index("c")
    return x, y, c, 4 * x + 2 * y + c


def _gather(shards):
    n = len(shards)

    def body(*refs):
        ins, outs = refs[:n], refs[n:2 * n]
        send_sems, recv_sems, local_sems = refs[2 * n:]
        x, y, c, me = _mesh_place()
        sibling = (x, y, 1 - c)
        chips = [(1 - x, y), (x, 1 - y), (1 - x, 1 - y)]

        def block(px, py, pc):
            return 4 * px + 2 * py + pc

        def copy(t, k, blk, to, src=None):
            return pltpu.make_async_remote_copy(
                src_ref=outs[t].at[blk] if src is None else src, dst_ref=outs[t].at[blk],
                send_sem=send_sems.at[t, k], recv_sem=recv_sems.at[t, k], device_id=to,
                device_id_type=pl.DeviceIdType.MESH)

        own = [pltpu.make_async_copy(ins[t], outs[t].at[me], local_sems.at[t]) for t in range(n)]
        first = []
        for t in range(n):
            own[t].start()
            first.append(copy(t, 0, me, sibling, src=ins[t]))
            first += [copy(t, 1 + j, me, (*chip, c), src=ins[t]) for j, chip in enumerate(chips)]
        for cp in first:
            cp.start()
        passed = []
        for j, chip in enumerate(chips):
            for t in range(n):
                copy(t, 1 + j, block(*chip, c), (x, y, c)).wait_recv()
                cp = copy(t, 4 + j, block(*chip, c), sibling)
                cp.start()
                passed.append(cp)
        for t in range(n):
            copy(t, 0, block(x, y, 1 - c), (x, y, c)).wait_recv()
            for j, chip in enumerate(chips):
                copy(t, 4 + j, block(*chip, 1 - c), (x, y, c)).wait_recv()
        for cp in first + passed:
            cp.wait_send()
        for cp in own:
            cp.wait()

    hbm = pl.BlockSpec(memory_space=pl.ANY)
    return pl.pallas_call(
        body, out_shape=[SDS((N_DEV,) + tuple(s.shape), s.dtype) for s in shards], name="gather",
        in_specs=[hbm] * n, out_specs=[hbm] * n,
        scratch_shapes=[pltpu.SemaphoreType.DMA((n, N_DEV - 1)), pltpu.SemaphoreType.DMA((n, N_DEV - 1)),
                        pltpu.SemaphoreType.DMA((n,))])(*shards)


N_CHIP = N_DEV // 2


def _scatter_core(items):
    n = len(items)

    def body(*refs):
        ins, outs = refs[:n], refs[n:2 * n]
        send_sems, recv_sems = refs[2 * n:]
        x, y, c, _ = _mesh_place()
        copies = []
        for it in range(n):
            for r in range(N_CHIP):
                cp = pltpu.make_async_remote_copy(
                    src_ref=ins[it].at[2 * r + 1 - c], dst_ref=outs[it].at[r], send_sem=send_sems.at[it, r],
                    recv_sem=recv_sems.at[it, r], device_id=(x, y, 1 - c), device_id_type=pl.DeviceIdType.MESH)
                cp.start()
                copies.append(cp)
        for cp in copies:
            cp.wait()

    hbm = pl.BlockSpec(memory_space=pl.ANY)
    return pl.pallas_call(
        body, out_shape=[SDS((N_CHIP,) + tuple(a.shape[1:]), a.dtype) for a in items], name="scatter_core",
        in_specs=[hbm] * n, out_specs=[hbm] * n,
        scratch_shapes=[pltpu.SemaphoreType.DMA((n, N_CHIP)), pltpu.SemaphoreType.DMA((n, N_CHIP))])(*items)


def _pair_add(item, other):
    shape = item.shape[1:]
    cols = shape[-1]
    rows = math.prod(shape) // cols
    tr = _tile(rows, 512, 16)

    def body(x_ref, o_ref, h_ref):
        c = lax.axis_index("c")
        mine = jnp.where(c == 0, x_ref[0].astype(F32), x_ref[1].astype(F32))
        h_ref[...] = (mine + o_ref[...].astype(F32)).astype(item.dtype)

    return pl.pallas_call(
        body, out_shape=SDS((N_CHIP, rows, cols), item.dtype), grid=(N_CHIP, rows // tr), name="pair_add",
        in_specs=[pl.BlockSpec((None, 2, tr, cols), lambda r, i: (r, 0, i, 0)),
                  pl.BlockSpec((None, tr, cols), lambda r, i: (r, i, 0))],
        out_specs=pl.BlockSpec((None, tr, cols), lambda r, i: (r, i, 0)),
        compiler_params=_cp("parallel", "parallel"))(
            item.reshape(N_CHIP, 2, rows, cols), other.reshape(N_CHIP, rows, cols)).reshape((N_CHIP,) + shape)


def _scatter_chip(items, groups):
    n = len(items)
    place = {it: (g, l) for g, members in enumerate(groups) for l, it in enumerate(members)}

    def body(*refs):
        ins, outs = refs[:n], refs[n:n + len(groups)]
        send_sems, recv_sems, local_sems = refs[n + len(groups):]
        x, y, c, _ = _mesh_place()
        chip = 2 * x + y
        copies = []
        for it in range(n):
            g, l = place[it]
            own = pltpu.make_async_copy(ins[it].at[chip], outs[g].at[chip, l], local_sems.at[it])
            own.start()
            copies.append(own)
            for kbits in range(1, N_CHIP):
                px = 1 - x if kbits & 2 else x
                py = 1 - y if kbits & 1 else y
                cp = pltpu.make_async_remote_copy(
                    src_ref=ins[it].at[2 * px + py], dst_ref=outs[g].at[chip, l],
                    send_sem=send_sems.at[it, kbits - 1], recv_sem=recv_sems.at[it, kbits - 1],
                    device_id=(px, py, c), device_id_type=pl.DeviceIdType.MESH)
                cp.start()
                copies.append(cp)
        for cp in copies:
            cp.wait()

    hbm = pl.BlockSpec(memory_space=pl.ANY)
    out_shape = [SDS((N_CHIP, len(members)) + tuple(items[members[0]].shape[1:]), items[members[0]].dtype)
                 for members in groups]
    return pl.pallas_call(
        body, out_shape=out_shape, name="scatter_chip", in_specs=[hbm] * n, out_specs=[hbm] * len(groups),
        scratch_shapes=[pltpu.SemaphoreType.DMA((n, N_CHIP - 1)), pltpu.SemaphoreType.DMA((n, N_CHIP - 1)),
                        pltpu.SemaphoreType.DMA((n,))])(*items)


def _scatter(items, groups):
    halves = _scatter_core(items)
    return _scatter_chip([_pair_add(a, h) for a, h in zip(items, halves)], groups)


def _cat_lanes(g, layer, nb, blk, width):
    _, _, rows, c = g.shape
    tr = _tile(rows, 256, 16)

    def body(g_ref, o_ref):
        for p in range(nb):
            o_ref[:, p * c:(p + 1) * c] = g_ref[p]
        if width > nb * c:
            o_ref[:, nb * c:] = jnp.zeros((tr, width - nb * c), g.dtype)

    return pl.pallas_call(
        body, out_shape=SDS((rows, width), g.dtype), grid=(rows // tr,), name="cat_lanes",
        in_specs=[pl.BlockSpec((nb, None, tr, c), lambda i: (blk, layer, i, 0))],
        out_specs=pl.BlockSpec((tr, width), lambda i: (i, 0)),
        compiler_params=_cp("parallel"))(g)


def _split_lanes(parts, c):
    rows = parts[0].shape[0]
    counts = [p.shape[1] // c for p in parts]
    tr = _tile(rows, 256, 16)

    def body(*refs):
        o_ref = refs[-1]
        q = 0
        for x_ref, cnt in zip(refs[:-1], counts):
            for p in range(cnt):
                o_ref[q] = x_ref[:, p * c:(p + 1) * c]
                q += 1

    return pl.pallas_call(
        body, out_shape=SDS((sum(counts), rows, c), parts[0].dtype), grid=(rows // tr,), name="split_lanes",
        in_specs=[pl.BlockSpec((tr, p.shape[1]), lambda i: (i, 0)) for p in parts],
        out_specs=pl.BlockSpec((sum(counts), tr, c), lambda i: (0, i, 0)),
        compiler_params=_cp("parallel"))(*parts)


def _unshard(g8, shard_shape, axis):
    full = jnp.moveaxis(g8.reshape((N_DEV,) + tuple(shard_shape)), 0, axis)
    shape = list(shard_shape)
    shape[axis] *= N_DEV
    return full.reshape(shape)


def _to_shards(full, axis):
    shape = list(full.shape)
    shape[axis:axis + 1] = [N_DEV, shape[axis] // N_DEV]
    return jnp.moveaxis(full.reshape(shape), axis, 0).reshape(N_DEV, -1)


SMALL = [n for n in SHARDED if n not in MATRICES]


def _flat_rows(parts):
    flat = jnp.concatenate([p.reshape(-1) for p in parts])
    chunk = 8 * ADAM_COLS
    n = -(-flat.shape[0] // chunk) * chunk
    return jnp.pad(flat, (0, n - flat.shape[0])).reshape(n // ADAM_COLS, ADAM_COLS)


def _prepare_vectors(small, shards):
    wt = {}
    flat = small.reshape(N_DEV, -1)
    off = 0
    for n in SMALL:
        size = shards[n].size
        wt[n] = _unshard(flat[:, off:off + size], shards[n].shape, SHARD_AXIS[n])
        off += size
    return wt


def _prepare_matrices(gathered):
    wt = {}
    for n in ('conv_w_out', 'fox_w_o', 'ffn_w_down'):
        if n in gathered:
            g = gathered[n]
            wt[n] = [g[:, l].reshape(N_DEV * g.shape[2], g.shape[3]) for l in range(g.shape[1])]
    if 'pool_w' in gathered:
        g = gathered['pool_w']
        wt['pool_w'] = [jnp.moveaxis(g[:, l], 0, 1).reshape(g.shape[2], N_DEV * g.shape[3], g.shape[4])
                        for l in range(g.shape[1])]
    if 'conv_w_in' in gathered:
        g = gathered['conv_w_in']
        wt['conv_w_in'] = [_cat_lanes(g, l, N_DEV, 0, N_DEV * g.shape[3]) for l in range(g.shape[1])]
    if 'fox_w_in' in gathered:
        g = gathered['fox_w_in']
        wt['fox_w_in'] = [_cat_lanes(g, l, N_DEV, 0, 3 * g.shape[2] + LANES) for l in range(g.shape[1])]
    if 'ffn_w_up' in gathered:
        g = gathered['ffn_w_up']
        half = N_DEV // 2
        wt['ffn_w_up_v'] = [_cat_lanes(g, l, half, 0, half * g.shape[3]) for l in range(g.shape[1])]
        wt['ffn_w_up_g'] = [_cat_lanes(g, l, half, 1, half * g.shape[3]) for l in range(g.shape[1])]
    return wt


def _pad_rows(w, rows):
    return jnp.pad(w, ((0, rows - w.shape[0]), (0, 0)))


def _fold(acc):
    return acc.sum(axis=0)


def _local_step(x, target, wt, late_shards=None, cut=None):
    wt = dict(wt)
    late_names = [n for n in MATRICES if late_shards and n in late_shards]
    late_recv = {}
    bsz, s_len, d = x.shape
    t = bsz * s_len
    depth = wt['norm_mix'].shape[0]
    n_heads = d // HEAD_DIM
    f = wt['ffn_w_up_v'][0].shape[1]
    row = lambda a: a.reshape(1, -1)
    grads = {n: {} for n in WEIGHTS}
    saved = []

    xc = x.reshape(t, d)
    hn_next = None
    for i in range(depth):
        j = i // 3
        kind = i % 3
        sv = {'x_mix': xc}
        gm = row(wt['norm_mix'][i])
        gf = row(wt['norm_ffn'][i])
        if kind == 0:
            hn = hn_next if hn_next is not None else _rmsnorm_fwd(xc, gm)
            p = _mm(hn, wt['conv_w_in'][j], bias=row(wt['conv_b_in'][j]), name="conv_in")
            u, sact = _conv_act_fwd(p.reshape(bsz, s_len, 2 * d), _pad_rows(wt['conv_dw'][j], 32),
                                    row(wt['conv_dw_b'][j]), row(wt['conv_ln_g'][j]), row(wt['conv_ln_b'][j]))
            sact = sact.reshape(t, d)
            xn, hf = _mm(sact, wt['conv_w_out'][j], bias=row(wt['conv_b_out'][j]), residual=xc, norm_gain=gf,
                         out_dtype=F32, name="conv_out")
            sv.update(hn=hn, p=p, u=u.reshape(t, d), sact=sact)
        elif kind == 1:
            xn, pp = _pool_fwd(xc.reshape(bsz, s_len, d), gm, wt['pool_w'][j], row(wt['pool_b'][j]),
                               row(wt['pool_scale'][j]))
            xn = xn.reshape(t, d)
            hf = _rmsnorm_fwd(xn, gf)
            sv.update(p=pp)
        else:
            hn = hn_next if hn_next is not None else _rmsnorm_fwd(xc, gm)
            wp = wt['fox_w_in'][j]
            bf = jnp.pad(wt['fox_b_f'][j], (0, LANES - n_heads)).reshape(1, LANES)
            gq = jnp.tile(wt['fox_q_gain'][j], n_heads).reshape(1, d)
            gk = jnp.tile(wt['fox_k_gain'][j], n_heads).reshape(1, d)
            proj = _mm(hn, wp, out_dtype=F32, name="fox_in")
            c = _fox_gate_fwd(proj.reshape(bsz, s_len, -1), bf, n_heads)
            crow = jnp.swapaxes(c, 1, 2)[:, :n_heads].reshape(bsz, n_heads // 2, 2, s_len)
            qn, kn, vb = _fox_qknorm_fwd(proj, gq, gk, d)
            shp = (bsz, s_len, d)
            o, lse, got = _flash_fwd(qn.reshape(shp), kn.reshape(shp), vb.reshape(shp), crow,
                                     gather=[late_shards[n] for n in late_names])
            for key, layers in _prepare_matrices(dict(zip(late_names, got))).items():
                wt[key] = wt[key] + layers
            o = o.reshape(t, d)
            xn, hf = _mm(o, wt['fox_w_o'][j], residual=xc, norm_gain=gf, out_dtype=F32, name="fox_out")
            sv.update(hn=hn, wp=wp, bf=bf, gq=gq, gk=gk, proj=proj, crow=crow, qn=qn, kn=kn, vb=vb, o=o, lse=lse)
        xc = xn
        sv['x_ffn'] = xc
        shf = (bsz, s_len, f)
        uv = _mm(hf, wt['ffn_w_up_v'][i], name="ffn_up").reshape(shf)
        ug = _mm(hf, wt['ffn_w_up_g'][i], name="ffn_up").reshape(shf)
        dw8 = _pad_rows(wt['ffn_dw'][i], 8)
        af, vv, vg = _ffn_act_fwd(uv, ug, dw8, row(wt['ffn_dw_b'][i]))
        af = af.reshape(t, f)
        if i + 1 < depth and (i + 1) % 3 != 1:
            xc, hn_next = _mm(af, wt['ffn_w_down'][i], residual=xc, norm_gain=row(wt['norm_mix'][i + 1]),
                              out_dtype=F32, name="ffn_down")
        else:
            xc, hn_next = _mm(af, wt['ffn_w_down'][i], residual=xc, out_dtype=F32, name="ffn_down"), None
        sv.update(hf=hf, uv=uv, ug=ug, vv=vv, vg=vg, af=af, dw8=dw8)
        saved.append(sv)

    dx, sq = _loss_head(xc, target.reshape(t, d))

    for i in reversed(range(depth)):
        j = i // 3
        kind = i % 3
        sv = saved[i]
        shf = (bsz, s_len, f)
        da = _mm(dx, wt['ffn_w_down'][i], trans_b=True, name="ffn_down_dgrad")
        gw, _ = _wgrad(sv['af'], dx, name="ffn_down_wgrad")
        grads['ffn_w_down'][i] = gw.reshape(N_DEV, f // N_DEV, d)
        dvv, dvg, dbv, dbg = _ffn_act_bwd1(sv['vv'], sv['vg'], da.reshape(shf))
        grads['ffn_dw_b'][i] = jnp.concatenate([_fold(dbv), _fold(dbg)])
        duv, dug, ddwv, ddwg = _ffn_act_bwd2(dvv, dvg, sv['uv'], sv['ug'], sv['dw8'])
        grads['ffn_dw'][i] = jnp.concatenate([ddwv.sum(axis=1), ddwg.sum(axis=1)], axis=1)
        duv, dug = duv.reshape(t, f), dug.reshape(t, f)
        gv, _ = _wgrad(sv['hf'], duv, name="ffn_up_wgrad")
        gg, _ = _wgrad(sv['hf'], dug, name="ffn_up_wgrad")
        grads['ffn_w_up'][i] = _split_lanes([gv, gg], 2 * f // N_DEV)
        dx, dg = _mm(duv, wt['ffn_w_up_v'][i], trans_b=True, a2=dug, b2=wt['ffn_w_up_g'][i],
                     norm_bwd=(sv['x_ffn'], row(wt['norm_ffn'][i]), dx), name="ffn_up_dgrad")
        grads['norm_ffn'][i] = _fold(dg)
        gm = row(wt['norm_mix'][i])
        if kind == 0:
            dsact = _mm(dx, wt['conv_w_out'][j], trans_b=True, name="conv_out_dgrad")
            gw, cs = _wgrad(sv['sact'], dx, name="conv_out_wgrad")
            grads['conv_w_out'][j] = gw.reshape(N_DEV, d // N_DEV, d)
            grads['conv_b_out'][j] = _fold(cs)
            du, dlg, dlb, dwb = _conv_act_bwd1(sv['u'], dsact, row(wt['conv_ln_g'][j]), row(wt['conv_ln_b'][j]))
            grads['conv_ln_g'][j], grads['conv_ln_b'][j], grads['conv_dw_b'][j] = _fold(dlg), _fold(dlb), _fold(dwb)
            dp, ddw = _conv_act_bwd2(du.reshape(bsz, s_len, d), sv['p'].reshape(bsz, s_len, 2 * d),
                                     _pad_rows(wt['conv_dw'][j], 32))
            grads['conv_dw'][j] = ddw.sum(axis=1)[:wt['conv_dw'].shape[1]]
            dp = dp.reshape(t, 2 * d)
            gw, cs = _wgrad(sv['hn'], dp, name="conv_in_wgrad")
            grads['conv_w_in'][j] = _split_lanes([gw], 2 * d // N_DEV)
            grads['conv_b_in'][j] = _fold(cs)
            dx, dg = _mm(dp, wt['conv_w_in'][j], trans_b=True, norm_bwd=(sv['x_mix'], gm, dx), name="conv_in_dgrad")
            grads['norm_mix'][i] = _fold(dg)
        elif kind == 1:
            shp = (bsz, s_len, d)
            dxn, dwp, dbp, dsc, dg = _pool_bwd(sv['x_mix'].reshape(shp), dx.reshape(shp), sv['p'], gm, wt['pool_w'][j],
                                               row(wt['pool_b'][j]), row(wt['pool_scale'][j]))
            dx = dxn.reshape(t, d)
            ng, cg = dwp.shape[0], dwp.shape[1]
            grads['pool_w'][j] = jnp.moveaxis(dwp.reshape(ng, N_DEV, cg // N_DEV, cg), 1, 0).astype(BF16)
            grads['pool_b'][j] = _fold(dbp).reshape(wt['pool_b'].shape[1:])
            grads['pool_scale'][j] = _fold(dsc)
            grads['norm_mix'][i] = _fold(dg)
        else:
            shp = (bsz, s_len, d)
            do = _mm(dx, wt['fox_w_o'][j], trans_b=True, name="fox_out_dgrad")
            gw, _ = _wgrad(sv['o'], dx, name="fox_out_wgrad")
            grads['fox_w_o'][j] = gw.reshape(N_DEV, d // N_DEV, d)
            fl_args = (sv['qn'].reshape(shp), sv['kn'].reshape(shp), sv['vb'].reshape(shp), do.reshape(shp),
                       sv['o'].reshape(shp), sv['lse'], sv['crow'])
            items, groups = [], []
            for n in late_names:
                members = [grads[n][l] for l in sorted(grads[n]) if l >= cut[n]]
                groups.append(list(range(len(items), len(items) + len(members))))
                items += members
            dq, dk, dv, dcrow, got = _flash_bwd(*fl_args, items=items, groups=groups)
            late_recv = dict(zip(late_names, got))
            dc = jnp.swapaxes(dcrow.reshape(bsz, n_heads, s_len), 1, 2)
            dc = jnp.pad(dc, ((0, 0), (0, 0), (0, LANES - n_heads)))
            dfl, dbf = _fox_gate_bwd(dc, sv['proj'].reshape(bsz, s_len, -1), sv['bf'], n_heads)
            grads['fox_b_f'][j] = _fold(dbf)[:n_heads]
            dqkv, dgq, dgk = _fox_qknorm_bwd(sv['proj'], dq.reshape(t, d), dk.reshape(t, d), dv.reshape(t, d),
                                             sv['gq'], sv['gk'], d)
            grads['fox_q_gain'][j] = _fold(dgq).reshape(n_heads, HEAD_DIM).sum(axis=0)
            grads['fox_k_gain'][j] = _fold(dgk).reshape(n_heads, HEAD_DIM).sum(axis=0)
            dproj = jnp.concatenate([dqkv, dfl.reshape(t, LANES)], axis=1)
            dwp, _ = _wgrad(sv['hn'], dproj, name="fox_in_wgrad")
            grads['fox_w_in'][j] = _split_lanes([dwp], (3 * d + n_heads) // N_DEV)
            dx, dg = _mm(dproj, sv['wp'], trans_b=True, norm_bwd=(sv['x_mix'], gm, dx), name="fox_in_dgrad")
            grads['norm_mix'][i] = _fold(dg)

    small = {n: jnp.stack([g[k] for k in sorted(g)]) for n, g in grads.items() if n not in MATRICES}
    big = {n: [grads[n][k] for k in sorted(grads[n]) if n not in late_recv or k < cut[n]] for n in MATRICES}
    return sq.sum(), dx.reshape(bsz, s_len, d), small, big, late_recv


def _train_step(x, target, w, m, v):
    depth = w['norm_mix'].shape[0]
    attn = [i for i in range(depth) if i % 3 == 2]
    cut = {n: w[n].shape[0] for n in MATRICES}
    if len(attn) == 1:
        cut['ffn_w_up'] = cut['ffn_w_down'] = attn[0]
        later_conv = [i // 3 for i in range(attn[0] + 1, depth) if i % 3 == 0]
        if later_conv:
            cut['conv_w_in'] = cut['conv_w_out'] = later_conv[0]
    late_shards = {n: w[n][cut[n]:].astype(BF16) for n in MATRICES if 0 < cut[n] < w[n].shape[0]}
    got = _gather([w[n][:cut[n]].astype(BF16) for n in MATRICES] + [_flat_rows([w[n] for n in SMALL])])
    wt = _prepare_matrices(dict(zip(MATRICES, got[:-1])))
    wt.update(_prepare_vectors(got[-1], w))
    wt.update({n: w[n] for n in REPLICATED})
    sq, grad_x, gsmall, gbig, late_recv = _local_step(x, target, wt, late_shards, cut)
    d = x.shape[-1]

    shard_rows = jnp.concatenate([_to_shards(gsmall[n], SHARD_AXIS[n]) for n in SMALL], axis=1)
    rep = jnp.concatenate([gsmall[n].reshape(-1) for n in REPLICATED] + [(0.5 / d) * sq.reshape(1)])
    rows = jnp.concatenate([shard_rows, jnp.broadcast_to(rep, (N_DEV, rep.shape[0]))], axis=1)
    chunk = 8 * ADAM_COLS
    n_all = rows.shape[1]
    n_pad = -(-n_all // chunk) * chunk
    rows = jnp.pad(rows, ((0, 0), (0, n_pad - n_all))).reshape(N_DEV, n_pad // ADAM_COLS, ADAM_COLS)

    items, groups = [], []
    for n in MATRICES:
        groups.append(list(range(len(items), len(items) + len(gbig[n]))))
        items += gbig[n]
    groups.append([len(items)])
    items.append(rows)
    recv = _scatter(items, groups)

    res = [{}, {}, {}, {}]
    for n, r in zip(MATRICES, recv[:-1]):
        if n in late_recv:
            c = cut[n]
            outs = zip(_adamw(r, w[n][:c], m[n][:c], v[n][:c]), _adamw(late_recv[n], w[n][c:], m[n][c:], v[n][c:]))
            outs = [jnp.concatenate(pair, axis=0) for pair in outs]
        else:
            outs = _adamw(r, w[n], m[n], v[n])
        for k, o in enumerate(outs):
            res[k][n] = o
    order = SMALL + REPLICATED

    def flat(tree):
        parts = jnp.concatenate([tree[n].reshape(-1) for n in order])
        return jnp.pad(parts, (0, n_pad - parts.shape[0])).reshape(n_pad // ADAM_COLS, ADAM_COLS)

    outs = [o.reshape(-1) for o in _adamw(recv[-1].reshape((N_CHIP,) + rows.shape[1:]), flat(w), flat(m), flat(v))]
    off = 0
    for n in order:
        size = w[n].size
        for k in range(4):
            res[k][n] = outs[k][off:off + size].reshape(w[n].shape)
        off += size
    loss = outs[0][n_all - 1]
    return (loss, grad_x, *[res[0][n] for n in WEIGHTS], *[res[1][n] for n in WEIGHTS],
            *[res[2][n] for n in WEIGHTS], *[res[3][n] for n in WEIGHTS])


def kernel(x, norm_mix, norm_ffn, conv_w_in, conv_b_in, conv_dw, conv_dw_b, conv_ln_g, conv_ln_b, conv_w_out, conv_b_out, pool_w, pool_b, pool_scale, fox_w_in, fox_b_f, fox_q_gain, fox_k_gain, fox_w_o, ffn_w_up, ffn_dw, ffn_dw_b, ffn_w_down, loss_target, m_norm_mix, m_norm_ffn, m_conv_w_in, m_conv_b_in, m_conv_dw, m_conv_dw_b, m_conv_ln_g, m_conv_ln_b, m_conv_w_out, m_conv_b_out, m_pool_w, m_pool_b, m_pool_scale, m_fox_w_in, m_fox_b_f, m_fox_q_gain, m_fox_k_gain, m_fox_w_o, m_ffn_w_up, m_ffn_dw, m_ffn_dw_b, m_ffn_w_down, v_norm_mix, v_norm_ffn, v_conv_w_in, v_conv_b_in, v_conv_dw, v_conv_dw_b, v_conv_ln_g, v_conv_ln_b, v_conv_w_out, v_conv_b_out, v_pool_w, v_pool_b, v_pool_scale, v_fox_w_in, v_fox_b_f, v_fox_q_gain, v_fox_k_gain, v_fox_w_o, v_ffn_w_up, v_ffn_dw, v_ffn_dw_b, v_ffn_w_down):
    w = dict(zip(WEIGHTS, (norm_mix, norm_ffn, conv_w_in, conv_b_in, conv_dw, conv_dw_b, conv_ln_g, conv_ln_b, conv_w_out, conv_b_out, pool_w, pool_b, pool_scale, fox_w_in, fox_b_f, fox_q_gain, fox_k_gain, fox_w_o, ffn_w_up, ffn_dw, ffn_dw_b, ffn_w_down)))
    m = dict(zip(WEIGHTS, (m_norm_mix, m_norm_ffn, m_conv_w_in, m_conv_b_in, m_conv_dw, m_conv_dw_b, m_conv_ln_g, m_conv_ln_b, m_conv_w_out, m_conv_b_out, m_pool_w, m_pool_b, m_pool_scale, m_fox_w_in, m_fox_b_f, m_fox_q_gain, m_fox_k_gain, m_fox_w_o, m_ffn_w_up, m_ffn_dw, m_ffn_dw_b, m_ffn_w_down)))
    v = dict(zip(WEIGHTS, (v_norm_mix, v_norm_ffn, v_conv_w_in, v_conv_b_in, v_conv_dw, v_conv_dw_b, v_conv_ln_g, v_conv_ln_b, v_conv_w_out, v_conv_b_out, v_pool_w, v_pool_b, v_pool_scale, v_fox_w_in, v_fox_b_f, v_fox_q_gain, v_fox_k_gain, v_fox_w_o, v_ffn_w_up, v_ffn_dw, v_ffn_dw_b, v_ffn_w_down)))
    return _train_step(x, loss_target, w, m, v)
```

```python
import functools
import math

import jax
import jax.numpy as jnp
from jax import lax
from jax.experimental import pallas as pl
from jax.experimental.pallas import tpu as pltpu

F32, BF16 = jnp.float32, jnp.bfloat16
SDS = jax.ShapeDtypeStruct

N_DEV = 8
EPS = 1e-6
POOL_WINDOWS = (2, 4, 8, 16)
HEAD_DIM = 64
ADAM_LR, ADAM_B1, ADAM_B2, ADAM_EPS, ADAM_WD, ADAM_STEP = 0.001, 0.9, 0.999, 1e-08, 0.01, 10
LANES = 128
VMEM_LIMIT_BYTES = 48 * 1024 * 1024
NEG = -1e30

WEIGHTS = ['norm_mix', 'norm_ffn', 'conv_w_in', 'conv_b_in', 'conv_dw', 'conv_dw_b', 'conv_ln_g', 'conv_ln_b',
           'conv_w_out', 'conv_b_out', 'pool_w', 'pool_b', 'pool_scale', 'fox_w_in', 'fox_b_f', 'fox_q_gain',
           'fox_k_gain', 'fox_w_o', 'ffn_w_up', 'ffn_dw', 'ffn_dw_b', 'ffn_w_down']
SHARD_AXIS = {'conv_w_in': 2, 'conv_b_in': 1, 'conv_dw': 2, 'conv_dw_b': 1, 'conv_ln_g': 1, 'conv_ln_b': 1,
              'conv_w_out': 1, 'conv_b_out': 1, 'pool_w': 2, 'pool_b': 2, 'fox_w_in': 2, 'fox_w_o': 1,
              'ffn_w_up': 2, 'ffn_dw': 2, 'ffn_w_down': 1}
MATRICES = ('conv_w_in', 'conv_w_out', 'pool_w', 'fox_w_in', 'fox_w_o', 'ffn_w_up', 'ffn_w_down')
SHARDED = [n for n in WEIGHTS if n in SHARD_AXIS]
REPLICATED = [n for n in WEIGHTS if n not in SHARD_AXIS]


def _cp(*sem):
    return pltpu.CompilerParams(dimension_semantics=sem, vmem_limit_bytes=VMEM_LIMIT_BYTES)


def _tile(n, pref, align=8):
    if n <= pref:
        return n
    t = (pref // align) * align
    while t >= align:
        if n % t == 0:
            return t
        t -= align
    return n


def _fold8(x):
    r, c = x.shape
    return x.reshape(r // 8, 8, c).sum(axis=0)


def _sigmoid(x):
    return 0.5 * jnp.tanh(0.5 * x) + 0.5


def _shifts_back(cur, tail, n):
    hb = tail.shape[0]
    xe = jnp.concatenate([tail, cur], axis=0)
    return [cur] + [pltpu.roll(xe, j, axis=0)[hb:] for j in range(1, n)]


def _shifts_fwd(cur, head, n):
    ts = cur.shape[0]
    xe = jnp.concatenate([cur, head], axis=0)
    ln = xe.shape[0]
    return [cur] + [pltpu.roll(xe, ln - j, axis=0)[:ts] for j in range(1, n)]


def _dot_hi(a, b):
    return jnp.dot(a, b, preferred_element_type=F32, precision=lax.Precision.HIGHEST)


def _rmsnorm_fwd(x, g):
    t, d = x.shape
    tm = _tile(t, 512)

    def body(x_ref, g_ref, h_ref):
        xv = x_ref[...]
        r = lax.rsqrt(jnp.mean(xv * xv, axis=-1, keepdims=True) + EPS)
        h_ref[...] = (xv * r * g_ref[...]).astype(BF16)

    return pl.pallas_call(
        body, out_shape=SDS((t, d), BF16), grid=(t // tm,), name="rmsnorm_fwd",
        in_specs=[pl.BlockSpec((tm, d), lambda i: (i, 0)), pl.BlockSpec((1, d), lambda i: (0, 0))],
        out_specs=pl.BlockSpec((tm, d), lambda i: (i, 0)), compiler_params=_cp("parallel"))(x, g)


def _mm(a, b, *, trans_b=False, bias=None, residual=None, a2=None, b2=None, norm_gain=None, norm_bwd=None,
        out_dtype=BF16, name="mm"):
    m, k = a.shape
    n = b.shape[0] if trans_b else b.shape[1]
    tm, tn, tk = _tile(m, 1024, 16), _tile(n, 1536, LANES), _tile(k, 1536, LANES)
    if norm_bwd is not None:
        if a2 is not None:
            tm = _tile(m, 512, 16)
        else:
            tk = _tile(k, 768, LANES)
    nk = k // tk
    two = a2 is not None
    steps = 2 * nk if two else nk
    dims = (((1,), (1,)), ((), ())) if trans_b else (((1,), (0,)), ((), ()))
    has_bias, has_res, has_norm, has_nbwd = bias is not None, residual is not None, norm_gain is not None, \
        norm_bwd is not None
    assert not (has_norm or has_nbwd) or tn == n
    n_out = 2 if (has_norm or has_nbwd) else 1

    def body(*refs):
        pos = 4 if two else 2
        bias_ref = refs[pos] if has_bias else None
        pos += has_bias
        res_ref = refs[pos] if has_res else None
        pos += has_res
        gain_ref = refs[pos] if has_norm else None
        pos += has_norm
        x_ref, g_ref, dres_ref = refs[pos:pos + 3] if has_nbwd else (None, None, None)
        pos += 3 * has_nbwd
        outs = refs[pos:pos + n_out]
        first = (pl.program_id(0) == 0) & (pl.program_id(1) == 0) & (pl.program_id(2) == 0)

        def finish(r):
            if has_bias:
                r = r + bias_ref[...]
            if has_res:
                r = r + res_ref[...]
            if has_nbwd:
                xv = x_ref[...]
                rs = lax.rsqrt(jnp.mean(xv * xv, axis=-1, keepdims=True) + EPS)
                xh = xv * rs
                u = r * g_ref[...]
                outs[0][...] = dres_ref[...] + rs * (u - xh * jnp.mean(u * xh, axis=-1, keepdims=True))
                outs[1][...] += _fold8(r * xh)
                return
            outs[0][...] = r.astype(out_dtype)
            if has_norm:
                outs[1][...] = (r * lax.rsqrt(jnp.mean(r * r, axis=-1, keepdims=True) + EPS)
                                * gain_ref[...]).astype(BF16)

        def dot(a_ref, b_ref):
            return lax.dot_general(a_ref[...].astype(BF16), b_ref[...].astype(BF16), dims, preferred_element_type=F32)

        if has_nbwd:
            @pl.when(first)
            def _():
                outs[1][...] = jnp.zeros_like(outs[1])

        if steps == 1:
            finish(dot(refs[0], refs[1]))
            return
        acc_ref = refs[-1]
        kk = pl.program_id(2)

        @pl.when(kk == 0)
        def _():
            acc_ref[...] = jnp.zeros_like(acc_ref)

        @pl.when(kk < nk)
        def _():
            acc_ref[...] += dot(refs[0], refs[1])

        if two:
            @pl.when(kk >= nk)
            def _():
                acc_ref[...] += dot(refs[2], refs[3])

        @pl.when(kk == steps - 1)
        def _():
            finish(acc_ref[...])

    def pair(first):
        kmap = (lambda kk: jnp.minimum(kk, nk - 1)) if first else (lambda kk: jnp.maximum(kk - nk, 0))
        a_spec = pl.BlockSpec((tm, tk), lambda j, i, kk: (i, kmap(kk)))
        if trans_b:
            b_spec = pl.BlockSpec((tn, tk), lambda j, i, kk: (j, kmap(kk)))
        else:
            b_spec = pl.BlockSpec((tk, tn), lambda j, i, kk: (kmap(kk), j))
        return [a_spec, b_spec]

    in_specs, args = pair(True), [a, b]
    if two:
        in_specs += pair(False)
        args += [a2, b2]
    if has_bias:
        in_specs.append(pl.BlockSpec((1, tn), lambda j, i, kk: (0, j)))
        args.append(bias)
    tile = pl.BlockSpec((tm, tn), lambda j, i, kk: (i, j))
    vec = pl.BlockSpec((1, tn), lambda j, i, kk: (0, j))
    if has_res:
        in_specs.append(tile)
        args.append(residual)
    if has_norm:
        in_specs.append(vec)
        args.append(norm_gain)
    if has_nbwd:
        in_specs += [tile, vec, tile]
        args += list(norm_bwd)
    out_shape, out_specs = [SDS((m, n), F32 if has_nbwd else out_dtype)], [tile]
    if has_norm:
        out_shape.append(SDS((m, n), BF16))
        out_specs.append(tile)
    if has_nbwd:
        out_shape.append(SDS((8, n), F32))
        out_specs.append(pl.BlockSpec((8, tn), lambda j, i, kk: (0, 0)))
    outs = pl.pallas_call(
        body, out_shape=out_shape, grid=(n // tn, m // tm, steps), name=name, in_specs=in_specs, out_specs=out_specs,
        scratch_shapes=[] if steps == 1 else [pltpu.VMEM((tm, tn), F32)],
        compiler_params=_cp("arbitrary", "arbitrary", "arbitrary"))(*args)
    return outs[0] if n_out == 1 else tuple(outs)


def _wgrad(a, g, *, out_dtype=BF16, name="wgrad"):
    m, ka = a.shape
    n = g.shape[1]
    ta, tn, tm = _tile(ka, 1536, LANES), _tile(n, 1536, LANES), _tile(m, 1024)
    nm = m // tm

    def body(a_ref, g_ref, o_ref, cs_ref, acc_ref):
        i, mm = pl.program_id(1), pl.program_id(2)

        @pl.when(mm == 0)
        def _():
            acc_ref[...] = jnp.zeros_like(acc_ref)

        @pl.when((mm == 0) & (i == 0))
        def _():
            cs_ref[...] = jnp.zeros_like(cs_ref)

        gv = g_ref[...]
        acc_ref[...] += lax.dot_general(a_ref[...].astype(BF16), gv.astype(BF16), (((0,), (0,)), ((), ())),
                                        preferred_element_type=F32)

        @pl.when(i == 0)
        def _():
            cs_ref[...] += _fold8(gv.astype(F32))

        @pl.when(mm == nm - 1)
        def _():
            o_ref[...] = acc_ref[...].astype(out_dtype)

    return pl.pallas_call(
        body, out_shape=(SDS((ka, n), out_dtype), SDS((8, n), F32)), grid=(n // tn, ka // ta, nm), name=name,
        in_specs=[pl.BlockSpec((tm, ta), lambda j, i, mm: (mm, i)), pl.BlockSpec((tm, tn), lambda j, i, mm: (mm, j))],
        out_specs=(pl.BlockSpec((ta, tn), lambda j, i, mm: (i, j)), pl.BlockSpec((8, tn), lambda j, i, mm: (0, j))),
        scratch_shapes=[pltpu.VMEM((ta, tn), F32)],
        compiler_params=_cp("arbitrary", "arbitrary", "arbitrary"))(a, g)


FFN_HALO = 16


def _ffn_conv(uc_ref, up_ref, w_ref, b_ref, s):
    u = uc_ref[...].astype(F32)
    tail = jnp.where(s > 0, up_ref[...].astype(F32), 0.0)
    sh = _shifts_back(u, tail, 3)
    return sh, sh[2] * w_ref[0:1, :] + sh[1] * w_ref[1:2, :] + sh[0] * w_ref[2:3, :] + b_ref[...]


def _ffn_act_fwd(uv, ug, dw8, b):
    bsz, s_len, f = uv.shape
    tc, ts = f, _tile(s_len, 256, FFN_HALO)
    nf, r = f // tc, ts // FFN_HALO

    def body(uv_ref, uvp_ref, ug_ref, ugp_ref, wv_ref, wg_ref, bv_ref, bg_ref, a_ref, vv_ref, vg_ref):
        s = pl.program_id(2)
        _, val = _ffn_conv(uv_ref, uvp_ref, wv_ref, bv_ref, s)
        _, gate = _ffn_conv(ug_ref, ugp_ref, wg_ref, bg_ref, s)
        a_ref[...] = (gate * _sigmoid(gate) * val).astype(BF16)
        vv_ref[...] = val.astype(BF16)
        vg_ref[...] = gate.astype(BF16)

    cur = pl.BlockSpec((None, ts, tc), lambda bi, j, s: (bi, s, j))
    prev = pl.BlockSpec((None, FFN_HALO, tc), lambda bi, j, s: (bi, jnp.maximum(s * r - 1, 0), j))

    def par(rows, off):
        return pl.BlockSpec((rows, tc), lambda bi, j, s: (0, j + off))

    return pl.pallas_call(
        body, out_shape=(SDS((bsz, s_len, f), BF16),) * 3, grid=(bsz, nf, s_len // ts), name="ffn_act_fwd",
        in_specs=[cur, prev, cur, prev, par(8, 0), par(8, nf), par(1, 0), par(1, nf)], out_specs=(cur, cur, cur),
        compiler_params=_cp("parallel", "parallel", "arbitrary"))(uv, uv, ug, ug, dw8, dw8, b, b)


def _ffn_act_bwd1(vv, vg, da):
    bsz, s_len, f = vv.shape
    tc, ts = f, _tile(s_len, 256, FFN_HALO)
    nf = f // tc

    def body(vv_ref, vg_ref, da_ref, dvv_ref, dvg_ref, dbv_ref, dbg_ref):
        @pl.when((pl.program_id(1) == 0) & (pl.program_id(2) == 0))
        def _():
            dbv_ref[...] = jnp.zeros_like(dbv_ref)
            dbg_ref[...] = jnp.zeros_like(dbg_ref)

        val, gate = vv_ref[...].astype(F32), vg_ref[...].astype(F32)
        sg = _sigmoid(gate)
        dav = da_ref[...].astype(F32)
        dval = dav * gate * sg
        dgate = dav * val * (sg * (1.0 + gate * (1.0 - sg)))
        dvv_ref[...] = dval.astype(BF16)
        dvg_ref[...] = dgate.astype(BF16)
        dbv_ref[...] += _fold8(dval)
        dbg_ref[...] += _fold8(dgate)

    cur = pl.BlockSpec((None, ts, tc), lambda j, bi, s: (bi, s, j))
    acc1 = pl.BlockSpec((8, tc), lambda j, bi, s: (0, j))
    return pl.pallas_call(
        body, out_shape=(SDS((bsz, s_len, f), BF16), SDS((bsz, s_len, f), BF16), SDS((8, f), F32), SDS((8, f), F32)),
        grid=(nf, bsz, s_len // ts), name="ffn_act_bwd1", in_specs=[cur, cur, cur], out_specs=(cur, cur, acc1, acc1),
        compiler_params=_cp("arbitrary", "arbitrary", "arbitrary"))(vv, vg, da)


def _ffn_act_bwd2(dvv, dvg, uv, ug, dw8):
    bsz, s_len, f = dvv.shape
    tc, ts = f, _tile(s_len, 256, FFN_HALO)
    nf, r, ns = f // tc, ts // FFN_HALO, s_len // ts

    def body(vc_ref, vn_ref, gc_ref, gn_ref, uv_ref, ug_ref, wv_ref, wg_ref, duv_ref, dug_ref, ddwv_ref, ddwg_ref):
        bi, s = pl.program_id(1), pl.program_id(2)

        @pl.when((bi == 0) & (s == 0))
        def _():
            ddwv_ref[...] = jnp.zeros_like(ddwv_ref)
            ddwg_ref[...] = jnp.zeros_like(ddwg_ref)

        for dc_ref, dn_ref, u_ref, w_ref, du_ref, ddw_ref in (
                (vc_ref, vn_ref, uv_ref, wv_ref, duv_ref, ddwv_ref),
                (gc_ref, gn_ref, ug_ref, wg_ref, dug_ref, ddwg_ref)):
            d = dc_ref[...].astype(F32)
            head = jnp.where(s < ns - 1, dn_ref[...].astype(F32), 0.0)
            sh = _shifts_fwd(d, head, 3)
            du_ref[...] = (sh[0] * w_ref[2:3, :] + sh[1] * w_ref[1:2, :] + sh[2] * w_ref[0:1, :]).astype(BF16)
            u = u_ref[...].astype(F32)
            for j in range(3):
                ddw_ref[2 - j] += _fold8(sh[j] * u)

    cur = pl.BlockSpec((None, ts, tc), lambda j, bi, s: (bi, s, j))
    nxt = pl.BlockSpec((None, FFN_HALO, tc),
                       lambda j, bi, s: (bi, jnp.minimum((s + 1) * r, s_len // FFN_HALO - 1), j))

    def par(off):
        return pl.BlockSpec((8, tc), lambda j, bi, s: (0, j + off))

    acc3 = pl.BlockSpec((3, 8, tc), lambda j, bi, s: (0, 0, j))
    return pl.pallas_call(
        body, out_shape=(SDS((bsz, s_len, f), BF16), SDS((bsz, s_len, f), BF16), SDS((3, 8, f), F32),
                         SDS((3, 8, f), F32)),
        grid=(nf, bsz, ns), name="ffn_act_bwd2",
        in_specs=[cur, nxt, cur, nxt, cur, cur, par(0), par(nf)], out_specs=(cur, cur, acc3, acc3),
        compiler_params=_cp("arbitrary", "arbitrary", "arbitrary"))(dvv, dvv, dvg, dvg, uv, ug, dw8, dw8)


CONV_HALO = 32
CONV_CHUNK = 256


def _conv_act_fwd(p, dw32, dwb, ln_g, ln_b):
    bsz, s_len, d2 = p.shape
    d = d2 // 2
    kw = 31
    ts = _tile(s_len, 256, CONV_HALO)
    r = ts // CONV_HALO
    cc = min(CONV_CHUNK, d)

    def body(pc_ref, pp_ref, w_ref, wb_ref, g_ref, b_ref, u_ref, s_ref):
        s = pl.program_id(1)
        tot = jnp.zeros((ts, 1), F32)
        for c0 in range(0, d, cc):
            a = pc_ref[:, c0:c0 + cc].astype(F32)
            g = pc_ref[:, d + c0:d + c0 + cc].astype(F32)
            z = a * _sigmoid(g)
            ap = pp_ref[:, c0:c0 + cc].astype(F32)
            gp = pp_ref[:, d + c0:d + c0 + cc].astype(F32)
            tail = jnp.where(s > 0, ap * _sigmoid(gp), 0.0)
            sh = _shifts_back(z, tail, kw)
            acc = wb_ref[:, c0:c0 + cc] + sh[0] * w_ref[kw - 1:kw, c0:c0 + cc]
            for j in range(1, kw):
                acc = acc + sh[j] * w_ref[kw - 1 - j:kw - j, c0:c0 + cc]
            u_ref[:, c0:c0 + cc] = acc
            tot = tot + jnp.sum(acc, axis=-1, keepdims=True)
        u = u_ref[...]
        mu = tot / d
        uc = u - mu
        var = jnp.mean(uc * uc, axis=-1, keepdims=True)
        ul = uc * lax.rsqrt(var + EPS) * g_ref[...] + b_ref[...]
        s_ref[...] = (ul * _sigmoid(ul)).astype(BF16)

    vec = pl.BlockSpec((1, d), lambda bi, s: (0, 0))
    return pl.pallas_call(
        body, out_shape=(SDS((bsz, s_len, d), F32), SDS((bsz, s_len, d), BF16)), grid=(bsz, s_len // ts),
        name="conv_act_fwd",
        in_specs=[pl.BlockSpec((None, ts, d2), lambda bi, s: (bi, s, 0)),
                  pl.BlockSpec((None, CONV_HALO, d2), lambda bi, s: (bi, jnp.maximum(s * r - 1, 0), 0)),
                  pl.BlockSpec((32, d), lambda bi, s: (0, 0)), vec, vec, vec],
        out_specs=(pl.BlockSpec((None, ts, d), lambda bi, s: (bi, s, 0)),
                   pl.BlockSpec((None, ts, d), lambda bi, s: (bi, s, 0))),
        compiler_params=_cp("parallel", "arbitrary"))(p, p, dw32, dwb, ln_g, ln_b)


def _conv_act_bwd1(u, ds, ln_g, ln_b):
    t, d = u.shape
    ts = _tile(t, 256)

    def body(u_ref, ds_ref, g_ref, b_ref, du_ref, dg_ref, db_ref, dwb_ref):
        @pl.when(pl.program_id(0) == 0)
        def _():
            dg_ref[...] = jnp.zeros_like(dg_ref)
            db_ref[...] = jnp.zeros_like(db_ref)
            dwb_ref[...] = jnp.zeros_like(dwb_ref)

        uv = u_ref[...]
        uc = uv - jnp.mean(uv, axis=-1, keepdims=True)
        rstd = lax.rsqrt(jnp.mean(uc * uc, axis=-1, keepdims=True) + EPS)
        uh = uc * rstd
        ul = uh * g_ref[...] + b_ref[...]
        sg = _sigmoid(ul)
        dul = ds_ref[...].astype(F32) * (sg * (1.0 + ul * (1.0 - sg)))
        duh = dul * g_ref[...]
        du = rstd * (duh - jnp.mean(duh, axis=-1, keepdims=True) - uh * jnp.mean(duh * uh, axis=-1, keepdims=True))
        du_ref[...] = du
        dg_ref[...] += _fold8(dul * uh)
        db_ref[...] += _fold8(dul)
        dwb_ref[...] += _fold8(du)

    row = pl.BlockSpec((ts, d), lambda i: (i, 0))
    vec = pl.BlockSpec((1, d), lambda i: (0, 0))
    acc = pl.BlockSpec((8, d), lambda i: (0, 0))
    return pl.pallas_call(
        body, out_shape=(SDS((t, d), F32), SDS((8, d), F32), SDS((8, d), F32), SDS((8, d), F32)), grid=(t // ts,),
        name="conv_act_bwd1", in_specs=[row, row, vec, vec], out_specs=(row, acc, acc, acc),
        compiler_params=_cp("arbitrary"))(u, ds, ln_g, ln_b)


def _conv_act_bwd2(du, p, dw32):
    bsz, s_len, d2 = p.shape
    d = d2 // 2
    kw = 31
    ts = _tile(s_len, 256, CONV_HALO)
    r, ns = ts // CONV_HALO, s_len // ts
    cc = min(CONV_CHUNK, d)

    def body(dc_ref, dn_ref, pc_ref, pp_ref, w_ref, dp_ref, ddw_ref):
        bi, s = pl.program_id(0), pl.program_id(1)

        @pl.when((bi == 0) & (s == 0))
        def _():
            ddw_ref[...] = jnp.zeros_like(ddw_ref)

        for c0 in range(0, d, cc):
            a = pc_ref[:, c0:c0 + cc].astype(F32)
            g = pc_ref[:, d + c0:d + c0 + cc].astype(F32)
            sg = _sigmoid(g)
            z = a * sg
            ap = pp_ref[:, c0:c0 + cc].astype(F32)
            gp = pp_ref[:, d + c0:d + c0 + cc].astype(F32)
            tail = jnp.where(s > 0, ap * _sigmoid(gp), 0.0)
            duv = dc_ref[:, c0:c0 + cc]
            head = jnp.where(s < ns - 1, dn_ref[:, c0:c0 + cc], 0.0)
            zb = _shifts_back(z, tail, kw)
            for k in range(kw):
                ddw_ref[k, :, c0:c0 + cc] += _fold8(duv * zb[kw - 1 - k])
            df = _shifts_fwd(duv, head, kw)
            dz = df[0] * w_ref[kw - 1:kw, c0:c0 + cc]
            for j in range(1, kw):
                dz = dz + df[j] * w_ref[kw - 1 - j:kw - j, c0:c0 + cc]
            dp_ref[:, c0:c0 + cc] = (dz * sg).astype(BF16)
            dp_ref[:, d + c0:d + c0 + cc] = (dz * a * sg * (1.0 - sg)).astype(BF16)

    return pl.pallas_call(
        body, out_shape=(SDS((bsz, s_len, d2), BF16), SDS((32, 8, d), F32)), grid=(bsz, ns), name="conv_act_bwd2",
        in_specs=[pl.BlockSpec((None, ts, d), lambda bi, s: (bi, s, 0)),
                  pl.BlockSpec((None, CONV_HALO, d),
                               lambda bi, s: (bi, jnp.minimum((s + 1) * r, s_len // CONV_HALO - 1), 0)),
                  pl.BlockSpec((None, ts, d2), lambda bi, s: (bi, s, 0)),
                  pl.BlockSpec((None, CONV_HALO, d2), lambda bi, s: (bi, jnp.maximum(s * r - 1, 0), 0)),
                  pl.BlockSpec((32, d), lambda bi, s: (0, 0))],
        out_specs=(pl.BlockSpec((None, ts, d2), lambda bi, s: (bi, s, 0)),
                   pl.BlockSpec((32, 8, d), lambda bi, s: (0, 0, 0))),
        compiler_params=_cp("arbitrary", "arbitrary"))(du, du, p, p, dw32)


POOL_HALO = 16


def _pool_counts(s, ts, rows, w):
    t = s * ts + lax.broadcasted_iota(jnp.int32, (rows, 1), 0)
    return jnp.minimum(t + 1, w).astype(F32)


def _pool_fwd(x, gmix, w, b, scale):
    bsz, s_len, d = x.shape
    ng = len(POOL_WINDOWS)
    cg = d // ng
    ts = _tile(s_len, 512, POOL_HALO)
    r = ts // POOL_HALO

    def body(xc_ref, xp_ref, g_ref, w_ref, b_ref, sc_ref, y_ref, p_ref):
        s = pl.program_id(1)

        def norm(v):
            return v * lax.rsqrt(jnp.mean(v * v, axis=-1, keepdims=True) + EPS) * g_ref[...]

        xc = xc_ref[...]
        h = norm(xc)
        tail = jnp.where(s > 0, norm(xp_ref[...]), 0.0)
        for gi, win in enumerate(POOL_WINDOWS):
            lo, hi = gi * cg, (gi + 1) * cg
            hg = h[:, lo:hi]
            acc = jnp.concatenate([tail[:, lo:hi], hg], axis=0)
            step = 1
            while step < win:
                acc = acc + pltpu.roll(acc, step, axis=0)
                step *= 2
            pg = acc[POOL_HALO:] / _pool_counts(s, ts, ts, win) - hg
            pb = pg.astype(BF16)
            p_ref[:, lo:hi] = pb
            yg = jnp.dot(pb, w_ref[gi], preferred_element_type=F32) + b_ref[:, lo:hi]
            y_ref[:, lo:hi] = xc[:, lo:hi] + yg * sc_ref[:, lo:hi]

    vec = pl.BlockSpec((1, d), lambda bi, s: (0, 0))
    blk = pl.BlockSpec((None, ts, d), lambda bi, s: (bi, s, 0))
    return pl.pallas_call(
        body, out_shape=(SDS((bsz, s_len, d), F32), SDS((bsz, s_len, d), BF16)), grid=(bsz, s_len // ts),
        name="pool_fwd",
        in_specs=[blk, pl.BlockSpec((None, POOL_HALO, d), lambda bi, s: (bi, jnp.maximum(s * r - 1, 0), 0)),
                  vec, pl.BlockSpec((ng, cg, cg), lambda bi, s: (0, 0, 0)), vec, vec],
        out_specs=(blk, blk), compiler_params=_cp("parallel", "arbitrary"))(x, x, gmix, w, b, scale)


def _pool_bwd(x, dy, p, gmix, w, b, scale):
    bsz, s_len, d = x.shape
    ng = len(POOL_WINDOWS)
    cg = d // ng
    ts = _tile(s_len, 512, POOL_HALO)
    r, ns = ts // POOL_HALO, s_len // ts
    nt = (((1,), (1,)), ((), ()))
    tn = (((0,), (0,)), ((), ()))

    def body(x_ref, dy_ref, dyn_ref, p_ref, g_ref, w_ref, b_ref, sc_ref, dx_ref, dw_ref, db_ref, dsc_ref, dg_ref):
        bi, s = pl.program_id(0), pl.program_id(1)

        @pl.when((bi == 0) & (s == 0))
        def _():
            dw_ref[...] = jnp.zeros_like(dw_ref)
            db_ref[...] = jnp.zeros_like(db_ref)
            dsc_ref[...] = jnp.zeros_like(dsc_ref)
            dg_ref[...] = jnp.zeros_like(dg_ref)

        dy = dy_ref[...]
        dyy = dy * sc_ref[...]
        dyy_n = jnp.where(s < ns - 1, dyn_ref[...] * sc_ref[...], 0.0)
        db_ref[...] += _fold8(dyy)
        xv = x_ref[...]
        rr = lax.rsqrt(jnp.mean(xv * xv, axis=-1, keepdims=True) + EPS)
        xh = xv * rr
        for gi, win in enumerate(POOL_WINDOWS):
            lo, hi = gi * cg, (gi + 1) * cg
            pb = p_ref[:, lo:hi]
            wg = w_ref[gi]
            pre = jnp.dot(pb, wg, preferred_element_type=F32) + b_ref[:, lo:hi]
            dsc_ref[:, lo:hi] += _fold8(dy[:, lo:hi] * pre)
            dyb = dyy[:, lo:hi].astype(BF16)
            dw_ref[gi] += lax.dot_general(pb, dyb, tn, preferred_element_type=F32)
            dp = lax.dot_general(dyb, wg, nt, preferred_element_type=F32)
            dp_n = lax.dot_general(dyy_n[:, lo:hi].astype(BF16), wg, nt, preferred_element_type=F32)
            q = dp / _pool_counts(s, ts, ts, win)
            q_n = dp_n / _pool_counts(s + 1, ts, POOL_HALO, win)
            acc = jnp.concatenate([q, q_n], axis=0)
            ln = ts + POOL_HALO
            step = 1
            while step < win:
                acc = acc + pltpu.roll(acc, ln - step, axis=0)
                step *= 2
            dh = acc[:ts] - dp
            xhg = xh[:, lo:hi]
            dg_ref[:, lo:hi] += _fold8(dh * xhg)
            dx_ref[:, lo:hi] = dh * g_ref[:, lo:hi]
        u = dx_ref[...]
        dx_ref[...] = dy + rr * (u - xh * jnp.mean(u * xh, axis=-1, keepdims=True))

    vec = pl.BlockSpec((1, d), lambda bi, s: (0, 0))
    acc8 = pl.BlockSpec((8, d), lambda bi, s: (0, 0))
    blk = pl.BlockSpec((None, ts, d), lambda bi, s: (bi, s, 0))
    wspec = pl.BlockSpec((ng, cg, cg), lambda bi, s: (0, 0, 0))
    return pl.pallas_call(
        body, out_shape=(SDS((bsz, s_len, d), F32), SDS((ng, cg, cg), F32), SDS((8, d), F32), SDS((8, d), F32),
                         SDS((8, d), F32)),
        grid=(bsz, ns), name="pool_bwd",
        in_specs=[blk, blk,
                  pl.BlockSpec((None, POOL_HALO, d),
                               lambda bi, s: (bi, jnp.minimum((s + 1) * r, s_len // POOL_HALO - 1), 0)),
                  blk, vec, wspec, vec, vec],
        out_specs=(blk, wspec, acc8, acc8, acc8),
        compiler_params=_cp("arbitrary", "arbitrary"))(x, dy, dy, p, gmix, w, b, scale)


def _tri(n, upper):
    row = lax.broadcasted_iota(jnp.int32, (n, n), 0)
    col = lax.broadcasted_iota(jnp.int32, (n, n), 1)
    return jnp.where((col >= row) if upper else (col <= row), 1.0, 0.0).astype(F32)


def _fox_gate_fwd(proj, bf, n_heads):
    bsz, s_len, width = proj.shape
    col = width // LANES - 1
    ts = _tile(s_len, 512)

    def body(fl_ref, b_ref, c_ref, carry_ref):
        @pl.when(pl.program_id(1) == 0)
        def _():
            carry_ref[...] = jnp.zeros_like(carry_ref)

        xv = fl_ref[...] + b_ref[...]
        logf = jnp.minimum(xv, 0.0) - jnp.log(1.0 + jnp.exp(-jnp.abs(xv)))
        lane = lax.broadcasted_iota(jnp.int32, (1, LANES), 1)
        logf = jnp.where(lane < n_heads, logf, 0.0)
        c = _dot_hi(_tri(ts, False), logf) + carry_ref[0:1, :]
        c_ref[...] = c
        carry_ref[0:1, :] = c[ts - 1:ts, :]

    return pl.pallas_call(
        body, out_shape=SDS((bsz, s_len, LANES), F32), grid=(bsz, s_len // ts), name="fox_gate_fwd",
        in_specs=[pl.BlockSpec((None, ts, LANES), lambda bi, s: (bi, s, col)),
                  pl.BlockSpec((1, LANES), lambda bi, s: (0, 0))],
        out_specs=pl.BlockSpec((None, ts, LANES), lambda bi, s: (bi, s, 0)),
        scratch_shapes=[pltpu.VMEM((8, LANES), F32)],
        compiler_params=_cp("arbitrary", "arbitrary"))(proj, bf)


def _fox_gate_bwd(dc, proj, bf, n_heads):
    bsz, s_len, width = proj.shape
    col = width // LANES - 1
    ts = _tile(s_len, 512)
    ns = s_len // ts

    def body(dc_ref, fl_ref, b_ref, dfl_ref, db_ref, carry_ref):
        bi, s = pl.program_id(0), pl.program_id(1)

        @pl.when((bi == 0) & (s == 0))
        def _():
            db_ref[...] = jnp.zeros_like(db_ref)

        @pl.when(s == 0)
        def _():
            carry_ref[...] = jnp.zeros_like(carry_ref)

        dlogf = _dot_hi(_tri(ts, True), dc_ref[...]) + carry_ref[0:1, :]
        carry_ref[0:1, :] = dlogf[0:1, :]
        lane = lax.broadcasted_iota(jnp.int32, (1, LANES), 1)
        dfl = jnp.where(lane < n_heads, dlogf * (1.0 - _sigmoid(fl_ref[...] + b_ref[...])), 0.0)
        dfl_ref[...] = dfl.astype(BF16)
        db_ref[...] += _fold8(dfl)

    return pl.pallas_call(
        body, out_shape=(SDS((bsz, s_len, LANES), BF16), SDS((8, LANES), F32)), grid=(bsz, ns), name="fox_gate_bwd",
        in_specs=[pl.BlockSpec((None, ts, LANES), lambda bi, s: (bi, ns - 1 - s, 0)),
                  pl.BlockSpec((None, ts, LANES), lambda bi, s: (bi, ns - 1 - s, col)),
                  pl.BlockSpec((1, LANES), lambda bi, s: (0, 0))],
        out_specs=(pl.BlockSpec((None, ts, LANES), lambda bi, s: (bi, ns - 1 - s, 0)),
                   pl.BlockSpec((8, LANES), lambda bi, s: (0, 0))),
        scratch_shapes=[pltpu.VMEM((8, LANES), F32)],
        compiler_params=_cp("arbitrary", "arbitrary"))(dc, proj, bf)


def _head_maps(d):
    ch = lax.broadcasted_iota(jnp.int32, (d, LANES), 0) // HEAD_DIM
    hd = lax.broadcasted_iota(jnp.int32, (d, LANES), 1)
    e = jnp.where(ch == hd, 1.0, 0.0).astype(BF16)
    cht = lax.broadcasted_iota(jnp.int32, (LANES, d), 1) // HEAD_DIM
    hdt = lax.broadcasted_iota(jnp.int32, (LANES, d), 0)
    et = jnp.where(cht == hdt, 1.0, 0.0).astype(BF16)
    return e, et


def _dot_sel(x, e):
    a = x.astype(BF16)
    r = x - a.astype(F32)
    b = r.astype(BF16)
    c = (r - b.astype(F32)).astype(BF16)
    return (jnp.dot(a, e, preferred_element_type=F32) + jnp.dot(b, e, preferred_element_type=F32)
            + jnp.dot(c, e, preferred_element_type=F32))


def _fox_qknorm_fwd(proj, gq, gk, d):
    t = proj.shape[0]
    ts = _tile(t, 256)
    scale = 1.0 / math.sqrt(HEAD_DIM)

    def body(q_ref, k_ref, v_ref, gq_ref, gk_ref, qn_ref, kn_ref, vb_ref):
        e, et = _head_maps(d)

        def norm(v, g):
            r = lax.rsqrt(_dot_sel(v * v, e) / HEAD_DIM + EPS)
            return v * _dot_sel(r, et) * g

        qn_ref[...] = (norm(q_ref[...], gq_ref[...]) * scale).astype(BF16)
        kn_ref[...] = norm(k_ref[...], gk_ref[...]).astype(BF16)
        vb_ref[...] = v_ref[...].astype(BF16)

    def colblk(j):
        return pl.BlockSpec((ts, d), lambda i: (i, j))

    vec = pl.BlockSpec((1, d), lambda i: (0, 0))
    out = pl.BlockSpec((ts, d), lambda i: (i, 0))
    return pl.pallas_call(
        body, out_shape=(SDS((t, d), BF16),) * 3, grid=(t // ts,), name="fox_qknorm_fwd",
        in_specs=[colblk(0), colblk(1), colblk(2), vec, vec], out_specs=(out, out, out),
        compiler_params=_cp("parallel"))(proj, proj, proj, gq, gk)


def _fox_qknorm_bwd(proj, dq, dk, dv, gq, gk, d):
    t = proj.shape[0]
    ts = _tile(t, 256)
    scale = 1.0 / math.sqrt(HEAD_DIM)

    def body(q_ref, k_ref, dq_ref, dk_ref, dv_ref, gq_ref, gk_ref, dp_ref, dgq_ref, dgk_ref):
        @pl.when(pl.program_id(0) == 0)
        def _():
            dgq_ref[...] = jnp.zeros_like(dgq_ref)
            dgk_ref[...] = jnp.zeros_like(dgk_ref)

        e, et = _head_maps(d)

        def back(v, g, dn, dg_ref):
            r = _dot_sel(lax.rsqrt(_dot_sel(v * v, e) / HEAD_DIM + EPS), et)
            vh = v * r
            dg_ref[...] += _fold8(dn * vh)
            u = dn * g
            mh = _dot_sel(_dot_sel(u * vh, e) / HEAD_DIM, et)
            return r * (u - vh * mh)

        dp_ref[:, 0:d] = back(q_ref[...], gq_ref[...], dq_ref[...] * scale, dgq_ref).astype(BF16)
        dp_ref[:, d:2 * d] = back(k_ref[...], gk_ref[...], dk_ref[...], dgk_ref).astype(BF16)
        dp_ref[:, 2 * d:3 * d] = dv_ref[...]

    def colblk(j):
        return pl.BlockSpec((ts, d), lambda i: (i, j))

    row = pl.BlockSpec((ts, d), lambda i: (i, 0))
    vec = pl.BlockSpec((1, d), lambda i: (0, 0))
    acc = pl.BlockSpec((8, d), lambda i: (0, 0))
    return pl.pallas_call(
        body, out_shape=(SDS((t, 3 * d), BF16), SDS((8, d), F32), SDS((8, d), F32)), grid=(t // ts,),
        name="fox_qknorm_bwd", in_specs=[colblk(0), colblk(1), row, row, row, vec, vec],
        out_specs=(pl.BlockSpec((ts, 3 * d), lambda i: (i, 0)), acc, acc),
        compiler_params=_cp("arbitrary"))(proj, proj, dq, dk, dv, gq, gk)


ATT_BLOCK = 512
_NT = (((1,), (1,)), ((), ()))
_TN = (((0,), (0,)), ((), ()))


def _head_mask(h):
    return (lax.broadcasted_iota(jnp.int32, (1, LANES), 1) // HEAD_DIM) == h


def _causal(qi, ki, tq, tk):
    row = qi * tq + lax.broadcasted_iota(jnp.int32, (tq, 1), 0)
    col = ki * tk + lax.broadcasted_iota(jnp.int32, (1, tk), 1)
    return col <= row


def _direct_exchange(ins, outs, place, sems, gather):
    send_sems, recv_sems, local_sems = sems
    x, y, c, me = _mesh_place()
    copies = []
    for t in range(len(ins)):
        dst = outs[t].at[me] if gather else outs[place[t][0]].at[me, place[t][1]]
        copies.append(pltpu.make_async_copy(ins[t] if gather else ins[t].at[me], dst, local_sems.at[t]))
        for kbits in range(1, N_DEV):
            px = 1 - x if kbits & 4 else x
            py = 1 - y if kbits & 2 else y
            pc = 1 - c if kbits & 1 else c
            copies.append(pltpu.make_async_remote_copy(
                src_ref=ins[t] if gather else ins[t].at[4 * px + 2 * py + pc], dst_ref=dst,
                send_sem=send_sems.at[t, kbits - 1], recv_sem=recv_sems.at[t, kbits - 1],
                device_id=(px, py, pc), device_id_type=pl.DeviceIdType.MESH))
    return copies


def _exchange_scratch(n):
    return [pltpu.SemaphoreType.DMA((n, N_DEV - 1)), pltpu.SemaphoreType.DMA((n, N_DEV - 1)),
            pltpu.SemaphoreType.DMA((n,))]


def _flash_fwd(q, k, v, crow, gather=()):
    bsz, s_len, d = q.shape
    nj = d // LANES
    tq = tk = _tile(s_len, ATT_BLOCK, LANES)
    nq = s_len // tq
    ng = len(gather)

    pairs = [(a, b) for a in range(nq) for b in range(a + 1)]
    qtab = jnp.asarray([a for a, _ in pairs], jnp.int32)
    ktab = jnp.asarray([b for _, b in pairs], jnp.int32)

    def body(qtab_ref, ktab_ref, q_ref, k_ref, v_ref, c_ref, *rest):
        g_in, (o_ref, lse_ref), g_out = rest[:ng], rest[ng:ng + 2], rest[ng + 2:2 * ng + 2]
        m_ref, l_ref, acc_ref = rest[2 * ng + 2:2 * ng + 5]
        step_id = pl.program_id(2)
        qi, ki = qtab_ref[step_id], ktab_ref[step_id]
        if ng:
            sems = rest[2 * ng + 5:]
            outer = (pl.program_id(0), pl.program_id(1))

            @pl.when((outer[0] == 0) & (outer[1] == 0) & (step_id == 0))
            def _():
                for cp in _direct_exchange(g_in, g_out, None, sems, True):
                    cp.start()

            @pl.when((outer[0] == bsz - 1) & (outer[1] == nj - 1) & (step_id == len(pairs) - 1))
            def _():
                for cp in _direct_exchange(g_in, g_out, None, sems, True):
                    cp.wait()

        @pl.when(ki == 0)
        def _():
            m_ref[...] = jnp.full_like(m_ref, NEG)
            l_ref[...] = jnp.zeros_like(l_ref)
            acc_ref[...] = jnp.zeros_like(acc_ref)

        def step(masked):
            qv, kv, vv = q_ref[...], k_ref[...], v_ref[...]
            for h in range(2):
                qh = jnp.where(_head_mask(h), qv, jnp.zeros_like(qv))
                s = lax.dot_general(qh, kv, _NT, preferred_element_type=F32) - c_ref[h:h + 1, :]
                if masked:
                    s = jnp.where(_causal(qi, ki, tq, tk), s, NEG)
                m_prev = m_ref[h]
                m_new = jnp.maximum(m_prev, jnp.max(s, axis=1, keepdims=True))
                pm = jnp.exp(s - m_new)
                alpha = jnp.exp(m_prev - m_new)
                l_ref[h] = alpha * l_ref[h] + jnp.sum(pm, axis=1, keepdims=True)
                p_hi = pm.astype(BF16)
                p_lo = (pm - p_hi.astype(F32)).astype(BF16)
                acc_ref[h] = (alpha * acc_ref[h] + jnp.dot(p_hi, vv, preferred_element_type=F32)
                              + jnp.dot(p_lo, vv, preferred_element_type=F32))
                m_ref[h] = m_new

        pl.when(ki < qi)(functools.partial(step, False))
        pl.when(ki == qi)(functools.partial(step, True))

        @pl.when(ki == qi)
        def _():
            m0 = _head_mask(0)
            o_ref[...] = jnp.where(m0, acc_ref[0] / l_ref[0], acc_ref[1] / l_ref[1])
            lse_ref[...] = jnp.where(m0, m_ref[0] + jnp.log(l_ref[0]), m_ref[1] + jnp.log(l_ref[1]))

    qblk = pl.BlockSpec((None, tq, LANES), lambda bi, j, t, qt, kt: (bi, qt[t], j))
    kblk = pl.BlockSpec((None, tk, LANES), lambda bi, j, t, qt, kt: (bi, kt[t], j))
    hbm = pl.BlockSpec(memory_space=pl.ANY)
    outs = pl.pallas_call(
        body, out_shape=[SDS((bsz, s_len, d), F32), SDS((bsz, nj, s_len, LANES), F32)]
        + [SDS((N_DEV,) + tuple(a.shape), a.dtype) for a in gather], name="flash_fwd",
        grid_spec=pltpu.PrefetchScalarGridSpec(
            num_scalar_prefetch=2, grid=(bsz, nj, len(pairs)),
            in_specs=[qblk, kblk, kblk,
                      pl.BlockSpec((None, None, 2, tk), lambda bi, j, t, qt, kt: (bi, j, 0, kt[t]))] + [hbm] * ng,
            out_specs=[qblk, pl.BlockSpec((None, None, tq, LANES), lambda bi, j, t, qt, kt: (bi, j, qt[t], 0))]
            + [hbm] * ng,
            scratch_shapes=[pltpu.VMEM((2, tq, 1), F32), pltpu.VMEM((2, tq, 1), F32),
                            pltpu.VMEM((2, tq, LANES), F32)] + (_exchange_scratch(ng) if ng else [])),
        compiler_params=_cp("arbitrary", "arbitrary", "arbitrary"))(qtab, ktab, q, k, v, crow, *gather)
    return outs[0], outs[1], list(outs[2:])


def _flash_probs(qv, kv, vv, dov, ov, lse, c_ref, h, mask):
    hm = _head_mask(h)
    qh = jnp.where(hm, qv, jnp.zeros_like(qv))
    s = lax.dot_general(qh, kv, _NT, preferred_element_type=F32) - c_ref[h:h + 1, :]
    pm = jnp.exp(s - lse[:, h * HEAD_DIM:h * HEAD_DIM + 1])
    if mask is not None:
        pm = jnp.where(mask, pm, 0.0)
    doh = jnp.where(hm, dov, jnp.zeros_like(dov))
    dpm = lax.dot_general(doh, vv, _NT, preferred_element_type=F32)
    delta = jnp.sum(jnp.where(hm, dov.astype(F32) * ov, 0.0), axis=1, keepdims=True)
    return pm, pm * (dpm - delta)


def _flash_bwd(q, k, v, do, o, lse, crow, items=(), groups=()):
    bsz, s_len, d = q.shape
    nj = d // LANES
    tq = tk = _tile(s_len, ATT_BLOCK, LANES)
    nq = s_len // tq

    pairs = [(b, a) for b in range(nq) for a in range(b, nq)]
    n_live = len(pairs)
    ktab = jnp.asarray([b for b, _ in pairs] + [nq - 1] * nq, jnp.int32)
    qtab = jnp.asarray([a for _, a in pairs] + list(range(nq)), jnp.int32)

    n_it, n_grp = len(items), len(groups)
    place = {it: (g, l) for g, members in enumerate(groups) for l, it in enumerate(members)}

    def body(ktab_ref, qtab_ref, q_ref, k_ref, v_ref, do_ref, o_ref, lse_ref, c_ref, *rest):
        x_in, (dq_ref, dk_ref, dv_ref, dc_ref) = rest[:n_it], rest[n_it:n_it + 4]
        x_out = rest[n_it + 4:n_it + 4 + n_grp]
        dqa_ref, dka_ref, dva_ref, dca_ref = rest[n_it + 4 + n_grp:n_it + 8 + n_grp]
        step_id = pl.program_id(2)
        ki, qi = ktab_ref[step_id], qtab_ref[step_id]
        live = step_id < n_live
        rows = pl.ds(pl.multiple_of(qi * tq, tq), tq)
        if n_it:
            sems = rest[n_it + 8 + n_grp:]
            outer = (pl.program_id(0), pl.program_id(1))

            @pl.when((outer[0] == 0) & (outer[1] == 0) & (step_id == 0))
            def _():
                for cp in _direct_exchange(x_in, x_out, place, sems, False):
                    cp.start()

            @pl.when((outer[0] == bsz - 1) & (outer[1] == nj - 1) & (step_id == n_live + nq - 1))
            def _():
                for cp in _direct_exchange(x_in, x_out, place, sems, False):
                    cp.wait()

        @pl.when(step_id == 0)
        def _():
            dqa_ref[...] = jnp.zeros_like(dqa_ref)

        @pl.when(live & (qi == ki))
        def _():
            dka_ref[...] = jnp.zeros_like(dka_ref)
            dva_ref[...] = jnp.zeros_like(dva_ref)
            dca_ref[...] = jnp.zeros_like(dca_ref)

        def step(masked):
            qv, kv, vv, dov, ov, lse = q_ref[...], k_ref[...], v_ref[...], do_ref[...], o_ref[...], lse_ref[...]
            mask = _causal(qi, ki, tq, tk) if masked else None
            for h in range(2):
                pm, ds = _flash_probs(qv, kv, vv, dov, ov, lse, c_ref, h, mask)
                dsb = ds.astype(BF16)
                dva_ref[h] += lax.dot_general(pm.astype(BF16), dov, _TN, preferred_element_type=F32)
                dka_ref[h] += lax.dot_general(dsb, qv, _TN, preferred_element_type=F32)
                dqa_ref[h, rows, :] += jnp.dot(dsb, kv, preferred_element_type=F32)
                dca_ref[h:h + 1, :] -= jnp.sum(ds, axis=0, keepdims=True)

        pl.when(live & (qi > ki))(functools.partial(step, False))
        pl.when(live & (qi == ki))(functools.partial(step, True))

        @pl.when(live & (qi == nq - 1))
        def _():
            m0 = _head_mask(0)
            dk_ref[...] = jnp.where(m0, dka_ref[0], dka_ref[1])
            dv_ref[...] = jnp.where(m0, dva_ref[0], dva_ref[1]).astype(BF16)
            dc_ref[...] = dca_ref[0:2, :]

        @pl.when(jnp.logical_not(live))
        def _():
            dq_ref[...] = jnp.where(_head_mask(0), dqa_ref[0, rows, :], dqa_ref[1, rows, :])

    def qside(bi, j, t, kt, qt):
        return (bi, jnp.where(t < n_live, qt[t], nq - 1), j)

    def kside(bi, j, t, kt, qt):
        return (bi, kt[t], j)

    def dqside(bi, j, t, kt, qt):
        return (bi, jnp.where(t < n_live, 0, qt[t]), j)

    qblk, kblk = pl.BlockSpec((None, tq, LANES), qside), pl.BlockSpec((None, tk, LANES), kside)
    cblk = pl.BlockSpec((None, None, 2, tk), lambda bi, j, t, kt, qt: (bi, j, 0, kt[t]))
    hbm = pl.BlockSpec(memory_space=pl.ANY)
    outs = pl.pallas_call(
        body, out_shape=[SDS((bsz, s_len, d), F32), SDS((bsz, s_len, d), F32), SDS((bsz, s_len, d), BF16),
                         SDS((bsz, nj, 2, s_len), F32)]
        + [SDS((N_DEV, len(members)) + tuple(items[members[0]].shape[1:]), items[members[0]].dtype)
           for members in groups], name="flash_bwd",
        grid_spec=pltpu.PrefetchScalarGridSpec(
            num_scalar_prefetch=2, grid=(bsz, nj, n_live + nq),
            in_specs=[qblk, kblk, kblk, qblk, qblk,
                      pl.BlockSpec((None, None, tq, LANES),
                                   lambda bi, j, t, kt, qt: (bi, j, jnp.where(t < n_live, qt[t], nq - 1), 0)),
                      cblk] + [hbm] * n_it,
            out_specs=[pl.BlockSpec((None, tq, LANES), dqside), kblk, kblk, cblk] + [hbm] * n_grp,
            scratch_shapes=[pltpu.VMEM((2, s_len, LANES), F32), pltpu.VMEM((2, tk, LANES), F32),
                            pltpu.VMEM((2, tk, LANES), F32), pltpu.VMEM((8, tk), F32)]
            + (_exchange_scratch(n_it) if n_it else [])),
        compiler_params=_cp("arbitrary", "arbitrary", "arbitrary"))(ktab, qtab, q, k, v, do, o, lse, crow, *items)
    return outs[0], outs[1], outs[2], outs[3], list(outs[4:])


def _loss_head(y, target):
    t, d = y.shape
    tm = _tile(t, 512)

    def body(y_ref, t_ref, dy_ref, acc_ref):
        @pl.when(pl.program_id(0) == 0)
        def _():
            acc_ref[...] = jnp.zeros_like(acc_ref)

        err = y_ref[...] - t_ref[...]
        dy_ref[...] = err / d
        acc_ref[...] += _fold8(err * err)

    row = pl.BlockSpec((tm, d), lambda i: (i, 0))
    return pl.pallas_call(
        body, out_shape=(SDS((t, d), F32), SDS((8, d), F32)), grid=(t // tm,), name="loss_head",
        in_specs=[row, row], out_specs=(row, pl.BlockSpec((8, d), lambda i: (0, 0))),
        compiler_params=_cp("arbitrary"))(y, target)


ADAM_COLS = 1024


def _adamw(g8, w, m, v):
    shape = w.shape
    cols = shape[-1]
    rows = w.size // cols
    n_parts = g8.shape[0]
    g8, w, m, v = g8.reshape(n_parts, rows, cols), w.reshape(rows, cols), m.reshape(rows, cols), v.reshape(rows, cols)
    tr = _tile(rows, 256, 16)
    c1 = 1.0 - ADAM_B1 ** ADAM_STEP
    c2 = 1.0 - ADAM_B2 ** ADAM_STEP

    def body(g8_ref, w_ref, m_ref, v_ref, g_ref, d_ref, nm_ref, nv_ref):
        g = g8_ref[0].astype(F32)
        for i in range(1, n_parts):
            g = g + g8_ref[i].astype(F32)
        mn = ADAM_B1 * m_ref[...] + (1.0 - ADAM_B1) * g
        vn = ADAM_B2 * v_ref[...] + (1.0 - ADAM_B2) * (g * g)
        g_ref[...] = g
        nm_ref[...] = mn
        nv_ref[...] = vn
        d_ref[...] = -ADAM_LR * ((mn / c1) / (jnp.sqrt(vn / c2) + ADAM_EPS) + ADAM_WD * w_ref[...])

    blk = pl.BlockSpec((tr, cols), lambda i: (i, 0))
    outs = pl.pallas_call(
        body, out_shape=(SDS((rows, cols), F32),) * 4, grid=(rows // tr,), name="adamw",
        in_specs=[pl.BlockSpec((n_parts, tr, cols), lambda i: (0, i, 0)), blk, blk, blk], out_specs=(blk,) * 4,
        compiler_params=_cp("parallel"))(g8, w, m, v)
    return [o.reshape(shape) for o in outs]


def _mesh_place():
    x, y, c = lax.axis_index("x"), lax.axis_index("y"), lax.axis_index("c")
    return x, y, c, 4 * x + 2 * y + c


def _gather(shards):
    n = len(shards)

    def body(*refs):
        ins, outs = refs[:n], refs[n:2 * n]
        send_sems, recv_sems, local_sems = refs[2 * n:]
        x, y, c, me = _mesh_place()
        sibling = (x, y, 1 - c)
        chips = [(1 - x, y), (x, 1 - y), (1 - x, 1 - y)]

        def block(px, py, pc):
            return 4 * px + 2 * py + pc

        def copy(t, k, blk, to, src=None):
            return pltpu.make_async_remote_copy(
                src_ref=outs[t].at[blk] if src is None else src, dst_ref=outs[t].at[blk],
                send_sem=send_sems.at[t, k], recv_sem=recv_sems.at[t, k], device_id=to,
                device_id_type=pl.DeviceIdType.MESH)

        own = [pltpu.make_async_copy(ins[t], outs[t].at[me], local_sems.at[t]) for t in range(n)]
        first = []
        for t in range(n):
            own[t].start()
            first.append(copy(t, 0, me, sibling, src=ins[t]))
            first += [copy(t, 1 + j, me, (*chip, c), src=ins[t]) for j, chip in enumerate(chips)]
        for cp in first:
            cp.start()
        passed = []
        for j, chip in enumerate(chips):
            for t in range(n):
                copy(t, 1 + j, block(*chip, c), (x, y, c)).wait_recv()
                cp = copy(t, 4 + j, block(*chip, c), sibling)
                cp.start()
                passed.append(cp)
        for t in range(n):
            copy(t, 0, block(x, y, 1 - c), (x, y, c)).wait_recv()
            for j, chip in enumerate(chips):
                copy(t, 4 + j, block(*chip, 1 - c), (x, y, c)).wait_recv()
        for cp in first + passed:
            cp.wait_send()
        for cp in own:
            cp.wait()

    hbm = pl.BlockSpec(memory_space=pl.ANY)
    return pl.pallas_call(
        body, out_shape=[SDS((N_DEV,) + tuple(s.shape), s.dtype) for s in shards], name="gather",
        in_specs=[hbm] * n, out_specs=[hbm] * n,
        scratch_shapes=[pltpu.SemaphoreType.DMA((n, N_DEV - 1)), pltpu.SemaphoreType.DMA((n, N_DEV - 1)),
                        pltpu.SemaphoreType.DMA((n,))])(*shards)


N_CHIP = N_DEV // 2


def _scatter_core(items):
    n = len(items)

    def body(*refs):
        ins, outs = refs[:n], refs[n:2 * n]
        send_sems, recv_sems = refs[2 * n:]
        x, y, c, _ = _mesh_place()
        copies = []
        for it in range(n):
            for r in range(N_CHIP):
                cp = pltpu.make_async_remote_copy(
                    src_ref=ins[it].at[2 * r + 1 - c], dst_ref=outs[it].at[r], send_sem=send_sems.at[it, r],
                    recv_sem=recv_sems.at[it, r], device_id=(x, y, 1 - c), device_id_type=pl.DeviceIdType.MESH)
                cp.start()
                copies.append(cp)
        for cp in copies:
            cp.wait()

    hbm = pl.BlockSpec(memory_space=pl.ANY)
    return pl.pallas_call(
        body, out_shape=[SDS((N_CHIP,) + tuple(a.shape[1:]), a.dtype) for a in items], name="scatter_core",
        in_specs=[hbm] * n, out_specs=[hbm] * n,
        scratch_shapes=[pltpu.SemaphoreType.DMA((n, N_CHIP)), pltpu.SemaphoreType.DMA((n, N_CHIP))])(*items)


def _pair_add(item, other):
    shape = item.shape[1:]
    cols = shape[-1]
    rows = math.prod(shape) // cols
    tr = _tile(rows, 512, 16)

    def body(x_ref, o_ref, h_ref):
        c = lax.axis_index("c")
        mine = jnp.where(c == 0, x_ref[0].astype(F32), x_ref[1].astype(F32))
        h_ref[...] = (mine + o_ref[...].astype(F32)).astype(item.dtype)

    return pl.pallas_call(
        body, out_shape=SDS((N_CHIP, rows, cols), item.dtype), grid=(N_CHIP, rows // tr), name="pair_add",
        in_specs=[pl.BlockSpec((None, 2, tr, cols), lambda r, i: (r, 0, i, 0)),
                  pl.BlockSpec((None, tr, cols), lambda r, i: (r, i, 0))],
        out_specs=pl.BlockSpec((None, tr, cols), lambda r, i: (r, i, 0)),
        compiler_params=_cp("parallel", "parallel"))(
            item.reshape(N_CHIP, 2, rows, cols), other.reshape(N_CHIP, rows, cols)).reshape((N_CHIP,) + shape)


def _scatter_chip(items, groups):
    n = len(items)
    place = {it: (g, l) for g, members in enumerate(groups) for l, it in enumerate(members)}

    def body(*refs):
        ins, outs = refs[:n], refs[n:n + len(groups)]
        send_sems, recv_sems, local_sems = refs[n + len(groups):]
        x, y, c, _ = _mesh_place()
        chip = 2 * x + y
        copies = []
        for it in range(n):
            g, l = place[it]
            own = pltpu.make_async_copy(ins[it].at[chip], outs[g].at[chip, l], local_sems.at[it])
            own.start()
            copies.append(own)
            for kbits in range(1, N_CHIP):
                px = 1 - x if kbits & 2 else x
                py = 1 - y if kbits & 1 else y
                cp = pltpu.make_async_remote_copy(
                    src_ref=ins[it].at[2 * px + py], dst_ref=outs[g].at[chip, l],
                    send_sem=send_sems.at[it, kbits - 1], recv_sem=recv_sems.at[it, kbits - 1],
                    device_id=(px, py, c), device_id_type=pl.DeviceIdType.MESH)
                cp.start()
                copies.append(cp)
        for cp in copies:
            cp.wait()

    hbm = pl.BlockSpec(memory_space=pl.ANY)
    out_shape = [SDS((N_CHIP, len(members)) + tuple(items[members[0]].shape[1:]), items[members[0]].dtype)
                 for members in groups]
    return pl.pallas_call(
        body, out_shape=out_shape, name="scatter_chip", in_specs=[hbm] * n, out_specs=[hbm] * len(groups),
        scratch_shapes=[pltpu.SemaphoreType.DMA((n, N_CHIP - 1)), pltpu.SemaphoreType.DMA((n, N_CHIP - 1)),
                        pltpu.SemaphoreType.DMA((n,))])(*items)


def _scatter(items, groups):
    halves = _scatter_core(items)
    return _scatter_chip([_pair_add(a, h) for a, h in zip(items, halves)], groups)


def _cat_lanes(g, layer, nb, blk, width):
    _, _, rows, c = g.shape
    tr = _tile(rows, 256, 16)

    def body(g_ref, o_ref):
        for p in range(nb):
            o_ref[:, p * c:(p + 1) * c] = g_ref[p]
        if width > nb * c:
            o_ref[:, nb * c:] = jnp.zeros((tr, width - nb * c), g.dtype)

    return pl.pallas_call(
        body, out_shape=SDS((rows, width), g.dtype), grid=(rows // tr,), name="cat_lanes",
        in_specs=[pl.BlockSpec((nb, None, tr, c), lambda i: (blk, layer, i, 0))],
        out_specs=pl.BlockSpec((tr, width), lambda i: (i, 0)),
        compiler_params=_cp("parallel"))(g)


def _split_lanes(parts, c):
    rows = parts[0].shape[0]
    counts = [p.shape[1] // c for p in parts]
    tr = _tile(rows, 256, 16)

    def body(*refs):
        o_ref = refs[-1]
        q = 0
        for x_ref, cnt in zip(refs[:-1], counts):
            for p in range(cnt):
                o_ref[q] = x_ref[:, p * c:(p + 1) * c]
                q += 1

    return pl.pallas_call(
        body, out_shape=SDS((sum(counts), rows, c), parts[0].dtype), grid=(rows // tr,), name="split_lanes",
        in_specs=[pl.BlockSpec((tr, p.shape[1]), lambda i: (i, 0)) for p in parts],
        out_specs=pl.BlockSpec((sum(counts), tr, c), lambda i: (0, i, 0)),
        compiler_params=_cp("parallel"))(*parts)


def _unshard(g8, shard_shape, axis):
    full = jnp.moveaxis(g8.reshape((N_DEV,) + tuple(shard_shape)), 0, axis)
    shape = list(shard_shape)
    shape[axis] *= N_DEV
    return full.reshape(shape)


def _to_shards(full, axis):
    shape = list(full.shape)
    shape[axis:axis + 1] = [N_DEV, shape[axis] // N_DEV]
    return jnp.moveaxis(full.reshape(shape), axis, 0).reshape(N_DEV, -1)


SMALL = [n for n in SHARDED if n not in MATRICES]


def _flat_rows(parts):
    flat = jnp.concatenate([p.reshape(-1) for p in parts])
    chunk = 8 * ADAM_COLS
    n = -(-flat.shape[0] // chunk) * chunk
    return jnp.pad(flat, (0, n - flat.shape[0])).reshape(n // ADAM_COLS, ADAM_COLS)


def _prepare_vectors(small, shards):
    wt = {}
    flat = small.reshape(N_DEV, -1)
    off = 0
    for n in SMALL:
        size = shards[n].size
        wt[n] = _unshard(flat[:, off:off + size], shards[n].shape, SHARD_AXIS[n])
        off += size
    return wt


def _prepare_matrices(gathered):
    wt = {}
    for n in ('conv_w_out', 'fox_w_o', 'ffn_w_down'):
        if n in gathered:
            g = gathered[n]
            wt[n] = [g[:, l].reshape(N_DEV * g.shape[2], g.shape[3]) for l in range(g.shape[1])]
    if 'pool_w' in gathered:
        g = gathered['pool_w']
        wt['pool_w'] = [jnp.moveaxis(g[:, l], 0, 1).reshape(g.shape[2], N_DEV * g.shape[3], g.shape[4])
                        for l in range(g.shape[1])]
    if 'conv_w_in' in gathered:
        g = gathered['conv_w_in']
        wt['conv_w_in'] = [_cat_lanes(g, l, N_DEV, 0, N_DEV * g.shape[3]) for l in range(g.shape[1])]
    if 'fox_w_in' in gathered:
        g = gathered['fox_w_in']
        wt['fox_w_in'] = [_cat_lanes(g, l, N_DEV, 0, 3 * g.shape[2] + LANES) for l in range(g.shape[1])]
    if 'ffn_w_up' in gathered:
        g = gathered['ffn_w_up']
        half = N_DEV // 2
        wt['ffn_w_up_v'] = [_cat_lanes(g, l, half, 0, half * g.shape[3]) for l in range(g.shape[1])]
        wt['ffn_w_up_g'] = [_cat_lanes(g, l, half, 1, half * g.shape[3]) for l in range(g.shape[1])]
    return wt


def _pad_rows(w, rows):
    return jnp.pad(w, ((0, rows - w.shape[0]), (0, 0)))


def _fold(acc):
    return acc.sum(axis=0)


def _local_step(x, target, wt, late_shards=None, cut=None):
    wt = dict(wt)
    late_names = [n for n in MATRICES if late_shards and n in late_shards]
    late_recv = {}
    bsz, s_len, d = x.shape
    t = bsz * s_len
    depth = wt['norm_mix'].shape[0]
    n_heads = d // HEAD_DIM
    f = wt['ffn_w_up_v'][0].shape[1]
    row = lambda a: a.reshape(1, -1)
    grads = {n: {} for n in WEIGHTS}
    saved = []

    xc = x.reshape(t, d)
    hn_next = None
    for i in range(depth):
        j = i // 3
        kind = i % 3
        sv = {'x_mix': xc}
        gm = row(wt['norm_mix'][i])
        gf = row(wt['norm_ffn'][i])
        if kind == 0:
            hn = hn_next if hn_next is not None else _rmsnorm_fwd(xc, gm)
            p = _mm(hn, wt['conv_w_in'][j], bias=row(wt['conv_b_in'][j]), name="conv_in")
            u, sact = _conv_act_fwd(p.reshape(bsz, s_len, 2 * d), _pad_rows(wt['conv_dw'][j], 32),
                                    row(wt['conv_dw_b'][j]), row(wt['conv_ln_g'][j]), row(wt['conv_ln_b'][j]))
            sact = sact.reshape(t, d)
            xn, hf = _mm(sact, wt['conv_w_out'][j], bias=row(wt['conv_b_out'][j]), residual=xc, norm_gain=gf,
                         out_dtype=F32, name="conv_out")
            sv.update(hn=hn, p=p, u=u.reshape(t, d), sact=sact)
        elif kind == 1:
            xn, pp = _pool_fwd(xc.reshape(bsz, s_len, d), gm, wt['pool_w'][j], row(wt['pool_b'][j]),
                               row(wt['pool_scale'][j]))
            xn = xn.reshape(t, d)
            hf = _rmsnorm_fwd(xn, gf)
            sv.update(p=pp)
        else:
            hn = hn_next if hn_next is not None else _rmsnorm_fwd(xc, gm)
            wp = wt['fox_w_in'][j]
            bf = jnp.pad(wt['fox_b_f'][j], (0, LANES - n_heads)).reshape(1, LANES)
            gq = jnp.tile(wt['fox_q_gain'][j], n_heads).reshape(1, d)
            gk = jnp.tile(wt['fox_k_gain'][j], n_heads).reshape(1, d)
            proj = _mm(hn, wp, out_dtype=F32, name="fox_in")
            c = _fox_gate_fwd(proj.reshape(bsz, s_len, -1), bf, n_heads)
            crow = jnp.swapaxes(c, 1, 2)[:, :n_heads].reshape(bsz, n_heads // 2, 2, s_len)
            qn, kn, vb = _fox_qknorm_fwd(proj, gq, gk, d)
            shp = (bsz, s_len, d)
            o, lse, got = _flash_fwd(qn.reshape(shp), kn.reshape(shp), vb.reshape(shp), crow,
                                     gather=[late_shards[n] for n in late_names])
            for key, layers in _prepare_matrices(dict(zip(late_names, got))).items():
                wt[key] = wt[key] + layers
            o = o.reshape(t, d)
            xn, hf = _mm(o, wt['fox_w_o'][j], residual=xc, norm_gain=gf, out_dtype=F32, name="fox_out")
            sv.update(hn=hn, wp=wp, bf=bf, gq=gq, gk=gk, proj=proj, crow=crow, qn=qn, kn=kn, vb=vb, o=o, lse=lse)
        xc = xn
        sv['x_ffn'] = xc
        shf = (bsz, s_len, f)
        uv = _mm(hf, wt['ffn_w_up_v'][i], name="ffn_up").reshape(shf)
        ug = _mm(hf, wt['ffn_w_up_g'][i], name="ffn_up").reshape(shf)
        dw8 = _pad_rows(wt['ffn_dw'][i], 8)
        af, vv, vg = _ffn_act_fwd(uv, ug, dw8, row(wt['ffn_dw_b'][i]))
        af = af.reshape(t, f)
        if i + 1 < depth and (i + 1) % 3 != 1:
            xc, hn_next = _mm(af, wt['ffn_w_down'][i], residual=xc, norm_gain=row(wt['norm_mix'][i + 1]),
                              out_dtype=F32, name="ffn_down")
        else:
            xc, hn_next = _mm(af, wt['ffn_w_down'][i], residual=xc, out_dtype=F32, name="ffn_down"), None
        sv.update(hf=hf, uv=uv, ug=ug, vv=vv, vg=vg, af=af, dw8=dw8)
        saved.append(sv)

    dx, sq = _loss_head(xc, target.reshape(t, d))

    for i in reversed(range(depth)):
        j = i // 3
        kind = i % 3
        sv = saved[i]
        shf = (bsz, s_len, f)
        da = _mm(dx, wt['ffn_w_down'][i], trans_b=True, name="ffn_down_dgrad")
        gw, _ = _wgrad(sv['af'], dx, name="ffn_down_wgrad")
        grads['ffn_w_down'][i] = gw.reshape(N_DEV, f // N_DEV, d)
        dvv, dvg, dbv, dbg = _ffn_act_bwd1(sv['vv'], sv['vg'], da.reshape(shf))
        grads['ffn_dw_b'][i] = jnp.concatenate([_fold(dbv), _fold(dbg)])
        duv, dug, ddwv, ddwg = _ffn_act_bwd2(dvv, dvg, sv['uv'], sv['ug'], sv['dw8'])
        grads['ffn_dw'][i] = jnp.concatenate([ddwv.sum(axis=1), ddwg.sum(axis=1)], axis=1)
        duv, dug = duv.reshape(t, f), dug.reshape(t, f)
        gv, _ = _wgrad(sv['hf'], duv, name="ffn_up_wgrad")
        gg, _ = _wgrad(sv['hf'], dug, name="ffn_up_wgrad")
        grads['ffn_w_up'][i] = _split_lanes([gv, gg], 2 * f // N_DEV)
        dx, dg = _mm(duv, wt['ffn_w_up_v'][i], trans_b=True, a2=dug, b2=wt['ffn_w_up_g'][i],
                     norm_bwd=(sv['x_ffn'], row(wt['norm_ffn'][i]), dx), name="ffn_up_dgrad")
        grads['norm_ffn'][i] = _fold(dg)
        gm = row(wt['norm_mix'][i])
        if kind == 0:
            dsact = _mm(dx, wt['conv_w_out'][j], trans_b=True, name="conv_out_dgrad")
            gw, cs = _wgrad(sv['sact'], dx, name="conv_out_wgrad")
            grads['conv_w_out'][j] = gw.reshape(N_DEV, d // N_DEV, d)
            grads['conv_b_out'][j] = _fold(cs)
            du, dlg, dlb, dwb = _conv_act_bwd1(sv['u'], dsact, row(wt['conv_ln_g'][j]), row(wt['conv_ln_b'][j]))
            grads['conv_ln_g'][j], grads['conv_ln_b'][j], grads['conv_dw_b'][j] = _fold(dlg), _fold(dlb), _fold(dwb)
            dp, ddw = _conv_act_bwd2(du.reshape(bsz, s_len, d), sv['p'].reshape(bsz, s_len, 2 * d),
                                     _pad_rows(wt['conv_dw'][j], 32))
            grads['conv_dw'][j] = ddw.sum(axis=1)[:wt['conv_dw'].shape[1]]
            dp = dp.reshape(t, 2 * d)
            gw, cs = _wgrad(sv['hn'], dp, name="conv_in_wgrad")
            grads['conv_w_in'][j] = _split_lanes([gw], 2 * d // N_DEV)
            grads['conv_b_in'][j] = _fold(cs)
            dx, dg = _mm(dp, wt['conv_w_in'][j], trans_b=True, norm_bwd=(sv['x_mix'], gm, dx), name="conv_in_dgrad")
            grads['norm_mix'][i] = _fold(dg)
        elif kind == 1:
            shp = (bsz, s_len, d)
            dxn, dwp, dbp, dsc, dg = _pool_bwd(sv['x_mix'].reshape(shp), dx.reshape(shp), sv['p'], gm, wt['pool_w'][j],
                                               row(wt['pool_b'][j]), row(wt['pool_scale'][j]))
            dx = dxn.reshape(t, d)
            ng, cg = dwp.shape[0], dwp.shape[1]
            grads['pool_w'][j] = jnp.moveaxis(dwp.reshape(ng, N_DEV, cg // N_DEV, cg), 1, 0).astype(BF16)
            grads['pool_b'][j] = _fold(dbp).reshape(wt['pool_b'].shape[1:])
            grads['pool_scale'][j] = _fold(dsc)
            grads['norm_mix'][i] = _fold(dg)
        else:
            shp = (bsz, s_len, d)
            do = _mm(dx, wt['fox_w_o'][j], trans_b=True, name="fox_out_dgrad")
            gw, _ = _wgrad(sv['o'], dx, name="fox_out_wgrad")
            grads['fox_w_o'][j] = gw.reshape(N_DEV, d // N_DEV, d)
            fl_args = (sv['qn'].reshape(shp), sv['kn'].reshape(shp), sv['vb'].reshape(shp), do.reshape(shp),
                       sv['o'].reshape(shp), sv['lse'], sv['crow'])
            items, groups = [], []
            for n in late_names:
                members = [grads[n][l] for l in sorted(grads[n]) if l >= cut[n]]
                groups.append(list(range(len(items), len(items) + len(members))))
                items += members
            dq, dk, dv, dcrow, got = _flash_bwd(*fl_args, items=items, groups=groups)
            late_recv = dict(zip(late_names, got))
            dc = jnp.swapaxes(dcrow.reshape(bsz, n_heads, s_len), 1, 2)
            dc = jnp.pad(dc, ((0, 0), (0, 0), (0, LANES - n_heads)))
            dfl, dbf = _fox_gate_bwd(dc, sv['proj'].reshape(bsz, s_len, -1), sv['bf'], n_heads)
            grads['fox_b_f'][j] = _fold(dbf)[:n_heads]
            dqkv, dgq, dgk = _fox_qknorm_bwd(sv['proj'], dq.reshape(t, d), dk.reshape(t, d), dv.reshape(t, d),
                                             sv['gq'], sv['gk'], d)
            grads['fox_q_gain'][j] = _fold(dgq).reshape(n_heads, HEAD_DIM).sum(axis=0)
            grads['fox_k_gain'][j] = _fold(dgk).reshape(n_heads, HEAD_DIM).sum(axis=0)
            dproj = jnp.concatenate([dqkv, dfl.reshape(t, LANES)], axis=1)
            dwp, _ = _wgrad(sv['hn'], dproj, name="fox_in_wgrad")
            grads['fox_w_in'][j] = _split_lanes([dwp], (3 * d + n_heads) // N_DEV)
            dx, dg = _mm(dproj, sv['wp'], trans_b=True, norm_bwd=(sv['x_mix'], gm, dx), name="fox_in_dgrad")
            grads['norm_mix'][i] = _fold(dg)

    small = {n: jnp.stack([g[k] for k in sorted(g)]) for n, g in grads.items() if n not in MATRICES}
    big = {n: [grads[n][k] for k in sorted(grads[n]) if n not in late_recv or k < cut[n]] for n in MATRICES}
    return sq.sum(), dx.reshape(bsz, s_len, d), small, big, late_recv


def _train_step(x, target, w, m, v):
    depth = w['norm_mix'].shape[0]
    attn = [i for i in range(depth) if i % 3 == 2]
    cut = {n: w[n].shape[0] for n in MATRICES}
    if len(attn) == 1:
        cut['ffn_w_up'] = cut['ffn_w_down'] = attn[0]
        later_conv = [i // 3 for i in range(attn[0] + 1, depth) if i % 3 == 0]
        if later_conv:
            cut['conv_w_in'] = cut['conv_w_out'] = later_conv[0]
    late_shards = {n: w[n][cut[n]:].astype(BF16) for n in MATRICES if 0 < cut[n] < w[n].shape[0]}
    got = _gather([w[n][:cut[n]].astype(BF16) for n in MATRICES] + [_flat_rows([w[n] for n in SMALL])])
    wt = _prepare_matrices(dict(zip(MATRICES, got[:-1])))
    wt.update(_prepare_vectors(got[-1], w))
    wt.update({n: w[n] for n in REPLICATED})
    sq, grad_x, gsmall, gbig, late_recv = _local_step(x, target, wt, late_shards, cut)
    d = x.shape[-1]

    shard_rows = jnp.concatenate([_to_shards(gsmall[n], SHARD_AXIS[n]) for n in SMALL], axis=1)
    rep = jnp.concatenate([gsmall[n].reshape(-1) for n in REPLICATED] + [(0.5 / d) * sq.reshape(1)])
    rows = jnp.concatenate([shard_rows, jnp.broadcast_to(rep, (N_DEV, rep.shape[0]))], axis=1)
    chunk = 8 * ADAM_COLS
    n_all = rows.shape[1]
    n_pad = -(-n_all // chunk) * chunk
    rows = jnp.pad(rows, ((0, 0), (0, n_pad - n_all))).reshape(N_DEV, n_pad // ADAM_COLS, ADAM_COLS)

    items, groups = [], []
    for n in MATRICES:
        groups.append(list(range(len(items), len(items) + len(gbig[n]))))
        items += gbig[n]
    groups.append([len(items)])
    items.append(rows)
    recv = _scatter(items, groups)

    res = [{}, {}, {}, {}]
    for n, r in zip(MATRICES, recv[:-1]):
        if n in late_recv:
            c = cut[n]
            outs = zip(_adamw(r, w[n][:c], m[n][:c], v[n][:c]), _adamw(late_recv[n], w[n][c:], m[n][c:], v[n][c:]))
            outs = [jnp.concatenate(pair, axis=0) for pair in outs]
        else:
            outs = _adamw(r, w[n], m[n], v[n])
        for k, o in enumerate(outs):
            res[k][n] = o
    order = SMALL + REPLICATED

    def flat(tree):
        parts = jnp.concatenate([tree[n].reshape(-1) for n in order])
        return jnp.pad(parts, (0, n_pad - parts.shape[0])).reshape(n_pad // ADAM_COLS, ADAM_COLS)

    outs = [o.reshape(-1) for o in _adamw(recv[-1].reshape((N_CHIP,) + rows.shape[1:]), flat(w), flat(m), flat(v))]
    off = 0
    for n in order:
        size = w[n].size
        for k in range(4):
            res[k][n] = outs[k][off:off + size].reshape(w[n].shape)
        off += size
    loss = outs[0][n_all - 1]
    return (loss, grad_x, *[res[0][n] for n in WEIGHTS], *[res[1][n] for n in WEIGHTS],
            *[res[2][n] for n in WEIGHTS], *[res[3][n] for n in WEIGHTS])


def kernel(x, norm_mix, norm_ffn, conv_w_in, conv_b_in, conv_dw, conv_dw_b, conv_ln_g, conv_ln_b, conv_w_out, conv_b_out, pool_w, pool_b, pool_scale, fox_w_in, fox_b_f, fox_q_gain, fox_k_gain, fox_w_o, ffn_w_up, ffn_dw, ffn_dw_b, ffn_w_down, loss_target, m_norm_mix, m_norm_ffn, m_conv_w_in, m_conv_b_in, m_conv_dw, m_conv_dw_b, m_conv_ln_g, m_conv_ln_b, m_conv_w_out, m_conv_b_out, m_pool_w, m_pool_b, m_pool_scale, m_fox_w_in, m_fox_b_f, m_fox_q_gain, m_fox_k_gain, m_fox_w_o, m_ffn_w_up, m_ffn_dw, m_ffn_dw_b, m_ffn_w_down, v_norm_mix, v_norm_ffn, v_conv_w_in, v_conv_b_in, v_conv_dw, v_conv_dw_b, v_conv_ln_g, v_conv_ln_b, v_conv_w_out, v_conv_b_out, v_pool_w, v_pool_b, v_pool_scale, v_fox_w_in, v_fox_b_f, v_fox_q_gain, v_fox_k_gain, v_fox_w_o, v_ffn_w_up, v_ffn_dw, v_ffn_dw_b, v_ffn_w_down):
    w = dict(zip(WEIGHTS, (norm_mix, norm_ffn, conv_w_in, conv_b_in, conv_dw, conv_dw_b, conv_ln_g, conv_ln_b, conv_w_out, conv_b_out, pool_w, pool_b, pool_scale, fox_w_in, fox_b_f, fox_q_gain, fox_k_gain, fox_w_o, ffn_w_up, ffn_dw, ffn_dw_b, ffn_w_down)))
    m = dict(zip(WEIGHTS, (m_norm_mix, m_norm_ffn, m_conv_w_in, m_conv_b_in, m_conv_dw, m_conv_dw_b, m_conv_ln_g, m_conv_ln_b, m_conv_w_out, m_conv_b_out, m_pool_w, m_pool_b, m_pool_scale, m_fox_w_in, m_fox_b_f, m_fox_q_gain, m_fox_k_gain, m_fox_w_o, m_ffn_w_up, m_ffn_dw, m_ffn_dw_b, m_ffn_w_down)))
    v = dict(zip(WEIGHTS, (v_norm_mix, v_norm_ffn, v_conv_w_in, v_conv_b_in, v_conv_dw, v_conv_dw_b, v_conv_ln_g, v_conv_ln_b, v_conv_w_out, v_conv_b_out, v_pool_w, v_pool_b, v_pool_scale, v_fox_w_in, v_fox_b_f, v_fox_q_gain, v_fox_k_gain, v_fox_w_o, v_ffn_w_up, v_ffn_dw, v_ffn_dw_b, v_ffn_w_down)))
    return _train_step(x, loss_target, w, m, v)
```

```python
import functools
import math

import jax
import jax.numpy as jnp
from jax import lax
from jax.experimental import pallas as pl
from jax.experimental.pallas import tpu as pltpu

F32, BF16 = jnp.float32, jnp.bfloat16
SDS = jax.ShapeDtypeStruct

N_DEV = 8
EPS = 1e-6
POOL_WINDOWS = (2, 4, 8, 16)
HEAD_DIM = 64
ADAM_LR, ADAM_B1, ADAM_B2, ADAM_EPS, ADAM_WD, ADAM_STEP = 0.001, 0.9, 0.999, 1e-08, 0.01, 10
LANES = 128
VMEM_LIMIT_BYTES = 48 * 1024 * 1024
NEG = -1e30

WEIGHTS = ['norm_mix', 'norm_ffn', 'conv_w_in', 'conv_b_in', 'conv_dw', 'conv_dw_b', 'conv_ln_g', 'conv_ln_b',
           'conv_w_out', 'conv_b_out', 'pool_w', 'pool_b', 'pool_scale', 'fox_w_in', 'fox_b_f', 'fox_q_gain',
           'fox_k_gain', 'fox_w_o', 'ffn_w_up', 'ffn_dw', 'ffn_dw_b', 'ffn_w_down']
SHARD_AXIS = {'conv_w_in': 2, 'conv_b_in': 1, 'conv_dw': 2, 'conv_dw_b': 1, 'conv_ln_g': 1, 'conv_ln_b': 1,
              'conv_w_out': 1, 'conv_b_out': 1, 'pool_w': 2, 'pool_b': 2, 'fox_w_in': 2, 'fox_w_o': 1,
              'ffn_w_up': 2, 'ffn_dw': 2, 'ffn_w_down': 1}
MATRICES = ('conv_w_in', 'conv_w_out', 'pool_w', 'fox_w_in', 'fox_w_o', 'ffn_w_up', 'ffn_w_down')
SHARDED = [n for n in WEIGHTS if n in SHARD_AXIS]
REPLICATED = [n for n in WEIGHTS if n not in SHARD_AXIS]


def _cp(*sem):
    return pltpu.CompilerParams(dimension_semantics=sem, vmem_limit_bytes=VMEM_LIMIT_BYTES)


def _tile(n, pref, align=8):
    if n <= pref:
        return n
    t = (pref // align) * align
    while t >= align:
        if n % t == 0:
            return t
        t -= align
    return n


def _fold8(x):
    r, c = x.shape
    return x.reshape(r // 8, 8, c).sum(axis=0)


def _sigmoid(x):
    return 0.5 * jnp.tanh(0.5 * x) + 0.5


def _shifts_back(cur, tail, n):
    hb = tail.shape[0]
    xe = jnp.concatenate([tail, cur], axis=0)
    return [cur] + [pltpu.roll(xe, j, axis=0)[hb:] for j in range(1, n)]


def _shifts_fwd(cur, head, n):
    ts = cur.shape[0]
    xe = jnp.concatenate([cur, head], axis=0)
    ln = xe.shape[0]
    return [cur] + [pltpu.roll(xe, ln - j, axis=0)[:ts] for j in range(1, n)]


def _dot_hi(a, b):
    return jnp.dot(a, b, preferred_element_type=F32, precision=lax.Precision.HIGHEST)


def _rmsnorm_fwd(x, g):
    t, d = x.shape
    tm = _tile(t, 512)

    def body(x_ref, g_ref, h_ref):
        xv = x_ref[...]
        r = lax.rsqrt(jnp.mean(xv * xv, axis=-1, keepdims=True) + EPS)
        h_ref[...] = (xv * r * g_ref[...]).astype(BF16)

    return pl.pallas_call(
        body, out_shape=SDS((t, d), BF16), grid=(t // tm,), name="rmsnorm_fwd",
        in_specs=[pl.BlockSpec((tm, d), lambda i: (i, 0)), pl.BlockSpec((1, d), lambda i: (0, 0))],
        out_specs=pl.BlockSpec((tm, d), lambda i: (i, 0)), compiler_params=_cp("parallel"))(x, g)


def _mm(a, b, *, trans_b=False, bias=None, residual=None, a2=None, b2=None, norm_gain=None, norm_bwd=None,
        out_dtype=BF16, name="mm"):
    m, k = a.shape
    n = b.shape[0] if trans_b else b.shape[1]
    tm, tn, tk = _tile(m, 1024, 16), _tile(n, 1536, LANES), _tile(k, 1536, LANES)
    if norm_bwd is not None:
        if a2 is not None:
            tm, tk = _tile(m, 256, 16), k
        else:
            tk = _tile(k, 768, LANES)
    nk = k // tk
    two = a2 is not None
    steps = 2 * nk if two else nk
    dims = (((1,), (1,)), ((), ())) if trans_b else (((1,), (0,)), ((), ()))
    has_bias, has_res, has_norm, has_nbwd = bias is not None, residual is not None, norm_gain is not None, \
        norm_bwd is not None
    assert not (has_norm or has_nbwd) or tn == n
    n_out = 2 if (has_norm or has_nbwd) else 1

    def body(*refs):
        pos = 4 if two else 2
        bias_ref = refs[pos] if has_bias else None
        pos += has_bias
        res_ref = refs[pos] if has_res else None
        pos += has_res
        gain_ref = refs[pos] if has_norm else None
        pos += has_norm
        x_ref, g_ref, dres_ref = refs[pos:pos + 3] if has_nbwd else (None, None, None)
        pos += 3 * has_nbwd
        outs = refs[pos:pos + n_out]
        first = (pl.program_id(0) == 0) & (pl.program_id(1) == 0) & (pl.program_id(2) == 0)

        def finish(r):
            if has_bias:
                r = r + bias_ref[...]
            if has_res:
                r = r + res_ref[...]
            if has_nbwd:
                xv = x_ref[...]
                rs = lax.rsqrt(jnp.mean(xv * xv, axis=-1, keepdims=True) + EPS)
                xh = xv * rs
                u = r * g_ref[...]
                outs[0][...] = dres_ref[...] + rs * (u - xh * jnp.mean(u * xh, axis=-1, keepdims=True))
                outs[1][...] += _fold8(r * xh)
                return
            outs[0][...] = r.astype(out_dtype)
            if has_norm:
                outs[1][...] = (r * lax.rsqrt(jnp.mean(r * r, axis=-1, keepdims=True) + EPS)
                                * gain_ref[...]).astype(BF16)

        def dot(a_ref, b_ref):
            return lax.dot_general(a_ref[...].astype(BF16), b_ref[...].astype(BF16), dims, preferred_element_type=F32)

        if has_nbwd:
            @pl.when(first)
            def _():
                outs[1][...] = jnp.zeros_like(outs[1])

        if steps == 1:
            finish(dot(refs[0], refs[1]))
            return
        acc_ref = refs[-1]
        kk = pl.program_id(2)

        @pl.when(kk == 0)
        def _():
            acc_ref[...] = jnp.zeros_like(acc_ref)

        @pl.when(kk < nk)
        def _():
            acc_ref[...] += dot(refs[0], refs[1])

        if two:
            @pl.when(kk >= nk)
            def _():
                acc_ref[...] += dot(refs[2], refs[3])

        @pl.when(kk == steps - 1)
        def _():
            finish(acc_ref[...])

    def pair(first):
        kmap = (lambda kk: jnp.minimum(kk, nk - 1)) if first else (lambda kk: jnp.maximum(kk - nk, 0))
        a_spec = pl.BlockSpec((tm, tk), lambda j, i, kk: (i, kmap(kk)))
        if trans_b:
            b_spec = pl.BlockSpec((tn, tk), lambda j, i, kk: (j, kmap(kk)))
        else:
            b_spec = pl.BlockSpec((tk, tn), lambda j, i, kk: (kmap(kk), j))
        return [a_spec, b_spec]

    in_specs, args = pair(True), [a, b]
    if two:
        in_specs += pair(False)
        args += [a2, b2]
    if has_bias:
        in_specs.append(pl.BlockSpec((1, tn), lambda j, i, kk: (0, j)))
        args.append(bias)
    tile = pl.BlockSpec((tm, tn), lambda j, i, kk: (i, j))
    vec = pl.BlockSpec((1, tn), lambda j, i, kk: (0, j))
    if has_res:
        in_specs.append(tile)
        args.append(residual)
    if has_norm:
        in_specs.append(vec)
        args.append(norm_gain)
    if has_nbwd:
        in_specs += [tile, vec, tile]
        args += list(norm_bwd)
    out_shape, out_specs = [SDS((m, n), F32 if has_nbwd else out_dtype)], [tile]
    if has_norm:
        out_shape.append(SDS((m, n), BF16))
        out_specs.append(tile)
    if has_nbwd:
        out_shape.append(SDS((8, n), F32))
        out_specs.append(pl.BlockSpec((8, tn), lambda j, i, kk: (0, 0)))
    outs = pl.pallas_call(
        body, out_shape=out_shape, grid=(n // tn, m // tm, steps), name=name, in_specs=in_specs, out_specs=out_specs,
        scratch_shapes=[] if steps == 1 else [pltpu.VMEM((tm, tn), F32)],
        compiler_params=_cp("arbitrary", "arbitrary", "arbitrary"))(*args)
    return outs[0] if n_out == 1 else tuple(outs)


def _wgrad(a, g, *, out_dtype=BF16, name="wgrad"):
    m, ka = a.shape
    n = g.shape[1]
    ta, tn, tm = _tile(ka, 1536, LANES), _tile(n, 1536, LANES), _tile(m, 1024)
    nm = m // tm

    def body(a_ref, g_ref, o_ref, cs_ref, acc_ref):
        i, mm = pl.program_id(1), pl.program_id(2)

        @pl.when(mm == 0)
        def _():
            acc_ref[...] = jnp.zeros_like(acc_ref)

        @pl.when((mm == 0) & (i == 0))
        def _():
            cs_ref[...] = jnp.zeros_like(cs_ref)

        gv = g_ref[...]
        acc_ref[...] += lax.dot_general(a_ref[...].astype(BF16), gv.astype(BF16), (((0,), (0,)), ((), ())),
                                        preferred_element_type=F32)

        @pl.when(i == 0)
        def _():
            cs_ref[...] += _fold8(gv.astype(F32))

        @pl.when(mm == nm - 1)
        def _():
            o_ref[...] = acc_ref[...].astype(out_dtype)

    return pl.pallas_call(
        body, out_shape=(SDS((ka, n), out_dtype), SDS((8, n), F32)), grid=(n // tn, ka // ta, nm), name=name,
        in_specs=[pl.BlockSpec((tm, ta), lambda j, i, mm: (mm, i)), pl.BlockSpec((tm, tn), lambda j, i, mm: (mm, j))],
        out_specs=(pl.BlockSpec((ta, tn), lambda j, i, mm: (i, j)), pl.BlockSpec((8, tn), lambda j, i, mm: (0, j))),
        scratch_shapes=[pltpu.VMEM((ta, tn), F32)],
        compiler_params=_cp("arbitrary", "arbitrary", "arbitrary"))(a, g)


FFN_HALO = 16


def _ffn_conv(uc_ref, up_ref, w_ref, b_ref, s):
    u = uc_ref[...].astype(F32)
    tail = jnp.where(s > 0, up_ref[...].astype(F32), 0.0)
    sh = _shifts_back(u, tail, 3)
    return sh, sh[2] * w_ref[0:1, :] + sh[1] * w_ref[1:2, :] + sh[0] * w_ref[2:3, :] + b_ref[...]


def _ffn_act_fwd(uv, ug, dw8, b):
    bsz, s_len, f = uv.shape
    tc, ts = f, _tile(s_len, 256, FFN_HALO)
    nf, r = f // tc, ts // FFN_HALO

    def body(uv_ref, uvp_ref, ug_ref, ugp_ref, wv_ref, wg_ref, bv_ref, bg_ref, a_ref, vv_ref, vg_ref):
        s = pl.program_id(2)
        _, val = _ffn_conv(uv_ref, uvp_ref, wv_ref, bv_ref, s)
        _, gate = _ffn_conv(ug_ref, ugp_ref, wg_ref, bg_ref, s)
        a_ref[...] = (gate * _sigmoid(gate) * val).astype(BF16)
        vv_ref[...] = val.astype(BF16)
        vg_ref[...] = gate.astype(BF16)

    cur = pl.BlockSpec((None, ts, tc), lambda bi, j, s: (bi, s, j))
    prev = pl.BlockSpec((None, FFN_HALO, tc), lambda bi, j, s: (bi, jnp.maximum(s * r - 1, 0), j))

    def par(rows, off):
        return pl.BlockSpec((rows, tc), lambda bi, j, s: (0, j + off))

    return pl.pallas_call(
        body, out_shape=(SDS((bsz, s_len, f), BF16),) * 3, grid=(bsz, nf, s_len // ts), name="ffn_act_fwd",
        in_specs=[cur, prev, cur, prev, par(8, 0), par(8, nf), par(1, 0), par(1, nf)], out_specs=(cur, cur, cur),
        compiler_params=_cp("parallel", "parallel", "arbitrary"))(uv, uv, ug, ug, dw8, dw8, b, b)


def _ffn_act_bwd1(vv, vg, da):
    bsz, s_len, f = vv.shape
    tc, ts = f, _tile(s_len, 256, FFN_HALO)
    nf = f // tc

    def body(vv_ref, vg_ref, da_ref, dvv_ref, dvg_ref, dbv_ref, dbg_ref):
        @pl.when((pl.program_id(1) == 0) & (pl.program_id(2) == 0))
        def _():
            dbv_ref[...] = jnp.zeros_like(dbv_ref)
            dbg_ref[...] = jnp.zeros_like(dbg_ref)

        val, gate = vv_ref[...].astype(F32), vg_ref[...].astype(F32)
        sg = _sigmoid(gate)
        dav = da_ref[...].astype(F32)
        dval = dav * gate * sg
        dgate = dav * val * (sg * (1.0 + gate * (1.0 - sg)))
        dvv_ref[...] = dval.astype(BF16)
        dvg_ref[...] = dgate.astype(BF16)
        dbv_ref[...] += _fold8(dval)
        dbg_ref[...] += _fold8(dgate)

    cur = pl.BlockSpec((None, ts, tc), lambda j, bi, s: (bi, s, j))
    acc1 = pl.BlockSpec((8, tc), lambda j, bi, s: (0, j))
    return pl.pallas_call(
        body, out_shape=(SDS((bsz, s_len, f), BF16), SDS((bsz, s_len, f), BF16), SDS((8, f), F32), SDS((8, f), F32)),
        grid=(nf, bsz, s_len // ts), name="ffn_act_bwd1", in_specs=[cur, cur, cur], out_specs=(cur, cur, acc1, acc1),
        compiler_params=_cp("arbitrary", "arbitrary", "arbitrary"))(vv, vg, da)


def _ffn_act_bwd2(dvv, dvg, uv, ug, dw8):
    bsz, s_len, f = dvv.shape
    tc, ts = f, _tile(s_len, 256, FFN_HALO)
    nf, r, ns = f // tc, ts // FFN_HALO, s_len // ts

    def body(vc_ref, vn_ref, gc_ref, gn_ref, uv_ref, ug_ref, wv_ref, wg_ref, duv_ref, dug_ref, ddwv_ref, ddwg_ref):
        bi, s = pl.program_id(1), pl.program_id(2)

        @pl.when((bi == 0) & (s == 0))
        def _():
            ddwv_ref[...] = jnp.zeros_like(ddwv_ref)
            ddwg_ref[...] = jnp.zeros_like(ddwg_ref)

        for dc_ref, dn_ref, u_ref, w_ref, du_ref, ddw_ref in (
                (vc_ref, vn_ref, uv_ref, wv_ref, duv_ref, ddwv_ref),
                (gc_ref, gn_ref, ug_ref, wg_ref, dug_ref, ddwg_ref)):
            d = dc_ref[...].astype(F32)
            head = jnp.where(s < ns - 1, dn_ref[...].astype(F32), 0.0)
            sh = _shifts_fwd(d, head, 3)
            du_ref[...] = (sh[0] * w_ref[2:3, :] + sh[1] * w_ref[1:2, :] + sh[2] * w_ref[0:1, :]).astype(BF16)
            u = u_ref[...].astype(F32)
            for j in range(3):
                ddw_ref[2 - j] += _fold8(sh[j] * u)

    cur = pl.BlockSpec((None, ts, tc), lambda j, bi, s: (bi, s, j))
    nxt = pl.BlockSpec((None, FFN_HALO, tc),
                       lambda j, bi, s: (bi, jnp.minimum((s + 1) * r, s_len // FFN_HALO - 1), j))

    def par(off):
        return pl.BlockSpec((8, tc), lambda j, bi, s: (0, j + off))

    acc3 = pl.BlockSpec((3, 8, tc), lambda j, bi, s: (0, 0, j))
    return pl.pallas_call(
        body, out_shape=(SDS((bsz, s_len, f), BF16), SDS((bsz, s_len, f), BF16), SDS((3, 8, f), F32),
                         SDS((3, 8, f), F32)),
        grid=(nf, bsz, ns), name="ffn_act_bwd2",
        in_specs=[cur, nxt, cur, nxt, cur, cur, par(0), par(nf)], out_specs=(cur, cur, acc3, acc3),
        compiler_params=_cp("arbitrary", "arbitrary", "arbitrary"))(dvv, dvv, dvg, dvg, uv, ug, dw8, dw8)


CONV_HALO = 32
CONV_CHUNK = 256


def _conv_act_fwd(p, dw32, dwb, ln_g, ln_b):
    bsz, s_len, d2 = p.shape
    d = d2 // 2
    kw = 31
    ts = _tile(s_len, 256, CONV_HALO)
    r = ts // CONV_HALO
    cc = min(CONV_CHUNK, d)

    def body(pc_ref, pp_ref, w_ref, wb_ref, g_ref, b_ref, u_ref, s_ref):
        s = pl.program_id(1)
        tot = jnp.zeros((ts, 1), F32)
        for c0 in range(0, d, cc):
            a = pc_ref[:, c0:c0 + cc].astype(F32)
            g = pc_ref[:, d + c0:d + c0 + cc].astype(F32)
            z = a * _sigmoid(g)
            ap = pp_ref[:, c0:c0 + cc].astype(F32)
            gp = pp_ref[:, d + c0:d + c0 + cc].astype(F32)
            tail = jnp.where(s > 0, ap * _sigmoid(gp), 0.0)
            sh = _shifts_back(z, tail, kw)
            acc = wb_ref[:, c0:c0 + cc] + sh[0] * w_ref[kw - 1:kw, c0:c0 + cc]
            for j in range(1, kw):
                acc = acc + sh[j] * w_ref[kw - 1 - j:kw - j, c0:c0 + cc]
            u_ref[:, c0:c0 + cc] = acc
            tot = tot + jnp.sum(acc, axis=-1, keepdims=True)
        u = u_ref[...]
        mu = tot / d
        uc = u - mu
        var = jnp.mean(uc * uc, axis=-1, keepdims=True)
        ul = uc * lax.rsqrt(var + EPS) * g_ref[...] + b_ref[...]
        s_ref[...] = (ul * _sigmoid(ul)).astype(BF16)

    vec = pl.BlockSpec((1, d), lambda bi, s: (0, 0))
    return pl.pallas_call(
        body, out_shape=(SDS((bsz, s_len, d), F32), SDS((bsz, s_len, d), BF16)), grid=(bsz, s_len // ts),
        name="conv_act_fwd",
        in_specs=[pl.BlockSpec((None, ts, d2), lambda bi, s: (bi, s, 0)),
                  pl.BlockSpec((None, CONV_HALO, d2), lambda bi, s: (bi, jnp.maximum(s * r - 1, 0), 0)),
                  pl.BlockSpec((32, d), lambda bi, s: (0, 0)), vec, vec, vec],
        out_specs=(pl.BlockSpec((None, ts, d), lambda bi, s: (bi, s, 0)),
                   pl.BlockSpec((None, ts, d), lambda bi, s: (bi, s, 0))),
        compiler_params=_cp("parallel", "arbitrary"))(p, p, dw32, dwb, ln_g, ln_b)


def _conv_act_bwd1(u, ds, ln_g, ln_b):
    t, d = u.shape
    ts = _tile(t, 256)

    def body(u_ref, ds_ref, g_ref, b_ref, du_ref, dg_ref, db_ref, dwb_ref):
        @pl.when(pl.program_id(0) == 0)
        def _():
            dg_ref[...] = jnp.zeros_like(dg_ref)
            db_ref[...] = jnp.zeros_like(db_ref)
            dwb_ref[...] = jnp.zeros_like(dwb_ref)

        uv = u_ref[...]
        uc = uv - jnp.mean(uv, axis=-1, keepdims=True)
        rstd = lax.rsqrt(jnp.mean(uc * uc, axis=-1, keepdims=True) + EPS)
        uh = uc * rstd
        ul = uh * g_ref[...] + b_ref[...]
        sg = _sigmoid(ul)
        dul = ds_ref[...].astype(F32) * (sg * (1.0 + ul * (1.0 - sg)))
        duh = dul * g_ref[...]
        du = rstd * (duh - jnp.mean(duh, axis=-1, keepdims=True) - uh * jnp.mean(duh * uh, axis=-1, keepdims=True))
        du_ref[...] = du
        dg_ref[...] += _fold8(dul * uh)
        db_ref[...] += _fold8(dul)
        dwb_ref[...] += _fold8(du)

    row = pl.BlockSpec((ts, d), lambda i: (i, 0))
    vec = pl.BlockSpec((1, d), lambda i: (0, 0))
    acc = pl.BlockSpec((8, d), lambda i: (0, 0))
    return pl.pallas_call(
        body, out_shape=(SDS((t, d), F32), SDS((8, d), F32), SDS((8, d), F32), SDS((8, d), F32)), grid=(t // ts,),
        name="conv_act_bwd1", in_specs=[row, row, vec, vec], out_specs=(row, acc, acc, acc),
        compiler_params=_cp("arbitrary"))(u, ds, ln_g, ln_b)


def _conv_act_bwd2(du, p, dw32):
    bsz, s_len, d2 = p.shape
    d = d2 // 2
    kw = 31
    ts = _tile(s_len, 256, CONV_HALO)
    r, ns = ts // CONV_HALO, s_len // ts
    cc = min(CONV_CHUNK, d)

    def body(dc_ref, dn_ref, pc_ref, pp_ref, w_ref, dp_ref, ddw_ref):
        bi, s = pl.program_id(0), pl.program_id(1)

        @pl.when((bi == 0) & (s == 0))
        def _():
            ddw_ref[...] = jnp.zeros_like(ddw_ref)

        for c0 in range(0, d, cc):
            a = pc_ref[:, c0:c0 + cc].astype(F32)
            g = pc_ref[:, d + c0:d + c0 + cc].astype(F32)
            sg = _sigmoid(g)
            z = a * sg
            ap = pp_ref[:, c0:c0 + cc].astype(F32)
            gp = pp_ref[:, d + c0:d + c0 + cc].astype(F32)
            tail = jnp.where(s > 0, ap * _sigmoid(gp), 0.0)
            duv = dc_ref[:, c0:c0 + cc]
            head = jnp.where(s < ns - 1, dn_ref[:, c0:c0 + cc], 0.0)
            zb = _shifts_back(z, tail, kw)
            for k in range(kw):
                ddw_ref[k, :, c0:c0 + cc] += _fold8(duv * zb[kw - 1 - k])
            df = _shifts_fwd(duv, head, kw)
            dz = df[0] * w_ref[kw - 1:kw, c0:c0 + cc]
            for j in range(1, kw):
                dz = dz + df[j] * w_ref[kw - 1 - j:kw - j, c0:c0 + cc]
            dp_ref[:, c0:c0 + cc] = (dz * sg).astype(BF16)
            dp_ref[:, d + c0:d + c0 + cc] = (dz * a * sg * (1.0 - sg)).astype(BF16)

    return pl.pallas_call(
        body, out_shape=(SDS((bsz, s_len, d2), BF16), SDS((32, 8, d), F32)), grid=(bsz, ns), name="conv_act_bwd2",
        in_specs=[pl.BlockSpec((None, ts, d), lambda bi, s: (bi, s, 0)),
                  pl.BlockSpec((None, CONV_HALO, d),
                               lambda bi, s: (bi, jnp.minimum((s + 1) * r, s_len // CONV_HALO - 1), 0)),
                  pl.BlockSpec((None, ts, d2), lambda bi, s: (bi, s, 0)),
                  pl.BlockSpec((None, CONV_HALO, d2), lambda bi, s: (bi, jnp.maximum(s * r - 1, 0), 0)),
                  pl.BlockSpec((32, d), lambda bi, s: (0, 0))],
        out_specs=(pl.BlockSpec((None, ts, d2), lambda bi, s: (bi, s, 0)),
                   pl.BlockSpec((32, 8, d), lambda bi, s: (0, 0, 0))),
        compiler_params=_cp("arbitrary", "arbitrary"))(du, du, p, p, dw32)


POOL_HALO = 16


def _pool_counts(s, ts, rows, w):
    t = s * ts + lax.broadcasted_iota(jnp.int32, (rows, 1), 0)
    return jnp.minimum(t + 1, w).astype(F32)


def _pool_fwd(x, gmix, w, b, scale):
    bsz, s_len, d = x.shape
    ng = len(POOL_WINDOWS)
    cg = d // ng
    ts = _tile(s_len, 512, POOL_HALO)
    r = ts // POOL_HALO

    def body(xc_ref, xp_ref, g_ref, w_ref, b_ref, sc_ref, y_ref, p_ref):
        s = pl.program_id(1)

        def norm(v):
            return v * lax.rsqrt(jnp.mean(v * v, axis=-1, keepdims=True) + EPS) * g_ref[...]

        xc = xc_ref[...]
        h = norm(xc)
        tail = jnp.where(s > 0, norm(xp_ref[...]), 0.0)
        for gi, win in enumerate(POOL_WINDOWS):
            lo, hi = gi * cg, (gi + 1) * cg
            hg = h[:, lo:hi]
            acc = jnp.concatenate([tail[:, lo:hi], hg], axis=0)
            step = 1
            while step < win:
                acc = acc + pltpu.roll(acc, step, axis=0)
                step *= 2
            pg = acc[POOL_HALO:] / _pool_counts(s, ts, ts, win) - hg
            pb = pg.astype(BF16)
            p_ref[:, lo:hi] = pb
            yg = jnp.dot(pb, w_ref[gi], preferred_element_type=F32) + b_ref[:, lo:hi]
            y_ref[:, lo:hi] = xc[:, lo:hi] + yg * sc_ref[:, lo:hi]

    vec = pl.BlockSpec((1, d), lambda bi, s: (0, 0))
    blk = pl.BlockSpec((None, ts, d), lambda bi, s: (bi, s, 0))
    return pl.pallas_call(
        body, out_shape=(SDS((bsz, s_len, d), F32), SDS((bsz, s_len, d), BF16)), grid=(bsz, s_len // ts),
        name="pool_fwd",
        in_specs=[blk, pl.BlockSpec((None, POOL_HALO, d), lambda bi, s: (bi, jnp.maximum(s * r - 1, 0), 0)),
                  vec, pl.BlockSpec((ng, cg, cg), lambda bi, s: (0, 0, 0)), vec, vec],
        out_specs=(blk, blk), compiler_params=_cp("parallel", "arbitrary"))(x, x, gmix, w, b, scale)


def _pool_bwd(x, dy, p, gmix, w, b, scale):
    bsz, s_len, d = x.shape
    ng = len(POOL_WINDOWS)
    cg = d // ng
    ts = _tile(s_len, 512, POOL_HALO)
    r, ns = ts // POOL_HALO, s_len // ts
    nt = (((1,), (1,)), ((), ()))
    tn = (((0,), (0,)), ((), ()))

    def body(x_ref, dy_ref, dyn_ref, p_ref, g_ref, w_ref, b_ref, sc_ref, dx_ref, dw_ref, db_ref, dsc_ref, dg_ref):
        bi, s = pl.program_id(0), pl.program_id(1)

        @pl.when((bi == 0) & (s == 0))
        def _():
            dw_ref[...] = jnp.zeros_like(dw_ref)
            db_ref[...] = jnp.zeros_like(db_ref)
            dsc_ref[...] = jnp.zeros_like(dsc_ref)
            dg_ref[...] = jnp.zeros_like(dg_ref)

        dy = dy_ref[...]
        dyy = dy * sc_ref[...]
        dyy_n = jnp.where(s < ns - 1, dyn_ref[...] * sc_ref[...], 0.0)
        db_ref[...] += _fold8(dyy)
        xv = x_ref[...]
        rr = lax.rsqrt(jnp.mean(xv * xv, axis=-1, keepdims=True) + EPS)
        xh = xv * rr
        for gi, win in enumerate(POOL_WINDOWS):
            lo, hi = gi * cg, (gi + 1) * cg
            pb = p_ref[:, lo:hi]
            wg = w_ref[gi]
            pre = jnp.dot(pb, wg, preferred_element_type=F32) + b_ref[:, lo:hi]
            dsc_ref[:, lo:hi] += _fold8(dy[:, lo:hi] * pre)
            dyb = dyy[:, lo:hi].astype(BF16)
            dw_ref[gi] += lax.dot_general(pb, dyb, tn, preferred_element_type=F32)
            dp = lax.dot_general(dyb, wg, nt, preferred_element_type=F32)
            dp_n = lax.dot_general(dyy_n[:, lo:hi].astype(BF16), wg, nt, preferred_element_type=F32)
            q = dp / _pool_counts(s, ts, ts, win)
            q_n = dp_n / _pool_counts(s + 1, ts, POOL_HALO, win)
            acc = jnp.concatenate([q, q_n], axis=0)
            ln = ts + POOL_HALO
            step = 1
            while step < win:
                acc = acc + pltpu.roll(acc, ln - step, axis=0)
                step *= 2
            dh = acc[:ts] - dp
            xhg = xh[:, lo:hi]
            dg_ref[:, lo:hi] += _fold8(dh * xhg)
            dx_ref[:, lo:hi] = dh * g_ref[:, lo:hi]
        u = dx_ref[...]
        dx_ref[...] = dy + rr * (u - xh * jnp.mean(u * xh, axis=-1, keepdims=True))

    vec = pl.BlockSpec((1, d), lambda bi, s: (0, 0))
    acc8 = pl.BlockSpec((8, d), lambda bi, s: (0, 0))
    blk = pl.BlockSpec((None, ts, d), lambda bi, s: (bi, s, 0))
    wspec = pl.BlockSpec((ng, cg, cg), lambda bi, s: (0, 0, 0))
    return pl.pallas_call(
        body, out_shape=(SDS((bsz, s_len, d), F32), SDS((ng, cg, cg), F32), SDS((8, d), F32), SDS((8, d), F32),
                         SDS((8, d), F32)),
        grid=(bsz, ns), name="pool_bwd",
        in_specs=[blk, blk,
                  pl.BlockSpec((None, POOL_HALO, d),
                               lambda bi, s: (bi, jnp.minimum((s + 1) * r, s_len // POOL_HALO - 1), 0)),
                  blk, vec, wspec, vec, vec],
        out_specs=(blk, wspec, acc8, acc8, acc8),
        compiler_params=_cp("arbitrary", "arbitrary"))(x, dy, dy, p, gmix, w, b, scale)


def _tri(n, upper):
    row = lax.broadcasted_iota(jnp.int32, (n, n), 0)
    col = lax.broadcasted_iota(jnp.int32, (n, n), 1)
    return jnp.where((col >= row) if upper else (col <= row), 1.0, 0.0).astype(F32)


def _fox_gate_fwd(proj, bf, n_heads):
    bsz, s_len, width = proj.shape
    col = width // LANES - 1
    ts = _tile(s_len, 512)

    def body(fl_ref, b_ref, c_ref, carry_ref):
        @pl.when(pl.program_id(1) == 0)
        def _():
            carry_ref[...] = jnp.zeros_like(carry_ref)

        xv = fl_ref[...] + b_ref[...]
        logf = jnp.minimum(xv, 0.0) - jnp.log(1.0 + jnp.exp(-jnp.abs(xv)))
        lane = lax.broadcasted_iota(jnp.int32, (1, LANES), 1)
        logf = jnp.where(lane < n_heads, logf, 0.0)
        c = _dot_hi(_tri(ts, False), logf) + carry_ref[0:1, :]
        c_ref[...] = c
        carry_ref[0:1, :] = c[ts - 1:ts, :]

    return pl.pallas_call(
        body, out_shape=SDS((bsz, s_len, LANES), F32), grid=(bsz, s_len // ts), name="fox_gate_fwd",
        in_specs=[pl.BlockSpec((None, ts, LANES), lambda bi, s: (bi, s, col)),
                  pl.BlockSpec((1, LANES), lambda bi, s: (0, 0))],
        out_specs=pl.BlockSpec((None, ts, LANES), lambda bi, s: (bi, s, 0)),
        scratch_shapes=[pltpu.VMEM((8, LANES), F32)],
        compiler_params=_cp("arbitrary", "arbitrary"))(proj, bf)


def _fox_gate_bwd(dc, proj, bf, n_heads):
    bsz, s_len, width = proj.shape
    col = width // LANES - 1
    ts = _tile(s_len, 512)
    ns = s_len // ts

    def body(dc_ref, fl_ref, b_ref, dfl_ref, db_ref, carry_ref):
        bi, s = pl.program_id(0), pl.program_id(1)

        @pl.when((bi == 0) & (s == 0))
        def _():
            db_ref[...] = jnp.zeros_like(db_ref)

        @pl.when(s == 0)
        def _():
            carry_ref[...] = jnp.zeros_like(carry_ref)

        dlogf = _dot_hi(_tri(ts, True), dc_ref[...]) + carry_ref[0:1, :]
        carry_ref[0:1, :] = dlogf[0:1, :]
        lane = lax.broadcasted_iota(jnp.int32, (1, LANES), 1)
        dfl = jnp.where(lane < n_heads, dlogf * (1.0 - _sigmoid(fl_ref[...] + b_ref[...])), 0.0)
        dfl_ref[...] = dfl.astype(BF16)
        db_ref[...] += _fold8(dfl)

    return pl.pallas_call(
        body, out_shape=(SDS((bsz, s_len, LANES), BF16), SDS((8, LANES), F32)), grid=(bsz, ns), name="fox_gate_bwd",
        in_specs=[pl.BlockSpec((None, ts, LANES), lambda bi, s: (bi, ns - 1 - s, 0)),
                  pl.BlockSpec((None, ts, LANES), lambda bi, s: (bi, ns - 1 - s, col)),
                  pl.BlockSpec((1, LANES), lambda bi, s: (0, 0))],
        out_specs=(pl.BlockSpec((None, ts, LANES), lambda bi, s: (bi, ns - 1 - s, 0)),
                   pl.BlockSpec((8, LANES), lambda bi, s: (0, 0))),
        scratch_shapes=[pltpu.VMEM((8, LANES), F32)],
        compiler_params=_cp("arbitrary", "arbitrary"))(dc, proj, bf)


def _head_maps(d):
    ch = lax.broadcasted_iota(jnp.int32, (d, LANES), 0) // HEAD_DIM
    hd = lax.broadcasted_iota(jnp.int32, (d, LANES), 1)
    e = jnp.where(ch == hd, 1.0, 0.0).astype(BF16)
    cht = lax.broadcasted_iota(jnp.int32, (LANES, d), 1) // HEAD_DIM
    hdt = lax.broadcasted_iota(jnp.int32, (LANES, d), 0)
    et = jnp.where(cht == hdt, 1.0, 0.0).astype(BF16)
    return e, et


def _dot_sel(x, e):
    a = x.astype(BF16)
    r = x - a.astype(F32)
    b = r.astype(BF16)
    c = (r - b.astype(F32)).astype(BF16)
    return (jnp.dot(a, e, preferred_element_type=F32) + jnp.dot(b, e, preferred_element_type=F32)
            + jnp.dot(c, e, preferred_element_type=F32))


def _fox_qknorm_fwd(proj, gq, gk, d):
    t = proj.shape[0]
    ts = _tile(t, 256)
    scale = 1.0 / math.sqrt(HEAD_DIM)

    def body(q_ref, k_ref, v_ref, gq_ref, gk_ref, qn_ref, kn_ref, vb_ref):
        e, et = _head_maps(d)

        def norm(v, g):
            r = lax.rsqrt(_dot_sel(v * v, e) / HEAD_DIM + EPS)
            return v * _dot_sel(r, et) * g

        qn_ref[...] = (norm(q_ref[...], gq_ref[...]) * scale).astype(BF16)
        kn_ref[...] = norm(k_ref[...], gk_ref[...]).astype(BF16)
        vb_ref[...] = v_ref[...].astype(BF16)

    def colblk(j):
        return pl.BlockSpec((ts, d), lambda i: (i, j))

    vec = pl.BlockSpec((1, d), lambda i: (0, 0))
    out = pl.BlockSpec((ts, d), lambda i: (i, 0))
    return pl.pallas_call(
        body, out_shape=(SDS((t, d), BF16),) * 3, grid=(t // ts,), name="fox_qknorm_fwd",
        in_specs=[colblk(0), colblk(1), colblk(2), vec, vec], out_specs=(out, out, out),
        compiler_params=_cp("parallel"))(proj, proj, proj, gq, gk)


def _fox_qknorm_bwd(proj, dq, dk, dv, gq, gk, d):
    t = proj.shape[0]
    ts = _tile(t, 256)
    scale = 1.0 / math.sqrt(HEAD_DIM)

    def body(q_ref, k_ref, dq_ref, dk_ref, dv_ref, gq_ref, gk_ref, dp_ref, dgq_ref, dgk_ref):
        @pl.when(pl.program_id(0) == 0)
        def _():
            dgq_ref[...] = jnp.zeros_like(dgq_ref)
            dgk_ref[...] = jnp.zeros_like(dgk_ref)

        e, et = _head_maps(d)

        def back(v, g, dn, dg_ref):
            r = _dot_sel(lax.rsqrt(_dot_sel(v * v, e) / HEAD_DIM + EPS), et)
            vh = v * r
            dg_ref[...] += _fold8(dn * vh)
            u = dn * g
            mh = _dot_sel(_dot_sel(u * vh, e) / HEAD_DIM, et)
            return r * (u - vh * mh)

        dp_ref[:, 0:d] = back(q_ref[...], gq_ref[...], dq_ref[...] * scale, dgq_ref).astype(BF16)
        dp_ref[:, d:2 * d] = back(k_ref[...], gk_ref[...], dk_ref[...], dgk_ref).astype(BF16)
        dp_ref[:, 2 * d:3 * d] = dv_ref[...]

    def colblk(j):
        return pl.BlockSpec((ts, d), lambda i: (i, j))

    row = pl.BlockSpec((ts, d), lambda i: (i, 0))
    vec = pl.BlockSpec((1, d), lambda i: (0, 0))
    acc = pl.BlockSpec((8, d), lambda i: (0, 0))
    return pl.pallas_call(
        body, out_shape=(SDS((t, 3 * d), BF16), SDS((8, d), F32), SDS((8, d), F32)), grid=(t // ts,),
        name="fox_qknorm_bwd", in_specs=[colblk(0), colblk(1), row, row, row, vec, vec],
        out_specs=(pl.BlockSpec((ts, 3 * d), lambda i: (i, 0)), acc, acc),
        compiler_params=_cp("arbitrary"))(proj, proj, dq, dk, dv, gq, gk)


ATT_BLOCK = 512
_NT = (((1,), (1,)), ((), ()))
_TN = (((0,), (0,)), ((), ()))


def _head_mask(h):
    return (lax.broadcasted_iota(jnp.int32, (1, LANES), 1) // HEAD_DIM) == h


def _causal(qi, ki, tq, tk):
    row = qi * tq + lax.broadcasted_iota(jnp.int32, (tq, 1), 0)
    col = ki * tk + lax.broadcasted_iota(jnp.int32, (1, tk), 1)
    return col <= row


def _direct_exchange(ins, outs, place, sems, gather):
    send_sems, recv_sems, local_sems = sems
    x, y, c, me = _mesh_place()
    copies = []
    for t in range(len(ins)):
        dst = outs[t].at[me] if gather else outs[place[t][0]].at[me, place[t][1]]
        copies.append(pltpu.make_async_copy(ins[t] if gather else ins[t].at[me], dst, local_sems.at[t]))
        for kbits in range(1, N_DEV):
            px = 1 - x if kbits & 4 else x
            py = 1 - y if kbits & 2 else y
            pc = 1 - c if kbits & 1 else c
            copies.append(pltpu.make_async_remote_copy(
                src_ref=ins[t] if gather else ins[t].at[4 * px + 2 * py + pc], dst_ref=dst,
                send_sem=send_sems.at[t, kbits - 1], recv_sem=recv_sems.at[t, kbits - 1],
                device_id=(px, py, pc), device_id_type=pl.DeviceIdType.MESH))
    return copies


def _exchange_scratch(n):
    return [pltpu.SemaphoreType.DMA((n, N_DEV - 1)), pltpu.SemaphoreType.DMA((n, N_DEV - 1)),
            pltpu.SemaphoreType.DMA((n,))]


def _flash_fwd(q, k, v, crow, gather=()):
    bsz, s_len, d = q.shape
    nj = d // LANES
    tq = tk = _tile(s_len, ATT_BLOCK, LANES)
    nq = s_len // tq
    ng = len(gather)

    pairs = [(a, b) for a in range(nq) for b in range(a + 1)]
    qtab = jnp.asarray([a for a, _ in pairs], jnp.int32)
    ktab = jnp.asarray([b for _, b in pairs], jnp.int32)

    def body(qtab_ref, ktab_ref, q_ref, k_ref, v_ref, c_ref, *rest):
        g_in, (o_ref, lse_ref), g_out = rest[:ng], rest[ng:ng + 2], rest[ng + 2:2 * ng + 2]
        m_ref, l_ref, acc_ref = rest[2 * ng + 2:2 * ng + 5]
        step_id = pl.program_id(2)
        qi, ki = qtab_ref[step_id], ktab_ref[step_id]
        if ng:
            sems = rest[2 * ng + 5:]
            outer = (pl.program_id(0), pl.program_id(1))

            @pl.when((outer[0] == 0) & (outer[1] == 0) & (step_id == 0))
            def _():
                for cp in _direct_exchange(g_in, g_out, None, sems, True):
                    cp.start()

            @pl.when((outer[0] == bsz - 1) & (outer[1] == nj - 1) & (step_id == len(pairs) - 1))
            def _():
                for cp in _direct_exchange(g_in, g_out, None, sems, True):
                    cp.wait()

        @pl.when(ki == 0)
        def _():
            m_ref[...] = jnp.full_like(m_ref, NEG)
            l_ref[...] = jnp.zeros_like(l_ref)
            acc_ref[...] = jnp.zeros_like(acc_ref)

        def step(masked):
            qv, kv, vv = q_ref[...], k_ref[...], v_ref[...]
            for h in range(2):
                qh = jnp.where(_head_mask(h), qv, jnp.zeros_like(qv))
                s = lax.dot_general(qh, kv, _NT, preferred_element_type=F32) - c_ref[h:h + 1, :]
                if masked:
                    s = jnp.where(_causal(qi, ki, tq, tk), s, NEG)
                m_prev = m_ref[h]
                m_new = jnp.maximum(m_prev, jnp.max(s, axis=1, keepdims=True))
                pm = jnp.exp(s - m_new)
                alpha = jnp.exp(m_prev - m_new)
                l_ref[h] = alpha * l_ref[h] + jnp.sum(pm, axis=1, keepdims=True)
                p_hi = pm.astype(BF16)
                p_lo = (pm - p_hi.astype(F32)).astype(BF16)
                acc_ref[h] = (alpha * acc_ref[h] + jnp.dot(p_hi, vv, preferred_element_type=F32)
                              + jnp.dot(p_lo, vv, preferred_element_type=F32))
                m_ref[h] = m_new

        pl.when(ki < qi)(functools.partial(step, False))
        pl.when(ki == qi)(functools.partial(step, True))

        @pl.when(ki == qi)
        def _():
            m0 = _head_mask(0)
            o_ref[...] = jnp.where(m0, acc_ref[0] / l_ref[0], acc_ref[1] / l_ref[1])
            lse_ref[...] = jnp.where(m0, m_ref[0] + jnp.log(l_ref[0]), m_ref[1] + jnp.log(l_ref[1]))

    qblk = pl.BlockSpec((None, tq, LANES), lambda bi, j, t, qt, kt: (bi, qt[t], j))
    kblk = pl.BlockSpec((None, tk, LANES), lambda bi, j, t, qt, kt: (bi, kt[t], j))
    hbm = pl.BlockSpec(memory_space=pl.ANY)
    outs = pl.pallas_call(
        body, out_shape=[SDS((bsz, s_len, d), F32), SDS((bsz, nj, s_len, LANES), F32)]
        + [SDS((N_DEV,) + tuple(a.shape), a.dtype) for a in gather], name="flash_fwd",
        grid_spec=pltpu.PrefetchScalarGridSpec(
            num_scalar_prefetch=2, grid=(bsz, nj, len(pairs)),
            in_specs=[qblk, kblk, kblk,
                      pl.BlockSpec((None, None, 2, tk), lambda bi, j, t, qt, kt: (bi, j, 0, kt[t]))] + [hbm] * ng,
            out_specs=[qblk, pl.BlockSpec((None, None, tq, LANES), lambda bi, j, t, qt, kt: (bi, j, qt[t], 0))]
            + [hbm] * ng,
            scratch_shapes=[pltpu.VMEM((2, tq, 1), F32), pltpu.VMEM((2, tq, 1), F32),
                            pltpu.VMEM((2, tq, LANES), F32)] + (_exchange_scratch(ng) if ng else [])),
        compiler_params=_cp("arbitrary", "arbitrary", "arbitrary"))(qtab, ktab, q, k, v, crow, *gather)
    return outs[0], outs[1], list(outs[2:])


def _flash_probs(qv, kv, vv, dov, ov, lse, c_ref, h, mask):
    hm = _head_mask(h)
    qh = jnp.where(hm, qv, jnp.zeros_like(qv))
    s = lax.dot_general(qh, kv, _NT, preferred_element_type=F32) - c_ref[h:h + 1, :]
    pm = jnp.exp(s - lse[:, h * HEAD_DIM:h * HEAD_DIM + 1])
    if mask is not None:
        pm = jnp.where(mask, pm, 0.0)
    doh = jnp.where(hm, dov, jnp.zeros_like(dov))
    dpm = lax.dot_general(doh, vv, _NT, preferred_element_type=F32)
    delta = jnp.sum(jnp.where(hm, dov.astype(F32) * ov, 0.0), axis=1, keepdims=True)
    return pm, pm * (dpm - delta)


def _flash_bwd(q, k, v, do, o, lse, crow, items=(), groups=()):
    bsz, s_len, d = q.shape
    nj = d // LANES
    tq = tk = _tile(s_len, ATT_BLOCK, LANES)
    nq = s_len // tq

    pairs = [(b, a) for b in range(nq) for a in range(b, nq)]
    n_live = len(pairs)
    ktab = jnp.asarray([b for b, _ in pairs] + [nq - 1] * nq, jnp.int32)
    qtab = jnp.asarray([a for _, a in pairs] + list(range(nq)), jnp.int32)

    n_it, n_grp = len(items), len(groups)
    place = {it: (g, l) for g, members in enumerate(groups) for l, it in enumerate(members)}

    def body(ktab_ref, qtab_ref, q_ref, k_ref, v_ref, do_ref, o_ref, lse_ref, c_ref, *rest):
        x_in, (dq_ref, dk_ref, dv_ref, dc_ref) = rest[:n_it], rest[n_it:n_it + 4]
        x_out = rest[n_it + 4:n_it + 4 + n_grp]
        dqa_ref, dka_ref, dva_ref, dca_ref = rest[n_it + 4 + n_grp:n_it + 8 + n_grp]
        step_id = pl.program_id(2)
        ki, qi = ktab_ref[step_id], qtab_ref[step_id]
        live = step_id < n_live
        rows = pl.ds(pl.multiple_of(qi * tq, tq), tq)
        if n_it:
            sems = rest[n_it + 8 + n_grp:]
            outer = (pl.program_id(0), pl.program_id(1))

            @pl.when((outer[0] == 0) & (outer[1] == 0) & (step_id == 0))
            def _():
                for cp in _direct_exchange(x_in, x_out, place, sems, False):
                    cp.start()

            @pl.when((outer[0] == bsz - 1) & (outer[1] == nj - 1) & (step_id == n_live + nq - 1))
            def _():
                for cp in _direct_exchange(x_in, x_out, place, sems, False):
                    cp.wait()

        @pl.when(step_id == 0)
        def _():
            dqa_ref[...] = jnp.zeros_like(dqa_ref)

        @pl.when(live & (qi == ki))
        def _():
            dka_ref[...] = jnp.zeros_like(dka_ref)
            dva_ref[...] = jnp.zeros_like(dva_ref)
            dca_ref[...] = jnp.zeros_like(dca_ref)

        def step(masked):
            qv, kv, vv, dov, ov, lse = q_ref[...], k_ref[...], v_ref[...], do_ref[...], o_ref[...], lse_ref[...]
            mask = _causal(qi, ki, tq, tk) if masked else None
            for h in range(2):
                pm, ds = _flash_probs(qv, kv, vv, dov, ov, lse, c_ref, h, mask)
                dsb = ds.astype(BF16)
                dva_ref[h] += lax.dot_general(pm.astype(BF16), dov, _TN, preferred_element_type=F32)
                dka_ref[h] += lax.dot_general(dsb, qv, _TN, preferred_element_type=F32)
                dqa_ref[h, rows, :] += jnp.dot(dsb, kv, preferred_element_type=F32)
                dca_ref[h:h + 1, :] -= jnp.sum(ds, axis=0, keepdims=True)

        pl.when(live & (qi > ki))(functools.partial(step, False))
        pl.when(live & (qi == ki))(functools.partial(step, True))

        @pl.when(live & (qi == nq - 1))
        def _():
            m0 = _head_mask(0)
            dk_ref[...] = jnp.where(m0, dka_ref[0], dka_ref[1])
            dv_ref[...] = jnp.where(m0, dva_ref[0], dva_ref[1]).astype(BF16)
            dc_ref[...] = dca_ref[0:2, :]

        @pl.when(jnp.logical_not(live))
        def _():
            dq_ref[...] = jnp.where(_head_mask(0), dqa_ref[0, rows, :], dqa_ref[1, rows, :])

    def qside(bi, j, t, kt, qt):
        return (bi, jnp.where(t < n_live, qt[t], nq - 1), j)

    def kside(bi, j, t, kt, qt):
        return (bi, kt[t], j)

    def dqside(bi, j, t, kt, qt):
        return (bi, jnp.where(t < n_live, 0, qt[t]), j)

    qblk, kblk = pl.BlockSpec((None, tq, LANES), qside), pl.BlockSpec((None, tk, LANES), kside)
    cblk = pl.BlockSpec((None, None, 2, tk), lambda bi, j, t, kt, qt: (bi, j, 0, kt[t]))
    hbm = pl.BlockSpec(memory_space=pl.ANY)
    outs = pl.pallas_call(
        body, out_shape=[SDS((bsz, s_len, d), F32), SDS((bsz, s_len, d), F32), SDS((bsz, s_len, d), BF16),
                         SDS((bsz, nj, 2, s_len), F32)]
        + [SDS((N_DEV, len(members)) + tuple(items[members[0]].shape[1:]), items[members[0]].dtype)
           for members in groups], name="flash_bwd",
        grid_spec=pltpu.PrefetchScalarGridSpec(
            num_scalar_prefetch=2, grid=(bsz, nj, n_live + nq),
            in_specs=[qblk, kblk, kblk, qblk, qblk,
                      pl.BlockSpec((None, None, tq, LANES),
                                   lambda bi, j, t, kt, qt: (bi, j, jnp.where(t < n_live, qt[t], nq - 1), 0)),
                      cblk] + [hbm] * n_it,
            out_specs=[pl.BlockSpec((None, tq, LANES), dqside), kblk, kblk, cblk] + [hbm] * n_grp,
            scratch_shapes=[pltpu.VMEM((2, s_len, LANES), F32), pltpu.VMEM((2, tk, LANES), F32),
                            pltpu.VMEM((2, tk, LANES), F32), pltpu.VMEM((8, tk), F32)]
            + (_exchange_scratch(n_it) if n_it else [])),
        compiler_params=_cp("arbitrary", "arbitrary", "arbitrary"))(ktab, qtab, q, k, v, do, o, lse, crow, *items)
    return outs[0], outs[1], outs[2], outs[3], list(outs[4:])


def _loss_head(y, target):
    t, d = y.shape
    tm = _tile(t, 512)

    def body(y_ref, t_ref, dy_ref, acc_ref):
        @pl.when(pl.program_id(0) == 0)
        def _():
            acc_ref[...] = jnp.zeros_like(acc_ref)

        err = y_ref[...] - t_ref[...]
        dy_ref[...] = err / d
        acc_ref[...] += _fold8(err * err)

    row = pl.BlockSpec((tm, d), lambda i: (i, 0))
    return pl.pallas_call(
        body, out_shape=(SDS((t, d), F32), SDS((8, d), F32)), grid=(t // tm,), name="loss_head",
        in_specs=[row, row], out_specs=(row, pl.BlockSpec((8, d), lambda i: (0, 0))),
        compiler_params=_cp("arbitrary"))(y, target)


ADAM_COLS = 1024


def _adamw(g8, w, m, v):
    shape = w.shape
    cols = shape[-1]
    rows = w.size // cols
    n_parts = g8.shape[0]
    g8, w, m, v = g8.reshape(n_parts, rows, cols), w.reshape(rows, cols), m.reshape(rows, cols), v.reshape(rows, cols)
    tr = _tile(rows, 256, 16)
    c1 = 1.0 - ADAM_B1 ** ADAM_STEP
    c2 = 1.0 - ADAM_B2 ** ADAM_STEP

    def body(g8_ref, w_ref, m_ref, v_ref, g_ref, d_ref, nm_ref, nv_ref):
        g = g8_ref[0].astype(F32)
        for i in range(1, n_parts):
            g = g + g8_ref[i].astype(F32)
        mn = ADAM_B1 * m_ref[...] + (1.0 - ADAM_B1) * g
        vn = ADAM_B2 * v_ref[...] + (1.0 - ADAM_B2) * (g * g)
        g_ref[...] = g
        nm_ref[...] = mn
        nv_ref[...] = vn
        d_ref[...] = -ADAM_LR * ((mn / c1) / (jnp.sqrt(vn / c2) + ADAM_EPS) + ADAM_WD * w_ref[...])

    blk = pl.BlockSpec((tr, cols), lambda i: (i, 0))
    outs = pl.pallas_call(
        body, out_shape=(SDS((rows, cols), F32),) * 4, grid=(rows // tr,), name="adamw",
        in_specs=[pl.BlockSpec((n_parts, tr, cols), lambda i: (0, i, 0)), blk, blk, blk], out_specs=(blk,) * 4,
        compiler_params=_cp("parallel"))(g8, w, m, v)
    return [o.reshape(shape) for o in outs]


def _mesh_place():
    x, y, c = lax.axis_index("x"), lax.axis_index("y"), lax.axis_index("c")
    return x, y, c, 4 * x + 2 * y + c


def _gather(shards):
    n = len(shards)

    def body(*refs):
        ins, outs = refs[:n], refs[n:2 * n]
        send_sems, recv_sems, local_sems = refs[2 * n:]
        x, y, c, me = _mesh_place()
        sibling = (x, y, 1 - c)
        chips = [(1 - x, y), (x, 1 - y), (1 - x, 1 - y)]

        def block(px, py, pc):
            return 4 * px + 2 * py + pc

        def copy(t, k, blk, to, src=None):
            return pltpu.make_async_remote_copy(
                src_ref=outs[t].at[blk] if src is None else src, dst_ref=outs[t].at[blk],
                send_sem=send_sems.at[t, k], recv_sem=recv_sems.at[t, k], device_id=to,
                device_id_type=pl.DeviceIdType.MESH)

        own = [pltpu.make_async_copy(ins[t], outs[t].at[me], local_sems.at[t]) for t in range(n)]
        first = []
        for t in range(n):
            own[t].start()
            first.append(copy(t, 0, me, sibling, src=ins[t]))
            first += [copy(t, 1 + j, me, (*chip, c), src=ins[t]) for j, chip in enumerate(chips)]
        for cp in first:
            cp.start()
        passed = []
        for j, chip in enumerate(chips):
            for t in range(n):
                copy(t, 1 + j, block(*chip, c), (x, y, c)).wait_recv()
                cp = copy(t, 4 + j, block(*chip, c), sibling)
                cp.start()
                passed.append(cp)
        for t in range(n):
            copy(t, 0, block(x, y, 1 - c), (x, y, c)).wait_recv()
            for j, chip in enumerate(chips):
                copy(t, 4 + j, block(*chip, 1 - c), (x, y, c)).wait_recv()
        for cp in first + passed:
            cp.wait_send()
        for cp in own:
            cp.wait()

    hbm = pl.BlockSpec(memory_space=pl.ANY)
    return pl.pallas_call(
        body, out_shape=[SDS((N_DEV,) + tuple(s.shape), s.dtype) for s in shards], name="gather",
        in_specs=[hbm] * n, out_specs=[hbm] * n,
        scratch_shapes=[pltpu.SemaphoreType.DMA((n, N_DEV - 1)), pltpu.SemaphoreType.DMA((n, N_DEV - 1)),
                        pltpu.SemaphoreType.DMA((n,))])(*shards)


N_CHIP = N_DEV // 2


def _scatter_core(items):
    n = len(items)

    def body(*refs):
        ins, outs = refs[:n], refs[n:2 * n]
        send_sems, recv_sems = refs[2 * n:]
        x, y, c, _ = _mesh_place()
        copies = []
        for it in range(n):
            for r in range(N_CHIP):
                cp = pltpu.make_async_remote_copy(
                    src_ref=ins[it].at[2 * r + 1 - c], dst_ref=outs[it].at[r], send_sem=send_sems.at[it, r],
                    recv_sem=recv_sems.at[it, r], device_id=(x, y, 1 - c), device_id_type=pl.DeviceIdType.MESH)
                cp.start()
                copies.append(cp)
        for cp in copies:
            cp.wait()

    hbm = pl.BlockSpec(memory_space=pl.ANY)
    return pl.pallas_call(
        body, out_shape=[SDS((N_CHIP,) + tuple(a.shape[1:]), a.dtype) for a in items], name="scatter_core",
        in_specs=[hbm] * n, out_specs=[hbm] * n,
        scratch_shapes=[pltpu.SemaphoreType.DMA((n, N_CHIP)), pltpu.SemaphoreType.DMA((n, N_CHIP))])(*items)


def _pair_add(item, other):
    shape = item.shape[1:]
    cols = shape[-1]
    rows = math.prod(shape) // cols
    tr = _tile(rows, 512, 16)

    def body(x_ref, o_ref, h_ref):
        c = lax.axis_index("c")
        mine = jnp.where(c == 0, x_ref[0].astype(F32), x_ref[1].astype(F32))
        h_ref[...] = (mine + o_ref[...].astype(F32)).astype(item.dtype)

    return pl.pallas_call(
        body, out_shape=SDS((N_CHIP, rows, cols), item.dtype), grid=(N_CHIP, rows // tr), name="pair_add",
        in_specs=[pl.BlockSpec((None, 2, tr, cols), lambda r, i: (r, 0, i, 0)),
                  pl.BlockSpec((None, tr, cols), lambda r, i: (r, i, 0))],
        out_specs=pl.BlockSpec((None, tr, cols), lambda r, i: (r, i, 0)),
        compiler_params=_cp("parallel", "parallel"))(
            item.reshape(N_CHIP, 2, rows, cols), other.reshape(N_CHIP, rows, cols)).reshape((N_CHIP,) + shape)


def _scatter_chip(items, groups):
    n = len(items)
    place = {it: (g, l) for g, members in enumerate(groups) for l, it in enumerate(members)}

    def body(*refs):
        ins, outs = refs[:n], refs[n:n + len(groups)]
        send_sems, recv_sems, local_sems = refs[n + len(groups):]
        x, y, c, _ = _mesh_place()
        chip = 2 * x + y
        copies = []
        for it in range(n):
            g, l = place[it]
            own = pltpu.make_async_copy(ins[it].at[chip], outs[g].at[chip, l], local_sems.at[it])
            own.start()
            copies.append(own)
            for kbits in range(1, N_CHIP):
                px = 1 - x if kbits & 2 else x
                py = 1 - y if kbits & 1 else y
                cp = pltpu.make_async_remote_copy(
                    src_ref=ins[it].at[2 * px + py], dst_ref=outs[g].at[chip, l],
                    send_sem=send_sems.at[it, kbits - 1], recv_sem=recv_sems.at[it, kbits - 1],
                    device_id=(px, py, c), device_id_type=pl.DeviceIdType.MESH)
                cp.start()
                copies.append(cp)
        for cp in copies:
            cp.wait()

    hbm = pl.BlockSpec(memory_space=pl.ANY)
    out_shape = [SDS((N_CHIP, len(members)) + tuple(items[members[0]].shape[1:]), items[members[0]].dtype)
                 for members in groups]
    return pl.pallas_call(
        body, out_shape=out_shape, name="scatter_chip", in_specs=[hbm] * n, out_specs=[hbm] * len(groups),
        scratch_shapes=[pltpu.SemaphoreType.DMA((n, N_CHIP - 1)), pltpu.SemaphoreType.DMA((n, N_CHIP - 1)),
                        pltpu.SemaphoreType.DMA((n,))])(*items)


def _scatter(items, groups):
    halves = _scatter_core(items)
    return _scatter_chip([_pair_add(a, h) for a, h in zip(items, halves)], groups)


def _cat_lanes(g, layer, nb, blk, width):
    _, _, rows, c = g.shape
    tr = _tile(rows, 256, 16)

    def body(g_ref, o_ref):
        for p in range(nb):
            o_ref[:, p * c:(p + 1) * c] = g_ref[p]
        if width > nb * c:
            o_ref[:, nb * c:] = jnp.zeros((tr, width - nb * c), g.dtype)

    return pl.pallas_call(
        body, out_shape=SDS((rows, width), g.dtype), grid=(rows // tr,), name="cat_lanes",
        in_specs=[pl.BlockSpec((nb, None, tr, c), lambda i: (blk, layer, i, 0))],
        out_specs=pl.BlockSpec((tr, width), lambda i: (i, 0)),
        compiler_params=_cp("parallel"))(g)


def _split_lanes(parts, c):
    rows = parts[0].shape[0]
    counts = [p.shape[1] // c for p in parts]
    tr = _tile(rows, 256, 16)

    def body(*refs):
        o_ref = refs[-1]
        q = 0
        for x_ref, cnt in zip(refs[:-1], counts):
            for p in range(cnt):
                o_ref[q] = x_ref[:, p * c:(p + 1) * c]
                q += 1

    return pl.pallas_call(
        body, out_shape=SDS((sum(counts), rows, c), parts[0].dtype), grid=(rows // tr,), name="split_lanes",
        in_specs=[pl.BlockSpec((tr, p.shape[1]), lambda i: (i, 0)) for p in parts],
        out_specs=pl.BlockSpec((sum(counts), tr, c), lambda i: (0, i, 0)),
        compiler_params=_cp("parallel"))(*parts)


def _unshard(g8, shard_shape, axis):
    full = jnp.moveaxis(g8.reshape((N_DEV,) + tuple(shard_shape)), 0, axis)
    shape = list(shard_shape)
    shape[axis] *= N_DEV
    return full.reshape(shape)


def _to_shards(full, axis):
    shape = list(full.shape)
    shape[axis:axis + 1] = [N_DEV, shape[axis] // N_DEV]
    return jnp.moveaxis(full.reshape(shape), axis, 0).reshape(N_DEV, -1)


SMALL = [n for n in SHARDED if n not in MATRICES]


def _flat_rows(parts):
    flat = jnp.concatenate([p.reshape(-1) for p in parts])
    chunk = 8 * ADAM_COLS
    n = -(-flat.shape[0] // chunk) * chunk
    return jnp.pad(flat, (0, n - flat.shape[0])).reshape(n // ADAM_COLS, ADAM_COLS)


def _prepare_vectors(small, shards):
    wt = {}
    flat = small.reshape(N_DEV, -1)
    off = 0
    for n in SMALL:
        size = shards[n].size
        wt[n] = _unshard(flat[:, off:off + size], shards[n].shape, SHARD_AXIS[n])
        off += size
    return wt


def _prepare_matrices(gathered):
    wt = {}
    for n in ('conv_w_out', 'fox_w_o', 'ffn_w_down'):
        if n in gathered:
            g = gathered[n]
            wt[n] = [g[:, l].reshape(N_DEV * g.shape[2], g.shape[3]) for l in range(g.shape[1])]
    if 'pool_w' in gathered:
        g = gathered['pool_w']
        wt['pool_w'] = [jnp.moveaxis(g[:, l], 0, 1).reshape(g.shape[2], N_DEV * g.shape[3], g.shape[4])
                        for l in range(g.shape[1])]
    if 'conv_w_in' in gathered:
        g = gathered['conv_w_in']
        wt['conv_w_in'] = [_cat_lanes(g, l, N_DEV, 0, N_DEV * g.shape[3]) for l in range(g.shape[1])]
    if 'fox_w_in' in gathered:
        g = gathered['fox_w_in']
        wt['fox_w_in'] = [_cat_lanes(g, l, N_DEV, 0, 3 * g.shape[2] + LANES) for l in range(g.shape[1])]
    if 'ffn_w_up' in gathered:
        g = gathered['ffn_w_up']
        half = N_DEV // 2
        wt['ffn_w_up_v'] = [_cat_lanes(g, l, half, 0, half * g.shape[3]) for l in range(g.shape[1])]
        wt['ffn_w_up_g'] = [_cat_lanes(g, l, half, 1, half * g.shape[3]) for l in range(g.shape[1])]
    return wt


def _pad_rows(w, rows):
    return jnp.pad(w, ((0, rows - w.shape[0]), (0, 0)))


def _fold(acc):
    return acc.sum(axis=0)


def _local_step(x, target, wt, late_shards=None, cut=None):
    wt = dict(wt)
    late_names = [n for n in MATRICES if late_shards and n in late_shards]
    late_recv = {}
    bsz, s_len, d = x.shape
    t = bsz * s_len
    depth = wt['norm_mix'].shape[0]
    n_heads = d // HEAD_DIM
    f = wt['ffn_w_up_v'][0].shape[1]
    row = lambda a: a.reshape(1, -1)
    grads = {n: {} for n in WEIGHTS}
    saved = []

    xc = x.reshape(t, d)
    hn_next = None
    for i in range(depth):
        j = i // 3
        kind = i % 3
        sv = {'x_mix': xc}
        gm = row(wt['norm_mix'][i])
        gf = row(wt['norm_ffn'][i])
        if kind == 0:
            hn = hn_next if hn_next is not None else _rmsnorm_fwd(xc, gm)
            p = _mm(hn, wt['conv_w_in'][j], bias=row(wt['conv_b_in'][j]), name="conv_in")
            u, sact = _conv_act_fwd(p.reshape(bsz, s_len, 2 * d), _pad_rows(wt['conv_dw'][j], 32),
                                    row(wt['conv_dw_b'][j]), row(wt['conv_ln_g'][j]), row(wt['conv_ln_b'][j]))
            sact = sact.reshape(t, d)
            xn, hf = _mm(sact, wt['conv_w_out'][j], bias=row(wt['conv_b_out'][j]), residual=xc, norm_gain=gf,
                         out_dtype=F32, name="conv_out")
            sv.update(hn=hn, p=p, u=u.reshape(t, d), sact=sact)
        elif kind == 1:
            xn, pp = _pool_fwd(xc.reshape(bsz, s_len, d), gm, wt['pool_w'][j], row(wt['pool_b'][j]),
                               row(wt['pool_scale'][j]))
            xn = xn.reshape(t, d)
            hf = _rmsnorm_fwd(xn, gf)
            sv.update(p=pp)
        else:
            hn = hn_next if hn_next is not None else _rmsnorm_fwd(xc, gm)
            wp = wt['fox_w_in'][j]
            bf = jnp.pad(wt['fox_b_f'][j], (0, LANES - n_heads)).reshape(1, LANES)
            gq = jnp.tile(wt['fox_q_gain'][j], n_heads).reshape(1, d)
            gk = jnp.tile(wt['fox_k_gain'][j], n_heads).reshape(1, d)
            proj = _mm(hn, wp, out_dtype=F32, name="fox_in")
            c = _fox_gate_fwd(proj.reshape(bsz, s_len, -1), bf, n_heads)
            crow = jnp.swapaxes(c, 1, 2)[:, :n_heads].reshape(bsz, n_heads // 2, 2, s_len)
            qn, kn, vb = _fox_qknorm_fwd(proj, gq, gk, d)
            shp = (bsz, s_len, d)
            o, lse, got = _flash_fwd(qn.reshape(shp), kn.reshape(shp), vb.reshape(shp), crow,
                                     gather=[late_shards[n] for n in late_names])
            for key, layers in _prepare_matrices(dict(zip(late_names, got))).items():
                wt[key] = wt[key] + layers
            o = o.reshape(t, d)
            xn, hf = _mm(o, wt['fox_w_o'][j], residual=xc, norm_gain=gf, out_dtype=F32, name="fox_out")
            sv.update(hn=hn, wp=wp, bf=bf, gq=gq, gk=gk, proj=proj, crow=crow, qn=qn, kn=kn, vb=vb, o=o, lse=lse)
        xc = xn
        sv['x_ffn'] = xc
        shf = (bsz, s_len, f)
        uv = _mm(hf, wt['ffn_w_up_v'][i], name="ffn_up").reshape(shf)
        ug = _mm(hf, wt['ffn_w_up_g'][i], name="ffn_up").reshape(shf)
        dw8 = _pad_rows(wt['ffn_dw'][i], 8)
        af, vv, vg = _ffn_act_fwd(uv, ug, dw8, row(wt['ffn_dw_b'][i]))
        af = af.reshape(t, f)
        if i + 1 < depth and (i + 1) % 3 != 1:
            xc, hn_next = _mm(af, wt['ffn_w_down'][i], residual=xc, norm_gain=row(wt['norm_mix'][i + 1]),
                              out_dtype=F32, name="ffn_down")
        else:
            xc, hn_next = _mm(af, wt['ffn_w_down'][i], residual=xc, out_dtype=F32, name="ffn_down"), None
        sv.update(hf=hf, uv=uv, ug=ug, vv=vv, vg=vg, af=af, dw8=dw8)
        saved.append(sv)

    dx, sq = _loss_head(xc, target.reshape(t, d))

    for i in reversed(range(depth)):
        j = i // 3
        kind = i % 3
        sv = saved[i]
        shf = (bsz, s_len, f)
        da = _mm(dx, wt['ffn_w_down'][i], trans_b=True, name="ffn_down_dgrad")
        gw, _ = _wgrad(sv['af'], dx, name="ffn_down_wgrad")
        grads['ffn_w_down'][i] = gw.reshape(N_DEV, f // N_DEV, d)
        dvv, dvg, dbv, dbg = _ffn_act_bwd1(sv['vv'], sv['vg'], da.reshape(shf))
        grads['ffn_dw_b'][i] = jnp.concatenate([_fold(dbv), _fold(dbg)])
        duv, dug, ddwv, ddwg = _ffn_act_bwd2(dvv, dvg, sv['uv'], sv['ug'], sv['dw8'])
        grads['ffn_dw'][i] = jnp.concatenate([ddwv.sum(axis=1), ddwg.sum(axis=1)], axis=1)
        duv, dug = duv.reshape(t, f), dug.reshape(t, f)
        gv, _ = _wgrad(sv['hf'], duv, name="ffn_up_wgrad")
        gg, _ = _wgrad(sv['hf'], dug, name="ffn_up_wgrad")
        grads['ffn_w_up'][i] = _split_lanes([gv, gg], 2 * f // N_DEV)
        dx, dg = _mm(duv, wt['ffn_w_up_v'][i], trans_b=True, a2=dug, b2=wt['ffn_w_up_g'][i],
                     norm_bwd=(sv['x_ffn'], row(wt['norm_ffn'][i]), dx), name="ffn_up_dgrad")
        grads['norm_ffn'][i] = _fold(dg)
        gm = row(wt['norm_mix'][i])
        if kind == 0:
            dsact = _mm(dx, wt['conv_w_out'][j], trans_b=True, name="conv_out_dgrad")
            gw, cs = _wgrad(sv['sact'], dx, name="conv_out_wgrad")
            grads['conv_w_out'][j] = gw.reshape(N_DEV, d // N_DEV, d)
            grads['conv_b_out'][j] = _fold(cs)
            du, dlg, dlb, dwb = _conv_act_bwd1(sv['u'], dsact, row(wt['conv_ln_g'][j]), row(wt['conv_ln_b'][j]))
            grads['conv_ln_g'][j], grads['conv_ln_b'][j], grads['conv_dw_b'][j] = _fold(dlg), _fold(dlb), _fold(dwb)
            dp, ddw = _conv_act_bwd2(du.reshape(bsz, s_len, d), sv['p'].reshape(bsz, s_len, 2 * d),
                                     _pad_rows(wt['conv_dw'][j], 32))
            grads['conv_dw'][j] = ddw.sum(axis=1)[:wt['conv_dw'].shape[1]]
            dp = dp.reshape(t, 2 * d)
            gw, cs = _wgrad(sv['hn'], dp, name="conv_in_wgrad")
            grads['conv_w_in'][j] = _split_lanes([gw], 2 * d // N_DEV)
            grads['conv_b_in'][j] = _fold(cs)
            dx, dg = _mm(dp, wt['conv_w_in'][j], trans_b=True, norm_bwd=(sv['x_mix'], gm, dx), name="conv_in_dgrad")
            grads['norm_mix'][i] = _fold(dg)
        elif kind == 1:
            shp = (bsz, s_len, d)
            dxn, dwp, dbp, dsc, dg = _pool_bwd(sv['x_mix'].reshape(shp), dx.reshape(shp), sv['p'], gm, wt['pool_w'][j],
                                               row(wt['pool_b'][j]), row(wt['pool_scale'][j]))
            dx = dxn.reshape(t, d)
            ng, cg = dwp.shape[0], dwp.shape[1]
            grads['pool_w'][j] = jnp.moveaxis(dwp.reshape(ng, N_DEV, cg // N_DEV, cg), 1, 0).astype(BF16)
            grads['pool_b'][j] = _fold(dbp).reshape(wt['pool_b'].shape[1:])
            grads['pool_scale'][j] = _fold(dsc)
            grads['norm_mix'][i] = _fold(dg)
        else:
            shp = (bsz, s_len, d)
            do = _mm(dx, wt['fox_w_o'][j], trans_b=True, name="fox_out_dgrad")
            gw, _ = _wgrad(sv['o'], dx, name="fox_out_wgrad")
            grads['fox_w_o'][j] = gw.reshape(N_DEV, d // N_DEV, d)
            fl_args = (sv['qn'].reshape(shp), sv['kn'].reshape(shp), sv['vb'].reshape(shp), do.reshape(shp),
                       sv['o'].reshape(shp), sv['lse'], sv['crow'])
            items, groups = [], []
            for n in late_names:
                members = [grads[n][l] for l in sorted(grads[n]) if l >= cut[n]]
                groups.append(list(range(len(items), len(items) + len(members))))
                items += members
            dq, dk, dv, dcrow, got = _flash_bwd(*fl_args, items=items, groups=groups)
            late_recv = dict(zip(late_names, got))
            dc = jnp.swapaxes(dcrow.reshape(bsz, n_heads, s_len), 1, 2)
            dc = jnp.pad(dc, ((0, 0), (0, 0), (0, LANES - n_heads)))
            dfl, dbf = _fox_gate_bwd(dc, sv['proj'].reshape(bsz, s_len, -1), sv['bf'], n_heads)
            grads['fox_b_f'][j] = _fold(dbf)[:n_heads]
            dqkv, dgq, dgk = _fox_qknorm_bwd(sv['proj'], dq.reshape(t, d), dk.reshape(t, d), dv.reshape(t, d),
                                             sv['gq'], sv['gk'], d)
            grads['fox_q_gain'][j] = _fold(dgq).reshape(n_heads, HEAD_DIM).sum(axis=0)
            grads['fox_k_gain'][j] = _fold(dgk).reshape(n_heads, HEAD_DIM).sum(axis=0)
            dproj = jnp.concatenate([dqkv, dfl.reshape(t, LANES)], axis=1)
            dwp, _ = _wgrad(sv['hn'], dproj, name="fox_in_wgrad")
            grads['fox_w_in'][j] = _split_lanes([dwp], (3 * d + n_heads) // N_DEV)
            dx, dg = _mm(dproj, sv['wp'], trans_b=True, norm_bwd=(sv['x_mix'], gm, dx), name="fox_in_dgrad")
            grads['norm_mix'][i] = _fold(dg)

    small = {n: jnp.stack([g[k] for k in sorted(g)]) for n, g in grads.items() if n not in MATRICES}
    big = {n: [grads[n][k] for k in sorted(grads[n]) if n not in late_recv or k < cut[n]] for n in MATRICES}
    return sq.sum(), dx.reshape(bsz, s_len, d), small, big, late_recv


def _train_step(x, target, w, m, v):
    depth = w['norm_mix'].shape[0]
    attn = [i for i in range(depth) if i % 3 == 2]
    cut = {n: w[n].shape[0] for n in MATRICES}
    if len(attn) == 1:
        cut['ffn_w_up'] = cut['ffn_w_down'] = attn[0]
        later_conv = [i // 3 for i in range(attn[0] + 1, depth) if i % 3 == 0]
        if later_conv:
            cut['conv_w_in'] = cut['conv_w_out'] = later_conv[0]
    late_shards = {n: w[n][cut[n]:].astype(BF16) for n in MATRICES if 0 < cut[n] < w[n].shape[0]}
    got = _gather([w[n][:cut[n]].astype(BF16) for n in MATRICES] + [_flat_rows([w[n] for n in SMALL])])
    wt = _prepare_matrices(dict(zip(MATRICES, got[:-1])))
    wt.update(_prepare_vectors(got[-1], w))
    wt.update({n: w[n] for n in REPLICATED})
    sq, grad_x, gsmall, gbig, late_recv = _local_step(x, target, wt, late_shards, cut)
    d = x.shape[-1]

    shard_rows = jnp.concatenate([_to_shards(gsmall[n], SHARD_AXIS[n]) for n in SMALL], axis=1)
    rep = jnp.concatenate([gsmall[n].reshape(-1) for n in REPLICATED] + [(0.5 / d) * sq.reshape(1)])
    rows = jnp.concatenate([shard_rows, jnp.broadcast_to(rep, (N_DEV, rep.shape[0]))], axis=1)
    chunk = 8 * ADAM_COLS
    n_all = rows.shape[1]
    n_pad = -(-n_all // chunk) * chunk
    rows = jnp.pad(rows, ((0, 0), (0, n_pad - n_all))).reshape(N_DEV, n_pad // ADAM_COLS, ADAM_COLS)

    items, groups = [], []
    for n in MATRICES:
        groups.append(list(range(len(items), len(items) + len(gbig[n]))))
        items += gbig[n]
    groups.append([len(items)])
    items.append(rows)
    recv = _scatter(items, groups)

    res = [{}, {}, {}, {}]
    for n, r in zip(MATRICES, recv[:-1]):
        if n in late_recv:
            c = cut[n]
            outs = zip(_adamw(r, w[n][:c], m[n][:c], v[n][:c]), _adamw(late_recv[n], w[n][c:], m[n][c:], v[n][c:]))
            outs = [jnp.concatenate(pair, axis=0) for pair in outs]
        else:
            outs = _adamw(r, w[n], m[n], v[n])
        for k, o in enumerate(outs):
            res[k][n] = o
    order = SMALL + REPLICATED

    def flat(tree):
        parts = jnp.concatenate([tree[n].reshape(-1) for n in order])
        return jnp.pad(parts, (0, n_pad - parts.shape[0])).reshape(n_pad // ADAM_COLS, ADAM_COLS)

    outs = [o.reshape(-1) for o in _adamw(recv[-1].reshape((N_CHIP,) + rows.shape[1:]), flat(w), flat(m), flat(v))]
    off = 0
    for n in order:
        size = w[n].size
        for k in range(4):
            res[k][n] = outs[k][off:off + size].reshape(w[n].shape)
        off += size
    loss = outs[0][n_all - 1]
    return (loss, grad_x, *[res[0][n] for n in WEIGHTS], *[res[1][n] for n in WEIGHTS],
            *[res[2][n] for n in WEIGHTS], *[res[3][n] for n in WEIGHTS])


def kernel(x, norm_mix, norm_ffn, conv_w_in, conv_b_in, conv_dw, conv_dw_b, conv_ln_g, conv_ln_b, conv_w_out, conv_b_out, pool_w, pool_b, pool_scale, fox_w_in, fox_b_f, fox_q_gain, fox_k_gain, fox_w_o, ffn_w_up, ffn_dw, ffn_dw_b, ffn_w_down, loss_target, m_norm_mix, m_norm_ffn, m_conv_w_in, m_conv_b_in, m_conv_dw, m_conv_dw_b, m_conv_ln_g, m_conv_ln_b, m_conv_w_out, m_conv_b_out, m_pool_w, m_pool_b, m_pool_scale, m_fox_w_in, m_fox_b_f, m_fox_q_gain, m_fox_k_gain, m_fox_w_o, m_ffn_w_up, m_ffn_dw, m_ffn_dw_b, m_ffn_w_down, v_norm_mix, v_norm_ffn, v_conv_w_in, v_conv_b_in, v_conv_dw, v_conv_dw_b, v_conv_ln_g, v_conv_ln_b, v_conv_w_out, v_conv_b_out, v_pool_w, v_pool_b, v_pool_scale, v_fox_w_in, v_fox_b_f, v_fox_q_gain, v_fox_k_gain, v_fox_w_o, v_ffn_w_up, v_ffn_dw, v_ffn_dw_b, v_ffn_w_down):
    w = dict(zip(WEIGHTS, (norm_mix, norm_ffn, conv_w_in, conv_b_in, conv_dw, conv_dw_b, conv_ln_g, conv_ln_b, conv_w_out, conv_b_out, pool_w, pool_b, pool_scale, fox_w_in, fox_b_f, fox_q_gain, fox_k_gain, fox_w_o, ffn_w_up, ffn_dw, ffn_dw_b, ffn_w_down)))
    m = dict(zip(WEIGHTS, (m_norm_mix, m_norm_ffn, m_conv_w_in, m_conv_b_in, m_conv_dw, m_conv_dw_b, m_conv_ln_g, m_conv_ln_b, m_conv_w_out, m_conv_b_out, m_pool_w, m_pool_b, m_pool_scale, m_fox_w_in, m_fox_b_f, m_fox_q_gain, m_fox_k_gain, m_fox_w_o, m_ffn_w_up, m_ffn_dw, m_ffn_dw_b, m_ffn_w_down)))
    v = dict(zip(WEIGHTS, (v_norm_mix, v_norm_ffn, v_conv_w_in, v_conv_b_in, v_conv_dw, v_conv_dw_b, v_conv_ln_g, v_conv_ln_b, v_conv_w_out, v_conv_b_out, v_pool_w, v_pool_b, v_pool_scale, v_fox_w_in, v_fox_b_f, v_fox_q_gain, v_fox_k_gain, v_fox_w_o, v_ffn_w_up, v_ffn_dw, v_ffn_dw_b, v_ffn_w_down)))
    return _train_step(x, loss_target, w, m, v)
```

```python
import functools
import math

import jax
import jax.numpy as jnp
from jax import lax
from jax.experimental import pallas as pl
from jax.experimental.pallas import tpu as pltpu

F32, BF16 = jnp.float32, jnp.bfloat16
SDS = jax.ShapeDtypeStruct

N_DEV = 8
EPS = 1e-6
POOL_WINDOWS = (2, 4, 8, 16)
HEAD_DIM = 64
ADAM_LR, ADAM_B1, ADAM_B2, ADAM_EPS, ADAM_WD, ADAM_STEP = 0.001, 0.9, 0.999, 1e-08, 0.01, 10
LANES = 128
VMEM_LIMIT_BYTES = 48 * 1024 * 1024
RESIDENT_WEIGHT_BYTES = 6 * 1024 * 1024
NEG = -1e30

WEIGHTS = ['norm_mix', 'norm_ffn', 'conv_w_in', 'conv_b_in', 'conv_dw', 'conv_dw_b', 'conv_ln_g', 'conv_ln_b',
           'conv_w_out', 'conv_b_out', 'pool_w', 'pool_b', 'pool_scale', 'fox_w_in', 'fox_b_f', 'fox_q_gain',
           'fox_k_gain', 'fox_w_o', 'ffn_w_up', 'ffn_dw', 'ffn_dw_b', 'ffn_w_down']
SHARD_AXIS = {'conv_w_in': 2, 'conv_b_in': 1, 'conv_dw': 2, 'conv_dw_b': 1, 'conv_ln_g': 1, 'conv_ln_b': 1,
              'conv_w_out': 1, 'conv_b_out': 1, 'pool_w': 2, 'pool_b': 2, 'fox_w_in': 2, 'fox_w_o': 1,
              'ffn_w_up': 2, 'ffn_dw': 2, 'ffn_w_down': 1}
MATRICES = ('conv_w_in', 'conv_w_out', 'pool_w', 'fox_w_in', 'fox_w_o', 'ffn_w_up', 'ffn_w_down')
SHARDED = [n for n in WEIGHTS if n in SHARD_AXIS]
REPLICATED = [n for n in WEIGHTS if n not in SHARD_AXIS]


def _cp(*sem):
    return pltpu.CompilerParams(dimension_semantics=sem, vmem_limit_bytes=VMEM_LIMIT_BYTES)


def _tile(n, pref, align=8):
    if n <= pref:
        return n
    t = (pref // align) * align
    while t >= align:
        if n % t == 0:
            return t
        t -= align
    return n


def _fold8(x):
    r, c = x.shape
    return x.reshape(r // 8, 8, c).sum(axis=0)


def _sigmoid(x):
    return 0.5 * jnp.tanh(0.5 * x) + 0.5


def _shifts_back(cur, tail, n):
    hb = tail.shape[0]
    xe = jnp.concatenate([tail, cur], axis=0)
    return [cur] + [pltpu.roll(xe, j, axis=0)[hb:] for j in range(1, n)]


def _shifts_fwd(cur, head, n):
    ts = cur.shape[0]
    xe = jnp.concatenate([cur, head], axis=0)
    ln = xe.shape[0]
    return [cur] + [pltpu.roll(xe, ln - j, axis=0)[:ts] for j in range(1, n)]


def _dot_hi(a, b):
    return jnp.dot(a, b, preferred_element_type=F32, precision=lax.Precision.HIGHEST)


def _rmsnorm_fwd(x, g):
    t, d = x.shape
    tm = _tile(t, 512)

    def body(x_ref, g_ref, h_ref):
        xv = x_ref[...]
        r = lax.rsqrt(jnp.mean(xv * xv, axis=-1, keepdims=True) + EPS)
        h_ref[...] = (xv * r * g_ref[...]).astype(BF16)

    return pl.pallas_call(
        body, out_shape=SDS((t, d), BF16), grid=(t // tm,), name="rmsnorm_fwd",
        in_specs=[pl.BlockSpec((tm, d), lambda i: (i, 0)), pl.BlockSpec((1, d), lambda i: (0, 0))],
        out_specs=pl.BlockSpec((tm, d), lambda i: (i, 0)), compiler_params=_cp("parallel"))(x, g)


def _mm(a, b, *, trans_b=False, bias=None, residual=None, a2=None, b2=None, norm_gain=None, norm_bwd=None,
        out_dtype=BF16, name="mm"):
    m, k = a.shape
    n = b.shape[0] if trans_b else b.shape[1]
    tm, tn, tk = _tile(m, 1024, 16), _tile(n, 1536, LANES), _tile(k, 1536, LANES)
    small_b = k * n * 2 <= RESIDENT_WEIGHT_BYTES
    if norm_bwd is not None:
        if a2 is not None:
            tm, tk = _tile(m, 256, 16), k
        elif small_b:
            tm, tk = _tile(m, 512, 16), k
        else:
            tk = _tile(k, 768, LANES)
    elif small_b and tk < k:
        tm, tk = _tile(m, 512, 16), k
    nk = k // tk
    two = a2 is not None
    steps = 2 * nk if two else nk
    dims = (((1,), (1,)), ((), ())) if trans_b else (((1,), (0,)), ((), ()))
    has_bias, has_res, has_norm, has_nbwd = bias is not None, residual is not None, norm_gain is not None, \
        norm_bwd is not None
    assert not (has_norm or has_nbwd) or tn == n
    n_out = 2 if (has_norm or has_nbwd) else 1

    def body(*refs):
        pos = 4 if two else 2
        bias_ref = refs[pos] if has_bias else None
        pos += has_bias
        res_ref = refs[pos] if has_res else None
        pos += has_res
        gain_ref = refs[pos] if has_norm else None
        pos += has_norm
        x_ref, g_ref, dres_ref = refs[pos:pos + 3] if has_nbwd else (None, None, None)
        pos += 3 * has_nbwd
        outs = refs[pos:pos + n_out]
        first = (pl.program_id(0) == 0) & (pl.program_id(1) == 0) & (pl.program_id(2) == 0)

        def finish(r):
            if has_bias:
                r = r + bias_ref[...]
            if has_res:
                r = r + res_ref[...]
            if has_nbwd:
                xv = x_ref[...]
                rs = lax.rsqrt(jnp.mean(xv * xv, axis=-1, keepdims=True) + EPS)
                xh = xv * rs
                u = r * g_ref[...]
                outs[0][...] = dres_ref[...] + rs * (u - xh * jnp.mean(u * xh, axis=-1, keepdims=True))
                outs[1][...] += _fold8(r * xh)
                return
            outs[0][...] = r.astype(out_dtype)
            if has_norm:
                outs[1][...] = (r * lax.rsqrt(jnp.mean(r * r, axis=-1, keepdims=True) + EPS)
                                * gain_ref[...]).astype(BF16)

        def dot(a_ref, b_ref):
            return lax.dot_general(a_ref[...].astype(BF16), b_ref[...].astype(BF16), dims, preferred_element_type=F32)

        if has_nbwd:
            @pl.when(first)
            def _():
                outs[1][...] = jnp.zeros_like(outs[1])

        if steps == 1:
            finish(dot(refs[0], refs[1]))
            return
        acc_ref = refs[-1]
        kk = pl.program_id(2)

        @pl.when(kk == 0)
        def _():
            acc_ref[...] = jnp.zeros_like(acc_ref)

        @pl.when(kk < nk)
        def _():
            acc_ref[...] += dot(refs[0], refs[1])

        if two:
            @pl.when(kk >= nk)
            def _():
                acc_ref[...] += dot(refs[2], refs[3])

        @pl.when(kk == steps - 1)
        def _():
            finish(acc_ref[...])

    def pair(first):
        kmap = (lambda kk: jnp.minimum(kk, nk - 1)) if first else (lambda kk: jnp.maximum(kk - nk, 0))
        a_spec = pl.BlockSpec((tm, tk), lambda j, i, kk: (i, kmap(kk)))
        if trans_b:
            b_spec = pl.BlockSpec((tn, tk), lambda j, i, kk: (j, kmap(kk)))
        else:
            b_spec = pl.BlockSpec((tk, tn), lambda j, i, kk: (kmap(kk), j))
        return [a_spec, b_spec]

    in_specs, args = pair(True), [a, b]
    if two:
        in_specs += pair(False)
        args += [a2, b2]
    if has_bias:
        in_specs.append(pl.BlockSpec((1, tn), lambda j, i, kk: (0, j)))
        args.append(bias)
    tile = pl.BlockSpec((tm, tn), lambda j, i, kk: (i, j))
    vec = pl.BlockSpec((1, tn), lambda j, i, kk: (0, j))
    if has_res:
        in_specs.append(tile)
        args.append(residual)
    if has_norm:
        in_specs.append(vec)
        args.append(norm_gain)
    if has_nbwd:
        in_specs += [tile, vec, tile]
        args += list(norm_bwd)
    out_shape, out_specs = [SDS((m, n), F32 if has_nbwd else out_dtype)], [tile]
    if has_norm:
        out_shape.append(SDS((m, n), BF16))
        out_specs.append(tile)
    if has_nbwd:
        out_shape.append(SDS((8, n), F32))
        out_specs.append(pl.BlockSpec((8, tn), lambda j, i, kk: (0, 0)))
    outs = pl.pallas_call(
        body, out_shape=out_shape, grid=(n // tn, m // tm, steps), name=name, in_specs=in_specs, out_specs=out_specs,
        scratch_shapes=[] if steps == 1 else [pltpu.VMEM((tm, tn), F32)],
        compiler_params=_cp("arbitrary", "arbitrary", "arbitrary"))(*args)
    return outs[0] if n_out == 1 else tuple(outs)


def _wgrad(a, g, *, out_dtype=BF16, name="wgrad"):
    m, ka = a.shape
    n = g.shape[1]
    ta, tn, tm = _tile(ka, 1536, LANES), _tile(n, 1536, LANES), _tile(m, 1024)
    nm = m // tm

    def body(a_ref, g_ref, o_ref, cs_ref, acc_ref):
        i, mm = pl.program_id(1), pl.program_id(2)

        @pl.when(mm == 0)
        def _():
            acc_ref[...] = jnp.zeros_like(acc_ref)

        @pl.when((mm == 0) & (i == 0))
        def _():
            cs_ref[...] = jnp.zeros_like(cs_ref)

        gv = g_ref[...]
        acc_ref[...] += lax.dot_general(a_ref[...].astype(BF16), gv.astype(BF16), (((0,), (0,)), ((), ())),
                                        preferred_element_type=F32)

        @pl.when(i == 0)
        def _():
            cs_ref[...] += _fold8(gv.astype(F32))

        @pl.when(mm == nm - 1)
        def _():
            o_ref[...] = acc_ref[...].astype(out_dtype)

    return pl.pallas_call(
        body, out_shape=(SDS((ka, n), out_dtype), SDS((8, n), F32)), grid=(n // tn, ka // ta, nm), name=name,
        in_specs=[pl.BlockSpec((tm, ta), lambda j, i, mm: (mm, i)), pl.BlockSpec((tm, tn), lambda j, i, mm: (mm, j))],
        out_specs=(pl.BlockSpec((ta, tn), lambda j, i, mm: (i, j)), pl.BlockSpec((8, tn), lambda j, i, mm: (0, j))),
        scratch_shapes=[pltpu.VMEM((ta, tn), F32)],
        compiler_params=_cp("arbitrary", "arbitrary", "arbitrary"))(a, g)


FFN_HALO = 16


def _ffn_conv(uc_ref, up_ref, w_ref, b_ref, s):
    u = uc_ref[...].astype(F32)
    tail = jnp.where(s > 0, up_ref[...].astype(F32), 0.0)
    sh = _shifts_back(u, tail, 3)
    return sh, sh[2] * w_ref[0:1, :] + sh[1] * w_ref[1:2, :] + sh[0] * w_ref[2:3, :] + b_ref[...]


def _ffn_act_fwd(uv, ug, dw8, b):
    bsz, s_len, f = uv.shape
    tc, ts = f, _tile(s_len, 256, FFN_HALO)
    nf, r = f // tc, ts // FFN_HALO

    def body(uv_ref, uvp_ref, ug_ref, ugp_ref, wv_ref, wg_ref, bv_ref, bg_ref, a_ref, vv_ref, vg_ref):
        s = pl.program_id(2)
        _, val = _ffn_conv(uv_ref, uvp_ref, wv_ref, bv_ref, s)
        _, gate = _ffn_conv(ug_ref, ugp_ref, wg_ref, bg_ref, s)
        a_ref[...] = (gate * _sigmoid(gate) * val).astype(BF16)
        vv_ref[...] = val.astype(BF16)
        vg_ref[...] = gate.astype(BF16)

    cur = pl.BlockSpec((None, ts, tc), lambda bi, j, s: (bi, s, j))
    prev = pl.BlockSpec((None, FFN_HALO, tc), lambda bi, j, s: (bi, jnp.maximum(s * r - 1, 0), j))

    def par(rows, off):
        return pl.BlockSpec((rows, tc), lambda bi, j, s: (0, j + off))

    return pl.pallas_call(
        body, out_shape=(SDS((bsz, s_len, f), BF16),) * 3, grid=(bsz, nf, s_len // ts), name="ffn_act_fwd",
        in_specs=[cur, prev, cur, prev, par(8, 0), par(8, nf), par(1, 0), par(1, nf)], out_specs=(cur, cur, cur),
        compiler_params=_cp("parallel", "parallel", "arbitrary"))(uv, uv, ug, ug, dw8, dw8, b, b)


def _ffn_act_bwd1(vv, vg, da):
    bsz, s_len, f = vv.shape
    tc, ts = f, _tile(s_len, 256, FFN_HALO)
    nf = f // tc

    def body(vv_ref, vg_ref, da_ref, dvv_ref, dvg_ref, dbv_ref, dbg_ref):
        @pl.when((pl.program_id(1) == 0) & (pl.program_id(2) == 0))
        def _():
            dbv_ref[...] = jnp.zeros_like(dbv_ref)
            dbg_ref[...] = jnp.zeros_like(dbg_ref)

        val, gate = vv_ref[...].astype(F32), vg_ref[...].astype(F32)
        sg = _sigmoid(gate)
        dav = da_ref[...].astype(F32)
        dval = dav * gate * sg
        dgate = dav * val * (sg * (1.0 + gate * (1.0 - sg)))
        dvv_ref[...] = dval.astype(BF16)
        dvg_ref[...] = dgate.astype(BF16)
        dbv_ref[...] += _fold8(dval)
        dbg_ref[...] += _fold8(dgate)

    cur = pl.BlockSpec((None, ts, tc), lambda j, bi, s: (bi, s, j))
    acc1 = pl.BlockSpec((8, tc), lambda j, bi, s: (0, j))
    return pl.pallas_call(
        body, out_shape=(SDS((bsz, s_len, f), BF16), SDS((bsz, s_len, f), BF16), SDS((8, f), F32), SDS((8, f), F32)),
        grid=(nf, bsz, s_len // ts), name="ffn_act_bwd1", in_specs=[cur, cur, cur], out_specs=(cur, cur, acc1, acc1),
        compiler_params=_cp("arbitrary", "arbitrary", "arbitrary"))(vv, vg, da)


def _ffn_act_bwd2(dvv, dvg, uv, ug, dw8):
    bsz, s_len, f = dvv.shape
    tc, ts = f, _tile(s_len, 256, FFN_HALO)
    nf, r, ns = f // tc, ts // FFN_HALO, s_len // ts

    def body(vc_ref, vn_ref, gc_ref, gn_ref, uv_ref, ug_ref, wv_ref, wg_ref, duv_ref, dug_ref, ddwv_ref, ddwg_ref):
        bi, s = pl.program_id(1), pl.program_id(2)

        @pl.when((bi == 0) & (s == 0))
        def _():
            ddwv_ref[...] = jnp.zeros_like(ddwv_ref)
            ddwg_ref[...] = jnp.zeros_like(ddwg_ref)

        for dc_ref, dn_ref, u_ref, w_ref, du_ref, ddw_ref in (
                (vc_ref, vn_ref, uv_ref, wv_ref, duv_ref, ddwv_ref),
                (gc_ref, gn_ref, ug_ref, wg_ref, dug_ref, ddwg_ref)):
            d = dc_ref[...].astype(F32)
            head = jnp.where(s < ns - 1, dn_ref[...].astype(F32), 0.0)
            sh = _shifts_fwd(d, head, 3)
            du_ref[...] = (sh[0] * w_ref[2:3, :] + sh[1] * w_ref[1:2, :] + sh[2] * w_ref[0:1, :]).astype(BF16)
            u = u_ref[...].astype(F32)
            for j in range(3):
                ddw_ref[2 - j] += _fold8(sh[j] * u)

    cur = pl.BlockSpec((None, ts, tc), lambda j, bi, s: (bi, s, j))
    nxt = pl.BlockSpec((None, FFN_HALO, tc),
                       lambda j, bi, s: (bi, jnp.minimum((s + 1) * r, s_len // FFN_HALO - 1), j))

    def par(off):
        return pl.BlockSpec((8, tc), lambda j, bi, s: (0, j + off))

    acc3 = pl.BlockSpec((3, 8, tc), lambda j, bi, s: (0, 0, j))
    return pl.pallas_call(
        body, out_shape=(SDS((bsz, s_len, f), BF16), SDS((bsz, s_len, f), BF16), SDS((3, 8, f), F32),
                         SDS((3, 8, f), F32)),
        grid=(nf, bsz, ns), name="ffn_act_bwd2",
        in_specs=[cur, nxt, cur, nxt, cur, cur, par(0), par(nf)], out_specs=(cur, cur, acc3, acc3),
        compiler_params=_cp("arbitrary", "arbitrary", "arbitrary"))(dvv, dvv, dvg, dvg, uv, ug, dw8, dw8)


CONV_HALO = 32
CONV_CHUNK = 256


def _conv_act_fwd(p, dw32, dwb, ln_g, ln_b):
    bsz, s_len, d2 = p.shape
    d = d2 // 2
    kw = 31
    ts = _tile(s_len, 256, CONV_HALO)
    r = ts // CONV_HALO
    cc = min(CONV_CHUNK, d)

    def body(pc_ref, pp_ref, w_ref, wb_ref, g_ref, b_ref, u_ref, s_ref):
        s = pl.program_id(1)
        tot = jnp.zeros((ts, 1), F32)
        for c0 in range(0, d, cc):
            a = pc_ref[:, c0:c0 + cc].astype(F32)
            g = pc_ref[:, d + c0:d + c0 + cc].astype(F32)
            z = a * _sigmoid(g)
            ap = pp_ref[:, c0:c0 + cc].astype(F32)
            gp = pp_ref[:, d + c0:d + c0 + cc].astype(F32)
            tail = jnp.where(s > 0, ap * _sigmoid(gp), 0.0)
            sh = _shifts_back(z, tail, kw)
            acc = wb_ref[:, c0:c0 + cc] + sh[0] * w_ref[kw - 1:kw, c0:c0 + cc]
            for j in range(1, kw):
                acc = acc + sh[j] * w_ref[kw - 1 - j:kw - j, c0:c0 + cc]
            u_ref[:, c0:c0 + cc] = acc
            tot = tot + jnp.sum(acc, axis=-1, keepdims=True)
        u = u_ref[...]
        mu = tot / d
        uc = u - mu
        var = jnp.mean(uc * uc, axis=-1, keepdims=True)
        ul = uc * lax.rsqrt(var + EPS) * g_ref[...] + b_ref[...]
        s_ref[...] = (ul * _sigmoid(ul)).astype(BF16)

    vec = pl.BlockSpec((1, d), lambda bi, s: (0, 0))
    return pl.pallas_call(
        body, out_shape=(SDS((bsz, s_len, d), F32), SDS((bsz, s_len, d), BF16)), grid=(bsz, s_len // ts),
        name="conv_act_fwd",
        in_specs=[pl.BlockSpec((None, ts, d2), lambda bi, s: (bi, s, 0)),
                  pl.BlockSpec((None, CONV_HALO, d2), lambda bi, s: (bi, jnp.maximum(s * r - 1, 0), 0)),
                  pl.BlockSpec((32, d), lambda bi, s: (0, 0)), vec, vec, vec],
        out_specs=(pl.BlockSpec((None, ts, d), lambda bi, s: (bi, s, 0)),
                   pl.BlockSpec((None, ts, d), lambda bi, s: (bi, s, 0))),
        compiler_params=_cp("parallel", "arbitrary"))(p, p, dw32, dwb, ln_g, ln_b)


def _conv_act_bwd1(u, ds, ln_g, ln_b):
    t, d = u.shape
    ts = _tile(t, 256)

    def body(u_ref, ds_ref, g_ref, b_ref, du_ref, dg_ref, db_ref, dwb_ref):
        @pl.when(pl.program_id(0) == 0)
        def _():
            dg_ref[...] = jnp.zeros_like(dg_ref)
            db_ref[...] = jnp.zeros_like(db_ref)
            dwb_ref[...] = jnp.zeros_like(dwb_ref)

        uv = u_ref[...]
        uc = uv - jnp.mean(uv, axis=-1, keepdims=True)
        rstd = lax.rsqrt(jnp.mean(uc * uc, axis=-1, keepdims=True) + EPS)
        uh = uc * rstd
        ul = uh * g_ref[...] + b_ref[...]
        sg = _sigmoid(ul)
        dul = ds_ref[...].astype(F32) * (sg * (1.0 + ul * (1.0 - sg)))
        duh = dul * g_ref[...]
        du = rstd * (duh - jnp.mean(duh, axis=-1, keepdims=True) - uh * jnp.mean(duh * uh, axis=-1, keepdims=True))
        du_ref[...] = du
        dg_ref[...] += _fold8(dul * uh)
        db_ref[...] += _fold8(dul)
        dwb_ref[...] += _fold8(du)

    row = pl.BlockSpec((ts, d), lambda i: (i, 0))
    vec = pl.BlockSpec((1, d), lambda i: (0, 0))
    acc = pl.BlockSpec((8, d), lambda i: (0, 0))
    return pl.pallas_call(
        body, out_shape=(SDS((t, d), F32), SDS((8, d), F32), SDS((8, d), F32), SDS((8, d), F32)), grid=(t // ts,),
        name="conv_act_bwd1", in_specs=[row, row, vec, vec], out_specs=(row, acc, acc, acc),
        compiler_params=_cp("arbitrary"))(u, ds, ln_g, ln_b)


def _conv_act_bwd2(du, p, dw32):
    bsz, s_len, d2 = p.shape
    d = d2 // 2
    kw = 31
    ts = _tile(s_len, 256, CONV_HALO)
    r, ns = ts // CONV_HALO, s_len // ts
    cc = min(CONV_CHUNK, d)

    def body(dc_ref, dn_ref, pc_ref, pp_ref, w_ref, dp_ref, ddw_ref):
        bi, s = pl.program_id(0), pl.program_id(1)

        @pl.when((bi == 0) & (s == 0))
        def _():
            ddw_ref[...] = jnp.zeros_like(ddw_ref)

        for c0 in range(0, d, cc):
            a = pc_ref[:, c0:c0 + cc].astype(F32)
            g = pc_ref[:, d + c0:d + c0 + cc].astype(F32)
            sg = _sigmoid(g)
            z = a * sg
            ap = pp_ref[:, c0:c0 + cc].astype(F32)
            gp = pp_ref[:, d + c0:d + c0 + cc].astype(F32)
            tail = jnp.where(s > 0, ap * _sigmoid(gp), 0.0)
            duv = dc_ref[:, c0:c0 + cc]
            head = jnp.where(s < ns - 1, dn_ref[:, c0:c0 + cc], 0.0)
            zb = _shifts_back(z, tail, kw)
            for k in range(kw):
                ddw_ref[k, :, c0:c0 + cc] += _fold8(duv * zb[kw - 1 - k])
            df = _shifts_fwd(duv, head, kw)
            dz = df[0] * w_ref[kw - 1:kw, c0:c0 + cc]
            for j in range(1, kw):
                dz = dz + df[j] * w_ref[kw - 1 - j:kw - j, c0:c0 + cc]
            dp_ref[:, c0:c0 + cc] = (dz * sg).astype(BF16)
            dp_ref[:, d + c0:d + c0 + cc] = (dz * a * sg * (1.0 - sg)).astype(BF16)

    return pl.pallas_call(
        body, out_shape=(SDS((bsz, s_len, d2), BF16), SDS((32, 8, d), F32)), grid=(bsz, ns), name="conv_act_bwd2",
        in_specs=[pl.BlockSpec((None, ts, d), lambda bi, s: (bi, s, 0)),
                  pl.BlockSpec((None, CONV_HALO, d),
                               lambda bi, s: (bi, jnp.minimum((s + 1) * r, s_len // CONV_HALO - 1), 0)),
                  pl.BlockSpec((None, ts, d2), lambda bi, s: (bi, s, 0)),
                  pl.BlockSpec((None, CONV_HALO, d2), lambda bi, s: (bi, jnp.maximum(s * r - 1, 0), 0)),
                  pl.BlockSpec((32, d), lambda bi, s: (0, 0))],
        out_specs=(pl.BlockSpec((None, ts, d2), lambda bi, s: (bi, s, 0)),
                   pl.BlockSpec((32, 8, d), lambda bi, s: (0, 0, 0))),
        compiler_params=_cp("arbitrary", "arbitrary"))(du, du, p, p, dw32)


POOL_HALO = 16


def _pool_counts(s, ts, rows, w):
    t = s * ts + lax.broadcasted_iota(jnp.int32, (rows, 1), 0)
    return jnp.minimum(t + 1, w).astype(F32)


def _pool_fwd(x, gmix, w, b, scale):
    bsz, s_len, d = x.shape
    ng = len(POOL_WINDOWS)
    cg = d // ng
    ts = _tile(s_len, 512, POOL_HALO)
    r = ts // POOL_HALO

    def body(xc_ref, xp_ref, g_ref, w_ref, b_ref, sc_ref, y_ref, p_ref):
        s = pl.program_id(1)

        def norm(v):
            return v * lax.rsqrt(jnp.mean(v * v, axis=-1, keepdims=True) + EPS) * g_ref[...]

        xc = xc_ref[...]
        h = norm(xc)
        tail = jnp.where(s > 0, norm(xp_ref[...]), 0.0)
        for gi, win in enumerate(POOL_WINDOWS):
            lo, hi = gi * cg, (gi + 1) * cg
            hg = h[:, lo:hi]
            acc = jnp.concatenate([tail[:, lo:hi], hg], axis=0)
            step = 1
            while step < win:
                acc = acc + pltpu.roll(acc, step, axis=0)
                step *= 2
            pg = acc[POOL_HALO:] / _pool_counts(s, ts, ts, win) - hg
            pb = pg.astype(BF16)
            p_ref[:, lo:hi] = pb
            yg = jnp.dot(pb, w_ref[gi], preferred_element_type=F32) + b_ref[:, lo:hi]
            y_ref[:, lo:hi] = xc[:, lo:hi] + yg * sc_ref[:, lo:hi]

    vec = pl.BlockSpec((1, d), lambda bi, s: (0, 0))
    blk = pl.BlockSpec((None, ts, d), lambda bi, s: (bi, s, 0))
    return pl.pallas_call(
        body, out_shape=(SDS((bsz, s_len, d), F32), SDS((bsz, s_len, d), BF16)), grid=(bsz, s_len // ts),
        name="pool_fwd",
        in_specs=[blk, pl.BlockSpec((None, POOL_HALO, d), lambda bi, s: (bi, jnp.maximum(s * r - 1, 0), 0)),
                  vec, pl.BlockSpec((ng, cg, cg), lambda bi, s: (0, 0, 0)), vec, vec],
        out_specs=(blk, blk), compiler_params=_cp("parallel", "arbitrary"))(x, x, gmix, w, b, scale)


def _pool_bwd(x, dy, p, gmix, w, b, scale):
    bsz, s_len, d = x.shape
    ng = len(POOL_WINDOWS)
    cg = d // ng
    ts = _tile(s_len, 512, POOL_HALO)
    r, ns = ts // POOL_HALO, s_len // ts
    nt = (((1,), (1,)), ((), ()))
    tn = (((0,), (0,)), ((), ()))

    def body(x_ref, dy_ref, dyn_ref, p_ref, g_ref, w_ref, b_ref, sc_ref, dx_ref, dw_ref, db_ref, dsc_ref, dg_ref):
        bi, s = pl.program_id(0), pl.program_id(1)

        @pl.when((bi == 0) & (s == 0))
        def _():
            dw_ref[...] = jnp.zeros_like(dw_ref)
            db_ref[...] = jnp.zeros_like(db_ref)
            dsc_ref[...] = jnp.zeros_like(dsc_ref)
            dg_ref[...] = jnp.zeros_like(dg_ref)

        dy = dy_ref[...]
        dyy = dy * sc_ref[...]
        dyy_n = jnp.where(s < ns - 1, dyn_ref[...] * sc_ref[...], 0.0)
        db_ref[...] += _fold8(dyy)
        xv = x_ref[...]
        rr = lax.rsqrt(jnp.mean(xv * xv, axis=-1, keepdims=True) + EPS)
        xh = xv * rr
        for gi, win in enumerate(POOL_WINDOWS):
            lo, hi = gi * cg, (gi + 1) * cg
            pb = p_ref[:, lo:hi]
            wg = w_ref[gi]
            pre = jnp.dot(pb, wg, preferred_element_type=F32) + b_ref[:, lo:hi]
            dsc_ref[:, lo:hi] += _fold8(dy[:, lo:hi] * pre)
            dyb = dyy[:, lo:hi].astype(BF16)
            dw_ref[gi] += lax.dot_general(pb, dyb, tn, preferred_element_type=F32)
            dp = lax.dot_general(dyb, wg, nt, preferred_element_type=F32)
            dp_n = lax.dot_general(dyy_n[:, lo:hi].astype(BF16), wg, nt, preferred_element_type=F32)
            q = dp / _pool_counts(s, ts, ts, win)
            q_n = dp_n / _pool_counts(s + 1, ts, POOL_HALO, win)
            acc = jnp.concatenate([q, q_n], axis=0)
            ln = ts + POOL_HALO
            step = 1
            while step < win:
                acc = acc + pltpu.roll(acc, ln - step, axis=0)
                step *= 2
            dh = acc[:ts] - dp
            xhg = xh[:, lo:hi]
            dg_ref[:, lo:hi] += _fold8(dh * xhg)
            dx_ref[:, lo:hi] = dh * g_ref[:, lo:hi]
        u = dx_ref[...]
        dx_ref[...] = dy + rr * (u - xh * jnp.mean(u * xh, axis=-1, keepdims=True))

    vec = pl.BlockSpec((1, d), lambda bi, s: (0, 0))
    acc8 = pl.BlockSpec((8, d), lambda bi, s: (0, 0))
    blk = pl.BlockSpec((None, ts, d), lambda bi, s: (bi, s, 0))
    wspec = pl.BlockSpec((ng, cg, cg), lambda bi, s: (0, 0, 0))
    return pl.pallas_call(
        body, out_shape=(SDS((bsz, s_len, d), F32), SDS((ng, cg, cg), F32), SDS((8, d), F32), SDS((8, d), F32),
                         SDS((8, d), F32)),
        grid=(bsz, ns), name="pool_bwd",
        in_specs=[blk, blk,
                  pl.BlockSpec((None, POOL_HALO, d),
                               lambda bi, s: (bi, jnp.minimum((s + 1) * r, s_len // POOL_HALO - 1), 0)),
                  blk, vec, wspec, vec, vec],
        out_specs=(blk, wspec, acc8, acc8, acc8),
        compiler_params=_cp("arbitrary", "arbitrary"))(x, dy, dy, p, gmix, w, b, scale)


def _tri(n, upper):
    row = lax.broadcasted_iota(jnp.int32, (n, n), 0)
    col = lax.broadcasted_iota(jnp.int32, (n, n), 1)
    return jnp.where((col >= row) if upper else (col <= row), 1.0, 0.0).astype(F32)


def _fox_gate_fwd(proj, bf, n_heads):
    bsz, s_len, width = proj.shape
    col = width // LANES - 1
    ts = _tile(s_len, 512)

    def body(fl_ref, b_ref, c_ref, carry_ref):
        @pl.when(pl.program_id(1) == 0)
        def _():
            carry_ref[...] = jnp.zeros_like(carry_ref)

        xv = fl_ref[...] + b_ref[...]
        logf = jnp.minimum(xv, 0.0) - jnp.log(1.0 + jnp.exp(-jnp.abs(xv)))
        lane = lax.broadcasted_iota(jnp.int32, (1, LANES), 1)
        logf = jnp.where(lane < n_heads, logf, 0.0)
        c = _dot_hi(_tri(ts, False), logf) + carry_ref[0:1, :]
        c_ref[...] = c
        carry_ref[0:1, :] = c[ts - 1:ts, :]

    return pl.pallas_call(
        body, out_shape=SDS((bsz, s_len, LANES), F32), grid=(bsz, s_len // ts), name="fox_gate_fwd",
        in_specs=[pl.BlockSpec((None, ts, LANES), lambda bi, s: (bi, s, col)),
                  pl.BlockSpec((1, LANES), lambda bi, s: (0, 0))],
        out_specs=pl.BlockSpec((None, ts, LANES), lambda bi, s: (bi, s, 0)),
        scratch_shapes=[pltpu.VMEM((8, LANES), F32)],
        compiler_params=_cp("arbitrary", "arbitrary"))(proj, bf)


def _fox_gate_bwd(dc, proj, bf, n_heads):
    bsz, s_len, width = proj.shape
    col = width // LANES - 1
    ts = _tile(s_len, 512)
    ns = s_len // ts

    def body(dc_ref, fl_ref, b_ref, dfl_ref, db_ref, carry_ref):
        bi, s = pl.program_id(0), pl.program_id(1)

        @pl.when((bi == 0) & (s == 0))
        def _():
            db_ref[...] = jnp.zeros_like(db_ref)

        @pl.when(s == 0)
        def _():
            carry_ref[...] = jnp.zeros_like(carry_ref)

        dlogf = _dot_hi(_tri(ts, True), dc_ref[...]) + carry_ref[0:1, :]
        carry_ref[0:1, :] = dlogf[0:1, :]
        lane = lax.broadcasted_iota(jnp.int32, (1, LANES), 1)
        dfl = jnp.where(lane < n_heads, dlogf * (1.0 - _sigmoid(fl_ref[...] + b_ref[...])), 0.0)
        dfl_ref[...] = dfl.astype(BF16)
        db_ref[...] += _fold8(dfl)

    return pl.pallas_call(
        body, out_shape=(SDS((bsz, s_len, LANES), BF16), SDS((8, LANES), F32)), grid=(bsz, ns), name="fox_gate_bwd",
        in_specs=[pl.BlockSpec((None, ts, LANES), lambda bi, s: (bi, ns - 1 - s, 0)),
                  pl.BlockSpec((None, ts, LANES), lambda bi, s: (bi, ns - 1 - s, col)),
                  pl.BlockSpec((1, LANES), lambda bi, s: (0, 0))],
        out_specs=(pl.BlockSpec((None, ts, LANES), lambda bi, s: (bi, ns - 1 - s, 0)),
                   pl.BlockSpec((8, LANES), lambda bi, s: (0, 0))),
        scratch_shapes=[pltpu.VMEM((8, LANES), F32)],
        compiler_params=_cp("arbitrary", "arbitrary"))(dc, proj, bf)


def _head_maps(d):
    ch = lax.broadcasted_iota(jnp.int32, (d, LANES), 0) // HEAD_DIM
    hd = lax.broadcasted_iota(jnp.int32, (d, LANES), 1)
    e = jnp.where(ch == hd, 1.0, 0.0).astype(BF16)
    cht = lax.broadcasted_iota(jnp.int32, (LANES, d), 1) // HEAD_DIM
    hdt = lax.broadcasted_iota(jnp.int32, (LANES, d), 0)
    et = jnp.where(cht == hdt, 1.0, 0.0).astype(BF16)
    return e, et


def _dot_sel(x, e):
    a = x.astype(BF16)
    r = x - a.astype(F32)
    b = r.astype(BF16)
    c = (r - b.astype(F32)).astype(BF16)
    return (jnp.dot(a, e, preferred_element_type=F32) + jnp.dot(b, e, preferred_element_type=F32)
            + jnp.dot(c, e, preferred_element_type=F32))


def _fox_qknorm_fwd(proj, gq, gk, d):
    t = proj.shape[0]
    ts = _tile(t, 256)
    scale = 1.0 / math.sqrt(HEAD_DIM)

    def body(q_ref, k_ref, v_ref, gq_ref, gk_ref, qn_ref, kn_ref, vb_ref):
        e, et = _head_maps(d)

        def norm(v, g):
            r = lax.rsqrt(_dot_sel(v * v, e) / HEAD_DIM + EPS)
            return v * _dot_sel(r, et) * g

        qn_ref[...] = (norm(q_ref[...], gq_ref[...]) * scale).astype(BF16)
        kn_ref[...] = norm(k_ref[...], gk_ref[...]).astype(BF16)
        vb_ref[...] = v_ref[...].astype(BF16)

    def colblk(j):
        return pl.BlockSpec((ts, d), lambda i: (i, j))

    vec = pl.BlockSpec((1, d), lambda i: (0, 0))
    out = pl.BlockSpec((ts, d), lambda i: (i, 0))
    return pl.pallas_call(
        body, out_shape=(SDS((t, d), BF16),) * 3, grid=(t // ts,), name="fox_qknorm_fwd",
        in_specs=[colblk(0), colblk(1), colblk(2), vec, vec], out_specs=(out, out, out),
        compiler_params=_cp("parallel"))(proj, proj, proj, gq, gk)


def _fox_qknorm_bwd(proj, dq, dk, dv, gq, gk, d):
    t = proj.shape[0]
    ts = _tile(t, 256)
    scale = 1.0 / math.sqrt(HEAD_DIM)

    def body(q_ref, k_ref, dq_ref, dk_ref, dv_ref, gq_ref, gk_ref, dp_ref, dgq_ref, dgk_ref):
        @pl.when(pl.program_id(0) == 0)
        def _():
            dgq_ref[...] = jnp.zeros_like(dgq_ref)
            dgk_ref[...] = jnp.zeros_like(dgk_ref)

        e, et = _head_maps(d)

        def back(v, g, dn, dg_ref):
            r = _dot_sel(lax.rsqrt(_dot_sel(v * v, e) / HEAD_DIM + EPS), et)
            vh = v * r
            dg_ref[...] += _fold8(dn * vh)
            u = dn * g
            mh = _dot_sel(_dot_sel(u * vh, e) / HEAD_DIM, et)
            return r * (u - vh * mh)

        dp_ref[:, 0:d] = back(q_ref[...], gq_ref[...], dq_ref[...] * scale, dgq_ref).astype(BF16)
        dp_ref[:, d:2 * d] = back(k_ref[...], gk_ref[...], dk_ref[...], dgk_ref).astype(BF16)
        dp_ref[:, 2 * d:3 * d] = dv_ref[...]

    def colblk(j):
        return pl.BlockSpec((ts, d), lambda i: (i, j))

    row = pl.BlockSpec((ts, d), lambda i: (i, 0))
    vec = pl.BlockSpec((1, d), lambda i: (0, 0))
    acc = pl.BlockSpec((8, d), lambda i: (0, 0))
    return pl.pallas_call(
        body, out_shape=(SDS((t, 3 * d), BF16), SDS((8, d), F32), SDS((8, d), F32)), grid=(t // ts,),
        name="fox_qknorm_bwd", in_specs=[colblk(0), colblk(1), row, row, row, vec, vec],
        out_specs=(pl.BlockSpec((ts, 3 * d), lambda i: (i, 0)), acc, acc),
        compiler_params=_cp("arbitrary"))(proj, proj, dq, dk, dv, gq, gk)


ATT_BLOCK = 512
_NT = (((1,), (1,)), ((), ()))
_TN = (((0,), (0,)), ((), ()))


def _head_mask(h):
    return (lax.broadcasted_iota(jnp.int32, (1, LANES), 1) // HEAD_DIM) == h


def _causal(qi, ki, tq, tk):
    row = qi * tq + lax.broadcasted_iota(jnp.int32, (tq, 1), 0)
    col = ki * tk + lax.broadcasted_iota(jnp.int32, (1, tk), 1)
    return col <= row


def _direct_exchange(ins, outs, place, sems, gather):
    send_sems, recv_sems, local_sems = sems
    x, y, c, me = _mesh_place()
    copies = []
    for t in range(len(ins)):
        dst = outs[t].at[me] if gather else outs[place[t][0]].at[me, place[t][1]]
        copies.append(pltpu.make_async_copy(ins[t] if gather else ins[t].at[me], dst, local_sems.at[t]))
        for kbits in range(1, N_DEV):
            px = 1 - x if kbits & 4 else x
            py = 1 - y if kbits & 2 else y
            pc = 1 - c if kbits & 1 else c
            copies.append(pltpu.make_async_remote_copy(
                src_ref=ins[t] if gather else ins[t].at[4 * px + 2 * py + pc], dst_ref=dst,
                send_sem=send_sems.at[t, kbits - 1], recv_sem=recv_sems.at[t, kbits - 1],
                device_id=(px, py, pc), device_id_type=pl.DeviceIdType.MESH))
    return copies


def _exchange_scratch(n):
    return [pltpu.SemaphoreType.DMA((n, N_DEV - 1)), pltpu.SemaphoreType.DMA((n, N_DEV - 1)),
            pltpu.SemaphoreType.DMA((n,))]


def _flash_fwd(q, k, v, crow, gather=()):
    bsz, s_len, d = q.shape
    nj = d // LANES
    tq = tk = _tile(s_len, ATT_BLOCK, LANES)
    nq = s_len // tq
    ng = len(gather)

    pairs = [(a, b) for a in range(nq) for b in range(a + 1)]
    qtab = jnp.asarray([a for a, _ in pairs], jnp.int32)
    ktab = jnp.asarray([b for _, b in pairs], jnp.int32)

    def body(qtab_ref, ktab_ref, q_ref, k_ref, v_ref, c_ref, *rest):
        g_in, (o_ref, lse_ref), g_out = rest[:ng], rest[ng:ng + 2], rest[ng + 2:2 * ng + 2]
        m_ref, l_ref, acc_ref = rest[2 * ng + 2:2 * ng + 5]
        step_id = pl.program_id(2)
        qi, ki = qtab_ref[step_id], ktab_ref[step_id]
        if ng:
            sems = rest[2 * ng + 5:]
            outer = (pl.program_id(0), pl.program_id(1))

            @pl.when((outer[0] == 0) & (outer[1] == 0) & (step_id == 0))
            def _():
                for cp in _direct_exchange(g_in, g_out, None, sems, True):
                    cp.start()

            @pl.when((outer[0] == bsz - 1) & (outer[1] == nj - 1) & (step_id == len(pairs) - 1))
            def _():
                for cp in _direct_exchange(g_in, g_out, None, sems, True):
                    cp.wait()

        @pl.when(ki == 0)
        def _():
            m_ref[...] = jnp.full_like(m_ref, NEG)
            l_ref[...] = jnp.zeros_like(l_ref)
            acc_ref[...] = jnp.zeros_like(acc_ref)

        def step(masked):
            qv, kv, vv = q_ref[...], k_ref[...], v_ref[...]
            for h in range(2):
                qh = jnp.where(_head_mask(h), qv, jnp.zeros_like(qv))
                s = lax.dot_general(qh, kv, _NT, preferred_element_type=F32) - c_ref[h:h + 1, :]
                if masked:
                    s = jnp.where(_causal(qi, ki, tq, tk), s, NEG)
                m_prev = m_ref[h]
                m_new = jnp.maximum(m_prev, jnp.max(s, axis=1, keepdims=True))
                pm = jnp.exp(s - m_new)
                alpha = jnp.exp(m_prev - m_new)
                l_ref[h] = alpha * l_ref[h] + jnp.sum(pm, axis=1, keepdims=True)
                p_hi = pm.astype(BF16)
                p_lo = (pm - p_hi.astype(F32)).astype(BF16)
                acc_ref[h] = (alpha * acc_ref[h] + jnp.dot(p_hi, vv, preferred_element_type=F32)
                              + jnp.dot(p_lo, vv, preferred_element_type=F32))
                m_ref[h] = m_new

        pl.when(ki < qi)(functools.partial(step, False))
        pl.when(ki == qi)(functools.partial(step, True))

        @pl.when(ki == qi)
        def _():
            m0 = _head_mask(0)
            o_ref[...] = jnp.where(m0, acc_ref[0] / l_ref[0], acc_ref[1] / l_ref[1])
            lse_ref[...] = jnp.where(m0, m_ref[0] + jnp.log(l_ref[0]), m_ref[1] + jnp.log(l_ref[1]))

    qblk = pl.BlockSpec((None, tq, LANES), lambda bi, j, t, qt, kt: (bi, qt[t], j))
    kblk = pl.BlockSpec((None, tk, LANES), lambda bi, j, t, qt, kt: (bi, kt[t], j))
    hbm = pl.BlockSpec(memory_space=pl.ANY)
    outs = pl.pallas_call(
        body, out_shape=[SDS((bsz, s_len, d), F32), SDS((bsz, nj, s_len, LANES), F32)]
        + [SDS((N_DEV,) + tuple(a.shape), a.dtype) for a in gather], name="flash_fwd",
        grid_spec=pltpu.PrefetchScalarGridSpec(
            num_scalar_prefetch=2, grid=(bsz, nj, len(pairs)),
            in_specs=[qblk, kblk, kblk,
                      pl.BlockSpec((None, None, 2, tk), lambda bi, j, t, qt, kt: (bi, j, 0, kt[t]))] + [hbm] * ng,
            out_specs=[qblk, pl.BlockSpec((None, None, tq, LANES), lambda bi, j, t, qt, kt: (bi, j, qt[t], 0))]
            + [hbm] * ng,
            scratch_shapes=[pltpu.VMEM((2, tq, 1), F32), pltpu.VMEM((2, tq, 1), F32),
                            pltpu.VMEM((2, tq, LANES), F32)] + (_exchange_scratch(ng) if ng else [])),
        compiler_params=_cp("arbitrary", "arbitrary", "arbitrary"))(qtab, ktab, q, k, v, crow, *gather)
    return outs[0], outs[1], list(outs[2:])


def _flash_probs(qv, kv, vv, dov, ov, lse, c_ref, h, mask):
    hm = _head_mask(h)
    qh = jnp.where(hm, qv, jnp.zeros_like(qv))
    s = lax.dot_general(qh, kv, _NT, preferred_element_type=F32) - c_ref[h:h + 1, :]
    pm = jnp.exp(s - lse[:, h * HEAD_DIM:h * HEAD_DIM + 1])
    if mask is not None:
        pm = jnp.where(mask, pm, 0.0)
    doh = jnp.where(hm, dov, jnp.zeros_like(dov))
    dpm = lax.dot_general(doh, vv, _NT, preferred_element_type=F32)
    delta = jnp.sum(jnp.where(hm, dov.astype(F32) * ov, 0.0), axis=1, keepdims=True)
    return pm, pm * (dpm - delta)


def _flash_bwd(q, k, v, do, o, lse, crow, items=(), groups=()):
    bsz, s_len, d = q.shape
    nj = d // LANES
    tq = tk = _tile(s_len, ATT_BLOCK, LANES)
    nq = s_len // tq

    pairs = [(b, a) for b in range(nq) for a in range(b, nq)]
    n_live = len(pairs)
    ktab = jnp.asarray([b for b, _ in pairs] + [nq - 1] * nq, jnp.int32)
    qtab = jnp.asarray([a for _, a in pairs] + list(range(nq)), jnp.int32)

    n_it, n_grp = len(items), len(groups)
    place = {it: (g, l) for g, members in enumerate(groups) for l, it in enumerate(members)}

    def body(ktab_ref, qtab_ref, q_ref, k_ref, v_ref, do_ref, o_ref, lse_ref, c_ref, *rest):
        x_in, (dq_ref, dk_ref, dv_ref, dc_ref) = rest[:n_it], rest[n_it:n_it + 4]
        x_out = rest[n_it + 4:n_it + 4 + n_grp]
        dqa_ref, dka_ref, dva_ref, dca_ref = rest[n_it + 4 + n_grp:n_it + 8 + n_grp]
        step_id = pl.program_id(2)
        ki, qi = ktab_ref[step_id], qtab_ref[step_id]
        live = step_id < n_live
        rows = pl.ds(pl.multiple_of(qi * tq, tq), tq)
        if n_it:
            sems = rest[n_it + 8 + n_grp:]
            outer = (pl.program_id(0), pl.program_id(1))

            @pl.when((outer[0] == 0) & (outer[1] == 0) & (step_id == 0))
            def _():
                for cp in _direct_exchange(x_in, x_out, place, sems, False):
                    cp.start()

            @pl.when((outer[0] == bsz - 1) & (outer[1] == nj - 1) & (step_id == n_live + nq - 1))
            def _():
                for cp in _direct_exchange(x_in, x_out, place, sems, False):
                    cp.wait()

        @pl.when(step_id == 0)
        def _():
            dqa_ref[...] = jnp.zeros_like(dqa_ref)

        @pl.when(live & (qi == ki))
        def _():
            dka_ref[...] = jnp.zeros_like(dka_ref)
            dva_ref[...] = jnp.zeros_like(dva_ref)
            dca_ref[...] = jnp.zeros_like(dca_ref)

        def step(masked):
            qv, kv, vv, dov, ov, lse = q_ref[...], k_ref[...], v_ref[...], do_ref[...], o_ref[...], lse_ref[...]
            mask = _causal(qi, ki, tq, tk) if masked else None
            for h in range(2):
                pm, ds = _flash_probs(qv, kv, vv, dov, ov, lse, c_ref, h, mask)
                dsb = ds.astype(BF16)
                dva_ref[h] += lax.dot_general(pm.astype(BF16), dov, _TN, preferred_element_type=F32)
                dka_ref[h] += lax.dot_general(dsb, qv, _TN, preferred_element_type=F32)
                dqa_ref[h, rows, :] += jnp.dot(dsb, kv, preferred_element_type=F32)
                dca_ref[h:h + 1, :] -= jnp.sum(ds, axis=0, keepdims=True)

        pl.when(live & (qi > ki))(functools.partial(step, False))
        pl.when(live & (qi == ki))(functools.partial(step, True))

        @pl.when(live & (qi == nq - 1))
        def _():
            m0 = _head_mask(0)
            dk_ref[...] = jnp.where(m0, dka_ref[0], dka_ref[1])
            dv_ref[...] = jnp.where(m0, dva_ref[0], dva_ref[1]).astype(BF16)
            dc_ref[...] = dca_ref[0:2, :]

        @pl.when(jnp.logical_not(live))
        def _():
            dq_ref[...] = jnp.where(_head_mask(0), dqa_ref[0, rows, :], dqa_ref[1, rows, :])

    def qside(bi, j, t, kt, qt):
        return (bi, jnp.where(t < n_live, qt[t], nq - 1), j)

    def kside(bi, j, t, kt, qt):
        return (bi, kt[t], j)

    def dqside(bi, j, t, kt, qt):
        return (bi, jnp.where(t < n_live, 0, qt[t]), j)

    qblk, kblk = pl.BlockSpec((None, tq, LANES), qside), pl.BlockSpec((None, tk, LANES), kside)
    cblk = pl.BlockSpec((None, None, 2, tk), lambda bi, j, t, kt, qt: (bi, j, 0, kt[t]))
    hbm = pl.BlockSpec(memory_space=pl.ANY)
    outs = pl.pallas_call(
        body, out_shape=[SDS((bsz, s_len, d), F32), SDS((bsz, s_len, d), F32), SDS((bsz, s_len, d), BF16),
                         SDS((bsz, nj, 2, s_len), F32)]
        + [SDS((N_DEV, len(members)) + tuple(items[members[0]].shape[1:]), items[members[0]].dtype)
           for members in groups], name="flash_bwd",
        grid_spec=pltpu.PrefetchScalarGridSpec(
            num_scalar_prefetch=2, grid=(bsz, nj, n_live + nq),
            in_specs=[qblk, kblk, kblk, qblk, qblk,
                      pl.BlockSpec((None, None, tq, LANES),
                                   lambda bi, j, t, kt, qt: (bi, j, jnp.where(t < n_live, qt[t], nq - 1), 0)),
                      cblk] + [hbm] * n_it,
            out_specs=[pl.BlockSpec((None, tq, LANES), dqside), kblk, kblk, cblk] + [hbm] * n_grp,
            scratch_shapes=[pltpu.VMEM((2, s_len, LANES), F32), pltpu.VMEM((2, tk, LANES), F32),
                            pltpu.VMEM((2, tk, LANES), F32), pltpu.VMEM((8, tk), F32)]
            + (_exchange_scratch(n_it) if n_it else [])),
        compiler_params=_cp("arbitrary", "arbitrary", "arbitrary"))(ktab, qtab, q, k, v, do, o, lse, crow, *items)
    return outs[0], outs[1], outs[2], outs[3], list(outs[4:])


def _loss_head(y, target):
    t, d = y.shape
    tm = _tile(t, 512)

    def body(y_ref, t_ref, dy_ref, acc_ref):
        @pl.when(pl.program_id(0) == 0)
        def _():
            acc_ref[...] = jnp.zeros_like(acc_ref)

        err = y_ref[...] - t_ref[...]
        dy_ref[...] = err / d
        acc_ref[...] += _fold8(err * err)

    row = pl.BlockSpec((tm, d), lambda i: (i, 0))
    return pl.pallas_call(
        body, out_shape=(SDS((t, d), F32), SDS((8, d), F32)), grid=(t // tm,), name="loss_head",
        in_specs=[row, row], out_specs=(row, pl.BlockSpec((8, d), lambda i: (0, 0))),
        compiler_params=_cp("arbitrary"))(y, target)


ADAM_COLS = 1024


def _adamw(g8, w, m, v):
    shape = w.shape
    cols = shape[-1]
    rows = w.size // cols
    n_parts = g8.shape[0]
    g8, w, m, v = g8.reshape(n_parts, rows, cols), w.reshape(rows, cols), m.reshape(rows, cols), v.reshape(rows, cols)
    tr = _tile(rows, 256, 16)
    c1 = 1.0 - ADAM_B1 ** ADAM_STEP
    c2 = 1.0 - ADAM_B2 ** ADAM_STEP

    def body(g8_ref, w_ref, m_ref, v_ref, g_ref, d_ref, nm_ref, nv_ref):
        g = g8_ref[0].astype(F32)
        for i in range(1, n_parts):
            g = g + g8_ref[i].astype(F32)
        mn = ADAM_B1 * m_ref[...] + (1.0 - ADAM_B1) * g
        vn = ADAM_B2 * v_ref[...] + (1.0 - ADAM_B2) * (g * g)
        g_ref[...] = g
        nm_ref[...] = mn
        nv_ref[...] = vn
        d_ref[...] = -ADAM_LR * ((mn / c1) / (jnp.sqrt(vn / c2) + ADAM_EPS) + ADAM_WD * w_ref[...])

    blk = pl.BlockSpec((tr, cols), lambda i: (i, 0))
    outs = pl.pallas_call(
        body, out_shape=(SDS((rows, cols), F32),) * 4, grid=(rows // tr,), name="adamw",
        in_specs=[pl.BlockSpec((n_parts, tr, cols), lambda i: (0, i, 0)), blk, blk, blk], out_specs=(blk,) * 4,
        compiler_params=_cp("parallel"))(g8, w, m, v)
    return [o.reshape(shape) for o in outs]


def _mesh_place():
    x, y, c = lax.axis_index("x"), lax.axis_index("y"), lax.axis_index("c")
    return x, y, c, 4 * x + 2 * y + c


def _gather(shards):
    n = len(shards)

    def body(*refs):
        ins, outs = refs[:n], refs[n:2 * n]
        send_sems, recv_sems, local_sems = refs[2 * n:]
        x, y, c, me = _mesh_place()
        sibling = (x, y, 1 - c)
        chips = [(1 - x, y), (x, 1 - y), (1 - x, 1 - y)]

        def block(px, py, pc):
            return 4 * px + 2 * py + pc

        def copy(t, k, blk, to, src=None):
            return pltpu.make_async_remote_copy(
                src_ref=outs[t].at[blk] if src is None else src, dst_ref=outs[t].at[blk],
                send_sem=send_sems.at[t, k], recv_sem=recv_sems.at[t, k], device_id=to,
                device_id_type=pl.DeviceIdType.MESH)

        own = [pltpu.make_async_copy(ins[t], outs[t].at[me], local_sems.at[t]) for t in range(n)]
        first = []
        for t in range(n):
            own[t].start()
            first.append(copy(t, 0, me, sibling, src=ins[t]))
            first += [copy(t, 1 + j, me, (*chip, c), src=ins[t]) for j, chip in enumerate(chips)]
        for cp in first:
            cp.start()
        passed = []
        for j, chip in enumerate(chips):
            for t in range(n):
                copy(t, 1 + j, block(*chip, c), (x, y, c)).wait_recv()
                cp = copy(t, 4 + j, block(*chip, c), sibling)
                cp.start()
                passed.append(cp)
        for t in range(n):
            copy(t, 0, block(x, y, 1 - c), (x, y, c)).wait_recv()
            for j, chip in enumerate(chips):
                copy(t, 4 + j, block(*chip, 1 - c), (x, y, c)).wait_recv()
        for cp in first + passed:
            cp.wait_send()
        for cp in own:
            cp.wait()

    hbm = pl.BlockSpec(memory_space=pl.ANY)
    return pl.pallas_call(
        body, out_shape=[SDS((N_DEV,) + tuple(s.shape), s.dtype) for s in shards], name="gather",
        in_specs=[hbm] * n, out_specs=[hbm] * n,
        scratch_shapes=[pltpu.SemaphoreType.DMA((n, N_DEV - 1)), pltpu.SemaphoreType.DMA((n, N_DEV - 1)),
                        pltpu.SemaphoreType.DMA((n,))])(*shards)


N_CHIP = N_DEV // 2


def _scatter_core(items):
    n = len(items)

    def body(*refs):
        ins, outs = refs[:n], refs[n:2 * n]
        send_sems, recv_sems = refs[2 * n:]
        x, y, c, _ = _mesh_place()
        copies = []
        for it in range(n):
            for r in range(N_CHIP):
                cp = pltpu.make_async_remote_copy(
                    src_ref=ins[it].at[2 * r + 1 - c], dst_ref=outs[it].at[r], send_sem=send_sems.at[it, r],
                    recv_sem=recv_sems.at[it, r], device_id=(x, y, 1 - c), device_id_type=pl.DeviceIdType.MESH)
                cp.start()
                copies.append(cp)
        for cp in copies:
            cp.wait()

    hbm = pl.BlockSpec(memory_space=pl.ANY)
    return pl.pallas_call(
        body, out_shape=[SDS((N_CHIP,) + tuple(a.shape[1:]), a.dtype) for a in items], name="scatter_core",
        in_specs=[hbm] * n, out_specs=[hbm] * n,
        scratch_shapes=[pltpu.SemaphoreType.DMA((n, N_CHIP)), pltpu.SemaphoreType.DMA((n, N_CHIP))])(*items)


def _pair_add(item, other):
    shape = item.shape[1:]
    cols = shape[-1]
    rows = math.prod(shape) // cols
    tr = _tile(rows, 512, 16)

    def body(x_ref, o_ref, h_ref):
        c = lax.axis_index("c")
        mine = jnp.where(c == 0, x_ref[0].astype(F32), x_ref[1].astype(F32))
        h_ref[...] = (mine + o_ref[...].astype(F32)).astype(item.dtype)

    return pl.pallas_call(
        body, out_shape=SDS((N_CHIP, rows, cols), item.dtype), grid=(N_CHIP, rows // tr), name="pair_add",
        in_specs=[pl.BlockSpec((None, 2, tr, cols), lambda r, i: (r, 0, i, 0)),
                  pl.BlockSpec((None, tr, cols), lambda r, i: (r, i, 0))],
        out_specs=pl.BlockSpec((None, tr, cols), lambda r, i: (r, i, 0)),
        compiler_params=_cp("parallel", "parallel"))(
            item.reshape(N_CHIP, 2, rows, cols), other.reshape(N_CHIP, rows, cols)).reshape((N_CHIP,) + shape)


def _scatter_chip(items, groups):
    n = len(items)
    place = {it: (g, l) for g, members in enumerate(groups) for l, it in enumerate(members)}

    def body(*refs):
        ins, outs = refs[:n], refs[n:n + len(groups)]
        send_sems, recv_sems, local_sems = refs[n + len(groups):]
        x, y, c, _ = _mesh_place()
        chip = 2 * x + y
        copies = []
        for it in range(n):
            g, l = place[it]
            own = pltpu.make_async_copy(ins[it].at[chip], outs[g].at[chip, l], local_sems.at[it])
            own.start()
            copies.append(own)
            for kbits in range(1, N_CHIP):
                px = 1 - x if kbits & 2 else x
                py = 1 - y if kbits & 1 else y
                cp = pltpu.make_async_remote_copy(
                    src_ref=ins[it].at[2 * px + py], dst_ref=outs[g].at[chip, l],
                    send_sem=send_sems.at[it, kbits - 1], recv_sem=recv_sems.at[it, kbits - 1],
                    device_id=(px, py, c), device_id_type=pl.DeviceIdType.MESH)
                cp.start()
                copies.append(cp)
        for cp in copies:
            cp.wait()

    hbm = pl.BlockSpec(memory_space=pl.ANY)
    out_shape = [SDS((N_CHIP, len(members)) + tuple(items[members[0]].shape[1:]), items[members[0]].dtype)
                 for members in groups]
    return pl.pallas_call(
        body, out_shape=out_shape, name="scatter_chip", in_specs=[hbm] * n, out_specs=[hbm] * len(groups),
        scratch_shapes=[pltpu.SemaphoreType.DMA((n, N_CHIP - 1)), pltpu.SemaphoreType.DMA((n, N_CHIP - 1)),
                        pltpu.SemaphoreType.DMA((n,))])(*items)


def _scatter(items, groups):
    halves = _scatter_core(items)
    return _scatter_chip([_pair_add(a, h) for a, h in zip(items, halves)], groups)


def _cat_lanes(g, layer, nb, blk, width):
    _, _, rows, c = g.shape
    tr = _tile(rows, 256, 16)

    def body(g_ref, o_ref):
        for p in range(nb):
            o_ref[:, p * c:(p + 1) * c] = g_ref[p]
        if width > nb * c:
            o_ref[:, nb * c:] = jnp.zeros((tr, width - nb * c), g.dtype)

    return pl.pallas_call(
        body, out_shape=SDS((rows, width), g.dtype), grid=(rows // tr,), name="cat_lanes",
        in_specs=[pl.BlockSpec((nb, None, tr, c), lambda i: (blk, layer, i, 0))],
        out_specs=pl.BlockSpec((tr, width), lambda i: (i, 0)),
        compiler_params=_cp("parallel"))(g)


def _split_lanes(parts, c):
    rows = parts[0].shape[0]
    counts = [p.shape[1] // c for p in parts]
    tr = _tile(rows, 256, 16)

    def body(*refs):
        o_ref = refs[-1]
        q = 0
        for x_ref, cnt in zip(refs[:-1], counts):
            for p in range(cnt):
                o_ref[q] = x_ref[:, p * c:(p + 1) * c]
                q += 1

    return pl.pallas_call(
        body, out_shape=SDS((sum(counts), rows, c), parts[0].dtype), grid=(rows // tr,), name="split_lanes",
        in_specs=[pl.BlockSpec((tr, p.shape[1]), lambda i: (i, 0)) for p in parts],
        out_specs=pl.BlockSpec((sum(counts), tr, c), lambda i: (0, i, 0)),
        compiler_params=_cp("parallel"))(*parts)


def _unshard(g8, shard_shape, axis):
    full = jnp.moveaxis(g8.reshape((N_DEV,) + tuple(shard_shape)), 0, axis)
    shape = list(shard_shape)
    shape[axis] *= N_DEV
    return full.reshape(shape)


def _to_shards(full, axis):
    shape = list(full.shape)
    shape[axis:axis + 1] = [N_DEV, shape[axis] // N_DEV]
    return jnp.moveaxis(full.reshape(shape), axis, 0).reshape(N_DEV, -1)


SMALL = [n for n in SHARDED if n not in MATRICES]


def _flat_rows(parts):
    flat = jnp.concatenate([p.reshape(-1) for p in parts])
    chunk = 8 * ADAM_COLS
    n = -(-flat.shape[0] // chunk) * chunk
    return jnp.pad(flat, (0, n - flat.shape[0])).reshape(n // ADAM_COLS, ADAM_COLS)


def _prepare_vectors(small, shards):
    wt = {}
    flat = small.reshape(N_DEV, -1)
    off = 0
    for n in SMALL:
        size = shards[n].size
        wt[n] = _unshard(flat[:, off:off + size], shards[n].shape, SHARD_AXIS[n])
        off += size
    return wt


def _prepare_matrices(gathered):
    wt = {}
    for n in ('conv_w_out', 'fox_w_o', 'ffn_w_down'):
        if n in gathered:
            g = gathered[n]
            wt[n] = [g[:, l].reshape(N_DEV * g.shape[2], g.shape[3]) for l in range(g.shape[1])]
    if 'pool_w' in gathered:
        g = gathered['pool_w']
        wt['pool_w'] = [jnp.moveaxis(g[:, l], 0, 1).reshape(g.shape[2], N_DEV * g.shape[3], g.shape[4])
                        for l in range(g.shape[1])]
    if 'conv_w_in' in gathered:
        g = gathered['conv_w_in']
        wt['conv_w_in'] = [_cat_lanes(g, l, N_DEV, 0, N_DEV * g.shape[3]) for l in range(g.shape[1])]
    if 'fox_w_in' in gathered:
        g = gathered['fox_w_in']
        wt['fox_w_in'] = [_cat_lanes(g, l, N_DEV, 0, 3 * g.shape[2] + LANES) for l in range(g.shape[1])]
    if 'ffn_w_up' in gathered:
        g = gathered['ffn_w_up']
        half = N_DEV // 2
        wt['ffn_w_up_v'] = [_cat_lanes(g, l, half, 0, half * g.shape[3]) for l in range(g.shape[1])]
        wt['ffn_w_up_g'] = [_cat_lanes(g, l, half, 1, half * g.shape[3]) for l in range(g.shape[1])]
    return wt


def _pad_rows(w, rows):
    return jnp.pad(w, ((0, rows - w.shape[0]), (0, 0)))


def _fold(acc):
    return acc.sum(axis=0)


def _local_step(x, target, wt, late_shards=None, cut=None):
    wt = dict(wt)
    late_names = [n for n in MATRICES if late_shards and n in late_shards]
    late_recv = {}
    bsz, s_len, d = x.shape
    t = bsz * s_len
    depth = wt['norm_mix'].shape[0]
    n_heads = d // HEAD_DIM
    f = wt['ffn_w_up_v'][0].shape[1]
    row = lambda a: a.reshape(1, -1)
    grads = {n: {} for n in WEIGHTS}
    saved = []

    xc = x.reshape(t, d)
    hn_next = None
    for i in range(depth):
        j = i // 3
        kind = i % 3
        sv = {'x_mix': xc}
        gm = row(wt['norm_mix'][i])
        gf = row(wt['norm_ffn'][i])
        if kind == 0:
            hn = hn_next if hn_next is not None else _rmsnorm_fwd(xc, gm)
            p = _mm(hn, wt['conv_w_in'][j], bias=row(wt['conv_b_in'][j]), name="conv_in")
            u, sact = _conv_act_fwd(p.reshape(bsz, s_len, 2 * d), _pad_rows(wt['conv_dw'][j], 32),
                                    row(wt['conv_dw_b'][j]), row(wt['conv_ln_g'][j]), row(wt['conv_ln_b'][j]))
            sact = sact.reshape(t, d)
            xn, hf = _mm(sact, wt['conv_w_out'][j], bias=row(wt['conv_b_out'][j]), residual=xc, norm_gain=gf,
                         out_dtype=F32, name="conv_out")
            sv.update(hn=hn, p=p, u=u.reshape(t, d), sact=sact)
        elif kind == 1:
            xn, pp = _pool_fwd(xc.reshape(bsz, s_len, d), gm, wt['pool_w'][j], row(wt['pool_b'][j]),
                               row(wt['pool_scale'][j]))
            xn = xn.reshape(t, d)
            hf = _rmsnorm_fwd(xn, gf)
            sv.update(p=pp)
        else:
            hn = hn_next if hn_next is not None else _rmsnorm_fwd(xc, gm)
            wp = wt['fox_w_in'][j]
            bf = jnp.pad(wt['fox_b_f'][j], (0, LANES - n_heads)).reshape(1, LANES)
            gq = jnp.tile(wt['fox_q_gain'][j], n_heads).reshape(1, d)
            gk = jnp.tile(wt['fox_k_gain'][j], n_heads).reshape(1, d)
            proj = _mm(hn, wp, out_dtype=F32, name="fox_in")
            c = _fox_gate_fwd(proj.reshape(bsz, s_len, -1), bf, n_heads)
            crow = jnp.swapaxes(c, 1, 2)[:, :n_heads].reshape(bsz, n_heads // 2, 2, s_len)
            qn, kn, vb = _fox_qknorm_fwd(proj, gq, gk, d)
            shp = (bsz, s_len, d)
            o, lse, got = _flash_fwd(qn.reshape(shp), kn.reshape(shp), vb.reshape(shp), crow,
                                     gather=[late_shards[n] for n in late_names])
            for key, layers in _prepare_matrices(dict(zip(late_names, got))).items():
                wt[key] = wt[key] + layers
            o = o.reshape(t, d)
            xn, hf = _mm(o, wt['fox_w_o'][j], residual=xc, norm_gain=gf, out_dtype=F32, name="fox_out")
            sv.update(hn=hn, wp=wp, bf=bf, gq=gq, gk=gk, proj=proj, crow=crow, qn=qn, kn=kn, vb=vb, o=o, lse=lse)
        xc = xn
        sv['x_ffn'] = xc
        shf = (bsz, s_len, f)
        uv = _mm(hf, wt['ffn_w_up_v'][i], name="ffn_up").reshape(shf)
        ug = _mm(hf, wt['ffn_w_up_g'][i], name="ffn_up").reshape(shf)
        dw8 = _pad_rows(wt['ffn_dw'][i], 8)
        af, vv, vg = _ffn_act_fwd(uv, ug, dw8, row(wt['ffn_dw_b'][i]))
        af = af.reshape(t, f)
        if i + 1 < depth and (i + 1) % 3 != 1:
            xc, hn_next = _mm(af, wt['ffn_w_down'][i], residual=xc, norm_gain=row(wt['norm_mix'][i + 1]),
                              out_dtype=F32, name="ffn_down")
        else:
            xc, hn_next = _mm(af, wt['ffn_w_down'][i], residual=xc, out_dtype=F32, name="ffn_down"), None
        sv.update(hf=hf, uv=uv, ug=ug, vv=vv, vg=vg, af=af, dw8=dw8)
        saved.append(sv)

    dx, sq = _loss_head(xc, target.reshape(t, d))

    for i in reversed(range(depth)):
        j = i // 3
        kind = i % 3
        sv = saved[i]
        shf = (bsz, s_len, f)
        da = _mm(dx, wt['ffn_w_down'][i], trans_b=True, name="ffn_down_dgrad")
        gw, _ = _wgrad(sv['af'], dx, name="ffn_down_wgrad")
        grads['ffn_w_down'][i] = gw.reshape(N_DEV, f // N_DEV, d)
        dvv, dvg, dbv, dbg = _ffn_act_bwd1(sv['vv'], sv['vg'], da.reshape(shf))
        grads['ffn_dw_b'][i] = jnp.concatenate([_fold(dbv), _fold(dbg)])
        duv, dug, ddwv, ddwg = _ffn_act_bwd2(dvv, dvg, sv['uv'], sv['ug'], sv['dw8'])
        grads['ffn_dw'][i] = jnp.concatenate([ddwv.sum(axis=1), ddwg.sum(axis=1)], axis=1)
        duv, dug = duv.reshape(t, f), dug.reshape(t, f)
        gv, _ = _wgrad(sv['hf'], duv, name="ffn_up_wgrad")
        gg, _ = _wgrad(sv['hf'], dug, name="ffn_up_wgrad")
        grads['ffn_w_up'][i] = _split_lanes([gv, gg], 2 * f // N_DEV)
        dx, dg = _mm(duv, wt['ffn_w_up_v'][i], trans_b=True, a2=dug, b2=wt['ffn_w_up_g'][i],
                     norm_bwd=(sv['x_ffn'], row(wt['norm_ffn'][i]), dx), name="ffn_up_dgrad")
        grads['norm_ffn'][i] = _fold(dg)
        gm = row(wt['norm_mix'][i])
        if kind == 0:
            dsact = _mm(dx, wt['conv_w_out'][j], trans_b=True, name="conv_out_dgrad")
            gw, cs = _wgrad(sv['sact'], dx, name="conv_out_wgrad")
            grads['conv_w_out'][j] = gw.reshape(N_DEV, d // N_DEV, d)
            grads['conv_b_out'][j] = _fold(cs)
            du, dlg, dlb, dwb = _conv_act_bwd1(sv['u'], dsact, row(wt['conv_ln_g'][j]), row(wt['conv_ln_b'][j]))
            grads['conv_ln_g'][j], grads['conv_ln_b'][j], grads['conv_dw_b'][j] = _fold(dlg), _fold(dlb), _fold(dwb)
            dp, ddw = _conv_act_bwd2(du.reshape(bsz, s_len, d), sv['p'].reshape(bsz, s_len, 2 * d),
                                     _pad_rows(wt['conv_dw'][j], 32))
            grads['conv_dw'][j] = ddw.sum(axis=1)[:wt['conv_dw'].shape[1]]
            dp = dp.reshape(t, 2 * d)
            gw, cs = _wgrad(sv['hn'], dp, name="conv_in_wgrad")
            grads['conv_w_in'][j] = _split_lanes([gw], 2 * d // N_DEV)
            grads['conv_b_in'][j] = _fold(cs)
            dx, dg = _mm(dp, wt['conv_w_in'][j], trans_b=True, norm_bwd=(sv['x_mix'], gm, dx), name="conv_in_dgrad")
            grads['norm_mix'][i] = _fold(dg)
        elif kind == 1:
            shp = (bsz, s_len, d)
            dxn, dwp, dbp, dsc, dg = _pool_bwd(sv['x_mix'].reshape(shp), dx.reshape(shp), sv['p'], gm, wt['pool_w'][j],
                                               row(wt['pool_b'][j]), row(wt['pool_scale'][j]))
            dx = dxn.reshape(t, d)
            ng, cg = dwp.shape[0], dwp.shape[1]
            grads['pool_w'][j] = jnp.moveaxis(dwp.reshape(ng, N_DEV, cg // N_DEV, cg), 1, 0).astype(BF16)
            grads['pool_b'][j] = _fold(dbp).reshape(wt['pool_b'].shape[1:])
            grads['pool_scale'][j] = _fold(dsc)
            grads['norm_mix'][i] = _fold(dg)
        else:
            shp = (bsz, s_len, d)
            do = _mm(dx, wt['fox_w_o'][j], trans_b=True, name="fox_out_dgrad")
            gw, _ = _wgrad(sv['o'], dx, name="fox_out_wgrad")
            grads['fox_w_o'][j] = gw.reshape(N_DEV, d // N_DEV, d)
            fl_args = (sv['qn'].reshape(shp), sv['kn'].reshape(shp), sv['vb'].reshape(shp), do.reshape(shp),
                       sv['o'].reshape(shp), sv['lse'], sv['crow'])
            items, groups = [], []
            for n in late_names:
                members = [grads[n][l] for l in sorted(grads[n]) if l >= cut[n]]
                groups.append(list(range(len(items), len(items) + len(members))))
                items += members
            dq, dk, dv, dcrow, got = _flash_bwd(*fl_args, items=items, groups=groups)
            late_recv = dict(zip(late_names, got))
            dc = jnp.swapaxes(dcrow.reshape(bsz, n_heads, s_len), 1, 2)
            dc = jnp.pad(dc, ((0, 0), (0, 0), (0, LANES - n_heads)))
            dfl, dbf = _fox_gate_bwd(dc, sv['proj'].reshape(bsz, s_len, -1), sv['bf'], n_heads)
            grads['fox_b_f'][j] = _fold(dbf)[:n_heads]
            dqkv, dgq, dgk = _fox_qknorm_bwd(sv['proj'], dq.reshape(t, d), dk.reshape(t, d), dv.reshape(t, d),
                                             sv['gq'], sv['gk'], d)
            grads['fox_q_gain'][j] = _fold(dgq).reshape(n_heads, HEAD_DIM).sum(axis=0)
            grads['fox_k_gain'][j] = _fold(dgk).reshape(n_heads, HEAD_DIM).sum(axis=0)
            dproj = jnp.concatenate([dqkv, dfl.reshape(t, LANES)], axis=1)
            dwp, _ = _wgrad(sv['hn'], dproj, name="fox_in_wgrad")
            grads['fox_w_in'][j] = _split_lanes([dwp], (3 * d + n_heads) // N_DEV)
            dx, dg = _mm(dproj, sv['wp'], trans_b=True, norm_bwd=(sv['x_mix'], gm, dx), name="fox_in_dgrad")
            grads['norm_mix'][i] = _fold(dg)

    small = {n: jnp.stack([g[k] for k in sorted(g)]) for n, g in grads.items() if n not in MATRICES}
    big = {n: [grads[n][k] for k in sorted(grads[n]) if n not in late_recv or k < cut[n]] for n in MATRICES}
    return sq.sum(), dx.reshape(bsz, s_len, d), small, big, late_recv


def _train_step(x, target, w, m, v):
    depth = w['norm_mix'].shape[0]
    attn = [i for i in range(depth) if i % 3 == 2]
    cut = {n: w[n].shape[0] for n in MATRICES}
    if len(attn) == 1:
        cut['ffn_w_up'] = cut['ffn_w_down'] = attn[0]
        later_conv = [i // 3 for i in range(attn[0] + 1, depth) if i % 3 == 0]
        if later_conv:
            cut['conv_w_in'] = cut['conv_w_out'] = later_conv[0]
    late_shards = {n: w[n][cut[n]:].astype(BF16) for n in MATRICES if 0 < cut[n] < w[n].shape[0]}
    got = _gather([w[n][:cut[n]].astype(BF16) for n in MATRICES] + [_flat_rows([w[n] for n in SMALL])])
    wt = _prepare_matrices(dict(zip(MATRICES, got[:-1])))
    wt.update(_prepare_vectors(got[-1], w))
    wt.update({n: w[n] for n in REPLICATED})
    sq, grad_x, gsmall, gbig, late_recv = _local_step(x, target, wt, late_shards, cut)
    d = x.shape[-1]

    shard_rows = jnp.concatenate([_to_shards(gsmall[n], SHARD_AXIS[n]) for n in SMALL], axis=1)
    rep = jnp.concatenate([gsmall[n].reshape(-1) for n in REPLICATED] + [(0.5 / d) * sq.reshape(1)])
    rows = jnp.concatenate([shard_rows, jnp.broadcast_to(rep, (N_DEV, rep.shape[0]))], axis=1)
    chunk = 8 * ADAM_COLS
    n_all = rows.shape[1]
    n_pad = -(-n_all // chunk) * chunk
    rows = jnp.pad(rows, ((0, 0), (0, n_pad - n_all))).reshape(N_DEV, n_pad // ADAM_COLS, ADAM_COLS)

    items, groups = [], []
    for n in MATRICES:
        groups.append(list(range(len(items), len(items) + len(gbig[n]))))
        items += gbig[n]
    groups.append([len(items)])
    items.append(rows)
    recv = _scatter(items, groups)

    res = [{}, {}, {}, {}]
    for n, r in zip(MATRICES, recv[:-1]):
        if n in late_recv:
            c = cut[n]
            outs = zip(_adamw(r, w[n][:c], m[n][:c], v[n][:c]), _adamw(late_recv[n], w[n][c:], m[n][c:], v[n][c:]))
            outs = [jnp.concatenate(pair, axis=0) for pair in outs]
        else:
            outs = _adamw(r, w[n], m[n], v[n])
        for k, o in enumerate(outs):
            res[k][n] = o
    order = SMALL + REPLICATED

    def flat(tree):
        parts = jnp.concatenate([tree[n].reshape(-1) for n in order])
        return jnp.pad(parts, (0, n_pad - parts.shape[0])).reshape(n_pad // ADAM_COLS, ADAM_COLS)

    outs = [o.reshape(-1) for o in _adamw(recv[-1].reshape((N_CHIP,) + rows.shape[1:]), flat(w), flat(m), flat(v))]
    off = 0
    for n in order:
        size = w[n].size
        for k in range(4):
            res[k][n] = outs[k][off:off + size].reshape(w[n].shape)
        off += size
    loss = outs[0][n_all - 1]
    return (loss, grad_x, *[res[0][n] for n in WEIGHTS], *[res[1][n] for n in WEIGHTS],
            *[res[2][n] for n in WEIGHTS], *[res[3][n] for n in WEIGHTS])


def kernel(x, norm_mix, norm_ffn, conv_w_in, conv_b_in, conv_dw, conv_dw_b, conv_ln_g, conv_ln_b, conv_w_out, conv_b_out, pool_w, pool_b, pool_scale, fox_w_in, fox_b_f, fox_q_gain, fox_k_gain, fox_w_o, ffn_w_up, ffn_dw, ffn_dw_b, ffn_w_down, loss_target, m_norm_mix, m_norm_ffn, m_conv_w_in, m_conv_b_in, m_conv_dw, m_conv_dw_b, m_conv_ln_g, m_conv_ln_b, m_conv_w_out, m_conv_b_out, m_pool_w, m_pool_b, m_pool_scale, m_fox_w_in, m_fox_b_f, m_fox_q_gain, m_fox_k_gain, m_fox_w_o, m_ffn_w_up, m_ffn_dw, m_ffn_dw_b, m_ffn_w_down, v_norm_mix, v_norm_ffn, v_conv_w_in, v_conv_b_in, v_conv_dw, v_conv_dw_b, v_conv_ln_g, v_conv_ln_b, v_conv_w_out, v_conv_b_out, v_pool_w, v_pool_b, v_pool_scale, v_fox_w_in, v_fox_b_f, v_fox_q_gain, v_fox_k_gain, v_fox_w_o, v_ffn_w_up, v_ffn_dw, v_ffn_dw_b, v_ffn_w_down):
    w = dict(zip(WEIGHTS, (norm_mix, norm_ffn, conv_w_in, conv_b_in, conv_dw, conv_dw_b, conv_ln_g, conv_ln_b, conv_w_out, conv_b_out, pool_w, pool_b, pool_scale, fox_w_in, fox_b_f, fox_q_gain, fox_k_gain, fox_w_o, ffn_w_up, ffn_dw, ffn_dw_b, ffn_w_down)))
    m = dict(zip(WEIGHTS, (m_norm_mix, m_norm_ffn, m_conv_w_in, m_conv_b_in, m_conv_dw, m_conv_dw_b, m_conv_ln_g, m_conv_ln_b, m_conv_w_out, m_conv_b_out, m_pool_w, m_pool_b, m_pool_scale, m_fox_w_in, m_fox_b_f, m_fox_q_gain, m_fox_k_gain, m_fox_w_o, m_ffn_w_up, m_ffn_dw, m_ffn_dw_b, m_ffn_w_down)))
    v = dict(zip(WEIGHTS, (v_norm_mix, v_norm_ffn, v_conv_w_in, v_conv_b_in, v_conv_dw, v_conv_dw_b, v_conv_ln_g, v_conv_ln_b, v_conv_w_out, v_conv_b_out, v_pool_w, v_pool_b, v_pool_scale, v_fox_w_in, v_fox_b_f, v_fox_q_gain, v_fox_k_gain, v_fox_w_o, v_ffn_w_up, v_ffn_dw, v_ffn_dw_b, v_ffn_w_down)))
    return _train_step(x, loss_target, w, m, v)
```

```python
import functools
import math

import jax
import jax.numpy as jnp
from jax import lax
from jax.experimental import pallas as pl
from jax.experimental.pallas import tpu as pltpu

F32, BF16 = jnp.float32, jnp.bfloat16
SDS = jax.ShapeDtypeStruct

N_DEV = 8
EPS = 1e-6
POOL_WINDOWS = (2, 4, 8, 16)
HEAD_DIM = 64
ADAM_LR, ADAM_B1, ADAM_B2, ADAM_EPS, ADAM_WD, ADAM_STEP = 0.001, 0.9, 0.999, 1e-08, 0.01, 10
LANES = 128
VMEM_LIMIT_BYTES = 48 * 1024 * 1024
RESIDENT_WEIGHT_BYTES = 7 * 1024 * 1024
NEG = -1e30

WEIGHTS = ['norm_mix', 'norm_ffn', 'conv_w_in', 'conv_b_in', 'conv_dw', 'conv_dw_b', 'conv_ln_g', 'conv_ln_b',
           'conv_w_out', 'conv_b_out', 'pool_w', 'pool_b', 'pool_scale', 'fox_w_in', 'fox_b_f', 'fox_q_gain',
           'fox_k_gain', 'fox_w_o', 'ffn_w_up', 'ffn_dw', 'ffn_dw_b', 'ffn_w_down']
SHARD_AXIS = {'conv_w_in': 2, 'conv_b_in': 1, 'conv_dw': 2, 'conv_dw_b': 1, 'conv_ln_g': 1, 'conv_ln_b': 1,
              'conv_w_out': 1, 'conv_b_out': 1, 'pool_w': 2, 'pool_b': 2, 'fox_w_in': 2, 'fox_w_o': 1,
              'ffn_w_up': 2, 'ffn_dw': 2, 'ffn_w_down': 1}
MATRICES = ('conv_w_in', 'conv_w_out', 'pool_w', 'fox_w_in', 'fox_w_o', 'ffn_w_up', 'ffn_w_down')
SHARDED = [n for n in WEIGHTS if n in SHARD_AXIS]
REPLICATED = [n for n in WEIGHTS if n not in SHARD_AXIS]


def _cp(*sem):
    return pltpu.CompilerParams(dimension_semantics=sem, vmem_limit_bytes=VMEM_LIMIT_BYTES)


def _tile(n, pref, align=8):
    if n <= pref:
        return n
    t = (pref // align) * align
    while t >= align:
        if n % t == 0:
            return t
        t -= align
    return n


def _fold8(x):
    r, c = x.shape
    return x.reshape(r // 8, 8, c).sum(axis=0)


def _sigmoid(x):
    return 0.5 * jnp.tanh(0.5 * x) + 0.5


def _shifts_back(cur, tail, n):
    hb = tail.shape[0]
    xe = jnp.concatenate([tail, cur], axis=0)
    return [cur] + [pltpu.roll(xe, j, axis=0)[hb:] for j in range(1, n)]


def _shifts_fwd(cur, head, n):
    ts = cur.shape[0]
    xe = jnp.concatenate([cur, head], axis=0)
    ln = xe.shape[0]
    return [cur] + [pltpu.roll(xe, ln - j, axis=0)[:ts] for j in range(1, n)]


def _dot_hi(a, b):
    return jnp.dot(a, b, preferred_element_type=F32, precision=lax.Precision.HIGHEST)


def _rmsnorm_fwd(x, g):
    t, d = x.shape
    tm = _tile(t, 512)

    def body(x_ref, g_ref, h_ref):
        xv = x_ref[...]
        r = lax.rsqrt(jnp.mean(xv * xv, axis=-1, keepdims=True) + EPS)
        h_ref[...] = (xv * r * g_ref[...]).astype(BF16)

    return pl.pallas_call(
        body, out_shape=SDS((t, d), BF16), grid=(t // tm,), name="rmsnorm_fwd",
        in_specs=[pl.BlockSpec((tm, d), lambda i: (i, 0)), pl.BlockSpec((1, d), lambda i: (0, 0))],
        out_specs=pl.BlockSpec((tm, d), lambda i: (i, 0)), compiler_params=_cp("parallel"))(x, g)


def _mm(a, b, *, trans_b=False, bias=None, residual=None, a2=None, b2=None, norm_gain=None, norm_bwd=None,
        out_dtype=BF16, name="mm"):
    m, k = a.shape
    n = b.shape[0] if trans_b else b.shape[1]
    tm, tn, tk = _tile(m, 1024, 16), _tile(n, 1536, LANES), _tile(k, 1536, LANES)
    small_b = k * n * 2 <= RESIDENT_WEIGHT_BYTES
    if norm_bwd is not None:
        if a2 is not None:
            tm, tk = _tile(m, 256, 16), k
        elif small_b:
            tm, tk = _tile(m, 512, 16), k
        else:
            tk = _tile(k, 768, LANES)
    elif small_b and tk < k:
        tm, tk = _tile(m, 512, 16), k
    nk = k // tk
    two = a2 is not None
    steps = 2 * nk if two else nk
    dims = (((1,), (1,)), ((), ())) if trans_b else (((1,), (0,)), ((), ()))
    has_bias, has_res, has_norm, has_nbwd = bias is not None, residual is not None, norm_gain is not None, \
        norm_bwd is not None
    assert not (has_norm or has_nbwd) or tn == n
    n_out = 2 if (has_norm or has_nbwd) else 1

    def body(*refs):
        pos = 4 if two else 2
        bias_ref = refs[pos] if has_bias else None
        pos += has_bias
        res_ref = refs[pos] if has_res else None
        pos += has_res
        gain_ref = refs[pos] if has_norm else None
        pos += has_norm
        x_ref, g_ref, dres_ref = refs[pos:pos + 3] if has_nbwd else (None, None, None)
        pos += 3 * has_nbwd
        outs = refs[pos:pos + n_out]
        first = (pl.program_id(0) == 0) & (pl.program_id(1) == 0) & (pl.program_id(2) == 0)

        def finish(r):
            if has_bias:
                r = r + bias_ref[...]
            if has_res:
                r = r + res_ref[...]
            if has_nbwd:
                xv = x_ref[...]
                rs = lax.rsqrt(jnp.mean(xv * xv, axis=-1, keepdims=True) + EPS)
                xh = xv * rs
                u = r * g_ref[...]
                outs[0][...] = dres_ref[...] + rs * (u - xh * jnp.mean(u * xh, axis=-1, keepdims=True))
                outs[1][...] += _fold8(r * xh)
                return
            outs[0][...] = r.astype(out_dtype)
            if has_norm:
                outs[1][...] = (r * lax.rsqrt(jnp.mean(r * r, axis=-1, keepdims=True) + EPS)
                                * gain_ref[...]).astype(BF16)

        def dot(a_ref, b_ref):
            return lax.dot_general(a_ref[...].astype(BF16), b_ref[...].astype(BF16), dims, preferred_element_type=F32)

        if has_nbwd:
            @pl.when(first)
            def _():
                outs[1][...] = jnp.zeros_like(outs[1])

        if steps == 1:
            finish(dot(refs[0], refs[1]))
            return
        acc_ref = refs[-1]
        kk = pl.program_id(2)

        @pl.when(kk == 0)
        def _():
            acc_ref[...] = jnp.zeros_like(acc_ref)

        @pl.when(kk < nk)
        def _():
            acc_ref[...] += dot(refs[0], refs[1])

        if two:
            @pl.when(kk >= nk)
            def _():
                acc_ref[...] += dot(refs[2], refs[3])

        @pl.when(kk == steps - 1)
        def _():
            finish(acc_ref[...])

    def pair(first):
        kmap = (lambda kk: jnp.minimum(kk, nk - 1)) if first else (lambda kk: jnp.maximum(kk - nk, 0))
        a_spec = pl.BlockSpec((tm, tk), lambda j, i, kk: (i, kmap(kk)))
        if trans_b:
            b_spec = pl.BlockSpec((tn, tk), lambda j, i, kk: (j, kmap(kk)))
        else:
            b_spec = pl.BlockSpec((tk, tn), lambda j, i, kk: (kmap(kk), j))
        return [a_spec, b_spec]

    in_specs, args = pair(True), [a, b]
    if two:
        in_specs += pair(False)
        args += [a2, b2]
    if has_bias:
        in_specs.append(pl.BlockSpec((1, tn), lambda j, i, kk: (0, j)))
        args.append(bias)
    tile = pl.BlockSpec((tm, tn), lambda j, i, kk: (i, j))
    vec = pl.BlockSpec((1, tn), lambda j, i, kk: (0, j))
    if has_res:
        in_specs.append(tile)
        args.append(residual)
    if has_norm:
        in_specs.append(vec)
        args.append(norm_gain)
    if has_nbwd:
        in_specs += [tile, vec, tile]
        args += list(norm_bwd)
    out_shape, out_specs = [SDS((m, n), F32 if has_nbwd else out_dtype)], [tile]
    if has_norm:
        out_shape.append(SDS((m, n), BF16))
        out_specs.append(tile)
    if has_nbwd:
        out_shape.append(SDS((8, n), F32))
        out_specs.append(pl.BlockSpec((8, tn), lambda j, i, kk: (0, 0)))
    outs = pl.pallas_call(
        body, out_shape=out_shape, grid=(n // tn, m // tm, steps), name=name, in_specs=in_specs, out_specs=out_specs,
        scratch_shapes=[] if steps == 1 else [pltpu.VMEM((tm, tn), F32)],
        compiler_params=_cp("arbitrary", "arbitrary", "arbitrary"))(*args)
    return outs[0] if n_out == 1 else tuple(outs)


def _wgrad(a, g, *, out_dtype=BF16, name="wgrad"):
    m, ka = a.shape
    n = g.shape[1]
    ta, tn = _tile(ka, 1536, LANES), _tile(n, 1536, LANES)
    tm = _tile(m, 2048 if g.dtype == BF16 else 1024)
    nm = m // tm

    def body(a_ref, g_ref, o_ref, cs_ref, acc_ref):
        i, mm = pl.program_id(1), pl.program_id(2)

        @pl.when(mm == 0)
        def _():
            acc_ref[...] = jnp.zeros_like(acc_ref)

        @pl.when((mm == 0) & (i == 0))
        def _():
            cs_ref[...] = jnp.zeros_like(cs_ref)

        gv = g_ref[...]
        acc_ref[...] += lax.dot_general(a_ref[...].astype(BF16), gv.astype(BF16), (((0,), (0,)), ((), ())),
                                        preferred_element_type=F32)

        @pl.when(i == 0)
        def _():
            cs_ref[...] += _fold8(gv.astype(F32))

        @pl.when(mm == nm - 1)
        def _():
            o_ref[...] = acc_ref[...].astype(out_dtype)

    return pl.pallas_call(
        body, out_shape=(SDS((ka, n), out_dtype), SDS((8, n), F32)), grid=(n // tn, ka // ta, nm), name=name,
        in_specs=[pl.BlockSpec((tm, ta), lambda j, i, mm: (mm, i)), pl.BlockSpec((tm, tn), lambda j, i, mm: (mm, j))],
        out_specs=(pl.BlockSpec((ta, tn), lambda j, i, mm: (i, j)), pl.BlockSpec((8, tn), lambda j, i, mm: (0, j))),
        scratch_shapes=[pltpu.VMEM((ta, tn), F32)],
        compiler_params=_cp("arbitrary", "arbitrary", "arbitrary"))(a, g)


FFN_HALO = 16


def _ffn_conv(uc_ref, up_ref, w_ref, b_ref, s):
    u = uc_ref[...].astype(F32)
    tail = jnp.where(s > 0, up_ref[...].astype(F32), 0.0)
    sh = _shifts_back(u, tail, 3)
    return sh, sh[2] * w_ref[0:1, :] + sh[1] * w_ref[1:2, :] + sh[0] * w_ref[2:3, :] + b_ref[...]


def _ffn_act_fwd(uv, ug, dw8, b):
    bsz, s_len, f = uv.shape
    tc, ts = f, _tile(s_len, 256, FFN_HALO)
    nf, r = f // tc, ts // FFN_HALO

    def body(uv_ref, uvp_ref, ug_ref, ugp_ref, wv_ref, wg_ref, bv_ref, bg_ref, a_ref, vv_ref, vg_ref):
        s = pl.program_id(2)
        _, val = _ffn_conv(uv_ref, uvp_ref, wv_ref, bv_ref, s)
        _, gate = _ffn_conv(ug_ref, ugp_ref, wg_ref, bg_ref, s)
        a_ref[...] = (gate * _sigmoid(gate) * val).astype(BF16)
        vv_ref[...] = val.astype(BF16)
        vg_ref[...] = gate.astype(BF16)

    cur = pl.BlockSpec((None, ts, tc), lambda bi, j, s: (bi, s, j))
    prev = pl.BlockSpec((None, FFN_HALO, tc), lambda bi, j, s: (bi, jnp.maximum(s * r - 1, 0), j))

    def par(rows, off):
        return pl.BlockSpec((rows, tc), lambda bi, j, s: (0, j + off))

    return pl.pallas_call(
        body, out_shape=(SDS((bsz, s_len, f), BF16),) * 3, grid=(bsz, nf, s_len // ts), name="ffn_act_fwd",
        in_specs=[cur, prev, cur, prev, par(8, 0), par(8, nf), par(1, 0), par(1, nf)], out_specs=(cur, cur, cur),
        compiler_params=_cp("parallel", "parallel", "arbitrary"))(uv, uv, ug, ug, dw8, dw8, b, b)


def _ffn_act_bwd1(vv, vg, da):
    bsz, s_len, f = vv.shape
    tc, ts = f, _tile(s_len, 256, FFN_HALO)
    nf = f // tc

    def body(vv_ref, vg_ref, da_ref, dvv_ref, dvg_ref, dbv_ref, dbg_ref):
        @pl.when((pl.program_id(1) == 0) & (pl.program_id(2) == 0))
        def _():
            dbv_ref[...] = jnp.zeros_like(dbv_ref)
            dbg_ref[...] = jnp.zeros_like(dbg_ref)

        val, gate = vv_ref[...].astype(F32), vg_ref[...].astype(F32)
        sg = _sigmoid(gate)
        dav = da_ref[...].astype(F32)
        dval = dav * gate * sg
        dgate = dav * val * (sg * (1.0 + gate * (1.0 - sg)))
        dvv_ref[...] = dval.astype(BF16)
        dvg_ref[...] = dgate.astype(BF16)
        dbv_ref[...] += _fold8(dval)
        dbg_ref[...] += _fold8(dgate)

    cur = pl.BlockSpec((None, ts, tc), lambda j, bi, s: (bi, s, j))
    acc1 = pl.BlockSpec((8, tc), lambda j, bi, s: (0, j))
    return pl.pallas_call(
        body, out_shape=(SDS((bsz, s_len, f), BF16), SDS((bsz, s_len, f), BF16), SDS((8, f), F32), SDS((8, f), F32)),
        grid=(nf, bsz, s_len // ts), name="ffn_act_bwd1", in_specs=[cur, cur, cur], out_specs=(cur, cur, acc1, acc1),
        compiler_params=_cp("arbitrary", "arbitrary", "arbitrary"))(vv, vg, da)


def _ffn_act_bwd2(dvv, dvg, uv, ug, dw8):
    bsz, s_len, f = dvv.shape
    tc, ts = f, _tile(s_len, 256, FFN_HALO)
    nf, r, ns = f // tc, ts // FFN_HALO, s_len // ts

    def body(vc_ref, vn_ref, gc_ref, gn_ref, uv_ref, ug_ref, wv_ref, wg_ref, duv_ref, dug_ref, ddwv_ref, ddwg_ref):
        bi, s = pl.program_id(1), pl.program_id(2)

        @pl.when((bi == 0) & (s == 0))
        def _():
            ddwv_ref[...] = jnp.zeros_like(ddwv_ref)
            ddwg_ref[...] = jnp.zeros_like(ddwg_ref)

        for dc_ref, dn_ref, u_ref, w_ref, du_ref, ddw_ref in (
                (vc_ref, vn_ref, uv_ref, wv_ref, duv_ref, ddwv_ref),
                (gc_ref, gn_ref, ug_ref, wg_ref, dug_ref, ddwg_ref)):
            d = dc_ref[...].astype(F32)
            head = jnp.where(s < ns - 1, dn_ref[...].astype(F32), 0.0)
            sh = _shifts_fwd(d, head, 3)
            du_ref[...] = (sh[0] * w_ref[2:3, :] + sh[1] * w_ref[1:2, :] + sh[2] * w_ref[0:1, :]).astype(BF16)
            u = u_ref[...].astype(F32)
            for j in range(3):
                ddw_ref[2 - j] += _fold8(sh[j] * u)

    cur = pl.BlockSpec((None, ts, tc), lambda j, bi, s: (bi, s, j))
    nxt = pl.BlockSpec((None, FFN_HALO, tc),
                       lambda j, bi, s: (bi, jnp.minimum((s + 1) * r, s_len // FFN_HALO - 1), j))

    def par(off):
        return pl.BlockSpec((8, tc), lambda j, bi, s: (0, j + off))

    acc3 = pl.BlockSpec((3, 8, tc), lambda j, bi, s: (0, 0, j))
    return pl.pallas_call(
        body, out_shape=(SDS((bsz, s_len, f), BF16), SDS((bsz, s_len, f), BF16), SDS((3, 8, f), F32),
                         SDS((3, 8, f), F32)),
        grid=(nf, bsz, ns), name="ffn_act_bwd2",
        in_specs=[cur, nxt, cur, nxt, cur, cur, par(0), par(nf)], out_specs=(cur, cur, acc3, acc3),
        compiler_params=_cp("arbitrary", "arbitrary", "arbitrary"))(dvv, dvv, dvg, dvg, uv, ug, dw8, dw8)


CONV_HALO = 32
CONV_CHUNK = 256


def _conv_act_fwd(p, dw32, dwb, ln_g, ln_b):
    bsz, s_len, d2 = p.shape
    d = d2 // 2
    kw = 31
    ts = _tile(s_len, 256, CONV_HALO)
    r = ts // CONV_HALO
    cc = min(CONV_CHUNK, d)

    def body(pc_ref, pp_ref, w_ref, wb_ref, g_ref, b_ref, u_ref, s_ref):
        s = pl.program_id(1)
        tot = jnp.zeros((ts, 1), F32)
        for c0 in range(0, d, cc):
            a = pc_ref[:, c0:c0 + cc].astype(F32)
            g = pc_ref[:, d + c0:d + c0 + cc].astype(F32)
            z = a * _sigmoid(g)
            ap = pp_ref[:, c0:c0 + cc].astype(F32)
            gp = pp_ref[:, d + c0:d + c0 + cc].astype(F32)
            tail = jnp.where(s > 0, ap * _sigmoid(gp), 0.0)
            sh = _shifts_back(z, tail, kw)
            acc = wb_ref[:, c0:c0 + cc] + sh[0] * w_ref[kw - 1:kw, c0:c0 + cc]
            for j in range(1, kw):
                acc = acc + sh[j] * w_ref[kw - 1 - j:kw - j, c0:c0 + cc]
            u_ref[:, c0:c0 + cc] = acc
            tot = tot + jnp.sum(acc, axis=-1, keepdims=True)
        u = u_ref[...]
        mu = tot / d
        uc = u - mu
        var = jnp.mean(uc * uc, axis=-1, keepdims=True)
        ul = uc * lax.rsqrt(var + EPS) * g_ref[...] + b_ref[...]
        s_ref[...] = (ul * _sigmoid(ul)).astype(BF16)

    vec = pl.BlockSpec((1, d), lambda bi, s: (0, 0))
    return pl.pallas_call(
        body, out_shape=(SDS((bsz, s_len, d), F32), SDS((bsz, s_len, d), BF16)), grid=(bsz, s_len // ts),
        name="conv_act_fwd",
        in_specs=[pl.BlockSpec((None, ts, d2), lambda bi, s: (bi, s, 0)),
                  pl.BlockSpec((None, CONV_HALO, d2), lambda bi, s: (bi, jnp.maximum(s * r - 1, 0), 0)),
                  pl.BlockSpec((32, d), lambda bi, s: (0, 0)), vec, vec, vec],
        out_specs=(pl.BlockSpec((None, ts, d), lambda bi, s: (bi, s, 0)),
                   pl.BlockSpec((None, ts, d), lambda bi, s: (bi, s, 0))),
        compiler_params=_cp("parallel", "arbitrary"))(p, p, dw32, dwb, ln_g, ln_b)


def _conv_act_bwd1(u, ds, ln_g, ln_b):
    t, d = u.shape
    ts = _tile(t, 256)

    def body(u_ref, ds_ref, g_ref, b_ref, du_ref, dg_ref, db_ref, dwb_ref):
        @pl.when(pl.program_id(0) == 0)
        def _():
            dg_ref[...] = jnp.zeros_like(dg_ref)
            db_ref[...] = jnp.zeros_like(db_ref)
            dwb_ref[...] = jnp.zeros_like(dwb_ref)

        uv = u_ref[...]
        uc = uv - jnp.mean(uv, axis=-1, keepdims=True)
        rstd = lax.rsqrt(jnp.mean(uc * uc, axis=-1, keepdims=True) + EPS)
        uh = uc * rstd
        ul = uh * g_ref[...] + b_ref[...]
        sg = _sigmoid(ul)
        dul = ds_ref[...].astype(F32) * (sg * (1.0 + ul * (1.0 - sg)))
        duh = dul * g_ref[...]
        du = rstd * (duh - jnp.mean(duh, axis=-1, keepdims=True) - uh * jnp.mean(duh * uh, axis=-1, keepdims=True))
        du_ref[...] = du
        dg_ref[...] += _fold8(dul * uh)
        db_ref[...] += _fold8(dul)
        dwb_ref[...] += _fold8(du)

    row = pl.BlockSpec((ts, d), lambda i: (i, 0))
    vec = pl.BlockSpec((1, d), lambda i: (0, 0))
    acc = pl.BlockSpec((8, d), lambda i: (0, 0))
    return pl.pallas_call(
        body, out_shape=(SDS((t, d), F32), SDS((8, d), F32), SDS((8, d), F32), SDS((8, d), F32)), grid=(t // ts,),
        name="conv_act_bwd1", in_specs=[row, row, vec, vec], out_specs=(row, acc, acc, acc),
        compiler_params=_cp("arbitrary"))(u, ds, ln_g, ln_b)


def _conv_act_bwd2(du, p, dw32):
    bsz, s_len, d2 = p.shape
    d = d2 // 2
    kw = 31
    ts = _tile(s_len, 256, CONV_HALO)
    r, ns = ts // CONV_HALO, s_len // ts
    cc = min(CONV_CHUNK, d)

    def body(dc_ref, dn_ref, pc_ref, pp_ref, w_ref, dp_ref, ddw_ref):
        bi, s = pl.program_id(0), pl.program_id(1)

        @pl.when((bi == 0) & (s == 0))
        def _():
            ddw_ref[...] = jnp.zeros_like(ddw_ref)

        for c0 in range(0, d, cc):
            a = pc_ref[:, c0:c0 + cc].astype(F32)
            g = pc_ref[:, d + c0:d + c0 + cc].astype(F32)
            sg = _sigmoid(g)
            z = a * sg
            ap = pp_ref[:, c0:c0 + cc].astype(F32)
            gp = pp_ref[:, d + c0:d + c0 + cc].astype(F32)
            tail = jnp.where(s > 0, ap * _sigmoid(gp), 0.0)
            duv = dc_ref[:, c0:c0 + cc]
            head = jnp.where(s < ns - 1, dn_ref[:, c0:c0 + cc], 0.0)
            zb = _shifts_back(z, tail, kw)
            for k in range(kw):
                ddw_ref[k, :, c0:c0 + cc] += _fold8(duv * zb[kw - 1 - k])
            df = _shifts_fwd(duv, head, kw)
            dz = df[0] * w_ref[kw - 1:kw, c0:c0 + cc]
            for j in range(1, kw):
                dz = dz + df[j] * w_ref[kw - 1 - j:kw - j, c0:c0 + cc]
            dp_ref[:, c0:c0 + cc] = (dz * sg).astype(BF16)
            dp_ref[:, d + c0:d + c0 + cc] = (dz * a * sg * (1.0 - sg)).astype(BF16)

    return pl.pallas_call(
        body, out_shape=(SDS((bsz, s_len, d2), BF16), SDS((32, 8, d), F32)), grid=(bsz, ns), name="conv_act_bwd2",
        in_specs=[pl.BlockSpec((None, ts, d), lambda bi, s: (bi, s, 0)),
                  pl.BlockSpec((None, CONV_HALO, d),
                               lambda bi, s: (bi, jnp.minimum((s + 1) * r, s_len // CONV_HALO - 1), 0)),
                  pl.BlockSpec((None, ts, d2), lambda bi, s: (bi, s, 0)),
                  pl.BlockSpec((None, CONV_HALO, d2), lambda bi, s: (bi, jnp.maximum(s * r - 1, 0), 0)),
                  pl.BlockSpec((32, d), lambda bi, s: (0, 0))],
        out_specs=(pl.BlockSpec((None, ts, d2), lambda bi, s: (bi, s, 0)),
                   pl.BlockSpec((32, 8, d), lambda bi, s: (0, 0, 0))),
        compiler_params=_cp("arbitrary", "arbitrary"))(du, du, p, p, dw32)


POOL_HALO = 16


def _pool_counts(s, ts, rows, w):
    t = s * ts + lax.broadcasted_iota(jnp.int32, (rows, 1), 0)
    return jnp.minimum(t + 1, w).astype(F32)


def _pool_fwd(x, gmix, w, b, scale):
    bsz, s_len, d = x.shape
    ng = len(POOL_WINDOWS)
    cg = d // ng
    ts = _tile(s_len, 512, POOL_HALO)
    r = ts // POOL_HALO

    def body(xc_ref, xp_ref, g_ref, w_ref, b_ref, sc_ref, y_ref, p_ref):
        s = pl.program_id(1)

        def norm(v):
            return v * lax.rsqrt(jnp.mean(v * v, axis=-1, keepdims=True) + EPS) * g_ref[...]

        xc = xc_ref[...]
        h = norm(xc)
        tail = jnp.where(s > 0, norm(xp_ref[...]), 0.0)
        for gi, win in enumerate(POOL_WINDOWS):
            lo, hi = gi * cg, (gi + 1) * cg
            hg = h[:, lo:hi]
            acc = jnp.concatenate([tail[:, lo:hi], hg], axis=0)
            step = 1
            while step < win:
                acc = acc + pltpu.roll(acc, step, axis=0)
                step *= 2
            pg = acc[POOL_HALO:] / _pool_counts(s, ts, ts, win) - hg
            pb = pg.astype(BF16)
            p_ref[:, lo:hi] = pb
            yg = jnp.dot(pb, w_ref[gi], preferred_element_type=F32) + b_ref[:, lo:hi]
            y_ref[:, lo:hi] = xc[:, lo:hi] + yg * sc_ref[:, lo:hi]

    vec = pl.BlockSpec((1, d), lambda bi, s: (0, 0))
    blk = pl.BlockSpec((None, ts, d), lambda bi, s: (bi, s, 0))
    return pl.pallas_call(
        body, out_shape=(SDS((bsz, s_len, d), F32), SDS((bsz, s_len, d), BF16)), grid=(bsz, s_len // ts),
        name="pool_fwd",
        in_specs=[blk, pl.BlockSpec((None, POOL_HALO, d), lambda bi, s: (bi, jnp.maximum(s * r - 1, 0), 0)),
                  vec, pl.BlockSpec((ng, cg, cg), lambda bi, s: (0, 0, 0)), vec, vec],
        out_specs=(blk, blk), compiler_params=_cp("parallel", "arbitrary"))(x, x, gmix, w, b, scale)


def _pool_bwd(x, dy, p, gmix, w, b, scale):
    bsz, s_len, d = x.shape
    ng = len(POOL_WINDOWS)
    cg = d // ng
    ts = _tile(s_len, 512, POOL_HALO)
    r, ns = ts // POOL_HALO, s_len // ts
    nt = (((1,), (1,)), ((), ()))
    tn = (((0,), (0,)), ((), ()))

    def body(x_ref, dy_ref, dyn_ref, p_ref, g_ref, w_ref, b_ref, sc_ref, dx_ref, dw_ref, db_ref, dsc_ref, dg_ref):
        bi, s = pl.program_id(0), pl.program_id(1)

        @pl.when((bi == 0) & (s == 0))
        def _():
            dw_ref[...] = jnp.zeros_like(dw_ref)
            db_ref[...] = jnp.zeros_like(db_ref)
            dsc_ref[...] = jnp.zeros_like(dsc_ref)
            dg_ref[...] = jnp.zeros_like(dg_ref)

        dy = dy_ref[...]
        dyy = dy * sc_ref[...]
        dyy_n = jnp.where(s < ns - 1, dyn_ref[...] * sc_ref[...], 0.0)
        db_ref[...] += _fold8(dyy)
        xv = x_ref[...]
        rr = lax.rsqrt(jnp.mean(xv * xv, axis=-1, keepdims=True) + EPS)
        xh = xv * rr
        for gi, win in enumerate(POOL_WINDOWS):
            lo, hi = gi * cg, (gi + 1) * cg
            pb = p_ref[:, lo:hi]
            wg = w_ref[gi]
            pre = jnp.dot(pb, wg, preferred_element_type=F32) + b_ref[:, lo:hi]
            dsc_ref[:, lo:hi] += _fold8(dy[:, lo:hi] * pre)
            dyb = dyy[:, lo:hi].astype(BF16)
            dw_ref[gi] += lax.dot_general(pb, dyb, tn, preferred_element_type=F32)
            dp = lax.dot_general(dyb, wg, nt, preferred_element_type=F32)
            dp_n = lax.dot_general(dyy_n[:, lo:hi].astype(BF16), wg, nt, preferred_element_type=F32)
            q = dp / _pool_counts(s, ts, ts, win)
            q_n = dp_n / _pool_counts(s + 1, ts, POOL_HALO, win)
            acc = jnp.concatenate([q, q_n], axis=0)
            ln = ts + POOL_HALO
            step = 1
            while step < win:
                acc = acc + pltpu.roll(acc, ln - step, axis=0)
                step *= 2
            dh = acc[:ts] - dp
            xhg = xh[:, lo:hi]
            dg_ref[:, lo:hi] += _fold8(dh * xhg)
            dx_ref[:, lo:hi] = dh * g_ref[:, lo:hi]
        u = dx_ref[...]
        dx_ref[...] = dy + rr * (u - xh * jnp.mean(u * xh, axis=-1, keepdims=True))

    vec = pl.BlockSpec((1, d), lambda bi, s: (0, 0))
    acc8 = pl.BlockSpec((8, d), lambda bi, s: (0, 0))
    blk = pl.BlockSpec((None, ts, d), lambda bi, s: (bi, s, 0))
    wspec = pl.BlockSpec((ng, cg, cg), lambda bi, s: (0, 0, 0))
    return pl.pallas_call(
        body, out_shape=(SDS((bsz, s_len, d), F32), SDS((ng, cg, cg), F32), SDS((8, d), F32), SDS((8, d), F32),
                         SDS((8, d), F32)),
        grid=(bsz, ns), name="pool_bwd",
        in_specs=[blk, blk,
                  pl.BlockSpec((None, POOL_HALO, d),
                               lambda bi, s: (bi, jnp.minimum((s + 1) * r, s_len // POOL_HALO - 1), 0)),
                  blk, vec, wspec, vec, vec],
        out_specs=(blk, wspec, acc8, acc8, acc8),
        compiler_params=_cp("arbitrary", "arbitrary"))(x, dy, dy, p, gmix, w, b, scale)


def _tri(n, upper):
    row = lax.broadcasted_iota(jnp.int32, (n, n), 0)
    col = lax.broadcasted_iota(jnp.int32, (n, n), 1)
    return jnp.where((col >= row) if upper else (col <= row), 1.0, 0.0).astype(F32)


def _fox_gate_fwd(proj, bf, n_heads):
    bsz, s_len, width = proj.shape
    col = width // LANES - 1
    ts = _tile(s_len, 512)

    def body(fl_ref, b_ref, c_ref, carry_ref):
        @pl.when(pl.program_id(1) == 0)
        def _():
            carry_ref[...] = jnp.zeros_like(carry_ref)

        xv = fl_ref[...] + b_ref[...]
        logf = jnp.minimum(xv, 0.0) - jnp.log(1.0 + jnp.exp(-jnp.abs(xv)))
        lane = lax.broadcasted_iota(jnp.int32, (1, LANES), 1)
        logf = jnp.where(lane < n_heads, logf, 0.0)
        c = _dot_hi(_tri(ts, False), logf) + carry_ref[0:1, :]
        c_ref[...] = c
        carry_ref[0:1, :] = c[ts - 1:ts, :]

    return pl.pallas_call(
        body, out_shape=SDS((bsz, s_len, LANES), F32), grid=(bsz, s_len // ts), name="fox_gate_fwd",
        in_specs=[pl.BlockSpec((None, ts, LANES), lambda bi, s: (bi, s, col)),
                  pl.BlockSpec((1, LANES), lambda bi, s: (0, 0))],
        out_specs=pl.BlockSpec((None, ts, LANES), lambda bi, s: (bi, s, 0)),
        scratch_shapes=[pltpu.VMEM((8, LANES), F32)],
        compiler_params=_cp("arbitrary", "arbitrary"))(proj, bf)


def _fox_gate_bwd(dc, proj, bf, n_heads):
    bsz, s_len, width = proj.shape
    col = width // LANES - 1
    ts = _tile(s_len, 512)
    ns = s_len // ts

    def body(dc_ref, fl_ref, b_ref, dfl_ref, db_ref, carry_ref):
        bi, s = pl.program_id(0), pl.program_id(1)

        @pl.when((bi == 0) & (s == 0))
        def _():
            db_ref[...] = jnp.zeros_like(db_ref)

        @pl.when(s == 0)
        def _():
            carry_ref[...] = jnp.zeros_like(carry_ref)

        dlogf = _dot_hi(_tri(ts, True), dc_ref[...]) + carry_ref[0:1, :]
        carry_ref[0:1, :] = dlogf[0:1, :]
        lane = lax.broadcasted_iota(jnp.int32, (1, LANES), 1)
        dfl = jnp.where(lane < n_heads, dlogf * (1.0 - _sigmoid(fl_ref[...] + b_ref[...])), 0.0)
        dfl_ref[...] = dfl.astype(BF16)
        db_ref[...] += _fold8(dfl)

    return pl.pallas_call(
        body, out_shape=(SDS((bsz, s_len, LANES), BF16), SDS((8, LANES), F32)), grid=(bsz, ns), name="fox_gate_bwd",
        in_specs=[pl.BlockSpec((None, ts, LANES), lambda bi, s: (bi, ns - 1 - s, 0)),
                  pl.BlockSpec((None, ts, LANES), lambda bi, s: (bi, ns - 1 - s, col)),
                  pl.BlockSpec((1, LANES), lambda bi, s: (0, 0))],
        out_specs=(pl.BlockSpec((None, ts, LANES), lambda bi, s: (bi, ns - 1 - s, 0)),
                   pl.BlockSpec((8, LANES), lambda bi, s: (0, 0))),
        scratch_shapes=[pltpu.VMEM((8, LANES), F32)],
        compiler_params=_cp("arbitrary", "arbitrary"))(dc, proj, bf)


def _head_maps(d):
    ch = lax.broadcasted_iota(jnp.int32, (d, LANES), 0) // HEAD_DIM
    hd = lax.broadcasted_iota(jnp.int32, (d, LANES), 1)
    e = jnp.where(ch == hd, 1.0, 0.0).astype(BF16)
    cht = lax.broadcasted_iota(jnp.int32, (LANES, d), 1) // HEAD_DIM
    hdt = lax.broadcasted_iota(jnp.int32, (LANES, d), 0)
    et = jnp.where(cht == hdt, 1.0, 0.0).astype(BF16)
    return e, et


def _dot_sel(x, e):
    a = x.astype(BF16)
    r = x - a.astype(F32)
    b = r.astype(BF16)
    c = (r - b.astype(F32)).astype(BF16)
    return (jnp.dot(a, e, preferred_element_type=F32) + jnp.dot(b, e, preferred_element_type=F32)
            + jnp.dot(c, e, preferred_element_type=F32))


def _fox_qknorm_fwd(proj, gq, gk, d):
    t = proj.shape[0]
    ts = _tile(t, 256)
    scale = 1.0 / math.sqrt(HEAD_DIM)

    def body(q_ref, k_ref, v_ref, gq_ref, gk_ref, qn_ref, kn_ref, vb_ref):
        e, et = _head_maps(d)

        def norm(v, g):
            r = lax.rsqrt(_dot_sel(v * v, e) / HEAD_DIM + EPS)
            return v * _dot_sel(r, et) * g

        qn_ref[...] = (norm(q_ref[...], gq_ref[...]) * scale).astype(BF16)
        kn_ref[...] = norm(k_ref[...], gk_ref[...]).astype(BF16)
        vb_ref[...] = v_ref[...].astype(BF16)

    def colblk(j):
        return pl.BlockSpec((ts, d), lambda i: (i, j))

    vec = pl.BlockSpec((1, d), lambda i: (0, 0))
    out = pl.BlockSpec((ts, d), lambda i: (i, 0))
    return pl.pallas_call(
        body, out_shape=(SDS((t, d), BF16),) * 3, grid=(t // ts,), name="fox_qknorm_fwd",
        in_specs=[colblk(0), colblk(1), colblk(2), vec, vec], out_specs=(out, out, out),
        compiler_params=_cp("parallel"))(proj, proj, proj, gq, gk)


def _fox_qknorm_bwd(proj, dq, dk, dv, gq, gk, d):
    t = proj.shape[0]
    ts = _tile(t, 256)
    scale = 1.0 / math.sqrt(HEAD_DIM)

    def body(q_ref, k_ref, dq_ref, dk_ref, dv_ref, gq_ref, gk_ref, dp_ref, dgq_ref, dgk_ref):
        @pl.when(pl.program_id(0) == 0)
        def _():
            dgq_ref[...] = jnp.zeros_like(dgq_ref)
            dgk_ref[...] = jnp.zeros_like(dgk_ref)

        e, et = _head_maps(d)

        def back(v, g, dn, dg_ref):
            r = _dot_sel(lax.rsqrt(_dot_sel(v * v, e) / HEAD_DIM + EPS), et)
            vh = v * r
            dg_ref[...] += _fold8(dn * vh)
            u = dn * g
            mh = _dot_sel(_dot_sel(u * vh, e) / HEAD_DIM, et)
            return r * (u - vh * mh)

        dp_ref[:, 0:d] = back(q_ref[...], gq_ref[...], dq_ref[...] * scale, dgq_ref).astype(BF16)
        dp_ref[:, d:2 * d] = back(k_ref[...], gk_ref[...], dk_ref[...], dgk_ref).astype(BF16)
        dp_ref[:, 2 * d:3 * d] = dv_ref[...]

    def colblk(j):
        return pl.BlockSpec((ts, d), lambda i: (i, j))

    row = pl.BlockSpec((ts, d), lambda i: (i, 0))
    vec = pl.BlockSpec((1, d), lambda i: (0, 0))
    acc = pl.BlockSpec((8, d), lambda i: (0, 0))
    return pl.pallas_call(
        body, out_shape=(SDS((t, 3 * d), BF16), SDS((8, d), F32), SDS((8, d), F32)), grid=(t // ts,),
        name="fox_qknorm_bwd", in_specs=[colblk(0), colblk(1), row, row, row, vec, vec],
        out_specs=(pl.BlockSpec((ts, 3 * d), lambda i: (i, 0)), acc, acc),
        compiler_params=_cp("arbitrary"))(proj, proj, dq, dk, dv, gq, gk)


ATT_BLOCK = 512
_NT = (((1,), (1,)), ((), ()))
_TN = (((0,), (0,)), ((), ()))


def _head_mask(h):
    return (lax.broadcasted_iota(jnp.int32, (1, LANES), 1) // HEAD_DIM) == h


def _causal(qi, ki, tq, tk):
    row = qi * tq + lax.broadcasted_iota(jnp.int32, (tq, 1), 0)
    col = ki * tk + lax.broadcasted_iota(jnp.int32, (1, tk), 1)
    return col <= row


def _direct_exchange(ins, outs, place, sems, gather):
    send_sems, recv_sems, local_sems = sems
    x, y, c, me = _mesh_place()
    copies = []
    for t in range(len(ins)):
        dst = outs[t].at[me] if gather else outs[place[t][0]].at[me, place[t][1]]
        copies.append(pltpu.make_async_copy(ins[t] if gather else ins[t].at[me], dst, local_sems.at[t]))
        for kbits in range(1, N_DEV):
            px = 1 - x if kbits & 4 else x
            py = 1 - y if kbits & 2 else y
            pc = 1 - c if kbits & 1 else c
            copies.append(pltpu.make_async_remote_copy(
                src_ref=ins[t] if gather else ins[t].at[4 * px + 2 * py + pc], dst_ref=dst,
                send_sem=send_sems.at[t, kbits - 1], recv_sem=recv_sems.at[t, kbits - 1],
                device_id=(px, py, pc), device_id_type=pl.DeviceIdType.MESH))
    return copies


def _exchange_scratch(n):
    return [pltpu.SemaphoreType.DMA((n, N_DEV - 1)), pltpu.SemaphoreType.DMA((n, N_DEV - 1)),
            pltpu.SemaphoreType.DMA((n,))]


def _flash_fwd(q, k, v, crow, gather=()):
    bsz, s_len, d = q.shape
    nj = d // LANES
    tq = tk = _tile(s_len, ATT_BLOCK, LANES)
    nq = s_len // tq
    ng = len(gather)

    pairs = [(a, b) for a in range(nq) for b in range(a + 1)]
    qtab = jnp.asarray([a for a, _ in pairs], jnp.int32)
    ktab = jnp.asarray([b for _, b in pairs], jnp.int32)

    def body(qtab_ref, ktab_ref, q_ref, k_ref, v_ref, c_ref, *rest):
        g_in, (o_ref, lse_ref), g_out = rest[:ng], rest[ng:ng + 2], rest[ng + 2:2 * ng + 2]
        m_ref, l_ref, acc_ref = rest[2 * ng + 2:2 * ng + 5]
        step_id = pl.program_id(2)
        qi, ki = qtab_ref[step_id], ktab_ref[step_id]
        if ng:
            sems = rest[2 * ng + 5:]
            outer = (pl.program_id(0), pl.program_id(1))

            @pl.when((outer[0] == 0) & (outer[1] == 0) & (step_id == 0))
            def _():
                for cp in _direct_exchange(g_in, g_out, None, sems, True):
                    cp.start()

            @pl.when((outer[0] == bsz - 1) & (outer[1] == nj - 1) & (step_id == len(pairs) - 1))
            def _():
                for cp in _direct_exchange(g_in, g_out, None, sems, True):
                    cp.wait()

        @pl.when(ki == 0)
        def _():
            m_ref[...] = jnp.full_like(m_ref, NEG)
            l_ref[...] = jnp.zeros_like(l_ref)
            acc_ref[...] = jnp.zeros_like(acc_ref)

        def step(masked):
            qv, kv, vv = q_ref[...], k_ref[...], v_ref[...]
            for h in range(2):
                qh = jnp.where(_head_mask(h), qv, jnp.zeros_like(qv))
                s = lax.dot_general(qh, kv, _NT, preferred_element_type=F32) - c_ref[h:h + 1, :]
                if masked:
                    s = jnp.where(_causal(qi, ki, tq, tk), s, NEG)
                m_prev = m_ref[h]
                m_new = jnp.maximum(m_prev, jnp.max(s, axis=1, keepdims=True))
                pm = jnp.exp(s - m_new)
                alpha = jnp.exp(m_prev - m_new)
                l_ref[h] = alpha * l_ref[h] + jnp.sum(pm, axis=1, keepdims=True)
                p_hi = pm.astype(BF16)
                p_lo = (pm - p_hi.astype(F32)).astype(BF16)
                acc_ref[h] = (alpha * acc_ref[h] + jnp.dot(p_hi, vv, preferred_element_type=F32)
                              + jnp.dot(p_lo, vv, preferred_element_type=F32))
                m_ref[h] = m_new

        pl.when(ki < qi)(functools.partial(step, False))
        pl.when(ki == qi)(functools.partial(step, True))

        @pl.when(ki == qi)
        def _():
            m0 = _head_mask(0)
            o_ref[...] = jnp.where(m0, acc_ref[0] / l_ref[0], acc_ref[1] / l_ref[1])
            lse_ref[...] = jnp.where(m0, m_ref[0] + jnp.log(l_ref[0]), m_ref[1] + jnp.log(l_ref[1]))

    qblk = pl.BlockSpec((None, tq, LANES), lambda bi, j, t, qt, kt: (bi, qt[t], j))
    kblk = pl.BlockSpec((None, tk, LANES), lambda bi, j, t, qt, kt: (bi, kt[t], j))
    hbm = pl.BlockSpec(memory_space=pl.ANY)
    outs = pl.pallas_call(
        body, out_shape=[SDS((bsz, s_len, d), F32), SDS((bsz, nj, s_len, LANES), F32)]
        + [SDS((N_DEV,) + tuple(a.shape), a.dtype) for a in gather], name="flash_fwd",
        grid_spec=pltpu.PrefetchScalarGridSpec(
            num_scalar_prefetch=2, grid=(bsz, nj, len(pairs)),
            in_specs=[qblk, kblk, kblk,
                      pl.BlockSpec((None, None, 2, tk), lambda bi, j, t, qt, kt: (bi, j, 0, kt[t]))] + [hbm] * ng,
            out_specs=[qblk, pl.BlockSpec((None, None, tq, LANES), lambda bi, j, t, qt, kt: (bi, j, qt[t], 0))]
            + [hbm] * ng,
            scratch_shapes=[pltpu.VMEM((2, tq, 1), F32), pltpu.VMEM((2, tq, 1), F32),
                            pltpu.VMEM((2, tq, LANES), F32)] + (_exchange_scratch(ng) if ng else [])),
        compiler_params=_cp("arbitrary", "arbitrary", "arbitrary"))(qtab, ktab, q, k, v, crow, *gather)
    return outs[0], outs[1], list(outs[2:])


def _flash_probs(qv, kv, vv, dov, ov, lse, c_ref, h, mask):
    hm = _head_mask(h)
    qh = jnp.where(hm, qv, jnp.zeros_like(qv))
    s = lax.dot_general(qh, kv, _NT, preferred_element_type=F32) - c_ref[h:h + 1, :]
    pm = jnp.exp(s - lse[:, h * HEAD_DIM:h * HEAD_DIM + 1])
    if mask is not None:
        pm = jnp.where(mask, pm, 0.0)
    doh = jnp.where(hm, dov, jnp.zeros_like(dov))
    dpm = lax.dot_general(doh, vv, _NT, preferred_element_type=F32)
    delta = jnp.sum(jnp.where(hm, dov.astype(F32) * ov, 0.0), axis=1, keepdims=True)
    return pm, pm * (dpm - delta)


def _flash_bwd(q, k, v, do, o, lse, crow, items=(), groups=()):
    bsz, s_len, d = q.shape
    nj = d // LANES
    tq = tk = _tile(s_len, ATT_BLOCK, LANES)
    nq = s_len // tq

    pairs = [(b, a) for b in range(nq) for a in range(b, nq)]
    n_live = len(pairs)
    ktab = jnp.asarray([b for b, _ in pairs] + [nq - 1] * nq, jnp.int32)
    qtab = jnp.asarray([a for _, a in pairs] + list(range(nq)), jnp.int32)

    n_it, n_grp = len(items), len(groups)
    place = {it: (g, l) for g, members in enumerate(groups) for l, it in enumerate(members)}

    def body(ktab_ref, qtab_ref, q_ref, k_ref, v_ref, do_ref, o_ref, lse_ref, c_ref, *rest):
        x_in, (dq_ref, dk_ref, dv_ref, dc_ref) = rest[:n_it], rest[n_it:n_it + 4]
        x_out = rest[n_it + 4:n_it + 4 + n_grp]
        dqa_ref, dka_ref, dva_ref, dca_ref = rest[n_it + 4 + n_grp:n_it + 8 + n_grp]
        step_id = pl.program_id(2)
        ki, qi = ktab_ref[step_id], qtab_ref[step_id]
        live = step_id < n_live
        rows = pl.ds(pl.multiple_of(qi * tq, tq), tq)
        if n_it:
            sems = rest[n_it + 8 + n_grp:]
            outer = (pl.program_id(0), pl.program_id(1))

            @pl.when((outer[0] == 0) & (outer[1] == 0) & (step_id == 0))
            def _():
                for cp in _direct_exchange(x_in, x_out, place, sems, False):
                    cp.start()

            @pl.when((outer[0] == bsz - 1) & (outer[1] == nj - 1) & (step_id == n_live + nq - 1))
            def _():
                for cp in _direct_exchange(x_in, x_out, place, sems, False):
                    cp.wait()

        @pl.when(step_id == 0)
        def _():
            dqa_ref[...] = jnp.zeros_like(dqa_ref)

        @pl.when(live & (qi == ki))
        def _():
            dka_ref[...] = jnp.zeros_like(dka_ref)
            dva_ref[...] = jnp.zeros_like(dva_ref)
            dca_ref[...] = jnp.zeros_like(dca_ref)

        def step(masked):
            qv, kv, vv, dov, ov, lse = q_ref[...], k_ref[...], v_ref[...], do_ref[...], o_ref[...], lse_ref[...]
            mask = _causal(qi, ki, tq, tk) if masked else None
            for h in range(2):
                pm, ds = _flash_probs(qv, kv, vv, dov, ov, lse, c_ref, h, mask)
                dsb = ds.astype(BF16)
                dva_ref[h] += lax.dot_general(pm.astype(BF16), dov, _TN, preferred_element_type=F32)
                dka_ref[h] += lax.dot_general(dsb, qv, _TN, preferred_element_type=F32)
                dqa_ref[h, rows, :] += jnp.dot(dsb, kv, preferred_element_type=F32)
                dca_ref[h:h + 1, :] -= jnp.sum(ds, axis=0, keepdims=True)

        pl.when(live & (qi > ki))(functools.partial(step, False))
        pl.when(live & (qi == ki))(functools.partial(step, True))

        @pl.when(live & (qi == nq - 1))
        def _():
            m0 = _head_mask(0)
            dk_ref[...] = jnp.where(m0, dka_ref[0], dka_ref[1])
            dv_ref[...] = jnp.where(m0, dva_ref[0], dva_ref[1]).astype(BF16)
            dc_ref[...] = dca_ref[0:2, :]

        @pl.when(jnp.logical_not(live))
        def _():
            dq_ref[...] = jnp.where(_head_mask(0), dqa_ref[0, rows, :], dqa_ref[1, rows, :])

    def qside(bi, j, t, kt, qt):
        return (bi, jnp.where(t < n_live, qt[t], nq - 1), j)

    def kside(bi, j, t, kt, qt):
        return (bi, kt[t], j)

    def dqside(bi, j, t, kt, qt):
        return (bi, jnp.where(t < n_live, 0, qt[t]), j)

    qblk, kblk = pl.BlockSpec((None, tq, LANES), qside), pl.BlockSpec((None, tk, LANES), kside)
    cblk = pl.BlockSpec((None, None, 2, tk), lambda bi, j, t, kt, qt: (bi, j, 0, kt[t]))
    hbm = pl.BlockSpec(memory_space=pl.ANY)
    outs = pl.pallas_call(
        body, out_shape=[SDS((bsz, s_len, d), F32), SDS((bsz, s_len, d), F32), SDS((bsz, s_len, d), BF16),
                         SDS((bsz, nj, 2, s_len), F32)]
        + [SDS((N_DEV, len(members)) + tuple(items[members[0]].shape[1:]), items[members[0]].dtype)
           for members in groups], name="flash_bwd",
        grid_spec=pltpu.PrefetchScalarGridSpec(
            num_scalar_prefetch=2, grid=(bsz, nj, n_live + nq),
            in_specs=[qblk, kblk, kblk, qblk, qblk,
                      pl.BlockSpec((None, None, tq, LANES),
                                   lambda bi, j, t, kt, qt: (bi, j, jnp.where(t < n_live, qt[t], nq - 1), 0)),
                      cblk] + [hbm] * n_it,
            out_specs=[pl.BlockSpec((None, tq, LANES), dqside), kblk, kblk, cblk] + [hbm] * n_grp,
            scratch_shapes=[pltpu.VMEM((2, s_len, LANES), F32), pltpu.VMEM((2, tk, LANES), F32),
                            pltpu.VMEM((2, tk, LANES), F32), pltpu.VMEM((8, tk), F32)]
            + (_exchange_scratch(n_it) if n_it else [])),
        compiler_params=_cp("arbitrary", "arbitrary", "arbitrary"))(ktab, qtab, q, k, v, do, o, lse, crow, *items)
    return outs[0], outs[1], outs[2], outs[3], list(outs[4:])


def _loss_head(y, target):
    t, d = y.shape
    tm = _tile(t, 512)

    def body(y_ref, t_ref, dy_ref, acc_ref):
        @pl.when(pl.program_id(0) == 0)
        def _():
            acc_ref[...] = jnp.zeros_like(acc_ref)

        err = y_ref[...] - t_ref[...]
        dy_ref[...] = err / d
        acc_ref[...] += _fold8(err * err)

    row = pl.BlockSpec((tm, d), lambda i: (i, 0))
    return pl.pallas_call(
        body, out_shape=(SDS((t, d), F32), SDS((8, d), F32)), grid=(t // tm,), name="loss_head",
        in_specs=[row, row], out_specs=(row, pl.BlockSpec((8, d), lambda i: (0, 0))),
        compiler_params=_cp("arbitrary"))(y, target)


ADAM_COLS = 1024


def _adamw(g8, w, m, v):
    shape = w.shape
    cols = shape[-1]
    rows = w.size // cols
    n_parts = g8.shape[0]
    g8, w, m, v = g8.reshape(n_parts, rows, cols), w.reshape(rows, cols), m.reshape(rows, cols), v.reshape(rows, cols)
    tr = _tile(rows, 256, 16)
    c1 = 1.0 - ADAM_B1 ** ADAM_STEP
    c2 = 1.0 - ADAM_B2 ** ADAM_STEP

    def body(g8_ref, w_ref, m_ref, v_ref, g_ref, d_ref, nm_ref, nv_ref):
        g = g8_ref[0].astype(F32)
        for i in range(1, n_parts):
            g = g + g8_ref[i].astype(F32)
        mn = ADAM_B1 * m_ref[...] + (1.0 - ADAM_B1) * g
        vn = ADAM_B2 * v_ref[...] + (1.0 - ADAM_B2) * (g * g)
        g_ref[...] = g
        nm_ref[...] = mn
        nv_ref[...] = vn
        d_ref[...] = -ADAM_LR * ((mn / c1) / (jnp.sqrt(vn / c2) + ADAM_EPS) + ADAM_WD * w_ref[...])

    blk = pl.BlockSpec((tr, cols), lambda i: (i, 0))
    outs = pl.pallas_call(
        body, out_shape=(SDS((rows, cols), F32),) * 4, grid=(rows // tr,), name="adamw",
        in_specs=[pl.BlockSpec((n_parts, tr, cols), lambda i: (0, i, 0)), blk, blk, blk], out_specs=(blk,) * 4,
        compiler_params=_cp("parallel"))(g8, w, m, v)
    return [o.reshape(shape) for o in outs]


def _mesh_place():
    x, y, c = lax.axis_index("x"), lax.axis_index("y"), lax.axis_index("c")
    return x, y, c, 4 * x + 2 * y + c


def _gather(shards):
    n = len(shards)

    def body(*refs):
        ins, outs = refs[:n], refs[n:2 * n]
        send_sems, recv_sems, local_sems = refs[2 * n:]
        x, y, c, me = _mesh_place()
        sibling = (x, y, 1 - c)
        chips = [(1 - x, y), (x, 1 - y), (1 - x, 1 - y)]

        def block(px, py, pc):
            return 4 * px + 2 * py + pc

        def copy(t, k, blk, to, src=None):
            return pltpu.make_async_remote_copy(
                src_ref=outs[t].at[blk] if src is None else src, dst_ref=outs[t].at[blk],
                send_sem=send_sems.at[t, k], recv_sem=recv_sems.at[t, k], device_id=to,
                device_id_type=pl.DeviceIdType.MESH)

        own = [pltpu.make_async_copy(ins[t], outs[t].at[me], local_sems.at[t]) for t in range(n)]
        first = []
        for t in range(n):
            own[t].start()
            first.append(copy(t, 0, me, sibling, src=ins[t]))
            first += [copy(t, 1 + j, me, (*chip, c), src=ins[t]) for j, chip in enumerate(chips)]
        for cp in first:
            cp.start()
        passed = []
        for j, chip in enumerate(chips):
            for t in range(n):
                copy(t, 1 + j, block(*chip, c), (x, y, c)).wait_recv()
                cp = copy(t, 4 + j, block(*chip, c), sibling)
                cp.start()
                passed.append(cp)
        for t in range(n):
            copy(t, 0, block(x, y, 1 - c), (x, y, c)).wait_recv()
            for j, chip in enumerate(chips):
                copy(t, 4 + j, block(*chip, 1 - c), (x, y, c)).wait_recv()
        for cp in first + passed:
            cp.wait_send()
        for cp in own:
            cp.wait()

    hbm = pl.BlockSpec(memory_space=pl.ANY)
    return pl.pallas_call(
        body, out_shape=[SDS((N_DEV,) + tuple(s.shape), s.dtype) for s in shards], name="gather",
        in_specs=[hbm] * n, out_specs=[hbm] * n,
        scratch_shapes=[pltpu.SemaphoreType.DMA((n, N_DEV - 1)), pltpu.SemaphoreType.DMA((n, N_DEV - 1)),
                        pltpu.SemaphoreType.DMA((n,))])(*shards)


N_CHIP = N_DEV // 2


def _scatter_core(items):
    n = len(items)

    def body(*refs):
        ins, outs = refs[:n], refs[n:2 * n]
        send_sems, recv_sems = refs[2 * n:]
        x, y, c, _ = _mesh_place()
        copies = []
        for it in range(n):
            for r in range(N_CHIP):
                cp = pltpu.make_async_remote_copy(
                    src_ref=ins[it].at[2 * r + 1 - c], dst_ref=outs[it].at[r], send_sem=send_sems.at[it, r],
                    recv_sem=recv_sems.at[it, r], device_id=(x, y, 1 - c), device_id_type=pl.DeviceIdType.MESH)
                cp.start()
                copies.append(cp)
        for cp in copies:
            cp.wait()

    hbm = pl.BlockSpec(memory_space=pl.ANY)
    return pl.pallas_call(
        body, out_shape=[SDS((N_CHIP,) + tuple(a.shape[1:]), a.dtype) for a in items], name="scatter_core",
        in_specs=[hbm] * n, out_specs=[hbm] * n,
        scratch_shapes=[pltpu.SemaphoreType.DMA((n, N_CHIP)), pltpu.SemaphoreType.DMA((n, N_CHIP))])(*items)


def _pair_add(item, other):
    shape = item.shape[1:]
    cols = shape[-1]
    rows = math.prod(shape) // cols
    tr = _tile(rows, 512, 16)

    def body(x_ref, o_ref, h_ref):
        c = lax.axis_index("c")
        mine = jnp.where(c == 0, x_ref[0].astype(F32), x_ref[1].astype(F32))
        h_ref[...] = (mine + o_ref[...].astype(F32)).astype(item.dtype)

    return pl.pallas_call(
        body, out_shape=SDS((N_CHIP, rows, cols), item.dtype), grid=(N_CHIP, rows // tr), name="pair_add",
        in_specs=[pl.BlockSpec((None, 2, tr, cols), lambda r, i: (r, 0, i, 0)),
                  pl.BlockSpec((None, tr, cols), lambda r, i: (r, i, 0))],
        out_specs=pl.BlockSpec((None, tr, cols), lambda r, i: (r, i, 0)),
        compiler_params=_cp("parallel", "parallel"))(
            item.reshape(N_CHIP, 2, rows, cols), other.reshape(N_CHIP, rows, cols)).reshape((N_CHIP,) + shape)


def _scatter_chip(items, groups):
    n = len(items)
    place = {it: (g, l) for g, members in enumerate(groups) for l, it in enumerate(members)}

    def body(*refs):
        ins, outs = refs[:n], refs[n:n + len(groups)]
        send_sems, recv_sems, local_sems = refs[n + len(groups):]
        x, y, c, _ = _mesh_place()
        chip = 2 * x + y
        copies = []
        for it in range(n):
            g, l = place[it]
            own = pltpu.make_async_copy(ins[it].at[chip], outs[g].at[chip, l], local_sems.at[it])
            own.start()
            copies.append(own)
            for kbits in range(1, N_CHIP):
                px = 1 - x if kbits & 2 else x
                py = 1 - y if kbits & 1 else y
                cp = pltpu.make_async_remote_copy(
                    src_ref=ins[it].at[2 * px + py], dst_ref=outs[g].at[chip, l],
                    send_sem=send_sems.at[it, kbits - 1], recv_sem=recv_sems.at[it, kbits - 1],
                    device_id=(px, py, c), device_id_type=pl.DeviceIdType.MESH)
                cp.start()
                copies.append(cp)
        for cp in copies:
            cp.wait()

    hbm = pl.BlockSpec(memory_space=pl.ANY)
    out_shape = [SDS((N_CHIP, len(members)) + tuple(items[members[0]].shape[1:]), items[members[0]].dtype)
                 for members in groups]
    return pl.pallas_call(
        body, out_shape=out_shape, name="scatter_chip", in_specs=[hbm] * n, out_specs=[hbm] * len(groups),
        scratch_shapes=[pltpu.SemaphoreType.DMA((n, N_CHIP - 1)), pltpu.SemaphoreType.DMA((n, N_CHIP - 1)),
                        pltpu.SemaphoreType.DMA((n,))])(*items)


def _scatter(items, groups):
    halves = _scatter_core(items)
    return _scatter_chip([_pair_add(a, h) for a, h in zip(items, halves)], groups)


def _cat_lanes(g, layer, nb, blk, width):
    _, _, rows, c = g.shape
    tr = _tile(rows, 256, 16)

    def body(g_ref, o_ref):
        for p in range(nb):
            o_ref[:, p * c:(p + 1) * c] = g_ref[p]
        if width > nb * c:
            o_ref[:, nb * c:] = jnp.zeros((tr, width - nb * c), g.dtype)

    return pl.pallas_call(
        body, out_shape=SDS((rows, width), g.dtype), grid=(rows // tr,), name="cat_lanes",
        in_specs=[pl.BlockSpec((nb, None, tr, c), lambda i: (blk, layer, i, 0))],
        out_specs=pl.BlockSpec((tr, width), lambda i: (i, 0)),
        compiler_params=_cp("parallel"))(g)


def _split_lanes(parts, c):
    rows = parts[0].shape[0]
    counts = [p.shape[1] // c for p in parts]
    tr = _tile(rows, 256, 16)

    def body(*refs):
        o_ref = refs[-1]
        q = 0
        for x_ref, cnt in zip(refs[:-1], counts):
            for p in range(cnt):
                o_ref[q] = x_ref[:, p * c:(p + 1) * c]
                q += 1

    return pl.pallas_call(
        body, out_shape=SDS((sum(counts), rows, c), parts[0].dtype), grid=(rows // tr,), name="split_lanes",
        in_specs=[pl.BlockSpec((tr, p.shape[1]), lambda i: (i, 0)) for p in parts],
        out_specs=pl.BlockSpec((sum(counts), tr, c), lambda i: (0, i, 0)),
        compiler_params=_cp("parallel"))(*parts)


def _unshard(g8, shard_shape, axis):
    full = jnp.moveaxis(g8.reshape((N_DEV,) + tuple(shard_shape)), 0, axis)
    shape = list(shard_shape)
    shape[axis] *= N_DEV
    return full.reshape(shape)


def _to_shards(full, axis):
    shape = list(full.shape)
    shape[axis:axis + 1] = [N_DEV, shape[axis] // N_DEV]
    return jnp.moveaxis(full.reshape(shape), axis, 0).reshape(N_DEV, -1)


SMALL = [n for n in SHARDED if n not in MATRICES]


def _flat_rows(parts):
    flat = jnp.concatenate([p.reshape(-1) for p in parts])
    chunk = 8 * ADAM_COLS
    n = -(-flat.shape[0] // chunk) * chunk
    return jnp.pad(flat, (0, n - flat.shape[0])).reshape(n // ADAM_COLS, ADAM_COLS)


def _prepare_vectors(small, shards):
    wt = {}
    flat = small.reshape(N_DEV, -1)
    off = 0
    for n in SMALL:
        size = shards[n].size
        wt[n] = _unshard(flat[:, off:off + size], shards[n].shape, SHARD_AXIS[n])
        off += size
    return wt


def _prepare_matrices(gathered):
    wt = {}
    for n in ('conv_w_out', 'fox_w_o', 'ffn_w_down'):
        if n in gathered:
            g = gathered[n]
            wt[n] = [g[:, l].reshape(N_DEV * g.shape[2], g.shape[3]) for l in range(g.shape[1])]
    if 'pool_w' in gathered:
        g = gathered['pool_w']
        wt['pool_w'] = [jnp.moveaxis(g[:, l], 0, 1).reshape(g.shape[2], N_DEV * g.shape[3], g.shape[4])
                        for l in range(g.shape[1])]
    if 'conv_w_in' in gathered:
        g = gathered['conv_w_in']
        wt['conv_w_in'] = [_cat_lanes(g, l, N_DEV, 0, N_DEV * g.shape[3]) for l in range(g.shape[1])]
    if 'fox_w_in' in gathered:
        g = gathered['fox_w_in']
        wt['fox_w_in'] = [_cat_lanes(g, l, N_DEV, 0, 3 * g.shape[2] + LANES) for l in range(g.shape[1])]
    if 'ffn_w_up' in gathered:
        g = gathered['ffn_w_up']
        half = N_DEV // 2
        wt['ffn_w_up_v'] = [_cat_lanes(g, l, half, 0, half * g.shape[3]) for l in range(g.shape[1])]
        wt['ffn_w_up_g'] = [_cat_lanes(g, l, half, 1, half * g.shape[3]) for l in range(g.shape[1])]
    return wt


def _pad_rows(w, rows):
    return jnp.pad(w, ((0, rows - w.shape[0]), (0, 0)))


def _fold(acc):
    return acc.sum(axis=0)


def _local_step(x, target, wt, late_shards=None, cut=None):
    wt = dict(wt)
    late_names = [n for n in MATRICES if late_shards and n in late_shards]
    late_recv = {}
    bsz, s_len, d = x.shape
    t = bsz * s_len
    depth = wt['norm_mix'].shape[0]
    n_heads = d // HEAD_DIM
    f = wt['ffn_w_up_v'][0].shape[1]
    row = lambda a: a.reshape(1, -1)
    grads = {n: {} for n in WEIGHTS}
    saved = []

    xc = x.reshape(t, d)
    hn_next = None
    for i in range(depth):
        j = i // 3
        kind = i % 3
        sv = {'x_mix': xc}
        gm = row(wt['norm_mix'][i])
        gf = row(wt['norm_ffn'][i])
        if kind == 0:
            hn = hn_next if hn_next is not None else _rmsnorm_fwd(xc, gm)
            p = _mm(hn, wt['conv_w_in'][j], bias=row(wt['conv_b_in'][j]), name="conv_in")
            u, sact = _conv_act_fwd(p.reshape(bsz, s_len, 2 * d), _pad_rows(wt['conv_dw'][j], 32),
                                    row(wt['conv_dw_b'][j]), row(wt['conv_ln_g'][j]), row(wt['conv_ln_b'][j]))
            sact = sact.reshape(t, d)
            xn, hf = _mm(sact, wt['conv_w_out'][j], bias=row(wt['conv_b_out'][j]), residual=xc, norm_gain=gf,
                         out_dtype=F32, name="conv_out")
            sv.update(hn=hn, p=p, u=u.reshape(t, d), sact=sact)
        elif kind == 1:
            xn, pp = _pool_fwd(xc.reshape(bsz, s_len, d), gm, wt['pool_w'][j], row(wt['pool_b'][j]),
                               row(wt['pool_scale'][j]))
            xn = xn.reshape(t, d)
            hf = _rmsnorm_fwd(xn, gf)
            sv.update(p=pp)
        else:
            hn = hn_next if hn_next is not None else _rmsnorm_fwd(xc, gm)
            wp = wt['fox_w_in'][j]
            bf = jnp.pad(wt['fox_b_f'][j], (0, LANES - n_heads)).reshape(1, LANES)
            gq = jnp.tile(wt['fox_q_gain'][j], n_heads).reshape(1, d)
            gk = jnp.tile(wt['fox_k_gain'][j], n_heads).reshape(1, d)
            proj = _mm(hn, wp, out_dtype=F32, name="fox_in")
            c = _fox_gate_fwd(proj.reshape(bsz, s_len, -1), bf, n_heads)
            crow = jnp.swapaxes(c, 1, 2)[:, :n_heads].reshape(bsz, n_heads // 2, 2, s_len)
            qn, kn, vb = _fox_qknorm_fwd(proj, gq, gk, d)
            shp = (bsz, s_len, d)
            o, lse, got = _flash_fwd(qn.reshape(shp), kn.reshape(shp), vb.reshape(shp), crow,
                                     gather=[late_shards[n] for n in late_names])
            for key, layers in _prepare_matrices(dict(zip(late_names, got))).items():
                wt[key] = wt[key] + layers
            o = o.reshape(t, d)
            xn, hf = _mm(o, wt['fox_w_o'][j], residual=xc, norm_gain=gf, out_dtype=F32, name="fox_out")
            sv.update(hn=hn, wp=wp, bf=bf, gq=gq, gk=gk, proj=proj, crow=crow, qn=qn, kn=kn, vb=vb, o=o, lse=lse)
        xc = xn
        sv['x_ffn'] = xc
        shf = (bsz, s_len, f)
        uv = _mm(hf, wt['ffn_w_up_v'][i], name="ffn_up").reshape(shf)
        ug = _mm(hf, wt['ffn_w_up_g'][i], name="ffn_up").reshape(shf)
        dw8 = _pad_rows(wt['ffn_dw'][i], 8)
        af, vv, vg = _ffn_act_fwd(uv, ug, dw8, row(wt['ffn_dw_b'][i]))
        af = af.reshape(t, f)
        if i + 1 < depth and (i + 1) % 3 != 1:
            xc, hn_next = _mm(af, wt['ffn_w_down'][i], residual=xc, norm_gain=row(wt['norm_mix'][i + 1]),
                              out_dtype=F32, name="ffn_down")
        else:
            xc, hn_next = _mm(af, wt['ffn_w_down'][i], residual=xc, out_dtype=F32, name="ffn_down"), None
        sv.update(hf=hf, uv=uv, ug=ug, vv=vv, vg=vg, af=af, dw8=dw8)
        saved.append(sv)

    dx, sq = _loss_head(xc, target.reshape(t, d))

    for i in reversed(range(depth)):
        j = i // 3
        kind = i % 3
        sv = saved[i]
        shf = (bsz, s_len, f)
        da = _mm(dx, wt['ffn_w_down'][i], trans_b=True, name="ffn_down_dgrad")
        gw, _ = _wgrad(sv['af'], dx, name="ffn_down_wgrad")
        grads['ffn_w_down'][i] = gw.reshape(N_DEV, f // N_DEV, d)
        dvv, dvg, dbv, dbg = _ffn_act_bwd1(sv['vv'], sv['vg'], da.reshape(shf))
        grads['ffn_dw_b'][i] = jnp.concatenate([_fold(dbv), _fold(dbg)])
        duv, dug, ddwv, ddwg = _ffn_act_bwd2(dvv, dvg, sv['uv'], sv['ug'], sv['dw8'])
        grads['ffn_dw'][i] = jnp.concatenate([ddwv.sum(axis=1), ddwg.sum(axis=1)], axis=1)
        duv, dug = duv.reshape(t, f), dug.reshape(t, f)
        gv, _ = _wgrad(sv['hf'], duv, name="ffn_up_wgrad")
        gg, _ = _wgrad(sv['hf'], dug, name="ffn_up_wgrad")
        grads['ffn_w_up'][i] = _split_lanes([gv, gg], 2 * f // N_DEV)
        dx, dg = _mm(duv, wt['ffn_w_up_v'][i], trans_b=True, a2=dug, b2=wt['ffn_w_up_g'][i],
                     norm_bwd=(sv['x_ffn'], row(wt['norm_ffn'][i]), dx), name="ffn_up_dgrad")
        grads['norm_ffn'][i] = _fold(dg)
        gm = row(wt['norm_mix'][i])
        if kind == 0:
            dsact = _mm(dx, wt['conv_w_out'][j], trans_b=True, name="conv_out_dgrad")
            gw, cs = _wgrad(sv['sact'], dx, name="conv_out_wgrad")
            grads['conv_w_out'][j] = gw.reshape(N_DEV, d // N_DEV, d)
            grads['conv_b_out'][j] = _fold(cs)
            du, dlg, dlb, dwb = _conv_act_bwd1(sv['u'], dsact, row(wt['conv_ln_g'][j]), row(wt['conv_ln_b'][j]))
            grads['conv_ln_g'][j], grads['conv_ln_b'][j], grads['conv_dw_b'][j] = _fold(dlg), _fold(dlb), _fold(dwb)
            dp, ddw = _conv_act_bwd2(du.reshape(bsz, s_len, d), sv['p'].reshape(bsz, s_len, 2 * d),
                                     _pad_rows(wt['conv_dw'][j], 32))
            grads['conv_dw'][j] = ddw.sum(axis=1)[:wt['conv_dw'].shape[1]]
            dp = dp.reshape(t, 2 * d)
            gw, cs = _wgrad(sv['hn'], dp, name="conv_in_wgrad")
            grads['conv_w_in'][j] = _split_lanes([gw], 2 * d // N_DEV)
            grads['conv_b_in'][j] = _fold(cs)
            dx, dg = _mm(dp, wt['conv_w_in'][j], trans_b=True, norm_bwd=(sv['x_mix'], gm, dx), name="conv_in_dgrad")
            grads['norm_mix'][i] = _fold(dg)
        elif kind == 1:
            shp = (bsz, s_len, d)
            dxn, dwp, dbp, dsc, dg = _pool_bwd(sv['x_mix'].reshape(shp), dx.reshape(shp), sv['p'], gm, wt['pool_w'][j],
                                               row(wt['pool_b'][j]), row(wt['pool_scale'][j]))
            dx = dxn.reshape(t, d)
            ng, cg = dwp.shape[0], dwp.shape[1]
            grads['pool_w'][j] = jnp.moveaxis(dwp.reshape(ng, N_DEV, cg // N_DEV, cg), 1, 0).astype(BF16)
            grads['pool_b'][j] = _fold(dbp).reshape(wt['pool_b'].shape[1:])
            grads['pool_scale'][j] = _fold(dsc)
            grads['norm_mix'][i] = _fold(dg)
        else:
            shp = (bsz, s_len, d)
            do = _mm(dx, wt['fox_w_o'][j], trans_b=True, name="fox_out_dgrad")
            gw, _ = _wgrad(sv['o'], dx, name="fox_out_wgrad")
            grads['fox_w_o'][j] = gw.reshape(N_DEV, d // N_DEV, d)
            fl_args = (sv['qn'].reshape(shp), sv['kn'].reshape(shp), sv['vb'].reshape(shp), do.reshape(shp),
                       sv['o'].reshape(shp), sv['lse'], sv['crow'])
            items, groups = [], []
            for n in late_names:
                members = [grads[n][l] for l in sorted(grads[n]) if l >= cut[n]]
                groups.append(list(range(len(items), len(items) + len(members))))
                items += members
            dq, dk, dv, dcrow, got = _flash_bwd(*fl_args, items=items, groups=groups)
            late_recv = dict(zip(late_names, got))
            dc = jnp.swapaxes(dcrow.reshape(bsz, n_heads, s_len), 1, 2)
            dc = jnp.pad(dc, ((0, 0), (0, 0), (0, LANES - n_heads)))
            dfl, dbf = _fox_gate_bwd(dc, sv['proj'].reshape(bsz, s_len, -1), sv['bf'], n_heads)
            grads['fox_b_f'][j] = _fold(dbf)[:n_heads]
            dqkv, dgq, dgk = _fox_qknorm_bwd(sv['proj'], dq.reshape(t, d), dk.reshape(t, d), dv.reshape(t, d),
                                             sv['gq'], sv['gk'], d)
            grads['fox_q_gain'][j] = _fold(dgq).reshape(n_heads, HEAD_DIM).sum(axis=0)
            grads['fox_k_gain'][j] = _fold(dgk).reshape(n_heads, HEAD_DIM).sum(axis=0)
            dproj = jnp.concatenate([dqkv, dfl.reshape(t, LANES)], axis=1)
            dwp, _ = _wgrad(sv['hn'], dproj, name="fox_in_wgrad")
            grads['fox_w_in'][j] = _split_lanes([dwp], (3 * d + n_heads) // N_DEV)
            dx, dg = _mm(dproj, sv['wp'], trans_b=True, norm_bwd=(sv['x_mix'], gm, dx), name="fox_in_dgrad")
            grads['norm_mix'][i] = _fold(dg)

    small = {n: jnp.stack([g[k] for k in sorted(g)]) for n, g in grads.items() if n not in MATRICES}
    big = {n: [grads[n][k] for k in sorted(grads[n]) if n not in late_recv or k < cut[n]] for n in MATRICES}
    return sq.sum(), dx.reshape(bsz, s_len, d), small, big, late_recv


def _train_step(x, target, w, m, v):
    depth = w['norm_mix'].shape[0]
    attn = [i for i in range(depth) if i % 3 == 2]
    cut = {n: w[n].shape[0] for n in MATRICES}
    if len(attn) == 1:
        cut['ffn_w_up'] = cut['ffn_w_down'] = attn[0]
        later_conv = [i // 3 for i in range(attn[0] + 1, depth) if i % 3 == 0]
        if later_conv:
            cut['conv_w_in'] = cut['conv_w_out'] = later_conv[0]
    late_shards = {n: w[n][cut[n]:].astype(BF16) for n in MATRICES if 0 < cut[n] < w[n].shape[0]}
    got = _gather([w[n][:cut[n]].astype(BF16) for n in MATRICES] + [_flat_rows([w[n] for n in SMALL])])
    wt = _prepare_matrices(dict(zip(MATRICES, got[:-1])))
    wt.update(_prepare_vectors(got[-1], w))
    wt.update({n: w[n] for n in REPLICATED})
    sq, grad_x, gsmall, gbig, late_recv = _local_step(x, target, wt, late_shards, cut)
    d = x.shape[-1]

    shard_rows = jnp.concatenate([_to_shards(gsmall[n], SHARD_AXIS[n]) for n in SMALL], axis=1)
    rep = jnp.concatenate([gsmall[n].reshape(-1) for n in REPLICATED] + [(0.5 / d) * sq.reshape(1)])
    rows = jnp.concatenate([shard_rows, jnp.broadcast_to(rep, (N_DEV, rep.shape[0]))], axis=1)
    chunk = 8 * ADAM_COLS
    n_all = rows.shape[1]
    n_pad = -(-n_all // chunk) * chunk
    rows = jnp.pad(rows, ((0, 0), (0, n_pad - n_all))).reshape(N_DEV, n_pad // ADAM_COLS, ADAM_COLS)

    items, groups = [], []
    for n in MATRICES:
        groups.append(list(range(len(items), len(items) + len(gbig[n]))))
        items += gbig[n]
    groups.append([len(items)])
    items.append(rows)
    recv = _scatter(items, groups)

    res = [{}, {}, {}, {}]
    for n, r in zip(MATRICES, recv[:-1]):
        if n in late_recv:
            c = cut[n]
            outs = zip(_adamw(r, w[n][:c], m[n][:c], v[n][:c]), _adamw(late_recv[n], w[n][c:], m[n][c:], v[n][c:]))
            outs = [jnp.concatenate(pair, axis=0) for pair in outs]
        else:
            outs = _adamw(r, w[n], m[n], v[n])
        for k, o in enumerate(outs):
            res[k][n] = o
    order = SMALL + REPLICATED

    def flat(tree):
        parts = jnp.concatenate([tree[n].reshape(-1) for n in order])
        return jnp.pad(parts, (0, n_pad - parts.shape[0])).reshape(n_pad // ADAM_COLS, ADAM_COLS)

    outs = [o.reshape(-1) for o in _adamw(recv[-1].reshape((N_CHIP,) + rows.shape[1:]), flat(w), flat(m), flat(v))]
    off = 0
    for n in order:
        size = w[n].size
        for k in range(4):
            res[k][n] = outs[k][off:off + size].reshape(w[n].shape)
        off += size
    loss = outs[0][n_all - 1]
    return (loss, grad_x, *[res[0][n] for n in WEIGHTS], *[res[1][n] for n in WEIGHTS],
            *[res[2][n] for n in WEIGHTS], *[res[3][n] for n in WEIGHTS])


def kernel(x, norm_mix, norm_ffn, conv_w_in, conv_b_in, conv_dw, conv_dw_b, conv_ln_g, conv_ln_b, conv_w_out, conv_b_out, pool_w, pool_b, pool_scale, fox_w_in, fox_b_f, fox_q_gain, fox_k_gain, fox_w_o, ffn_w_up, ffn_dw, ffn_dw_b, ffn_w_down, loss_target, m_norm_mix, m_norm_ffn, m_conv_w_in, m_conv_b_in, m_conv_dw, m_conv_dw_b, m_conv_ln_g, m_conv_ln_b, m_conv_w_out, m_conv_b_out, m_pool_w, m_pool_b, m_pool_scale, m_fox_w_in, m_fox_b_f, m_fox_q_gain, m_fox_k_gain, m_fox_w_o, m_ffn_w_up, m_ffn_dw, m_ffn_dw_b, m_ffn_w_down, v_norm_mix, v_norm_ffn, v_conv_w_in, v_conv_b_in, v_conv_dw, v_conv_dw_b, v_conv_ln_g, v_conv_ln_b, v_conv_w_out, v_conv_b_out, v_pool_w, v_pool_b, v_pool_scale, v_fox_w_in, v_fox_b_f, v_fox_q_gain, v_fox_k_gain, v_fox_w_o, v_ffn_w_up, v_ffn_dw, v_ffn_dw_b, v_ffn_w_down):
    w = dict(zip(WEIGHTS, (norm_mix, norm_ffn, conv_w_in, conv_b_in, conv_dw, conv_dw_b, conv_ln_g, conv_ln_b, conv_w_out, conv_b_out, pool_w, pool_b, pool_scale, fox_w_in, fox_b_f, fox_q_gain, fox_k_gain, fox_w_o, ffn_w_up, ffn_dw, ffn_dw_b, ffn_w_down)))
    m = dict(zip(WEIGHTS, (m_norm_mix, m_norm_ffn, m_conv_w_in, m_conv_b_in, m_conv_dw, m_conv_dw_b, m_conv_ln_g, m_conv_ln_b, m_conv_w_out, m_conv_b_out, m_pool_w, m_pool_b, m_pool_scale, m_fox_w_in, m_fox_b_f, m_fox_q_gain, m_fox_k_gain, m_fox_w_o, m_ffn_w_up, m_ffn_dw, m_ffn_dw_b, m_ffn_w_down)))
    v = dict(zip(WEIGHTS, (v_norm_mix, v_norm_ffn, v_conv_w_in, v_conv_b_in, v_conv_dw, v_conv_dw_b, v_conv_ln_g, v_conv_ln_b, v_conv_w_out, v_conv_b_out, v_pool_w, v_pool_b, v_pool_scale, v_fox_w_in, v_fox_b_f, v_fox_q_gain, v_fox_k_gain, v_fox_w_o, v_ffn_w_up, v_ffn_dw, v_ffn_dw_b, v_ffn_w_down)))
    return _train_step(x, loss_target, w, m, v)
```

```python
import functools
import math

import jax
import jax.numpy as jnp
from jax import lax
from jax.experimental import pallas as pl
from jax.experimental.pallas import tpu as pltpu

F32, BF16 = jnp.float32, jnp.bfloat16
SDS = jax.ShapeDtypeStruct

N_DEV = 8
EPS = 1e-6
POOL_WINDOWS = (2, 4, 8, 16)
HEAD_DIM = 64
ADAM_LR, ADAM_B1, ADAM_B2, ADAM_EPS, ADAM_WD, ADAM_STEP = 0.001, 0.9, 0.999, 1e-08, 0.01, 10
LANES = 128
VMEM_LIMIT_BYTES = 48 * 1024 * 1024
RESIDENT_WEIGHT_BYTES = 7 * 1024 * 1024
NEG = -1e30

WEIGHTS = ['norm_mix', 'norm_ffn', 'conv_w_in', 'conv_b_in', 'conv_dw', 'conv_dw_b', 'conv_ln_g', 'conv_ln_b',
           'conv_w_out', 'conv_b_out', 'pool_w', 'pool_b', 'pool_scale', 'fox_w_in', 'fox_b_f', 'fox_q_gain',
           'fox_k_gain', 'fox_w_o', 'ffn_w_up', 'ffn_dw', 'ffn_dw_b', 'ffn_w_down']
SHARD_AXIS = {'conv_w_in': 2, 'conv_b_in': 1, 'conv_dw': 2, 'conv_dw_b': 1, 'conv_ln_g': 1, 'conv_ln_b': 1,
              'conv_w_out': 1, 'conv_b_out': 1, 'pool_w': 2, 'pool_b': 2, 'fox_w_in': 2, 'fox_w_o': 1,
              'ffn_w_up': 2, 'ffn_dw': 2, 'ffn_w_down': 1}
MATRICES = ('conv_w_in', 'conv_w_out', 'pool_w', 'fox_w_in', 'fox_w_o', 'ffn_w_up', 'ffn_w_down')
SHARDED = [n for n in WEIGHTS if n in SHARD_AXIS]
REPLICATED = [n for n in WEIGHTS if n not in SHARD_AXIS]


def _cp(*sem):
    return pltpu.CompilerParams(dimension_semantics=sem, vmem_limit_bytes=VMEM_LIMIT_BYTES)


def _tile(n, pref, align=8):
    if n <= pref:
        return n
    t = (pref // align) * align
    while t >= align:
        if n % t == 0:
            return t
        t -= align
    return n


def _fold8(x):
    r, c = x.shape
    return x.reshape(r // 8, 8, c).sum(axis=0)


def _sigmoid(x):
    return 0.5 * jnp.tanh(0.5 * x) + 0.5


def _shifts_back(cur, tail, n):
    hb = tail.shape[0]
    xe = jnp.concatenate([tail, cur], axis=0)
    return [cur] + [pltpu.roll(xe, j, axis=0)[hb:] for j in range(1, n)]


def _shifts_fwd(cur, head, n):
    ts = cur.shape[0]
    xe = jnp.concatenate([cur, head], axis=0)
    ln = xe.shape[0]
    return [cur] + [pltpu.roll(xe, ln - j, axis=0)[:ts] for j in range(1, n)]


def _dot_hi(a, b):
    return jnp.dot(a, b, preferred_element_type=F32, precision=lax.Precision.HIGHEST)


def _rmsnorm_fwd(x, g):
    t, d = x.shape
    tm = _tile(t, 512)

    def body(x_ref, g_ref, h_ref):
        xv = x_ref[...]
        r = lax.rsqrt(jnp.mean(xv * xv, axis=-1, keepdims=True) + EPS)
        h_ref[...] = (xv * r * g_ref[...]).astype(BF16)

    return pl.pallas_call(
        body, out_shape=SDS((t, d), BF16), grid=(t // tm,), name="rmsnorm_fwd",
        in_specs=[pl.BlockSpec((tm, d), lambda i: (i, 0)), pl.BlockSpec((1, d), lambda i: (0, 0))],
        out_specs=pl.BlockSpec((tm, d), lambda i: (i, 0)), compiler_params=_cp("parallel"))(x, g)


def _mm(a, b, *, trans_b=False, bias=None, residual=None, a2=None, b2=None, norm_gain=None, norm_bwd=None,
        out_dtype=BF16, name="mm"):
    m, k = a.shape
    n = b.shape[0] if trans_b else b.shape[1]
    tm, tn, tk = _tile(m, 1024, 16), _tile(n, 1536, LANES), _tile(k, 1536, LANES)
    small_b = k * n * 2 <= RESIDENT_WEIGHT_BYTES
    if norm_bwd is not None:
        if a2 is not None:
            tm, tk = _tile(m, 256, 16), k
        elif small_b:
            tm, tk = _tile(m, 512, 16), k
        else:
            tk = _tile(k, 768, LANES)
    elif small_b:
        tm, tn, tk = _tile(m, 512, 16), n, k
    nk = k // tk
    two = a2 is not None
    steps = 2 * nk if two else nk
    dims = (((1,), (1,)), ((), ())) if trans_b else (((1,), (0,)), ((), ()))
    has_bias, has_res, has_norm, has_nbwd = bias is not None, residual is not None, norm_gain is not None, \
        norm_bwd is not None
    assert not (has_norm or has_nbwd) or tn == n
    n_out = 2 if (has_norm or has_nbwd) else 1

    def body(*refs):
        pos = 4 if two else 2
        bias_ref = refs[pos] if has_bias else None
        pos += has_bias
        res_ref = refs[pos] if has_res else None
        pos += has_res
        gain_ref = refs[pos] if has_norm else None
        pos += has_norm
        x_ref, g_ref, dres_ref = refs[pos:pos + 3] if has_nbwd else (None, None, None)
        pos += 3 * has_nbwd
        outs = refs[pos:pos + n_out]
        first = (pl.program_id(0) == 0) & (pl.program_id(1) == 0) & (pl.program_id(2) == 0)

        def finish(r):
            if has_bias:
                r = r + bias_ref[...]
            if has_res:
                r = r + res_ref[...]
            if has_nbwd:
                xv = x_ref[...]
                rs = lax.rsqrt(jnp.mean(xv * xv, axis=-1, keepdims=True) + EPS)
                xh = xv * rs
                u = r * g_ref[...]
                outs[0][...] = dres_ref[...] + rs * (u - xh * jnp.mean(u * xh, axis=-1, keepdims=True))
                outs[1][...] += _fold8(r * xh)
                return
            outs[0][...] = r.astype(out_dtype)
            if has_norm:
                outs[1][...] = (r * lax.rsqrt(jnp.mean(r * r, axis=-1, keepdims=True) + EPS)
                                * gain_ref[...]).astype(BF16)

        def dot(a_ref, b_ref):
            return lax.dot_general(a_ref[...].astype(BF16), b_ref[...].astype(BF16), dims, preferred_element_type=F32)

        if has_nbwd:
            @pl.when(first)
            def _():
                outs[1][...] = jnp.zeros_like(outs[1])

        if steps == 1:
            finish(dot(refs[0], refs[1]))
            return
        acc_ref = refs[-1]
        kk = pl.program_id(2)

        @pl.when(kk == 0)
        def _():
            acc_ref[...] = jnp.zeros_like(acc_ref)

        @pl.when(kk < nk)
        def _():
            acc_ref[...] += dot(refs[0], refs[1])

        if two:
            @pl.when(kk >= nk)
            def _():
                acc_ref[...] += dot(refs[2], refs[3])

        @pl.when(kk == steps - 1)
        def _():
            finish(acc_ref[...])

    def pair(first):
        kmap = (lambda kk: jnp.minimum(kk, nk - 1)) if first else (lambda kk: jnp.maximum(kk - nk, 0))
        a_spec = pl.BlockSpec((tm, tk), lambda j, i, kk: (i, kmap(kk)))
        if trans_b:
            b_spec = pl.BlockSpec((tn, tk), lambda j, i, kk: (j, kmap(kk)))
        else:
            b_spec = pl.BlockSpec((tk, tn), lambda j, i, kk: (kmap(kk), j))
        return [a_spec, b_spec]

    in_specs, args = pair(True), [a, b]
    if two:
        in_specs += pair(False)
        args += [a2, b2]
    if has_bias:
        in_specs.append(pl.BlockSpec((1, tn), lambda j, i, kk: (0, j)))
        args.append(bias)
    tile = pl.BlockSpec((tm, tn), lambda j, i, kk: (i, j))
    vec = pl.BlockSpec((1, tn), lambda j, i, kk: (0, j))
    if has_res:
        in_specs.append(tile)
        args.append(residual)
    if has_norm:
        in_specs.append(vec)
        args.append(norm_gain)
    if has_nbwd:
        in_specs += [tile, vec, tile]
        args += list(norm_bwd)
    out_shape, out_specs = [SDS((m, n), F32 if has_nbwd else out_dtype)], [tile]
    if has_norm:
        out_shape.append(SDS((m, n), BF16))
        out_specs.append(tile)
    if has_nbwd:
        out_shape.append(SDS((8, n), F32))
        out_specs.append(pl.BlockSpec((8, tn), lambda j, i, kk: (0, 0)))
    outs = pl.pallas_call(
        body, out_shape=out_shape, grid=(n // tn, m // tm, steps), name=name, in_specs=in_specs, out_specs=out_specs,
        scratch_shapes=[] if steps == 1 else [pltpu.VMEM((tm, tn), F32)],
        compiler_params=_cp("arbitrary", "arbitrary", "arbitrary"))(*args)
    return outs[0] if n_out == 1 else tuple(outs)


def _wgrad(a, g, *, out_dtype=BF16, name="wgrad"):
    m, ka = a.shape
    n = g.shape[1]
    ta, tn = _tile(ka, 1536, LANES), _tile(n, 1536, LANES)
    tm = _tile(m, 2048 if g.dtype == BF16 else 1024)
    nm = m // tm

    def body(a_ref, g_ref, o_ref, cs_ref, acc_ref):
        i, mm = pl.program_id(1), pl.program_id(2)

        @pl.when(mm == 0)
        def _():
            acc_ref[...] = jnp.zeros_like(acc_ref)

        @pl.when((mm == 0) & (i == 0))
        def _():
            cs_ref[...] = jnp.zeros_like(cs_ref)

        gv = g_ref[...]
        acc_ref[...] += lax.dot_general(a_ref[...].astype(BF16), gv.astype(BF16), (((0,), (0,)), ((), ())),
                                        preferred_element_type=F32)

        @pl.when(i == 0)
        def _():
            cs_ref[...] += _fold8(gv.astype(F32))

        @pl.when(mm == nm - 1)
        def _():
            o_ref[...] = acc_ref[...].astype(out_dtype)

    return pl.pallas_call(
        body, out_shape=(SDS((ka, n), out_dtype), SDS((8, n), F32)), grid=(n // tn, ka // ta, nm), name=name,
        in_specs=[pl.BlockSpec((tm, ta), lambda j, i, mm: (mm, i)), pl.BlockSpec((tm, tn), lambda j, i, mm: (mm, j))],
        out_specs=(pl.BlockSpec((ta, tn), lambda j, i, mm: (i, j)), pl.BlockSpec((8, tn), lambda j, i, mm: (0, j))),
        scratch_shapes=[pltpu.VMEM((ta, tn), F32)],
        compiler_params=_cp("arbitrary", "arbitrary", "arbitrary"))(a, g)


FFN_HALO = 16


def _ffn_conv(uc_ref, up_ref, w_ref, b_ref, s):
    u = uc_ref[...].astype(F32)
    tail = jnp.where(s > 0, up_ref[...].astype(F32), 0.0)
    sh = _shifts_back(u, tail, 3)
    return sh, sh[2] * w_ref[0:1, :] + sh[1] * w_ref[1:2, :] + sh[0] * w_ref[2:3, :] + b_ref[...]


def _ffn_act_fwd(uv, ug, dw8, b):
    bsz, s_len, f = uv.shape
    tc, ts = f, _tile(s_len, 256, FFN_HALO)
    nf, r = f // tc, ts // FFN_HALO

    def body(uv_ref, uvp_ref, ug_ref, ugp_ref, wv_ref, wg_ref, bv_ref, bg_ref, a_ref, vv_ref, vg_ref):
        s = pl.program_id(2)
        _, val = _ffn_conv(uv_ref, uvp_ref, wv_ref, bv_ref, s)
        _, gate = _ffn_conv(ug_ref, ugp_ref, wg_ref, bg_ref, s)
        a_ref[...] = (gate * _sigmoid(gate) * val).astype(BF16)
        vv_ref[...] = val.astype(BF16)
        vg_ref[...] = gate.astype(BF16)

    cur = pl.BlockSpec((None, ts, tc), lambda bi, j, s: (bi, s, j))
    prev = pl.BlockSpec((None, FFN_HALO, tc), lambda bi, j, s: (bi, jnp.maximum(s * r - 1, 0), j))

    def par(rows, off):
        return pl.BlockSpec((rows, tc), lambda bi, j, s: (0, j + off))

    return pl.pallas_call(
        body, out_shape=(SDS((bsz, s_len, f), BF16),) * 3, grid=(bsz, nf, s_len // ts), name="ffn_act_fwd",
        in_specs=[cur, prev, cur, prev, par(8, 0), par(8, nf), par(1, 0), par(1, nf)], out_specs=(cur, cur, cur),
        compiler_params=_cp("parallel", "parallel", "arbitrary"))(uv, uv, ug, ug, dw8, dw8, b, b)


def _ffn_act_bwd1(vv, vg, da):
    bsz, s_len, f = vv.shape
    tc, ts = f, _tile(s_len, 256, FFN_HALO)
    nf = f // tc

    def body(vv_ref, vg_ref, da_ref, dvv_ref, dvg_ref, dbv_ref, dbg_ref):
        @pl.when((pl.program_id(1) == 0) & (pl.program_id(2) == 0))
        def _():
            dbv_ref[...] = jnp.zeros_like(dbv_ref)
            dbg_ref[...] = jnp.zeros_like(dbg_ref)

        val, gate = vv_ref[...].astype(F32), vg_ref[...].astype(F32)
        sg = _sigmoid(gate)
        dav = da_ref[...].astype(F32)
        dval = dav * gate * sg
        dgate = dav * val * (sg * (1.0 + gate * (1.0 - sg)))
        dvv_ref[...] = dval.astype(BF16)
        dvg_ref[...] = dgate.astype(BF16)
        dbv_ref[...] += _fold8(dval)
        dbg_ref[...] += _fold8(dgate)

    cur = pl.BlockSpec((None, ts, tc), lambda j, bi, s: (bi, s, j))
    acc1 = pl.BlockSpec((8, tc), lambda j, bi, s: (0, j))
    return pl.pallas_call(
        body, out_shape=(SDS((bsz, s_len, f), BF16), SDS((bsz, s_len, f), BF16), SDS((8, f), F32), SDS((8, f), F32)),
        grid=(nf, bsz, s_len // ts), name="ffn_act_bwd1", in_specs=[cur, cur, cur], out_specs=(cur, cur, acc1, acc1),
        compiler_params=_cp("arbitrary", "arbitrary", "arbitrary"))(vv, vg, da)


def _ffn_act_bwd2(dvv, dvg, uv, ug, dw8):
    bsz, s_len, f = dvv.shape
    tc, ts = f, _tile(s_len, 256, FFN_HALO)
    nf, r, ns = f // tc, ts // FFN_HALO, s_len // ts

    def body(vc_ref, vn_ref, gc_ref, gn_ref, uv_ref, ug_ref, wv_ref, wg_ref, duv_ref, dug_ref, ddwv_ref, ddwg_ref):
        bi, s = pl.program_id(1), pl.program_id(2)

        @pl.when((bi == 0) & (s == 0))
        def _():
            ddwv_ref[...] = jnp.zeros_like(ddwv_ref)
            ddwg_ref[...] = jnp.zeros_like(ddwg_ref)

        for dc_ref, dn_ref, u_ref, w_ref, du_ref, ddw_ref in (
                (vc_ref, vn_ref, uv_ref, wv_ref, duv_ref, ddwv_ref),
                (gc_ref, gn_ref, ug_ref, wg_ref, dug_ref, ddwg_ref)):
            d = dc_ref[...].astype(F32)
            head = jnp.where(s < ns - 1, dn_ref[...].astype(F32), 0.0)
            sh = _shifts_fwd(d, head, 3)
            du_ref[...] = (sh[0] * w_ref[2:3, :] + sh[1] * w_ref[1:2, :] + sh[2] * w_ref[0:1, :]).astype(BF16)
            u = u_ref[...].astype(F32)
            for j in range(3):
                ddw_ref[2 - j] += _fold8(sh[j] * u)

    cur = pl.BlockSpec((None, ts, tc), lambda j, bi, s: (bi, s, j))
    nxt = pl.BlockSpec((None, FFN_HALO, tc),
                       lambda j, bi, s: (bi, jnp.minimum((s + 1) * r, s_len // FFN_HALO - 1), j))

    def par(off):
        return pl.BlockSpec((8, tc), lambda j, bi, s: (0, j + off))

    acc3 = pl.BlockSpec((3, 8, tc), lambda j, bi, s: (0, 0, j))
    return pl.pallas_call(
        body, out_shape=(SDS((bsz, s_len, f), BF16), SDS((bsz, s_len, f), BF16), SDS((3, 8, f), F32),
                         SDS((3, 8, f), F32)),
        grid=(nf, bsz, ns), name="ffn_act_bwd2",
        in_specs=[cur, nxt, cur, nxt, cur, cur, par(0), par(nf)], out_specs=(cur, cur, acc3, acc3),
        compiler_params=_cp("arbitrary", "arbitrary", "arbitrary"))(dvv, dvv, dvg, dvg, uv, ug, dw8, dw8)


CONV_HALO = 32
CONV_CHUNK = 256


def _conv_act_fwd(p, dw32, dwb, ln_g, ln_b):
    bsz, s_len, d2 = p.shape
    d = d2 // 2
    kw = 31
    ts = _tile(s_len, 256, CONV_HALO)
    r = ts // CONV_HALO
    cc = min(CONV_CHUNK, d)

    def body(pc_ref, pp_ref, w_ref, wb_ref, g_ref, b_ref, u_ref, s_ref):
        s = pl.program_id(1)
        tot = jnp.zeros((ts, 1), F32)
        for c0 in range(0, d, cc):
            a = pc_ref[:, c0:c0 + cc].astype(F32)
            g = pc_ref[:, d + c0:d + c0 + cc].astype(F32)
            z = a * _sigmoid(g)
            ap = pp_ref[:, c0:c0 + cc].astype(F32)
            gp = pp_ref[:, d + c0:d + c0 + cc].astype(F32)
            tail = jnp.where(s > 0, ap * _sigmoid(gp), 0.0)
            sh = _shifts_back(z, tail, kw)
            acc = wb_ref[:, c0:c0 + cc] + sh[0] * w_ref[kw - 1:kw, c0:c0 + cc]
            for j in range(1, kw):
                acc = acc + sh[j] * w_ref[kw - 1 - j:kw - j, c0:c0 + cc]
            u_ref[:, c0:c0 + cc] = acc
            tot = tot + jnp.sum(acc, axis=-1, keepdims=True)
        u = u_ref[...]
        mu = tot / d
        uc = u - mu
        var = jnp.mean(uc * uc, axis=-1, keepdims=True)
        ul = uc * lax.rsqrt(var + EPS) * g_ref[...] + b_ref[...]
        s_ref[...] = (ul * _sigmoid(ul)).astype(BF16)

    vec = pl.BlockSpec((1, d), lambda bi, s: (0, 0))
    return pl.pallas_call(
        body, out_shape=(SDS((bsz, s_len, d), F32), SDS((bsz, s_len, d), BF16)), grid=(bsz, s_len // ts),
        name="conv_act_fwd",
        in_specs=[pl.BlockSpec((None, ts, d2), lambda bi, s: (bi, s, 0)),
                  pl.BlockSpec((None, CONV_HALO, d2), lambda bi, s: (bi, jnp.maximum(s * r - 1, 0), 0)),
                  pl.BlockSpec((32, d), lambda bi, s: (0, 0)), vec, vec, vec],
        out_specs=(pl.BlockSpec((None, ts, d), lambda bi, s: (bi, s, 0)),
                   pl.BlockSpec((None, ts, d), lambda bi, s: (bi, s, 0))),
        compiler_params=_cp("parallel", "arbitrary"))(p, p, dw32, dwb, ln_g, ln_b)


def _conv_act_bwd1(u, ds, ln_g, ln_b):
    t, d = u.shape
    ts = _tile(t, 256)

    def body(u_ref, ds_ref, g_ref, b_ref, du_ref, dg_ref, db_ref, dwb_ref):
        @pl.when(pl.program_id(0) == 0)
        def _():
            dg_ref[...] = jnp.zeros_like(dg_ref)
            db_ref[...] = jnp.zeros_like(db_ref)
            dwb_ref[...] = jnp.zeros_like(dwb_ref)

        uv = u_ref[...]
        uc = uv - jnp.mean(uv, axis=-1, keepdims=True)
        rstd = lax.rsqrt(jnp.mean(uc * uc, axis=-1, keepdims=True) + EPS)
        uh = uc * rstd
        ul = uh * g_ref[...] + b_ref[...]
        sg = _sigmoid(ul)
        dul = ds_ref[...].astype(F32) * (sg * (1.0 + ul * (1.0 - sg)))
        duh = dul * g_ref[...]
        du = rstd * (duh - jnp.mean(duh, axis=-1, keepdims=True) - uh * jnp.mean(duh * uh, axis=-1, keepdims=True))
        du_ref[...] = du
        dg_ref[...] += _fold8(dul * uh)
        db_ref[...] += _fold8(dul)
        dwb_ref[...] += _fold8(du)

    row = pl.BlockSpec((ts, d), lambda i: (i, 0))
    vec = pl.BlockSpec((1, d), lambda i: (0, 0))
    acc = pl.BlockSpec((8, d), lambda i: (0, 0))
    return pl.pallas_call(
        body, out_shape=(SDS((t, d), F32), SDS((8, d), F32), SDS((8, d), F32), SDS((8, d), F32)), grid=(t // ts,),
        name="conv_act_bwd1", in_specs=[row, row, vec, vec], out_specs=(row, acc, acc, acc),
        compiler_params=_cp("arbitrary"))(u, ds, ln_g, ln_b)


def _conv_act_bwd2(du, p, dw32):
    bsz, s_len, d2 = p.shape
    d = d2 // 2
    kw = 31
    ts = _tile(s_len, 256, CONV_HALO)
    r, ns = ts // CONV_HALO, s_len // ts
    cc = min(CONV_CHUNK, d)

    def body(dc_ref, dn_ref, pc_ref, pp_ref, w_ref, dp_ref, ddw_ref):
        bi, s = pl.program_id(0), pl.program_id(1)

        @pl.when((bi == 0) & (s == 0))
        def _():
            ddw_ref[...] = jnp.zeros_like(ddw_ref)

        for c0 in range(0, d, cc):
            a = pc_ref[:, c0:c0 + cc].astype(F32)
            g = pc_ref[:, d + c0:d + c0 + cc].astype(F32)
            sg = _sigmoid(g)
            z = a * sg
            ap = pp_ref[:, c0:c0 + cc].astype(F32)
            gp = pp_ref[:, d + c0:d + c0 + cc].astype(F32)
            tail = jnp.where(s > 0, ap * _sigmoid(gp), 0.0)
            duv = dc_ref[:, c0:c0 + cc]
            head = jnp.where(s < ns - 1, dn_ref[:, c0:c0 + cc], 0.0)
            zb = _shifts_back(z, tail, kw)
            for k in range(kw):
                ddw_ref[k, :, c0:c0 + cc] += _fold8(duv * zb[kw - 1 - k])
            df = _shifts_fwd(duv, head, kw)
            dz = df[0] * w_ref[kw - 1:kw, c0:c0 + cc]
            for j in range(1, kw):
                dz = dz + df[j] * w_ref[kw - 1 - j:kw - j, c0:c0 + cc]
            dp_ref[:, c0:c0 + cc] = (dz * sg).astype(BF16)
            dp_ref[:, d + c0:d + c0 + cc] = (dz * a * sg * (1.0 - sg)).astype(BF16)

    return pl.pallas_call(
        body, out_shape=(SDS((bsz, s_len, d2), BF16), SDS((32, 8, d), F32)), grid=(bsz, ns), name="conv_act_bwd2",
        in_specs=[pl.BlockSpec((None, ts, d), lambda bi, s: (bi, s, 0)),
                  pl.BlockSpec((None, CONV_HALO, d),
                               lambda bi, s: (bi, jnp.minimum((s + 1) * r, s_len // CONV_HALO - 1), 0)),
                  pl.BlockSpec((None, ts, d2), lambda bi, s: (bi, s, 0)),
                  pl.BlockSpec((None, CONV_HALO, d2), lambda bi, s: (bi, jnp.maximum(s * r - 1, 0), 0)),
                  pl.BlockSpec((32, d), lambda bi, s: (0, 0))],
        out_specs=(pl.BlockSpec((None, ts, d2), lambda bi, s: (bi, s, 0)),
                   pl.BlockSpec((32, 8, d), lambda bi, s: (0, 0, 0))),
        compiler_params=_cp("arbitrary", "arbitrary"))(du, du, p, p, dw32)


POOL_HALO = 16


def _pool_counts(s, ts, rows, w):
    t = s * ts + lax.broadcasted_iota(jnp.int32, (rows, 1), 0)
    return jnp.minimum(t + 1, w).astype(F32)


def _pool_fwd(x, gmix, w, b, scale):
    bsz, s_len, d = x.shape
    ng = len(POOL_WINDOWS)
    cg = d // ng
    ts = _tile(s_len, 512, POOL_HALO)
    r = ts // POOL_HALO

    def body(xc_ref, xp_ref, g_ref, w_ref, b_ref, sc_ref, y_ref, p_ref):
        s = pl.program_id(1)

        def norm(v):
            return v * lax.rsqrt(jnp.mean(v * v, axis=-1, keepdims=True) + EPS) * g_ref[...]

        xc = xc_ref[...]
        h = norm(xc)
        tail = jnp.where(s > 0, norm(xp_ref[...]), 0.0)
        for gi, win in enumerate(POOL_WINDOWS):
            lo, hi = gi * cg, (gi + 1) * cg
            hg = h[:, lo:hi]
            acc = jnp.concatenate([tail[:, lo:hi], hg], axis=0)
            step = 1
            while step < win:
                acc = acc + pltpu.roll(acc, step, axis=0)
                step *= 2
            pg = acc[POOL_HALO:] / _pool_counts(s, ts, ts, win) - hg
            pb = pg.astype(BF16)
            p_ref[:, lo:hi] = pb
            yg = jnp.dot(pb, w_ref[gi], preferred_element_type=F32) + b_ref[:, lo:hi]
            y_ref[:, lo:hi] = xc[:, lo:hi] + yg * sc_ref[:, lo:hi]

    vec = pl.BlockSpec((1, d), lambda bi, s: (0, 0))
    blk = pl.BlockSpec((None, ts, d), lambda bi, s: (bi, s, 0))
    return pl.pallas_call(
        body, out_shape=(SDS((bsz, s_len, d), F32), SDS((bsz, s_len, d), BF16)), grid=(bsz, s_len // ts),
        name="pool_fwd",
        in_specs=[blk, pl.BlockSpec((None, POOL_HALO, d), lambda bi, s: (bi, jnp.maximum(s * r - 1, 0), 0)),
                  vec, pl.BlockSpec((ng, cg, cg), lambda bi, s: (0, 0, 0)), vec, vec],
        out_specs=(blk, blk), compiler_params=_cp("parallel", "arbitrary"))(x, x, gmix, w, b, scale)


def _pool_bwd(x, dy, p, gmix, w, b, scale):
    bsz, s_len, d = x.shape
    ng = len(POOL_WINDOWS)
    cg = d // ng
    ts = _tile(s_len, 512, POOL_HALO)
    r, ns = ts // POOL_HALO, s_len // ts
    nt = (((1,), (1,)), ((), ()))
    tn = (((0,), (0,)), ((), ()))

    def body(x_ref, dy_ref, dyn_ref, p_ref, g_ref, w_ref, b_ref, sc_ref, dx_ref, dw_ref, db_ref, dsc_ref, dg_ref):
        bi, s = pl.program_id(0), pl.program_id(1)

        @pl.when((bi == 0) & (s == 0))
        def _():
            dw_ref[...] = jnp.zeros_like(dw_ref)
            db_ref[...] = jnp.zeros_like(db_ref)
            dsc_ref[...] = jnp.zeros_like(dsc_ref)
            dg_ref[...] = jnp.zeros_like(dg_ref)

        dy = dy_ref[...]
        dyy = dy * sc_ref[...]
        dyy_n = jnp.where(s < ns - 1, dyn_ref[...] * sc_ref[...], 0.0)
        db_ref[...] += _fold8(dyy)
        xv = x_ref[...]
        rr = lax.rsqrt(jnp.mean(xv * xv, axis=-1, keepdims=True) + EPS)
        xh = xv * rr
        for gi, win in enumerate(POOL_WINDOWS):
            lo, hi = gi * cg, (gi + 1) * cg
            pb = p_ref[:, lo:hi]
            wg = w_ref[gi]
            pre = jnp.dot(pb, wg, preferred_element_type=F32) + b_ref[:, lo:hi]
            dsc_ref[:, lo:hi] += _fold8(dy[:, lo:hi] * pre)
            dyb = dyy[:, lo:hi].astype(BF16)
            dw_ref[gi] += lax.dot_general(pb, dyb, tn, preferred_element_type=F32)
            dp = lax.dot_general(dyb, wg, nt, preferred_element_type=F32)
            dp_n = lax.dot_general(dyy_n[:, lo:hi].astype(BF16), wg, nt, preferred_element_type=F32)
            q = dp / _pool_counts(s, ts, ts, win)
            q_n = dp_n / _pool_counts(s + 1, ts, POOL_HALO, win)
            acc = jnp.concatenate([q, q_n], axis=0)
            ln = ts + POOL_HALO
            step = 1
            while step < win:
                acc = acc + pltpu.roll(acc, ln - step, axis=0)
                step *= 2
            dh = acc[:ts] - dp
            xhg = xh[:, lo:hi]
            dg_ref[:, lo:hi] += _fold8(dh * xhg)
            dx_ref[:, lo:hi] = dh * g_ref[:, lo:hi]
        u = dx_ref[...]
        dx_ref[...] = dy + rr * (u - xh * jnp.mean(u * xh, axis=-1, keepdims=True))

    vec = pl.BlockSpec((1, d), lambda bi, s: (0, 0))
    acc8 = pl.BlockSpec((8, d), lambda bi, s: (0, 0))
    blk = pl.BlockSpec((None, ts, d), lambda bi, s: (bi, s, 0))
    wspec = pl.BlockSpec((ng, cg, cg), lambda bi, s: (0, 0, 0))
    return pl.pallas_call(
        body, out_shape=(SDS((bsz, s_len, d), F32), SDS((ng, cg, cg), F32), SDS((8, d), F32), SDS((8, d), F32),
                         SDS((8, d), F32)),
        grid=(bsz, ns), name="pool_bwd",
        in_specs=[blk, blk,
                  pl.BlockSpec((None, POOL_HALO, d),
                               lambda bi, s: (bi, jnp.minimum((s + 1) * r, s_len // POOL_HALO - 1), 0)),
                  blk, vec, wspec, vec, vec],
        out_specs=(blk, wspec, acc8, acc8, acc8),
        compiler_params=_cp("arbitrary", "arbitrary"))(x, dy, dy, p, gmix, w, b, scale)


def _tri(n, upper):
    row = lax.broadcasted_iota(jnp.int32, (n, n), 0)
    col = lax.broadcasted_iota(jnp.int32, (n, n), 1)
    return jnp.where((col >= row) if upper else (col <= row), 1.0, 0.0).astype(F32)


def _fox_gate_fwd(proj, bf, n_heads):
    bsz, s_len, width = proj.shape
    col = width // LANES - 1
    ts = _tile(s_len, 512)

    def body(fl_ref, b_ref, c_ref, carry_ref):
        @pl.when(pl.program_id(1) == 0)
        def _():
            carry_ref[...] = jnp.zeros_like(carry_ref)

        xv = fl_ref[...] + b_ref[...]
        logf = jnp.minimum(xv, 0.0) - jnp.log(1.0 + jnp.exp(-jnp.abs(xv)))
        lane = lax.broadcasted_iota(jnp.int32, (1, LANES), 1)
        logf = jnp.where(lane < n_heads, logf, 0.0)
        c = _dot_hi(_tri(ts, False), logf) + carry_ref[0:1, :]
        c_ref[...] = c
        carry_ref[0:1, :] = c[ts - 1:ts, :]

    return pl.pallas_call(
        body, out_shape=SDS((bsz, s_len, LANES), F32), grid=(bsz, s_len // ts), name="fox_gate_fwd",
        in_specs=[pl.BlockSpec((None, ts, LANES), lambda bi, s: (bi, s, col)),
                  pl.BlockSpec((1, LANES), lambda bi, s: (0, 0))],
        out_specs=pl.BlockSpec((None, ts, LANES), lambda bi, s: (bi, s, 0)),
        scratch_shapes=[pltpu.VMEM((8, LANES), F32)],
        compiler_params=_cp("arbitrary", "arbitrary"))(proj, bf)


def _fox_gate_bwd(dc, proj, bf, n_heads):
    bsz, s_len, width = proj.shape
    col = width // LANES - 1
    ts = _tile(s_len, 512)
    ns = s_len // ts

    def body(dc_ref, fl_ref, b_ref, dfl_ref, db_ref, carry_ref):
        bi, s = pl.program_id(0), pl.program_id(1)

        @pl.when((bi == 0) & (s == 0))
        def _():
            db_ref[...] = jnp.zeros_like(db_ref)

        @pl.when(s == 0)
        def _():
            carry_ref[...] = jnp.zeros_like(carry_ref)

        dlogf = _dot_hi(_tri(ts, True), dc_ref[...]) + carry_ref[0:1, :]
        carry_ref[0:1, :] = dlogf[0:1, :]
        lane = lax.broadcasted_iota(jnp.int32, (1, LANES), 1)
        dfl = jnp.where(lane < n_heads, dlogf * (1.0 - _sigmoid(fl_ref[...] + b_ref[...])), 0.0)
        dfl_ref[...] = dfl.astype(BF16)
        db_ref[...] += _fold8(dfl)

    return pl.pallas_call(
        body, out_shape=(SDS((bsz, s_len, LANES), BF16), SDS((8, LANES), F32)), grid=(bsz, ns), name="fox_gate_bwd",
        in_specs=[pl.BlockSpec((None, ts, LANES), lambda bi, s: (bi, ns - 1 - s, 0)),
                  pl.BlockSpec((None, ts, LANES), lambda bi, s: (bi, ns - 1 - s, col)),
                  pl.BlockSpec((1, LANES), lambda bi, s: (0, 0))],
        out_specs=(pl.BlockSpec((None, ts, LANES), lambda bi, s: (bi, ns - 1 - s, 0)),
                   pl.BlockSpec((8, LANES), lambda bi, s: (0, 0))),
        scratch_shapes=[pltpu.VMEM((8, LANES), F32)],
        compiler_params=_cp("arbitrary", "arbitrary"))(dc, proj, bf)


def _head_maps(d):
    ch = lax.broadcasted_iota(jnp.int32, (d, LANES), 0) // HEAD_DIM
    hd = lax.broadcasted_iota(jnp.int32, (d, LANES), 1)
    e = jnp.where(ch == hd, 1.0, 0.0).astype(BF16)
    cht = lax.broadcasted_iota(jnp.int32, (LANES, d), 1) // HEAD_DIM
    hdt = lax.broadcasted_iota(jnp.int32, (LANES, d), 0)
    et = jnp.where(cht == hdt, 1.0, 0.0).astype(BF16)
    return e, et


def _dot_sel(x, e):
    a = x.astype(BF16)
    r = x - a.astype(F32)
    b = r.astype(BF16)
    c = (r - b.astype(F32)).astype(BF16)
    return (jnp.dot(a, e, preferred_element_type=F32) + jnp.dot(b, e, preferred_element_type=F32)
            + jnp.dot(c, e, preferred_element_type=F32))


def _fox_qknorm_fwd(proj, gq, gk, d):
    t = proj.shape[0]
    ts = _tile(t, 256)
    scale = 1.0 / math.sqrt(HEAD_DIM)

    def body(q_ref, k_ref, v_ref, gq_ref, gk_ref, qn_ref, kn_ref, vb_ref):
        e, et = _head_maps(d)

        def norm(v, g):
            r = lax.rsqrt(_dot_sel(v * v, e) / HEAD_DIM + EPS)
            return v * _dot_sel(r, et) * g

        qn_ref[...] = (norm(q_ref[...], gq_ref[...]) * scale).astype(BF16)
        kn_ref[...] = norm(k_ref[...], gk_ref[...]).astype(BF16)
        vb_ref[...] = v_ref[...].astype(BF16)

    def colblk(j):
        return pl.BlockSpec((ts, d), lambda i: (i, j))

    vec = pl.BlockSpec((1, d), lambda i: (0, 0))
    out = pl.BlockSpec((ts, d), lambda i: (i, 0))
    return pl.pallas_call(
        body, out_shape=(SDS((t, d), BF16),) * 3, grid=(t // ts,), name="fox_qknorm_fwd",
        in_specs=[colblk(0), colblk(1), colblk(2), vec, vec], out_specs=(out, out, out),
        compiler_params=_cp("parallel"))(proj, proj, proj, gq, gk)


def _fox_qknorm_bwd(proj, dq, dk, dv, gq, gk, d):
    t = proj.shape[0]
    ts = _tile(t, 256)
    scale = 1.0 / math.sqrt(HEAD_DIM)

    def body(q_ref, k_ref, dq_ref, dk_ref, dv_ref, gq_ref, gk_ref, dp_ref, dgq_ref, dgk_ref):
        @pl.when(pl.program_id(0) == 0)
        def _():
            dgq_ref[...] = jnp.zeros_like(dgq_ref)
            dgk_ref[...] = jnp.zeros_like(dgk_ref)

        e, et = _head_maps(d)

        def back(v, g, dn, dg_ref):
            r = _dot_sel(lax.rsqrt(_dot_sel(v * v, e) / HEAD_DIM + EPS), et)
            vh = v * r
            dg_ref[...] += _fold8(dn * vh)
            u = dn * g
            mh = _dot_sel(_dot_sel(u * vh, e) / HEAD_DIM, et)
            return r * (u - vh * mh)

        dp_ref[:, 0:d] = back(q_ref[...], gq_ref[...], dq_ref[...] * scale, dgq_ref).astype(BF16)
        dp_ref[:, d:2 * d] = back(k_ref[...], gk_ref[...], dk_ref[...], dgk_ref).astype(BF16)
        dp_ref[:, 2 * d:3 * d] = dv_ref[...]

    def colblk(j):
        return pl.BlockSpec((ts, d), lambda i: (i, j))

    row = pl.BlockSpec((ts, d), lambda i: (i, 0))
    vec = pl.BlockSpec((1, d), lambda i: (0, 0))
    acc = pl.BlockSpec((8, d), lambda i: (0, 0))
    return pl.pallas_call(
        body, out_shape=(SDS((t, 3 * d), BF16), SDS((8, d), F32), SDS((8, d), F32)), grid=(t // ts,),
        name="fox_qknorm_bwd", in_specs=[colblk(0), colblk(1), row, row, row, vec, vec],
        out_specs=(pl.BlockSpec((ts, 3 * d), lambda i: (i, 0)), acc, acc),
        compiler_params=_cp("arbitrary"))(proj, proj, dq, dk, dv, gq, gk)


ATT_BLOCK = 512
_NT = (((1,), (1,)), ((), ()))
_TN = (((0,), (0,)), ((), ()))


def _head_mask(h):
    return (lax.broadcasted_iota(jnp.int32, (1, LANES), 1) // HEAD_DIM) == h


def _causal(qi, ki, tq, tk):
    row = qi * tq + lax.broadcasted_iota(jnp.int32, (tq, 1), 0)
    col = ki * tk + lax.broadcasted_iota(jnp.int32, (1, tk), 1)
    return col <= row


def _direct_exchange(ins, outs, place, sems, gather):
    send_sems, recv_sems, local_sems = sems
    x, y, c, me = _mesh_place()
    copies = []
    for t in range(len(ins)):
        dst = outs[t].at[me] if gather else outs[place[t][0]].at[me, place[t][1]]
        copies.append(pltpu.make_async_copy(ins[t] if gather else ins[t].at[me], dst, local_sems.at[t]))
        for kbits in range(1, N_DEV):
            px = 1 - x if kbits & 4 else x
            py = 1 - y if kbits & 2 else y
            pc = 1 - c if kbits & 1 else c
            copies.append(pltpu.make_async_remote_copy(
                src_ref=ins[t] if gather else ins[t].at[4 * px + 2 * py + pc], dst_ref=dst,
                send_sem=send_sems.at[t, kbits - 1], recv_sem=recv_sems.at[t, kbits - 1],
                device_id=(px, py, pc), device_id_type=pl.DeviceIdType.MESH))
    return copies


def _exchange_scratch(n):
    return [pltpu.SemaphoreType.DMA((n, N_DEV - 1)), pltpu.SemaphoreType.DMA((n, N_DEV - 1)),
            pltpu.SemaphoreType.DMA((n,))]


def _flash_fwd(q, k, v, crow, gather=()):
    bsz, s_len, d = q.shape
    nj = d // LANES
    tq = tk = _tile(s_len, ATT_BLOCK, LANES)
    nq = s_len // tq
    ng = len(gather)

    pairs = [(a, b) for a in range(nq) for b in range(a + 1)]
    qtab = jnp.asarray([a for a, _ in pairs], jnp.int32)
    ktab = jnp.asarray([b for _, b in pairs], jnp.int32)

    def body(qtab_ref, ktab_ref, q_ref, k_ref, v_ref, c_ref, *rest):
        g_in, (o_ref, lse_ref), g_out = rest[:ng], rest[ng:ng + 2], rest[ng + 2:2 * ng + 2]
        m_ref, l_ref, acc_ref = rest[2 * ng + 2:2 * ng + 5]
        step_id = pl.program_id(2)
        qi, ki = qtab_ref[step_id], ktab_ref[step_id]
        if ng:
            sems = rest[2 * ng + 5:]
            outer = (pl.program_id(0), pl.program_id(1))

            @pl.when((outer[0] == 0) & (outer[1] == 0) & (step_id == 0))
            def _():
                for cp in _direct_exchange(g_in, g_out, None, sems, True):
                    cp.start()

            @pl.when((outer[0] == bsz - 1) & (outer[1] == nj - 1) & (step_id == len(pairs) - 1))
            def _():
                for cp in _direct_exchange(g_in, g_out, None, sems, True):
                    cp.wait()

        @pl.when(ki == 0)
        def _():
            m_ref[...] = jnp.full_like(m_ref, NEG)
            l_ref[...] = jnp.zeros_like(l_ref)
            acc_ref[...] = jnp.zeros_like(acc_ref)

        def step(masked):
            qv, kv, vv = q_ref[...], k_ref[...], v_ref[...]
            for h in range(2):
                qh = jnp.where(_head_mask(h), qv, jnp.zeros_like(qv))
                s = lax.dot_general(qh, kv, _NT, preferred_element_type=F32) - c_ref[h:h + 1, :]
                if masked:
                    s = jnp.where(_causal(qi, ki, tq, tk), s, NEG)
                m_prev = m_ref[h]
                m_new = jnp.maximum(m_prev, jnp.max(s, axis=1, keepdims=True))
                pm = jnp.exp(s - m_new)
                alpha = jnp.exp(m_prev - m_new)
                l_ref[h] = alpha * l_ref[h] + jnp.sum(pm, axis=1, keepdims=True)
                p_hi = pm.astype(BF16)
                p_lo = (pm - p_hi.astype(F32)).astype(BF16)
                acc_ref[h] = (alpha * acc_ref[h] + jnp.dot(p_hi, vv, preferred_element_type=F32)
                              + jnp.dot(p_lo, vv, preferred_element_type=F32))
                m_ref[h] = m_new

        pl.when(ki < qi)(functools.partial(step, False))
        pl.when(ki == qi)(functools.partial(step, True))

        @pl.when(ki == qi)
        def _():
            m0 = _head_mask(0)
            o_ref[...] = jnp.where(m0, acc_ref[0] / l_ref[0], acc_ref[1] / l_ref[1])
            lse_ref[...] = jnp.where(m0, m_ref[0] + jnp.log(l_ref[0]), m_ref[1] + jnp.log(l_ref[1]))

    qblk = pl.BlockSpec((None, tq, LANES), lambda bi, j, t, qt, kt: (bi, qt[t], j))
    kblk = pl.BlockSpec((None, tk, LANES), lambda bi, j, t, qt, kt: (bi, kt[t], j))
    hbm = pl.BlockSpec(memory_space=pl.ANY)
    outs = pl.pallas_call(
        body, out_shape=[SDS((bsz, s_len, d), F32), SDS((bsz, nj, s_len, LANES), F32)]
        + [SDS((N_DEV,) + tuple(a.shape), a.dtype) for a in gather], name="flash_fwd",
        grid_spec=pltpu.PrefetchScalarGridSpec(
            num_scalar_prefetch=2, grid=(bsz, nj, len(pairs)),
            in_specs=[qblk, kblk, kblk,
                      pl.BlockSpec((None, None, 2, tk), lambda bi, j, t, qt, kt: (bi, j, 0, kt[t]))] + [hbm] * ng,
            out_specs=[qblk, pl.BlockSpec((None, None, tq, LANES), lambda bi, j, t, qt, kt: (bi, j, qt[t], 0))]
            + [hbm] * ng,
            scratch_shapes=[pltpu.VMEM((2, tq, 1), F32), pltpu.VMEM((2, tq, 1), F32),
                            pltpu.VMEM((2, tq, LANES), F32)] + (_exchange_scratch(ng) if ng else [])),
        compiler_params=_cp("arbitrary", "arbitrary", "arbitrary"))(qtab, ktab, q, k, v, crow, *gather)
    return outs[0], outs[1], list(outs[2:])


def _flash_probs(qv, kv, vv, dov, ov, lse, c_ref, h, mask):
    hm = _head_mask(h)
    qh = jnp.where(hm, qv, jnp.zeros_like(qv))
    s = lax.dot_general(qh, kv, _NT, preferred_element_type=F32) - c_ref[h:h + 1, :]
    pm = jnp.exp(s - lse[:, h * HEAD_DIM:h * HEAD_DIM + 1])
    if mask is not None:
        pm = jnp.where(mask, pm, 0.0)
    doh = jnp.where(hm, dov, jnp.zeros_like(dov))
    dpm = lax.dot_general(doh, vv, _NT, preferred_element_type=F32)
    delta = jnp.sum(jnp.where(hm, dov.astype(F32) * ov, 0.0), axis=1, keepdims=True)
    return pm, pm * (dpm - delta)


def _flash_bwd(q, k, v, do, o, lse, crow, items=(), groups=()):
    bsz, s_len, d = q.shape
    nj = d // LANES
    tq = tk = _tile(s_len, ATT_BLOCK, LANES)
    nq = s_len // tq

    pairs = [(b, a) for b in range(nq) for a in range(b, nq)]
    n_live = len(pairs)
    ktab = jnp.asarray([b for b, _ in pairs] + [nq - 1] * nq, jnp.int32)
    qtab = jnp.asarray([a for _, a in pairs] + list(range(nq)), jnp.int32)

    n_it, n_grp = len(items), len(groups)
    place = {it: (g, l) for g, members in enumerate(groups) for l, it in enumerate(members)}

    def body(ktab_ref, qtab_ref, q_ref, k_ref, v_ref, do_ref, o_ref, lse_ref, c_ref, *rest):
        x_in, (dq_ref, dk_ref, dv_ref, dc_ref) = rest[:n_it], rest[n_it:n_it + 4]
        x_out = rest[n_it + 4:n_it + 4 + n_grp]
        dqa_ref, dka_ref, dva_ref, dca_ref = rest[n_it + 4 + n_grp:n_it + 8 + n_grp]
        step_id = pl.program_id(2)
        ki, qi = ktab_ref[step_id], qtab_ref[step_id]
        live = step_id < n_live
        rows = pl.ds(pl.multiple_of(qi * tq, tq), tq)
        if n_it:
            sems = rest[n_it + 8 + n_grp:]
            outer = (pl.program_id(0), pl.program_id(1))

            @pl.when((outer[0] == 0) & (outer[1] == 0) & (step_id == 0))
            def _():
                for cp in _direct_exchange(x_in, x_out, place, sems, False):
                    cp.start()

            @pl.when((outer[0] == bsz - 1) & (outer[1] == nj - 1) & (step_id == n_live + nq - 1))
            def _():
                for cp in _direct_exchange(x_in, x_out, place, sems, False):
                    cp.wait()

        @pl.when(step_id == 0)
        def _():
            dqa_ref[...] = jnp.zeros_like(dqa_ref)

        @pl.when(live & (qi == ki))
        def _():
            dka_ref[...] = jnp.zeros_like(dka_ref)
            dva_ref[...] = jnp.zeros_like(dva_ref)
            dca_ref[...] = jnp.zeros_like(dca_ref)

        def step(masked):
            qv, kv, vv, dov, ov, lse = q_ref[...], k_ref[...], v_ref[...], do_ref[...], o_ref[...], lse_ref[...]
            mask = _causal(qi, ki, tq, tk) if masked else None
            for h in range(2):
                pm, ds = _flash_probs(qv, kv, vv, dov, ov, lse, c_ref, h, mask)
                dsb = ds.astype(BF16)
                dva_ref[h] += lax.dot_general(pm.astype(BF16), dov, _TN, preferred_element_type=F32)
                dka_ref[h] += lax.dot_general(dsb, qv, _TN, preferred_element_type=F32)
                dqa_ref[h, rows, :] += jnp.dot(dsb, kv, preferred_element_type=F32)
                dca_ref[h:h + 1, :] -= jnp.sum(ds, axis=0, keepdims=True)

        pl.when(live & (qi > ki))(functools.partial(step, False))
        pl.when(live & (qi == ki))(functools.partial(step, True))

        @pl.when(live & (qi == nq - 1))
        def _():
            m0 = _head_mask(0)
            dk_ref[...] = jnp.where(m0, dka_ref[0], dka_ref[1])
            dv_ref[...] = jnp.where(m0, dva_ref[0], dva_ref[1]).astype(BF16)
            dc_ref[...] = dca_ref[0:2, :]

        @pl.when(jnp.logical_not(live))
        def _():
            dq_ref[...] = jnp.where(_head_mask(0), dqa_ref[0, rows, :], dqa_ref[1, rows, :])

    def qside(bi, j, t, kt, qt):
        return (bi, jnp.where(t < n_live, qt[t], nq - 1), j)

    def kside(bi, j, t, kt, qt):
        return (bi, kt[t], j)

    def dqside(bi, j, t, kt, qt):
        return (bi, jnp.where(t < n_live, 0, qt[t]), j)

    qblk, kblk = pl.BlockSpec((None, tq, LANES), qside), pl.BlockSpec((None, tk, LANES), kside)
    cblk = pl.BlockSpec((None, None, 2, tk), lambda bi, j, t, kt, qt: (bi, j, 0, kt[t]))
    hbm = pl.BlockSpec(memory_space=pl.ANY)
    outs = pl.pallas_call(
        body, out_shape=[SDS((bsz, s_len, d), F32), SDS((bsz, s_len, d), F32), SDS((bsz, s_len, d), BF16),
                         SDS((bsz, nj, 2, s_len), F32)]
        + [SDS((N_DEV, len(members)) + tuple(items[members[0]].shape[1:]), items[members[0]].dtype)
           for members in groups], name="flash_bwd",
        grid_spec=pltpu.PrefetchScalarGridSpec(
            num_scalar_prefetch=2, grid=(bsz, nj, n_live + nq),
            in_specs=[qblk, kblk, kblk, qblk, qblk,
                      pl.BlockSpec((None, None, tq, LANES),
                                   lambda bi, j, t, kt, qt: (bi, j, jnp.where(t < n_live, qt[t], nq - 1), 0)),
                      cblk] + [hbm] * n_it,
            out_specs=[pl.BlockSpec((None, tq, LANES), dqside), kblk, kblk, cblk] + [hbm] * n_grp,
            scratch_shapes=[pltpu.VMEM((2, s_len, LANES), F32), pltpu.VMEM((2, tk, LANES), F32),
                            pltpu.VMEM((2, tk, LANES), F32), pltpu.VMEM((8, tk), F32)]
            + (_exchange_scratch(n_it) if n_it else [])),
        compiler_params=_cp("arbitrary", "arbitrary", "arbitrary"))(ktab, qtab, q, k, v, do, o, lse, crow, *items)
    return outs[0], outs[1], outs[2], outs[3], list(outs[4:])


def _loss_head(y, target):
    t, d = y.shape
    tm = _tile(t, 512)

    def body(y_ref, t_ref, dy_ref, acc_ref):
        @pl.when(pl.program_id(0) == 0)
        def _():
            acc_ref[...] = jnp.zeros_like(acc_ref)

        err = y_ref[...] - t_ref[...]
        dy_ref[...] = err / d
        acc_ref[...] += _fold8(err * err)

    row = pl.BlockSpec((tm, d), lambda i: (i, 0))
    return pl.pallas_call(
        body, out_shape=(SDS((t, d), F32), SDS((8, d), F32)), grid=(t // tm,), name="loss_head",
        in_specs=[row, row], out_specs=(row, pl.BlockSpec((8, d), lambda i: (0, 0))),
        compiler_params=_cp("arbitrary"))(y, target)


ADAM_COLS = 1024


def _adamw(g8, w, m, v):
    shape = w.shape
    cols = shape[-1]
    rows = w.size // cols
    n_parts = g8.shape[0]
    g8, w, m, v = g8.reshape(n_parts, rows, cols), w.reshape(rows, cols), m.reshape(rows, cols), v.reshape(rows, cols)
    tr = _tile(rows, 256, 16)
    c1 = 1.0 - ADAM_B1 ** ADAM_STEP
    c2 = 1.0 - ADAM_B2 ** ADAM_STEP

    def body(g8_ref, w_ref, m_ref, v_ref, g_ref, d_ref, nm_ref, nv_ref):
        g = g8_ref[0].astype(F32)
        for i in range(1, n_parts):
            g = g + g8_ref[i].astype(F32)
        mn = ADAM_B1 * m_ref[...] + (1.0 - ADAM_B1) * g
        vn = ADAM_B2 * v_ref[...] + (1.0 - ADAM_B2) * (g * g)
        g_ref[...] = g
        nm_ref[...] = mn
        nv_ref[...] = vn
        d_ref[...] = -ADAM_LR * ((mn / c1) / (jnp.sqrt(vn / c2) + ADAM_EPS) + ADAM_WD * w_ref[...])

    blk = pl.BlockSpec((tr, cols), lambda i: (i, 0))
    outs = pl.pallas_call(
        body, out_shape=(SDS((rows, cols), F32),) * 4, grid=(rows // tr,), name="adamw",
        in_specs=[pl.BlockSpec((n_parts, tr, cols), lambda i: (0, i, 0)), blk, blk, blk], out_specs=(blk,) * 4,
        compiler_params=_cp("parallel"))(g8, w, m, v)
    return [o.reshape(shape) for o in outs]


def _mesh_place():
    x, y, c = lax.axis_index("x"), lax.axis_index("y"), lax.axis_index("c")
    return x, y, c, 4 * x + 2 * y + c


def _gather(shards):
    n = len(shards)

    def body(*refs):
        ins, outs = refs[:n], refs[n:2 * n]
        send_sems, recv_sems, local_sems = refs[2 * n:]
        x, y, c, me = _mesh_place()
        sibling = (x, y, 1 - c)
        chips = [(1 - x, y), (x, 1 - y), (1 - x, 1 - y)]

        def block(px, py, pc):
            return 4 * px + 2 * py + pc

        def copy(t, k, blk, to, src=None):
            return pltpu.make_async_remote_copy(
                src_ref=outs[t].at[blk] if src is None else src, dst_ref=outs[t].at[blk],
                send_sem=send_sems.at[t, k], recv_sem=recv_sems.at[t, k], device_id=to,
                device_id_type=pl.DeviceIdType.MESH)

        own = [pltpu.make_async_copy(ins[t], outs[t].at[me], local_sems.at[t]) for t in range(n)]
        first = []
        for t in range(n):
            own[t].start()
            first.append(copy(t, 0, me, sibling, src=ins[t]))
            first += [copy(t, 1 + j, me, (*chip, c), src=ins[t]) for j, chip in enumerate(chips)]
        for cp in first:
            cp.start()
        passed = []
        for j, chip in enumerate(chips):
            for t in range(n):
                copy(t, 1 + j, block(*chip, c), (x, y, c)).wait_recv()
                cp = copy(t, 4 + j, block(*chip, c), sibling)
                cp.start()
                passed.append(cp)
        for t in range(n):
            copy(t, 0, block(x, y, 1 - c), (x, y, c)).wait_recv()
            for j, chip in enumerate(chips):
                copy(t, 4 + j, block(*chip, 1 - c), (x, y, c)).wait_recv()
        for cp in first + passed:
            cp.wait_send()
        for cp in own:
            cp.wait()

    hbm = pl.BlockSpec(memory_space=pl.ANY)
    return pl.pallas_call(
        body, out_shape=[SDS((N_DEV,) + tuple(s.shape), s.dtype) for s in shards], name="gather",
        in_specs=[hbm] * n, out_specs=[hbm] * n,
        scratch_shapes=[pltpu.SemaphoreType.DMA((n, N_DEV - 1)), pltpu.SemaphoreType.DMA((n, N_DEV - 1)),
                        pltpu.SemaphoreType.DMA((n,))])(*shards)


N_CHIP = N_DEV // 2


def _scatter_core(items):
    n = len(items)

    def body(*refs):
        ins, outs = refs[:n], refs[n:2 * n]
        send_sems, recv_sems = refs[2 * n:]
        x, y, c, _ = _mesh_place()
        copies = []
        for it in range(n):
            for r in range(N_CHIP):
                cp = pltpu.make_async_remote_copy(
                    src_ref=ins[it].at[2 * r + 1 - c], dst_ref=outs[it].at[r], send_sem=send_sems.at[it, r],
                    recv_sem=recv_sems.at[it, r], device_id=(x, y, 1 - c), device_id_type=pl.DeviceIdType.MESH)
                cp.start()
                copies.append(cp)
        for cp in copies:
            cp.wait()

    hbm = pl.BlockSpec(memory_space=pl.ANY)
    return pl.pallas_call(
        body, out_shape=[SDS((N_CHIP,) + tuple(a.shape[1:]), a.dtype) for a in items], name="scatter_core",
        in_specs=[hbm] * n, out_specs=[hbm] * n,
        scratch_shapes=[pltpu.SemaphoreType.DMA((n, N_CHIP)), pltpu.SemaphoreType.DMA((n, N_CHIP))])(*items)


def _pair_add(item, other):
    shape = item.shape[1:]
    cols = shape[-1]
    rows = math.prod(shape) // cols
    tr = _tile(rows, 512, 16)

    def body(x_ref, o_ref, h_ref):
        c = lax.axis_index("c")
        mine = jnp.where(c == 0, x_ref[0].astype(F32), x_ref[1].astype(F32))
        h_ref[...] = (mine + o_ref[...].astype(F32)).astype(item.dtype)

    return pl.pallas_call(
        body, out_shape=SDS((N_CHIP, rows, cols), item.dtype), grid=(N_CHIP, rows // tr), name="pair_add",
        in_specs=[pl.BlockSpec((None, 2, tr, cols), lambda r, i: (r, 0, i, 0)),
                  pl.BlockSpec((None, tr, cols), lambda r, i: (r, i, 0))],
        out_specs=pl.BlockSpec((None, tr, cols), lambda r, i: (r, i, 0)),
        compiler_params=_cp("parallel", "parallel"))(
            item.reshape(N_CHIP, 2, rows, cols), other.reshape(N_CHIP, rows, cols)).reshape((N_CHIP,) + shape)


def _scatter_chip(items, groups):
    n = len(items)
    place = {it: (g, l) for g, members in enumerate(groups) for l, it in enumerate(members)}

    def body(*refs):
        ins, outs = refs[:n], refs[n:n + len(groups)]
        send_sems, recv_sems, local_sems = refs[n + len(groups):]
        x, y, c, _ = _mesh_place()
        chip = 2 * x + y
        copies = []
        for it in range(n):
            g, l = place[it]
            own = pltpu.make_async_copy(ins[it].at[chip], outs[g].at[chip, l], local_sems.at[it])
            own.start()
            copies.append(own)
            for kbits in range(1, N_CHIP):
                px = 1 - x if kbits & 2 else x
                py = 1 - y if kbits & 1 else y
                cp = pltpu.make_async_remote_copy(
                    src_ref=ins[it].at[2 * px + py], dst_ref=outs[g].at[chip, l],
                    send_sem=send_sems.at[it, kbits - 1], recv_sem=recv_sems.at[it, kbits - 1],
                    device_id=(px, py, c), device_id_type=pl.DeviceIdType.MESH)
                cp.start()
                copies.append(cp)
        for cp in copies:
            cp.wait()

    hbm = pl.BlockSpec(memory_space=pl.ANY)
    out_shape = [SDS((N_CHIP, len(members)) + tuple(items[members[0]].shape[1:]), items[members[0]].dtype)
                 for members in groups]
    return pl.pallas_call(
        body, out_shape=out_shape, name="scatter_chip", in_specs=[hbm] * n, out_specs=[hbm] * len(groups),
        scratch_shapes=[pltpu.SemaphoreType.DMA((n, N_CHIP - 1)), pltpu.SemaphoreType.DMA((n, N_CHIP - 1)),
                        pltpu.SemaphoreType.DMA((n,))])(*items)


def _scatter(items, groups):
    halves = _scatter_core(items)
    return _scatter_chip([_pair_add(a, h) for a, h in zip(items, halves)], groups)


def _cat_lanes(g, layer, nb, blk, width):
    _, _, rows, c = g.shape
    tr = _tile(rows, 256, 16)

    def body(g_ref, o_ref):
        for p in range(nb):
            o_ref[:, p * c:(p + 1) * c] = g_ref[p]
        if width > nb * c:
            o_ref[:, nb * c:] = jnp.zeros((tr, width - nb * c), g.dtype)

    return pl.pallas_call(
        body, out_shape=SDS((rows, width), g.dtype), grid=(rows // tr,), name="cat_lanes",
        in_specs=[pl.BlockSpec((nb, None, tr, c), lambda i: (blk, layer, i, 0))],
        out_specs=pl.BlockSpec((tr, width), lambda i: (i, 0)),
        compiler_params=_cp("parallel"))(g)


def _split_lanes(parts, c):
    rows = parts[0].shape[0]
    counts = [p.shape[1] // c for p in parts]
    tr = _tile(rows, 256, 16)

    def body(*refs):
        o_ref = refs[-1]
        q = 0
        for x_ref, cnt in zip(refs[:-1], counts):
            for p in range(cnt):
                o_ref[q] = x_ref[:, p * c:(p + 1) * c]
                q += 1

    return pl.pallas_call(
        body, out_shape=SDS((sum(counts), rows, c), parts[0].dtype), grid=(rows // tr,), name="split_lanes",
        in_specs=[pl.BlockSpec((tr, p.shape[1]), lambda i: (i, 0)) for p in parts],
        out_specs=pl.BlockSpec((sum(counts), tr, c), lambda i: (0, i, 0)),
        compiler_params=_cp("parallel"))(*parts)


def _unshard(g8, shard_shape, axis):
    full = jnp.moveaxis(g8.reshape((N_DEV,) + tuple(shard_shape)), 0, axis)
    shape = list(shard_shape)
    shape[axis] *= N_DEV
    return full.reshape(shape)


def _to_shards(full, axis):
    shape = list(full.shape)
    shape[axis:axis + 1] = [N_DEV, shape[axis] // N_DEV]
    return jnp.moveaxis(full.reshape(shape), axis, 0).reshape(N_DEV, -1)


SMALL = [n for n in SHARDED if n not in MATRICES]


def _flat_rows(parts):
    flat = jnp.concatenate([p.reshape(-1) for p in parts])
    chunk = 8 * ADAM_COLS
    n = -(-flat.shape[0] // chunk) * chunk
    return jnp.pad(flat, (0, n - flat.shape[0])).reshape(n // ADAM_COLS, ADAM_COLS)


def _prepare_vectors(small, shards):
    wt = {}
    flat = small.reshape(N_DEV, -1)
    off = 0
    for n in SMALL:
        size = shards[n].size
        wt[n] = _unshard(flat[:, off:off + size], shards[n].shape, SHARD_AXIS[n])
        off += size
    return wt


def _prepare_matrices(gathered):
    wt = {}
    for n in ('conv_w_out', 'fox_w_o', 'ffn_w_down'):
        if n in gathered:
            g = gathered[n]
            wt[n] = [g[:, l].reshape(N_DEV * g.shape[2], g.shape[3]) for l in range(g.shape[1])]
    if 'pool_w' in gathered:
        g = gathered['pool_w']
        wt['pool_w'] = [jnp.moveaxis(g[:, l], 0, 1).reshape(g.shape[2], N_DEV * g.shape[3], g.shape[4])
                        for l in range(g.shape[1])]
    if 'conv_w_in' in gathered:
        g = gathered['conv_w_in']
        wt['conv_w_in'] = [_cat_lanes(g, l, N_DEV, 0, N_DEV * g.shape[3]) for l in range(g.shape[1])]
    if 'fox_w_in' in gathered:
        g = gathered['fox_w_in']
        wt['fox_w_in'] = [_cat_lanes(g, l, N_DEV, 0, 3 * g.shape[2] + LANES) for l in range(g.shape[1])]
    if 'ffn_w_up' in gathered:
        g = gathered['ffn_w_up']
        half = N_DEV // 2
        wt['ffn_w_up_v'] = [_cat_lanes(g, l, half, 0, half * g.shape[3]) for l in range(g.shape[1])]
        wt['ffn_w_up_g'] = [_cat_lanes(g, l, half, 1, half * g.shape[3]) for l in range(g.shape[1])]
    return wt


def _pad_rows(w, rows):
    return jnp.pad(w, ((0, rows - w.shape[0]), (0, 0)))


def _fold(acc):
    return acc.sum(axis=0)


def _local_step(x, target, wt, late_shards=None, cut=None):
    wt = dict(wt)
    late_names = [n for n in MATRICES if late_shards and n in late_shards]
    late_recv = {}
    bsz, s_len, d = x.shape
    t = bsz * s_len
    depth = wt['norm_mix'].shape[0]
    n_heads = d // HEAD_DIM
    f = wt['ffn_w_up_v'][0].shape[1]
    row = lambda a: a.reshape(1, -1)
    grads = {n: {} for n in WEIGHTS}
    saved = []

    xc = x.reshape(t, d)
    hn_next = None
    for i in range(depth):
        j = i // 3
        kind = i % 3
        sv = {'x_mix': xc}
        gm = row(wt['norm_mix'][i])
        gf = row(wt['norm_ffn'][i])
        if kind == 0:
            hn = hn_next if hn_next is not None else _rmsnorm_fwd(xc, gm)
            p = _mm(hn, wt['conv_w_in'][j], bias=row(wt['conv_b_in'][j]), name="conv_in")
            u, sact = _conv_act_fwd(p.reshape(bsz, s_len, 2 * d), _pad_rows(wt['conv_dw'][j], 32),
                                    row(wt['conv_dw_b'][j]), row(wt['conv_ln_g'][j]), row(wt['conv_ln_b'][j]))
            sact = sact.reshape(t, d)
            xn, hf = _mm(sact, wt['conv_w_out'][j], bias=row(wt['conv_b_out'][j]), residual=xc, norm_gain=gf,
                         out_dtype=F32, name="conv_out")
            sv.update(hn=hn, p=p, u=u.reshape(t, d), sact=sact)
        elif kind == 1:
            xn, pp = _pool_fwd(xc.reshape(bsz, s_len, d), gm, wt['pool_w'][j], row(wt['pool_b'][j]),
                               row(wt['pool_scale'][j]))
            xn = xn.reshape(t, d)
            hf = _rmsnorm_fwd(xn, gf)
            sv.update(p=pp)
        else:
            hn = hn_next if hn_next is not None else _rmsnorm_fwd(xc, gm)
            wp = wt['fox_w_in'][j]
            bf = jnp.pad(wt['fox_b_f'][j], (0, LANES - n_heads)).reshape(1, LANES)
            gq = jnp.tile(wt['fox_q_gain'][j], n_heads).reshape(1, d)
            gk = jnp.tile(wt['fox_k_gain'][j], n_heads).reshape(1, d)
            proj = _mm(hn, wp, out_dtype=F32, name="fox_in")
            c = _fox_gate_fwd(proj.reshape(bsz, s_len, -1), bf, n_heads)
            crow = jnp.swapaxes(c, 1, 2)[:, :n_heads].reshape(bsz, n_heads // 2, 2, s_len)
            qn, kn, vb = _fox_qknorm_fwd(proj, gq, gk, d)
            shp = (bsz, s_len, d)
            o, lse, got = _flash_fwd(qn.reshape(shp), kn.reshape(shp), vb.reshape(shp), crow,
                                     gather=[late_shards[n] for n in late_names])
            for key, layers in _prepare_matrices(dict(zip(late_names, got))).items():
                wt[key] = wt[key] + layers
            o = o.reshape(t, d)
            xn, hf = _mm(o, wt['fox_w_o'][j], residual=xc, norm_gain=gf, out_dtype=F32, name="fox_out")
            sv.update(hn=hn, wp=wp, bf=bf, gq=gq, gk=gk, proj=proj, crow=crow, qn=qn, kn=kn, vb=vb, o=o, lse=lse)
        xc = xn
        sv['x_ffn'] = xc
        shf = (bsz, s_len, f)
        uv = _mm(hf, wt['ffn_w_up_v'][i], name="ffn_up").reshape(shf)
        ug = _mm(hf, wt['ffn_w_up_g'][i], name="ffn_up").reshape(shf)
        dw8 = _pad_rows(wt['ffn_dw'][i], 8)
        af, vv, vg = _ffn_act_fwd(uv, ug, dw8, row(wt['ffn_dw_b'][i]))
        af = af.reshape(t, f)
        if i + 1 < depth and (i + 1) % 3 != 1:
            xc, hn_next = _mm(af, wt['ffn_w_down'][i], residual=xc, norm_gain=row(wt['norm_mix'][i + 1]),
                              out_dtype=F32, name="ffn_down")
        else:
            xc, hn_next = _mm(af, wt['ffn_w_down'][i], residual=xc, out_dtype=F32, name="ffn_down"), None
        sv.update(hf=hf, uv=uv, ug=ug, vv=vv, vg=vg, af=af, dw8=dw8)
        saved.append(sv)

    dx, sq = _loss_head(xc, target.reshape(t, d))

    for i in reversed(range(depth)):
        j = i // 3
        kind = i % 3
        sv = saved[i]
        shf = (bsz, s_len, f)
        da = _mm(dx, wt['ffn_w_down'][i], trans_b=True, name="ffn_down_dgrad")
        gw, _ = _wgrad(sv['af'], dx, name="ffn_down_wgrad")
        grads['ffn_w_down'][i] = gw.reshape(N_DEV, f // N_DEV, d)
        dvv, dvg, dbv, dbg = _ffn_act_bwd1(sv['vv'], sv['vg'], da.reshape(shf))
        grads['ffn_dw_b'][i] = jnp.concatenate([_fold(dbv), _fold(dbg)])
        duv, dug, ddwv, ddwg = _ffn_act_bwd2(dvv, dvg, sv['uv'], sv['ug'], sv['dw8'])
        grads['ffn_dw'][i] = jnp.concatenate([ddwv.sum(axis=1), ddwg.sum(axis=1)], axis=1)
        duv, dug = duv.reshape(t, f), dug.reshape(t, f)
        gv, _ = _wgrad(sv['hf'], duv, name="ffn_up_wgrad")
        gg, _ = _wgrad(sv['hf'], dug, name="ffn_up_wgrad")
        grads['ffn_w_up'][i] = _split_lanes([gv, gg], 2 * f // N_DEV)
        dx, dg = _mm(duv, wt['ffn_w_up_v'][i], trans_b=True, a2=dug, b2=wt['ffn_w_up_g'][i],
                     norm_bwd=(sv['x_ffn'], row(wt['norm_ffn'][i]), dx), name="ffn_up_dgrad")
        grads['norm_ffn'][i] = _fold(dg)
        gm = row(wt['norm_mix'][i])
        if kind == 0:
            dsact = _mm(dx, wt['conv_w_out'][j], trans_b=True, name="conv_out_dgrad")
            gw, cs = _wgrad(sv['sact'], dx, name="conv_out_wgrad")
            grads['conv_w_out'][j] = gw.reshape(N_DEV, d // N_DEV, d)
            grads['conv_b_out'][j] = _fold(cs)
            du, dlg, dlb, dwb = _conv_act_bwd1(sv['u'], dsact, row(wt['conv_ln_g'][j]), row(wt['conv_ln_b'][j]))
            grads['conv_ln_g'][j], grads['conv_ln_b'][j], grads['conv_dw_b'][j] = _fold(dlg), _fold(dlb), _fold(dwb)
            dp, ddw = _conv_act_bwd2(du.reshape(bsz, s_len, d), sv['p'].reshape(bsz, s_len, 2 * d),
                                     _pad_rows(wt['conv_dw'][j], 32))
            grads['conv_dw'][j] = ddw.sum(axis=1)[:wt['conv_dw'].shape[1]]
            dp = dp.reshape(t, 2 * d)
            gw, cs = _wgrad(sv['hn'], dp, name="conv_in_wgrad")
            grads['conv_w_in'][j] = _split_lanes([gw], 2 * d // N_DEV)
            grads['conv_b_in'][j] = _fold(cs)
            dx, dg = _mm(dp, wt['conv_w_in'][j], trans_b=True, norm_bwd=(sv['x_mix'], gm, dx), name="conv_in_dgrad")
            grads['norm_mix'][i] = _fold(dg)
        elif kind == 1:
            shp = (bsz, s_len, d)
            dxn, dwp, dbp, dsc, dg = _pool_bwd(sv['x_mix'].reshape(shp), dx.reshape(shp), sv['p'], gm, wt['pool_w'][j],
                                               row(wt['pool_b'][j]), row(wt['pool_scale'][j]))
            dx = dxn.reshape(t, d)
            ng, cg = dwp.shape[0], dwp.shape[1]
            grads['pool_w'][j] = jnp.moveaxis(dwp.reshape(ng, N_DEV, cg // N_DEV, cg), 1, 0).astype(BF16)
            grads['pool_b'][j] = _fold(dbp).reshape(wt['pool_b'].shape[1:])
            grads['pool_scale'][j] = _fold(dsc)
            grads['norm_mix'][i] = _fold(dg)
        else:
            shp = (bsz, s_len, d)
            do = _mm(dx, wt['fox_w_o'][j], trans_b=True, name="fox_out_dgrad")
            gw, _ = _wgrad(sv['o'], dx, name="fox_out_wgrad")
            grads['fox_w_o'][j] = gw.reshape(N_DEV, d // N_DEV, d)
            fl_args = (sv['qn'].reshape(shp), sv['kn'].reshape(shp), sv['vb'].reshape(shp), do.reshape(shp),
                       sv['o'].reshape(shp), sv['lse'], sv['crow'])
            items, groups = [], []
            for n in late_names:
                members = [grads[n][l] for l in sorted(grads[n]) if l >= cut[n]]
                groups.append(list(range(len(items), len(items) + len(members))))
                items += members
            dq, dk, dv, dcrow, got = _flash_bwd(*fl_args, items=items, groups=groups)
            late_recv = dict(zip(late_names, got))
            dc = jnp.swapaxes(dcrow.reshape(bsz, n_heads, s_len), 1, 2)
            dc = jnp.pad(dc, ((0, 0), (0, 0), (0, LANES - n_heads)))
            dfl, dbf = _fox_gate_bwd(dc, sv['proj'].reshape(bsz, s_len, -1), sv['bf'], n_heads)
            grads['fox_b_f'][j] = _fold(dbf)[:n_heads]
            dqkv, dgq, dgk = _fox_qknorm_bwd(sv['proj'], dq.reshape(t, d), dk.reshape(t, d), dv.reshape(t, d),
                                             sv['gq'], sv['gk'], d)
            grads['fox_q_gain'][j] = _fold(dgq).reshape(n_heads, HEAD_DIM).sum(axis=0)
            grads['fox_k_gain'][j] = _fold(dgk).reshape(n_heads, HEAD_DIM).sum(axis=0)
            dproj = jnp.concatenate([dqkv, dfl.reshape(t, LANES)], axis=1)
            dwp, _ = _wgrad(sv['hn'], dproj, name="fox_in_wgrad")
            grads['fox_w_in'][j] = _split_lanes([dwp], (3 * d + n_heads) // N_DEV)
            dx, dg = _mm(dproj, sv['wp'], trans_b=True, norm_bwd=(sv['x_mix'], gm, dx), name="fox_in_dgrad")
            grads['norm_mix'][i] = _fold(dg)

    small = {n: jnp.stack([g[k] for k in sorted(g)]) for n, g in grads.items() if n not in MATRICES}
    big = {n: [grads[n][k] for k in sorted(grads[n]) if n not in late_recv or k < cut[n]] for n in MATRICES}
    return sq.sum(), dx.reshape(bsz, s_len, d), small, big, late_recv


def _train_step(x, target, w, m, v):
    depth = w['norm_mix'].shape[0]
    attn = [i for i in range(depth) if i % 3 == 2]
    cut = {n: w[n].shape[0] for n in MATRICES}
    if len(attn) == 1:
        cut['ffn_w_up'] = cut['ffn_w_down'] = attn[0]
        later_conv = [i // 3 for i in range(attn[0] + 1, depth) if i % 3 == 0]
        if later_conv:
            cut['conv_w_in'] = cut['conv_w_out'] = later_conv[0]
    late_shards = {n: w[n][cut[n]:].astype(BF16) for n in MATRICES if 0 < cut[n] < w[n].shape[0]}
    got = _gather([w[n][:cut[n]].astype(BF16) for n in MATRICES] + [_flat_rows([w[n] for n in SMALL])])
    wt = _prepare_matrices(dict(zip(MATRICES, got[:-1])))
    wt.update(_prepare_vectors(got[-1], w))
    wt.update({n: w[n] for n in REPLICATED})
    sq, grad_x, gsmall, gbig, late_recv = _local_step(x, target, wt, late_shards, cut)
    d = x.shape[-1]

    shard_rows = jnp.concatenate([_to_shards(gsmall[n], SHARD_AXIS[n]) for n in SMALL], axis=1)
    rep = jnp.concatenate([gsmall[n].reshape(-1) for n in REPLICATED] + [(0.5 / d) * sq.reshape(1)])
    rows = jnp.concatenate([shard_rows, jnp.broadcast_to(rep, (N_DEV, rep.shape[0]))], axis=1)
    chunk = 8 * ADAM_COLS
    n_all = rows.shape[1]
    n_pad = -(-n_all // chunk) * chunk
    rows = jnp.pad(rows, ((0, 0), (0, n_pad - n_all))).reshape(N_DEV, n_pad // ADAM_COLS, ADAM_COLS)

    items, groups = [], []
    for n in MATRICES:
        groups.append(list(range(len(items), len(items) + len(gbig[n]))))
        items += gbig[n]
    groups.append([len(items)])
    items.append(rows)
    recv = _scatter(items, groups)

    res = [{}, {}, {}, {}]
    for n, r in zip(MATRICES, recv[:-1]):
        if n in late_recv:
            c = cut[n]
            outs = zip(_adamw(r, w[n][:c], m[n][:c], v[n][:c]), _adamw(late_recv[n], w[n][c:], m[n][c:], v[n][c:]))
            outs = [jnp.concatenate(pair, axis=0) for pair in outs]
        else:
            outs = _adamw(r, w[n], m[n], v[n])
        for k, o in enumerate(outs):
            res[k][n] = o
    order = SMALL + REPLICATED

    def flat(tree):
        parts = jnp.concatenate([tree[n].reshape(-1) for n in order])
        return jnp.pad(parts, (0, n_pad - parts.shape[0])).reshape(n_pad // ADAM_COLS, ADAM_COLS)

    outs = [o.reshape(-1) for o in _adamw(recv[-1].reshape((N_CHIP,) + rows.shape[1:]), flat(w), flat(m), flat(v))]
    off = 0
    for n in order:
        size = w[n].size
        for k in range(4):
            res[k][n] = outs[k][off:off + size].reshape(w[n].shape)
        off += size
    loss = outs[0][n_all - 1]
    return (loss, grad_x, *[res[0][n] for n in WEIGHTS], *[res[1][n] for n in WEIGHTS],
            *[res[2][n] for n in WEIGHTS], *[res[3][n] for n in WEIGHTS])


def kernel(x, norm_mix, norm_ffn, conv_w_in, conv_b_in, conv_dw, conv_dw_b, conv_ln_g, conv_ln_b, conv_w_out, conv_b_out, pool_w, pool_b, pool_scale, fox_w_in, fox_b_f, fox_q_gain, fox_k_gain, fox_w_o, ffn_w_up, ffn_dw, ffn_dw_b, ffn_w_down, loss_target, m_norm_mix, m_norm_ffn, m_conv_w_in, m_conv_b_in, m_conv_dw, m_conv_dw_b, m_conv_ln_g, m_conv_ln_b, m_conv_w_out, m_conv_b_out, m_pool_w, m_pool_b, m_pool_scale, m_fox_w_in, m_fox_b_f, m_fox_q_gain, m_fox_k_gain, m_fox_w_o, m_ffn_w_up, m_ffn_dw, m_ffn_dw_b, m_ffn_w_down, v_norm_mix, v_norm_ffn, v_conv_w_in, v_conv_b_in, v_conv_dw, v_conv_dw_b, v_conv_ln_g, v_conv_ln_b, v_conv_w_out, v_conv_b_out, v_pool_w, v_pool_b, v_pool_scale, v_fox_w_in, v_fox_b_f, v_fox_q_gain, v_fox_k_gain, v_fox_w_o, v_ffn_w_up, v_ffn_dw, v_ffn_dw_b, v_ffn_w_down):
    w = dict(zip(WEIGHTS, (norm_mix, norm_ffn, conv_w_in, conv_b_in, conv_dw, conv_dw_b, conv_ln_g, conv_ln_b, conv_w_out, conv_b_out, pool_w, pool_b, pool_scale, fox_w_in, fox_b_f, fox_q_gain, fox_k_gain, fox_w_o, ffn_w_up, ffn_dw, ffn_dw_b, ffn_w_down)))
    m = dict(zip(WEIGHTS, (m_norm_mix, m_norm_ffn, m_conv_w_in, m_conv_b_in, m_conv_dw, m_conv_dw_b, m_conv_ln_g, m_conv_ln_b, m_conv_w_out, m_conv_b_out, m_pool_w, m_pool_b, m_pool_scale, m_fox_w_in, m_fox_b_f, m_fox_q_gain, m_fox_k_gain, m_fox_w_o, m_ffn_w_up, m_ffn_dw, m_ffn_dw_b, m_ffn_w_down)))
    v = dict(zip(WEIGHTS, (v_norm_mix, v_norm_ffn, v_conv_w_in, v_conv_b_in, v_conv_dw, v_conv_dw_b, v_conv_ln_g, v_conv_ln_b, v_conv_w_out, v_conv_b_out, v_pool_w, v_pool_b, v_pool_scale, v_fox_w_in, v_fox_b_f, v_fox_q_gain, v_fox_k_gain, v_fox_w_o, v_ffn_w_up, v_ffn_dw, v_ffn_dw_b, v_ffn_w_down)))
    return _train_step(x, loss_target, w, m, v)
```
